```python
import jax, jax.numpy as jnp
from jax import lax
import numpy as np

D_MODEL = 1024
BATCH = 8
SEQ = 4096
DEPTH = 1

HG_HEADS = 8
HG_KEY_DIM = 128
HG_VAL_DIM = D_MODEL // HG_HEADS
HG_KEY_WIDTH = HG_HEADS * HG_KEY_DIM
HG_VAL_WIDTH = HG_HEADS * HG_VAL_DIM
CHUNK = 32
CONV_WIDTH = D_MODEL
CONV_KERNEL = 31
D_FF = 4 * D_MODEL
EPS = 1e-6
SPLITS = (HG_KEY_WIDTH, HG_KEY_WIDTH, HG_VAL_WIDTH, HG_VAL_WIDTH,
          CONV_WIDTH, CONV_WIDTH, D_MODEL, D_MODEL)
IN_COLS = sum(SPLITS)

kernel_name = "hgrn2_conformer_gated_hybrid_block"


def rmsnorm(x, g):
    xf = x.astype(jnp.float32)
    y = xf * lax.rsqrt(jnp.mean(xf * xf, axis=-1, keepdims=True) + EPS)
    return (y * g.astype(jnp.float32)).astype(x.dtype)


def layernorm(x, g, b):
    xf = x.astype(jnp.float32)
    mu = jnp.mean(xf, axis=-1, keepdims=True)
    var = jnp.mean(jnp.square(xf - mu), axis=-1, keepdims=True)
    y = (xf - mu) * lax.rsqrt(var + EPS)
    return (y * g.astype(jnp.float32) + b.astype(jnp.float32)).astype(x.dtype)


def hgrn2_mixer(q_raw, f_raw, i_raw, og_raw, lb, norm_g):
    B, S, _ = q_raw.shape
    dt = q_raw.dtype
    n_chunks = S // CHUNK
    q = jax.nn.silu(q_raw.astype(jnp.float32))
    lbf = lb.astype(jnp.float32)
    f = lbf + (1.0 - lbf) * jax.nn.sigmoid(f_raw.astype(jnp.float32))
    k = 1.0 - f
    logf = jnp.log(f)
    v = i_raw.astype(jnp.float32)

    def to_chunks(t, d):
        return t.reshape(B, n_chunks, CHUNK, HG_HEADS, d).transpose(1, 0, 3, 2, 4)

    qc, kc, gc = to_chunks(q, HG_KEY_DIM), to_chunks(k, HG_KEY_DIM), to_chunks(logf, HG_KEY_DIM)
    vc = to_chunks(v, HG_VAL_DIM)
    causal = jnp.tril(jnp.ones((CHUNK, CHUNK), dtype=bool))

    def step(state, inp):
        qb, kb, vb, gb = inp
        b = jnp.cumsum(gb, axis=2)
        inter = jnp.einsum('bhik,bhkv->bhiv', qb * jnp.exp(b), state)
        diff = b[:, :, :, None, :] - b[:, :, None, :, :]
        decay = jnp.exp(jnp.where(causal[:, :, None], diff, -jnp.inf))
        scores = jnp.einsum('bhik,bhjk,bhijk->bhij', qb, kb, decay)
        intra = jnp.einsum('bhij,bhjv->bhiv', scores, vb)
        b_last = b[:, :, -1:, :]
        new_state = (jnp.exp(b_last[:, :, 0, :])[..., None] * state
                     + jnp.einsum('bhjk,bhjv->bhkv', kb * jnp.exp(b_last - b), vb))
        return new_state, inter + intra

    s0 = jnp.zeros((B, HG_HEADS, HG_KEY_DIM, HG_VAL_DIM), jnp.float32)
    _, o = lax.scan(step, s0, (qc, kc, vc, gc))
    o = o.transpose(1, 0, 3, 2, 4).reshape(B, S, HG_HEADS, HG_VAL_DIM)
    o = rmsnorm(o, norm_g.reshape(HG_HEADS, HG_VAL_DIM))
    o = o.reshape(B, S, HG_VAL_WIDTH) * jax.nn.silu(og_raw.astype(jnp.float32))
    return o.astype(dt)


def conformer_conv(val, gate, dw, db, ln_g, ln_b):
    u = val * jax.nn.sigmoid(gate)
    u = lax.conv_general_dilated(
        u, dw[:, None, :].astype(u.dtype), window_strides=(1,),
        padding=[(CONV_KERNEL - 1, 0)], dimension_numbers=('NWC', 'WIO', 'NWC'),
        feature_group_count=CONV_WIDTH) + db
    return jax.nn.silu(layernorm(u, ln_g, ln_b))


def _fwd_setup_inputs(seed: int = 0) -> dict:
    key = jax.random.key(seed)
    ks = jax.random.split(key, 24)
    nrm = lambda k, shape, s: jax.random.normal(k, shape, jnp.float32) * s
    gain = lambda k, shape: 1.0 + nrm(k, shape, 0.05)
    return {
        "x": nrm(ks[0], (BATCH, SEQ, D_MODEL), 1.0),
        "c": nrm(ks[1], (BATCH, D_MODEL), 1.0),
        "w_ada": nrm(ks[2], (DEPTH, D_MODEL, 6 * D_MODEL), 0.2 * D_MODEL ** -0.5),
        "b_ada": nrm(ks[3], (DEPTH, 6 * D_MODEL), 0.02),
        "pre_norm_tm": gain(ks[4], (DEPTH, D_MODEL)),
        "post_norm_tm": gain(ks[5], (DEPTH, D_MODEL)),
        "pre_norm_cm": gain(ks[6], (DEPTH, D_MODEL)),
        "post_norm_cm": gain(ks[7], (DEPTH, D_MODEL)),
        "w_in": nrm(ks[8], (DEPTH, D_MODEL, IN_COLS), D_MODEL ** -0.5),
        "b_in": nrm(ks[9], (DEPTH, IN_COLS), 0.02),
        "hg_lb_logits": nrm(ks[10], (DEPTH + 1, HG_KEY_WIDTH), 0.5),
        "hg_norm": gain(ks[11], (DEPTH, HG_VAL_WIDTH)),
        "conv_dw": nrm(ks[12], (DEPTH, CONV_KERNEL, CONV_WIDTH), CONV_KERNEL ** -0.5),
        "conv_db": nrm(ks[13], (DEPTH, CONV_WIDTH), 0.02),
        "conv_ln_g": gain(ks[14], (DEPTH, CONV_WIDTH)),
        "conv_ln_b": nrm(ks[15], (DEPTH, CONV_WIDTH), 0.02),
        "w_br_a": nrm(ks[16], (DEPTH, HG_VAL_WIDTH, D_MODEL), HG_VAL_WIDTH ** -0.5),
        "w_br_b": nrm(ks[17], (DEPTH, CONV_WIDTH, D_MODEL), CONV_WIDTH ** -0.5),
        "w_out": nrm(ks[18], (DEPTH, D_MODEL, D_MODEL), D_MODEL ** -0.5),
        "w_ff1": nrm(ks[19], (DEPTH, D_MODEL, D_FF), D_MODEL ** -0.5),
        "w_ff2": nrm(ks[20], (DEPTH, D_FF, D_MODEL), D_FF ** -0.5),
    }


def _fwd_reference(x, c, w_ada, b_ada, pre_norm_tm, post_norm_tm, pre_norm_cm, post_norm_cm,
              w_in, b_in, hg_lb_logits, hg_norm, conv_dw, conv_db, conv_ln_g, conv_ln_b,
              w_br_a, w_br_b, w_out, w_ff1, w_ff2):
    lb_all = jnp.cumsum(jax.nn.softmax(hg_lb_logits.astype(jnp.float32), axis=0), axis=0)
    c_act = jax.nn.silu(c)
    split_idx = list(np.cumsum(SPLITS)[:-1])
    for l in range(DEPTH):
        mod = c_act @ w_ada[l] + b_ada[l]
        sh1, sc1, g1, sh2, sc2, g2 = [m[:, None, :] for m in jnp.split(mod, 6, axis=-1)]

        h = rmsnorm(x, pre_norm_tm[l]) * (1.0 + sc1) + sh1
        p = h @ w_in[l] + b_in[l]
        q_r, f_r, i_r, og_r, cv_r, cg_r, ga_r, gb_r = jnp.split(p, split_idx, axis=-1)
        y_a = hgrn2_mixer(q_r, f_r, i_r, og_r, lb_all[l], hg_norm[l]) @ w_br_a[l]
        y_b = conformer_conv(cv_r, cg_r, conv_dw[l], conv_db[l],
                             conv_ln_g[l], conv_ln_b[l]) @ w_br_b[l]
        merged = jax.nn.sigmoid(ga_r) * y_a + jax.nn.sigmoid(gb_r) * y_b
        y = merged @ w_out[l]
        x = x + g1 * rmsnorm(y, post_norm_tm[l])

        h = rmsnorm(x, pre_norm_cm[l]) * (1.0 + sc2) + sh2
        y = jnp.square(jax.nn.relu(h @ w_ff1[l])) @ w_ff2[l]
        x = x + g2 * rmsnorm(y, post_norm_cm[l])
    return x


import jax as _jax
import jax.numpy as _jnp

TWIN_FORMAT = 'train_step'
FWD_PARAMS = ['x', 'c', 'w_ada', 'b_ada', 'pre_norm_tm', 'post_norm_tm', 'pre_norm_cm', 'post_norm_cm', 'w_in', 'b_in', 'hg_lb_logits', 'hg_norm', 'conv_dw', 'conv_db', 'conv_ln_g', 'conv_ln_b', 'w_br_a', 'w_br_b', 'w_out', 'w_ff1', 'w_ff2']
TWIN_WEIGHTS = ['w_ada', 'b_ada', 'pre_norm_tm', 'post_norm_tm', 'pre_norm_cm', 'post_norm_cm', 'w_in', 'b_in', 'hg_lb_logits', 'hg_norm', 'conv_dw', 'conv_db', 'conv_ln_g', 'conv_ln_b', 'w_br_a', 'w_br_b', 'w_out', 'w_ff1', 'w_ff2']
TWIN_DIFF_INPUT = 'x'
TWIN_INPUTS = ['x', 'c', 'w_ada', 'b_ada', 'pre_norm_tm', 'post_norm_tm', 'pre_norm_cm', 'post_norm_cm', 'w_in', 'b_in', 'hg_lb_logits', 'hg_norm', 'conv_dw', 'conv_db', 'conv_ln_g', 'conv_ln_b', 'w_br_a', 'w_br_b', 'w_out', 'w_ff1', 'w_ff2', 'loss_target', 'm_w_ada', 'm_b_ada', 'm_pre_norm_tm', 'm_post_norm_tm', 'm_pre_norm_cm', 'm_post_norm_cm', 'm_w_in', 'm_b_in', 'm_hg_lb_logits', 'm_hg_norm', 'm_conv_dw', 'm_conv_db', 'm_conv_ln_g', 'm_conv_ln_b', 'm_w_br_a', 'm_w_br_b', 'm_w_out', 'm_w_ff1', 'm_w_ff2', 'v_w_ada', 'v_b_ada', 'v_pre_norm_tm', 'v_post_norm_tm', 'v_pre_norm_cm', 'v_post_norm_cm', 'v_w_in', 'v_b_in', 'v_hg_lb_logits', 'v_hg_norm', 'v_conv_dw', 'v_conv_db', 'v_conv_ln_g', 'v_conv_ln_b', 'v_w_br_a', 'v_w_br_b', 'v_w_out', 'v_w_ff1', 'v_w_ff2']
TWIN_OUTPUTS = ['loss', 'grad_x', 'grad_w_ada', 'grad_b_ada', 'grad_pre_norm_tm', 'grad_post_norm_tm', 'grad_pre_norm_cm', 'grad_post_norm_cm', 'grad_w_in', 'grad_b_in', 'grad_hg_lb_logits', 'grad_hg_norm', 'grad_conv_dw', 'grad_conv_db', 'grad_conv_ln_g', 'grad_conv_ln_b', 'grad_w_br_a', 'grad_w_br_b', 'grad_w_out', 'grad_w_ff1', 'grad_w_ff2', 'delta_w_ada', 'delta_b_ada', 'delta_pre_norm_tm', 'delta_post_norm_tm', 'delta_pre_norm_cm', 'delta_post_norm_cm', 'delta_w_in', 'delta_b_in', 'delta_hg_lb_logits', 'delta_hg_norm', 'delta_conv_dw', 'delta_conv_db', 'delta_conv_ln_g', 'delta_conv_ln_b', 'delta_w_br_a', 'delta_w_br_b', 'delta_w_out', 'delta_w_ff1', 'delta_w_ff2', 'new_m_w_ada', 'new_m_b_ada', 'new_m_pre_norm_tm', 'new_m_post_norm_tm', 'new_m_pre_norm_cm', 'new_m_post_norm_cm', 'new_m_w_in', 'new_m_b_in', 'new_m_hg_lb_logits', 'new_m_hg_norm', 'new_m_conv_dw', 'new_m_conv_db', 'new_m_conv_ln_g', 'new_m_conv_ln_b', 'new_m_w_br_a', 'new_m_w_br_b', 'new_m_w_out', 'new_m_w_ff1', 'new_m_w_ff2', 'new_v_w_ada', 'new_v_b_ada', 'new_v_pre_norm_tm', 'new_v_post_norm_tm', 'new_v_pre_norm_cm', 'new_v_post_norm_cm', 'new_v_w_in', 'new_v_b_in', 'new_v_hg_lb_logits', 'new_v_hg_norm', 'new_v_conv_dw', 'new_v_conv_db', 'new_v_conv_ln_g', 'new_v_conv_ln_b', 'new_v_w_br_a', 'new_v_w_br_b', 'new_v_w_out', 'new_v_w_ff1', 'new_v_w_ff2']
TWIN_LEAF_KINDS = {'loss': 'loss', 'grad_x': 'grad_x', 'grad_w_ada': 'grad_w', 'grad_b_ada': 'grad_w', 'grad_pre_norm_tm': 'grad_w', 'grad_post_norm_tm': 'grad_w', 'grad_pre_norm_cm': 'grad_w', 'grad_post_norm_cm': 'grad_w', 'grad_w_in': 'grad_w', 'grad_b_in': 'grad_w', 'grad_hg_lb_logits': 'grad_w', 'grad_hg_norm': 'grad_w', 'grad_conv_dw': 'grad_w', 'grad_conv_db': 'grad_w', 'grad_conv_ln_g': 'grad_w', 'grad_conv_ln_b': 'grad_w', 'grad_w_br_a': 'grad_w', 'grad_w_br_b': 'grad_w', 'grad_w_out': 'grad_w', 'grad_w_ff1': 'grad_w', 'grad_w_ff2': 'grad_w', 'delta_w_ada': 'delta_w', 'delta_b_ada': 'delta_w', 'delta_pre_norm_tm': 'delta_w', 'delta_post_norm_tm': 'delta_w', 'delta_pre_norm_cm': 'delta_w', 'delta_post_norm_cm': 'delta_w', 'delta_w_in': 'delta_w', 'delta_b_in': 'delta_w', 'delta_hg_lb_logits': 'delta_w', 'delta_hg_norm': 'delta_w', 'delta_conv_dw': 'delta_w', 'delta_conv_db': 'delta_w', 'delta_conv_ln_g': 'delta_w', 'delta_conv_ln_b': 'delta_w', 'delta_w_br_a': 'delta_w', 'delta_w_br_b': 'delta_w', 'delta_w_out': 'delta_w', 'delta_w_ff1': 'delta_w', 'delta_w_ff2': 'delta_w', 'new_m_w_ada': 'new_m', 'new_m_b_ada': 'new_m', 'new_m_pre_norm_tm': 'new_m', 'new_m_post_norm_tm': 'new_m', 'new_m_pre_norm_cm': 'new_m', 'new_m_post_norm_cm': 'new_m', 'new_m_w_in': 'new_m', 'new_m_b_in': 'new_m', 'new_m_hg_lb_logits': 'new_m', 'new_m_hg_norm': 'new_m', 'new_m_conv_dw': 'new_m', 'new_m_conv_db': 'new_m', 'new_m_conv_ln_g': 'new_m', 'new_m_conv_ln_b': 'new_m', 'new_m_w_br_a': 'new_m', 'new_m_w_br_b': 'new_m', 'new_m_w_out': 'new_m', 'new_m_w_ff1': 'new_m', 'new_m_w_ff2': 'new_m', 'new_v_w_ada': 'new_v', 'new_v_b_ada': 'new_v', 'new_v_pre_norm_tm': 'new_v', 'new_v_post_norm_tm': 'new_v', 'new_v_pre_norm_cm': 'new_v', 'new_v_post_norm_cm': 'new_v', 'new_v_w_in': 'new_v', 'new_v_b_in': 'new_v', 'new_v_hg_lb_logits': 'new_v', 'new_v_hg_norm': 'new_v', 'new_v_conv_dw': 'new_v', 'new_v_conv_db': 'new_v', 'new_v_conv_ln_g': 'new_v', 'new_v_conv_ln_b': 'new_v', 'new_v_w_br_a': 'new_v', 'new_v_w_br_b': 'new_v', 'new_v_w_out': 'new_v', 'new_v_w_ff1': 'new_v', 'new_v_w_ff2': 'new_v'}


def _forward(args):
    return _fwd_reference(*[args[k] for k in FWD_PARAMS])


def _output_shape():
    out = _jax.eval_shape(lambda: _forward(_fwd_setup_inputs(0)))
    return out.shape, out.dtype

N_MICROBATCH = 1
ADAM_LR = 0.001
ADAM_B1 = 0.9
ADAM_B2 = 0.999
ADAM_EPS = 1e-08
ADAM_WD = 0.01
ADAM_STEP = 10
PER_EXAMPLE_BATCH_AXIS = {'x': 0, 'c': 0, 'loss_target': 0}
SHARED_INPUTS = []
_WEIGHT_DTYPES = {'w_ada': _jnp.float32, 'b_ada': _jnp.float32, 'pre_norm_tm': _jnp.float32, 'post_norm_tm': _jnp.float32, 'pre_norm_cm': _jnp.float32, 'post_norm_cm': _jnp.float32, 'w_in': _jnp.float32, 'b_in': _jnp.float32, 'hg_lb_logits': _jnp.float32, 'hg_norm': _jnp.float32, 'conv_dw': _jnp.float32, 'conv_db': _jnp.float32, 'conv_ln_g': _jnp.float32, 'conv_ln_b': _jnp.float32, 'w_br_a': _jnp.float32, 'w_br_b': _jnp.float32, 'w_out': _jnp.float32, 'w_ff1': _jnp.float32, 'w_ff2': _jnp.float32}
MOMENT_SCALE = {'w_ada': 6.545358e-01, 'b_ada': 1.323553e+00, 'pre_norm_tm': 4.328069e-02, 'post_norm_tm': 5.845616e-01, 'pre_norm_cm': 4.756551e-02, 'post_norm_cm': 6.249333e-01, 'w_in': 1.599090e-02, 'b_in': 4.245718e-02, 'hg_lb_logits': 1.999025e-03, 'hg_norm': 2.593444e-02, 'conv_dw': 2.472676e-02, 'conv_db': 9.719978e-02, 'conv_ln_g': 4.609863e-02, 'conv_ln_b': 5.843154e-02, 'w_br_a': 2.497719e-02, 'w_br_b': 2.899291e-02, 'w_out': 3.946426e-02, 'w_ff1': 2.332321e-02, 'w_ff2': 5.978815e-02}


def _to_microbatches(a, axis):
    t = _jnp.moveaxis(a, axis, 0)
    t = t.reshape((N_MICROBATCH, t.shape[0] // N_MICROBATCH) + t.shape[1:])
    return _jnp.moveaxis(t, 1, axis + 1)


def setup_inputs(seed: int = 0) -> dict:
    inp = _fwd_setup_inputs(seed)
    key = _jax.random.fold_in(_jax.random.key(seed), 7919)
    shape, _ = _output_shape()
    out = dict(inp)
    out["loss_target"] = _jax.random.normal(_jax.random.fold_in(key, 0), shape, _jnp.float32)
    for i, name in enumerate(TWIN_WEIGHTS):
        w = inp[name].astype(_jnp.float32)
        if MOMENT_SCALE is None:
            s = _jnp.sqrt(_jnp.mean(_jnp.square(w)) + 1e-30)
        else:
            s = MOMENT_SCALE[name]
        km, kv = _jax.random.split(_jax.random.fold_in(key, i + 1))
        out[name] = w
        out["m_" + name] = s * _jax.random.normal(km, w.shape, _jnp.float32)
        out["v_" + name] = (s * s) * _jax.random.uniform(kv, w.shape, _jnp.float32, 0.5, 1.5)
    if N_MICROBATCH > 1:
        for name, axis in PER_EXAMPLE_BATCH_AXIS.items():
            out[name] = _to_microbatches(out[name], axis)
    return {'x': out['x'], 'c': out['c'], 'w_ada': out['w_ada'], 'b_ada': out['b_ada'], 'pre_norm_tm': out['pre_norm_tm'], 'post_norm_tm': out['post_norm_tm'], 'pre_norm_cm': out['pre_norm_cm'], 'post_norm_cm': out['post_norm_cm'], 'w_in': out['w_in'], 'b_in': out['b_in'], 'hg_lb_logits': out['hg_lb_logits'], 'hg_norm': out['hg_norm'], 'conv_dw': out['conv_dw'], 'conv_db': out['conv_db'], 'conv_ln_g': out['conv_ln_g'], 'conv_ln_b': out['conv_ln_b'], 'w_br_a': out['w_br_a'], 'w_br_b': out['w_br_b'], 'w_out': out['w_out'], 'w_ff1': out['w_ff1'], 'w_ff2': out['w_ff2'], 'loss_target': out['loss_target'], 'm_w_ada': out['m_w_ada'], 'm_b_ada': out['m_b_ada'], 'm_pre_norm_tm': out['m_pre_norm_tm'], 'm_post_norm_tm': out['m_post_norm_tm'], 'm_pre_norm_cm': out['m_pre_norm_cm'], 'm_post_norm_cm': out['m_post_norm_cm'], 'm_w_in': out['m_w_in'], 'm_b_in': out['m_b_in'], 'm_hg_lb_logits': out['m_hg_lb_logits'], 'm_hg_norm': out['m_hg_norm'], 'm_conv_dw': out['m_conv_dw'], 'm_conv_db': out['m_conv_db'], 'm_conv_ln_g': out['m_conv_ln_g'], 'm_conv_ln_b': out['m_conv_ln_b'], 'm_w_br_a': out['m_w_br_a'], 'm_w_br_b': out['m_w_br_b'], 'm_w_out': out['m_w_out'], 'm_w_ff1': out['m_w_ff1'], 'm_w_ff2': out['m_w_ff2'], 'v_w_ada': out['v_w_ada'], 'v_b_ada': out['v_b_ada'], 'v_pre_norm_tm': out['v_pre_norm_tm'], 'v_post_norm_tm': out['v_post_norm_tm'], 'v_pre_norm_cm': out['v_pre_norm_cm'], 'v_post_norm_cm': out['v_post_norm_cm'], 'v_w_in': out['v_w_in'], 'v_b_in': out['v_b_in'], 'v_hg_lb_logits': out['v_hg_lb_logits'], 'v_hg_norm': out['v_hg_norm'], 'v_conv_dw': out['v_conv_dw'], 'v_conv_db': out['v_conv_db'], 'v_conv_ln_g': out['v_conv_ln_g'], 'v_conv_ln_b': out['v_conv_ln_b'], 'v_w_br_a': out['v_w_br_a'], 'v_w_br_b': out['v_w_br_b'], 'v_w_out': out['v_w_out'], 'v_w_ff1': out['v_w_ff1'], 'v_w_ff2': out['v_w_ff2']}


def _loss(weights, diff, rest, loss_target):
    with _jax.named_scope("forward"):
        args = {**rest, TWIN_DIFF_INPUT: diff, **{k: w.astype(_WEIGHT_DTYPES[k]) for k, w in weights.items()}}
        y = _forward(args)
    with _jax.named_scope("loss_head"):
        err = _jnp.square(y.astype(_jnp.float32) - loss_target)
        return 0.5 * _jnp.sum(_jnp.mean(err, axis=-1)) if err.ndim else 0.5 * err


def _adamw(w, g, m, v):
    m = ADAM_B1 * m + (1.0 - ADAM_B1) * g
    v = ADAM_B2 * v + (1.0 - ADAM_B2) * _jnp.square(g)
    m_hat = m / (1.0 - ADAM_B1 ** ADAM_STEP)
    v_hat = v / (1.0 - ADAM_B2 ** ADAM_STEP)
    delta = -ADAM_LR * (m_hat / (_jnp.sqrt(v_hat) + ADAM_EPS) + ADAM_WD * w)
    return delta, m, v


def reference(x, c, w_ada, b_ada, pre_norm_tm, post_norm_tm, pre_norm_cm, post_norm_cm, w_in, b_in, hg_lb_logits, hg_norm, conv_dw, conv_db, conv_ln_g, conv_ln_b, w_br_a, w_br_b, w_out, w_ff1, w_ff2, loss_target, m_w_ada, m_b_ada, m_pre_norm_tm, m_post_norm_tm, m_pre_norm_cm, m_post_norm_cm, m_w_in, m_b_in, m_hg_lb_logits, m_hg_norm, m_conv_dw, m_conv_db, m_conv_ln_g, m_conv_ln_b, m_w_br_a, m_w_br_b, m_w_out, m_w_ff1, m_w_ff2, v_w_ada, v_b_ada, v_pre_norm_tm, v_post_norm_tm, v_pre_norm_cm, v_post_norm_cm, v_w_in, v_b_in, v_hg_lb_logits, v_hg_norm, v_conv_dw, v_conv_db, v_conv_ln_g, v_conv_ln_b, v_w_br_a, v_w_br_b, v_w_out, v_w_ff1, v_w_ff2):
    given = dict(x=x, c=c, w_ada=w_ada, b_ada=b_ada, pre_norm_tm=pre_norm_tm, post_norm_tm=post_norm_tm, pre_norm_cm=pre_norm_cm, post_norm_cm=post_norm_cm, w_in=w_in, b_in=b_in, hg_lb_logits=hg_lb_logits, hg_norm=hg_norm, conv_dw=conv_dw, conv_db=conv_db, conv_ln_g=conv_ln_g, conv_ln_b=conv_ln_b, w_br_a=w_br_a, w_br_b=w_br_b, w_out=w_out, w_ff1=w_ff1, w_ff2=w_ff2, loss_target=loss_target, m_w_ada=m_w_ada, m_b_ada=m_b_ada, m_pre_norm_tm=m_pre_norm_tm, m_post_norm_tm=m_post_norm_tm, m_pre_norm_cm=m_pre_norm_cm, m_post_norm_cm=m_post_norm_cm, m_w_in=m_w_in, m_b_in=m_b_in, m_hg_lb_logits=m_hg_lb_logits, m_hg_norm=m_hg_norm, m_conv_dw=m_conv_dw, m_conv_db=m_conv_db, m_conv_ln_g=m_conv_ln_g, m_conv_ln_b=m_conv_ln_b, m_w_br_a=m_w_br_a, m_w_br_b=m_w_br_b, m_w_out=m_w_out, m_w_ff1=m_w_ff1, m_w_ff2=m_w_ff2, v_w_ada=v_w_ada, v_b_ada=v_b_ada, v_pre_norm_tm=v_pre_norm_tm, v_post_norm_tm=v_post_norm_tm, v_pre_norm_cm=v_pre_norm_cm, v_post_norm_cm=v_post_norm_cm, v_w_in=v_w_in, v_b_in=v_b_in, v_hg_lb_logits=v_hg_lb_logits, v_hg_norm=v_hg_norm, v_conv_dw=v_conv_dw, v_conv_db=v_conv_db, v_conv_ln_g=v_conv_ln_g, v_conv_ln_b=v_conv_ln_b, v_w_br_a=v_w_br_a, v_w_br_b=v_w_br_b, v_w_out=v_w_out, v_w_ff1=v_w_ff1, v_w_ff2=v_w_ff2)
    weights = {n: given[n] for n in TWIN_WEIGHTS}
    shared = {n: given[n] for n in SHARED_INPUTS}
    per_example = {n: given[n] for n in ['x', 'c']}
    grad_fn = _jax.value_and_grad(_loss, argnums=(0, 1))

    def one_microbatch(ex, loss_target):
        ex = dict(ex)
        diff = ex.pop(TWIN_DIFF_INPUT)
        return grad_fn(weights, diff, {**shared, **ex}, loss_target)

    if N_MICROBATCH == 1:
        loss, (grad_w, grad_x) = one_microbatch(per_example, given["loss_target"])
    else:
        def body(carry, xs):
            loss_sum, grad_sum = carry
            l_k, (gw_k, gx_k) = one_microbatch(xs[0], xs[1])
            with _jax.named_scope("update"):
                return (loss_sum + l_k, _jax.tree.map(_jnp.add, grad_sum, gw_k)), gx_k

        init = (_jnp.zeros((), _jnp.float32), _jax.tree.map(_jnp.zeros_like, weights))
        (loss, grad_w), grad_x = _jax.lax.scan(body, init, (per_example, given["loss_target"]))
    with _jax.named_scope("update"):
        delta_w, new_m, new_v = {}, {}, {}
        for n in TWIN_WEIGHTS:
            delta_w[n], new_m[n], new_v[n] = _adamw(weights[n], grad_w[n], given["m_" + n], given["v_" + n])
    return (loss, grad_x, *[grad_w[n] for n in TWIN_WEIGHTS], *[delta_w[n] for n in TWIN_WEIGHTS],
            *[new_m[n] for n in TWIN_WEIGHTS], *[new_v[n] for n in TWIN_WEIGHTS])
```

```python
import functools

import jax
import jax.numpy as jnp
from jax import lax
from jax.experimental import pallas as pl
from jax.experimental.pallas import tpu as pltpu

F32 = jnp.float32
BF16 = jnp.bfloat16
MESH = pl.DeviceIdType.MESH
AXES = ("x", "y", "c")

D = 1024
HEADS = 8
DK = 128
CH = 128
LEVELS = 7
NSEG = 1 + 2 * LEVELS
KW = 31
HALO = 32
DFF = 4096
NDEV = 8
EPS = 1e-6
ADAM_LR, ADAM_B1, ADAM_B2, ADAM_EPS, ADAM_WD, ADAM_STEP = 0.001, 0.9, 0.999, 1e-08, 0.01, 10
VMEM_LIMIT = 58 * 1024 * 1024

_NN = (((1,), (0,)), ((), ()))
_NT = (((1,), (1,)), ((), ()))
_TN = (((0,), (0,)), ((), ()))

VSPEC = pl.BlockSpec(memory_space=pltpu.VMEM)
ANYSPEC = pl.BlockSpec(memory_space=pl.ANY)


def _params(*sem):
    return pltpu.CompilerParams(dimension_semantics=sem or None, vmem_limit_bytes=VMEM_LIMIT)


def _dot(a, b, dims):
    return lax.dot_general(a, b, dims, preferred_element_type=F32)


@jax.custom_vjp
def mm_nn(a, b):
    return _dot(a.astype(BF16), b.astype(BF16), _NN)


def _mm_nn_fwd(a, b):
    ab, bb = a.astype(BF16), b.astype(BF16)
    return _dot(ab, bb, _NN), (ab, bb)


def _mm_nn_bwd(res, ct):
    ab, bb = res
    cb = ct.astype(BF16)
    return _dot(cb, bb, _NT), _dot(ab, cb, _TN)


mm_nn.defvjp(_mm_nn_fwd, _mm_nn_bwd)


@jax.custom_vjp
def mm_nt(a, b):
    return _dot(a.astype(BF16), b.astype(BF16), _NT)


def _mm_nt_fwd(a, b):
    ab, bb = a.astype(BF16), b.astype(BF16)
    return _dot(ab, bb, _NT), (ab, bb)


def _mm_nt_bwd(res, ct):
    ab, bb = res
    cb = ct.astype(BF16)
    return _dot(cb, bb, _NN), _dot(cb, ab, _TN)


mm_nt.defvjp(_mm_nt_fwd, _mm_nt_bwd)


@jax.custom_vjp
def mm_tn(a, b):
    return _dot(a.astype(BF16), b.astype(BF16), _TN)


def _mm_tn_fwd(a, b):
    ab, bb = a.astype(BF16), b.astype(BF16)
    return _dot(ab, bb, _TN), (ab, bb)


def _mm_tn_bwd(res, ct):
    ab, bb = res
    cb = ct.astype(BF16)
    return _dot(bb, cb, _NT), _dot(ab, cb, _NN)


mm_tn.defvjp(_mm_tn_fwd, _mm_tn_bwd)


def _rms(x):
    return x * lax.rsqrt(jnp.mean(x * x, axis=-1, keepdims=True) + EPS)


def _silu(x):
    return x * jax.nn.sigmoid(x)


def f_modulate(x, gain, sc, sh):
    return _rms(x) * gain * (1.0 + sc) + sh


def f_residual(x, y, gate, gain):
    return x + gate * (_rms(y) * gain)


def f_merge(ga, gb, ya, yb):
    return jax.nn.sigmoid(ga) * ya + jax.nn.sigmoid(gb) * yb


def f_head_out(o, og, hg):
    heads = [_rms(o[:, h * DK:(h + 1) * DK]) for h in range(HEADS)]
    return jnp.concatenate(heads, axis=1) * hg * _silu(og)


def f_conv_act(u, g, b):
    mu = jnp.mean(u, axis=-1, keepdims=True)
    var = jnp.mean(jnp.square(u - mu), axis=-1, keepdims=True)
    return _silu((u - mu) * lax.rsqrt(var + EPS) * g + b)


def f_glu(cv, cg):
    return cv * jax.nn.sigmoid(cg)


def _split2(x):
    hi = x.astype(BF16)
    return hi, (x - hi.astype(F32)).astype(BF16)


def _seg_matrix(transposed):
    shape = (CH, CH)
    i = lax.broadcasted_iota(jnp.int32, shape, 1 if transposed else 0)
    t = lax.broadcasted_iota(jnp.int32, shape, 0 if transposed else 1)
    blocks = [t <= i]
    for lev in range(LEVELS):
        sh = LEVELS - 1 - lev
        h = 1 << sh
        upper = ((i >> sh) & 1) == 1
        mid = ((i >> (sh + 1)) << (sh + 1)) + (h - 1)
        blocks.append(upper & (t > mid) & (t <= i))
        blocks.append(jnp.logical_not(upper) & (t > i) & (t <= mid))
    blocks = [jnp.where(b, 1.0, 0.0).astype(BF16) for b in blocks]
    return jnp.concatenate(blocks, axis=1 if transposed else 0)


@jax.custom_vjp
def segsum(g):
    hi, lo = _split2(g)
    a = _seg_matrix(False)
    return _dot(a, hi, _NN) + _dot(a, lo, _NN)


def _segsum_fwd(g):
    return segsum(g), None


def _segsum_bwd(_, ct):
    hi, lo = _split2(ct)
    at = _seg_matrix(True)
    return (_dot(at, hi, _NN) + _dot(at, lo, _NN),)


segsum.defvjp(_segsum_fwd, _segsum_bwd)


def _score_masks():
    i = lax.broadcasted_iota(jnp.int32, (CH, CH), 0)
    j = lax.broadcasted_iota(jnp.int32, (CH, CH), 1)
    masks = [i == j]
    for lev in range(LEVELS):
        sh = LEVELS - 1 - lev
        same = (i >> (sh + 1)) == (j >> (sh + 1))
        masks.append(same & (((i >> sh) & 1) == 1) & (((j >> sh) & 1) == 0))
    return [jnp.where(m, 1.0, 0.0) for m in masks]


def f_hgrn_chunk(q_r, f_r, v, logits):
    l0, l1 = logits[0:1, :], logits[1:2, :]
    mx = lax.stop_gradient(jnp.maximum(l0, l1))
    e0, e1 = jnp.exp(l0 - mx), jnp.exp(l1 - mx)
    lb = e0 / (e0 + e1)
    q = _silu(q_r)
    f = lb + (1.0 - lb) * jax.nn.sigmoid(f_r)
    k = 1.0 - f
    seg = segsum(jnp.log(f))
    b = seg[0:CH]
    btot = b[CH - 1:CH, :]
    qe = q * jnp.exp(b)
    ke = k * jnp.exp(btot - b)
    dec = jnp.exp(btot)
    qs, ks = [q], [k]
    for lev in range(LEVELS):
        qs.append(q * jnp.exp(seg[(1 + 2 * lev) * CH:(2 + 2 * lev) * CH]))
        ks.append(k * jnp.exp(seg[(2 + 2 * lev) * CH:(3 + 2 * lev) * CH]))
    masks = _score_masks()
    intra, ut = [], []
    for h in range(HEADS):
        sl = slice(h * DK, (h + 1) * DK)
        sc = None
        for lev in range(LEVELS + 1):
            s = mm_nt(qs[lev][:, sl], ks[lev][:, sl]) * masks[lev]
            sc = s if sc is None else sc + s
        intra.append(mm_nn(sc, v[:, sl]))
        ut.append(mm_tn(v[:, sl], ke[:, sl]))
    return jnp.concatenate(intra, axis=1), qe, jnp.concatenate(ut, axis=1), dec


def _inter(qe_b, st_ref, rows):
    out = []
    for ci in range(rows // CH):
        st = st_ref[ci].astype(BF16)
        heads = [_dot(qe_b[ci * CH:(ci + 1) * CH, h * DK:(h + 1) * DK], st[:, h * DK:(h + 1) * DK], _NT)
                 for h in range(HEADS)]
        out.append(jnp.concatenate(heads, axis=1))
    return jnp.concatenate(out, axis=0)


def _adamw(w, g, m, v):
    m = ADAM_B1 * m + (1.0 - ADAM_B1) * g
    v = ADAM_B2 * v + (1.0 - ADAM_B2) * jnp.square(g)
    m_hat = m / (1.0 - ADAM_B1 ** ADAM_STEP)
    v_hat = v / (1.0 - ADAM_B2 ** ADAM_STEP)
    delta = -ADAM_LR * (m_hat / (jnp.sqrt(v_hat) + ADAM_EPS) + ADAM_WD * w)
    return delta, m, v


def _me():
    return lax.axis_index("x"), lax.axis_index("y"), lax.axis_index("c")


def _peer(k):
    x, y, c = _me()
    mask = k + 1
    px = (1 - x) if (mask >> 2) & 1 else x
    py = (1 - y) if (mask >> 1) & 1 else y
    pc = (1 - c) if mask & 1 else c
    return (px, py, pc), 4 * px + 2 * py + pc


def ada_forward(c, w_ada, b_ada):
    wcols = w_ada.shape[1]

    def body(c_ref, w_ref, b_ref, mod_ref, call_ref, part_ref, modp_ref, send_sems, recv_sems):
        x, y, cc = _me()
        me = 4 * x + 2 * y + cc
        call_ref[me] = _silu(c_ref[...])
        sends = []
        for k in range(NDEV - 1):
            dev, _ = _peer(k)
            cp = pltpu.make_async_remote_copy(call_ref.at[me], call_ref.at[me], send_sems.at[k], recv_sems.at[k],
                                              device_id=dev, device_id_type=MESH)
            cp.start()
            sends.append(cp)
        for k in range(NDEV - 1):
            _, pidx = _peer(k)
            pltpu.make_async_remote_copy(call_ref.at[pidx], call_ref.at[pidx], send_sems.at[k], recv_sems.at[k],
                                         device_id=_peer(k)[0], device_id_type=MESH).wait_recv()
        call = jnp.concatenate([call_ref[r] for r in range(NDEV)], axis=0)
        part = _dot(call.astype(BF16), w_ref[...].astype(BF16), _NN)
        for r in range(NDEV):
            part_ref[r] = part[r:r + 1, :]
        modp_ref[me] = part_ref[me]
        for k in range(NDEV - 1):
            dev, pidx = _peer(k)
            cp = pltpu.make_async_remote_copy(part_ref.at[pidx], modp_ref.at[me], send_sems.at[NDEV - 1 + k],
                                              recv_sems.at[NDEV - 1 + k], device_id=dev, device_id_type=MESH)
            cp.start()
            sends.append(cp)
        for k in range(NDEV - 1):
            dev, pidx = _peer(k)
            pltpu.make_async_remote_copy(part_ref.at[pidx], modp_ref.at[pidx], send_sems.at[NDEV - 1 + k],
                                         recv_sems.at[NDEV - 1 + k], device_id=dev, device_id_type=MESH).wait_recv()
        for cp in sends:
            cp.wait_send()
        mod_ref[...] = jnp.concatenate([modp_ref[r] for r in range(NDEV)], axis=1) + b_ref[...]

    mod, call = pl.pallas_call(
        body, name="ada_forward",
        out_shape=(jax.ShapeDtypeStruct((1, NDEV * wcols), F32), jax.ShapeDtypeStruct((NDEV, 1, D), F32)),
        in_specs=[VSPEC, VSPEC, VSPEC], out_specs=(VSPEC, VSPEC),
        scratch_shapes=[pltpu.VMEM((NDEV, 1, wcols), F32), pltpu.VMEM((NDEV, 1, wcols), F32),
                        pltpu.SemaphoreType.DMA((2 * (NDEV - 1),)), pltpu.SemaphoreType.DMA((2 * (NDEV - 1),))],
        compiler_params=_params(),
    )(c, w_ada, b_ada)
    return mod, call.reshape(NDEV, D)


def all_gather_blocks(shards):
    n = len(shards)

    def body(*refs):
        ins, outs = refs[:n], refs[n:2 * n]
        send_sems, recv_sems, local_sems = refs[2 * n:]
        x, y, c = _me()
        me, sibling = (x, y, c), (x, y, 1 - c)
        chips = [(1 - x, y), (x, 1 - y), (1 - x, 1 - y)]

        def slot(p):
            return 4 * p[0] + 2 * p[1] + p[2]

        def copy(a, k, block, to, src=None):
            return pltpu.make_async_remote_copy(
                src_ref=outs[a].at[slot(block)] if src is None else src, dst_ref=outs[a].at[slot(block)],
                send_sem=send_sems.at[a * 7 + k], recv_sem=recv_sems.at[a * 7 + k], device_id=to, device_id_type=MESH)

        started = []
        mine = []
        for a in range(n):
            cp = pltpu.make_async_copy(ins[a], outs[a].at[slot(me)], local_sems.at[a])
            cp.start()
            mine.append(cp)
            first = [copy(a, 0, me, sibling, src=ins[a])]
            first += [copy(a, 1 + j, me, (*chip, c), src=ins[a]) for j, chip in enumerate(chips)]
            for cp in first:
                cp.start()
            started += first
        for j, chip in enumerate(chips):
            for a in range(n):
                copy(a, 1 + j, (*chip, c), me).wait_recv()
                cp = copy(a, 4 + j, (*chip, c), sibling)
                cp.start()
                started.append(cp)
        for a in range(n):
            copy(a, 0, sibling, me).wait_recv()
            for j, chip in enumerate(chips):
                copy(a, 4 + j, (*chip, 1 - c), me).wait_recv()
        for cp in started:
            cp.wait_send()
        for cp in mine:
            cp.wait()

    return pl.pallas_call(
        body, name="all_gather_blocks",
        out_shape=tuple(jax.ShapeDtypeStruct((NDEV,) + s.shape, s.dtype) for s in shards),
        in_specs=[ANYSPEC] * n, out_specs=tuple([ANYSPEC] * n),
        scratch_shapes=[pltpu.SemaphoreType.DMA((7 * n,)), pltpu.SemaphoreType.DMA((7 * n,)),
                        pltpu.SemaphoreType.DMA((n,))],
        compiler_params=_params(),
    )(*shards)


def exchange_partials(blocks):
    n = len(blocks)

    def body(*refs):
        ins, outs = refs[:n], refs[n:2 * n]
        send_sems, recv_sems = refs[2 * n:]
        started = []
        for k in range(NDEV - 1):
            dev, pidx = _peer(k)
            for a in range(n):
                cp = pltpu.make_async_remote_copy(ins[a].at[pidx], outs[a].at[k], send_sems.at[a * 7 + k],
                                                  recv_sems.at[a * 7 + k], device_id=dev, device_id_type=MESH)
                cp.start()
                started.append(cp)
        for cp in started:
            cp.wait_recv()
        for cp in started:
            cp.wait_send()

    return pl.pallas_call(
        body, name="exchange_partials",
        out_shape=tuple(jax.ShapeDtypeStruct((NDEV - 1,) + b.shape[1:], b.dtype) for b in blocks),
        in_specs=[ANYSPEC] * n, out_specs=tuple([ANYSPEC] * n),
        scratch_shapes=[pltpu.SemaphoreType.DMA((7 * n,)), pltpu.SemaphoreType.DMA((7 * n,))],
        compiler_params=_params(),
    )(*blocks)


def gather_and_sum_rows(vec, keep):
    rows = vec.shape[0]

    def body(v_ref, sum_ref, keep_ref, buf_ref, send_sems, recv_sems):
        x, y, c = _me()
        me = 4 * x + 2 * y + c
        buf_ref[me] = v_ref[...]
        sends = []
        for k in range(NDEV - 1):
            dev, _ = _peer(k)
            cp = pltpu.make_async_remote_copy(v_ref, buf_ref.at[me], send_sems.at[k], recv_sems.at[k],
                                              device_id=dev, device_id_type=MESH)
            cp.start()
            sends.append(cp)
        for k in range(NDEV - 1):
            dev, pidx = _peer(k)
            pltpu.make_async_remote_copy(v_ref, buf_ref.at[pidx], send_sems.at[k], recv_sems.at[k],
                                         device_id=dev, device_id_type=MESH).wait_recv()
        for cp in sends:
            cp.wait_send()
        total = buf_ref[0]
        for r in range(1, NDEV):
            total = total + buf_ref[r]
        sum_ref[...] = total
        for r in range(NDEV):
            keep_ref[r] = buf_ref[r, 0:keep, :]

    return pl.pallas_call(
        body, name="gather_and_sum_rows",
        out_shape=(jax.ShapeDtypeStruct((rows, D), F32), jax.ShapeDtypeStruct((NDEV, keep, D), F32)),
        in_specs=[VSPEC], out_specs=(VSPEC, VSPEC),
        scratch_shapes=[pltpu.VMEM((NDEV, rows, D), F32), pltpu.SemaphoreType.DMA((NDEV - 1,)),
                        pltpu.SemaphoreType.DMA((NDEV - 1,))],
        compiler_params=_params(),
    )(vec)


def in_proj(x, gain, sc, sh, w_all, b_in):
    t = x.shape[0]
    tm = min(512, t)

    def body(x_ref, g_ref, sc_ref, sh_ref, w_ref, b_ref, p_ref, h_ref, h_sc):
        @pl.when(pl.program_id(1) == 0)
        def _():
            h = f_modulate(x_ref[...], g_ref[...], sc_ref[...], sh_ref[...]).astype(BF16)
            h_sc[...] = h
            h_ref[...] = h

        p_ref[...] = _dot(h_sc[...], w_ref[...], _NN) + b_ref[...]

    vec = pl.BlockSpec((1, D), lambda m, n: (0, 0))
    return pl.pallas_call(
        body, name="in_proj", grid=(t // tm, NDEV),
        out_shape=(jax.ShapeDtypeStruct((t, NDEV * D), F32), jax.ShapeDtypeStruct((t, D), BF16)),
        in_specs=[pl.BlockSpec((tm, D), lambda m, n: (m, 0)), vec, vec, vec,
                  pl.BlockSpec((None, D, D), lambda m, n: (n, 0, 0)), pl.BlockSpec((1, D), lambda m, n: (0, n))],
        out_specs=(pl.BlockSpec((tm, D), lambda m, n: (m, n)), pl.BlockSpec((tm, D), lambda m, n: (m, 0))),
        scratch_shapes=[pltpu.VMEM((tm, D), BF16)],
        compiler_params=_params("parallel", "arbitrary"),
    )(x, gain, sc, sh, w_all, b_in)


def hgrn_local(p, logits):
    t = p.shape[0]
    nc = t // CH

    def body(q_ref, f_ref, v_ref, l_ref, intra_ref, qe_ref, ut_ref, dec_ref):
        intra, qe, ut, dec = f_hgrn_chunk(q_ref[...], f_ref[...], v_ref[...], l_ref[...])
        intra_ref[...] = intra
        qe_ref[...] = qe.astype(BF16)
        ut_ref[...] = ut
        dec_ref[...] = dec

    col = lambda j: pl.BlockSpec((CH, D), lambda c: (c, j))
    return pl.pallas_call(
        body, name="hgrn_local", grid=(nc,),
        out_shape=(jax.ShapeDtypeStruct((t, D), F32), jax.ShapeDtypeStruct((t, D), BF16),
                   jax.ShapeDtypeStruct((nc, DK, D), F32), jax.ShapeDtypeStruct((nc, 1, D), F32)),
        in_specs=[col(0), col(1), col(2), pl.BlockSpec((2, D), lambda c: (0, 0))],
        out_specs=(pl.BlockSpec((CH, D), lambda c: (c, 0)), pl.BlockSpec((CH, D), lambda c: (c, 0)),
                   pl.BlockSpec((None, DK, D), lambda c: (c, 0, 0)), pl.BlockSpec((None, 1, D), lambda c: (c, 0, 0))),
        compiler_params=_params("parallel"),
    )(p, p, p, logits)


def state_scan(u, dec, reverse):
    nc = u.shape[0]

    def body(u_ref, d_ref, out_ref, st):
        @pl.when(pl.program_id(0) == 0)
        def _():
            st[...] = jnp.zeros_like(st)

        s = st[...]
        out_ref[...] = s
        st[...] = d_ref[...] * s + u_ref[...]

    idx = (lambda i: (nc - 1 - i, 0, 0)) if reverse else (lambda i: (i, 0, 0))
    return pl.pallas_call(
        body, name="state_scan_rev" if reverse else "state_scan", grid=(nc,),
        out_shape=jax.ShapeDtypeStruct((nc, DK, D), F32),
        in_specs=[pl.BlockSpec((None, DK, D), idx), pl.BlockSpec((None, 1, D), idx)],
        out_specs=pl.BlockSpec((None, DK, D), idx),
        scratch_shapes=[pltpu.VMEM((DK, D), F32)],
        compiler_params=_params("arbitrary"),
    )(u, dec)


def _conv_tile(t):
    return min(256, t)


def conv_forward(p, dw, db, ln_g, ln_b):
    t = p.shape[0]
    tm = _conv_tile(t)
    per = tm // HALO

    def body(cv_ref, cg_ref, cvp_ref, cgp_ref, dw_ref, db_ref, g_ref, b_ref, yc_ref, act_ref, uext):
        first = pl.program_id(0) == 0
        uext[0:HALO, :] = jnp.where(first, 0.0, f_glu(cvp_ref[...], cgp_ref[...]))
        uext[HALO:HALO + tm, :] = f_glu(cv_ref[...], cg_ref[...])
        acc = jnp.zeros((tm, D), F32) + db_ref[...]
        for w in range(KW):
            acc = acc + dw_ref[w:w + 1, :] * uext[pl.ds(HALO - KW + 1 + w, tm), :]
        yc_ref[...] = acc
        act_ref[...] = f_conv_act(acc, g_ref[...], b_ref[...]).astype(BF16)

    vec = pl.BlockSpec((1, D), lambda m: (0, 0))
    prev = lambda j: pl.BlockSpec((HALO, D), lambda m: (jnp.maximum(m * per - 1, 0), j))
    return pl.pallas_call(
        body, name="conv_forward", grid=(t // tm,),
        out_shape=(jax.ShapeDtypeStruct((t, D), F32), jax.ShapeDtypeStruct((t, D), BF16)),
        in_specs=[pl.BlockSpec((tm, D), lambda m: (m, 4)), pl.BlockSpec((tm, D), lambda m: (m, 5)), prev(4), prev(5),
                  pl.BlockSpec((HALO, D), lambda m: (0, 0)), vec, vec, vec],
        out_specs=(pl.BlockSpec((tm, D), lambda m: (m, 0)), pl.BlockSpec((tm, D), lambda m: (m, 0))),
        scratch_shapes=[pltpu.VMEM((HALO + tm, D), F32)],
        compiler_params=_params("parallel"),
    )(p, p, p, p, dw, db, ln_g, ln_b)


def _mix_tile(t):
    return min(256, t)


def _mix_forward_tile(x_ref, og_ref, ga_ref, gb_ref, qe_ref, intra_ref, st_ref, act_ref, wa_ref, wb_ref, wo_ref,
                      hg_ref, rows):
    o = _inter(qe_ref[...], st_ref, rows) + intra_ref[...]
    a = f_head_out(o, og_ref[...], hg_ref[...])
    ya = _dot(a.astype(BF16), wa_ref[...], _NN)
    yb = _dot(act_ref[...], wb_ref[...], _NN)
    merged = f_merge(ga_ref[...], gb_ref[...], ya, yb)
    y = _dot(merged.astype(BF16), wo_ref[...], _NN)
    return o, a, ya, yb, merged, y


def _mix_specs(tm):
    col = lambda j: pl.BlockSpec((tm, D), lambda m: (m, j))
    row = pl.BlockSpec((tm, D), lambda m: (m, 0))
    return col, row, pl.BlockSpec((1, D), lambda m: (0, 0)), pl.BlockSpec((tm // CH, DK, D), lambda m: (m, 0, 0))


def mix_forward(x, p, qe, intra, st_prev, act, wa, wb, wo, hg, g1, post):
    t = x.shape[0]
    tm = _mix_tile(t)

    def body(x_ref, og_ref, ga_ref, gb_ref, qe_ref, intra_ref, st_ref, act_ref, wa_ref, wb_ref, wo_ref, hg_ref,
             g1_ref, post_ref, x1_ref):
        y = _mix_forward_tile(x_ref, og_ref, ga_ref, gb_ref, qe_ref, intra_ref, st_ref, act_ref, wa_ref, wb_ref,
                              wo_ref, hg_ref, tm)[-1]
        x1_ref[...] = f_residual(x_ref[...], y, g1_ref[...], post_ref[...])

    col, row, vec, stspec = _mix_specs(tm)
    return pl.pallas_call(
        body, name="mix_forward", grid=(t // tm,),
        out_shape=jax.ShapeDtypeStruct((t, D), F32),
        in_specs=[row, col(3), col(6), col(7), row, row, stspec, row, VSPEC, VSPEC, VSPEC, vec, vec, vec],
        out_specs=row,
        compiler_params=_params("parallel"),
    )(x, p, p, p, qe, intra, st_prev, act, wa, wb, wo, hg, g1, post)


def ffn_forward_backward(x1, target, w1, w2, pre, sc, sh, g2, post):
    t = x1.shape[0]
    tm = min(256, t)
    nb = w1.shape[0]
    fb = w1.shape[2]

    def body(x_ref, tg_ref, w1_ref, w2_ref, pre_ref, sc_ref, sh_ref, g2_ref, post_ref,
             dx_ref, h2_ref, r_ref, dz_ref, dy2_ref, acc_ref, loss_ref, z_sc):
        @pl.when(pl.program_id(0) == 0)
        def _():
            acc_ref[...] = jnp.zeros_like(acc_ref)
            loss_ref[...] = jnp.zeros_like(loss_ref)

        x1v = x_ref[...]
        h2, vjp_h = jax.vjp(f_modulate, x1v, pre_ref[...], sc_ref[...], sh_ref[...])
        h2b = h2.astype(BF16)
        h2_ref[...] = h2b
        y2 = jnp.zeros((tm, D), F32)
        for n in range(nb):
            z = _dot(h2b, w1_ref[n], _NN)
            z_sc[:, n * fb:(n + 1) * fb] = z
            r = jnp.square(jnp.maximum(z, 0.0)).astype(BF16)
            r_ref[:, n * fb:(n + 1) * fb] = r
            y2 = y2 + _dot(r, w2_ref[n], _NN)
        out, vjp_r = jax.vjp(f_residual, x1v, y2, g2_ref[...], post_ref[...])
        err = out - tg_ref[...]
        tok = jnp.mean(jnp.square(err), axis=-1, keepdims=True)
        loss_ref[...] += 0.5 * jnp.sum(tok, axis=0, keepdims=True)
        dx_a, dy2, dg2, dpost = vjp_r(err * (1.0 / D))
        dy2b = dy2.astype(BF16)
        dy2_ref[...] = dy2b
        dh2 = jnp.zeros((tm, D), F32)
        for n in range(nb):
            dr = _dot(dy2b, w2_ref[n], _NT)
            dz = (dr * (2.0 * jnp.maximum(z_sc[:, n * fb:(n + 1) * fb], 0.0))).astype(BF16)
            dz_ref[:, n * fb:(n + 1) * fb] = dz
            dh2 = dh2 + _dot(dz, w1_ref[n], _NT)
        dx_b, dpre, dsc, dsh = vjp_h(dh2)
        dx_ref[...] = dx_a + dx_b
        acc_ref[0:1, :] += dpre
        acc_ref[1:2, :] += dpost
        acc_ref[2:3, :] += dsc
        acc_ref[3:4, :] += dsh
        acc_ref[4:5, :] += dg2

    row = pl.BlockSpec((tm, D), lambda m: (m, 0))
    wide = pl.BlockSpec((tm, DFF), lambda m: (m, 0))
    vec = pl.BlockSpec((1, D), lambda m: (0, 0))
    return pl.pallas_call(
        body, name="ffn_forward_backward", grid=(t // tm,),
        out_shape=(jax.ShapeDtypeStruct((t, D), F32), jax.ShapeDtypeStruct((t, D), BF16),
                   jax.ShapeDtypeStruct((t, DFF), BF16), jax.ShapeDtypeStruct((t, DFF), BF16),
                   jax.ShapeDtypeStruct((t, D), BF16), jax.ShapeDtypeStruct((8, D), F32),
                   jax.ShapeDtypeStruct((8, 128), F32)),
        in_specs=[row, row, VSPEC, VSPEC, vec, vec, vec, vec, vec],
        out_specs=(row, row, wide, wide, row, pl.BlockSpec((8, D), lambda m: (0, 0)),
                   pl.BlockSpec((8, 128), lambda m: (0, 0))),
        scratch_shapes=[pltpu.VMEM((tm, DFF), F32)],
        compiler_params=_params("arbitrary"),
    )(x1, target, w1, w2, pre, sc, sh, g2, post)


def mix_backward(x, p, qe, intra, st_prev, act, wa, wb, wo, hg, g1, post, dx1):
    t = x.shape[0]
    tm = _mix_tile(t)
    nc = t // CH

    def body(x_ref, og_ref, ga_ref, gb_ref, qe_ref, intra_ref, st_ref, act_ref, wa_ref, wb_ref, wo_ref, hg_ref,
             g1_ref, post_ref, dx1_ref,
             dog_ref, dga_ref, dgb_ref, do_ref, dqe_ref, vt_ref, dact_ref, a_ref, mg_ref, dya_ref, dyb_ref, dy_ref,
             acc_ref):
        @pl.when(pl.program_id(0) == 0)
        def _():
            acc_ref[...] = jnp.zeros_like(acc_ref)

        o, a, ya, yb, merged, y = _mix_forward_tile(x_ref, og_ref, ga_ref, gb_ref, qe_ref, intra_ref, st_ref, act_ref,
                                                    wa_ref, wb_ref, wo_ref, hg_ref, tm)
        a_ref[...] = a.astype(BF16)
        mg_ref[...] = merged.astype(BF16)
        _, vjp_r = jax.vjp(f_residual, x_ref[...], y, g1_ref[...], post_ref[...])
        _, dy, dg1, dpost = vjp_r(dx1_ref[...])
        dyb16 = dy.astype(BF16)
        dy_ref[...] = dyb16
        dmerged = _dot(dyb16, wo_ref[...], _NT)
        _, vjp_m = jax.vjp(f_merge, ga_ref[...], gb_ref[...], ya, yb)
        dga, dgb, dya, dyb = vjp_m(dmerged)
        dga_ref[...] = dga.astype(BF16)
        dgb_ref[...] = dgb.astype(BF16)
        dya16, dyb16b = dya.astype(BF16), dyb.astype(BF16)
        dya_ref[...] = dya16
        dyb_ref[...] = dyb16b
        da = _dot(dya16, wa_ref[...], _NT)
        dact_ref[...] = _dot(dyb16b, wb_ref[...], _NT)
        _, vjp_a = jax.vjp(f_head_out, o, og_ref[...], hg_ref[...])
        do, dog, dhg = vjp_a(da)
        dog_ref[...] = dog.astype(BF16)
        do_ref[...] = do
        do16 = do.astype(BF16)
        qe16 = qe_ref[...]
        for ci in range(tm // CH):
            st = st_ref[ci].astype(BF16)
            rows = slice(ci * CH, (ci + 1) * CH)
            dqe, vt = [], []
            for h in range(HEADS):
                sl = slice(h * DK, (h + 1) * DK)
                dqe.append(_dot(do16[rows, sl], st[:, sl], _NN))
                vt.append(_dot(do16[rows, sl], qe16[rows, sl], _TN))
            dqe_ref[rows, :] = jnp.concatenate(dqe, axis=1)
            vt_ref[ci] = jnp.concatenate(vt, axis=1)
        acc_ref[0:1, :] += dg1
        acc_ref[1:2, :] += dpost
        acc_ref[2:3, :] += dhg
        acc_ref[3:4, :] += jnp.sum(dog, axis=0, keepdims=True)
        acc_ref[4:5, :] += jnp.sum(dga, axis=0, keepdims=True)
        acc_ref[5:6, :] += jnp.sum(dgb, axis=0, keepdims=True)

    col, row, vec, stspec = _mix_specs(tm)
    b16 = jax.ShapeDtypeStruct((t, D), BF16)
    f32 = jax.ShapeDtypeStruct((t, D), F32)
    return pl.pallas_call(
        body, name="mix_backward", grid=(t // tm,),
        out_shape=(b16, b16, b16, f32, f32, jax.ShapeDtypeStruct((nc, DK, D), F32), f32, b16, b16, b16, b16, b16,
                   jax.ShapeDtypeStruct((8, D), F32)),
        in_specs=[row, col(3), col(6), col(7), row, row, stspec, row, VSPEC, VSPEC, VSPEC, vec, vec, vec, row],
        out_specs=(row, row, row, row, row, stspec, row, row, row, row, row, row,
                   pl.BlockSpec((8, D), lambda m: (0, 0))),
        compiler_params=_params("arbitrary"),
    )(x, p, p, p, qe, intra, st_prev, act, wa, wb, wo, hg, g1, post, dx1)


def conv_backward(p, yc, dact, dw, ln_g, ln_b):
    t = p.shape[0]
    tm = _conv_tile(t)
    per = tm // HALO
    nt = t // tm

    def body(cv_ref, cg_ref, cvp_ref, cgp_ref, yc_ref, ycn_ref, da_ref, dan_ref, dw_ref, g_ref, b_ref,
             dcv_ref, dcg_ref, acc_ref, ddw_ref, uext, dyext):
        m = pl.program_id(0)

        @pl.when(m == 0)
        def _():
            acc_ref[...] = jnp.zeros_like(acc_ref)
            ddw_ref[...] = jnp.zeros_like(ddw_ref)

        cv, cg = cv_ref[...], cg_ref[...]
        u, vjp_u = jax.vjp(f_glu, cv, cg)
        uext[0:HALO, :] = jnp.where(m == 0, 0.0, f_glu(cvp_ref[...], cgp_ref[...]))
        uext[HALO:HALO + tm, :] = u
        _, vjp_c = jax.vjp(f_conv_act, yc_ref[...], g_ref[...], b_ref[...])
        dyc, dg, db = vjp_c(da_ref[...])
        _, vjp_n = jax.vjp(f_conv_act, ycn_ref[...], g_ref[...], b_ref[...])
        dyn = vjp_n(dan_ref[...])[0]
        dyext[0:tm, :] = dyc
        dyext[tm:tm + HALO, :] = jnp.where(m == nt - 1, 0.0, dyn)
        du = jnp.zeros((tm, D), F32)
        for w in range(KW):
            du = du + dw_ref[w:w + 1, :] * dyext[pl.ds(KW - 1 - w, tm), :]
            ddw_ref[w:w + 1, :] += jnp.sum(dyc * uext[pl.ds(HALO - KW + 1 + w, tm), :], axis=0, keepdims=True)
        dcv, dcg = vjp_u(du)
        dcv_ref[...] = dcv.astype(BF16)
        dcg_ref[...] = dcg.astype(BF16)
        acc_ref[0:1, :] += jnp.sum(dyc, axis=0, keepdims=True)
        acc_ref[1:2, :] += dg
        acc_ref[2:3, :] += db
        acc_ref[3:4, :] += jnp.sum(dcv, axis=0, keepdims=True)
        acc_ref[4:5, :] += jnp.sum(dcg, axis=0, keepdims=True)

    vec = pl.BlockSpec((1, D), lambda m: (0, 0))
    row = pl.BlockSpec((tm, D), lambda m: (m, 0))
    prev = lambda j: pl.BlockSpec((HALO, D), lambda m: (jnp.maximum(m * per - 1, 0), j))
    nxt = pl.BlockSpec((HALO, D), lambda m: (jnp.minimum((m + 1) * per, t // HALO - 1), 0))
    b16 = jax.ShapeDtypeStruct((t, D), BF16)
    return pl.pallas_call(
        body, name="conv_backward", grid=(nt,),
        out_shape=(b16, b16, jax.ShapeDtypeStruct((8, D), F32), jax.ShapeDtypeStruct((HALO, D), F32)),
        in_specs=[pl.BlockSpec((tm, D), lambda m: (m, 4)), pl.BlockSpec((tm, D), lambda m: (m, 5)), prev(4), prev(5),
                  row, nxt, row, nxt, pl.BlockSpec((HALO, D), lambda m: (0, 0)), vec, vec],
        out_specs=(row, row, pl.BlockSpec((8, D), lambda m: (0, 0)), pl.BlockSpec((HALO, D), lambda m: (0, 0))),
        scratch_shapes=[pltpu.VMEM((HALO + tm, D), F32), pltpu.VMEM((tm + HALO, D), F32)],
        compiler_params=_params("arbitrary"),
    )(p, p, p, p, yc, yc, dact, dact, dw, ln_g, ln_b)


def hgrn_backward(p, logits, do, dqe, gt, st_prev):
    t = p.shape[0]
    nc = t // CH

    def body(q_ref, f_ref, v_ref, l_ref, do_ref, dqe_ref, gt_ref, st_ref, dq_ref, df_ref, dv_ref, acc_ref):
        @pl.when(pl.program_id(0) == 0)
        def _():
            acc_ref[...] = jnp.zeros_like(acc_ref)

        gt_v = gt_ref[...]
        ddec = jnp.sum(gt_v * st_ref[...], axis=0, keepdims=True)
        _, vjp = jax.vjp(f_hgrn_chunk, q_ref[...], f_ref[...], v_ref[...], l_ref[...])
        dq, df, dv, dl = vjp((do_ref[...], dqe_ref[...], gt_v, ddec))
        dq_ref[...] = dq.astype(BF16)
        df_ref[...] = df.astype(BF16)
        dv_ref[...] = dv.astype(BF16)
        acc_ref[0:2, :] += dl
        acc_ref[2:3, :] += jnp.sum(dq, axis=0, keepdims=True)
        acc_ref[3:4, :] += jnp.sum(df, axis=0, keepdims=True)
        acc_ref[4:5, :] += jnp.sum(dv, axis=0, keepdims=True)

    col = lambda j: pl.BlockSpec((CH, D), lambda c: (c, j))
    row = pl.BlockSpec((CH, D), lambda c: (c, 0))
    stspec = pl.BlockSpec((None, DK, D), lambda c: (c, 0, 0))
    b16 = jax.ShapeDtypeStruct((t, D), BF16)
    return pl.pallas_call(
        body, name="hgrn_backward", grid=(nc,),
        out_shape=(b16, b16, b16, jax.ShapeDtypeStruct((8, D), F32)),
        in_specs=[col(0), col(1), col(2), pl.BlockSpec((2, D), lambda c: (0, 0)), row, row, stspec, stspec],
        out_specs=(row, row, row, pl.BlockSpec((8, D), lambda c: (0, 0))),
        compiler_params=_params("arbitrary"),
    )(p, p, p, logits, do, dqe, gt, st_prev)


def in_proj_backward(dps, w_all, x, dx1, gain, sc, sh):
    t = x.shape[0]
    tm = min(256, t)

    def body(*refs):
        dp_refs = refs[:NDEV]
        w_ref, x_ref, dx1_ref, g_ref, sc_ref, sh_ref, gx_ref, acc_ref = refs[NDEV:]

        @pl.when(pl.program_id(0) == 0)
        def _():
            acc_ref[...] = jnp.zeros_like(acc_ref)

        dh = jnp.zeros((tm, D), F32)
        for n in range(NDEV):
            dh = dh + _dot(dp_refs[n][...], w_ref[n], _NT)
        _, vjp_h = jax.vjp(f_modulate, x_ref[...], g_ref[...], sc_ref[...], sh_ref[...])
        dx, dg, dsc, dsh = vjp_h(dh)
        gx_ref[...] = dx1_ref[...] + dx
        acc_ref[0:1, :] += dg
        acc_ref[1:2, :] += dsc
        acc_ref[2:3, :] += dsh

    row = pl.BlockSpec((tm, D), lambda m: (m, 0))
    vec = pl.BlockSpec((1, D), lambda m: (0, 0))
    return pl.pallas_call(
        body, name="in_proj_backward", grid=(t // tm,),
        out_shape=(jax.ShapeDtypeStruct((t, D), F32), jax.ShapeDtypeStruct((8, D), F32)),
        in_specs=[row] * NDEV + [VSPEC, row, row, vec, vec, vec],
        out_specs=(row, pl.BlockSpec((8, D), lambda m: (0, 0))),
        compiler_params=_params("arbitrary"),
    )(*dps, w_all, x, dx1, gain, sc, sh)


def weight_grad(a, b, nblk, ka, bn, a_blocked, name):
    t = a.shape[0]
    tk = min(512, t)
    nk = t // tk

    def body(a_ref, b_ref, f_ref, h_ref, acc):
        k = pl.program_id(1)

        @pl.when(k == 0)
        def _():
            acc[...] = jnp.zeros_like(acc)

        acc[...] += _dot(a_ref[...], b_ref[...], _TN)

        @pl.when(k == nk - 1)
        def _():
            f_ref[...] = acc[...]
            h_ref[...] = acc[...].astype(BF16)

    a_idx = (lambda n, k: (k, n)) if a_blocked else (lambda n, k: (k, 0))
    b_idx = (lambda n, k: (k, 0)) if a_blocked else (lambda n, k: (k, n))
    out = pl.BlockSpec((None, ka, bn), lambda n, k: (n, 0, 0))
    return pl.pallas_call(
        body, name=name, grid=(nblk, nk),
        out_shape=(jax.ShapeDtypeStruct((nblk, ka, bn), F32), jax.ShapeDtypeStruct((nblk, ka, bn), BF16)),
        in_specs=[pl.BlockSpec((tk, ka), a_idx), pl.BlockSpec((tk, bn), b_idx)],
        out_specs=(out, out),
        scratch_shapes=[pltpu.VMEM((ka, bn), F32)],
        compiler_params=_params("parallel", "arbitrary"),
    )(a, b)


def ada_backward(call_t, dmod_cols, w, m, v):
    def body(c_ref, d_ref, w_ref, m_ref, v_ref, g_ref, dl_ref, nm_ref, nv_ref):
        ct, dm = c_ref[...], d_ref[...]
        g = ct[:, 0:1] * dm[0:1, :]
        for r in range(1, NDEV):
            g = g + ct[:, r:r + 1] * dm[r:r + 1, :]
        g_ref[...] = g
        dl_ref[...], nm_ref[...], nv_ref[...] = _adamw(w_ref[...], g, m_ref[...], v_ref[...])

    s = jax.ShapeDtypeStruct(w.shape, F32)
    return pl.pallas_call(
        body, name="ada_backward", out_shape=(s, s, s, s),
        in_specs=[VSPEC] * 5, out_specs=(VSPEC,) * 4, compiler_params=_params(),
    )(call_t, dmod_cols, w, m, v)


def adamw_rows(w, g, m, v, name):
    def body(w_ref, g_ref, m_ref, v_ref, dl_ref, nm_ref, nv_ref):
        dl_ref[...], nm_ref[...], nv_ref[...] = _adamw(w_ref[...], g_ref[...], m_ref[...], v_ref[...])

    s = jax.ShapeDtypeStruct(w.shape, F32)
    return pl.pallas_call(
        body, name=name, out_shape=(s, s, s), in_specs=[VSPEC] * 4, out_specs=(VSPEC,) * 3, compiler_params=_params(),
    )(w, g, m, v)


def reduce_and_adamw(w, g_own, g_recv, m, v, name):
    r, c = w.shape
    br = min(128, r)

    def body(w_ref, go_ref, gr_ref, m_ref, v_ref, g_ref, dl_ref, nm_ref, nv_ref):
        g = go_ref[...]
        for k in range(NDEV - 1):
            g = g + gr_ref[k].astype(F32)
        g_ref[...] = g
        dl_ref[...], nm_ref[...], nv_ref[...] = _adamw(w_ref[...], g, m_ref[...], v_ref[...])

    blk = pl.BlockSpec((br, c), lambda i: (i, 0))
    s = jax.ShapeDtypeStruct(w.shape, F32)
    return pl.pallas_call(
        body, name=name, grid=(r // br,), out_shape=(s, s, s, s),
        in_specs=[blk, blk, pl.BlockSpec((NDEV - 1, br, c), lambda i: (0, i, 0)), blk, blk],
        out_specs=(blk, blk, blk, blk), compiler_params=_params("parallel"),
    )(w, g_own, g_recv, m, v)


def kernel(x, c, w_ada, b_ada, pre_norm_tm, post_norm_tm, pre_norm_cm, post_norm_cm, w_in, b_in, hg_lb_logits, hg_norm, conv_dw, conv_db, conv_ln_g, conv_ln_b, w_br_a, w_br_b, w_out, w_ff1, w_ff2, loss_target, m_w_ada, m_b_ada, m_pre_norm_tm, m_post_norm_tm, m_pre_norm_cm, m_post_norm_cm, m_w_in, m_b_in, m_hg_lb_logits, m_hg_norm, m_conv_dw, m_conv_db, m_conv_ln_g, m_conv_ln_b, m_w_br_a, m_w_br_b, m_w_out, m_w_ff1, m_w_ff2, v_w_ada, v_b_ada, v_pre_norm_tm, v_post_norm_tm, v_pre_norm_cm, v_post_norm_cm, v_w_in, v_b_in, v_hg_lb_logits, v_hg_norm, v_conv_dw, v_conv_db, v_conv_ln_g, v_conv_ln_b, v_w_br_a, v_w_br_b, v_w_out, v_w_ff1, v_w_ff2):
    t = x.shape[1]
    me = 4 * lax.axis_index("x") + 2 * lax.axis_index("y") + lax.axis_index("c")
    xs = x[0]
    tgt = loss_target[0]

    mod, call = ada_forward(c, w_ada[0], b_ada)
    sh1, sc1, g1, sh2, sc2, g2 = [mod[:, i * D:(i + 1) * D] for i in range(6)]
    shards = [w_in[0].astype(BF16), w_br_a[0].astype(BF16), w_br_b[0].astype(BF16), w_out[0].astype(BF16),
              w_ff1[0].astype(BF16), w_ff2[0].astype(BF16), conv_dw[0]]
    win_all, wa_all, wb_all, wo_all, w1_all, w2_all, dw_all = all_gather_blocks(shards)
    wa, wb, wo = wa_all.reshape(D, D), wb_all.reshape(D, D), wo_all.reshape(D, D)
    dw = jnp.pad(dw_all.transpose(1, 0, 2).reshape(KW, D), ((0, HALO - KW), (0, 0)))

    p, h = in_proj(xs, pre_norm_tm, sc1, sh1, win_all, b_in)
    intra, qe, ut, dec = hgrn_local(p, hg_lb_logits)
    st_prev = state_scan(ut, dec, False)
    yc, act = conv_forward(p, dw, conv_db, conv_ln_g, conv_ln_b)
    x1 = mix_forward(xs, p, qe, intra, st_prev, act, wa, wb, wo, hg_norm, g1, post_norm_tm)

    dx1, h2, r, dz, dy2, acc_ffn, loss_blk = ffn_forward_backward(x1, tgt, w1_all, w2_all, pre_norm_cm, sc2, sh2, g2,
                                                                   post_norm_cm)
    loss = lax.psum(loss_blk[0, 0], AXES)

    (dog, dga, dgb, do, dqe, vt, dact, a16, mg16, dya, dyb, dy, acc_mix) = mix_backward(
        xs, p, qe, intra, st_prev, act, wa, wb, wo, hg_norm, g1, post_norm_tm, dx1)
    gt = state_scan(vt, dec, True)
    dq, df, di, acc_hg = hgrn_backward(p, hg_lb_logits, do, dqe, gt, st_prev)
    dcv, dcg, acc_conv, ddw = conv_backward(p, yc, dact, dw, conv_ln_g, conv_ln_b)
    dps = [dq, df, di, dog, dcv, dcg, dga, dgb]
    grad_x, acc_in = in_proj_backward(dps, win_all, xs, dx1, pre_norm_tm, sc1, sh1)

    gw_in = [weight_grad(h, dps[n], 1, D, D, False, f"grad_w_in_{n}") for n in range(NDEV)]
    gin_f = jnp.concatenate([g[0] for g in gw_in], axis=0)
    gin_h = jnp.concatenate([g[1] for g in gw_in], axis=0)
    ga_f, ga_h = weight_grad(a16, dya, 1, D, D, False, "grad_w_br_a")
    gb_f, gb_h = weight_grad(act, dyb, 1, D, D, False, "grad_w_br_b")
    go_f, go_h = weight_grad(mg16, dy, 1, D, D, False, "grad_w_out")
    g1_f, g1_h = weight_grad(h2, dz, NDEV, D, DFF // NDEV, False, "grad_w_ff1")
    g2_f, g2_h = weight_grad(r, dy2, 4, D, D, True, "grad_w_ff2")
    rows = D // NDEV
    own = [gin_f, ga_f.reshape(NDEV, rows, D), gb_f.reshape(NDEV, rows, D), go_f.reshape(NDEV, rows, D), g1_f,
           g2_f.reshape(NDEV, DFF // NDEV, D)]
    travel = [gin_h, ga_h.reshape(NDEV, rows, D), gb_h.reshape(NDEV, rows, D), go_h.reshape(NDEV, rows, D), g1_h,
              g2_h.reshape(NDEV, DFF // NDEV, D)]
    recv = exchange_partials(travel)
    big = {}
    names = ["w_in", "w_br_a", "w_br_b", "w_out", "w_ff1", "w_ff2"]
    ws = [w_in, w_br_a, w_br_b, w_out, w_ff1, w_ff2]
    ms = [m_w_in, m_w_br_a, m_w_br_b, m_w_out, m_w_ff1, m_w_ff2]
    vs = [v_w_in, v_w_br_a, v_w_br_b, v_w_out, v_w_ff1, v_w_ff2]
    for i, nm in enumerate(names):
        g_own = lax.dynamic_index_in_dim(own[i], me, 0, keepdims=False)
        big[nm] = [o[None] for o in reduce_and_adamw(ws[i][0], g_own, recv[i], ms[i][0], vs[i][0], "adamw_" + nm)]

    dmod = jnp.concatenate([acc_in[2:3], acc_in[1:2], acc_mix[0:1], acc_ffn[3:4], acc_ffn[2:3], acc_ffn[4:5]], axis=0)
    db_in = jnp.concatenate([acc_hg[2:5], acc_mix[3:4], acc_conv[3:5], acc_mix[4:6]], axis=0)
    small = jnp.concatenate([
        dmod,
        acc_in[0:1], acc_mix[1:2], acc_ffn[0:1], acc_ffn[1:2],
        db_in,
        acc_hg[0:2],
        acc_mix[2:3],
        acc_conv[0:3],
        ddw,
    ], axis=0)
    total, kept = gather_and_sum_rows(small, 8)
    dmod_all = kept[:, 0:6, :].reshape(NDEV, 6 * D)
    wcols = w_ada.shape[2]
    gwa, dwa, nmwa, nvwa = ada_backward(call.T, lax.dynamic_slice_in_dim(dmod_all, me * wcols, wcols, axis=1),
                                        w_ada[0], m_w_ada[0], v_w_ada[0])
    ddw_mine = lax.dynamic_slice_in_dim(total[24:24 + KW], me * (D // NDEV), D // NDEV, axis=1)

    small_names = ["b_ada", "pre_norm_tm", "post_norm_tm", "pre_norm_cm", "post_norm_cm", "b_in", "hg_lb_logits",
                   "hg_norm", "conv_db", "conv_ln_g", "conv_ln_b", "conv_dw"]
    small_w = [b_ada, pre_norm_tm, post_norm_tm, pre_norm_cm, post_norm_cm, b_in, hg_lb_logits, hg_norm, conv_db,
               conv_ln_g, conv_ln_b, conv_dw]
    small_m = [m_b_ada, m_pre_norm_tm, m_post_norm_tm, m_pre_norm_cm, m_post_norm_cm, m_b_in, m_hg_lb_logits,
               m_hg_norm, m_conv_db, m_conv_ln_g, m_conv_ln_b, m_conv_dw]
    small_v = [v_b_ada, v_pre_norm_tm, v_post_norm_tm, v_pre_norm_cm, v_post_norm_cm, v_b_in, v_hg_lb_logits,
               v_hg_norm, v_conv_db, v_conv_ln_g, v_conv_ln_b, v_conv_dw]
    small_g = [total[0:6].reshape(1, 6 * D), total[6:7], total[7:8], total[8:9], total[9:10],
               total[10:18].reshape(1, 8 * D), total[18:20], total[20:21], total[21:22], total[22:23], total[23:24],
               ddw_mine[None]]
    sizes = [int(w.size) for w in small_w]
    padded = -(-sum(sizes) // (8 * D)) * (8 * D)

    def pack(arrs):
        flat = jnp.concatenate([a.reshape(-1) for a in arrs])
        return jnp.pad(flat, (0, padded - flat.shape[0])).reshape(padded // D, D)

    packed = adamw_rows(pack(small_w), pack(small_g), pack(small_m), pack(small_v), "adamw_small")
    sm = {}
    off = 0
    for nm, w, g, sz in zip(small_names, small_w, small_g, sizes):
        sm[nm] = [g.reshape(w.shape)] + [pk.reshape(-1)[off:off + sz].reshape(w.shape) for pk in packed]
        off += sz

    order = ["w_ada", "b_ada", "pre_norm_tm", "post_norm_tm", "pre_norm_cm", "post_norm_cm", "w_in", "b_in",
             "hg_lb_logits", "hg_norm", "conv_dw", "conv_db", "conv_ln_g", "conv_ln_b", "w_br_a", "w_br_b", "w_out",
             "w_ff1", "w_ff2"]
    res = dict(sm)
    res.update(big)
    res["w_ada"] = [gwa[None], dwa[None], nmwa[None], nvwa[None]]
    outs = [loss, grad_x[None]]
    for j in range(4):
        outs += [res[nm][j] for nm in order]
    return tuple(outs)
```

```python
import functools

import jax
import jax.numpy as jnp
from jax import lax
from jax.experimental import pallas as pl
from jax.experimental.pallas import tpu as pltpu

F32 = jnp.float32
BF16 = jnp.bfloat16
MESH = pl.DeviceIdType.MESH
AXES = ("x", "y", "c")

D = 1024
HEADS = 8
DK = 128
CH = 128
LEVELS = 7
NSEG = 1 + 2 * LEVELS
KW = 31
HALO = 32
DFF = 4096
NDEV = 8
EPS = 1e-6
ADAM_LR, ADAM_B1, ADAM_B2, ADAM_EPS, ADAM_WD, ADAM_STEP = 0.001, 0.9, 0.999, 1e-08, 0.01, 10
VMEM_LIMIT = 58 * 1024 * 1024

_NN = (((1,), (0,)), ((), ()))
_NT = (((1,), (1,)), ((), ()))
_TN = (((0,), (0,)), ((), ()))

VSPEC = pl.BlockSpec(memory_space=pltpu.VMEM)
ANYSPEC = pl.BlockSpec(memory_space=pl.ANY)


def _params(*sem):
    return pltpu.CompilerParams(dimension_semantics=sem or None, vmem_limit_bytes=VMEM_LIMIT)


def _dot(a, b, dims):
    return lax.dot_general(a, b, dims, preferred_element_type=F32)


@jax.custom_vjp
def mm_nn(a, b):
    return _dot(a.astype(BF16), b.astype(BF16), _NN)


def _mm_nn_fwd(a, b):
    ab, bb = a.astype(BF16), b.astype(BF16)
    return _dot(ab, bb, _NN), (ab, bb)


def _mm_nn_bwd(res, ct):
    ab, bb = res
    cb = ct.astype(BF16)
    return _dot(cb, bb, _NT), _dot(ab, cb, _TN)


mm_nn.defvjp(_mm_nn_fwd, _mm_nn_bwd)


@jax.custom_vjp
def mm_nt(a, b):
    return _dot(a.astype(BF16), b.astype(BF16), _NT)


def _mm_nt_fwd(a, b):
    ab, bb = a.astype(BF16), b.astype(BF16)
    return _dot(ab, bb, _NT), (ab, bb)


def _mm_nt_bwd(res, ct):
    ab, bb = res
    cb = ct.astype(BF16)
    return _dot(cb, bb, _NN), _dot(cb, ab, _TN)


mm_nt.defvjp(_mm_nt_fwd, _mm_nt_bwd)


@jax.custom_vjp
def mm_tn(a, b):
    return _dot(a.astype(BF16), b.astype(BF16), _TN)


def _mm_tn_fwd(a, b):
    ab, bb = a.astype(BF16), b.astype(BF16)
    return _dot(ab, bb, _TN), (ab, bb)


def _mm_tn_bwd(res, ct):
    ab, bb = res
    cb = ct.astype(BF16)
    return _dot(bb, cb, _NT), _dot(ab, cb, _NN)


mm_tn.defvjp(_mm_tn_fwd, _mm_tn_bwd)


def _rms(x):
    return x * lax.rsqrt(jnp.mean(x * x, axis=-1, keepdims=True) + EPS)


def _silu(x):
    return x * jax.nn.sigmoid(x)


def f_modulate(x, gain, sc, sh):
    return _rms(x) * gain * (1.0 + sc) + sh


def f_residual(x, y, gate, gain):
    return x + gate * (_rms(y) * gain)


def f_merge(ga, gb, ya, yb):
    return jax.nn.sigmoid(ga) * ya + jax.nn.sigmoid(gb) * yb


def f_head_out(o, og, hg):
    heads = [_rms(o[:, h * DK:(h + 1) * DK]) for h in range(HEADS)]
    return jnp.concatenate(heads, axis=1) * hg * _silu(og)


def f_conv_act(u, g, b):
    mu = jnp.mean(u, axis=-1, keepdims=True)
    var = jnp.mean(jnp.square(u - mu), axis=-1, keepdims=True)
    return _silu((u - mu) * lax.rsqrt(var + EPS) * g + b)


def f_glu(cv, cg):
    return cv * jax.nn.sigmoid(cg)


def _split2(x):
    hi = x.astype(BF16)
    return hi, (x - hi.astype(F32)).astype(BF16)


def _seg_matrix(transposed):
    shape = (CH, CH)
    i = lax.broadcasted_iota(jnp.int32, shape, 1 if transposed else 0)
    t = lax.broadcasted_iota(jnp.int32, shape, 0 if transposed else 1)
    blocks = [t <= i]
    for lev in range(LEVELS):
        sh = LEVELS - 1 - lev
        h = 1 << sh
        upper = ((i >> sh) & 1) == 1
        mid = ((i >> (sh + 1)) << (sh + 1)) + (h - 1)
        blocks.append(upper & (t > mid) & (t <= i))
        blocks.append(jnp.logical_not(upper) & (t > i) & (t <= mid))
    blocks = [jnp.where(b, 1.0, 0.0).astype(BF16) for b in blocks]
    return jnp.concatenate(blocks, axis=1 if transposed else 0)


@jax.custom_vjp
def segsum(g):
    hi, lo = _split2(g)
    a = _seg_matrix(False)
    return _dot(a, hi, _NN) + _dot(a, lo, _NN)


def _segsum_fwd(g):
    return segsum(g), None


def _segsum_bwd(_, ct):
    hi, lo = _split2(ct)
    at = _seg_matrix(True)
    return (_dot(at, hi, _NN) + _dot(at, lo, _NN),)


segsum.defvjp(_segsum_fwd, _segsum_bwd)


def _score_masks():
    i = lax.broadcasted_iota(jnp.int32, (CH, CH), 0)
    j = lax.broadcasted_iota(jnp.int32, (CH, CH), 1)
    masks = [i == j]
    for lev in range(LEVELS):
        sh = LEVELS - 1 - lev
        same = (i >> (sh + 1)) == (j >> (sh + 1))
        masks.append(same & (((i >> sh) & 1) == 1) & (((j >> sh) & 1) == 0))
    return [jnp.where(m, 1.0, 0.0) for m in masks]


def f_hgrn_chunk(q_r, f_r, v, logits):
    l0, l1 = logits[0:1, :], logits[1:2, :]
    mx = lax.stop_gradient(jnp.maximum(l0, l1))
    e0, e1 = jnp.exp(l0 - mx), jnp.exp(l1 - mx)
    lb = e0 / (e0 + e1)
    q = _silu(q_r)
    f = lb + (1.0 - lb) * jax.nn.sigmoid(f_r)
    k = 1.0 - f
    seg = segsum(jnp.log(f))
    b = seg[0:CH]
    btot = b[CH - 1:CH, :]
    qe = q * jnp.exp(b)
    ke = k * jnp.exp(btot - b)
    dec = jnp.exp(btot)
    qs, ks = [q], [k]
    for lev in range(LEVELS):
        qs.append(q * jnp.exp(seg[(1 + 2 * lev) * CH:(2 + 2 * lev) * CH]))
        ks.append(k * jnp.exp(seg[(2 + 2 * lev) * CH:(3 + 2 * lev) * CH]))
    masks = _score_masks()
    intra, ut = [], []
    for h in range(HEADS):
        sl = slice(h * DK, (h + 1) * DK)
        sc = None
        for lev in range(LEVELS + 1):
            s = mm_nt(qs[lev][:, sl], ks[lev][:, sl]) * masks[lev]
            sc = s if sc is None else sc + s
        intra.append(mm_nn(sc, v[:, sl]))
        ut.append(mm_tn(v[:, sl], ke[:, sl]))
    return jnp.concatenate(intra, axis=1), qe, jnp.concatenate(ut, axis=1), dec


def _inter(qe_b, st_ref, rows):
    out = []
    for ci in range(rows // CH):
        st = st_ref[ci].astype(BF16)
        heads = [_dot(qe_b[ci * CH:(ci + 1) * CH, h * DK:(h + 1) * DK], st[:, h * DK:(h + 1) * DK], _NT)
                 for h in range(HEADS)]
        out.append(jnp.concatenate(heads, axis=1))
    return jnp.concatenate(out, axis=0)


def _adamw(w, g, m, v):
    m = ADAM_B1 * m + (1.0 - ADAM_B1) * g
    v = ADAM_B2 * v + (1.0 - ADAM_B2) * jnp.square(g)
    m_hat = m / (1.0 - ADAM_B1 ** ADAM_STEP)
    v_hat = v / (1.0 - ADAM_B2 ** ADAM_STEP)
    delta = -ADAM_LR * (m_hat / (jnp.sqrt(v_hat) + ADAM_EPS) + ADAM_WD * w)
    return delta, m, v


def _me():
    return lax.axis_index("x"), lax.axis_index("y"), lax.axis_index("c")


def _peer(k):
    x, y, c = _me()
    mask = k + 1
    px = (1 - x) if (mask >> 2) & 1 else x
    py = (1 - y) if (mask >> 1) & 1 else y
    pc = (1 - c) if mask & 1 else c
    return (px, py, pc), 4 * px + 2 * py + pc


def ada_forward(c, w_ada, b_ada):
    wcols = w_ada.shape[1]

    def body(c_ref, w_ref, b_ref, mod_ref, call_ref, part_ref, modp_ref, send_sems, recv_sems):
        x, y, cc = _me()
        me = 4 * x + 2 * y + cc
        call_ref[me] = _silu(c_ref[...])
        sends = []
        for k in range(NDEV - 1):
            dev, _ = _peer(k)
            cp = pltpu.make_async_remote_copy(call_ref.at[me], call_ref.at[me], send_sems.at[k], recv_sems.at[k],
                                              device_id=dev, device_id_type=MESH)
            cp.start()
            sends.append(cp)
        for k in range(NDEV - 1):
            _, pidx = _peer(k)
            pltpu.make_async_remote_copy(call_ref.at[pidx], call_ref.at[pidx], send_sems.at[k], recv_sems.at[k],
                                         device_id=_peer(k)[0], device_id_type=MESH).wait_recv()
        call = jnp.concatenate([call_ref[r] for r in range(NDEV)], axis=0)
        part = _dot(call.astype(BF16), w_ref[...].astype(BF16), _NN)
        for r in range(NDEV):
            part_ref[r] = part[r:r + 1, :]
        modp_ref[me] = part_ref[me]
        for k in range(NDEV - 1):
            dev, pidx = _peer(k)
            cp = pltpu.make_async_remote_copy(part_ref.at[pidx], modp_ref.at[me], send_sems.at[NDEV - 1 + k],
                                              recv_sems.at[NDEV - 1 + k], device_id=dev, device_id_type=MESH)
            cp.start()
            sends.append(cp)
        for k in range(NDEV - 1):
            dev, pidx = _peer(k)
            pltpu.make_async_remote_copy(part_ref.at[pidx], modp_ref.at[pidx], send_sems.at[NDEV - 1 + k],
                                         recv_sems.at[NDEV - 1 + k], device_id=dev, device_id_type=MESH).wait_recv()
        for cp in sends:
            cp.wait_send()
        mod_ref[...] = jnp.concatenate([modp_ref[r] for r in range(NDEV)], axis=1) + b_ref[...]

    mod, call = pl.pallas_call(
        body, name="ada_forward",
        out_shape=(jax.ShapeDtypeStruct((1, NDEV * wcols), F32), jax.ShapeDtypeStruct((NDEV, 1, D), F32)),
        in_specs=[VSPEC, VSPEC, VSPEC], out_specs=(VSPEC, VSPEC),
        scratch_shapes=[pltpu.VMEM((NDEV, 1, wcols), F32), pltpu.VMEM((NDEV, 1, wcols), F32),
                        pltpu.SemaphoreType.DMA((2 * (NDEV - 1),)), pltpu.SemaphoreType.DMA((2 * (NDEV - 1),))],
        compiler_params=_params(),
    )(c, w_ada, b_ada)
    return mod, call.reshape(NDEV, D)


def _comm_sems(n, local):
    sems = [pltpu.SemaphoreType.DMA((7 * n,)), pltpu.SemaphoreType.DMA((7 * n,))]
    return sems + ([pltpu.SemaphoreType.DMA((n,))] if local else [])


def _gather2_ops(ins, outs, send_sems, recv_sems, local_sems):
    n = len(ins)
    x, y, c = _me()
    me, sibling = (x, y, c), (x, y, 1 - c)
    chips = [(1 - x, y), (x, 1 - y), (1 - x, 1 - y)]

    def slot(p):
        return 4 * p[0] + 2 * p[1] + p[2]

    def copy(a, k, block, to, src=None):
        return pltpu.make_async_remote_copy(
            src_ref=outs[a].at[slot(block)] if src is None else src, dst_ref=outs[a].at[slot(block)],
            send_sem=send_sems.at[a * 7 + k], recv_sem=recv_sems.at[a * 7 + k], device_id=to, device_id_type=MESH)

    def local(a):
        return pltpu.make_async_copy(ins[a], outs[a].at[slot(me)], local_sems.at[a])

    def first(a):
        return [copy(a, 0, me, sibling, src=ins[a])] + [copy(a, 1 + j, me, (*chip, c), src=ins[a])
                                                        for j, chip in enumerate(chips)]

    def passed(a):
        return [copy(a, 4 + j, (*chip, c), sibling) for j, chip in enumerate(chips)]

    def start():
        for a in range(n):
            local(a).start()
            for cp in first(a):
                cp.start()

    def forward():
        for j, chip in enumerate(chips):
            for a in range(n):
                copy(a, 1 + j, (*chip, c), me).wait_recv()
                passed(a)[j].start()

    def finish():
        for a in range(n):
            copy(a, 0, sibling, me).wait_recv()
            for j, chip in enumerate(chips):
                copy(a, 4 + j, (*chip, 1 - c), me).wait_recv()
        for a in range(n):
            for cp in first(a) + passed(a):
                cp.wait_send()
            local(a).wait()

    return start, forward, finish


def _gather1_ops(ins, outs, send_sems, recv_sems, local_sems):
    n = len(ins)
    x, y, c = _me()
    me = 4 * x + 2 * y + c

    def copy(a, k, block):
        return pltpu.make_async_remote_copy(ins[a], outs[a].at[block], send_sems.at[a * 7 + k],
                                            recv_sems.at[a * 7 + k], device_id=_peer(k)[0], device_id_type=MESH)

    def local(a):
        return pltpu.make_async_copy(ins[a], outs[a].at[me], local_sems.at[a])

    def start():
        for a in range(n):
            local(a).start()
            for k in range(NDEV - 1):
                copy(a, k, me).start()

    def finish():
        for a in range(n):
            for k in range(NDEV - 1):
                copy(a, k, _peer(k)[1]).wait_recv()
        for a in range(n):
            for k in range(NDEV - 1):
                copy(a, k, me).wait_send()
            local(a).wait()

    return start, finish


def _exchange_ops(ins, outs, send_sems, recv_sems):
    n = len(ins)

    def copy(a, k):
        dev, pidx = _peer(k)
        return pltpu.make_async_remote_copy(ins[a].at[pidx], outs[a].at[k], send_sems.at[a * 7 + k],
                                            recv_sems.at[a * 7 + k], device_id=dev, device_id_type=MESH)

    def start():
        for k in range(NDEV - 1):
            for a in range(n):
                copy(a, k).start()

    def finish():
        for k in range(NDEV - 1):
            for a in range(n):
                copy(a, k).wait_recv()
        for k in range(NDEV - 1):
            for a in range(n):
                copy(a, k).wait_send()

    return start, finish


def _gathered(shards):
    return tuple(jax.ShapeDtypeStruct((NDEV,) + s.shape, s.dtype) for s in shards)


def _exchanged(blocks):
    return tuple(jax.ShapeDtypeStruct((NDEV - 1,) + b.shape[1:], b.dtype) for b in blocks)


def all_gather_blocks(shards):
    n = len(shards)

    def body(*refs):
        start, forward, finish = _gather2_ops(refs[:n], refs[n:2 * n], *refs[2 * n:])
        start()
        forward()
        finish()

    return pl.pallas_call(
        body, name="all_gather_blocks", out_shape=_gathered(shards),
        in_specs=[ANYSPEC] * n, out_specs=tuple([ANYSPEC] * n),
        scratch_shapes=_comm_sems(n, True), compiler_params=_params(),
    )(*shards)


def gather_and_sum_rows(vec, keep):
    rows = vec.shape[0]

    def body(v_ref, sum_ref, keep_ref, buf_ref, send_sems, recv_sems):
        x, y, c = _me()
        me = 4 * x + 2 * y + c
        buf_ref[me] = v_ref[...]
        sends = []
        for k in range(NDEV - 1):
            dev, _ = _peer(k)
            cp = pltpu.make_async_remote_copy(v_ref, buf_ref.at[me], send_sems.at[k], recv_sems.at[k],
                                              device_id=dev, device_id_type=MESH)
            cp.start()
            sends.append(cp)
        for k in range(NDEV - 1):
            dev, pidx = _peer(k)
            pltpu.make_async_remote_copy(v_ref, buf_ref.at[pidx], send_sems.at[k], recv_sems.at[k],
                                         device_id=dev, device_id_type=MESH).wait_recv()
        for cp in sends:
            cp.wait_send()
        total = buf_ref[0]
        for r in range(1, NDEV):
            total = total + buf_ref[r]
        sum_ref[...] = total
        for r in range(NDEV):
            keep_ref[r] = buf_ref[r, 0:keep, :]

    return pl.pallas_call(
        body, name="gather_and_sum_rows",
        out_shape=(jax.ShapeDtypeStruct((rows, D), F32), jax.ShapeDtypeStruct((NDEV, keep, D), F32)),
        in_specs=[VSPEC], out_specs=(VSPEC, VSPEC),
        scratch_shapes=[pltpu.VMEM((NDEV, rows, D), F32), pltpu.SemaphoreType.DMA((NDEV - 1,)),
                        pltpu.SemaphoreType.DMA((NDEV - 1,))],
        compiler_params=_params(),
    )(vec)


def in_proj(x, gain, sc, sh, w_all, b_in, shards):
    t = x.shape[0]
    tm = min(512, t)
    n = len(shards)
    steps = (t // tm) * NDEV

    def body(*refs):
        x_ref, g_ref, sc_ref, sh_ref, w_ref, b_ref = refs[:6]
        p_ref, h_ref = refs[6 + n:8 + n]
        h_sc = refs[8 + 2 * n]
        start, forward, finish = _gather2_ops(refs[6:6 + n], refs[8 + n:8 + 2 * n], *refs[9 + 2 * n:])
        step = pl.program_id(0) * NDEV + pl.program_id(1)
        pl.when(step == 0)(start)

        @pl.when(pl.program_id(1) == 0)
        def _():
            h = f_modulate(x_ref[...], g_ref[...], sc_ref[...], sh_ref[...]).astype(BF16)
            h_sc[...] = h
            h_ref[...] = h

        p_ref[...] = _dot(h_sc[...], w_ref[...], _NN) + b_ref[...]
        pl.when(step == (3 * steps) // 4)(forward)
        pl.when(step == steps - 1)(finish)

    vec = pl.BlockSpec((1, D), lambda m, n: (0, 0))
    res = pl.pallas_call(
        body, name="in_proj", grid=(t // tm, NDEV),
        out_shape=(jax.ShapeDtypeStruct((t, NDEV * D), F32), jax.ShapeDtypeStruct((t, D), BF16)) + _gathered(shards),
        in_specs=[pl.BlockSpec((tm, D), lambda m, n: (m, 0)), vec, vec, vec,
                  pl.BlockSpec((None, D, D), lambda m, n: (n, 0, 0)), pl.BlockSpec((1, D), lambda m, n: (0, n))]
        + [ANYSPEC] * n,
        out_specs=(pl.BlockSpec((tm, D), lambda m, n: (m, n)), pl.BlockSpec((tm, D), lambda m, n: (m, 0)))
        + (ANYSPEC,) * n,
        scratch_shapes=[pltpu.VMEM((tm, D), BF16)] + _comm_sems(n, True),
        compiler_params=_params("arbitrary", "arbitrary"),
    )(x, gain, sc, sh, w_all, b_in, *shards)
    return res[0], res[1], res[2:]


def hgrn_local(p, logits, shards):
    t = p.shape[0]
    nc = t // CH
    n = len(shards)

    def body(*refs):
        q_ref, f_ref, v_ref, l_ref = refs[:4]
        intra_ref, qe_ref, ut_ref, dec_ref = refs[4 + n:8 + n]
        start, forward, finish = _gather2_ops(refs[4:4 + n], refs[8 + n:8 + 2 * n], *refs[8 + 2 * n:])
        step = pl.program_id(0)
        pl.when(step == 0)(start)
        intra, qe, ut, dec = f_hgrn_chunk(q_ref[...], f_ref[...], v_ref[...], l_ref[...])
        intra_ref[...] = intra
        qe_ref[...] = qe.astype(BF16)
        ut_ref[...] = ut
        dec_ref[...] = dec
        pl.when(step == (3 * nc) // 4)(forward)
        pl.when(step == nc - 1)(finish)

    col = lambda j: pl.BlockSpec((CH, D), lambda c: (c, j))
    res = pl.pallas_call(
        body, name="hgrn_local", grid=(nc,),
        out_shape=(jax.ShapeDtypeStruct((t, D), F32), jax.ShapeDtypeStruct((t, D), BF16),
                   jax.ShapeDtypeStruct((nc, DK, D), F32), jax.ShapeDtypeStruct((nc, 1, D), F32)) + _gathered(shards),
        in_specs=[col(0), col(1), col(2), pl.BlockSpec((2, D), lambda c: (0, 0))] + [ANYSPEC] * n,
        out_specs=(pl.BlockSpec((CH, D), lambda c: (c, 0)), pl.BlockSpec((CH, D), lambda c: (c, 0)),
                   pl.BlockSpec((None, DK, D), lambda c: (c, 0, 0)), pl.BlockSpec((None, 1, D), lambda c: (c, 0, 0)))
        + (ANYSPEC,) * n,
        scratch_shapes=_comm_sems(n, True),
        compiler_params=_params("arbitrary"),
    )(p, p, p, logits, *shards)
    return res[:4] + (res[4:],)


def state_scan(u, dec, reverse):
    nc = u.shape[0]

    def body(u_ref, d_ref, out_ref, st):
        @pl.when(pl.program_id(0) == 0)
        def _():
            st[...] = jnp.zeros_like(st)

        s = st[...]
        out_ref[...] = s
        st[...] = d_ref[...] * s + u_ref[...]

    idx = (lambda i: (nc - 1 - i, 0, 0)) if reverse else (lambda i: (i, 0, 0))
    return pl.pallas_call(
        body, name="state_scan_rev" if reverse else "state_scan", grid=(nc,),
        out_shape=jax.ShapeDtypeStruct((nc, DK, D), F32),
        in_specs=[pl.BlockSpec((None, DK, D), idx), pl.BlockSpec((None, 1, D), idx)],
        out_specs=pl.BlockSpec((None, DK, D), idx),
        scratch_shapes=[pltpu.VMEM((DK, D), F32)],
        compiler_params=_params("arbitrary"),
    )(u, dec)


def _conv_tile(t):
    return min(256, t)


def conv_forward(p, dw, db, ln_g, ln_b, shards):
    t = p.shape[0]
    tm = _conv_tile(t)
    per = tm // HALO
    n = len(shards)
    nt = t // tm

    def body(*refs):
        cv_ref, cg_ref, cvp_ref, cgp_ref, dw_ref, db_ref, g_ref, b_ref = refs[:8]
        yc_ref, act_ref = refs[8 + n:10 + n]
        uext = refs[10 + 2 * n]
        start, finish = _gather1_ops(refs[8:8 + n], refs[10 + n:10 + 2 * n], *refs[11 + 2 * n:])
        step = pl.program_id(0)
        pl.when(step == 0)(start)
        uext[0:HALO, :] = jnp.where(step == 0, 0.0, f_glu(cvp_ref[...], cgp_ref[...]))
        uext[HALO:HALO + tm, :] = f_glu(cv_ref[...], cg_ref[...])
        acc = jnp.zeros((tm, D), F32) + db_ref[...]
        for w in range(KW):
            acc = acc + dw_ref[w:w + 1, :] * uext[pl.ds(HALO - KW + 1 + w, tm), :]
        yc_ref[...] = acc
        act_ref[...] = f_conv_act(acc, g_ref[...], b_ref[...]).astype(BF16)
        pl.when(step == nt - 1)(finish)

    vec = pl.BlockSpec((1, D), lambda m: (0, 0))
    prev = lambda j: pl.BlockSpec((HALO, D), lambda m: (jnp.maximum(m * per - 1, 0), j))
    res = pl.pallas_call(
        body, name="conv_forward", grid=(nt,),
        out_shape=(jax.ShapeDtypeStruct((t, D), F32), jax.ShapeDtypeStruct((t, D), BF16)) + _gathered(shards),
        in_specs=[pl.BlockSpec((tm, D), lambda m: (m, 4)), pl.BlockSpec((tm, D), lambda m: (m, 5)), prev(4), prev(5),
                  pl.BlockSpec((HALO, D), lambda m: (0, 0)), vec, vec, vec] + [ANYSPEC] * n,
        out_specs=(pl.BlockSpec((tm, D), lambda m: (m, 0)), pl.BlockSpec((tm, D), lambda m: (m, 0))) + (ANYSPEC,) * n,
        scratch_shapes=[pltpu.VMEM((HALO + tm, D), F32)] + _comm_sems(n, True),
        compiler_params=_params("arbitrary"),
    )(p, p, p, p, dw, db, ln_g, ln_b, *shards)
    return res[0], res[1], res[2:]


def _mix_tile(t):
    return min(256, t)


def _mix_forward_tile(x_ref, og_ref, ga_ref, gb_ref, qe_ref, intra_ref, st_ref, act_ref, wa_ref, wb_ref, wo_ref,
                      hg_ref, rows):
    o = _inter(qe_ref[...], st_ref, rows) + intra_ref[...]
    a = f_head_out(o, og_ref[...], hg_ref[...])
    ya = _dot(a.astype(BF16), wa_ref[...], _NN)
    yb = _dot(act_ref[...], wb_ref[...], _NN)
    merged = f_merge(ga_ref[...], gb_ref[...], ya, yb)
    y = _dot(merged.astype(BF16), wo_ref[...], _NN)
    return o, a, ya, yb, merged, y


def _mix_specs(tm):
    col = lambda j: pl.BlockSpec((tm, D), lambda m: (m, j))
    row = pl.BlockSpec((tm, D), lambda m: (m, 0))
    return col, row, pl.BlockSpec((1, D), lambda m: (0, 0)), pl.BlockSpec((tm // CH, DK, D), lambda m: (m, 0, 0))


def mix_forward(x, p, qe, intra, st_prev, act, wa, wb, wo, hg, g1, post):
    t = x.shape[0]
    tm = _mix_tile(t)

    def body(x_ref, og_ref, ga_ref, gb_ref, qe_ref, intra_ref, st_ref, act_ref, wa_ref, wb_ref, wo_ref, hg_ref,
             g1_ref, post_ref, x1_ref):
        y = _mix_forward_tile(x_ref, og_ref, ga_ref, gb_ref, qe_ref, intra_ref, st_ref, act_ref, wa_ref, wb_ref,
                              wo_ref, hg_ref, tm)[-1]
        x1_ref[...] = f_residual(x_ref[...], y, g1_ref[...], post_ref[...])

    col, row, vec, stspec = _mix_specs(tm)
    return pl.pallas_call(
        body, name="mix_forward", grid=(t // tm,),
        out_shape=jax.ShapeDtypeStruct((t, D), F32),
        in_specs=[row, col(3), col(6), col(7), row, row, stspec, row, VSPEC, VSPEC, VSPEC, vec, vec, vec],
        out_specs=row,
        compiler_params=_params("parallel"),
    )(x, p, p, p, qe, intra, st_prev, act, wa, wb, wo, hg, g1, post)


def ffn_forward_backward(x1, target, w1, w2, pre, sc, sh, g2, post):
    t = x1.shape[0]
    tm = min(256, t)
    nb = w1.shape[0]
    fb = w1.shape[2]

    def body(x_ref, tg_ref, w1_ref, w2_ref, pre_ref, sc_ref, sh_ref, g2_ref, post_ref,
             dx_ref, h2_ref, r_ref, dz_ref, dy2_ref, acc_ref, loss_ref, z_sc):
        @pl.when(pl.program_id(0) == 0)
        def _():
            acc_ref[...] = jnp.zeros_like(acc_ref)
            loss_ref[...] = jnp.zeros_like(loss_ref)

        x1v = x_ref[...]
        h2, vjp_h = jax.vjp(f_modulate, x1v, pre_ref[...], sc_ref[...], sh_ref[...])
        h2b = h2.astype(BF16)
        h2_ref[...] = h2b
        y2 = jnp.zeros((tm, D), F32)
        for n in range(nb):
            z = _dot(h2b, w1_ref[n], _NN)
            z_sc[:, n * fb:(n + 1) * fb] = z
            r = jnp.square(jnp.maximum(z, 0.0)).astype(BF16)
            r_ref[:, n * fb:(n + 1) * fb] = r
            y2 = y2 + _dot(r, w2_ref[n], _NN)
        out, vjp_r = jax.vjp(f_residual, x1v, y2, g2_ref[...], post_ref[...])
        err = out - tg_ref[...]
        tok = jnp.mean(jnp.square(err), axis=-1, keepdims=True)
        loss_ref[...] += 0.5 * jnp.sum(tok, axis=0, keepdims=True)
        dx_a, dy2, dg2, dpost = vjp_r(err * (1.0 / D))
        dy2b = dy2.astype(BF16)
        dy2_ref[...] = dy2b
        dh2 = jnp.zeros((tm, D), F32)
        for n in range(nb):
            dr = _dot(dy2b, w2_ref[n], _NT)
            dz = (dr * (2.0 * jnp.maximum(z_sc[:, n * fb:(n + 1) * fb], 0.0))).astype(BF16)
            dz_ref[:, n * fb:(n + 1) * fb] = dz
            dh2 = dh2 + _dot(dz, w1_ref[n], _NT)
        dx_b, dpre, dsc, dsh = vjp_h(dh2)
        dx_ref[...] = dx_a + dx_b
        acc_ref[0:1, :] += dpre
        acc_ref[1:2, :] += dpost
        acc_ref[2:3, :] += dsc
        acc_ref[3:4, :] += dsh
        acc_ref[4:5, :] += dg2

    row = pl.BlockSpec((tm, D), lambda m: (m, 0))
    wide = pl.BlockSpec((tm, DFF), lambda m: (m, 0))
    vec = pl.BlockSpec((1, D), lambda m: (0, 0))
    return pl.pallas_call(
        body, name="ffn_forward_backward", grid=(t // tm,),
        out_shape=(jax.ShapeDtypeStruct((t, D), F32), jax.ShapeDtypeStruct((t, D), BF16),
                   jax.ShapeDtypeStruct((t, DFF), BF16), jax.ShapeDtypeStruct((t, DFF), BF16),
                   jax.ShapeDtypeStruct((t, D), BF16), jax.ShapeDtypeStruct((8, D), F32),
                   jax.ShapeDtypeStruct((8, 128), F32)),
        in_specs=[row, row, VSPEC, VSPEC, vec, vec, vec, vec, vec],
        out_specs=(row, row, wide, wide, row, pl.BlockSpec((8, D), lambda m: (0, 0)),
                   pl.BlockSpec((8, 128), lambda m: (0, 0))),
        scratch_shapes=[pltpu.VMEM((tm, DFF), F32)],
        compiler_params=_params("arbitrary"),
    )(x1, target, w1, w2, pre, sc, sh, g2, post)


def mix_backward(x, p, qe, intra, st_prev, act, wa, wb, wo, hg, g1, post, dx1):
    t = x.shape[0]
    tm = _mix_tile(t)
    nc = t // CH

    def body(x_ref, og_ref, ga_ref, gb_ref, qe_ref, intra_ref, st_ref, act_ref, wa_ref, wb_ref, wo_ref, hg_ref,
             g1_ref, post_ref, dx1_ref,
             dog_ref, dga_ref, dgb_ref, do_ref, dqe_ref, vt_ref, dact_ref, a_ref, mg_ref, dya_ref, dyb_ref, dy_ref,
             acc_ref):
        @pl.when(pl.program_id(0) == 0)
        def _():
            acc_ref[...] = jnp.zeros_like(acc_ref)

        o, a, ya, yb, merged, y = _mix_forward_tile(x_ref, og_ref, ga_ref, gb_ref, qe_ref, intra_ref, st_ref, act_ref,
                                                    wa_ref, wb_ref, wo_ref, hg_ref, tm)
        a_ref[...] = a.astype(BF16)
        mg_ref[...] = merged.astype(BF16)
        _, vjp_r = jax.vjp(f_residual, x_ref[...], y, g1_ref[...], post_ref[...])
        _, dy, dg1, dpost = vjp_r(dx1_ref[...])
        dyb16 = dy.astype(BF16)
        dy_ref[...] = dyb16
        dmerged = _dot(dyb16, wo_ref[...], _NT)
        _, vjp_m = jax.vjp(f_merge, ga_ref[...], gb_ref[...], ya, yb)
        dga, dgb, dya, dyb = vjp_m(dmerged)
        dga_ref[...] = dga.astype(BF16)
        dgb_ref[...] = dgb.astype(BF16)
        dya16, dyb16b = dya.astype(BF16), dyb.astype(BF16)
        dya_ref[...] = dya16
        dyb_ref[...] = dyb16b
        da = _dot(dya16, wa_ref[...], _NT)
        dact_ref[...] = _dot(dyb16b, wb_ref[...], _NT)
        _, vjp_a = jax.vjp(f_head_out, o, og_ref[...], hg_ref[...])
        do, dog, dhg = vjp_a(da)
        dog_ref[...] = dog.astype(BF16)
        do_ref[...] = do
        do16 = do.astype(BF16)
        qe16 = qe_ref[...]
        for ci in range(tm // CH):
            st = st_ref[ci].astype(BF16)
            rows = slice(ci * CH, (ci + 1) * CH)
            dqe, vt = [], []
            for h in range(HEADS):
                sl = slice(h * DK, (h + 1) * DK)
                dqe.append(_dot(do16[rows, sl], st[:, sl], _NN))
                vt.append(_dot(do16[rows, sl], qe16[rows, sl], _TN))
            dqe_ref[rows, :] = jnp.concatenate(dqe, axis=1)
            vt_ref[ci] = jnp.concatenate(vt, axis=1)
        acc_ref[0:1, :] += dg1
        acc_ref[1:2, :] += dpost
        acc_ref[2:3, :] += dhg
        acc_ref[3:4, :] += jnp.sum(dog, axis=0, keepdims=True)
        acc_ref[4:5, :] += jnp.sum(dga, axis=0, keepdims=True)
        acc_ref[5:6, :] += jnp.sum(dgb, axis=0, keepdims=True)

    col, row, vec, stspec = _mix_specs(tm)
    b16 = jax.ShapeDtypeStruct((t, D), BF16)
    f32 = jax.ShapeDtypeStruct((t, D), F32)
    return pl.pallas_call(
        body, name="mix_backward", grid=(t // tm,),
        out_shape=(b16, b16, b16, f32, f32, jax.ShapeDtypeStruct((nc, DK, D), F32), f32, b16, b16, b16, b16, b16,
                   jax.ShapeDtypeStruct((8, D), F32)),
        in_specs=[row, col(3), col(6), col(7), row, row, stspec, row, VSPEC, VSPEC, VSPEC, vec, vec, vec, row],
        out_specs=(row, row, row, row, row, stspec, row, row, row, row, row, row,
                   pl.BlockSpec((8, D), lambda m: (0, 0))),
        compiler_params=_params("arbitrary"),
    )(x, p, p, p, qe, intra, st_prev, act, wa, wb, wo, hg, g1, post, dx1)


def conv_backward(p, yc, dact, dw, ln_g, ln_b, blocks):
    t = p.shape[0]
    tm = _conv_tile(t)
    per = tm // HALO
    nt = t // tm
    n = len(blocks)

    def body(*refs):
        cv_ref, cg_ref, cvp_ref, cgp_ref, yc_ref, ycn_ref, da_ref, dan_ref, dw_ref, g_ref, b_ref = refs[:11]
        dcv_ref, dcg_ref, acc_ref, ddw_ref = refs[11 + n:15 + n]
        uext, dyext = refs[15 + 2 * n:17 + 2 * n]
        start, finish = _exchange_ops(refs[11:11 + n], refs[15 + n:15 + 2 * n], *refs[17 + 2 * n:])
        m = pl.program_id(0)

        @pl.when(m == 0)
        def _():
            start()
            acc_ref[...] = jnp.zeros_like(acc_ref)
            ddw_ref[...] = jnp.zeros_like(ddw_ref)

        cv, cg = cv_ref[...], cg_ref[...]
        u, vjp_u = jax.vjp(f_glu, cv, cg)
        uext[0:HALO, :] = jnp.where(m == 0, 0.0, f_glu(cvp_ref[...], cgp_ref[...]))
        uext[HALO:HALO + tm, :] = u
        _, vjp_c = jax.vjp(f_conv_act, yc_ref[...], g_ref[...], b_ref[...])
        dyc, dg, db = vjp_c(da_ref[...])
        _, vjp_n = jax.vjp(f_conv_act, ycn_ref[...], g_ref[...], b_ref[...])
        dyn = vjp_n(dan_ref[...])[0]
        dyext[0:tm, :] = dyc
        dyext[tm:tm + HALO, :] = jnp.where(m == nt - 1, 0.0, dyn)
        du = jnp.zeros((tm, D), F32)
        for w in range(KW):
            du = du + dw_ref[w:w + 1, :] * dyext[pl.ds(KW - 1 - w, tm), :]
            ddw_ref[w:w + 1, :] += jnp.sum(dyc * uext[pl.ds(HALO - KW + 1 + w, tm), :], axis=0, keepdims=True)
        dcv, dcg = vjp_u(du)
        dcv_ref[...] = dcv.astype(BF16)
        dcg_ref[...] = dcg.astype(BF16)
        acc_ref[0:1, :] += jnp.sum(dyc, axis=0, keepdims=True)
        acc_ref[1:2, :] += dg
        acc_ref[2:3, :] += db
        acc_ref[3:4, :] += jnp.sum(dcv, axis=0, keepdims=True)
        acc_ref[4:5, :] += jnp.sum(dcg, axis=0, keepdims=True)
        pl.when(m == nt - 1)(finish)

    vec = pl.BlockSpec((1, D), lambda m: (0, 0))
    row = pl.BlockSpec((tm, D), lambda m: (m, 0))
    prev = lambda j: pl.BlockSpec((HALO, D), lambda m: (jnp.maximum(m * per - 1, 0), j))
    nxt = pl.BlockSpec((HALO, D), lambda m: (jnp.minimum((m + 1) * per, t // HALO - 1), 0))
    b16 = jax.ShapeDtypeStruct((t, D), BF16)
    res = pl.pallas_call(
        body, name="conv_backward", grid=(nt,),
        out_shape=(b16, b16, jax.ShapeDtypeStruct((8, D), F32), jax.ShapeDtypeStruct((HALO, D), F32))
        + _exchanged(blocks),
        in_specs=[pl.BlockSpec((tm, D), lambda m: (m, 4)), pl.BlockSpec((tm, D), lambda m: (m, 5)), prev(4), prev(5),
                  row, nxt, row, nxt, pl.BlockSpec((HALO, D), lambda m: (0, 0)), vec, vec] + [ANYSPEC] * n,
        out_specs=(row, row, pl.BlockSpec((8, D), lambda m: (0, 0)), pl.BlockSpec((HALO, D), lambda m: (0, 0)))
        + (ANYSPEC,) * n,
        scratch_shapes=[pltpu.VMEM((HALO + tm, D), F32), pltpu.VMEM((tm + HALO, D), F32)] + _comm_sems(n, False),
        compiler_params=_params("arbitrary"),
    )(p, p, p, p, yc, yc, dact, dact, dw, ln_g, ln_b, *blocks)
    return res[:4] + (res[4:],)


def hgrn_backward(p, logits, do, dqe, gt, st_prev, blocks):
    t = p.shape[0]
    nc = t // CH
    n = len(blocks)

    def body(*refs):
        q_ref, f_ref, v_ref, l_ref, do_ref, dqe_ref, gt_ref, st_ref = refs[:8]
        dq_ref, df_ref, dv_ref, acc_ref = refs[8 + n:12 + n]
        start, finish = _exchange_ops(refs[8:8 + n], refs[12 + n:12 + 2 * n], *refs[12 + 2 * n:])

        @pl.when(pl.program_id(0) == 0)
        def _():
            start()
            acc_ref[...] = jnp.zeros_like(acc_ref)

        gt_v = gt_ref[...]
        ddec = jnp.sum(gt_v * st_ref[...], axis=0, keepdims=True)
        _, vjp = jax.vjp(f_hgrn_chunk, q_ref[...], f_ref[...], v_ref[...], l_ref[...])
        dq, df, dv, dl = vjp((do_ref[...], dqe_ref[...], gt_v, ddec))
        dq_ref[...] = dq.astype(BF16)
        df_ref[...] = df.astype(BF16)
        dv_ref[...] = dv.astype(BF16)
        acc_ref[0:2, :] += dl
        acc_ref[2:3, :] += jnp.sum(dq, axis=0, keepdims=True)
        acc_ref[3:4, :] += jnp.sum(df, axis=0, keepdims=True)
        acc_ref[4:5, :] += jnp.sum(dv, axis=0, keepdims=True)
        pl.when(pl.program_id(0) == nc - 1)(finish)

    col = lambda j: pl.BlockSpec((CH, D), lambda c: (c, j))
    row = pl.BlockSpec((CH, D), lambda c: (c, 0))
    stspec = pl.BlockSpec((None, DK, D), lambda c: (c, 0, 0))
    b16 = jax.ShapeDtypeStruct((t, D), BF16)
    res = pl.pallas_call(
        body, name="hgrn_backward", grid=(nc,),
        out_shape=(b16, b16, b16, jax.ShapeDtypeStruct((8, D), F32)) + _exchanged(blocks),
        in_specs=[col(0), col(1), col(2), pl.BlockSpec((2, D), lambda c: (0, 0)), row, row, stspec, stspec]
        + [ANYSPEC] * n,
        out_specs=(row, row, row, pl.BlockSpec((8, D), lambda c: (0, 0))) + (ANYSPEC,) * n,
        scratch_shapes=_comm_sems(n, False),
        compiler_params=_params("arbitrary"),
    )(p, p, p, logits, do, dqe, gt, st_prev, *blocks)
    return res[:4] + (res[4:],)


def in_proj_backward(dps, w_all, x, dx1, gain, sc, sh, blocks):
    t = x.shape[0]
    tm = min(256, t)
    nt = t // tm
    n = len(blocks)

    def body(*refs):
        dp_refs = refs[:NDEV]
        w_ref, x_ref, dx1_ref, g_ref, sc_ref, sh_ref = refs[NDEV:NDEV + 6]
        gx_ref, acc_ref = refs[NDEV + 6 + n:NDEV + 8 + n]
        start, finish = _exchange_ops(refs[NDEV + 6:NDEV + 6 + n], refs[NDEV + 8 + n:NDEV + 8 + 2 * n],
                                      *refs[NDEV + 8 + 2 * n:])

        @pl.when(pl.program_id(0) == 0)
        def _():
            start()
            acc_ref[...] = jnp.zeros_like(acc_ref)

        dh = jnp.zeros((tm, D), F32)
        for j in range(NDEV):
            dh = dh + _dot(dp_refs[j][...], w_ref[j], _NT)
        _, vjp_h = jax.vjp(f_modulate, x_ref[...], g_ref[...], sc_ref[...], sh_ref[...])
        dx, dg, dsc, dsh = vjp_h(dh)
        gx_ref[...] = dx1_ref[...] + dx
        acc_ref[0:1, :] += dg
        acc_ref[1:2, :] += dsc
        acc_ref[2:3, :] += dsh
        pl.when(pl.program_id(0) == nt - 1)(finish)

    row = pl.BlockSpec((tm, D), lambda m: (m, 0))
    vec = pl.BlockSpec((1, D), lambda m: (0, 0))
    res = pl.pallas_call(
        body, name="in_proj_backward", grid=(nt,),
        out_shape=(jax.ShapeDtypeStruct((t, D), F32), jax.ShapeDtypeStruct((8, D), F32)) + _exchanged(blocks),
        in_specs=[row] * NDEV + [VSPEC, row, row, vec, vec, vec] + [ANYSPEC] * n,
        out_specs=(row, pl.BlockSpec((8, D), lambda m: (0, 0))) + (ANYSPEC,) * n,
        scratch_shapes=_comm_sems(n, False),
        compiler_params=_params("arbitrary"),
    )(*dps, w_all, x, dx1, gain, sc, sh, *blocks)
    return res[0], res[1], res[2:]


def weight_grad(a, b, nblk, ka, bn, a_blocked, name):
    t = a.shape[0]
    tk = min(512, t)
    nk = t // tk

    def body(a_ref, b_ref, f_ref, h_ref, acc):
        k = pl.program_id(1)

        @pl.when(k == 0)
        def _():
            acc[...] = jnp.zeros_like(acc)

        acc[...] += _dot(a_ref[...], b_ref[...], _TN)

        @pl.when(k == nk - 1)
        def _():
            f_ref[...] = acc[...]
            h_ref[...] = acc[...].astype(BF16)

    a_idx = (lambda n, k: (k, n)) if a_blocked else (lambda n, k: (k, 0))
    b_idx = (lambda n, k: (k, 0)) if a_blocked else (lambda n, k: (k, n))
    out = pl.BlockSpec((None, ka, bn), lambda n, k: (n, 0, 0))
    return pl.pallas_call(
        body, name=name, grid=(nblk, nk),
        out_shape=(jax.ShapeDtypeStruct((nblk, ka, bn), F32), jax.ShapeDtypeStruct((nblk, ka, bn), BF16)),
        in_specs=[pl.BlockSpec((tk, ka), a_idx), pl.BlockSpec((tk, bn), b_idx)],
        out_specs=(out, out),
        scratch_shapes=[pltpu.VMEM((ka, bn), F32)],
        compiler_params=_params("parallel", "arbitrary"),
    )(a, b)


def ada_backward(call_t, dmod_cols, w, m, v):
    def body(c_ref, d_ref, w_ref, m_ref, v_ref, g_ref, dl_ref, nm_ref, nv_ref):
        ct, dm = c_ref[...], d_ref[...]
        g = ct[:, 0:1] * dm[0:1, :]
        for r in range(1, NDEV):
            g = g + ct[:, r:r + 1] * dm[r:r + 1, :]
        g_ref[...] = g
        dl_ref[...], nm_ref[...], nv_ref[...] = _adamw(w_ref[...], g, m_ref[...], v_ref[...])

    s = jax.ShapeDtypeStruct(w.shape, F32)
    return pl.pallas_call(
        body, name="ada_backward", out_shape=(s, s, s, s),
        in_specs=[VSPEC] * 5, out_specs=(VSPEC,) * 4, compiler_params=_params(),
    )(call_t, dmod_cols, w, m, v)


def adamw_rows(w, g, m, v, name):
    def body(w_ref, g_ref, m_ref, v_ref, dl_ref, nm_ref, nv_ref):
        dl_ref[...], nm_ref[...], nv_ref[...] = _adamw(w_ref[...], g_ref[...], m_ref[...], v_ref[...])

    s = jax.ShapeDtypeStruct(w.shape, F32)
    return pl.pallas_call(
        body, name=name, out_shape=(s, s, s), in_specs=[VSPEC] * 4, out_specs=(VSPEC,) * 3, compiler_params=_params(),
    )(w, g, m, v)


def reduce_and_adamw(w, g_own, g_recv, m, v, name):
    r, c = w.shape
    br = min(128, r)

    def body(w_ref, go_ref, gr_ref, m_ref, v_ref, g_ref, dl_ref, nm_ref, nv_ref):
        g = go_ref[...]
        for k in range(NDEV - 1):
            g = g + gr_ref[k].astype(F32)
        g_ref[...] = g
        dl_ref[...], nm_ref[...], nv_ref[...] = _adamw(w_ref[...], g, m_ref[...], v_ref[...])

    blk = pl.BlockSpec((br, c), lambda i: (i, 0))
    s = jax.ShapeDtypeStruct(w.shape, F32)
    return pl.pallas_call(
        body, name=name, grid=(r // br,), out_shape=(s, s, s, s),
        in_specs=[blk, blk, pl.BlockSpec((NDEV - 1, br, c), lambda i: (0, i, 0)), blk, blk],
        out_specs=(blk, blk, blk, blk), compiler_params=_params("parallel"),
    )(w, g_own, g_recv, m, v)


def kernel(x, c, w_ada, b_ada, pre_norm_tm, post_norm_tm, pre_norm_cm, post_norm_cm, w_in, b_in, hg_lb_logits, hg_norm, conv_dw, conv_db, conv_ln_g, conv_ln_b, w_br_a, w_br_b, w_out, w_ff1, w_ff2, loss_target, m_w_ada, m_b_ada, m_pre_norm_tm, m_post_norm_tm, m_pre_norm_cm, m_post_norm_cm, m_w_in, m_b_in, m_hg_lb_logits, m_hg_norm, m_conv_dw, m_conv_db, m_conv_ln_g, m_conv_ln_b, m_w_br_a, m_w_br_b, m_w_out, m_w_ff1, m_w_ff2, v_w_ada, v_b_ada, v_pre_norm_tm, v_post_norm_tm, v_pre_norm_cm, v_post_norm_cm, v_w_in, v_b_in, v_hg_lb_logits, v_hg_norm, v_conv_dw, v_conv_db, v_conv_ln_g, v_conv_ln_b, v_w_br_a, v_w_br_b, v_w_out, v_w_ff1, v_w_ff2):
    t = x.shape[1]
    me = 4 * lax.axis_index("x") + 2 * lax.axis_index("y") + lax.axis_index("c")
    xs = x[0]
    tgt = loss_target[0]

    mod, call = ada_forward(c, w_ada[0], b_ada)
    sh1, sc1, g1, sh2, sc2, g2 = [mod[:, i * D:(i + 1) * D] for i in range(6)]
    win_all, dw_all = all_gather_blocks([w_in[0].astype(BF16), conv_dw[0]])
    dw = jnp.pad(dw_all.transpose(1, 0, 2).reshape(KW, D), ((0, HALO - KW), (0, 0)))

    p, h, (w1_all,) = in_proj(xs, pre_norm_tm, sc1, sh1, win_all, b_in, [w_ff1[0].astype(BF16)])
    intra, qe, ut, dec, (w2_all,) = hgrn_local(p, hg_lb_logits, [w_ff2[0].astype(BF16)])
    st_prev = state_scan(ut, dec, False)
    yc, act, (wa_all, wb_all, wo_all) = conv_forward(
        p, dw, conv_db, conv_ln_g, conv_ln_b, [w_br_a[0].astype(BF16), w_br_b[0].astype(BF16), w_out[0].astype(BF16)])
    wa, wb, wo = wa_all.reshape(D, D), wb_all.reshape(D, D), wo_all.reshape(D, D)
    x1 = mix_forward(xs, p, qe, intra, st_prev, act, wa, wb, wo, hg_norm, g1, post_norm_tm)

    dx1, h2, r, dz, dy2, acc_ffn, loss_blk = ffn_forward_backward(x1, tgt, w1_all, w2_all, pre_norm_cm, sc2, sh2, g2,
                                                                   post_norm_cm)
    loss = lax.psum(loss_blk[0, 0], AXES)
    rows = D // NDEV
    g1_f, g1_h = weight_grad(h2, dz, NDEV, D, DFF // NDEV, False, "grad_w_ff1")
    g2_f, g2_h = weight_grad(r, dy2, 4, D, D, True, "grad_w_ff2")
    g2_f, g2_h = g2_f.reshape(NDEV, DFF // NDEV, D), g2_h.reshape(NDEV, DFF // NDEV, D)

    (dog, dga, dgb, do, dqe, vt, dact, a16, mg16, dya, dyb, dy, acc_mix) = mix_backward(
        xs, p, qe, intra, st_prev, act, wa, wb, wo, hg_norm, g1, post_norm_tm, dx1)
    gt = state_scan(vt, dec, True)
    dq, df, di, acc_hg, (r_ff1, r_ff2) = hgrn_backward(p, hg_lb_logits, do, dqe, gt, st_prev, [g1_h, g2_h])
    ga_f, ga_h = weight_grad(a16, dya, 1, D, D, False, "grad_w_br_a")
    gb_f, gb_h = weight_grad(act, dyb, 1, D, D, False, "grad_w_br_b")
    go_f, go_h = weight_grad(mg16, dy, 1, D, D, False, "grad_w_out")
    dcv, dcg, acc_conv, ddw, (r_a, r_b, r_o) = conv_backward(
        p, yc, dact, dw, conv_ln_g, conv_ln_b,
        [ga_h.reshape(NDEV, rows, D), gb_h.reshape(NDEV, rows, D), go_h.reshape(NDEV, rows, D)])
    dps = [dq, df, di, dog, dcv, dcg, dga, dgb]
    gw_in = [weight_grad(h, dps[n], 1, D, D, False, f"grad_w_in_{n}") for n in range(NDEV)]
    gin_f = jnp.concatenate([g[0] for g in gw_in], axis=0)
    gin_h = jnp.concatenate([g[1] for g in gw_in], axis=0)
    grad_x, acc_in, (r_in,) = in_proj_backward(dps, win_all, xs, dx1, pre_norm_tm, sc1, sh1, [gin_h])

    own = [gin_f, ga_f.reshape(NDEV, rows, D), gb_f.reshape(NDEV, rows, D), go_f.reshape(NDEV, rows, D), g1_f, g2_f]
    recv = [r_in, r_a, r_b, r_o, r_ff1, r_ff2]
    big = {}
    names = ["w_in", "w_br_a", "w_br_b", "w_out", "w_ff1", "w_ff2"]
    ws = [w_in, w_br_a, w_br_b, w_out, w_ff1, w_ff2]
    ms = [m_w_in, m_w_br_a, m_w_br_b, m_w_out, m_w_ff1, m_w_ff2]
    vs = [v_w_in, v_w_br_a, v_w_br_b, v_w_out, v_w_ff1, v_w_ff2]
    for i, nm in enumerate(names):
        g_own = lax.dynamic_index_in_dim(own[i], me, 0, keepdims=False)
        big[nm] = [o[None] for o in reduce_and_adamw(ws[i][0], g_own, recv[i], ms[i][0], vs[i][0], "adamw_" + nm)]

    dmod = jnp.concatenate([acc_in[2:3], acc_in[1:2], acc_mix[0:1], acc_ffn[3:4], acc_ffn[2:3], acc_ffn[4:5]], axis=0)
    db_in = jnp.concatenate([acc_hg[2:5], acc_mix[3:4], acc_conv[3:5], acc_mix[4:6]], axis=0)
    small = jnp.concatenate([
        dmod,
        acc_in[0:1], acc_mix[1:2], acc_ffn[0:1], acc_ffn[1:2],
        db_in,
        acc_hg[0:2],
        acc_mix[2:3],
        acc_conv[0:3],
        ddw,
    ], axis=0)
    total, kept = gather_and_sum_rows(small, 8)
    dmod_all = kept[:, 0:6, :].reshape(NDEV, 6 * D)
    wcols = w_ada.shape[2]
    gwa, dwa, nmwa, nvwa = ada_backward(call.T, lax.dynamic_slice_in_dim(dmod_all, me * wcols, wcols, axis=1),
                                        w_ada[0], m_w_ada[0], v_w_ada[0])
    ddw_mine = lax.dynamic_slice_in_dim(total[24:24 + KW], me * (D // NDEV), D // NDEV, axis=1)

    small_names = ["b_ada", "pre_norm_tm", "post_norm_tm", "pre_norm_cm", "post_norm_cm", "b_in", "hg_lb_logits",
                   "hg_norm", "conv_db", "conv_ln_g", "conv_ln_b", "conv_dw"]
    small_w = [b_ada, pre_norm_tm, post_norm_tm, pre_norm_cm, post_norm_cm, b_in, hg_lb_logits, hg_norm, conv_db,
               conv_ln_g, conv_ln_b, conv_dw]
    small_m = [m_b_ada, m_pre_norm_tm, m_post_norm_tm, m_pre_norm_cm, m_post_norm_cm, m_b_in, m_hg_lb_logits,
               m_hg_norm, m_conv_db, m_conv_ln_g, m_conv_ln_b, m_conv_dw]
    small_v = [v_b_ada, v_pre_norm_tm, v_post_norm_tm, v_pre_norm_cm, v_post_norm_cm, v_b_in, v_hg_lb_logits,
               v_hg_norm, v_conv_db, v_conv_ln_g, v_conv_ln_b, v_conv_dw]
    small_g = [total[0:6].reshape(1, 6 * D), total[6:7], total[7:8], total[8:9], total[9:10],
               total[10:18].reshape(1, 8 * D), total[18:20], total[20:21], total[21:22], total[22:23], total[23:24],
               ddw_mine[None]]
    sizes = [int(w.size) for w in small_w]
    padded = -(-sum(sizes) // (8 * D)) * (8 * D)

    def pack(arrs):
        flat = jnp.concatenate([a.reshape(-1) for a in arrs])
        return jnp.pad(flat, (0, padded - flat.shape[0])).reshape(padded // D, D)

    packed = adamw_rows(pack(small_w), pack(small_g), pack(small_m), pack(small_v), "adamw_small")
    sm = {}
    off = 0
    for nm, w, g, sz in zip(small_names, small_w, small_g, sizes):
        sm[nm] = [g.reshape(w.shape)] + [pk.reshape(-1)[off:off + sz].reshape(w.shape) for pk in packed]
        off += sz

    order = ["w_ada", "b_ada", "pre_norm_tm", "post_norm_tm", "pre_norm_cm", "post_norm_cm", "w_in", "b_in",
             "hg_lb_logits", "hg_norm", "conv_dw", "conv_db", "conv_ln_g", "conv_ln_b", "w_br_a", "w_br_b", "w_out",
             "w_ff1", "w_ff2"]
    res = dict(sm)
    res.update(big)
    res["w_ada"] = [gwa[None], dwa[None], nmwa[None], nvwa[None]]
    outs = [loss, grad_x[None]]
    for j in range(4):
        outs += [res[nm][j] for nm in order]
    return tuple(outs)
```

```python
import functools

import jax
import jax.numpy as jnp
from jax import lax
from jax.experimental import pallas as pl
from jax.experimental.pallas import tpu as pltpu

F32 = jnp.float32
BF16 = jnp.bfloat16
MESH = pl.DeviceIdType.MESH
AXES = ("x", "y", "c")

D = 1024
HEADS = 8
DK = 128
CH = 128
LEVELS = 7
KW = 31
HALO = 32
DFF = 4096
NDEV = 8
EPS = 1e-6
ADAM_LR, ADAM_B1, ADAM_B2, ADAM_EPS, ADAM_WD, ADAM_STEP = 0.001, 0.9, 0.999, 1e-08, 0.01, 10
VMEM_LIMIT = 58 * 1024 * 1024

_NN = (((1,), (0,)), ((), ()))
_NT = (((1,), (1,)), ((), ()))
_TN = (((0,), (0,)), ((), ()))

VSPEC = pl.BlockSpec(memory_space=pltpu.VMEM)
ANYSPEC = pl.BlockSpec(memory_space=pl.ANY)


def _params(*sem):
    return pltpu.CompilerParams(dimension_semantics=sem or None, vmem_limit_bytes=VMEM_LIMIT)


def _dot(a, b, dims):
    return lax.dot_general(a, b, dims, preferred_element_type=F32)


@jax.custom_vjp
def mm_nn(a, b):
    return _dot(a.astype(BF16), b.astype(BF16), _NN)


def _mm_nn_fwd(a, b):
    ab, bb = a.astype(BF16), b.astype(BF16)
    return _dot(ab, bb, _NN), (ab, bb)


def _mm_nn_bwd(res, ct):
    ab, bb = res
    cb = ct.astype(BF16)
    return _dot(cb, bb, _NT), _dot(ab, cb, _TN)


mm_nn.defvjp(_mm_nn_fwd, _mm_nn_bwd)


@jax.custom_vjp
def mm_nt(a, b):
    return _dot(a.astype(BF16), b.astype(BF16), _NT)


def _mm_nt_fwd(a, b):
    ab, bb = a.astype(BF16), b.astype(BF16)
    return _dot(ab, bb, _NT), (ab, bb)


def _mm_nt_bwd(res, ct):
    ab, bb = res
    cb = ct.astype(BF16)
    return _dot(cb, bb, _NN), _dot(cb, ab, _TN)


mm_nt.defvjp(_mm_nt_fwd, _mm_nt_bwd)


@jax.custom_vjp
def mm_tn(a, b):
    return _dot(a.astype(BF16), b.astype(BF16), _TN)


def _mm_tn_fwd(a, b):
    ab, bb = a.astype(BF16), b.astype(BF16)
    return _dot(ab, bb, _TN), (ab, bb)


def _mm_tn_bwd(res, ct):
    ab, bb = res
    cb = ct.astype(BF16)
    return _dot(bb, cb, _NT), _dot(ab, cb, _NN)


mm_tn.defvjp(_mm_tn_fwd, _mm_tn_bwd)


def _rms(x):
    return x * lax.rsqrt(jnp.mean(x * x, axis=-1, keepdims=True) + EPS)


def _silu(x):
    return x * jax.nn.sigmoid(x)


def f_modulate(x, gain, sc, sh):
    return _rms(x) * gain * (1.0 + sc) + sh


def f_residual(x, y, gate, gain):
    return x + gate * (_rms(y) * gain)


def f_merge(ga, gb, ya, yb):
    return jax.nn.sigmoid(ga) * ya + jax.nn.sigmoid(gb) * yb


def f_head_out(o, og, hg):
    heads = [_rms(o[:, h * DK:(h + 1) * DK]) for h in range(HEADS)]
    return jnp.concatenate(heads, axis=1) * hg * _silu(og)


def f_conv_act(u, g, b):
    mu = jnp.mean(u, axis=-1, keepdims=True)
    var = jnp.mean(jnp.square(u - mu), axis=-1, keepdims=True)
    return _silu((u - mu) * lax.rsqrt(var + EPS) * g + b)


def f_glu(cv, cg):
    return cv * jax.nn.sigmoid(cg)


def _split2(x):
    hi = x.astype(BF16)
    return hi, (x - hi.astype(F32)).astype(BF16)


def _tri(transposed):
    i = lax.broadcasted_iota(jnp.int32, (CH, CH), 1 if transposed else 0)
    t = lax.broadcasted_iota(jnp.int32, (CH, CH), 0 if transposed else 1)
    return jnp.where(t <= i, 1.0, 0.0).astype(BF16)


def _blocks3(x, rows):
    return x.reshape(CH // rows, rows, x.shape[-1])


def _mid_broadcast(b, lev):
    h = 1 << (LEVELS - 1 - lev)
    if h >= 4:
        x3 = _blocks3(b, 2 * h)
        return jnp.broadcast_to(x3[:, h - 1:h, :], x3.shape).reshape(b.shape)
    x3 = _blocks3(b, 8)
    sub = lax.broadcasted_iota(jnp.int32, x3.shape, 1)
    out = None
    for first in range(0, 8, 2 * h):
        piece = jnp.broadcast_to(x3[:, first + h - 1:first + h, :], x3.shape)
        out = piece if out is None else jnp.where(sub >= first, piece, out)
    return out.reshape(b.shape)


def _mid_scatter(d, lev):
    h = 1 << (LEVELS - 1 - lev)
    if h >= 4:
        x3 = _blocks3(d, 2 * h)
        row = lax.broadcasted_iota(jnp.int32, x3.shape, 1)
        total = jnp.sum(x3, axis=1, keepdims=True)
        return jnp.where(row == h - 1, total, 0.0).reshape(d.shape)
    x3 = _blocks3(d, 8)
    sub = lax.broadcasted_iota(jnp.int32, x3.shape, 1)
    out = jnp.zeros_like(x3)
    for first in range(0, 8, 2 * h):
        inside = (sub >= first) & (sub < first + 2 * h)
        total = jnp.sum(jnp.where(inside, x3, 0.0), axis=1, keepdims=True)
        out = jnp.where(sub == first + h - 1, total, out)
    return out.reshape(d.shape)


@jax.custom_vjp
def decay_sums(g):
    hi, lo = _split2(g)
    tri = _tri(False)
    b = _dot(tri, hi, _NN) + _dot(tri, lo, _NN)
    return (b,) + tuple(b - _mid_broadcast(b, lev) for lev in range(LEVELS))


def _decay_sums_fwd(g):
    return decay_sums(g), None


def _decay_sums_bwd(_, cts):
    db = cts[0]
    for lev in range(LEVELS):
        db = db + cts[1 + lev] - _mid_scatter(cts[1 + lev], lev)
    hi, lo = _split2(db)
    tri = _tri(True)
    return (_dot(tri, hi, _NN) + _dot(tri, lo, _NN),)


decay_sums.defvjp(_decay_sums_fwd, _decay_sums_bwd)


def _score_masks():
    i = lax.broadcasted_iota(jnp.int32, (CH, CH), 0)
    j = lax.broadcasted_iota(jnp.int32, (CH, CH), 1)
    masks = [i == j]
    for lev in range(LEVELS):
        sh = LEVELS - 1 - lev
        same = (i >> (sh + 1)) == (j >> (sh + 1))
        masks.append(same & (((i >> sh) & 1) == 1) & (((j >> sh) & 1) == 0))
    return [jnp.where(m, 1.0, 0.0) for m in masks]


def f_hgrn_chunk(q_r, f_r, v, logits):
    l0, l1 = logits[0:1, :], logits[1:2, :]
    mx = lax.stop_gradient(jnp.maximum(l0, l1))
    e0, e1 = jnp.exp(l0 - mx), jnp.exp(l1 - mx)
    lb = e0 / (e0 + e1)
    q = _silu(q_r)
    f = lb + (1.0 - lb) * jax.nn.sigmoid(f_r)
    k = 1.0 - f
    sums = decay_sums(jnp.log(f))
    b = sums[0]
    btot = b[CH - 1:CH, :]
    qe = q * jnp.exp(b)
    ke = k * jnp.exp(btot - b)
    dec = jnp.exp(btot)
    qs, ks = [q], [k]
    row = lax.broadcasted_iota(jnp.int32, b.shape, 0)
    for lev in range(LEVELS):
        upper = ((row >> (LEVELS - 1 - lev)) & 1) == 1
        e = sums[1 + lev]
        qs.append(q * jnp.exp(jnp.where(upper, e, 0.0)))
        ks.append(k * jnp.exp(jnp.where(upper, 0.0, -e)))
    masks = _score_masks()
    intra, ut = [], []
    for h in range(HEADS):
        sl = slice(h * DK, (h + 1) * DK)
        sc = None
        for lev in range(LEVELS + 1):
            s = mm_nt(qs[lev][:, sl], ks[lev][:, sl]) * masks[lev]
            sc = s if sc is None else sc + s
        intra.append(mm_nn(sc, v[:, sl]))
        ut.append(mm_tn(v[:, sl], ke[:, sl]))
    return jnp.concatenate(intra, axis=1), qe, jnp.concatenate(ut, axis=1), dec


def _inter(qe_b, st_ref, rows):
    out = []
    for ci in range(rows // CH):
        st = st_ref[ci].astype(BF16)
        heads = [_dot(qe_b[ci * CH:(ci + 1) * CH, h * DK:(h + 1) * DK], st[:, h * DK:(h + 1) * DK], _NT)
                 for h in range(HEADS)]
        out.append(jnp.concatenate(heads, axis=1))
    return jnp.concatenate(out, axis=0)


def _shift_stack(src, dst, rows):
    dst[0, 0:rows, :] = src[0:rows, :]
    for b in range(1, 8):
        dst[b, 0:rows - 8, :] = src[pl.ds(b, rows - 8), :]


def _shifted(stack, offset, rows):
    return stack[offset % 8, pl.ds(8 * (offset // 8), rows), :]


def _adamw(w, g, m, v):
    m = ADAM_B1 * m + (1.0 - ADAM_B1) * g
    v = ADAM_B2 * v + (1.0 - ADAM_B2) * jnp.square(g)
    m_hat = m / (1.0 - ADAM_B1 ** ADAM_STEP)
    v_hat = v / (1.0 - ADAM_B2 ** ADAM_STEP)
    delta = -ADAM_LR * (m_hat / (jnp.sqrt(v_hat) + ADAM_EPS) + ADAM_WD * w)
    return delta, m, v


def _me():
    return lax.axis_index("x"), lax.axis_index("y"), lax.axis_index("c")


def _peer(k):
    x, y, c = _me()
    mask = k + 1
    px = (1 - x) if (mask >> 2) & 1 else x
    py = (1 - y) if (mask >> 1) & 1 else y
    pc = (1 - c) if mask & 1 else c
    return (px, py, pc), 4 * px + 2 * py + pc


def ada_forward(c, w_ada, b_ada):
    wcols = w_ada.shape[1]

    def body(c_ref, w_ref, b_ref, mod_ref, call_ref, part_ref, modp_ref, send_sems, recv_sems):
        x, y, cc = _me()
        me = 4 * x + 2 * y + cc
        call_ref[me] = _silu(c_ref[...])
        sends = []
        for k in range(NDEV - 1):
            dev, _ = _peer(k)
            cp = pltpu.make_async_remote_copy(call_ref.at[me], call_ref.at[me], send_sems.at[k], recv_sems.at[k],
                                              device_id=dev, device_id_type=MESH)
            cp.start()
            sends.append(cp)
        for k in range(NDEV - 1):
            _, pidx = _peer(k)
            pltpu.make_async_remote_copy(call_ref.at[pidx], call_ref.at[pidx], send_sems.at[k], recv_sems.at[k],
                                         device_id=_peer(k)[0], device_id_type=MESH).wait_recv()
        call = jnp.concatenate([call_ref[r] for r in range(NDEV)], axis=0)
        part = _dot(call.astype(BF16), w_ref[...].astype(BF16), _NN)
        for r in range(NDEV):
            part_ref[r] = part[r:r + 1, :]
        modp_ref[me] = part_ref[me]
        for k in range(NDEV - 1):
            dev, pidx = _peer(k)
            cp = pltpu.make_async_remote_copy(part_ref.at[pidx], modp_ref.at[me], send_sems.at[NDEV - 1 + k],
                                              recv_sems.at[NDEV - 1 + k], device_id=dev, device_id_type=MESH)
            cp.start()
            sends.append(cp)
        for k in range(NDEV - 1):
            dev, pidx = _peer(k)
            pltpu.make_async_remote_copy(part_ref.at[pidx], modp_ref.at[pidx], send_sems.at[NDEV - 1 + k],
                                         recv_sems.at[NDEV - 1 + k], device_id=dev, device_id_type=MESH).wait_recv()
        for cp in sends:
            cp.wait_send()
        mod_ref[...] = jnp.concatenate([modp_ref[r] for r in range(NDEV)], axis=1) + b_ref[...]

    mod, call = pl.pallas_call(
        body, name="ada_forward",
        out_shape=(jax.ShapeDtypeStruct((1, NDEV * wcols), F32), jax.ShapeDtypeStruct((NDEV, 1, D), F32)),
        in_specs=[VSPEC, VSPEC, VSPEC], out_specs=(VSPEC, VSPEC),
        scratch_shapes=[pltpu.VMEM((NDEV, 1, wcols), F32), pltpu.VMEM((NDEV, 1, wcols), F32),
                        pltpu.SemaphoreType.DMA((2 * (NDEV - 1),)), pltpu.SemaphoreType.DMA((2 * (NDEV - 1),))],
        compiler_params=_params(),
    )(c, w_ada, b_ada)
    return mod, call.reshape(NDEV, D)


def _comm_sems(n, local):
    sems = [pltpu.SemaphoreType.DMA((7 * n,)), pltpu.SemaphoreType.DMA((7 * n,))]
    return sems + ([pltpu.SemaphoreType.DMA((n,))] if local else [])


def _gather2_ops(ins, outs, send_sems, recv_sems, local_sems):
    n = len(ins)
    x, y, c = _me()
    me, sibling = (x, y, c), (x, y, 1 - c)
    chips = [(1 - x, y), (x, 1 - y), (1 - x, 1 - y)]

    def slot(p):
        return 4 * p[0] + 2 * p[1] + p[2]

    def copy(a, k, block, to, src=None):
        return pltpu.make_async_remote_copy(
            src_ref=outs[a].at[slot(block)] if src is None else src, dst_ref=outs[a].at[slot(block)],
            send_sem=send_sems.at[a * 7 + k], recv_sem=recv_sems.at[a * 7 + k], device_id=to, device_id_type=MESH)

    def local(a):
        return pltpu.make_async_copy(ins[a], outs[a].at[slot(me)], local_sems.at[a])

    def first(a):
        return [copy(a, 0, me, sibling, src=ins[a])] + [copy(a, 1 + j, me, (*chip, c), src=ins[a])
                                                        for j, chip in enumerate(chips)]

    def passed(a):
        return [copy(a, 4 + j, (*chip, c), sibling) for j, chip in enumerate(chips)]

    def start():
        for a in range(n):
            local(a).start()
            for cp in first(a):
                cp.start()

    def forward():
        for j, chip in enumerate(chips):
            for a in range(n):
                copy(a, 1 + j, (*chip, c), me).wait_recv()
                passed(a)[j].start()

    def finish():
        for a in range(n):
            copy(a, 0, sibling, me).wait_recv()
            for j, chip in enumerate(chips):
                copy(a, 4 + j, (*chip, 1 - c), me).wait_recv()
        for a in range(n):
            for cp in first(a) + passed(a):
                cp.wait_send()
            local(a).wait()

    return start, forward, finish


def _gather1_ops(ins, outs, send_sems, recv_sems, local_sems):
    n = len(ins)
    x, y, c = _me()
    me = 4 * x + 2 * y + c

    def copy(a, k, block):
        return pltpu.make_async_remote_copy(ins[a], outs[a].at[block], send_sems.at[a * 7 + k],
                                            recv_sems.at[a * 7 + k], device_id=_peer(k)[0], device_id_type=MESH)

    def local(a):
        return pltpu.make_async_copy(ins[a], outs[a].at[me], local_sems.at[a])

    def start():
        for a in range(n):
            local(a).start()
            for k in range(NDEV - 1):
                copy(a, k, me).start()

    def finish():
        for a in range(n):
            for k in range(NDEV - 1):
                copy(a, k, _peer(k)[1]).wait_recv()
        for a in range(n):
            for k in range(NDEV - 1):
                copy(a, k, me).wait_send()
            local(a).wait()

    return start, finish


def _exchange_ops(ins, outs, send_sems, recv_sems):
    n = len(ins)

    def copy(a, k):
        dev, pidx = _peer(k)
        return pltpu.make_async_remote_copy(ins[a].at[pidx], outs[a].at[k], send_sems.at[a * 7 + k],
                                            recv_sems.at[a * 7 + k], device_id=dev, device_id_type=MESH)

    def start():
        for k in range(NDEV - 1):
            for a in range(n):
                copy(a, k).start()

    def finish():
        for k in range(NDEV - 1):
            for a in range(n):
                copy(a, k).wait_recv()
        for k in range(NDEV - 1):
            for a in range(n):
                copy(a, k).wait_send()

    return start, finish


def _gathered(shards):
    return tuple(jax.ShapeDtypeStruct((NDEV,) + s.shape, s.dtype) for s in shards)


def _exchanged(blocks):
    return tuple(jax.ShapeDtypeStruct((NDEV - 1,) + b.shape[1:], b.dtype) for b in blocks)


def all_gather_blocks(shards):
    n = len(shards)

    def body(*refs):
        start, forward, finish = _gather2_ops(refs[:n], refs[n:2 * n], *refs[2 * n:])
        start()
        forward()
        finish()

    return pl.pallas_call(
        body, name="all_gather_blocks", out_shape=_gathered(shards),
        in_specs=[ANYSPEC] * n, out_specs=tuple([ANYSPEC] * n),
        scratch_shapes=_comm_sems(n, True), compiler_params=_params(),
    )(*shards)


def gather_and_sum_rows(vec, keep):
    rows = vec.shape[0]

    def body(v_ref, sum_ref, keep_ref, buf_ref, send_sems, recv_sems):
        x, y, c = _me()
        me = 4 * x + 2 * y + c
        buf_ref[me] = v_ref[...]
        sends = []
        for k in range(NDEV - 1):
            dev, _ = _peer(k)
            cp = pltpu.make_async_remote_copy(v_ref, buf_ref.at[me], send_sems.at[k], recv_sems.at[k],
                                              device_id=dev, device_id_type=MESH)
            cp.start()
            sends.append(cp)
        for k in range(NDEV - 1):
            dev, pidx = _peer(k)
            pltpu.make_async_remote_copy(v_ref, buf_ref.at[pidx], send_sems.at[k], recv_sems.at[k],
                                         device_id=dev, device_id_type=MESH).wait_recv()
        for cp in sends:
            cp.wait_send()
        total = buf_ref[0]
        for r in range(1, NDEV):
            total = total + buf_ref[r]
        sum_ref[...] = total
        for r in range(NDEV):
            keep_ref[r] = buf_ref[r, 0:keep, :]

    return pl.pallas_call(
        body, name="gather_and_sum_rows",
        out_shape=(jax.ShapeDtypeStruct((rows, D), F32), jax.ShapeDtypeStruct((NDEV, keep, D), F32)),
        in_specs=[VSPEC], out_specs=(VSPEC, VSPEC),
        scratch_shapes=[pltpu.VMEM((NDEV, rows, D), F32), pltpu.SemaphoreType.DMA((NDEV - 1,)),
                        pltpu.SemaphoreType.DMA((NDEV - 1,))],
        compiler_params=_params(),
    )(vec)


def in_proj(x, gain, sc, sh, w_all, b_in, shards):
    t = x.shape[0]
    tm = min(512, t)
    n = len(shards)
    steps = (t // tm) * NDEV

    def body(*refs):
        x_ref, g_ref, sc_ref, sh_ref, w_ref, b_ref = refs[:6]
        p_ref, h_ref = refs[6 + n:8 + n]
        h_sc = refs[8 + 2 * n]
        start, forward, finish = _gather2_ops(refs[6:6 + n], refs[8 + n:8 + 2 * n], *refs[9 + 2 * n:])
        step = pl.program_id(0) * NDEV + pl.program_id(1)
        pl.when(step == 0)(start)

        @pl.when(pl.program_id(1) == 0)
        def _():
            h = f_modulate(x_ref[...], g_ref[...], sc_ref[...], sh_ref[...]).astype(BF16)
            h_sc[...] = h
            h_ref[...] = h

        p_ref[...] = _dot(h_sc[...], w_ref[...], _NN) + b_ref[...]
        pl.when(step == (3 * steps) // 4)(forward)
        pl.when(step == steps - 1)(finish)

    vec = pl.BlockSpec((1, D), lambda m, n: (0, 0))
    res = pl.pallas_call(
        body, name="in_proj", grid=(t // tm, NDEV),
        out_shape=(jax.ShapeDtypeStruct((t, NDEV * D), F32), jax.ShapeDtypeStruct((t, D), BF16)) + _gathered(shards),
        in_specs=[pl.BlockSpec((tm, D), lambda m, n: (m, 0)), vec, vec, vec,
                  pl.BlockSpec((None, D, D), lambda m, n: (n, 0, 0)), pl.BlockSpec((1, D), lambda m, n: (0, n))]
        + [ANYSPEC] * n,
        out_specs=(pl.BlockSpec((tm, D), lambda m, n: (m, n)), pl.BlockSpec((tm, D), lambda m, n: (m, 0)))
        + (ANYSPEC,) * n,
        scratch_shapes=[pltpu.VMEM((tm, D), BF16)] + _comm_sems(n, True),
        compiler_params=_params("arbitrary", "arbitrary"),
    )(x, gain, sc, sh, w_all, b_in, *shards)
    return res[0], res[1], res[2:]


def hgrn_local(p, logits, shards):
    t = p.shape[0]
    nc = t // CH
    n = len(shards)

    def body(*refs):
        q_ref, f_ref, v_ref, l_ref = refs[:4]
        intra_ref, qe_ref, ut_ref, dec_ref = refs[4 + n:8 + n]
        start, forward, finish = _gather2_ops(refs[4:4 + n], refs[8 + n:8 + 2 * n], *refs[8 + 2 * n:])
        step = pl.program_id(0)
        pl.when(step == 0)(start)
        intra, qe, ut, dec = f_hgrn_chunk(q_ref[...], f_ref[...], v_ref[...], l_ref[...])
        intra_ref[...] = intra
        qe_ref[...] = qe.astype(BF16)
        ut_ref[...] = ut
        dec_ref[...] = dec
        pl.when(step == (3 * nc) // 4)(forward)
        pl.when(step == nc - 1)(finish)

    col = lambda j: pl.BlockSpec((CH, D), lambda c: (c, j))
    res = pl.pallas_call(
        body, name="hgrn_local", grid=(nc,),
        out_shape=(jax.ShapeDtypeStruct((t, D), F32), jax.ShapeDtypeStruct((t, D), BF16),
                   jax.ShapeDtypeStruct((nc, DK, D), F32), jax.ShapeDtypeStruct((nc, 1, D), F32)) + _gathered(shards),
        in_specs=[col(0), col(1), col(2), pl.BlockSpec((2, D), lambda c: (0, 0))] + [ANYSPEC] * n,
        out_specs=(pl.BlockSpec((CH, D), lambda c: (c, 0)), pl.BlockSpec((CH, D), lambda c: (c, 0)),
                   pl.BlockSpec((None, DK, D), lambda c: (c, 0, 0)), pl.BlockSpec((None, 1, D), lambda c: (c, 0, 0)))
        + (ANYSPEC,) * n,
        scratch_shapes=_comm_sems(n, True),
        compiler_params=_params("arbitrary"),
    )(p, p, p, logits, *shards)
    return res[:4] + (res[4:],)


def state_scan(u, dec, reverse):
    nc = u.shape[0]

    def body(u_ref, d_ref, out_ref, st):
        @pl.when(pl.program_id(0) == 0)
        def _():
            st[...] = jnp.zeros_like(st)

        s = st[...]
        out_ref[...] = s
        st[...] = d_ref[...] * s + u_ref[...]

    idx = (lambda i: (nc - 1 - i, 0, 0)) if reverse else (lambda i: (i, 0, 0))
    return pl.pallas_call(
        body, name="state_scan_rev" if reverse else "state_scan", grid=(nc,),
        out_shape=jax.ShapeDtypeStruct((nc, DK, D), F32),
        in_specs=[pl.BlockSpec((None, DK, D), idx), pl.BlockSpec((None, 1, D), idx)],
        out_specs=pl.BlockSpec((None, DK, D), idx),
        scratch_shapes=[pltpu.VMEM((DK, D), F32)],
        compiler_params=_params("arbitrary"),
    )(u, dec)


def _conv_tile(t):
    return min(256, t)


def conv_forward(p, dw, db, ln_g, ln_b, shards):
    t = p.shape[0]
    tm = _conv_tile(t)
    per = tm // HALO
    n = len(shards)
    nt = t // tm

    def body(*refs):
        cv_ref, cg_ref, cvp_ref, cgp_ref, dw_ref, db_ref, g_ref, b_ref = refs[:8]
        yc_ref, act_ref = refs[8 + n:10 + n]
        uext, ush = refs[10 + 2 * n:12 + 2 * n]
        start, finish = _gather1_ops(refs[8:8 + n], refs[10 + n:10 + 2 * n], *refs[12 + 2 * n:])
        step = pl.program_id(0)
        pl.when(step == 0)(start)
        uext[0:HALO, :] = jnp.where(step == 0, 0.0, f_glu(cvp_ref[...], cgp_ref[...]))
        uext[HALO:HALO + tm, :] = f_glu(cv_ref[...], cg_ref[...])
        _shift_stack(uext, ush, tm + HALO)
        acc = jnp.zeros((tm, D), F32) + db_ref[...]
        for w in range(KW):
            acc = acc + dw_ref[w:w + 1, :] * _shifted(ush, HALO - KW + 1 + w, tm)
        yc_ref[...] = acc
        act_ref[...] = f_conv_act(acc, g_ref[...], b_ref[...]).astype(BF16)
        pl.when(step == nt - 1)(finish)

    vec = pl.BlockSpec((1, D), lambda m: (0, 0))
    prev = lambda j: pl.BlockSpec((HALO, D), lambda m: (jnp.maximum(m * per - 1, 0), j))
    res = pl.pallas_call(
        body, name="conv_forward", grid=(nt,),
        out_shape=(jax.ShapeDtypeStruct((t, D), F32), jax.ShapeDtypeStruct((t, D), BF16)) + _gathered(shards),
        in_specs=[pl.BlockSpec((tm, D), lambda m: (m, 4)), pl.BlockSpec((tm, D), lambda m: (m, 5)), prev(4), prev(5),
                  pl.BlockSpec((HALO, D), lambda m: (0, 0)), vec, vec, vec] + [ANYSPEC] * n,
        out_specs=(pl.BlockSpec((tm, D), lambda m: (m, 0)), pl.BlockSpec((tm, D), lambda m: (m, 0))) + (ANYSPEC,) * n,
        scratch_shapes=[pltpu.VMEM((HALO + tm, D), F32), pltpu.VMEM((8, tm + HALO, D), F32)] + _comm_sems(n, True),
        compiler_params=_params("arbitrary"),
    )(p, p, p, p, dw, db, ln_g, ln_b, *shards)
    return res[0], res[1], res[2:]


def _mix_tile(t):
    return min(256, t)


def _mix_forward_tile(x_ref, og_ref, ga_ref, gb_ref, qe_ref, intra_ref, st_ref, act_ref, wa_ref, wb_ref, wo_ref,
                      hg_ref, rows):
    o = _inter(qe_ref[...], st_ref, rows) + intra_ref[...]
    a = f_head_out(o, og_ref[...], hg_ref[...])
    ya = _dot(a.astype(BF16), wa_ref[...], _NN)
    yb = _dot(act_ref[...], wb_ref[...], _NN)
    merged = f_merge(ga_ref[...], gb_ref[...], ya, yb)
    y = _dot(merged.astype(BF16), wo_ref[...], _NN)
    return o, a, ya, yb, merged, y


def _mix_specs(tm):
    col = lambda j: pl.BlockSpec((tm, D), lambda m: (m, j))
    row = pl.BlockSpec((tm, D), lambda m: (m, 0))
    return col, row, pl.BlockSpec((1, D), lambda m: (0, 0)), pl.BlockSpec((tm // CH, DK, D), lambda m: (m, 0, 0))


def mix_forward(x, p, qe, intra, st_prev, act, wa, wb, wo, hg, g1, post):
    t = x.shape[0]
    tm = _mix_tile(t)

    def body(x_ref, og_ref, ga_ref, gb_ref, qe_ref, intra_ref, st_ref, act_ref, wa_ref, wb_ref, wo_ref, hg_ref,
             g1_ref, post_ref, x1_ref):
        y = _mix_forward_tile(x_ref, og_ref, ga_ref, gb_ref, qe_ref, intra_ref, st_ref, act_ref, wa_ref, wb_ref,
                              wo_ref, hg_ref, tm)[-1]
        x1_ref[...] = f_residual(x_ref[...], y, g1_ref[...], post_ref[...])

    col, row, vec, stspec = _mix_specs(tm)
    return pl.pallas_call(
        body, name="mix_forward", grid=(t // tm,),
        out_shape=jax.ShapeDtypeStruct((t, D), F32),
        in_specs=[row, col(3), col(6), col(7), row, row, stspec, row, VSPEC, VSPEC, VSPEC, vec, vec, vec],
        out_specs=row,
        compiler_params=_params("parallel"),
    )(x, p, p, p, qe, intra, st_prev, act, wa, wb, wo, hg, g1, post)


def ffn_forward_backward(x1, target, w1, w2, pre, sc, sh, g2, post):
    t = x1.shape[0]
    tm = min(256, t)
    nb = w1.shape[0]
    fb = w1.shape[2]

    def body(x_ref, tg_ref, w1_ref, w2_ref, pre_ref, sc_ref, sh_ref, g2_ref, post_ref,
             dx_ref, h2_ref, r_ref, dz_ref, dy2_ref, acc_ref, loss_ref, z_sc):
        @pl.when(pl.program_id(0) == 0)
        def _():
            acc_ref[...] = jnp.zeros_like(acc_ref)
            loss_ref[...] = jnp.zeros_like(loss_ref)

        x1v = x_ref[...]
        h2, vjp_h = jax.vjp(f_modulate, x1v, pre_ref[...], sc_ref[...], sh_ref[...])
        h2b = h2.astype(BF16)
        h2_ref[...] = h2b
        y2 = jnp.zeros((tm, D), F32)
        for n in range(nb):
            z = _dot(h2b, w1_ref[n], _NN)
            z_sc[:, n * fb:(n + 1) * fb] = z
            r = jnp.square(jnp.maximum(z, 0.0)).astype(BF16)
            r_ref[:, n * fb:(n + 1) * fb] = r
            y2 = y2 + _dot(r, w2_ref[n], _NN)
        out, vjp_r = jax.vjp(f_residual, x1v, y2, g2_ref[...], post_ref[...])
        err = out - tg_ref[...]
        tok = jnp.mean(jnp.square(err), axis=-1, keepdims=True)
        loss_ref[...] += 0.5 * jnp.sum(tok, axis=0, keepdims=True)
        dx_a, dy2, dg2, dpost = vjp_r(err * (1.0 / D))
        dy2b = dy2.astype(BF16)
        dy2_ref[...] = dy2b
        dh2 = jnp.zeros((tm, D), F32)
        for n in range(nb):
            dr = _dot(dy2b, w2_ref[n], _NT)
            dz = (dr * (2.0 * jnp.maximum(z_sc[:, n * fb:(n + 1) * fb], 0.0))).astype(BF16)
            dz_ref[:, n * fb:(n + 1) * fb] = dz
            dh2 = dh2 + _dot(dz, w1_ref[n], _NT)
        dx_b, dpre, dsc, dsh = vjp_h(dh2)
        dx_ref[...] = dx_a + dx_b
        acc_ref[0:1, :] += dpre
        acc_ref[1:2, :] += dpost
        acc_ref[2:3, :] += dsc
        acc_ref[3:4, :] += dsh
        acc_ref[4:5, :] += dg2

    row = pl.BlockSpec((tm, D), lambda m: (m, 0))
    wide = pl.BlockSpec((tm, DFF), lambda m: (m, 0))
    vec = pl.BlockSpec((1, D), lambda m: (0, 0))
    return pl.pallas_call(
        body, name="ffn_forward_backward", grid=(t // tm,),
        out_shape=(jax.ShapeDtypeStruct((t, D), F32), jax.ShapeDtypeStruct((t, D), BF16),
                   jax.ShapeDtypeStruct((t, DFF), BF16), jax.ShapeDtypeStruct((t, DFF), BF16),
                   jax.ShapeDtypeStruct((t, D), BF16), jax.ShapeDtypeStruct((8, D), F32),
                   jax.ShapeDtypeStruct((8, 128), F32)),
        in_specs=[row, row, VSPEC, VSPEC, vec, vec, vec, vec, vec],
        out_specs=(row, row, wide, wide, row, pl.BlockSpec((8, D), lambda m: (0, 0)),
                   pl.BlockSpec((8, 128), lambda m: (0, 0))),
        scratch_shapes=[pltpu.VMEM((tm, DFF), F32)],
        compiler_params=_params("arbitrary"),
    )(x1, target, w1, w2, pre, sc, sh, g2, post)


def mix_backward(x, p, qe, intra, st_prev, act, wa, wb, wo, hg, g1, post, dx1):
    t = x.shape[0]
    tm = _mix_tile(t)
    nc = t // CH

    def body(x_ref, og_ref, ga_ref, gb_ref, qe_ref, intra_ref, st_ref, act_ref, wa_ref, wb_ref, wo_ref, hg_ref,
             g1_ref, post_ref, dx1_ref,
             dog_ref, dga_ref, dgb_ref, do_ref, dqe_ref, vt_ref, dact_ref, a_ref, mg_ref, dya_ref, dyb_ref, dy_ref,
             acc_ref):
        @pl.when(pl.program_id(0) == 0)
        def _():
            acc_ref[...] = jnp.zeros_like(acc_ref)

        o, a, ya, yb, merged, y = _mix_forward_tile(x_ref, og_ref, ga_ref, gb_ref, qe_ref, intra_ref, st_ref, act_ref,
                                                    wa_ref, wb_ref, wo_ref, hg_ref, tm)
        a_ref[...] = a.astype(BF16)
        mg_ref[...] = merged.astype(BF16)
        _, vjp_r = jax.vjp(f_residual, x_ref[...], y, g1_ref[...], post_ref[...])
        _, dy, dg1, dpost = vjp_r(dx1_ref[...])
        dyb16 = dy.astype(BF16)
        dy_ref[...] = dyb16
        dmerged = _dot(dyb16, wo_ref[...], _NT)
        _, vjp_m = jax.vjp(f_merge, ga_ref[...], gb_ref[...], ya, yb)
        dga, dgb, dya, dyb = vjp_m(dmerged)
        dga_ref[...] = dga.astype(BF16)
        dgb_ref[...] = dgb.astype(BF16)
        dya16, dyb16b = dya.astype(BF16), dyb.astype(BF16)
        dya_ref[...] = dya16
        dyb_ref[...] = dyb16b
        da = _dot(dya16, wa_ref[...], _NT)
        dact_ref[...] = _dot(dyb16b, wb_ref[...], _NT)
        _, vjp_a = jax.vjp(f_head_out, o, og_ref[...], hg_ref[...])
        do, dog, dhg = vjp_a(da)
        dog_ref[...] = dog.astype(BF16)
        do_ref[...] = do
        do16 = do.astype(BF16)
        qe16 = qe_ref[...]
        for ci in range(tm // CH):
            st = st_ref[ci].astype(BF16)
            rows = slice(ci * CH, (ci + 1) * CH)
            dqe, vt = [], []
            for h in range(HEADS):
                sl = slice(h * DK, (h + 1) * DK)
                dqe.append(_dot(do16[rows, sl], st[:, sl], _NN))
                vt.append(_dot(do16[rows, sl], qe16[rows, sl], _TN))
            dqe_ref[rows, :] = jnp.concatenate(dqe, axis=1)
            vt_ref[ci] = jnp.concatenate(vt, axis=1)
        acc_ref[0:1, :] += dg1
        acc_ref[1:2, :] += dpost
        acc_ref[2:3, :] += dhg
        acc_ref[3:4, :] += jnp.sum(dog, axis=0, keepdims=True)
        acc_ref[4:5, :] += jnp.sum(dga, axis=0, keepdims=True)
        acc_ref[5:6, :] += jnp.sum(dgb, axis=0, keepdims=True)

    col, row, vec, stspec = _mix_specs(tm)
    b16 = jax.ShapeDtypeStruct((t, D), BF16)
    f32 = jax.ShapeDtypeStruct((t, D), F32)
    return pl.pallas_call(
        body, name="mix_backward", grid=(t // tm,),
        out_shape=(b16, b16, b16, f32, f32, jax.ShapeDtypeStruct((nc, DK, D), F32), f32, b16, b16, b16, b16, b16,
                   jax.ShapeDtypeStruct((8, D), F32)),
        in_specs=[row, col(3), col(6), col(7), row, row, stspec, row, VSPEC, VSPEC, VSPEC, vec, vec, vec, row],
        out_specs=(row, row, row, row, row, stspec, row, row, row, row, row, row,
                   pl.BlockSpec((8, D), lambda m: (0, 0))),
        compiler_params=_params("arbitrary"),
    )(x, p, p, p, qe, intra, st_prev, act, wa, wb, wo, hg, g1, post, dx1)


def conv_backward(p, yc, dact, dw, ln_g, ln_b, blocks):
    t = p.shape[0]
    tm = _conv_tile(t)
    per = tm // HALO
    nt = t // tm
    n = len(blocks)

    def body(*refs):
        cv_ref, cg_ref, cvp_ref, cgp_ref, yc_ref, ycn_ref, da_ref, dan_ref, dw_ref, g_ref, b_ref = refs[:11]
        dcv_ref, dcg_ref, acc_ref, ddw_ref = refs[11 + n:15 + n]
        uext, dyext, ush, dysh, ddw8 = refs[15 + 2 * n:20 + 2 * n]
        start, finish = _exchange_ops(refs[11:11 + n], refs[15 + n:15 + 2 * n], *refs[20 + 2 * n:])
        m = pl.program_id(0)

        @pl.when(m == 0)
        def _():
            start()
            acc_ref[...] = jnp.zeros_like(acc_ref)
            ddw8[...] = jnp.zeros_like(ddw8)

        cv, cg = cv_ref[...], cg_ref[...]
        u, vjp_u = jax.vjp(f_glu, cv, cg)
        uext[0:HALO, :] = jnp.where(m == 0, 0.0, f_glu(cvp_ref[...], cgp_ref[...]))
        uext[HALO:HALO + tm, :] = u
        _shift_stack(uext, ush, tm + HALO)
        _, vjp_c = jax.vjp(f_conv_act, yc_ref[...], g_ref[...], b_ref[...])
        dyc, dg, db = vjp_c(da_ref[...])
        _, vjp_n = jax.vjp(f_conv_act, ycn_ref[...], g_ref[...], b_ref[...])
        dyn = vjp_n(dan_ref[...])[0]
        dyext[0:tm, :] = dyc
        dyext[tm:tm + HALO, :] = jnp.where(m == nt - 1, 0.0, dyn)
        _shift_stack(dyext, dysh, tm + HALO)
        du = jnp.zeros((tm, D), F32)
        for w in range(KW):
            du = du + dw_ref[w:w + 1, :] * _shifted(dysh, KW - 1 - w, tm)
            prod = dyc * _shifted(ush, HALO - KW + 1 + w, tm)
            ddw8[w] += jnp.sum(prod.reshape(tm // 8, 8, D), axis=0)

        @pl.when(m == nt - 1)
        def _():
            ddw_ref[...] = jnp.sum(ddw8[...], axis=1)

        dcv, dcg = vjp_u(du)
        dcv_ref[...] = dcv.astype(BF16)
        dcg_ref[...] = dcg.astype(BF16)
        acc_ref[0:1, :] += jnp.sum(dyc, axis=0, keepdims=True)
        acc_ref[1:2, :] += dg
        acc_ref[2:3, :] += db
        acc_ref[3:4, :] += jnp.sum(dcv, axis=0, keepdims=True)
        acc_ref[4:5, :] += jnp.sum(dcg, axis=0, keepdims=True)
        pl.when(m == nt - 1)(finish)

    vec = pl.BlockSpec((1, D), lambda m: (0, 0))
    row = pl.BlockSpec((tm, D), lambda m: (m, 0))
    prev = lambda j: pl.BlockSpec((HALO, D), lambda m: (jnp.maximum(m * per - 1, 0), j))
    nxt = pl.BlockSpec((HALO, D), lambda m: (jnp.minimum((m + 1) * per, t // HALO - 1), 0))
    b16 = jax.ShapeDtypeStruct((t, D), BF16)
    res = pl.pallas_call(
        body, name="conv_backward", grid=(nt,),
        out_shape=(b16, b16, jax.ShapeDtypeStruct((8, D), F32), jax.ShapeDtypeStruct((HALO, D), F32))
        + _exchanged(blocks),
        in_specs=[pl.BlockSpec((tm, D), lambda m: (m, 4)), pl.BlockSpec((tm, D), lambda m: (m, 5)), prev(4), prev(5),
                  row, nxt, row, nxt, pl.BlockSpec((HALO, D), lambda m: (0, 0)), vec, vec] + [ANYSPEC] * n,
        out_specs=(row, row, pl.BlockSpec((8, D), lambda m: (0, 0)), pl.BlockSpec((HALO, D), lambda m: (0, 0)))
        + (ANYSPEC,) * n,
        scratch_shapes=[pltpu.VMEM((HALO + tm, D), F32), pltpu.VMEM((tm + HALO, D), F32),
                        pltpu.VMEM((8, tm + HALO, D), F32), pltpu.VMEM((8, tm + HALO, D), F32),
                        pltpu.VMEM((HALO, 8, D), F32)] + _comm_sems(n, False),
        compiler_params=_params("arbitrary"),
    )(p, p, p, p, yc, yc, dact, dact, dw, ln_g, ln_b, *blocks)
    return res[:4] + (res[4:],)


def hgrn_backward(p, logits, do, dqe, gt, st_prev, blocks):
    t = p.shape[0]
    nc = t // CH
    n = len(blocks)

    def body(*refs):
        q_ref, f_ref, v_ref, l_ref, do_ref, dqe_ref, gt_ref, st_ref = refs[:8]
        dq_ref, df_ref, dv_ref, acc_ref = refs[8 + n:12 + n]
        start, finish = _exchange_ops(refs[8:8 + n], refs[12 + n:12 + 2 * n], *refs[12 + 2 * n:])

        @pl.when(pl.program_id(0) == 0)
        def _():
            start()
            acc_ref[...] = jnp.zeros_like(acc_ref)

        gt_v = gt_ref[...]
        ddec = jnp.sum(gt_v * st_ref[...], axis=0, keepdims=True)
        _, vjp = jax.vjp(f_hgrn_chunk, q_ref[...], f_ref[...], v_ref[...], l_ref[...])
        dq, df, dv, dl = vjp((do_ref[...], dqe_ref[...], gt_v, ddec))
        dq_ref[...] = dq.astype(BF16)
        df_ref[...] = df.astype(BF16)
        dv_ref[...] = dv.astype(BF16)
        acc_ref[0:2, :] += dl
        acc_ref[2:3, :] += jnp.sum(dq, axis=0, keepdims=True)
        acc_ref[3:4, :] += jnp.sum(df, axis=0, keepdims=True)
        acc_ref[4:5, :] += jnp.sum(dv, axis=0, keepdims=True)
        pl.when(pl.program_id(0) == nc - 1)(finish)

    col = lambda j: pl.BlockSpec((CH, D), lambda c: (c, j))
    row = pl.BlockSpec((CH, D), lambda c: (c, 0))
    stspec = pl.BlockSpec((None, DK, D), lambda c: (c, 0, 0))
    b16 = jax.ShapeDtypeStruct((t, D), BF16)
    res = pl.pallas_call(
        body, name="hgrn_backward", grid=(nc,),
        out_shape=(b16, b16, b16, jax.ShapeDtypeStruct((8, D), F32)) + _exchanged(blocks),
        in_specs=[col(0), col(1), col(2), pl.BlockSpec((2, D), lambda c: (0, 0)), row, row, stspec, stspec]
        + [ANYSPEC] * n,
        out_specs=(row, row, row, pl.BlockSpec((8, D), lambda c: (0, 0))) + (ANYSPEC,) * n,
        scratch_shapes=_comm_sems(n, False),
        compiler_params=_params("arbitrary"),
    )(p, p, p, logits, do, dqe, gt, st_prev, *blocks)
    return res[:4] + (res[4:],)


def in_proj_backward(dps, w_all, x, dx1, gain, sc, sh, blocks):
    t = x.shape[0]
    tm = min(256, t)
    nt = t // tm
    n = len(blocks)

    def body(*refs):
        dp_refs = refs[:NDEV]
        w_ref, x_ref, dx1_ref, g_ref, sc_ref, sh_ref = refs[NDEV:NDEV + 6]
        gx_ref, acc_ref = refs[NDEV + 6 + n:NDEV + 8 + n]
        start, finish = _exchange_ops(refs[NDEV + 6:NDEV + 6 + n], refs[NDEV + 8 + n:NDEV + 8 + 2 * n],
                                      *refs[NDEV + 8 + 2 * n:])

        @pl.when(pl.program_id(0) == 0)
        def _():
            start()
            acc_ref[...] = jnp.zeros_like(acc_ref)

        dh = jnp.zeros((tm, D), F32)
        for j in range(NDEV):
            dh = dh + _dot(dp_refs[j][...], w_ref[j], _NT)
        _, vjp_h = jax.vjp(f_modulate, x_ref[...], g_ref[...], sc_ref[...], sh_ref[...])
        dx, dg, dsc, dsh = vjp_h(dh)
        gx_ref[...] = dx1_ref[...] + dx
        acc_ref[0:1, :] += dg
        acc_ref[1:2, :] += dsc
        acc_ref[2:3, :] += dsh
        pl.when(pl.program_id(0) == nt - 1)(finish)

    row = pl.BlockSpec((tm, D), lambda m: (m, 0))
    vec = pl.BlockSpec((1, D), lambda m: (0, 0))
    res = pl.pallas_call(
        body, name="in_proj_backward", grid=(nt,),
        out_shape=(jax.ShapeDtypeStruct((t, D), F32), jax.ShapeDtypeStruct((8, D), F32)) + _exchanged(blocks),
        in_specs=[row] * NDEV + [VSPEC, row, row, vec, vec, vec] + [ANYSPEC] * n,
        out_specs=(row, pl.BlockSpec((8, D), lambda m: (0, 0))) + (ANYSPEC,) * n,
        scratch_shapes=_comm_sems(n, False),
        compiler_params=_params("arbitrary"),
    )(*dps, w_all, x, dx1, gain, sc, sh, *blocks)
    return res[0], res[1], res[2:]


def weight_grad(a, b, nblk, ka, bn, a_blocked, name):
    t = a.shape[0]
    tk = min(512, t)
    nk = t // tk

    def body(a_ref, b_ref, f_ref, h_ref, acc):
        k = pl.program_id(1)

        @pl.when(k == 0)
        def _():
            acc[...] = jnp.zeros_like(acc)

        acc[...] += _dot(a_ref[...], b_ref[...], _TN)

        @pl.when(k == nk - 1)
        def _():
            f_ref[...] = acc[...]
            h_ref[...] = acc[...].astype(BF16)

    a_idx = (lambda n, k: (k, n)) if a_blocked else (lambda n, k: (k, 0))
    b_idx = (lambda n, k: (k, 0)) if a_blocked else (lambda n, k: (k, n))
    out = pl.BlockSpec((None, ka, bn), lambda n, k: (n, 0, 0))
    return pl.pallas_call(
        body, name=name, grid=(nblk, nk),
        out_shape=(jax.ShapeDtypeStruct((nblk, ka, bn), F32), jax.ShapeDtypeStruct((nblk, ka, bn), BF16)),
        in_specs=[pl.BlockSpec((tk, ka), a_idx), pl.BlockSpec((tk, bn), b_idx)],
        out_specs=(out, out),
        scratch_shapes=[pltpu.VMEM((ka, bn), F32)],
        compiler_params=_params("parallel", "arbitrary"),
    )(a, b)


def ada_backward(call_t, dmod_cols, w, m, v):
    def body(c_ref, d_ref, w_ref, m_ref, v_ref, g_ref, dl_ref, nm_ref, nv_ref):
        ct, dm = c_ref[...], d_ref[...]
        g = ct[:, 0:1] * dm[0:1, :]
        for r in range(1, NDEV):
            g = g + ct[:, r:r + 1] * dm[r:r + 1, :]
        g_ref[...] = g
        dl_ref[...], nm_ref[...], nv_ref[...] = _adamw(w_ref[...], g, m_ref[...], v_ref[...])

    s = jax.ShapeDtypeStruct(w.shape, F32)
    return pl.pallas_call(
        body, name="ada_backward", out_shape=(s, s, s, s),
        in_specs=[VSPEC] * 5, out_specs=(VSPEC,) * 4, compiler_params=_params(),
    )(call_t, dmod_cols, w, m, v)


def adamw_rows(w, g, m, v, name):
    def body(w_ref, g_ref, m_ref, v_ref, dl_ref, nm_ref, nv_ref):
        dl_ref[...], nm_ref[...], nv_ref[...] = _adamw(w_ref[...], g_ref[...], m_ref[...], v_ref[...])

    s = jax.ShapeDtypeStruct(w.shape, F32)
    return pl.pallas_call(
        body, name=name, out_shape=(s, s, s), in_specs=[VSPEC] * 4, out_specs=(VSPEC,) * 3, compiler_params=_params(),
    )(w, g, m, v)


def reduce_and_adamw(w, g_own, g_recv, m, v, name):
    r, c = w.shape
    br = min(128, r)

    def body(w_ref, go_ref, gr_ref, m_ref, v_ref, g_ref, dl_ref, nm_ref, nv_ref):
        g = go_ref[...]
        for k in range(NDEV - 1):
            g = g + gr_ref[k].astype(F32)
        g_ref[...] = g
        dl_ref[...], nm_ref[...], nv_ref[...] = _adamw(w_ref[...], g, m_ref[...], v_ref[...])

    blk = pl.BlockSpec((br, c), lambda i: (i, 0))
    s = jax.ShapeDtypeStruct(w.shape, F32)
    return pl.pallas_call(
        body, name=name, grid=(r // br,), out_shape=(s, s, s, s),
        in_specs=[blk, blk, pl.BlockSpec((NDEV - 1, br, c), lambda i: (0, i, 0)), blk, blk],
        out_specs=(blk, blk, blk, blk), compiler_params=_params("parallel"),
    )(w, g_own, g_recv, m, v)


def kernel(x, c, w_ada, b_ada, pre_norm_tm, post_norm_tm, pre_norm_cm, post_norm_cm, w_in, b_in, hg_lb_logits, hg_norm, conv_dw, conv_db, conv_ln_g, conv_ln_b, w_br_a, w_br_b, w_out, w_ff1, w_ff2, loss_target, m_w_ada, m_b_ada, m_pre_norm_tm, m_post_norm_tm, m_pre_norm_cm, m_post_norm_cm, m_w_in, m_b_in, m_hg_lb_logits, m_hg_norm, m_conv_dw, m_conv_db, m_conv_ln_g, m_conv_ln_b, m_w_br_a, m_w_br_b, m_w_out, m_w_ff1, m_w_ff2, v_w_ada, v_b_ada, v_pre_norm_tm, v_post_norm_tm, v_pre_norm_cm, v_post_norm_cm, v_w_in, v_b_in, v_hg_lb_logits, v_hg_norm, v_conv_dw, v_conv_db, v_conv_ln_g, v_conv_ln_b, v_w_br_a, v_w_br_b, v_w_out, v_w_ff1, v_w_ff2):
    t = x.shape[1]
    me = 4 * lax.axis_index("x") + 2 * lax.axis_index("y") + lax.axis_index("c")
    xs = x[0]
    tgt = loss_target[0]

    mod, call = ada_forward(c, w_ada[0], b_ada)
    sh1, sc1, g1, sh2, sc2, g2 = [mod[:, i * D:(i + 1) * D] for i in range(6)]
    win_all, dw_all = all_gather_blocks([w_in[0].astype(BF16), conv_dw[0]])
    dw = jnp.pad(dw_all.transpose(1, 0, 2).reshape(KW, D), ((0, HALO - KW), (0, 0)))

    p, h, (w1_all,) = in_proj(xs, pre_norm_tm, sc1, sh1, win_all, b_in, [w_ff1[0].astype(BF16)])
    intra, qe, ut, dec, (w2_all,) = hgrn_local(p, hg_lb_logits, [w_ff2[0].astype(BF16)])
    st_prev = state_scan(ut, dec, False)
    yc, act, (wa_all, wb_all, wo_all) = conv_forward(
        p, dw, conv_db, conv_ln_g, conv_ln_b, [w_br_a[0].astype(BF16), w_br_b[0].astype(BF16), w_out[0].astype(BF16)])
    wa, wb, wo = wa_all.reshape(D, D), wb_all.reshape(D, D), wo_all.reshape(D, D)
    x1 = mix_forward(xs, p, qe, intra, st_prev, act, wa, wb, wo, hg_norm, g1, post_norm_tm)

    dx1, h2, r, dz, dy2, acc_ffn, loss_blk = ffn_forward_backward(x1, tgt, w1_all, w2_all, pre_norm_cm, sc2, sh2, g2,
                                                                   post_norm_cm)
    loss = lax.psum(loss_blk[0, 0], AXES)
    rows = D // NDEV
    g1_f, g1_h = weight_grad(h2, dz, NDEV, D, DFF // NDEV, False, "grad_w_ff1")
    g2_f, g2_h = weight_grad(r, dy2, 4, D, D, True, "grad_w_ff2")
    g2_f, g2_h = g2_f.reshape(NDEV, DFF // NDEV, D), g2_h.reshape(NDEV, DFF // NDEV, D)

    (dog, dga, dgb, do, dqe, vt, dact, a16, mg16, dya, dyb, dy, acc_mix) = mix_backward(
        xs, p, qe, intra, st_prev, act, wa, wb, wo, hg_norm, g1, post_norm_tm, dx1)
    gt = state_scan(vt, dec, True)
    dq, df, di, acc_hg, (r_ff1, r_ff2) = hgrn_backward(p, hg_lb_logits, do, dqe, gt, st_prev, [g1_h, g2_h])
    ga_f, ga_h = weight_grad(a16, dya, 1, D, D, False, "grad_w_br_a")
    gb_f, gb_h = weight_grad(act, dyb, 1, D, D, False, "grad_w_br_b")
    go_f, go_h = weight_grad(mg16, dy, 1, D, D, False, "grad_w_out")
    dcv, dcg, acc_conv, ddw, (r_a, r_b, r_o) = conv_backward(
        p, yc, dact, dw, conv_ln_g, conv_ln_b,
        [ga_h.reshape(NDEV, rows, D), gb_h.reshape(NDEV, rows, D), go_h.reshape(NDEV, rows, D)])
    dps = [dq, df, di, dog, dcv, dcg, dga, dgb]
    gw_in = [weight_grad(h, dps[n], 1, D, D, False, f"grad_w_in_{n}") for n in range(NDEV)]
    gin_f = jnp.concatenate([g[0] for g in gw_in], axis=0)
    gin_h = jnp.concatenate([g[1] for g in gw_in], axis=0)
    grad_x, acc_in, (r_in,) = in_proj_backward(dps, win_all, xs, dx1, pre_norm_tm, sc1, sh1, [gin_h])

    own = [gin_f, ga_f.reshape(NDEV, rows, D), gb_f.reshape(NDEV, rows, D), go_f.reshape(NDEV, rows, D), g1_f, g2_f]
    recv = [r_in, r_a, r_b, r_o, r_ff1, r_ff2]
    big = {}
    names = ["w_in", "w_br_a", "w_br_b", "w_out", "w_ff1", "w_ff2"]
    ws = [w_in, w_br_a, w_br_b, w_out, w_ff1, w_ff2]
    ms = [m_w_in, m_w_br_a, m_w_br_b, m_w_out, m_w_ff1, m_w_ff2]
    vs = [v_w_in, v_w_br_a, v_w_br_b, v_w_out, v_w_ff1, v_w_ff2]
    for i, nm in enumerate(names):
        g_own = lax.dynamic_index_in_dim(own[i], me, 0, keepdims=False)
        big[nm] = [o[None] for o in reduce_and_adamw(ws[i][0], g_own, recv[i], ms[i][0], vs[i][0], "adamw_" + nm)]

    dmod = jnp.concatenate([acc_in[2:3], acc_in[1:2], acc_mix[0:1], acc_ffn[3:4], acc_ffn[2:3], acc_ffn[4:5]], axis=0)
    db_in = jnp.concatenate([acc_hg[2:5], acc_mix[3:4], acc_conv[3:5], acc_mix[4:6]], axis=0)
    small = jnp.concatenate([
        dmod,
        acc_in[0:1], acc_mix[1:2], acc_ffn[0:1], acc_ffn[1:2],
        db_in,
        acc_hg[0:2],
        acc_mix[2:3],
        acc_conv[0:3],
        ddw,
    ], axis=0)
    total, kept = gather_and_sum_rows(small, 8)
    dmod_all = kept[:, 0:6, :].reshape(NDEV, 6 * D)
    wcols = w_ada.shape[2]
    gwa, dwa, nmwa, nvwa = ada_backward(call.T, lax.dynamic_slice_in_dim(dmod_all, me * wcols, wcols, axis=1),
                                        w_ada[0], m_w_ada[0], v_w_ada[0])
    ddw_mine = lax.dynamic_slice_in_dim(total[24:24 + KW], me * (D // NDEV), D // NDEV, axis=1)

    small_names = ["b_ada", "pre_norm_tm", "post_norm_tm", "pre_norm_cm", "post_norm_cm", "b_in", "hg_lb_logits",
                   "hg_norm", "conv_db", "conv_ln_g", "conv_ln_b", "conv_dw"]
    small_w = [b_ada, pre_norm_tm, post_norm_tm, pre_norm_cm, post_norm_cm, b_in, hg_lb_logits, hg_norm, conv_db,
               conv_ln_g, conv_ln_b, conv_dw]
    small_m = [m_b_ada, m_pre_norm_tm, m_post_norm_tm, m_pre_norm_cm, m_post_norm_cm, m_b_in, m_hg_lb_logits,
               m_hg_norm, m_conv_db, m_conv_ln_g, m_conv_ln_b, m_conv_dw]
    small_v = [v_b_ada, v_pre_norm_tm, v_post_norm_tm, v_pre_norm_cm, v_post_norm_cm, v_b_in, v_hg_lb_logits,
               v_hg_norm, v_conv_db, v_conv_ln_g, v_conv_ln_b, v_conv_dw]
    small_g = [total[0:6].reshape(1, 6 * D), total[6:7], total[7:8], total[8:9], total[9:10],
               total[10:18].reshape(1, 8 * D), total[18:20], total[20:21], total[21:22], total[22:23], total[23:24],
               ddw_mine[None]]
    sizes = [int(w.size) for w in small_w]
    padded = -(-sum(sizes) // (8 * D)) * (8 * D)

    def pack(arrs):
        flat = jnp.concatenate([a.reshape(-1) for a in arrs])
        return jnp.pad(flat, (0, padded - flat.shape[0])).reshape(padded // D, D)

    packed = adamw_rows(pack(small_w), pack(small_g), pack(small_m), pack(small_v), "adamw_small")
    sm = {}
    off = 0
    for nm, w, g, sz in zip(small_names, small_w, small_g, sizes):
        sm[nm] = [g.reshape(w.shape)] + [pk.reshape(-1)[off:off + sz].reshape(w.shape) for pk in packed]
        off += sz

    order = ["w_ada", "b_ada", "pre_norm_tm", "post_norm_tm", "pre_norm_cm", "post_norm_cm", "w_in", "b_in",
             "hg_lb_logits", "hg_norm", "conv_dw", "conv_db", "conv_ln_g", "conv_ln_b", "w_br_a", "w_br_b", "w_out",
             "w_ff1", "w_ff2"]
    res = dict(sm)
    res.update(big)
    res["w_ada"] = [gwa[None], dwa[None], nmwa[None], nvwa[None]]
    outs = [loss, grad_x[None]]
    for j in range(4):
        outs += [res[nm][j] for nm in order]
    return tuple(outs)
```

```python
import functools

import jax
import jax.numpy as jnp
from jax import lax
from jax.experimental import pallas as pl
from jax.experimental.pallas import tpu as pltpu

F32 = jnp.float32
BF16 = jnp.bfloat16
MESH = pl.DeviceIdType.MESH
AXES = ("x", "y", "c")

D = 1024
HEADS = 8
DK = 128
CH = 128
LEVELS = 7
KW = 31
HALO = 32
DFF = 4096
NDEV = 8
EPS = 1e-6
LOSS_ROW = 21
ADAM_LR, ADAM_B1, ADAM_B2, ADAM_EPS, ADAM_WD, ADAM_STEP = 0.001, 0.9, 0.999, 1e-08, 0.01, 10
VMEM_LIMIT = 58 * 1024 * 1024

_NN = (((1,), (0,)), ((), ()))
_NT = (((1,), (1,)), ((), ()))
_TN = (((0,), (0,)), ((), ()))

VSPEC = pl.BlockSpec(memory_space=pltpu.VMEM)
ANYSPEC = pl.BlockSpec(memory_space=pl.ANY)


def _params(*sem):
    return pltpu.CompilerParams(dimension_semantics=sem or None, vmem_limit_bytes=VMEM_LIMIT)


def _dot(a, b, dims):
    return lax.dot_general(a, b, dims, preferred_element_type=F32)


@jax.custom_vjp
def mm_nn(a, b):
    return _dot(a.astype(BF16), b.astype(BF16), _NN)


def _mm_nn_fwd(a, b):
    ab, bb = a.astype(BF16), b.astype(BF16)
    return _dot(ab, bb, _NN), (ab, bb)


def _mm_nn_bwd(res, ct):
    ab, bb = res
    cb = ct.astype(BF16)
    return _dot(cb, bb, _NT), _dot(ab, cb, _TN)


mm_nn.defvjp(_mm_nn_fwd, _mm_nn_bwd)


@jax.custom_vjp
def mm_nt(a, b):
    return _dot(a.astype(BF16), b.astype(BF16), _NT)


def _mm_nt_fwd(a, b):
    ab, bb = a.astype(BF16), b.astype(BF16)
    return _dot(ab, bb, _NT), (ab, bb)


def _mm_nt_bwd(res, ct):
    ab, bb = res
    cb = ct.astype(BF16)
    return _dot(cb, bb, _NN), _dot(cb, ab, _TN)


mm_nt.defvjp(_mm_nt_fwd, _mm_nt_bwd)


@jax.custom_vjp
def mm_tn(a, b):
    return _dot(a.astype(BF16), b.astype(BF16), _TN)


def _mm_tn_fwd(a, b):
    ab, bb = a.astype(BF16), b.astype(BF16)
    return _dot(ab, bb, _TN), (ab, bb)


def _mm_tn_bwd(res, ct):
    ab, bb = res
    cb = ct.astype(BF16)
    return _dot(bb, cb, _NT), _dot(ab, cb, _NN)


mm_tn.defvjp(_mm_tn_fwd, _mm_tn_bwd)


def _rms(x):
    return x * lax.rsqrt(jnp.mean(x * x, axis=-1, keepdims=True) + EPS)


def _silu(x):
    return x * jax.nn.sigmoid(x)


def f_modulate(x, gain, sc, sh):
    return _rms(x) * gain * (1.0 + sc) + sh


def f_residual(x, y, gate, gain):
    return x + gate * (_rms(y) * gain)


def f_merge(ga, gb, ya, yb):
    return jax.nn.sigmoid(ga) * ya + jax.nn.sigmoid(gb) * yb


def f_head_out(o, og, hg):
    heads = [_rms(o[:, h * DK:(h + 1) * DK]) for h in range(HEADS)]
    return jnp.concatenate(heads, axis=1) * hg * _silu(og)


def f_conv_act(u, g, b):
    mu = jnp.mean(u, axis=-1, keepdims=True)
    var = jnp.mean(jnp.square(u - mu), axis=-1, keepdims=True)
    return _silu((u - mu) * lax.rsqrt(var + EPS) * g + b)


def f_glu(cv, cg):
    return cv * jax.nn.sigmoid(cg)


def _split2(x):
    hi = x.astype(BF16)
    return hi, (x - hi.astype(F32)).astype(BF16)


def _tri(transposed):
    i = lax.broadcasted_iota(jnp.int32, (CH, CH), 1 if transposed else 0)
    t = lax.broadcasted_iota(jnp.int32, (CH, CH), 0 if transposed else 1)
    return jnp.where(t <= i, 1.0, 0.0).astype(BF16)


def _blocks3(x, rows):
    return x.reshape(CH // rows, rows, x.shape[-1])


def _mid_broadcast(b, lev):
    h = 1 << (LEVELS - 1 - lev)
    if h >= 4:
        x3 = _blocks3(b, 2 * h)
        return jnp.broadcast_to(x3[:, h - 1:h, :], x3.shape).reshape(b.shape)
    x3 = _blocks3(b, 8)
    sub = lax.broadcasted_iota(jnp.int32, x3.shape, 1)
    out = None
    for first in range(0, 8, 2 * h):
        piece = jnp.broadcast_to(x3[:, first + h - 1:first + h, :], x3.shape)
        out = piece if out is None else jnp.where(sub >= first, piece, out)
    return out.reshape(b.shape)


def _mid_scatter(d, lev):
    h = 1 << (LEVELS - 1 - lev)
    if h >= 4:
        x3 = _blocks3(d, 2 * h)
        row = lax.broadcasted_iota(jnp.int32, x3.shape, 1)
        total = jnp.sum(x3, axis=1, keepdims=True)
        return jnp.where(row == h - 1, total, 0.0).reshape(d.shape)
    x3 = _blocks3(d, 8)
    sub = lax.broadcasted_iota(jnp.int32, x3.shape, 1)
    out = jnp.zeros_like(x3)
    for first in range(0, 8, 2 * h):
        inside = (sub >= first) & (sub < first + 2 * h)
        total = jnp.sum(jnp.where(inside, x3, 0.0), axis=1, keepdims=True)
        out = jnp.where(sub == first + h - 1, total, out)
    return out.reshape(d.shape)


@jax.custom_vjp
def decay_sums(g):
    hi, lo = _split2(g)
    tri = _tri(False)
    b = _dot(tri, hi, _NN) + _dot(tri, lo, _NN)
    return (b,) + tuple(b - _mid_broadcast(b, lev) for lev in range(LEVELS))


def _decay_sums_fwd(g):
    return decay_sums(g), None


def _decay_sums_bwd(_, cts):
    db = cts[0]
    for lev in range(LEVELS):
        db = db + cts[1 + lev] - _mid_scatter(cts[1 + lev], lev)
    hi, lo = _split2(db)
    tri = _tri(True)
    return (_dot(tri, hi, _NN) + _dot(tri, lo, _NN),)


decay_sums.defvjp(_decay_sums_fwd, _decay_sums_bwd)


def _score_masks():
    i = lax.broadcasted_iota(jnp.int32, (CH, CH), 0)
    j = lax.broadcasted_iota(jnp.int32, (CH, CH), 1)
    masks = [i == j]
    for lev in range(LEVELS):
        sh = LEVELS - 1 - lev
        same = (i >> (sh + 1)) == (j >> (sh + 1))
        masks.append(same & (((i >> sh) & 1) == 1) & (((j >> sh) & 1) == 0))
    return [jnp.where(m, 1.0, 0.0) for m in masks]


def f_hgrn_chunk(q_r, f_r, v, logits):
    l0, l1 = logits[0:1, :], logits[1:2, :]
    mx = lax.stop_gradient(jnp.maximum(l0, l1))
    e0, e1 = jnp.exp(l0 - mx), jnp.exp(l1 - mx)
    lb = e0 / (e0 + e1)
    q = _silu(q_r)
    f = lb + (1.0 - lb) * jax.nn.sigmoid(f_r)
    k = 1.0 - f
    sums = decay_sums(jnp.log(f))
    b = sums[0]
    btot = b[CH - 1:CH, :]
    qe = q * jnp.exp(b)
    ke = k * jnp.exp(btot - b)
    dec = jnp.exp(btot)
    qs, ks = [q], [k]
    row = lax.broadcasted_iota(jnp.int32, b.shape, 0)
    for lev in range(LEVELS):
        upper = ((row >> (LEVELS - 1 - lev)) & 1) == 1
        e = sums[1 + lev]
        qs.append(q * jnp.exp(jnp.where(upper, e, 0.0)))
        ks.append(k * jnp.exp(jnp.where(upper, 0.0, -e)))
    masks = _score_masks()
    intra, ut = [], []
    for h in range(HEADS):
        sl = slice(h * DK, (h + 1) * DK)
        sc = None
        for lev in range(LEVELS + 1):
            s = mm_nt(qs[lev][:, sl], ks[lev][:, sl]) * masks[lev]
            sc = s if sc is None else sc + s
        intra.append(mm_nn(sc, v[:, sl]))
        ut.append(mm_tn(v[:, sl], ke[:, sl]))
    return jnp.concatenate(intra, axis=1), qe, jnp.concatenate(ut, axis=1), dec


def _inter(qe_b, st_ref, rows):
    out = []
    for ci in range(rows // CH):
        st = st_ref[ci].astype(BF16)
        heads = [_dot(qe_b[ci * CH:(ci + 1) * CH, h * DK:(h + 1) * DK], st[:, h * DK:(h + 1) * DK], _NT)
                 for h in range(HEADS)]
        out.append(jnp.concatenate(heads, axis=1))
    return jnp.concatenate(out, axis=0)


def _shift_stack(src, dst, rows):
    dst[0, 0:rows, :] = src[0:rows, :]
    for b in range(1, 8):
        dst[b, 0:rows - 8, :] = src[pl.ds(b, rows - 8), :]


def _shifted(stack, offset, rows):
    return stack[offset % 8, pl.ds(8 * (offset // 8), rows), :]


def _adamw(w, g, m, v):
    m = ADAM_B1 * m + (1.0 - ADAM_B1) * g
    v = ADAM_B2 * v + (1.0 - ADAM_B2) * jnp.square(g)
    m_hat = m / (1.0 - ADAM_B1 ** ADAM_STEP)
    v_hat = v / (1.0 - ADAM_B2 ** ADAM_STEP)
    delta = -ADAM_LR * (m_hat / (jnp.sqrt(v_hat) + ADAM_EPS) + ADAM_WD * w)
    return delta, m, v


def _me():
    return lax.axis_index("x"), lax.axis_index("y"), lax.axis_index("c")


def _peer(k):
    x, y, c = _me()
    mask = k + 1
    px = (1 - x) if (mask >> 2) & 1 else x
    py = (1 - y) if (mask >> 1) & 1 else y
    pc = (1 - c) if mask & 1 else c
    return (px, py, pc), 4 * px + 2 * py + pc


def ada_forward(c, w_ada, b_ada):
    wcols = w_ada.shape[1]

    def body(c_ref, w_ref, b_ref, mod_ref, call_ref, part_ref, modp_ref, send_sems, recv_sems):
        x, y, cc = _me()
        me = 4 * x + 2 * y + cc
        call_ref[me] = _silu(c_ref[...])
        sends = []
        for k in range(NDEV - 1):
            dev, _ = _peer(k)
            cp = pltpu.make_async_remote_copy(call_ref.at[me], call_ref.at[me], send_sems.at[k], recv_sems.at[k],
                                              device_id=dev, device_id_type=MESH)
            cp.start()
            sends.append(cp)
        for k in range(NDEV - 1):
            _, pidx = _peer(k)
            pltpu.make_async_remote_copy(call_ref.at[pidx], call_ref.at[pidx], send_sems.at[k], recv_sems.at[k],
                                         device_id=_peer(k)[0], device_id_type=MESH).wait_recv()
        call = jnp.concatenate([call_ref[r] for r in range(NDEV)], axis=0)
        part = _dot(call.astype(BF16), w_ref[...].astype(BF16), _NN)
        for r in range(NDEV):
            part_ref[r] = part[r:r + 1, :]
        modp_ref[me] = part_ref[me]
        for k in range(NDEV - 1):
            dev, pidx = _peer(k)
            cp = pltpu.make_async_remote_copy(part_ref.at[pidx], modp_ref.at[me], send_sems.at[NDEV - 1 + k],
                                              recv_sems.at[NDEV - 1 + k], device_id=dev, device_id_type=MESH)
            cp.start()
            sends.append(cp)
        for k in range(NDEV - 1):
            dev, pidx = _peer(k)
            pltpu.make_async_remote_copy(part_ref.at[pidx], modp_ref.at[pidx], send_sems.at[NDEV - 1 + k],
                                         recv_sems.at[NDEV - 1 + k], device_id=dev, device_id_type=MESH).wait_recv()
        for cp in sends:
            cp.wait_send()
        mod_ref[...] = jnp.concatenate([modp_ref[r] for r in range(NDEV)], axis=1) + b_ref[...]

    mod, call = pl.pallas_call(
        body, name="ada_forward",
        out_shape=(jax.ShapeDtypeStruct((1, NDEV * wcols), F32), jax.ShapeDtypeStruct((NDEV, 1, D), F32)),
        in_specs=[VSPEC, VSPEC, VSPEC], out_specs=(VSPEC, VSPEC),
        scratch_shapes=[pltpu.VMEM((NDEV, 1, wcols), F32), pltpu.VMEM((NDEV, 1, wcols), F32),
                        pltpu.SemaphoreType.DMA((2 * (NDEV - 1),)), pltpu.SemaphoreType.DMA((2 * (NDEV - 1),))],
        compiler_params=_params(),
    )(c, w_ada, b_ada)
    return mod, call.reshape(NDEV, D)


def _comm_sems(n, local):
    sems = [pltpu.SemaphoreType.DMA((7 * n,)), pltpu.SemaphoreType.DMA((7 * n,))]
    return sems + ([pltpu.SemaphoreType.DMA((n,))] if local else [])


def _gather2_ops(ins, outs, send_sems, recv_sems, local_sems):
    n = len(ins)
    x, y, c = _me()
    me, sibling = (x, y, c), (x, y, 1 - c)
    chips = [(1 - x, y), (x, 1 - y), (1 - x, 1 - y)]

    def slot(p):
        return 4 * p[0] + 2 * p[1] + p[2]

    def copy(a, k, block, to, src=None):
        return pltpu.make_async_remote_copy(
            src_ref=outs[a].at[slot(block)] if src is None else src, dst_ref=outs[a].at[slot(block)],
            send_sem=send_sems.at[a * 7 + k], recv_sem=recv_sems.at[a * 7 + k], device_id=to, device_id_type=MESH)

    def local(a):
        return pltpu.make_async_copy(ins[a], outs[a].at[slot(me)], local_sems.at[a])

    def first(a):
        return [copy(a, 0, me, sibling, src=ins[a])] + [copy(a, 1 + j, me, (*chip, c), src=ins[a])
                                                        for j, chip in enumerate(chips)]

    def passed(a):
        return [copy(a, 4 + j, (*chip, c), sibling) for j, chip in enumerate(chips)]

    def start():
        for a in range(n):
            local(a).start()
            for cp in first(a):
                cp.start()

    def forward():
        for j, chip in enumerate(chips):
            for a in range(n):
                copy(a, 1 + j, (*chip, c), me).wait_recv()
                passed(a)[j].start()

    def finish():
        for a in range(n):
            copy(a, 0, sibling, me).wait_recv()
            for j, chip in enumerate(chips):
                copy(a, 4 + j, (*chip, 1 - c), me).wait_recv()
        for a in range(n):
            for cp in first(a) + passed(a):
                cp.wait_send()
            local(a).wait()

    return start, forward, finish


def _gather1_ops(ins, outs, send_sems, recv_sems, local_sems):
    n = len(ins)
    x, y, c = _me()
    me = 4 * x + 2 * y + c

    def copy(a, k, block):
        return pltpu.make_async_remote_copy(ins[a], outs[a].at[block], send_sems.at[a * 7 + k],
                                            recv_sems.at[a * 7 + k], device_id=_peer(k)[0], device_id_type=MESH)

    def local(a):
        return pltpu.make_async_copy(ins[a], outs[a].at[me], local_sems.at[a])

    def start():
        for a in range(n):
            local(a).start()
            for k in range(NDEV - 1):
                copy(a, k, me).start()

    def finish():
        for a in range(n):
            for k in range(NDEV - 1):
                copy(a, k, _peer(k)[1]).wait_recv()
        for a in range(n):
            for k in range(NDEV - 1):
                copy(a, k, me).wait_send()
            local(a).wait()

    return start, finish


def _exchange_ops(ins, outs, send_sems, recv_sems):
    n = len(ins)

    def copy(a, k):
        dev, pidx = _peer(k)
        return pltpu.make_async_remote_copy(ins[a].at[pidx], outs[a].at[k], send_sems.at[a * 7 + k],
                                            recv_sems.at[a * 7 + k], device_id=dev, device_id_type=MESH)

    def start():
        for k in range(NDEV - 1):
            for a in range(n):
                copy(a, k).start()

    def finish():
        for k in range(NDEV - 1):
            for a in range(n):
                copy(a, k).wait_recv()
        for k in range(NDEV - 1):
            for a in range(n):
                copy(a, k).wait_send()

    return start, finish


def _gathered(shards):
    return tuple(jax.ShapeDtypeStruct((NDEV,) + s.shape, s.dtype) for s in shards)


def _exchanged(blocks):
    return tuple(jax.ShapeDtypeStruct((NDEV - 1,) + b.shape[1:], b.dtype) for b in blocks)


def all_gather_blocks(shards):
    n = len(shards)

    def body(*refs):
        start, forward, finish = _gather2_ops(refs[:n], refs[n:2 * n], *refs[2 * n:])
        start()
        forward()
        finish()

    return pl.pallas_call(
        body, name="all_gather_blocks", out_shape=_gathered(shards),
        in_specs=[ANYSPEC] * n, out_specs=tuple([ANYSPEC] * n),
        scratch_shapes=_comm_sems(n, True), compiler_params=_params(),
    )(*shards)


def gather_and_sum_rows(parts, keep_rows):
    n = len(parts)
    offs = [sum(p.shape[0] for p in parts[:i]) for i in range(n)]
    rows = sum(p.shape[0] for p in parts)

    def body(*refs):
        sum_ref, keep_ref, buf_ref, send_sems, recv_sems = refs[n:]
        x, y, c = _me()
        me = 4 * x + 2 * y + c
        for i in range(n):
            buf_ref[me, offs[i]:offs[i] + parts[i].shape[0], :] = refs[i][...]

        def copy(k, block):
            return pltpu.make_async_remote_copy(buf_ref.at[block], buf_ref.at[block], send_sems.at[k],
                                                recv_sems.at[k], device_id=_peer(k)[0], device_id_type=MESH)

        for k in range(NDEV - 1):
            copy(k, me).start()
        for k in range(NDEV - 1):
            copy(k, _peer(k)[1]).wait_recv()
        for k in range(NDEV - 1):
            copy(k, me).wait_send()
        total = buf_ref[0]
        for r in range(1, NDEV):
            total = total + buf_ref[r]
        sum_ref[...] = total
        keep_ref[...] = jnp.zeros_like(keep_ref)
        for r in range(NDEV):
            for j, src in enumerate(keep_rows):
                keep_ref[r, j:j + 1, :] = buf_ref[r, src:src + 1, :]

    return pl.pallas_call(
        body, name="gather_and_sum_rows",
        out_shape=(jax.ShapeDtypeStruct((rows, D), F32), jax.ShapeDtypeStruct((NDEV, 8, D), F32)),
        in_specs=[VSPEC] * n, out_specs=(VSPEC, VSPEC),
        scratch_shapes=[pltpu.VMEM((NDEV, rows, D), F32), pltpu.SemaphoreType.DMA((NDEV - 1,)),
                        pltpu.SemaphoreType.DMA((NDEV - 1,))],
        compiler_params=_params(),
    )(*parts)


def in_proj(x, gain, sc, sh, w_all, b_in, shards):
    t = x.shape[0]
    tm = min(256, t)
    n = len(shards)
    steps = t // tm

    def body(*refs):
        x_ref, g_ref, sc_ref, sh_ref, w_ref, b_ref = refs[:6]
        p_ref, h_ref = refs[6 + n:8 + n]
        start, forward, finish = _gather2_ops(refs[6:6 + n], refs[8 + n:8 + 2 * n], *refs[8 + 2 * n:])
        step = pl.program_id(0)
        pl.when(step == 0)(start)
        h = f_modulate(x_ref[...], g_ref[...], sc_ref[...], sh_ref[...]).astype(BF16)
        h_ref[...] = h
        for j in range(NDEV):
            p_ref[:, j * D:(j + 1) * D] = _dot(h, w_ref[j], _NN) + b_ref[:, j * D:(j + 1) * D]
        pl.when(step == (3 * steps) // 4)(forward)
        pl.when(step == steps - 1)(finish)

    vec = pl.BlockSpec((1, D), lambda m: (0, 0))
    res = pl.pallas_call(
        body, name="in_proj", grid=(steps,),
        out_shape=(jax.ShapeDtypeStruct((t, NDEV * D), F32), jax.ShapeDtypeStruct((t, D), BF16)) + _gathered(shards),
        in_specs=[pl.BlockSpec((tm, D), lambda m: (m, 0)), vec, vec, vec, VSPEC,
                  pl.BlockSpec((1, NDEV * D), lambda m: (0, 0))] + [ANYSPEC] * n,
        out_specs=(pl.BlockSpec((tm, NDEV * D), lambda m: (m, 0)), pl.BlockSpec((tm, D), lambda m: (m, 0)))
        + (ANYSPEC,) * n,
        scratch_shapes=_comm_sems(n, True),
        compiler_params=_params("arbitrary"),
    )(x, gain, sc, sh, w_all, b_in, *shards)
    return res[0], res[1], res[2:]


def hgrn_local(p, logits, shards):
    t = p.shape[0]
    nc = t // CH
    n = len(shards)

    def body(*refs):
        q_ref, f_ref, v_ref, l_ref = refs[:4]
        intra_ref, qe_ref, ut_ref, dec_ref = refs[4 + n:8 + n]
        start, forward, finish = _gather2_ops(refs[4:4 + n], refs[8 + n:8 + 2 * n], *refs[8 + 2 * n:])
        step = pl.program_id(0)
        pl.when(step == 0)(start)
        intra, qe, ut, dec = f_hgrn_chunk(q_ref[...], f_ref[...], v_ref[...], l_ref[...])
        intra_ref[...] = intra
        qe_ref[...] = qe.astype(BF16)
        ut_ref[...] = ut
        dec_ref[...] = dec
        pl.when(step == (3 * nc) // 4)(forward)
        pl.when(step == nc - 1)(finish)

    col = lambda j: pl.BlockSpec((CH, D), lambda c: (c, j))
    res = pl.pallas_call(
        body, name="hgrn_local", grid=(nc,),
        out_shape=(jax.ShapeDtypeStruct((t, D), F32), jax.ShapeDtypeStruct((t, D), BF16),
                   jax.ShapeDtypeStruct((nc, DK, D), F32), jax.ShapeDtypeStruct((nc, 1, D), F32)) + _gathered(shards),
        in_specs=[col(0), col(1), col(2), pl.BlockSpec((2, D), lambda c: (0, 0))] + [ANYSPEC] * n,
        out_specs=(pl.BlockSpec((CH, D), lambda c: (c, 0)), pl.BlockSpec((CH, D), lambda c: (c, 0)),
                   pl.BlockSpec((None, DK, D), lambda c: (c, 0, 0)), pl.BlockSpec((None, 1, D), lambda c: (c, 0, 0)))
        + (ANYSPEC,) * n,
        scratch_shapes=_comm_sems(n, True),
        compiler_params=_params("arbitrary"),
    )(p, p, p, logits, *shards)
    return res[:4] + (res[4:],)


def state_scan(u, dec, reverse):
    nc = u.shape[0]

    def body(u_ref, d_ref, out_ref, st):
        @pl.when(pl.program_id(0) == 0)
        def _():
            st[...] = jnp.zeros_like(st)

        s = st[...]
        out_ref[...] = s
        st[...] = d_ref[...] * s + u_ref[...]

    idx = (lambda i: (nc - 1 - i, 0, 0)) if reverse else (lambda i: (i, 0, 0))
    return pl.pallas_call(
        body, name="state_scan_rev" if reverse else "state_scan", grid=(nc,),
        out_shape=jax.ShapeDtypeStruct((nc, DK, D), F32),
        in_specs=[pl.BlockSpec((None, DK, D), idx), pl.BlockSpec((None, 1, D), idx)],
        out_specs=pl.BlockSpec((None, DK, D), idx),
        scratch_shapes=[pltpu.VMEM((DK, D), F32)],
        compiler_params=_params("arbitrary"),
    )(u, dec)


def _conv_tile(t):
    return min(256, t)


def conv_forward(p, dw, db, ln_g, ln_b, shards):
    t = p.shape[0]
    tm = _conv_tile(t)
    per = tm // HALO
    n = len(shards)
    nt = t // tm

    def body(*refs):
        cv_ref, cg_ref, cvp_ref, cgp_ref, dw_ref, db_ref, g_ref, b_ref = refs[:8]
        yc_ref, act_ref = refs[8 + n:10 + n]
        uext, ush = refs[10 + 2 * n:12 + 2 * n]
        start, finish = _gather1_ops(refs[8:8 + n], refs[10 + n:10 + 2 * n], *refs[12 + 2 * n:])
        step = pl.program_id(0)
        pl.when(step == 0)(start)
        uext[0:HALO, :] = jnp.where(step == 0, 0.0, f_glu(cvp_ref[...], cgp_ref[...]))
        uext[HALO:HALO + tm, :] = f_glu(cv_ref[...], cg_ref[...])
        _shift_stack(uext, ush, tm + HALO)
        acc = jnp.zeros((tm, D), F32) + db_ref[...]
        for w in range(KW):
            acc = acc + dw_ref[w:w + 1, :] * _shifted(ush, HALO - KW + 1 + w, tm)
        yc_ref[...] = acc
        act_ref[...] = f_conv_act(acc, g_ref[...], b_ref[...]).astype(BF16)
        pl.when(step == nt - 1)(finish)

    vec = pl.BlockSpec((1, D), lambda m: (0, 0))
    prev = lambda j: pl.BlockSpec((HALO, D), lambda m: (jnp.maximum(m * per - 1, 0), j))
    res = pl.pallas_call(
        body, name="conv_forward", grid=(nt,),
        out_shape=(jax.ShapeDtypeStruct((t, D), F32), jax.ShapeDtypeStruct((t, D), BF16)) + _gathered(shards),
        in_specs=[pl.BlockSpec((tm, D), lambda m: (m, 4)), pl.BlockSpec((tm, D), lambda m: (m, 5)), prev(4), prev(5),
                  pl.BlockSpec((HALO, D), lambda m: (0, 0)), vec, vec, vec] + [ANYSPEC] * n,
        out_specs=(pl.BlockSpec((tm, D), lambda m: (m, 0)), pl.BlockSpec((tm, D), lambda m: (m, 0))) + (ANYSPEC,) * n,
        scratch_shapes=[pltpu.VMEM((HALO + tm, D), F32), pltpu.VMEM((8, tm + HALO, D), F32)] + _comm_sems(n, True),
        compiler_params=_params("arbitrary"),
    )(p, p, p, p, dw, db, ln_g, ln_b, *shards)
    return res[0], res[1], res[2:]


def _mix_tile(t):
    return min(256, t)


def _mix_forward_tile(x_ref, og_ref, ga_ref, gb_ref, qe_ref, intra_ref, st_ref, act_ref, wa_ref, wb_ref, wo_ref,
                      hg_ref, rows):
    o = _inter(qe_ref[...], st_ref, rows) + intra_ref[...]
    a = f_head_out(o, og_ref[...], hg_ref[...])
    ya = _dot(a.astype(BF16), wa_ref[...], _NN)
    yb = _dot(act_ref[...], wb_ref[...], _NN)
    merged = f_merge(ga_ref[...], gb_ref[...], ya, yb)
    y = _dot(merged.astype(BF16), wo_ref[...], _NN)
    return o, a, ya, yb, merged, y


def _mix_specs(tm):
    col = lambda j: pl.BlockSpec((tm, D), lambda m: (m, j))
    row = pl.BlockSpec((tm, D), lambda m: (m, 0))
    return col, row, pl.BlockSpec((1, D), lambda m: (0, 0)), pl.BlockSpec((tm // CH, DK, D), lambda m: (m, 0, 0))


def mix_forward(x, p, qe, intra, st_prev, act, wa, wb, wo, hg, g1, post):
    t = x.shape[0]
    tm = _mix_tile(t)

    def body(x_ref, og_ref, ga_ref, gb_ref, qe_ref, intra_ref, st_ref, act_ref, wa_ref, wb_ref, wo_ref, hg_ref,
             g1_ref, post_ref, x1_ref):
        y = _mix_forward_tile(x_ref, og_ref, ga_ref, gb_ref, qe_ref, intra_ref, st_ref, act_ref, wa_ref, wb_ref,
                              wo_ref, hg_ref, tm)[-1]
        x1_ref[...] = f_residual(x_ref[...], y, g1_ref[...], post_ref[...])

    col, row, vec, stspec = _mix_specs(tm)
    return pl.pallas_call(
        body, name="mix_forward", grid=(t // tm,),
        out_shape=jax.ShapeDtypeStruct((t, D), F32),
        in_specs=[row, col(3), col(6), col(7), row, row, stspec, row, VSPEC, VSPEC, VSPEC, vec, vec, vec],
        out_specs=row,
        compiler_params=_params("parallel"),
    )(x, p, p, p, qe, intra, st_prev, act, wa, wb, wo, hg, g1, post)


def ffn_forward_backward(x1, target, w1, w2, pre, sc, sh, g2, post):
    t = x1.shape[0]
    tm = min(256, t)
    nb = w1.shape[0]
    fb = w1.shape[2]

    def body(x_ref, tg_ref, w1_ref, w2_ref, pre_ref, sc_ref, sh_ref, g2_ref, post_ref,
             dx_ref, h2_ref, r_ref, dz_ref, dy2_ref, acc_ref, z_sc):
        @pl.when(pl.program_id(0) == 0)
        def _():
            acc_ref[...] = jnp.zeros_like(acc_ref)

        x1v = x_ref[...]
        h2, vjp_h = jax.vjp(f_modulate, x1v, pre_ref[...], sc_ref[...], sh_ref[...])
        h2b = h2.astype(BF16)
        h2_ref[...] = h2b
        y2 = jnp.zeros((tm, D), F32)
        for n in range(nb):
            z = _dot(h2b, w1_ref[n], _NN)
            z_sc[:, n * fb:(n + 1) * fb] = z
            r = jnp.square(jnp.maximum(z, 0.0)).astype(BF16)
            r_ref[:, n * fb:(n + 1) * fb] = r
            y2 = y2 + _dot(r, w2_ref[n], _NN)
        out, vjp_r = jax.vjp(f_residual, x1v, y2, g2_ref[...], post_ref[...])
        err = out - tg_ref[...]
        tok = jnp.mean(jnp.square(err), axis=-1, keepdims=True)
        acc_ref[5:6, :] += 0.5 * jnp.sum(tok, axis=0, keepdims=True)
        dx_a, dy2, dg2, dpost = vjp_r(err * (1.0 / D))
        dy2b = dy2.astype(BF16)
        dy2_ref[...] = dy2b
        dh2 = jnp.zeros((tm, D), F32)
        for n in range(nb):
            dr = _dot(dy2b, w2_ref[n], _NT)
            dz = (dr * (2.0 * jnp.maximum(z_sc[:, n * fb:(n + 1) * fb], 0.0))).astype(BF16)
            dz_ref[:, n * fb:(n + 1) * fb] = dz
            dh2 = dh2 + _dot(dz, w1_ref[n], _NT)
        dx_b, dpre, dsc, dsh = vjp_h(dh2)
        dx_ref[...] = dx_a + dx_b
        acc_ref[0:1, :] += dpre
        acc_ref[1:2, :] += dpost
        acc_ref[2:3, :] += dsc
        acc_ref[3:4, :] += dsh
        acc_ref[4:5, :] += dg2

    row = pl.BlockSpec((tm, D), lambda m: (m, 0))
    wide = pl.BlockSpec((tm, DFF), lambda m: (m, 0))
    vec = pl.BlockSpec((1, D), lambda m: (0, 0))
    return pl.pallas_call(
        body, name="ffn_forward_backward", grid=(t // tm,),
        out_shape=(jax.ShapeDtypeStruct((t, D), F32), jax.ShapeDtypeStruct((t, D), BF16),
                   jax.ShapeDtypeStruct((t, DFF), BF16), jax.ShapeDtypeStruct((t, DFF), BF16),
                   jax.ShapeDtypeStruct((t, D), BF16), jax.ShapeDtypeStruct((8, D), F32)),
        in_specs=[row, row, VSPEC, VSPEC, vec, vec, vec, vec, vec],
        out_specs=(row, row, wide, wide, row, pl.BlockSpec((8, D), lambda m: (0, 0))),
        scratch_shapes=[pltpu.VMEM((tm, DFF), F32)],
        compiler_params=_params("arbitrary"),
    )(x1, target, w1, w2, pre, sc, sh, g2, post)


def mix_backward(x, p, qe, intra, st_prev, act, wa, wb, wo, hg, g1, post, dx1):
    t = x.shape[0]
    tm = _mix_tile(t)
    nc = t // CH

    def body(x_ref, og_ref, ga_ref, gb_ref, qe_ref, intra_ref, st_ref, act_ref, wa_ref, wb_ref, wo_ref, hg_ref,
             g1_ref, post_ref, dx1_ref,
             dog_ref, dga_ref, dgb_ref, do_ref, dqe_ref, vt_ref, dact_ref, a_ref, mg_ref, dya_ref, dyb_ref, dy_ref,
             acc_ref):
        @pl.when(pl.program_id(0) == 0)
        def _():
            acc_ref[...] = jnp.zeros_like(acc_ref)

        o, a, ya, yb, merged, y = _mix_forward_tile(x_ref, og_ref, ga_ref, gb_ref, qe_ref, intra_ref, st_ref, act_ref,
                                                    wa_ref, wb_ref, wo_ref, hg_ref, tm)
        a_ref[...] = a.astype(BF16)
        mg_ref[...] = merged.astype(BF16)
        _, vjp_r = jax.vjp(f_residual, x_ref[...], y, g1_ref[...], post_ref[...])
        _, dy, dg1, dpost = vjp_r(dx1_ref[...])
        dyb16 = dy.astype(BF16)
        dy_ref[...] = dyb16
        dmerged = _dot(dyb16, wo_ref[...], _NT)
        _, vjp_m = jax.vjp(f_merge, ga_ref[...], gb_ref[...], ya, yb)
        dga, dgb, dya, dyb = vjp_m(dmerged)
        dga_ref[...] = dga.astype(BF16)
        dgb_ref[...] = dgb.astype(BF16)
        dya16, dyb16b = dya.astype(BF16), dyb.astype(BF16)
        dya_ref[...] = dya16
        dyb_ref[...] = dyb16b
        da = _dot(dya16, wa_ref[...], _NT)
        dact_ref[...] = _dot(dyb16b, wb_ref[...], _NT)
        _, vjp_a = jax.vjp(f_head_out, o, og_ref[...], hg_ref[...])
        do, dog, dhg = vjp_a(da)
        dog_ref[...] = dog.astype(BF16)
        do_ref[...] = do
        do16 = do.astype(BF16)
        qe16 = qe_ref[...]
        for ci in range(tm // CH):
            st = st_ref[ci].astype(BF16)
            rows = slice(ci * CH, (ci + 1) * CH)
            dqe, vt = [], []
            for h in range(HEADS):
                sl = slice(h * DK, (h + 1) * DK)
                dqe.append(_dot(do16[rows, sl], st[:, sl], _NN))
                vt.append(_dot(do16[rows, sl], qe16[rows, sl], _TN))
            dqe_ref[rows, :] = jnp.concatenate(dqe, axis=1)
            vt_ref[ci] = jnp.concatenate(vt, axis=1)
        acc_ref[0:1, :] += dg1
        acc_ref[1:2, :] += dpost
        acc_ref[2:3, :] += dhg
        acc_ref[3:4, :] += jnp.sum(dog, axis=0, keepdims=True)
        acc_ref[4:5, :] += jnp.sum(dga, axis=0, keepdims=True)
        acc_ref[5:6, :] += jnp.sum(dgb, axis=0, keepdims=True)

    col, row, vec, stspec = _mix_specs(tm)
    b16 = jax.ShapeDtypeStruct((t, D), BF16)
    f32 = jax.ShapeDtypeStruct((t, D), F32)
    return pl.pallas_call(
        body, name="mix_backward", grid=(t // tm,),
        out_shape=(b16, b16, b16, f32, f32, jax.ShapeDtypeStruct((nc, DK, D), F32), f32, b16, b16, b16, b16, b16,
                   jax.ShapeDtypeStruct((8, D), F32)),
        in_specs=[row, col(3), col(6), col(7), row, row, stspec, row, VSPEC, VSPEC, VSPEC, vec, vec, vec, row],
        out_specs=(row, row, row, row, row, stspec, row, row, row, row, row, row,
                   pl.BlockSpec((8, D), lambda m: (0, 0))),
        compiler_params=_params("arbitrary"),
    )(x, p, p, p, qe, intra, st_prev, act, wa, wb, wo, hg, g1, post, dx1)


def conv_backward(p, yc, dact, dw, ln_g, ln_b, blocks):
    t = p.shape[0]
    tm = _conv_tile(t)
    per = tm // HALO
    nt = t // tm
    n = len(blocks)

    def body(*refs):
        cv_ref, cg_ref, cvp_ref, cgp_ref, yc_ref, ycn_ref, da_ref, dan_ref, dw_ref, g_ref, b_ref = refs[:11]
        dcv_ref, dcg_ref, acc_ref, ddw_ref = refs[11 + n:15 + n]
        uext, dyext, ush, dysh, ddw8 = refs[15 + 2 * n:20 + 2 * n]
        start, finish = _exchange_ops(refs[11:11 + n], refs[15 + n:15 + 2 * n], *refs[20 + 2 * n:])
        m = pl.program_id(0)

        @pl.when(m == 0)
        def _():
            start()
            acc_ref[...] = jnp.zeros_like(acc_ref)
            ddw8[...] = jnp.zeros_like(ddw8)

        cv, cg = cv_ref[...], cg_ref[...]
        u, vjp_u = jax.vjp(f_glu, cv, cg)
        uext[0:HALO, :] = jnp.where(m == 0, 0.0, f_glu(cvp_ref[...], cgp_ref[...]))
        uext[HALO:HALO + tm, :] = u
        _shift_stack(uext, ush, tm + HALO)
        _, vjp_c = jax.vjp(f_conv_act, yc_ref[...], g_ref[...], b_ref[...])
        dyc, dg, db = vjp_c(da_ref[...])
        _, vjp_n = jax.vjp(f_conv_act, ycn_ref[...], g_ref[...], b_ref[...])
        dyn = vjp_n(dan_ref[...])[0]
        dyext[0:tm, :] = dyc
        dyext[tm:tm + HALO, :] = jnp.where(m == nt - 1, 0.0, dyn)
        _shift_stack(dyext, dysh, tm + HALO)
        du = jnp.zeros((tm, D), F32)
        for w in range(KW):
            du = du + dw_ref[w:w + 1, :] * _shifted(dysh, KW - 1 - w, tm)
            prod = dyc * _shifted(ush, HALO - KW + 1 + w, tm)
            ddw8[w] += jnp.sum(prod.reshape(tm // 8, 8, D), axis=0)

        @pl.when(m == nt - 1)
        def _():
            ddw_ref[...] = jnp.sum(ddw8[...], axis=1)

        dcv, dcg = vjp_u(du)
        dcv_ref[...] = dcv.astype(BF16)
        dcg_ref[...] = dcg.astype(BF16)
        acc_ref[0:1, :] += jnp.sum(dyc, axis=0, keepdims=True)
        acc_ref[1:2, :] += dg
        acc_ref[2:3, :] += db
        acc_ref[3:4, :] += jnp.sum(dcv, axis=0, keepdims=True)
        acc_ref[4:5, :] += jnp.sum(dcg, axis=0, keepdims=True)
        pl.when(m == nt - 1)(finish)

    vec = pl.BlockSpec((1, D), lambda m: (0, 0))
    row = pl.BlockSpec((tm, D), lambda m: (m, 0))
    prev = lambda j: pl.BlockSpec((HALO, D), lambda m: (jnp.maximum(m * per - 1, 0), j))
    nxt = pl.BlockSpec((HALO, D), lambda m: (jnp.minimum((m + 1) * per, t // HALO - 1), 0))
    b16 = jax.ShapeDtypeStruct((t, D), BF16)
    res = pl.pallas_call(
        body, name="conv_backward", grid=(nt,),
        out_shape=(b16, b16, jax.ShapeDtypeStruct((8, D), F32), jax.ShapeDtypeStruct((HALO, D), F32))
        + _exchanged(blocks),
        in_specs=[pl.BlockSpec((tm, D), lambda m: (m, 4)), pl.BlockSpec((tm, D), lambda m: (m, 5)), prev(4), prev(5),
                  row, nxt, row, nxt, pl.BlockSpec((HALO, D), lambda m: (0, 0)), vec, vec] + [ANYSPEC] * n,
        out_specs=(row, row, pl.BlockSpec((8, D), lambda m: (0, 0)), pl.BlockSpec((HALO, D), lambda m: (0, 0)))
        + (ANYSPEC,) * n,
        scratch_shapes=[pltpu.VMEM((HALO + tm, D), F32), pltpu.VMEM((tm + HALO, D), F32),
                        pltpu.VMEM((8, tm + HALO, D), F32), pltpu.VMEM((8, tm + HALO, D), F32),
                        pltpu.VMEM((HALO, 8, D), F32)] + _comm_sems(n, False),
        compiler_params=_params("arbitrary"),
    )(p, p, p, p, yc, yc, dact, dact, dw, ln_g, ln_b, *blocks)
    return res[:4] + (res[4:],)


def hgrn_backward(p, logits, do, dqe, gt, st_prev, blocks):
    t = p.shape[0]
    nc = t // CH
    n = len(blocks)

    def body(*refs):
        q_ref, f_ref, v_ref, l_ref, do_ref, dqe_ref, gt_ref, st_ref = refs[:8]
        dq_ref, df_ref, dv_ref, acc_ref = refs[8 + n:12 + n]
        start, finish = _exchange_ops(refs[8:8 + n], refs[12 + n:12 + 2 * n], *refs[12 + 2 * n:])

        @pl.when(pl.program_id(0) == 0)
        def _():
            start()
            acc_ref[...] = jnp.zeros_like(acc_ref)

        gt_v = gt_ref[...]
        ddec = jnp.sum(gt_v * st_ref[...], axis=0, keepdims=True)
        _, vjp = jax.vjp(f_hgrn_chunk, q_ref[...], f_ref[...], v_ref[...], l_ref[...])
        dq, df, dv, dl = vjp((do_ref[...], dqe_ref[...], gt_v, ddec))
        dq_ref[...] = dq.astype(BF16)
        df_ref[...] = df.astype(BF16)
        dv_ref[...] = dv.astype(BF16)
        acc_ref[0:2, :] += dl
        acc_ref[2:3, :] += jnp.sum(dq, axis=0, keepdims=True)
        acc_ref[3:4, :] += jnp.sum(df, axis=0, keepdims=True)
        acc_ref[4:5, :] += jnp.sum(dv, axis=0, keepdims=True)
        pl.when(pl.program_id(0) == nc - 1)(finish)

    col = lambda j: pl.BlockSpec((CH, D), lambda c: (c, j))
    row = pl.BlockSpec((CH, D), lambda c: (c, 0))
    stspec = pl.BlockSpec((None, DK, D), lambda c: (c, 0, 0))
    b16 = jax.ShapeDtypeStruct((t, D), BF16)
    res = pl.pallas_call(
        body, name="hgrn_backward", grid=(nc,),
        out_shape=(b16, b16, b16, jax.ShapeDtypeStruct((8, D), F32)) + _exchanged(blocks),
        in_specs=[col(0), col(1), col(2), pl.BlockSpec((2, D), lambda c: (0, 0)), row, row, stspec, stspec]
        + [ANYSPEC] * n,
        out_specs=(row, row, row, pl.BlockSpec((8, D), lambda c: (0, 0))) + (ANYSPEC,) * n,
        scratch_shapes=_comm_sems(n, False),
        compiler_params=_params("arbitrary"),
    )(p, p, p, logits, do, dqe, gt, st_prev, *blocks)
    return res[:4] + (res[4:],)


def in_proj_backward(dps, w_all, x, dx1, gain, sc, sh, blocks):
    t = x.shape[0]
    tm = min(256, t)
    nt = t // tm
    n = len(blocks)

    def body(*refs):
        dp_refs = refs[:NDEV]
        w_ref, x_ref, dx1_ref, g_ref, sc_ref, sh_ref = refs[NDEV:NDEV + 6]
        gx_ref, acc_ref = refs[NDEV + 6 + n:NDEV + 8 + n]
        start, finish = _exchange_ops(refs[NDEV + 6:NDEV + 6 + n], refs[NDEV + 8 + n:NDEV + 8 + 2 * n],
                                      *refs[NDEV + 8 + 2 * n:])

        @pl.when(pl.program_id(0) == 0)
        def _():
            start()
            acc_ref[...] = jnp.zeros_like(acc_ref)

        dh = jnp.zeros((tm, D), F32)
        for j in range(NDEV):
            dh = dh + _dot(dp_refs[j][...], w_ref[j], _NT)
        _, vjp_h = jax.vjp(f_modulate, x_ref[...], g_ref[...], sc_ref[...], sh_ref[...])
        dx, dg, dsc, dsh = vjp_h(dh)
        gx_ref[...] = dx1_ref[...] + dx
        acc_ref[0:1, :] += dg
        acc_ref[1:2, :] += dsc
        acc_ref[2:3, :] += dsh
        pl.when(pl.program_id(0) == nt - 1)(finish)

    row = pl.BlockSpec((tm, D), lambda m: (m, 0))
    vec = pl.BlockSpec((1, D), lambda m: (0, 0))
    res = pl.pallas_call(
        body, name="in_proj_backward", grid=(nt,),
        out_shape=(jax.ShapeDtypeStruct((t, D), F32), jax.ShapeDtypeStruct((8, D), F32)) + _exchanged(blocks),
        in_specs=[row] * NDEV + [VSPEC, row, row, vec, vec, vec] + [ANYSPEC] * n,
        out_specs=(row, pl.BlockSpec((8, D), lambda m: (0, 0))) + (ANYSPEC,) * n,
        scratch_shapes=_comm_sems(n, False),
        compiler_params=_params("arbitrary"),
    )(*dps, w_all, x, dx1, gain, sc, sh, *blocks)
    return res[0], res[1], res[2:]


def weight_grad(a, b, nblk, ka, bn, a_blocked, name):
    t = a.shape[0]
    tk = min(512, t)
    nk = t // tk

    def body(a_ref, b_ref, f_ref, h_ref, acc):
        k = pl.program_id(1)

        @pl.when(k == 0)
        def _():
            acc[...] = jnp.zeros_like(acc)

        acc[...] += _dot(a_ref[...], b_ref[...], _TN)

        @pl.when(k == nk - 1)
        def _():
            f_ref[...] = acc[...]
            h_ref[...] = acc[...].astype(BF16)

    a_idx = (lambda n, k: (k, n)) if a_blocked else (lambda n, k: (k, 0))
    b_idx = (lambda n, k: (k, 0)) if a_blocked else (lambda n, k: (k, n))
    out = pl.BlockSpec((None, ka, bn), lambda n, k: (n, 0, 0))
    return pl.pallas_call(
        body, name=name, grid=(nblk, nk),
        out_shape=(jax.ShapeDtypeStruct((nblk, ka, bn), F32), jax.ShapeDtypeStruct((nblk, ka, bn), BF16)),
        in_specs=[pl.BlockSpec((tk, ka), a_idx), pl.BlockSpec((tk, bn), b_idx)],
        out_specs=(out, out),
        scratch_shapes=[pltpu.VMEM((ka, bn), F32)],
        compiler_params=_params("parallel", "arbitrary"),
    )(a, b)


def ada_backward(call_t, dmod_cols, w, m, v):
    def body(c_ref, d_ref, w_ref, m_ref, v_ref, g_ref, dl_ref, nm_ref, nv_ref):
        ct, dm = c_ref[...], d_ref[...]
        g = ct[:, 0:1] * dm[0:1, :]
        for r in range(1, NDEV):
            g = g + ct[:, r:r + 1] * dm[r:r + 1, :]
        g_ref[...] = g
        dl_ref[...], nm_ref[...], nv_ref[...] = _adamw(w_ref[...], g, m_ref[...], v_ref[...])

    s = jax.ShapeDtypeStruct(w.shape, F32)
    return pl.pallas_call(
        body, name="ada_backward", out_shape=(s, s, s, s),
        in_specs=[VSPEC] * 5, out_specs=(VSPEC,) * 4, compiler_params=_params(),
    )(call_t, dmod_cols, w, m, v)


def adamw_small(total, ddw_mine, recipes, ws, ms, vs):
    n = len(ws)

    def body(*refs):
        tot, ddw = refs[0], refs[1]
        w_refs, m_refs, v_refs = refs[2:2 + n], refs[2 + n:2 + 2 * n], refs[2 + 2 * n:2 + 3 * n]
        outs = refs[2 + 3 * n:2 + 7 * n]
        loss_ref = refs[2 + 7 * n]
        for i, rec in enumerate(recipes):
            if rec == "dw":
                g = ddw[...]
            elif isinstance(rec, tuple):
                g = tot[rec[0]:rec[1], :]
            else:
                g = jnp.concatenate([tot[r:r + 1, :] for r in rec], axis=1) if len(rec) > 1 else tot[rec[0]:rec[0] + 1, :]
            dl, nm, nv = _adamw(w_refs[i][...], g, m_refs[i][...], v_refs[i][...])
            outs[4 * i][...] = g
            outs[4 * i + 1][...] = dl
            outs[4 * i + 2][...] = nm
            outs[4 * i + 3][...] = nv
        loss_ref[...] = tot[LOSS_ROW:LOSS_ROW + 1, 0:128]

    shapes = []
    for w in ws:
        shapes += [jax.ShapeDtypeStruct(w.shape, F32)] * 4
    res = pl.pallas_call(
        body, name="adamw_small", out_shape=tuple(shapes) + (jax.ShapeDtypeStruct((1, 128), F32),),
        in_specs=[VSPEC] * (2 + 3 * n), out_specs=(VSPEC,) * (4 * n + 1), compiler_params=_params(),
    )(total, ddw_mine, *ws, *ms, *vs)
    return [res[4 * i:4 * i + 4] for i in range(n)], res[4 * n]


def reduce_and_adamw(me, w, g_all, g_recv, m, v, name):
    r, c = w.shape
    br = min(128, r)

    def body(me_ref, w_ref, go_ref, gr_ref, m_ref, v_ref, g_ref, dl_ref, nm_ref, nv_ref):
        g = go_ref[...]
        for k in range(NDEV - 1):
            g = g + gr_ref[k].astype(F32)
        g_ref[...] = g
        dl_ref[...], nm_ref[...], nv_ref[...] = _adamw(w_ref[...], g, m_ref[...], v_ref[...])

    blk = pl.BlockSpec((br, c), lambda i, me_ref: (i, 0))
    s = jax.ShapeDtypeStruct(w.shape, F32)
    return pl.pallas_call(
        body, name=name, out_shape=(s, s, s, s),
        grid_spec=pltpu.PrefetchScalarGridSpec(
            num_scalar_prefetch=1, grid=(r // br,),
            in_specs=[blk, pl.BlockSpec((None, br, c), lambda i, me_ref: (me_ref[0], i, 0)),
                      pl.BlockSpec((NDEV - 1, br, c), lambda i, me_ref: (0, i, 0)), blk, blk],
            out_specs=(blk, blk, blk, blk)),
        compiler_params=_params("parallel"),
    )(me, w, g_all, g_recv, m, v)


def kernel(x, c, w_ada, b_ada, pre_norm_tm, post_norm_tm, pre_norm_cm, post_norm_cm, w_in, b_in, hg_lb_logits, hg_norm, conv_dw, conv_db, conv_ln_g, conv_ln_b, w_br_a, w_br_b, w_out, w_ff1, w_ff2, loss_target, m_w_ada, m_b_ada, m_pre_norm_tm, m_post_norm_tm, m_pre_norm_cm, m_post_norm_cm, m_w_in, m_b_in, m_hg_lb_logits, m_hg_norm, m_conv_dw, m_conv_db, m_conv_ln_g, m_conv_ln_b, m_w_br_a, m_w_br_b, m_w_out, m_w_ff1, m_w_ff2, v_w_ada, v_b_ada, v_pre_norm_tm, v_post_norm_tm, v_pre_norm_cm, v_post_norm_cm, v_w_in, v_b_in, v_hg_lb_logits, v_hg_norm, v_conv_dw, v_conv_db, v_conv_ln_g, v_conv_ln_b, v_w_br_a, v_w_br_b, v_w_out, v_w_ff1, v_w_ff2):
    t = x.shape[1]
    me = 4 * lax.axis_index("x") + 2 * lax.axis_index("y") + lax.axis_index("c")
    xs = x[0]
    tgt = loss_target[0]

    mod, call = ada_forward(c, w_ada[0], b_ada)
    sh1, sc1, g1, sh2, sc2, g2 = [mod[:, i * D:(i + 1) * D] for i in range(6)]
    win_all, dw_all = all_gather_blocks([w_in[0].astype(BF16), conv_dw[0]])
    dw = jnp.pad(dw_all.transpose(1, 0, 2).reshape(KW, D), ((0, HALO - KW), (0, 0)))

    p, h, (w1_all,) = in_proj(xs, pre_norm_tm, sc1, sh1, win_all, b_in, [w_ff1[0].astype(BF16)])
    intra, qe, ut, dec, (w2_all,) = hgrn_local(p, hg_lb_logits, [w_ff2[0].astype(BF16)])
    st_prev = state_scan(ut, dec, False)
    yc, act, (wa_all, wb_all, wo_all) = conv_forward(
        p, dw, conv_db, conv_ln_g, conv_ln_b, [w_br_a[0].astype(BF16), w_br_b[0].astype(BF16), w_out[0].astype(BF16)])
    wa, wb, wo = wa_all.reshape(D, D), wb_all.reshape(D, D), wo_all.reshape(D, D)
    x1 = mix_forward(xs, p, qe, intra, st_prev, act, wa, wb, wo, hg_norm, g1, post_norm_tm)

    dx1, h2, r, dz, dy2, acc_ffn = ffn_forward_backward(x1, tgt, w1_all, w2_all, pre_norm_cm, sc2, sh2, g2,
                                                        post_norm_cm)
    rows = D // NDEV
    g1_f, g1_h = weight_grad(h2, dz, NDEV, D, DFF // NDEV, False, "grad_w_ff1")
    g2_f, g2_h = weight_grad(r, dy2, 4, D, D, True, "grad_w_ff2")
    g2_f, g2_h = g2_f.reshape(NDEV, DFF // NDEV, D), g2_h.reshape(NDEV, DFF // NDEV, D)

    (dog, dga, dgb, do, dqe, vt, dact, a16, mg16, dya, dyb, dy, acc_mix) = mix_backward(
        xs, p, qe, intra, st_prev, act, wa, wb, wo, hg_norm, g1, post_norm_tm, dx1)
    gt = state_scan(vt, dec, True)
    dq, df, di, acc_hg, (r_ff1, r_ff2) = hgrn_backward(p, hg_lb_logits, do, dqe, gt, st_prev, [g1_h, g2_h])
    ga_f, ga_h = weight_grad(a16, dya, 1, D, D, False, "grad_w_br_a")
    gb_f, gb_h = weight_grad(act, dyb, 1, D, D, False, "grad_w_br_b")
    go_f, go_h = weight_grad(mg16, dy, 1, D, D, False, "grad_w_out")
    dcv, dcg, acc_conv, ddw, (r_a, r_b, r_o) = conv_backward(
        p, yc, dact, dw, conv_ln_g, conv_ln_b,
        [ga_h.reshape(NDEV, rows, D), gb_h.reshape(NDEV, rows, D), go_h.reshape(NDEV, rows, D)])
    dps = [dq, df, di, dog, dcv, dcg, dga, dgb]
    gw_in = [weight_grad(h, dps[n], 1, D, D, False, f"grad_w_in_{n}") for n in range(NDEV)]
    gin_f = jnp.concatenate([g[0] for g in gw_in], axis=0)
    gin_h = jnp.concatenate([g[1] for g in gw_in], axis=0)
    grad_x, acc_in, (r_in,) = in_proj_backward(dps, win_all, xs, dx1, pre_norm_tm, sc1, sh1, [gin_h])

    own = [gin_f, ga_f.reshape(NDEV, rows, D), gb_f.reshape(NDEV, rows, D), go_f.reshape(NDEV, rows, D), g1_f, g2_f]
    recv = [r_in, r_a, r_b, r_o, r_ff1, r_ff2]
    big = {}
    names = ["w_in", "w_br_a", "w_br_b", "w_out", "w_ff1", "w_ff2"]
    ws = [w_in, w_br_a, w_br_b, w_out, w_ff1, w_ff2]
    ms = [m_w_in, m_w_br_a, m_w_br_b, m_w_out, m_w_ff1, m_w_ff2]
    vs = [v_w_in, v_w_br_a, v_w_br_b, v_w_out, v_w_ff1, v_w_ff2]
    me1 = me.astype(jnp.int32).reshape(1)
    for i, nm in enumerate(names):
        big[nm] = [o[None] for o in reduce_and_adamw(me1, ws[i][0], own[i], recv[i], ms[i][0], vs[i][0], "adamw_" + nm)]

    dmod_rows = [2, 1, 8, 19, 18, 20]
    total, kept = gather_and_sum_rows([acc_in, acc_mix, acc_ffn, acc_hg, acc_conv, ddw], dmod_rows)
    dmod_all = kept[:, 0:6, :].reshape(NDEV, 6 * D)
    wcols = w_ada.shape[2]
    gwa, dwa, nmwa, nvwa = ada_backward(call.T, lax.dynamic_slice_in_dim(dmod_all, me * wcols, wcols, axis=1),
                                        w_ada[0], m_w_ada[0], v_w_ada[0])
    ddw_mine = lax.dynamic_slice_in_dim(total[40:40 + KW], me * (D // NDEV), D // NDEV, axis=1)

    small_names = ["b_ada", "pre_norm_tm", "post_norm_tm", "pre_norm_cm", "post_norm_cm", "b_in", "hg_lb_logits",
                   "hg_norm", "conv_db", "conv_ln_g", "conv_ln_b", "conv_dw"]
    recipes = [dmod_rows, [0], [9], [16], [17], [26, 27, 28, 11, 35, 36, 12, 13], (24, 26), [10], [32], [33], [34],
               "dw"]
    small_w = [b_ada, pre_norm_tm, post_norm_tm, pre_norm_cm, post_norm_cm, b_in, hg_lb_logits, hg_norm, conv_db,
               conv_ln_g, conv_ln_b, conv_dw[0]]
    small_m = [m_b_ada, m_pre_norm_tm, m_post_norm_tm, m_pre_norm_cm, m_post_norm_cm, m_b_in, m_hg_lb_logits,
               m_hg_norm, m_conv_db, m_conv_ln_g, m_conv_ln_b, m_conv_dw[0]]
    small_v = [v_b_ada, v_pre_norm_tm, v_post_norm_tm, v_pre_norm_cm, v_post_norm_cm, v_b_in, v_hg_lb_logits,
               v_hg_norm, v_conv_db, v_conv_ln_g, v_conv_ln_b, v_conv_dw[0]]
    small_out, loss_row = adamw_small(total, ddw_mine, recipes, small_w, small_m, small_v)
    loss = loss_row[0, 0]
    sm = {nm: list(o) for nm, o in zip(small_names, small_out)}
    sm["conv_dw"] = [o[None] for o in sm["conv_dw"]]

    order = ["w_ada", "b_ada", "pre_norm_tm", "post_norm_tm", "pre_norm_cm", "post_norm_cm", "w_in", "b_in",
             "hg_lb_logits", "hg_norm", "conv_dw", "conv_db", "conv_ln_g", "conv_ln_b", "w_br_a", "w_br_b", "w_out",
             "w_ff1", "w_ff2"]
    res = dict(sm)
    res.update(big)
    res["w_ada"] = [gwa[None], dwa[None], nmwa[None], nvwa[None]]
    outs = [loss, grad_x[None]]
    for j in range(4):
        outs += [res[nm][j] for nm in order]
    return tuple(outs)
```

```python
import functools

import jax
import jax.numpy as jnp
from jax import lax
from jax.experimental import pallas as pl
from jax.experimental.pallas import tpu as pltpu

F32 = jnp.float32
BF16 = jnp.bfloat16
MESH = pl.DeviceIdType.MESH
AXES = ("x", "y", "c")

D = 1024
HEADS = 8
DK = 128
CH = 128
LEVELS = 7
KW = 31
HALO = 32
DFF = 4096
NDEV = 8
EPS = 1e-6
DP_SPLIT = (3, 6, 7, 0, 1, 2, 4, 5)
LOSS_ROW = 21
ADAM_LR, ADAM_B1, ADAM_B2, ADAM_EPS, ADAM_WD, ADAM_STEP = 0.001, 0.9, 0.999, 1e-08, 0.01, 10
VMEM_LIMIT = 58 * 1024 * 1024

_NN = (((1,), (0,)), ((), ()))
_NT = (((1,), (1,)), ((), ()))
_TN = (((0,), (0,)), ((), ()))

VSPEC = pl.BlockSpec(memory_space=pltpu.VMEM)
ANYSPEC = pl.BlockSpec(memory_space=pl.ANY)


def _params(*sem):
    return pltpu.CompilerParams(dimension_semantics=sem or None, vmem_limit_bytes=VMEM_LIMIT)


def _dot(a, b, dims):
    return lax.dot_general(a, b, dims, preferred_element_type=F32)


@jax.custom_vjp
def mm_nn(a, b):
    return _dot(a.astype(BF16), b.astype(BF16), _NN)


def _mm_nn_fwd(a, b):
    ab, bb = a.astype(BF16), b.astype(BF16)
    return _dot(ab, bb, _NN), (ab, bb)


def _mm_nn_bwd(res, ct):
    ab, bb = res
    cb = ct.astype(BF16)
    return _dot(cb, bb, _NT), _dot(ab, cb, _TN)


mm_nn.defvjp(_mm_nn_fwd, _mm_nn_bwd)


@jax.custom_vjp
def mm_nt(a, b):
    return _dot(a.astype(BF16), b.astype(BF16), _NT)


def _mm_nt_fwd(a, b):
    ab, bb = a.astype(BF16), b.astype(BF16)
    return _dot(ab, bb, _NT), (ab, bb)


def _mm_nt_bwd(res, ct):
    ab, bb = res
    cb = ct.astype(BF16)
    return _dot(cb, bb, _NN), _dot(cb, ab, _TN)


mm_nt.defvjp(_mm_nt_fwd, _mm_nt_bwd)


@jax.custom_vjp
def mm_tn(a, b):
    return _dot(a.astype(BF16), b.astype(BF16), _TN)


def _mm_tn_fwd(a, b):
    ab, bb = a.astype(BF16), b.astype(BF16)
    return _dot(ab, bb, _TN), (ab, bb)


def _mm_tn_bwd(res, ct):
    ab, bb = res
    cb = ct.astype(BF16)
    return _dot(bb, cb, _NT), _dot(ab, cb, _NN)


mm_tn.defvjp(_mm_tn_fwd, _mm_tn_bwd)


def _rms(x):
    return x * lax.rsqrt(jnp.mean(x * x, axis=-1, keepdims=True) + EPS)


def _silu(x):
    return x * jax.nn.sigmoid(x)


def f_modulate(x, gain, sc, sh):
    return _rms(x) * gain * (1.0 + sc) + sh


def f_residual(x, y, gate, gain):
    return x + gate * (_rms(y) * gain)


def f_merge(ga, gb, ya, yb):
    return jax.nn.sigmoid(ga) * ya + jax.nn.sigmoid(gb) * yb


def f_head_out(o, og, hg):
    heads = [_rms(o[:, h * DK:(h + 1) * DK]) for h in range(HEADS)]
    return jnp.concatenate(heads, axis=1) * hg * _silu(og)


def f_conv_act(u, g, b):
    mu = jnp.mean(u, axis=-1, keepdims=True)
    var = jnp.mean(jnp.square(u - mu), axis=-1, keepdims=True)
    return _silu((u - mu) * lax.rsqrt(var + EPS) * g + b)


def f_glu(cv, cg):
    return cv * jax.nn.sigmoid(cg)


def _split2(x):
    hi = x.astype(BF16)
    return hi, (x - hi.astype(F32)).astype(BF16)


def _tri(transposed):
    i = lax.broadcasted_iota(jnp.int32, (CH, CH), 1 if transposed else 0)
    t = lax.broadcasted_iota(jnp.int32, (CH, CH), 0 if transposed else 1)
    return jnp.where(t <= i, 1.0, 0.0).astype(BF16)


def _blocks3(x, rows):
    return x.reshape(CH // rows, rows, x.shape[-1])


def _mid_broadcast(b, lev):
    h = 1 << (LEVELS - 1 - lev)
    if h >= 4:
        x3 = _blocks3(b, 2 * h)
        return jnp.broadcast_to(x3[:, h - 1:h, :], x3.shape).reshape(b.shape)
    x3 = _blocks3(b, 8)
    sub = lax.broadcasted_iota(jnp.int32, x3.shape, 1)
    out = None
    for first in range(0, 8, 2 * h):
        piece = jnp.broadcast_to(x3[:, first + h - 1:first + h, :], x3.shape)
        out = piece if out is None else jnp.where(sub >= first, piece, out)
    return out.reshape(b.shape)


def _mid_scatter(d, lev):
    h = 1 << (LEVELS - 1 - lev)
    if h >= 4:
        x3 = _blocks3(d, 2 * h)
        row = lax.broadcasted_iota(jnp.int32, x3.shape, 1)
        total = jnp.sum(x3, axis=1, keepdims=True)
        return jnp.where(row == h - 1, total, 0.0).reshape(d.shape)
    x3 = _blocks3(d, 8)
    sub = lax.broadcasted_iota(jnp.int32, x3.shape, 1)
    out = jnp.zeros_like(x3)
    for first in range(0, 8, 2 * h):
        inside = (sub >= first) & (sub < first + 2 * h)
        total = jnp.sum(jnp.where(inside, x3, 0.0), axis=1, keepdims=True)
        out = jnp.where(sub == first + h - 1, total, out)
    return out.reshape(d.shape)


@jax.custom_vjp
def decay_sums(g):
    hi, lo = _split2(g)
    tri = _tri(False)
    b = _dot(tri, hi, _NN) + _dot(tri, lo, _NN)
    return (b,) + tuple(b - _mid_broadcast(b, lev) for lev in range(LEVELS))


def _decay_sums_fwd(g):
    return decay_sums(g), None


def _decay_sums_bwd(_, cts):
    db = cts[0]
    for lev in range(LEVELS):
        db = db + cts[1 + lev] - _mid_scatter(cts[1 + lev], lev)
    hi, lo = _split2(db)
    tri = _tri(True)
    return (_dot(tri, hi, _NN) + _dot(tri, lo, _NN),)


decay_sums.defvjp(_decay_sums_fwd, _decay_sums_bwd)


def _score_masks():
    i = lax.broadcasted_iota(jnp.int32, (CH, CH), 0)
    j = lax.broadcasted_iota(jnp.int32, (CH, CH), 1)
    masks = [i == j]
    for lev in range(LEVELS):
        sh = LEVELS - 1 - lev
        same = (i >> (sh + 1)) == (j >> (sh + 1))
        masks.append(same & (((i >> sh) & 1) == 1) & (((j >> sh) & 1) == 0))
    return [jnp.where(m, 1.0, 0.0) for m in masks]


def f_hgrn_chunk(q_r, f_r, v, logits):
    l0, l1 = logits[0:1, :], logits[1:2, :]
    mx = lax.stop_gradient(jnp.maximum(l0, l1))
    e0, e1 = jnp.exp(l0 - mx), jnp.exp(l1 - mx)
    lb = e0 / (e0 + e1)
    q = _silu(q_r)
    f = lb + (1.0 - lb) * jax.nn.sigmoid(f_r)
    k = 1.0 - f
    sums = decay_sums(jnp.log(f))
    b = sums[0]
    btot = b[CH - 1:CH, :]
    qe = q * jnp.exp(b)
    ke = k * jnp.exp(btot - b)
    dec = jnp.exp(btot)
    qs, ks = [q], [k]
    row = lax.broadcasted_iota(jnp.int32, b.shape, 0)
    for lev in range(LEVELS):
        upper = ((row >> (LEVELS - 1 - lev)) & 1) == 1
        e = sums[1 + lev]
        qs.append(q * jnp.exp(jnp.where(upper, e, 0.0)))
        ks.append(k * jnp.exp(jnp.where(upper, 0.0, -e)))
    masks = _score_masks()
    intra, ut = [], []
    for h in range(HEADS):
        sl = slice(h * DK, (h + 1) * DK)
        sc = None
        for lev in range(LEVELS + 1):
            s = mm_nt(qs[lev][:, sl], ks[lev][:, sl]) * masks[lev]
            sc = s if sc is None else sc + s
        intra.append(mm_nn(sc, v[:, sl]))
        ut.append(mm_tn(v[:, sl], ke[:, sl]))
    return jnp.concatenate(intra, axis=1), qe, jnp.concatenate(ut, axis=1), dec


def _inter(qe_b, st_ref, rows):
    out = []
    for ci in range(rows // CH):
        st = st_ref[ci].astype(BF16)
        heads = [_dot(qe_b[ci * CH:(ci + 1) * CH, h * DK:(h + 1) * DK], st[:, h * DK:(h + 1) * DK], _NT)
                 for h in range(HEADS)]
        out.append(jnp.concatenate(heads, axis=1))
    return jnp.concatenate(out, axis=0)


def _shift_stack(src, dst, rows):
    dst[0, 0:rows, :] = src[0:rows, :]
    for b in range(1, 8):
        dst[b, 0:rows - 8, :] = src[pl.ds(b, rows - 8), :]


def _shifted(stack, offset, rows):
    return stack[offset % 8, pl.ds(8 * (offset // 8), rows), :]


def _adamw(w, g, m, v):
    m = ADAM_B1 * m + (1.0 - ADAM_B1) * g
    v = ADAM_B2 * v + (1.0 - ADAM_B2) * jnp.square(g)
    m_hat = m / (1.0 - ADAM_B1 ** ADAM_STEP)
    v_hat = v / (1.0 - ADAM_B2 ** ADAM_STEP)
    delta = -ADAM_LR * (m_hat / (jnp.sqrt(v_hat) + ADAM_EPS) + ADAM_WD * w)
    return delta, m, v


def _me():
    return lax.axis_index("x"), lax.axis_index("y"), lax.axis_index("c")


def _peer(k):
    x, y, c = _me()
    mask = k + 1
    px = (1 - x) if (mask >> 2) & 1 else x
    py = (1 - y) if (mask >> 1) & 1 else y
    pc = (1 - c) if mask & 1 else c
    return (px, py, pc), 4 * px + 2 * py + pc


def ada_forward(c, w_ada, b_ada):
    wcols = w_ada.shape[1]

    def body(c_ref, w_ref, b_ref, mod_ref, call_ref, part_ref, modp_ref, send_sems, recv_sems):
        x, y, cc = _me()
        me = 4 * x + 2 * y + cc
        call_ref[me] = _silu(c_ref[...])
        sends = []
        for k in range(NDEV - 1):
            dev, _ = _peer(k)
            cp = pltpu.make_async_remote_copy(call_ref.at[me], call_ref.at[me], send_sems.at[k], recv_sems.at[k],
                                              device_id=dev, device_id_type=MESH)
            cp.start()
            sends.append(cp)
        for k in range(NDEV - 1):
            _, pidx = _peer(k)
            pltpu.make_async_remote_copy(call_ref.at[pidx], call_ref.at[pidx], send_sems.at[k], recv_sems.at[k],
                                         device_id=_peer(k)[0], device_id_type=MESH).wait_recv()
        call = jnp.concatenate([call_ref[r] for r in range(NDEV)], axis=0)
        part = _dot(call.astype(BF16), w_ref[...].astype(BF16), _NN)
        for r in range(NDEV):
            part_ref[r] = part[r:r + 1, :]
        modp_ref[me] = part_ref[me]
        for k in range(NDEV - 1):
            dev, pidx = _peer(k)
            cp = pltpu.make_async_remote_copy(part_ref.at[pidx], modp_ref.at[me], send_sems.at[NDEV - 1 + k],
                                              recv_sems.at[NDEV - 1 + k], device_id=dev, device_id_type=MESH)
            cp.start()
            sends.append(cp)
        for k in range(NDEV - 1):
            dev, pidx = _peer(k)
            pltpu.make_async_remote_copy(part_ref.at[pidx], modp_ref.at[pidx], send_sems.at[NDEV - 1 + k],
                                         recv_sems.at[NDEV - 1 + k], device_id=dev, device_id_type=MESH).wait_recv()
        for cp in sends:
            cp.wait_send()
        mod_ref[...] = jnp.concatenate([modp_ref[r] for r in range(NDEV)], axis=1) + b_ref[...]

    mod, call = pl.pallas_call(
        body, name="ada_forward",
        out_shape=(jax.ShapeDtypeStruct((1, NDEV * wcols), F32), jax.ShapeDtypeStruct((NDEV, 1, D), F32)),
        in_specs=[VSPEC, VSPEC, VSPEC], out_specs=(VSPEC, VSPEC),
        scratch_shapes=[pltpu.VMEM((NDEV, 1, wcols), F32), pltpu.VMEM((NDEV, 1, wcols), F32),
                        pltpu.SemaphoreType.DMA((2 * (NDEV - 1),)), pltpu.SemaphoreType.DMA((2 * (NDEV - 1),))],
        compiler_params=_params(),
    )(c, w_ada, b_ada)
    return mod, call.reshape(NDEV, D)


def _comm_sems(n, local):
    sems = [pltpu.SemaphoreType.DMA((7 * n,)), pltpu.SemaphoreType.DMA((7 * n,))]
    return sems + ([pltpu.SemaphoreType.DMA((n,))] if local else [])


def _gather2_ops(ins, outs, send_sems, recv_sems, local_sems):
    n = len(ins)
    x, y, c = _me()
    me, sibling = (x, y, c), (x, y, 1 - c)
    chips = [(1 - x, y), (x, 1 - y), (1 - x, 1 - y)]

    def slot(p):
        return 4 * p[0] + 2 * p[1] + p[2]

    def copy(a, k, block, to, src=None):
        return pltpu.make_async_remote_copy(
            src_ref=outs[a].at[slot(block)] if src is None else src, dst_ref=outs[a].at[slot(block)],
            send_sem=send_sems.at[a * 7 + k], recv_sem=recv_sems.at[a * 7 + k], device_id=to, device_id_type=MESH)

    def local(a):
        return pltpu.make_async_copy(ins[a], outs[a].at[slot(me)], local_sems.at[a])

    def first(a):
        return [copy(a, 0, me, sibling, src=ins[a])] + [copy(a, 1 + j, me, (*chip, c), src=ins[a])
                                                        for j, chip in enumerate(chips)]

    def passed(a):
        return [copy(a, 4 + j, (*chip, c), sibling) for j, chip in enumerate(chips)]

    def start():
        for a in range(n):
            local(a).start()
            for cp in first(a):
                cp.start()

    def forward():
        for j, chip in enumerate(chips):
            for a in range(n):
                copy(a, 1 + j, (*chip, c), me).wait_recv()
                passed(a)[j].start()

    def finish():
        for a in range(n):
            copy(a, 0, sibling, me).wait_recv()
            for j, chip in enumerate(chips):
                copy(a, 4 + j, (*chip, 1 - c), me).wait_recv()
        for a in range(n):
            for cp in first(a) + passed(a):
                cp.wait_send()
            local(a).wait()

    return start, forward, finish


def _gather1_ops(ins, outs, send_sems, recv_sems, local_sems):
    n = len(ins)
    x, y, c = _me()
    me = 4 * x + 2 * y + c

    def copy(a, k, block):
        return pltpu.make_async_remote_copy(ins[a], outs[a].at[block], send_sems.at[a * 7 + k],
                                            recv_sems.at[a * 7 + k], device_id=_peer(k)[0], device_id_type=MESH)

    def local(a):
        return pltpu.make_async_copy(ins[a], outs[a].at[me], local_sems.at[a])

    def start():
        for a in range(n):
            local(a).start()
            for k in range(NDEV - 1):
                copy(a, k, me).start()

    def finish():
        for a in range(n):
            for k in range(NDEV - 1):
                copy(a, k, _peer(k)[1]).wait_recv()
        for a in range(n):
            for k in range(NDEV - 1):
                copy(a, k, me).wait_send()
            local(a).wait()

    return start, finish


def _exchange_ops(ins, outs, send_sems, recv_sems):
    n = len(ins)

    def copy(a, k):
        dev, pidx = _peer(k)
        return pltpu.make_async_remote_copy(ins[a].at[pidx], outs[a].at[k], send_sems.at[a * 7 + k],
                                            recv_sems.at[a * 7 + k], device_id=dev, device_id_type=MESH)

    def start():
        for k in range(NDEV - 1):
            for a in range(n):
                copy(a, k).start()

    def finish():
        for k in range(NDEV - 1):
            for a in range(n):
                copy(a, k).wait_recv()
        for k in range(NDEV - 1):
            for a in range(n):
                copy(a, k).wait_send()

    return start, finish


def _gathered(shards):
    return tuple(jax.ShapeDtypeStruct((NDEV,) + s.shape, s.dtype) for s in shards)


def _exchanged(blocks):
    return tuple(jax.ShapeDtypeStruct((NDEV - 1,) + b.shape[1:], b.dtype) for b in blocks)


def all_gather_blocks(shards):
    n = len(shards)

    def body(*refs):
        start, forward, finish = _gather2_ops(refs[:n], refs[n:2 * n], *refs[2 * n:])
        start()
        forward()
        finish()

    return pl.pallas_call(
        body, name="all_gather_blocks", out_shape=_gathered(shards),
        in_specs=[ANYSPEC] * n, out_specs=tuple([ANYSPEC] * n),
        scratch_shapes=_comm_sems(n, True), compiler_params=_params(),
    )(*shards)


def gather_and_sum_rows(parts, keep_rows):
    n = len(parts)
    offs = [sum(p.shape[0] for p in parts[:i]) for i in range(n)]
    rows = sum(p.shape[0] for p in parts)

    def body(*refs):
        sum_ref, keep_ref, buf_ref, send_sems, recv_sems = refs[n:]
        x, y, c = _me()
        me = 4 * x + 2 * y + c
        for i in range(n):
            buf_ref[me, offs[i]:offs[i] + parts[i].shape[0], :] = refs[i][...]

        def copy(k, block):
            return pltpu.make_async_remote_copy(buf_ref.at[block], buf_ref.at[block], send_sems.at[k],
                                                recv_sems.at[k], device_id=_peer(k)[0], device_id_type=MESH)

        for k in range(NDEV - 1):
            copy(k, me).start()
        for k in range(NDEV - 1):
            copy(k, _peer(k)[1]).wait_recv()
        for k in range(NDEV - 1):
            copy(k, me).wait_send()
        total = buf_ref[0]
        for r in range(1, NDEV):
            total = total + buf_ref[r]
        sum_ref[...] = total
        keep_ref[...] = jnp.zeros_like(keep_ref)
        for r in range(NDEV):
            for j, src in enumerate(keep_rows):
                keep_ref[r, j:j + 1, :] = buf_ref[r, src:src + 1, :]

    return pl.pallas_call(
        body, name="gather_and_sum_rows",
        out_shape=(jax.ShapeDtypeStruct((rows, D), F32), jax.ShapeDtypeStruct((NDEV, 8, D), F32)),
        in_specs=[VSPEC] * n, out_specs=(VSPEC, VSPEC),
        scratch_shapes=[pltpu.VMEM((NDEV, rows, D), F32), pltpu.SemaphoreType.DMA((NDEV - 1,)),
                        pltpu.SemaphoreType.DMA((NDEV - 1,))],
        compiler_params=_params(),
    )(*parts)


def in_proj(x, gain, sc, sh, w_all, b_in, shards):
    t = x.shape[0]
    tm = min(256, t)
    n = len(shards)
    steps = t // tm

    def body(*refs):
        x_ref, g_ref, sc_ref, sh_ref, w_ref, b_ref = refs[:6]
        p_ref, h_ref = refs[6 + n:8 + n]
        start, forward, finish = _gather2_ops(refs[6:6 + n], refs[8 + n:8 + 2 * n], *refs[8 + 2 * n:])
        step = pl.program_id(0)
        pl.when(step == 0)(start)
        h = f_modulate(x_ref[...], g_ref[...], sc_ref[...], sh_ref[...]).astype(BF16)
        h_ref[...] = h
        for j in range(NDEV):
            p_ref[:, j * D:(j + 1) * D] = _dot(h, w_ref[j], _NN) + b_ref[:, j * D:(j + 1) * D]
        pl.when(step == (3 * steps) // 4)(forward)
        pl.when(step == steps - 1)(finish)

    vec = pl.BlockSpec((1, D), lambda m: (0, 0))
    res = pl.pallas_call(
        body, name="in_proj", grid=(steps,),
        out_shape=(jax.ShapeDtypeStruct((t, NDEV * D), F32), jax.ShapeDtypeStruct((t, D), BF16)) + _gathered(shards),
        in_specs=[pl.BlockSpec((tm, D), lambda m: (m, 0)), vec, vec, vec, VSPEC,
                  pl.BlockSpec((1, NDEV * D), lambda m: (0, 0))] + [ANYSPEC] * n,
        out_specs=(pl.BlockSpec((tm, NDEV * D), lambda m: (m, 0)), pl.BlockSpec((tm, D), lambda m: (m, 0)))
        + (ANYSPEC,) * n,
        scratch_shapes=_comm_sems(n, True),
        compiler_params=_params("arbitrary"),
    )(x, gain, sc, sh, w_all, b_in, *shards)
    return res[0], res[1], res[2:]


def hgrn_local(p, logits, shards):
    t = p.shape[0]
    nc = t // CH
    n = len(shards)

    def body(*refs):
        q_ref, f_ref, v_ref, l_ref = refs[:4]
        intra_ref, qe_ref, ut_ref, dec_ref = refs[4 + n:8 + n]
        start, forward, finish = _gather2_ops(refs[4:4 + n], refs[8 + n:8 + 2 * n], *refs[8 + 2 * n:])
        step = pl.program_id(0)
        pl.when(step == 0)(start)
        intra, qe, ut, dec = f_hgrn_chunk(q_ref[...], f_ref[...], v_ref[...], l_ref[...])
        intra_ref[...] = intra
        qe_ref[...] = qe.astype(BF16)
        ut_ref[...] = ut
        dec_ref[...] = dec
        pl.when(step == (3 * nc) // 4)(forward)
        pl.when(step == nc - 1)(finish)

    col = lambda j: pl.BlockSpec((CH, D), lambda c: (c, j))
    res = pl.pallas_call(
        body, name="hgrn_local", grid=(nc,),
        out_shape=(jax.ShapeDtypeStruct((t, D), F32), jax.ShapeDtypeStruct((t, D), BF16),
                   jax.ShapeDtypeStruct((nc, DK, D), F32), jax.ShapeDtypeStruct((nc, 1, D), F32)) + _gathered(shards),
        in_specs=[col(0), col(1), col(2), pl.BlockSpec((2, D), lambda c: (0, 0))] + [ANYSPEC] * n,
        out_specs=(pl.BlockSpec((CH, D), lambda c: (c, 0)), pl.BlockSpec((CH, D), lambda c: (c, 0)),
                   pl.BlockSpec((None, DK, D), lambda c: (c, 0, 0)), pl.BlockSpec((None, 1, D), lambda c: (c, 0, 0)))
        + (ANYSPEC,) * n,
        scratch_shapes=_comm_sems(n, True),
        compiler_params=_params("arbitrary"),
    )(p, p, p, logits, *shards)
    return res[:4] + (res[4:],)


def state_scan(u, dec, reverse):
    nc = u.shape[0]

    def body(u_ref, d_ref, out_ref, st):
        @pl.when(pl.program_id(0) == 0)
        def _():
            st[...] = jnp.zeros_like(st)

        s = st[...]
        out_ref[...] = s
        st[...] = d_ref[...] * s + u_ref[...]

    idx = (lambda i: (nc - 1 - i, 0, 0)) if reverse else (lambda i: (i, 0, 0))
    return pl.pallas_call(
        body, name="state_scan_rev" if reverse else "state_scan", grid=(nc,),
        out_shape=jax.ShapeDtypeStruct((nc, DK, D), F32),
        in_specs=[pl.BlockSpec((None, DK, D), idx), pl.BlockSpec((None, 1, D), idx)],
        out_specs=pl.BlockSpec((None, DK, D), idx),
        scratch_shapes=[pltpu.VMEM((DK, D), F32)],
        compiler_params=_params("arbitrary"),
    )(u, dec)


def _conv_tile(t):
    return min(256, t)


def conv_forward(p, dw, db, ln_g, ln_b, shards):
    t = p.shape[0]
    tm = _conv_tile(t)
    per = tm // HALO
    n = len(shards)
    nt = t // tm

    def body(*refs):
        cv_ref, cg_ref, cvp_ref, cgp_ref, dw_ref, db_ref, g_ref, b_ref = refs[:8]
        yc_ref, act_ref = refs[8 + n:10 + n]
        uext, ush = refs[10 + 2 * n:12 + 2 * n]
        start, finish = _gather1_ops(refs[8:8 + n], refs[10 + n:10 + 2 * n], *refs[12 + 2 * n:])
        step = pl.program_id(0)
        pl.when(step == 0)(start)
        uext[0:HALO, :] = jnp.where(step == 0, 0.0, f_glu(cvp_ref[...], cgp_ref[...]))
        uext[HALO:HALO + tm, :] = f_glu(cv_ref[...], cg_ref[...])
        _shift_stack(uext, ush, tm + HALO)
        acc = jnp.zeros((tm, D), F32) + db_ref[...]
        for w in range(KW):
            acc = acc + dw_ref[w:w + 1, :] * _shifted(ush, HALO - KW + 1 + w, tm)
        yc_ref[...] = acc
        act_ref[...] = f_conv_act(acc, g_ref[...], b_ref[...]).astype(BF16)
        pl.when(step == nt - 1)(finish)

    vec = pl.BlockSpec((1, D), lambda m: (0, 0))
    prev = lambda j: pl.BlockSpec((HALO, D), lambda m: (jnp.maximum(m * per - 1, 0), j))
    res = pl.pallas_call(
        body, name="conv_forward", grid=(nt,),
        out_shape=(jax.ShapeDtypeStruct((t, D), F32), jax.ShapeDtypeStruct((t, D), BF16)) + _gathered(shards),
        in_specs=[pl.BlockSpec((tm, D), lambda m: (m, 4)), pl.BlockSpec((tm, D), lambda m: (m, 5)), prev(4), prev(5),
                  pl.BlockSpec((HALO, D), lambda m: (0, 0)), vec, vec, vec] + [ANYSPEC] * n,
        out_specs=(pl.BlockSpec((tm, D), lambda m: (m, 0)), pl.BlockSpec((tm, D), lambda m: (m, 0))) + (ANYSPEC,) * n,
        scratch_shapes=[pltpu.VMEM((HALO + tm, D), F32), pltpu.VMEM((8, tm + HALO, D), F32)] + _comm_sems(n, True),
        compiler_params=_params("arbitrary"),
    )(p, p, p, p, dw, db, ln_g, ln_b, *shards)
    return res[0], res[1], res[2:]


def _mix_tile(t):
    return min(256, t)


def _mix_forward_tile(x_ref, og_ref, ga_ref, gb_ref, qe_ref, intra_ref, st_ref, act_ref, wa_ref, wb_ref, wo_ref,
                      hg_ref, rows):
    o = _inter(qe_ref[...], st_ref, rows) + intra_ref[...]
    a = f_head_out(o, og_ref[...], hg_ref[...])
    ya = _dot(a.astype(BF16), wa_ref[...], _NN)
    yb = _dot(act_ref[...], wb_ref[...], _NN)
    merged = f_merge(ga_ref[...], gb_ref[...], ya, yb)
    y = _dot(merged.astype(BF16), wo_ref[...], _NN)
    return o, a, ya, yb, merged, y


def _mix_specs(tm):
    col = lambda j: pl.BlockSpec((tm, D), lambda m: (m, j))
    row = pl.BlockSpec((tm, D), lambda m: (m, 0))
    return col, row, pl.BlockSpec((1, D), lambda m: (0, 0)), pl.BlockSpec((tm // CH, DK, D), lambda m: (m, 0, 0))


def mix_forward(x, p, qe, intra, st_prev, act, wa, wb, wo, hg, g1, post):
    t = x.shape[0]
    tm = _mix_tile(t)

    def body(x_ref, og_ref, ga_ref, gb_ref, qe_ref, intra_ref, st_ref, act_ref, wa_ref, wb_ref, wo_ref, hg_ref,
             g1_ref, post_ref, x1_ref):
        y = _mix_forward_tile(x_ref, og_ref, ga_ref, gb_ref, qe_ref, intra_ref, st_ref, act_ref, wa_ref, wb_ref,
                              wo_ref, hg_ref, tm)[-1]
        x1_ref[...] = f_residual(x_ref[...], y, g1_ref[...], post_ref[...])

    col, row, vec, stspec = _mix_specs(tm)
    return pl.pallas_call(
        body, name="mix_forward", grid=(t // tm,),
        out_shape=jax.ShapeDtypeStruct((t, D), F32),
        in_specs=[row, col(3), col(6), col(7), row, row, stspec, row, VSPEC, VSPEC, VSPEC, vec, vec, vec],
        out_specs=row,
        compiler_params=_params("parallel"),
    )(x, p, p, p, qe, intra, st_prev, act, wa, wb, wo, hg, g1, post)


def ffn_forward_backward(x1, target, w1, w2, pre, sc, sh, g2, post):
    t = x1.shape[0]
    tm = min(256, t)
    nb = w1.shape[0]
    fb = w1.shape[2]

    def body(x_ref, tg_ref, w1_ref, w2_ref, pre_ref, sc_ref, sh_ref, g2_ref, post_ref,
             dx_ref, h2_ref, r_ref, dz_ref, dy2_ref, acc_ref, z_sc):
        @pl.when(pl.program_id(0) == 0)
        def _():
            acc_ref[...] = jnp.zeros_like(acc_ref)

        x1v = x_ref[...]
        h2, vjp_h = jax.vjp(f_modulate, x1v, pre_ref[...], sc_ref[...], sh_ref[...])
        h2b = h2.astype(BF16)
        h2_ref[...] = h2b
        y2 = jnp.zeros((tm, D), F32)
        for n in range(nb):
            z = _dot(h2b, w1_ref[n], _NN)
            z_sc[:, n * fb:(n + 1) * fb] = z
            r = jnp.square(jnp.maximum(z, 0.0)).astype(BF16)
            r_ref[:, n * fb:(n + 1) * fb] = r
            y2 = y2 + _dot(r, w2_ref[n], _NN)
        out, vjp_r = jax.vjp(f_residual, x1v, y2, g2_ref[...], post_ref[...])
        err = out - tg_ref[...]
        tok = jnp.mean(jnp.square(err), axis=-1, keepdims=True)
        acc_ref[5:6, :] += 0.5 * jnp.sum(tok, axis=0, keepdims=True)
        dx_a, dy2, dg2, dpost = vjp_r(err * (1.0 / D))
        dy2b = dy2.astype(BF16)
        dy2_ref[...] = dy2b
        dh2 = jnp.zeros((tm, D), F32)
        for n in range(nb):
            dr = _dot(dy2b, w2_ref[n], _NT)
            dz = (dr * (2.0 * jnp.maximum(z_sc[:, n * fb:(n + 1) * fb], 0.0))).astype(BF16)
            dz_ref[:, n * fb:(n + 1) * fb] = dz
            dh2 = dh2 + _dot(dz, w1_ref[n], _NT)
        dx_b, dpre, dsc, dsh = vjp_h(dh2)
        dx_ref[...] = dx_a + dx_b
        acc_ref[0:1, :] += dpre
        acc_ref[1:2, :] += dpost
        acc_ref[2:3, :] += dsc
        acc_ref[3:4, :] += dsh
        acc_ref[4:5, :] += dg2

    row = pl.BlockSpec((tm, D), lambda m: (m, 0))
    wide = pl.BlockSpec((tm, DFF), lambda m: (m, 0))
    vec = pl.BlockSpec((1, D), lambda m: (0, 0))
    return pl.pallas_call(
        body, name="ffn_forward_backward", grid=(t // tm,),
        out_shape=(jax.ShapeDtypeStruct((t, D), F32), jax.ShapeDtypeStruct((t, D), BF16),
                   jax.ShapeDtypeStruct((t, DFF), BF16), jax.ShapeDtypeStruct((t, DFF), BF16),
                   jax.ShapeDtypeStruct((t, D), BF16), jax.ShapeDtypeStruct((8, D), F32)),
        in_specs=[row, row, VSPEC, VSPEC, vec, vec, vec, vec, vec],
        out_specs=(row, row, wide, wide, row, pl.BlockSpec((8, D), lambda m: (0, 0))),
        scratch_shapes=[pltpu.VMEM((tm, DFF), F32)],
        compiler_params=_params("arbitrary"),
    )(x1, target, w1, w2, pre, sc, sh, g2, post)


def mix_backward(x, p, qe, intra, st_prev, act, wa, wb, wo, hg, g1, post, dx1):
    t = x.shape[0]
    tm = _mix_tile(t)
    nc = t // CH

    def body(x_ref, og_ref, ga_ref, gb_ref, qe_ref, intra_ref, st_ref, act_ref, wa_ref, wb_ref, wo_ref, hg_ref,
             g1_ref, post_ref, dx1_ref,
             dp_ref, do_ref, dqe_ref, vt_ref, dact_ref, a_ref, mg_ref, dya_ref, dyb_ref, dy_ref, acc_ref):
        @pl.when(pl.program_id(0) == 0)
        def _():
            acc_ref[...] = jnp.zeros_like(acc_ref)

        o, a, ya, yb, merged, y = _mix_forward_tile(x_ref, og_ref, ga_ref, gb_ref, qe_ref, intra_ref, st_ref, act_ref,
                                                    wa_ref, wb_ref, wo_ref, hg_ref, tm)
        a_ref[...] = a.astype(BF16)
        mg_ref[...] = merged.astype(BF16)
        _, vjp_r = jax.vjp(f_residual, x_ref[...], y, g1_ref[...], post_ref[...])
        _, dy, dg1, dpost = vjp_r(dx1_ref[...])
        dyb16 = dy.astype(BF16)
        dy_ref[...] = dyb16
        dmerged = _dot(dyb16, wo_ref[...], _NT)
        _, vjp_m = jax.vjp(f_merge, ga_ref[...], gb_ref[...], ya, yb)
        dga, dgb, dya, dyb = vjp_m(dmerged)
        dp_ref[:, D:2 * D] = dga.astype(BF16)
        dp_ref[:, 2 * D:3 * D] = dgb.astype(BF16)
        dya16, dyb16b = dya.astype(BF16), dyb.astype(BF16)
        dya_ref[...] = dya16
        dyb_ref[...] = dyb16b
        da = _dot(dya16, wa_ref[...], _NT)
        dact_ref[...] = _dot(dyb16b, wb_ref[...], _NT)
        _, vjp_a = jax.vjp(f_head_out, o, og_ref[...], hg_ref[...])
        do, dog, dhg = vjp_a(da)
        dp_ref[:, 0:D] = dog.astype(BF16)
        do_ref[...] = do
        do16 = do.astype(BF16)
        qe16 = qe_ref[...]
        for ci in range(tm // CH):
            st = st_ref[ci].astype(BF16)
            rows = slice(ci * CH, (ci + 1) * CH)
            dqe, vt = [], []
            for h in range(HEADS):
                sl = slice(h * DK, (h + 1) * DK)
                dqe.append(_dot(do16[rows, sl], st[:, sl], _NN))
                vt.append(_dot(do16[rows, sl], qe16[rows, sl], _TN))
            dqe_ref[rows, :] = jnp.concatenate(dqe, axis=1)
            vt_ref[ci] = jnp.concatenate(vt, axis=1)
        acc_ref[0:1, :] += dg1
        acc_ref[1:2, :] += dpost
        acc_ref[2:3, :] += dhg
        acc_ref[3:4, :] += jnp.sum(dog, axis=0, keepdims=True)
        acc_ref[4:5, :] += jnp.sum(dga, axis=0, keepdims=True)
        acc_ref[5:6, :] += jnp.sum(dgb, axis=0, keepdims=True)

    col, row, vec, stspec = _mix_specs(tm)
    b16 = jax.ShapeDtypeStruct((t, D), BF16)
    f32 = jax.ShapeDtypeStruct((t, D), F32)
    return pl.pallas_call(
        body, name="mix_backward", grid=(t // tm,),
        out_shape=(jax.ShapeDtypeStruct((t, NDEV * D), BF16), f32, f32, jax.ShapeDtypeStruct((nc, DK, D), F32), f32,
                   b16, b16, b16, b16, b16, jax.ShapeDtypeStruct((8, D), F32)),
        in_specs=[row, col(3), col(6), col(7), row, row, stspec, row, VSPEC, VSPEC, VSPEC, vec, vec, vec, row],
        out_specs=(pl.BlockSpec((tm, 3 * D), lambda m: (m, 0)), row, row, stspec, row, row, row, row, row, row,
                   pl.BlockSpec((8, D), lambda m: (0, 0))),
        compiler_params=_params("arbitrary"),
    )(x, p, p, p, qe, intra, st_prev, act, wa, wb, wo, hg, g1, post, dx1)


def conv_backward(p, yc, dact, dw, ln_g, ln_b, blocks, dp):
    t = p.shape[0]
    tm = _conv_tile(t)
    per = tm // HALO
    nt = t // tm
    n = len(blocks)

    def body(*refs):
        cv_ref, cg_ref, cvp_ref, cgp_ref, yc_ref, ycn_ref, da_ref, dan_ref, dw_ref, g_ref, b_ref = refs[:11]
        dp_ref, acc_ref, ddw_ref = refs[12 + n:15 + n]
        uext, dyext, ush, dysh, ddw8 = refs[15 + 2 * n:20 + 2 * n]
        start, finish = _exchange_ops(refs[11:11 + n], refs[15 + n:15 + 2 * n], *refs[20 + 2 * n:])
        m = pl.program_id(0)

        @pl.when(m == 0)
        def _():
            start()
            acc_ref[...] = jnp.zeros_like(acc_ref)
            ddw8[...] = jnp.zeros_like(ddw8)

        cv, cg = cv_ref[...], cg_ref[...]
        u, vjp_u = jax.vjp(f_glu, cv, cg)
        uext[0:HALO, :] = jnp.where(m == 0, 0.0, f_glu(cvp_ref[...], cgp_ref[...]))
        uext[HALO:HALO + tm, :] = u
        _shift_stack(uext, ush, tm + HALO)
        _, vjp_c = jax.vjp(f_conv_act, yc_ref[...], g_ref[...], b_ref[...])
        dyc, dg, db = vjp_c(da_ref[...])
        _, vjp_n = jax.vjp(f_conv_act, ycn_ref[...], g_ref[...], b_ref[...])
        dyn = vjp_n(dan_ref[...])[0]
        dyext[0:tm, :] = dyc
        dyext[tm:tm + HALO, :] = jnp.where(m == nt - 1, 0.0, dyn)
        _shift_stack(dyext, dysh, tm + HALO)
        du = jnp.zeros((tm, D), F32)
        for w in range(KW):
            du = du + dw_ref[w:w + 1, :] * _shifted(dysh, KW - 1 - w, tm)
            prod = dyc * _shifted(ush, HALO - KW + 1 + w, tm)
            ddw8[w] += jnp.sum(prod.reshape(tm // 8, 8, D), axis=0)

        @pl.when(m == nt - 1)
        def _():
            ddw_ref[...] = jnp.sum(ddw8[...], axis=1)

        dcv, dcg = vjp_u(du)
        dp_ref[:, 0:D] = dcv.astype(BF16)
        dp_ref[:, D:2 * D] = dcg.astype(BF16)
        acc_ref[0:1, :] += jnp.sum(dyc, axis=0, keepdims=True)
        acc_ref[1:2, :] += dg
        acc_ref[2:3, :] += db
        acc_ref[3:4, :] += jnp.sum(dcv, axis=0, keepdims=True)
        acc_ref[4:5, :] += jnp.sum(dcg, axis=0, keepdims=True)
        pl.when(m == nt - 1)(finish)

    vec = pl.BlockSpec((1, D), lambda m: (0, 0))
    row = pl.BlockSpec((tm, D), lambda m: (m, 0))
    prev = lambda j: pl.BlockSpec((HALO, D), lambda m: (jnp.maximum(m * per - 1, 0), j))
    nxt = pl.BlockSpec((HALO, D), lambda m: (jnp.minimum((m + 1) * per, t // HALO - 1), 0))
    res = pl.pallas_call(
        body, name="conv_backward", grid=(nt,),
        out_shape=(jax.ShapeDtypeStruct(dp.shape, dp.dtype), jax.ShapeDtypeStruct((8, D), F32),
                   jax.ShapeDtypeStruct((HALO, D), F32)) + _exchanged(blocks),
        in_specs=[pl.BlockSpec((tm, D), lambda m: (m, 4)), pl.BlockSpec((tm, D), lambda m: (m, 5)), prev(4), prev(5),
                  row, nxt, row, nxt, pl.BlockSpec((HALO, D), lambda m: (0, 0)), vec, vec] + [ANYSPEC] * (n + 1),
        out_specs=(pl.BlockSpec((tm, 2 * D), lambda m: (m, 3)), pl.BlockSpec((8, D), lambda m: (0, 0)),
                   pl.BlockSpec((HALO, D), lambda m: (0, 0))) + (ANYSPEC,) * n,
        scratch_shapes=[pltpu.VMEM((HALO + tm, D), F32), pltpu.VMEM((tm + HALO, D), F32),
                        pltpu.VMEM((8, tm + HALO, D), F32), pltpu.VMEM((8, tm + HALO, D), F32),
                        pltpu.VMEM((HALO, 8, D), F32)] + _comm_sems(n, False),
        input_output_aliases={11 + n: 0},
        compiler_params=_params("arbitrary"),
    )(p, p, p, p, yc, yc, dact, dact, dw, ln_g, ln_b, *blocks, dp)
    return res[:3] + (res[3:],)


def hgrn_backward(p, logits, do, dqe, gt, st_prev, blocks, dp):
    t = p.shape[0]
    nc = t // CH
    n = len(blocks)

    def body(*refs):
        q_ref, f_ref, v_ref, l_ref, do_ref, dqe_ref, gt_ref, st_ref = refs[:8]
        dp_ref, acc_ref = refs[9 + n:11 + n]
        start, finish = _exchange_ops(refs[8:8 + n], refs[11 + n:11 + 2 * n], *refs[11 + 2 * n:])

        @pl.when(pl.program_id(0) == 0)
        def _():
            start()
            acc_ref[...] = jnp.zeros_like(acc_ref)

        gt_v = gt_ref[...]
        ddec = jnp.sum(gt_v * st_ref[...], axis=0, keepdims=True)
        _, vjp = jax.vjp(f_hgrn_chunk, q_ref[...], f_ref[...], v_ref[...], l_ref[...])
        dq, df, dv, dl = vjp((do_ref[...], dqe_ref[...], gt_v, ddec))
        dp_ref[:, 0:D] = dq.astype(BF16)
        dp_ref[:, D:2 * D] = df.astype(BF16)
        dp_ref[:, 2 * D:3 * D] = dv.astype(BF16)
        acc_ref[0:2, :] += dl
        acc_ref[2:3, :] += jnp.sum(dq, axis=0, keepdims=True)
        acc_ref[3:4, :] += jnp.sum(df, axis=0, keepdims=True)
        acc_ref[4:5, :] += jnp.sum(dv, axis=0, keepdims=True)
        pl.when(pl.program_id(0) == nc - 1)(finish)

    col = lambda j: pl.BlockSpec((CH, D), lambda c: (c, j))
    row = pl.BlockSpec((CH, D), lambda c: (c, 0))
    stspec = pl.BlockSpec((None, DK, D), lambda c: (c, 0, 0))
    res = pl.pallas_call(
        body, name="hgrn_backward", grid=(nc,),
        out_shape=(jax.ShapeDtypeStruct(dp.shape, dp.dtype), jax.ShapeDtypeStruct((8, D), F32)) + _exchanged(blocks),
        in_specs=[col(0), col(1), col(2), pl.BlockSpec((2, D), lambda c: (0, 0)), row, row, stspec, stspec]
        + [ANYSPEC] * (n + 1),
        out_specs=(pl.BlockSpec((CH, 3 * D), lambda c: (c, 1)), pl.BlockSpec((8, D), lambda c: (0, 0)))
        + (ANYSPEC,) * n,
        scratch_shapes=_comm_sems(n, False),
        input_output_aliases={8 + n: 0},
        compiler_params=_params("arbitrary"),
    )(p, p, p, logits, do, dqe, gt, st_prev, *blocks, dp)
    return res[:2] + (res[2:],)


def in_proj_backward(dp, w_all, x, dx1, gain, sc, sh, blocks):
    t = x.shape[0]
    tm = min(256, t)
    nt = t // tm
    n = len(blocks)

    def body(*refs):
        dp_ref, w_ref, x_ref, dx1_ref, g_ref, sc_ref, sh_ref = refs[:7]
        gx_ref, acc_ref = refs[7 + n:9 + n]
        start, finish = _exchange_ops(refs[7:7 + n], refs[9 + n:9 + 2 * n], *refs[9 + 2 * n:])

        @pl.when(pl.program_id(0) == 0)
        def _():
            start()
            acc_ref[...] = jnp.zeros_like(acc_ref)

        dh = jnp.zeros((tm, D), F32)
        for j in range(NDEV):
            dh = dh + _dot(dp_ref[:, j * D:(j + 1) * D], w_ref[DP_SPLIT[j]], _NT)
        _, vjp_h = jax.vjp(f_modulate, x_ref[...], g_ref[...], sc_ref[...], sh_ref[...])
        dx, dg, dsc, dsh = vjp_h(dh)
        gx_ref[...] = dx1_ref[...] + dx
        acc_ref[0:1, :] += dg
        acc_ref[1:2, :] += dsc
        acc_ref[2:3, :] += dsh
        pl.when(pl.program_id(0) == nt - 1)(finish)

    row = pl.BlockSpec((tm, D), lambda m: (m, 0))
    vec = pl.BlockSpec((1, D), lambda m: (0, 0))
    res = pl.pallas_call(
        body, name="in_proj_backward", grid=(nt,),
        out_shape=(jax.ShapeDtypeStruct((t, D), F32), jax.ShapeDtypeStruct((8, D), F32)) + _exchanged(blocks),
        in_specs=[pl.BlockSpec((tm, NDEV * D), lambda m: (m, 0)), VSPEC, row, row, vec, vec, vec] + [ANYSPEC] * n,
        out_specs=(row, pl.BlockSpec((8, D), lambda m: (0, 0))) + (ANYSPEC,) * n,
        scratch_shapes=_comm_sems(n, False),
        compiler_params=_params("arbitrary"),
    )(dp, w_all, x, dx1, gain, sc, sh, *blocks)
    return res[0], res[1], res[2:]


def _dp_column(n):
    return jnp.where(n < 3, n + 3, jnp.where(n == 3, 0, jnp.where(n < 6, n + 2, n - 5)))


def weight_grad(a, b, nblk, ka, bn, a_blocked, name, b_col=lambda n: n):
    t = a.shape[0]
    tk = min(512, t)
    nk = t // tk

    def body(a_ref, b_ref, f_ref, h_ref, acc):
        k = pl.program_id(1)

        @pl.when(k == 0)
        def _():
            acc[...] = jnp.zeros_like(acc)

        acc[...] += _dot(a_ref[...], b_ref[...], _TN)

        @pl.when(k == nk - 1)
        def _():
            f_ref[...] = acc[...]
            h_ref[...] = acc[...].astype(BF16)

    a_idx = (lambda n, k: (k, n)) if a_blocked else (lambda n, k: (k, 0))
    b_idx = (lambda n, k: (k, 0)) if a_blocked else (lambda n, k: (k, b_col(n)))
    out = pl.BlockSpec((None, ka, bn), lambda n, k: (n, 0, 0))
    return pl.pallas_call(
        body, name=name, grid=(nblk, nk),
        out_shape=(jax.ShapeDtypeStruct((nblk, ka, bn), F32), jax.ShapeDtypeStruct((nblk, ka, bn), BF16)),
        in_specs=[pl.BlockSpec((tk, ka), a_idx), pl.BlockSpec((tk, bn), b_idx)],
        out_specs=(out, out),
        scratch_shapes=[pltpu.VMEM((ka, bn), F32)],
        compiler_params=_params("parallel", "arbitrary"),
    )(a, b)


def ada_backward(call_t, dmod_cols, w, m, v):
    def body(c_ref, d_ref, w_ref, m_ref, v_ref, g_ref, dl_ref, nm_ref, nv_ref):
        ct, dm = c_ref[...], d_ref[...]
        g = ct[:, 0:1] * dm[0:1, :]
        for r in range(1, NDEV):
            g = g + ct[:, r:r + 1] * dm[r:r + 1, :]
        g_ref[...] = g
        dl_ref[...], nm_ref[...], nv_ref[...] = _adamw(w_ref[...], g, m_ref[...], v_ref[...])

    s = jax.ShapeDtypeStruct(w.shape, F32)
    return pl.pallas_call(
        body, name="ada_backward", out_shape=(s, s, s, s),
        in_specs=[VSPEC] * 5, out_specs=(VSPEC,) * 4, compiler_params=_params(),
    )(call_t, dmod_cols, w, m, v)


def adamw_small(total, ddw_mine, recipes, ws, ms, vs):
    n = len(ws)

    def body(*refs):
        tot, ddw = refs[0], refs[1]
        w_refs, m_refs, v_refs = refs[2:2 + n], refs[2 + n:2 + 2 * n], refs[2 + 2 * n:2 + 3 * n]
        outs = refs[2 + 3 * n:2 + 7 * n]
        loss_ref = refs[2 + 7 * n]
        for i, rec in enumerate(recipes):
            if rec == "dw":
                g = ddw[...]
            elif isinstance(rec, tuple):
                g = tot[rec[0]:rec[1], :]
            else:
                g = jnp.concatenate([tot[r:r + 1, :] for r in rec], axis=1) if len(rec) > 1 else tot[rec[0]:rec[0] + 1, :]
            dl, nm, nv = _adamw(w_refs[i][...], g, m_refs[i][...], v_refs[i][...])
            outs[4 * i][...] = g
            outs[4 * i + 1][...] = dl
            outs[4 * i + 2][...] = nm
            outs[4 * i + 3][...] = nv
        loss_ref[...] = tot[LOSS_ROW:LOSS_ROW + 1, 0:128]

    shapes = []
    for w in ws:
        shapes += [jax.ShapeDtypeStruct(w.shape, F32)] * 4
    res = pl.pallas_call(
        body, name="adamw_small", out_shape=tuple(shapes) + (jax.ShapeDtypeStruct((1, 128), F32),),
        in_specs=[VSPEC] * (2 + 3 * n), out_specs=(VSPEC,) * (4 * n + 1), compiler_params=_params(),
    )(total, ddw_mine, *ws, *ms, *vs)
    return [res[4 * i:4 * i + 4] for i in range(n)], res[4 * n]


def reduce_and_adamw(me, w, g_all, g_recv, m, v, name):
    r, c = w.shape
    br = min(128, r)

    def body(me_ref, w_ref, go_ref, gr_ref, m_ref, v_ref, g_ref, dl_ref, nm_ref, nv_ref):
        g = go_ref[...]
        for k in range(NDEV - 1):
            g = g + gr_ref[k].astype(F32)
        g_ref[...] = g
        dl_ref[...], nm_ref[...], nv_ref[...] = _adamw(w_ref[...], g, m_ref[...], v_ref[...])

    blk = pl.BlockSpec((br, c), lambda i, me_ref: (i, 0))
    s = jax.ShapeDtypeStruct(w.shape, F32)
    return pl.pallas_call(
        body, name=name, out_shape=(s, s, s, s),
        grid_spec=pltpu.PrefetchScalarGridSpec(
            num_scalar_prefetch=1, grid=(r // br,),
            in_specs=[blk, pl.BlockSpec((None, br, c), lambda i, me_ref: (me_ref[0], i, 0)),
                      pl.BlockSpec((NDEV - 1, br, c), lambda i, me_ref: (0, i, 0)), blk, blk],
            out_specs=(blk, blk, blk, blk)),
        compiler_params=_params("parallel"),
    )(me, w, g_all, g_recv, m, v)


def kernel(x, c, w_ada, b_ada, pre_norm_tm, post_norm_tm, pre_norm_cm, post_norm_cm, w_in, b_in, hg_lb_logits, hg_norm, conv_dw, conv_db, conv_ln_g, conv_ln_b, w_br_a, w_br_b, w_out, w_ff1, w_ff2, loss_target, m_w_ada, m_b_ada, m_pre_norm_tm, m_post_norm_tm, m_pre_norm_cm, m_post_norm_cm, m_w_in, m_b_in, m_hg_lb_logits, m_hg_norm, m_conv_dw, m_conv_db, m_conv_ln_g, m_conv_ln_b, m_w_br_a, m_w_br_b, m_w_out, m_w_ff1, m_w_ff2, v_w_ada, v_b_ada, v_pre_norm_tm, v_post_norm_tm, v_pre_norm_cm, v_post_norm_cm, v_w_in, v_b_in, v_hg_lb_logits, v_hg_norm, v_conv_dw, v_conv_db, v_conv_ln_g, v_conv_ln_b, v_w_br_a, v_w_br_b, v_w_out, v_w_ff1, v_w_ff2):
    t = x.shape[1]
    me = 4 * lax.axis_index("x") + 2 * lax.axis_index("y") + lax.axis_index("c")
    xs = x[0]
    tgt = loss_target[0]

    mod, call = ada_forward(c, w_ada[0], b_ada)
    sh1, sc1, g1, sh2, sc2, g2 = [mod[:, i * D:(i + 1) * D] for i in range(6)]
    win_all, dw_all = all_gather_blocks([w_in[0].astype(BF16), conv_dw[0]])
    dw = jnp.pad(dw_all.transpose(1, 0, 2).reshape(KW, D), ((0, HALO - KW), (0, 0)))

    p, h, (w1_all,) = in_proj(xs, pre_norm_tm, sc1, sh1, win_all, b_in, [w_ff1[0].astype(BF16)])
    intra, qe, ut, dec, (w2_all,) = hgrn_local(p, hg_lb_logits, [w_ff2[0].astype(BF16)])
    st_prev = state_scan(ut, dec, False)
    yc, act, (wa_all, wb_all, wo_all) = conv_forward(
        p, dw, conv_db, conv_ln_g, conv_ln_b, [w_br_a[0].astype(BF16), w_br_b[0].astype(BF16), w_out[0].astype(BF16)])
    wa, wb, wo = wa_all.reshape(D, D), wb_all.reshape(D, D), wo_all.reshape(D, D)
    x1 = mix_forward(xs, p, qe, intra, st_prev, act, wa, wb, wo, hg_norm, g1, post_norm_tm)

    dx1, h2, r, dz, dy2, acc_ffn = ffn_forward_backward(x1, tgt, w1_all, w2_all, pre_norm_cm, sc2, sh2, g2,
                                                        post_norm_cm)
    rows = D // NDEV
    g1_f, g1_h = weight_grad(h2, dz, NDEV, D, DFF // NDEV, False, "grad_w_ff1")
    g2_f, g2_h = weight_grad(r, dy2, 4, D, D, True, "grad_w_ff2")
    g2_f, g2_h = g2_f.reshape(NDEV, DFF // NDEV, D), g2_h.reshape(NDEV, DFF // NDEV, D)

    (dp, do, dqe, vt, dact, a16, mg16, dya, dyb, dy, acc_mix) = mix_backward(
        xs, p, qe, intra, st_prev, act, wa, wb, wo, hg_norm, g1, post_norm_tm, dx1)
    gt = state_scan(vt, dec, True)
    dp, acc_hg, (r_ff1, r_ff2) = hgrn_backward(p, hg_lb_logits, do, dqe, gt, st_prev, [g1_h, g2_h], dp)
    ga_f, ga_h = weight_grad(a16, dya, 1, D, D, False, "grad_w_br_a")
    gb_f, gb_h = weight_grad(act, dyb, 1, D, D, False, "grad_w_br_b")
    go_f, go_h = weight_grad(mg16, dy, 1, D, D, False, "grad_w_out")
    dp, acc_conv, ddw, (r_a, r_b, r_o) = conv_backward(
        p, yc, dact, dw, conv_ln_g, conv_ln_b,
        [ga_h.reshape(NDEV, rows, D), gb_h.reshape(NDEV, rows, D), go_h.reshape(NDEV, rows, D)], dp)
    gin_f, gin_h = weight_grad(h, dp, NDEV, D, D, False, "grad_w_in", _dp_column)
    grad_x, acc_in, (r_in,) = in_proj_backward(dp, win_all, xs, dx1, pre_norm_tm, sc1, sh1, [gin_h])

    own = [gin_f, ga_f.reshape(NDEV, rows, D), gb_f.reshape(NDEV, rows, D), go_f.reshape(NDEV, rows, D), g1_f, g2_f]
    recv = [r_in, r_a, r_b, r_o, r_ff1, r_ff2]
    big = {}
    names = ["w_in", "w_br_a", "w_br_b", "w_out", "w_ff1", "w_ff2"]
    ws = [w_in, w_br_a, w_br_b, w_out, w_ff1, w_ff2]
    ms = [m_w_in, m_w_br_a, m_w_br_b, m_w_out, m_w_ff1, m_w_ff2]
    vs = [v_w_in, v_w_br_a, v_w_br_b, v_w_out, v_w_ff1, v_w_ff2]
    me1 = me.astype(jnp.int32).reshape(1)
    for i, nm in enumerate(names):
        big[nm] = [o[None] for o in reduce_and_adamw(me1, ws[i][0], own[i], recv[i], ms[i][0], vs[i][0], "adamw_" + nm)]

    dmod_rows = [2, 1, 8, 19, 18, 20]
    total, kept = gather_and_sum_rows([acc_in, acc_mix, acc_ffn, acc_hg, acc_conv, ddw], dmod_rows)
    dmod_all = kept[:, 0:6, :].reshape(NDEV, 6 * D)
    wcols = w_ada.shape[2]
    gwa, dwa, nmwa, nvwa = ada_backward(call.T, lax.dynamic_slice_in_dim(dmod_all, me * wcols, wcols, axis=1),
                                        w_ada[0], m_w_ada[0], v_w_ada[0])
    ddw_mine = lax.dynamic_slice_in_dim(total[40:40 + KW], me * (D // NDEV), D // NDEV, axis=1)

    small_names = ["b_ada", "pre_norm_tm", "post_norm_tm", "pre_norm_cm", "post_norm_cm", "b_in", "hg_lb_logits",
                   "hg_norm", "conv_db", "conv_ln_g", "conv_ln_b", "conv_dw"]
    recipes = [dmod_rows, [0], [9], [16], [17], [26, 27, 28, 11, 35, 36, 12, 13], (24, 26), [10], [32], [33], [34],
               "dw"]
    small_w = [b_ada, pre_norm_tm, post_norm_tm, pre_norm_cm, post_norm_cm, b_in, hg_lb_logits, hg_norm, conv_db,
               conv_ln_g, conv_ln_b, conv_dw[0]]
    small_m = [m_b_ada, m_pre_norm_tm, m_post_norm_tm, m_pre_norm_cm, m_post_norm_cm, m_b_in, m_hg_lb_logits,
               m_hg_norm, m_conv_db, m_conv_ln_g, m_conv_ln_b, m_conv_dw[0]]
    small_v = [v_b_ada, v_pre_norm_tm, v_post_norm_tm, v_pre_norm_cm, v_post_norm_cm, v_b_in, v_hg_lb_logits,
               v_hg_norm, v_conv_db, v_conv_ln_g, v_conv_ln_b, v_conv_dw[0]]
    small_out, loss_row = adamw_small(total, ddw_mine, recipes, small_w, small_m, small_v)
    loss = loss_row[0, 0]
    sm = {nm: list(o) for nm, o in zip(small_names, small_out)}
    sm["conv_dw"] = [o[None] for o in sm["conv_dw"]]

    order = ["w_ada", "b_ada", "pre_norm_tm", "post_norm_tm", "pre_norm_cm", "post_norm_cm", "w_in", "b_in",
             "hg_lb_logits", "hg_norm", "conv_dw", "conv_db", "conv_ln_g", "conv_ln_b", "w_br_a", "w_br_b", "w_out",
             "w_ff1", "w_ff2"]
    res = dict(sm)
    res.update(big)
    res["w_ada"] = [gwa[None], dwa[None], nmwa[None], nvwa[None]]
    outs = [loss, grad_x[None]]
    for j in range(4):
        outs += [res[nm][j] for nm in order]
    return tuple(outs)
```

```python
import functools

import jax
import jax.numpy as jnp
from jax import lax
from jax.experimental import pallas as pl
from jax.experimental.pallas import tpu as pltpu

F32 = jnp.float32
BF16 = jnp.bfloat16
MESH = pl.DeviceIdType.MESH
AXES = ("x", "y", "c")

D = 1024
HEADS = 8
DK = 128
CH = 128
LEVELS = 7
KW = 31
HALO = 32
DFF = 4096
NDEV = 8
EPS = 1e-6
DP_SPLIT = (3, 6, 7, 0, 1, 2, 4, 5)
LOSS_ROW = 21
ADAM_LR, ADAM_B1, ADAM_B2, ADAM_EPS, ADAM_WD, ADAM_STEP = 0.001, 0.9, 0.999, 1e-08, 0.01, 10
VMEM_LIMIT = 58 * 1024 * 1024

_NN = (((1,), (0,)), ((), ()))
_NT = (((1,), (1,)), ((), ()))
_TN = (((0,), (0,)), ((), ()))

VSPEC = pl.BlockSpec(memory_space=pltpu.VMEM)
ANYSPEC = pl.BlockSpec(memory_space=pl.ANY)


def _params(*sem):
    return pltpu.CompilerParams(dimension_semantics=sem or None, vmem_limit_bytes=VMEM_LIMIT)


def _dot(a, b, dims):
    return lax.dot_general(a, b, dims, preferred_element_type=F32)


@jax.custom_vjp
def mm_nn(a, b):
    return _dot(a.astype(BF16), b.astype(BF16), _NN)


def _mm_nn_fwd(a, b):
    ab, bb = a.astype(BF16), b.astype(BF16)
    return _dot(ab, bb, _NN), (ab, bb)


def _mm_nn_bwd(res, ct):
    ab, bb = res
    cb = ct.astype(BF16)
    return _dot(cb, bb, _NT), _dot(ab, cb, _TN)


mm_nn.defvjp(_mm_nn_fwd, _mm_nn_bwd)


@jax.custom_vjp
def mm_nt(a, b):
    return _dot(a.astype(BF16), b.astype(BF16), _NT)


def _mm_nt_fwd(a, b):
    ab, bb = a.astype(BF16), b.astype(BF16)
    return _dot(ab, bb, _NT), (ab, bb)


def _mm_nt_bwd(res, ct):
    ab, bb = res
    cb = ct.astype(BF16)
    return _dot(cb, bb, _NN), _dot(cb, ab, _TN)


mm_nt.defvjp(_mm_nt_fwd, _mm_nt_bwd)


@jax.custom_vjp
def mm_tn(a, b):
    return _dot(a.astype(BF16), b.astype(BF16), _TN)


def _mm_tn_fwd(a, b):
    ab, bb = a.astype(BF16), b.astype(BF16)
    return _dot(ab, bb, _TN), (ab, bb)


def _mm_tn_bwd(res, ct):
    ab, bb = res
    cb = ct.astype(BF16)
    return _dot(bb, cb, _NT), _dot(ab, cb, _NN)


mm_tn.defvjp(_mm_tn_fwd, _mm_tn_bwd)


def _rms(x):
    return x * lax.rsqrt(jnp.mean(x * x, axis=-1, keepdims=True) + EPS)


def _silu(x):
    return x * jax.nn.sigmoid(x)


def f_modulate(x, gain, sc, sh):
    return _rms(x) * gain * (1.0 + sc) + sh


def f_residual(x, y, gate, gain):
    return x + gate * (_rms(y) * gain)


def f_merge(ga, gb, ya, yb):
    return jax.nn.sigmoid(ga) * ya + jax.nn.sigmoid(gb) * yb


def f_head_out(o, og, hg):
    heads = [_rms(o[:, h * DK:(h + 1) * DK]) for h in range(HEADS)]
    return jnp.concatenate(heads, axis=1) * hg * _silu(og)


def f_conv_act(u, g, b):
    mu = jnp.mean(u, axis=-1, keepdims=True)
    var = jnp.mean(jnp.square(u - mu), axis=-1, keepdims=True)
    return _silu((u - mu) * lax.rsqrt(var + EPS) * g + b)


def f_glu(cv, cg):
    return cv * jax.nn.sigmoid(cg)


def _split2(x):
    hi = x.astype(BF16)
    return hi, (x - hi.astype(F32)).astype(BF16)


def _tri(transposed):
    i = lax.broadcasted_iota(jnp.int32, (CH, CH), 1 if transposed else 0)
    t = lax.broadcasted_iota(jnp.int32, (CH, CH), 0 if transposed else 1)
    return jnp.where(t <= i, 1.0, 0.0).astype(BF16)


def _blocks3(x, rows):
    return x.reshape(CH // rows, rows, x.shape[-1])


def _mid_broadcast(b, lev):
    h = 1 << (LEVELS - 1 - lev)
    if h >= 4:
        x3 = _blocks3(b, 2 * h)
        return jnp.broadcast_to(x3[:, h - 1:h, :], x3.shape).reshape(b.shape)
    x3 = _blocks3(b, 8)
    sub = lax.broadcasted_iota(jnp.int32, x3.shape, 1)
    out = None
    for first in range(0, 8, 2 * h):
        piece = jnp.broadcast_to(x3[:, first + h - 1:first + h, :], x3.shape)
        out = piece if out is None else jnp.where(sub >= first, piece, out)
    return out.reshape(b.shape)


def _mid_scatter(d, lev):
    h = 1 << (LEVELS - 1 - lev)
    if h >= 4:
        x3 = _blocks3(d, 2 * h)
        row = lax.broadcasted_iota(jnp.int32, x3.shape, 1)
        total = jnp.sum(x3, axis=1, keepdims=True)
        return jnp.where(row == h - 1, total, 0.0).reshape(d.shape)
    x3 = _blocks3(d, 8)
    sub = lax.broadcasted_iota(jnp.int32, x3.shape, 1)
    out = jnp.zeros_like(x3)
    for first in range(0, 8, 2 * h):
        inside = (sub >= first) & (sub < first + 2 * h)
        total = jnp.sum(jnp.where(inside, x3, 0.0), axis=1, keepdims=True)
        out = jnp.where(sub == first + h - 1, total, out)
    return out.reshape(d.shape)


@jax.custom_vjp
def decay_sums(g):
    hi, lo = _split2(g)
    tri = _tri(False)
    b = _dot(tri, hi, _NN) + _dot(tri, lo, _NN)
    return (b,) + tuple(b - _mid_broadcast(b, lev) for lev in range(LEVELS))


def _decay_sums_fwd(g):
    return decay_sums(g), None


def _decay_sums_bwd(_, cts):
    db = cts[0]
    for lev in range(LEVELS):
        db = db + cts[1 + lev] - _mid_scatter(cts[1 + lev], lev)
    hi, lo = _split2(db)
    tri = _tri(True)
    return (_dot(tri, hi, _NN) + _dot(tri, lo, _NN),)


decay_sums.defvjp(_decay_sums_fwd, _decay_sums_bwd)


def _score_masks():
    i = lax.broadcasted_iota(jnp.int32, (CH, CH), 0)
    j = lax.broadcasted_iota(jnp.int32, (CH, CH), 1)
    masks = [i == j]
    for lev in range(LEVELS):
        sh = LEVELS - 1 - lev
        same = (i >> (sh + 1)) == (j >> (sh + 1))
        masks.append(same & (((i >> sh) & 1) == 1) & (((j >> sh) & 1) == 0))
    return [jnp.where(m, 1.0, 0.0) for m in masks]


def f_hgrn_chunk(q_r, f_r, v, logits):
    l0, l1 = logits[0:1, :], logits[1:2, :]
    mx = lax.stop_gradient(jnp.maximum(l0, l1))
    e0, e1 = jnp.exp(l0 - mx), jnp.exp(l1 - mx)
    lb = e0 / (e0 + e1)
    q = _silu(q_r)
    f = lb + (1.0 - lb) * jax.nn.sigmoid(f_r)
    k = 1.0 - f
    sums = decay_sums(jnp.log(f))
    b = sums[0]
    btot = b[CH - 1:CH, :]
    qe = q * jnp.exp(b)
    ke = k * jnp.exp(btot - b)
    dec = jnp.exp(btot)
    qs, ks = [q], [k]
    row = lax.broadcasted_iota(jnp.int32, b.shape, 0)
    for lev in range(LEVELS):
        upper = ((row >> (LEVELS - 1 - lev)) & 1) == 1
        e = sums[1 + lev]
        qs.append(q * jnp.exp(jnp.where(upper, e, 0.0)))
        ks.append(k * jnp.exp(jnp.where(upper, 0.0, -e)))
    masks = _score_masks()
    intra, ut = [], []
    for h in range(HEADS):
        sl = slice(h * DK, (h + 1) * DK)
        sc = None
        for lev in range(LEVELS + 1):
            s = mm_nt(qs[lev][:, sl], ks[lev][:, sl]) * masks[lev]
            sc = s if sc is None else sc + s
        intra.append(mm_nn(sc, v[:, sl]))
        ut.append(mm_tn(v[:, sl], ke[:, sl]))
    return jnp.concatenate(intra, axis=1), qe, jnp.concatenate(ut, axis=1), dec


def _inter(qe_b, st_ref, rows):
    out = []
    for ci in range(rows // CH):
        st = st_ref[ci].astype(BF16)
        heads = [_dot(qe_b[ci * CH:(ci + 1) * CH, h * DK:(h + 1) * DK], st[:, h * DK:(h + 1) * DK], _NT)
                 for h in range(HEADS)]
        out.append(jnp.concatenate(heads, axis=1))
    return jnp.concatenate(out, axis=0)


def _shift_stack(src, dst, rows):
    dst[0, 0:rows, :] = src[0:rows, :]
    for b in range(1, 8):
        dst[b, 0:rows - 8, :] = src[pl.ds(b, rows - 8), :]


def _shifted(stack, offset, rows):
    return stack[offset % 8, pl.ds(8 * (offset // 8), rows), :]


def _adamw(w, g, m, v):
    m = ADAM_B1 * m + (1.0 - ADAM_B1) * g
    v = ADAM_B2 * v + (1.0 - ADAM_B2) * jnp.square(g)
    m_hat = m / (1.0 - ADAM_B1 ** ADAM_STEP)
    v_hat = v / (1.0 - ADAM_B2 ** ADAM_STEP)
    delta = -ADAM_LR * (m_hat / (jnp.sqrt(v_hat) + ADAM_EPS) + ADAM_WD * w)
    return delta, m, v


def _me():
    return lax.axis_index("x"), lax.axis_index("y"), lax.axis_index("c")


def _peer(k):
    x, y, c = _me()
    mask = k + 1
    px = (1 - x) if (mask >> 2) & 1 else x
    py = (1 - y) if (mask >> 1) & 1 else y
    pc = (1 - c) if mask & 1 else c
    return (px, py, pc), 4 * px + 2 * py + pc


def ada_forward(c, w_ada, b_ada):
    wcols = w_ada.shape[1]

    def body(c_ref, w_ref, b_ref, mod_ref, call_ref, part_ref, modp_ref, send_sems, recv_sems):
        x, y, cc = _me()
        me = 4 * x + 2 * y + cc
        call_ref[me] = _silu(c_ref[...])
        sends = []
        for k in range(NDEV - 1):
            dev, _ = _peer(k)
            cp = pltpu.make_async_remote_copy(call_ref.at[me], call_ref.at[me], send_sems.at[k], recv_sems.at[k],
                                              device_id=dev, device_id_type=MESH)
            cp.start()
            sends.append(cp)
        for k in range(NDEV - 1):
            _, pidx = _peer(k)
            pltpu.make_async_remote_copy(call_ref.at[pidx], call_ref.at[pidx], send_sems.at[k], recv_sems.at[k],
                                         device_id=_peer(k)[0], device_id_type=MESH).wait_recv()
        call = jnp.concatenate([call_ref[r] for r in range(NDEV)], axis=0)
        part = _dot(call.astype(BF16), w_ref[...].astype(BF16), _NN)
        for r in range(NDEV):
            part_ref[r] = part[r:r + 1, :]
        modp_ref[me] = part_ref[me]
        for k in range(NDEV - 1):
            dev, pidx = _peer(k)
            cp = pltpu.make_async_remote_copy(part_ref.at[pidx], modp_ref.at[me], send_sems.at[NDEV - 1 + k],
                                              recv_sems.at[NDEV - 1 + k], device_id=dev, device_id_type=MESH)
            cp.start()
            sends.append(cp)
        for k in range(NDEV - 1):
            dev, pidx = _peer(k)
            pltpu.make_async_remote_copy(part_ref.at[pidx], modp_ref.at[pidx], send_sems.at[NDEV - 1 + k],
                                         recv_sems.at[NDEV - 1 + k], device_id=dev, device_id_type=MESH).wait_recv()
        for cp in sends:
            cp.wait_send()
        mod_ref[...] = jnp.concatenate([modp_ref[r] for r in range(NDEV)], axis=1) + b_ref[...]

    mod, call = pl.pallas_call(
        body, name="ada_forward",
        out_shape=(jax.ShapeDtypeStruct((1, NDEV * wcols), F32), jax.ShapeDtypeStruct((NDEV, 1, D), F32)),
        in_specs=[VSPEC, VSPEC, VSPEC], out_specs=(VSPEC, VSPEC),
        scratch_shapes=[pltpu.VMEM((NDEV, 1, wcols), F32), pltpu.VMEM((NDEV, 1, wcols), F32),
                        pltpu.SemaphoreType.DMA((2 * (NDEV - 1),)), pltpu.SemaphoreType.DMA((2 * (NDEV - 1),))],
        compiler_params=_params(),
    )(c, w_ada, b_ada)
    return mod, call.reshape(NDEV, D)


def _comm_sems(n, local):
    sems = [pltpu.SemaphoreType.DMA((7 * n,)), pltpu.SemaphoreType.DMA((7 * n,))]
    return sems + ([pltpu.SemaphoreType.DMA((n,))] if local else [])


def _gather2_ops(ins, outs, send_sems, recv_sems, local_sems):
    n = len(ins)
    x, y, c = _me()
    me, sibling = (x, y, c), (x, y, 1 - c)
    chips = [(1 - x, y), (x, 1 - y), (1 - x, 1 - y)]

    def slot(p):
        return 4 * p[0] + 2 * p[1] + p[2]

    def copy(a, k, block, to, src=None):
        return pltpu.make_async_remote_copy(
            src_ref=outs[a].at[slot(block)] if src is None else src, dst_ref=outs[a].at[slot(block)],
            send_sem=send_sems.at[a * 7 + k], recv_sem=recv_sems.at[a * 7 + k], device_id=to, device_id_type=MESH)

    def local(a):
        return pltpu.make_async_copy(ins[a], outs[a].at[slot(me)], local_sems.at[a])

    def first(a):
        return [copy(a, 0, me, sibling, src=ins[a])] + [copy(a, 1 + j, me, (*chip, c), src=ins[a])
                                                        for j, chip in enumerate(chips)]

    def passed(a):
        return [copy(a, 4 + j, (*chip, c), sibling) for j, chip in enumerate(chips)]

    def start():
        for a in range(n):
            local(a).start()
            for cp in first(a):
                cp.start()

    def forward():
        for j, chip in enumerate(chips):
            for a in range(n):
                copy(a, 1 + j, (*chip, c), me).wait_recv()
                passed(a)[j].start()

    def finish():
        for a in range(n):
            copy(a, 0, sibling, me).wait_recv()
            for j, chip in enumerate(chips):
                copy(a, 4 + j, (*chip, 1 - c), me).wait_recv()
        for a in range(n):
            for cp in first(a) + passed(a):
                cp.wait_send()
            local(a).wait()

    return start, forward, finish


def _gather1_ops(ins, outs, send_sems, recv_sems, local_sems):
    n = len(ins)
    x, y, c = _me()
    me = 4 * x + 2 * y + c

    def copy(a, k, block):
        return pltpu.make_async_remote_copy(ins[a], outs[a].at[block], send_sems.at[a * 7 + k],
                                            recv_sems.at[a * 7 + k], device_id=_peer(k)[0], device_id_type=MESH)

    def local(a):
        return pltpu.make_async_copy(ins[a], outs[a].at[me], local_sems.at[a])

    def start():
        for a in range(n):
            local(a).start()
            for k in range(NDEV - 1):
                copy(a, k, me).start()

    def finish():
        for a in range(n):
            for k in range(NDEV - 1):
                copy(a, k, _peer(k)[1]).wait_recv()
        for a in range(n):
            for k in range(NDEV - 1):
                copy(a, k, me).wait_send()
            local(a).wait()

    return start, finish


def _exchange_ops(ins, outs, send_sems, recv_sems):
    n = len(ins)

    def copy(a, k):
        dev, pidx = _peer(k)
        return pltpu.make_async_remote_copy(ins[a].at[pidx], outs[a].at[k], send_sems.at[a * 7 + k],
                                            recv_sems.at[a * 7 + k], device_id=dev, device_id_type=MESH)

    def start():
        for k in range(NDEV - 1):
            for a in range(n):
                copy(a, k).start()

    def finish():
        for k in range(NDEV - 1):
            for a in range(n):
                copy(a, k).wait_recv()
        for k in range(NDEV - 1):
            for a in range(n):
                copy(a, k).wait_send()

    return start, finish


def _gathered(shards):
    return tuple(jax.ShapeDtypeStruct((NDEV,) + s.shape, s.dtype) for s in shards)


def _exchanged(blocks):
    return tuple(jax.ShapeDtypeStruct((NDEV - 1,) + b.shape[1:], b.dtype) for b in blocks)


def exchange_start(blocks):
    landing = lax.empty((NDEV - 1,) + blocks.shape[1:], blocks.dtype)
    hbm = pl.BlockSpec(memory_space=pltpu.HBM)
    sem = pl.BlockSpec(memory_space=pltpu.SEMAPHORE)

    def body(src_ref, land_ref, send_sems, recv_sems, src_thru, land_thru, token):
        for k in range(NDEV - 1):
            dev, pidx = _peer(k)
            pltpu.make_async_remote_copy(src_ref.at[pidx], land_ref.at[k], send_sems.at[k], recv_sems.at[k],
                                         device_id=dev, device_id_type=MESH).start()
        token[...] = jnp.zeros_like(token)

    return pl.pallas_call(
        body, name="exchange_start",
        out_shape=(pltpu.SemaphoreType.DMA((NDEV - 1,)), pltpu.SemaphoreType.DMA((NDEV - 1,)),
                   pltpu.HBM(blocks.shape, blocks.dtype), pltpu.HBM(landing.shape, landing.dtype),
                   jax.ShapeDtypeStruct((8, 128), F32)),
        in_specs=(hbm, hbm), out_specs=(sem, sem, hbm, hbm, VSPEC), input_output_aliases={0: 2, 1: 3},
        compiler_params=pltpu.CompilerParams(has_side_effects=pltpu.SideEffectType.DATAFLOW_SIDE_EFFECTING),
    )(pltpu.with_memory_space_constraint(blocks, pltpu.HBM), pltpu.with_memory_space_constraint(landing, pltpu.HBM))


def exchange_wait(send_sems, recv_sems, src_thru, land_thru, after):
    hbm = pl.BlockSpec(memory_space=pltpu.HBM)
    sem = pl.BlockSpec(memory_space=pltpu.SEMAPHORE)

    def body(src_ref, land_ref, send_sems, recv_sems, after_ref, src_dead, got_ref):
        for k in range(NDEV - 1):
            dev, pidx = _peer(k)
            cp = pltpu.make_async_remote_copy(src_ref.at[pidx], land_ref.at[k], send_sems.at[k], recv_sems.at[k],
                                              device_id=dev, device_id_type=MESH)
            cp.wait_send()
            cp.wait_recv()

    return pl.pallas_call(
        body, name="exchange_wait",
        out_shape=(pltpu.HBM(src_thru.shape, src_thru.dtype), pltpu.HBM(land_thru.shape, land_thru.dtype)),
        in_specs=(hbm, hbm, sem, sem, ANYSPEC), out_specs=(hbm, hbm), input_output_aliases={0: 0, 1: 1},
        compiler_params=pltpu.CompilerParams(has_side_effects=pltpu.SideEffectType.DATAFLOW_SIDE_EFFECTING),
    )(src_thru, land_thru, send_sems, recv_sems, after)[1]


def all_gather_blocks(shards):
    n = len(shards)

    def body(*refs):
        start, forward, finish = _gather2_ops(refs[:n], refs[n:2 * n], *refs[2 * n:])
        start()
        forward()
        finish()

    return pl.pallas_call(
        body, name="all_gather_blocks", out_shape=_gathered(shards),
        in_specs=[ANYSPEC] * n, out_specs=tuple([ANYSPEC] * n),
        scratch_shapes=_comm_sems(n, True), compiler_params=_params(),
    )(*shards)


def gather_and_sum_rows(parts, keep_rows):
    n = len(parts)
    offs = [sum(p.shape[0] for p in parts[:i]) for i in range(n)]
    rows = sum(p.shape[0] for p in parts)

    def body(*refs):
        sum_ref, keep_ref, buf_ref, send_sems, recv_sems = refs[n:]
        x, y, c = _me()
        me = 4 * x + 2 * y + c
        for i in range(n):
            buf_ref[me, offs[i]:offs[i] + parts[i].shape[0], :] = refs[i][...]

        def copy(k, block):
            return pltpu.make_async_remote_copy(buf_ref.at[block], buf_ref.at[block], send_sems.at[k],
                                                recv_sems.at[k], device_id=_peer(k)[0], device_id_type=MESH)

        for k in range(NDEV - 1):
            copy(k, me).start()
        for k in range(NDEV - 1):
            copy(k, _peer(k)[1]).wait_recv()
        for k in range(NDEV - 1):
            copy(k, me).wait_send()
        total = buf_ref[0]
        for r in range(1, NDEV):
            total = total + buf_ref[r]
        sum_ref[...] = total
        keep_ref[...] = jnp.zeros_like(keep_ref)
        for r in range(NDEV):
            for j, src in enumerate(keep_rows):
                keep_ref[r, j:j + 1, :] = buf_ref[r, src:src + 1, :]

    return pl.pallas_call(
        body, name="gather_and_sum_rows",
        out_shape=(jax.ShapeDtypeStruct((rows, D), F32), jax.ShapeDtypeStruct((NDEV, 8, D), F32)),
        in_specs=[VSPEC] * n, out_specs=(VSPEC, VSPEC),
        scratch_shapes=[pltpu.VMEM((NDEV, rows, D), F32), pltpu.SemaphoreType.DMA((NDEV - 1,)),
                        pltpu.SemaphoreType.DMA((NDEV - 1,))],
        compiler_params=_params(),
    )(*parts)


def in_proj(x, gain, sc, sh, w_all, b_in, shards):
    t = x.shape[0]
    tm = min(256, t)
    n = len(shards)
    steps = t // tm

    def body(*refs):
        x_ref, g_ref, sc_ref, sh_ref, w_ref, b_ref = refs[:6]
        p_ref, h_ref = refs[6 + n:8 + n]
        start, forward, finish = _gather2_ops(refs[6:6 + n], refs[8 + n:8 + 2 * n], *refs[8 + 2 * n:])
        step = pl.program_id(0)
        pl.when(step == 0)(start)
        h = f_modulate(x_ref[...], g_ref[...], sc_ref[...], sh_ref[...]).astype(BF16)
        h_ref[...] = h
        for j in range(NDEV):
            p_ref[:, j * D:(j + 1) * D] = _dot(h, w_ref[j], _NN) + b_ref[:, j * D:(j + 1) * D]
        pl.when(step == (3 * steps) // 4)(forward)
        pl.when(step == steps - 1)(finish)

    vec = pl.BlockSpec((1, D), lambda m: (0, 0))
    res = pl.pallas_call(
        body, name="in_proj", grid=(steps,),
        out_shape=(jax.ShapeDtypeStruct((t, NDEV * D), F32), jax.ShapeDtypeStruct((t, D), BF16)) + _gathered(shards),
        in_specs=[pl.BlockSpec((tm, D), lambda m: (m, 0)), vec, vec, vec, VSPEC,
                  pl.BlockSpec((1, NDEV * D), lambda m: (0, 0))] + [ANYSPEC] * n,
        out_specs=(pl.BlockSpec((tm, NDEV * D), lambda m: (m, 0)), pl.BlockSpec((tm, D), lambda m: (m, 0)))
        + (ANYSPEC,) * n,
        scratch_shapes=_comm_sems(n, True),
        compiler_params=_params("arbitrary"),
    )(x, gain, sc, sh, w_all, b_in, *shards)
    return res[0], res[1], res[2:]


def hgrn_local(p, logits, shards):
    t = p.shape[0]
    nc = t // CH
    n = len(shards)

    def body(*refs):
        q_ref, f_ref, v_ref, l_ref = refs[:4]
        intra_ref, qe_ref, ut_ref, dec_ref = refs[4 + n:8 + n]
        start, forward, finish = _gather2_ops(refs[4:4 + n], refs[8 + n:8 + 2 * n], *refs[8 + 2 * n:])
        step = pl.program_id(0)
        pl.when(step == 0)(start)
        intra, qe, ut, dec = f_hgrn_chunk(q_ref[...], f_ref[...], v_ref[...], l_ref[...])
        intra_ref[...] = intra
        qe_ref[...] = qe.astype(BF16)
        ut_ref[...] = ut
        dec_ref[...] = dec
        pl.when(step == (3 * nc) // 4)(forward)
        pl.when(step == nc - 1)(finish)

    col = lambda j: pl.BlockSpec((CH, D), lambda c: (c, j))
    res = pl.pallas_call(
        body, name="hgrn_local", grid=(nc,),
        out_shape=(jax.ShapeDtypeStruct((t, D), F32), jax.ShapeDtypeStruct((t, D), BF16),
                   jax.ShapeDtypeStruct((nc, DK, D), F32), jax.ShapeDtypeStruct((nc, 1, D), F32)) + _gathered(shards),
        in_specs=[col(0), col(1), col(2), pl.BlockSpec((2, D), lambda c: (0, 0))] + [ANYSPEC] * n,
        out_specs=(pl.BlockSpec((CH, D), lambda c: (c, 0)), pl.BlockSpec((CH, D), lambda c: (c, 0)),
                   pl.BlockSpec((None, DK, D), lambda c: (c, 0, 0)), pl.BlockSpec((None, 1, D), lambda c: (c, 0, 0)))
        + (ANYSPEC,) * n,
        scratch_shapes=_comm_sems(n, True),
        compiler_params=_params("arbitrary"),
    )(p, p, p, logits, *shards)
    return res[:4] + (res[4:],)


def state_scan(u, dec, reverse):
    nc = u.shape[0]

    def body(u_ref, d_ref, out_ref, st):
        @pl.when(pl.program_id(0) == 0)
        def _():
            st[...] = jnp.zeros_like(st)

        s = st[...]
        out_ref[...] = s
        st[...] = d_ref[...] * s + u_ref[...]

    idx = (lambda i: (nc - 1 - i, 0, 0)) if reverse else (lambda i: (i, 0, 0))
    return pl.pallas_call(
        body, name="state_scan_rev" if reverse else "state_scan", grid=(nc,),
        out_shape=jax.ShapeDtypeStruct((nc, DK, D), F32),
        in_specs=[pl.BlockSpec((None, DK, D), idx), pl.BlockSpec((None, 1, D), idx)],
        out_specs=pl.BlockSpec((None, DK, D), idx),
        scratch_shapes=[pltpu.VMEM((DK, D), F32)],
        compiler_params=_params("arbitrary"),
    )(u, dec)


def _conv_tile(t):
    return min(256, t)


def conv_forward(p, dw, db, ln_g, ln_b, shards):
    t = p.shape[0]
    tm = _conv_tile(t)
    per = tm // HALO
    n = len(shards)
    nt = t // tm

    def body(*refs):
        cv_ref, cg_ref, cvp_ref, cgp_ref, dw_ref, db_ref, g_ref, b_ref = refs[:8]
        yc_ref, act_ref = refs[8 + n:10 + n]
        uext, ush = refs[10 + 2 * n:12 + 2 * n]
        start, finish = _gather1_ops(refs[8:8 + n], refs[10 + n:10 + 2 * n], *refs[12 + 2 * n:])
        step = pl.program_id(0)
        pl.when(step == 0)(start)
        uext[0:HALO, :] = jnp.where(step == 0, 0.0, f_glu(cvp_ref[...], cgp_ref[...]))
        uext[HALO:HALO + tm, :] = f_glu(cv_ref[...], cg_ref[...])
        _shift_stack(uext, ush, tm + HALO)
        acc = jnp.zeros((tm, D), F32) + db_ref[...]
        for w in range(KW):
            acc = acc + dw_ref[w:w + 1, :] * _shifted(ush, HALO - KW + 1 + w, tm)
        yc_ref[...] = acc
        act_ref[...] = f_conv_act(acc, g_ref[...], b_ref[...]).astype(BF16)
        pl.when(step == nt - 1)(finish)

    vec = pl.BlockSpec((1, D), lambda m: (0, 0))
    prev = lambda j: pl.BlockSpec((HALO, D), lambda m: (jnp.maximum(m * per - 1, 0), j))
    res = pl.pallas_call(
        body, name="conv_forward", grid=(nt,),
        out_shape=(jax.ShapeDtypeStruct((t, D), F32), jax.ShapeDtypeStruct((t, D), BF16)) + _gathered(shards),
        in_specs=[pl.BlockSpec((tm, D), lambda m: (m, 4)), pl.BlockSpec((tm, D), lambda m: (m, 5)), prev(4), prev(5),
                  pl.BlockSpec((HALO, D), lambda m: (0, 0)), vec, vec, vec] + [ANYSPEC] * n,
        out_specs=(pl.BlockSpec((tm, D), lambda m: (m, 0)), pl.BlockSpec((tm, D), lambda m: (m, 0))) + (ANYSPEC,) * n,
        scratch_shapes=[pltpu.VMEM((HALO + tm, D), F32), pltpu.VMEM((8, tm + HALO, D), F32)] + _comm_sems(n, True),
        compiler_params=_params("arbitrary"),
    )(p, p, p, p, dw, db, ln_g, ln_b, *shards)
    return res[0], res[1], res[2:]


def _mix_tile(t):
    return min(256, t)


def _mix_forward_tile(x_ref, og_ref, ga_ref, gb_ref, qe_ref, intra_ref, st_ref, act_ref, wa_ref, wb_ref, wo_ref,
                      hg_ref, rows):
    o = _inter(qe_ref[...], st_ref, rows) + intra_ref[...]
    a = f_head_out(o, og_ref[...], hg_ref[...])
    ya = _dot(a.astype(BF16), wa_ref[...], _NN)
    yb = _dot(act_ref[...], wb_ref[...], _NN)
    merged = f_merge(ga_ref[...], gb_ref[...], ya, yb)
    y = _dot(merged.astype(BF16), wo_ref[...], _NN)
    return o, a, ya, yb, merged, y


def _mix_specs(tm):
    col = lambda j: pl.BlockSpec((tm, D), lambda m: (m, j))
    row = pl.BlockSpec((tm, D), lambda m: (m, 0))
    return col, row, pl.BlockSpec((1, D), lambda m: (0, 0)), pl.BlockSpec((tm // CH, DK, D), lambda m: (m, 0, 0))


def mix_forward(x, p, qe, intra, st_prev, act, wa, wb, wo, hg, g1, post):
    t = x.shape[0]
    tm = _mix_tile(t)

    def body(x_ref, og_ref, ga_ref, gb_ref, qe_ref, intra_ref, st_ref, act_ref, wa_ref, wb_ref, wo_ref, hg_ref,
             g1_ref, post_ref, x1_ref):
        y = _mix_forward_tile(x_ref, og_ref, ga_ref, gb_ref, qe_ref, intra_ref, st_ref, act_ref, wa_ref, wb_ref,
                              wo_ref, hg_ref, tm)[-1]
        x1_ref[...] = f_residual(x_ref[...], y, g1_ref[...], post_ref[...])

    col, row, vec, stspec = _mix_specs(tm)
    return pl.pallas_call(
        body, name="mix_forward", grid=(t // tm,),
        out_shape=jax.ShapeDtypeStruct((t, D), F32),
        in_specs=[row, col(3), col(6), col(7), row, row, stspec, row, VSPEC, VSPEC, VSPEC, vec, vec, vec],
        out_specs=row,
        compiler_params=_params("parallel"),
    )(x, p, p, p, qe, intra, st_prev, act, wa, wb, wo, hg, g1, post)


def ffn_forward_backward(x1, target, w1, w2, pre, sc, sh, g2, post):
    t = x1.shape[0]
    tm = min(256, t)
    nb = w1.shape[0]
    fb = w1.shape[2]

    def body(x_ref, tg_ref, w1_ref, w2_ref, pre_ref, sc_ref, sh_ref, g2_ref, post_ref,
             dx_ref, h2_ref, r_ref, dz_ref, dy2_ref, acc_ref, z_sc):
        @pl.when(pl.program_id(0) == 0)
        def _():
            acc_ref[...] = jnp.zeros_like(acc_ref)

        x1v = x_ref[...]
        h2, vjp_h = jax.vjp(f_modulate, x1v, pre_ref[...], sc_ref[...], sh_ref[...])
        h2b = h2.astype(BF16)
        h2_ref[...] = h2b
        y2 = jnp.zeros((tm, D), F32)
        for n in range(nb):
            z = _dot(h2b, w1_ref[n], _NN)
            z_sc[:, n * fb:(n + 1) * fb] = z
            r = jnp.square(jnp.maximum(z, 0.0)).astype(BF16)
            r_ref[:, n * fb:(n + 1) * fb] = r
            y2 = y2 + _dot(r, w2_ref[n], _NN)
        out, vjp_r = jax.vjp(f_residual, x1v, y2, g2_ref[...], post_ref[...])
        err = out - tg_ref[...]
        tok = jnp.mean(jnp.square(err), axis=-1, keepdims=True)
        acc_ref[5:6, :] += 0.5 * jnp.sum(tok, axis=0, keepdims=True)
        dx_a, dy2, dg2, dpost = vjp_r(err * (1.0 / D))
        dy2b = dy2.astype(BF16)
        dy2_ref[...] = dy2b
        dh2 = jnp.zeros((tm, D), F32)
        for n in range(nb):
            dr = _dot(dy2b, w2_ref[n], _NT)
            dz = (dr * (2.0 * jnp.maximum(z_sc[:, n * fb:(n + 1) * fb], 0.0))).astype(BF16)
            dz_ref[:, n * fb:(n + 1) * fb] = dz
            dh2 = dh2 + _dot(dz, w1_ref[n], _NT)
        dx_b, dpre, dsc, dsh = vjp_h(dh2)
        dx_ref[...] = dx_a + dx_b
        acc_ref[0:1, :] += dpre
        acc_ref[1:2, :] += dpost
        acc_ref[2:3, :] += dsc
        acc_ref[3:4, :] += dsh
        acc_ref[4:5, :] += dg2

    row = pl.BlockSpec((tm, D), lambda m: (m, 0))
    wide = pl.BlockSpec((tm, DFF), lambda m: (m, 0))
    vec = pl.BlockSpec((1, D), lambda m: (0, 0))
    return pl.pallas_call(
        body, name="ffn_forward_backward", grid=(t // tm,),
        out_shape=(jax.ShapeDtypeStruct((t, D), F32), jax.ShapeDtypeStruct((t, D), BF16),
                   jax.ShapeDtypeStruct((t, DFF), BF16), jax.ShapeDtypeStruct((t, DFF), BF16),
                   jax.ShapeDtypeStruct((t, D), BF16), jax.ShapeDtypeStruct((8, D), F32)),
        in_specs=[row, row, VSPEC, VSPEC, vec, vec, vec, vec, vec],
        out_specs=(row, row, wide, wide, row, pl.BlockSpec((8, D), lambda m: (0, 0))),
        scratch_shapes=[pltpu.VMEM((tm, DFF), F32)],
        compiler_params=_params("arbitrary"),
    )(x1, target, w1, w2, pre, sc, sh, g2, post)


def mix_backward(x, p, qe, intra, st_prev, act, wa, wb, wo, hg, g1, post, dx1):
    t = x.shape[0]
    tm = _mix_tile(t)
    nc = t // CH

    def body(x_ref, og_ref, ga_ref, gb_ref, qe_ref, intra_ref, st_ref, act_ref, wa_ref, wb_ref, wo_ref, hg_ref,
             g1_ref, post_ref, dx1_ref,
             dp_ref, do_ref, dqe_ref, vt_ref, dact_ref, a_ref, mg_ref, dya_ref, dyb_ref, dy_ref, acc_ref):
        @pl.when(pl.program_id(0) == 0)
        def _():
            acc_ref[...] = jnp.zeros_like(acc_ref)

        o, a, ya, yb, merged, y = _mix_forward_tile(x_ref, og_ref, ga_ref, gb_ref, qe_ref, intra_ref, st_ref, act_ref,
                                                    wa_ref, wb_ref, wo_ref, hg_ref, tm)
        a_ref[...] = a.astype(BF16)
        mg_ref[...] = merged.astype(BF16)
        _, vjp_r = jax.vjp(f_residual, x_ref[...], y, g1_ref[...], post_ref[...])
        _, dy, dg1, dpost = vjp_r(dx1_ref[...])
        dyb16 = dy.astype(BF16)
        dy_ref[...] = dyb16
        dmerged = _dot(dyb16, wo_ref[...], _NT)
        _, vjp_m = jax.vjp(f_merge, ga_ref[...], gb_ref[...], ya, yb)
        dga, dgb, dya, dyb = vjp_m(dmerged)
        dp_ref[:, D:2 * D] = dga.astype(BF16)
        dp_ref[:, 2 * D:3 * D] = dgb.astype(BF16)
        dya16, dyb16b = dya.astype(BF16), dyb.astype(BF16)
        dya_ref[...] = dya16
        dyb_ref[...] = dyb16b
        da = _dot(dya16, wa_ref[...], _NT)
        dact_ref[...] = _dot(dyb16b, wb_ref[...], _NT)
        _, vjp_a = jax.vjp(f_head_out, o, og_ref[...], hg_ref[...])
        do, dog, dhg = vjp_a(da)
        dp_ref[:, 0:D] = dog.astype(BF16)
        do_ref[...] = do
        do16 = do.astype(BF16)
        qe16 = qe_ref[...]
        for ci in range(tm // CH):
            st = st_ref[ci].astype(BF16)
            rows = slice(ci * CH, (ci + 1) * CH)
            dqe, vt = [], []
            for h in range(HEADS):
                sl = slice(h * DK, (h + 1) * DK)
                dqe.append(_dot(do16[rows, sl], st[:, sl], _NN))
                vt.append(_dot(do16[rows, sl], qe16[rows, sl], _TN))
            dqe_ref[rows, :] = jnp.concatenate(dqe, axis=1)
            vt_ref[ci] = jnp.concatenate(vt, axis=1)
        acc_ref[0:1, :] += dg1
        acc_ref[1:2, :] += dpost
        acc_ref[2:3, :] += dhg
        acc_ref[3:4, :] += jnp.sum(dog, axis=0, keepdims=True)
        acc_ref[4:5, :] += jnp.sum(dga, axis=0, keepdims=True)
        acc_ref[5:6, :] += jnp.sum(dgb, axis=0, keepdims=True)

    col, row, vec, stspec = _mix_specs(tm)
    b16 = jax.ShapeDtypeStruct((t, D), BF16)
    f32 = jax.ShapeDtypeStruct((t, D), F32)
    return pl.pallas_call(
        body, name="mix_backward", grid=(t // tm,),
        out_shape=(jax.ShapeDtypeStruct((t, NDEV * D), BF16), f32, f32, jax.ShapeDtypeStruct((nc, DK, D), F32), f32,
                   b16, b16, b16, b16, b16, jax.ShapeDtypeStruct((8, D), F32)),
        in_specs=[row, col(3), col(6), col(7), row, row, stspec, row, VSPEC, VSPEC, VSPEC, vec, vec, vec, row],
        out_specs=(pl.BlockSpec((tm, 3 * D), lambda m: (m, 0)), row, row, stspec, row, row, row, row, row, row,
                   pl.BlockSpec((8, D), lambda m: (0, 0))),
        compiler_params=_params("arbitrary"),
    )(x, p, p, p, qe, intra, st_prev, act, wa, wb, wo, hg, g1, post, dx1)


def conv_backward(p, yc, dact, dw, ln_g, ln_b, blocks, dp):
    t = p.shape[0]
    tm = _conv_tile(t)
    per = tm // HALO
    nt = t // tm
    n = len(blocks)

    def body(*refs):
        cv_ref, cg_ref, cvp_ref, cgp_ref, yc_ref, ycn_ref, da_ref, dan_ref, dw_ref, g_ref, b_ref = refs[:11]
        dp_ref, acc_ref, ddw_ref = refs[12 + n:15 + n]
        uext, dyext, ush, dysh, ddw8 = refs[15 + 2 * n:20 + 2 * n]
        start, finish = _exchange_ops(refs[11:11 + n], refs[15 + n:15 + 2 * n], *refs[20 + 2 * n:])
        m = pl.program_id(0)

        @pl.when(m == 0)
        def _():
            start()
            acc_ref[...] = jnp.zeros_like(acc_ref)
            ddw8[...] = jnp.zeros_like(ddw8)

        cv, cg = cv_ref[...], cg_ref[...]
        u, vjp_u = jax.vjp(f_glu, cv, cg)
        uext[0:HALO, :] = jnp.where(m == 0, 0.0, f_glu(cvp_ref[...], cgp_ref[...]))
        uext[HALO:HALO + tm, :] = u
        _shift_stack(uext, ush, tm + HALO)
        _, vjp_c = jax.vjp(f_conv_act, yc_ref[...], g_ref[...], b_ref[...])
        dyc, dg, db = vjp_c(da_ref[...])
        _, vjp_n = jax.vjp(f_conv_act, ycn_ref[...], g_ref[...], b_ref[...])
        dyn = vjp_n(dan_ref[...])[0]
        dyext[0:tm, :] = dyc
        dyext[tm:tm + HALO, :] = jnp.where(m == nt - 1, 0.0, dyn)
        _shift_stack(dyext, dysh, tm + HALO)
        du = jnp.zeros((tm, D), F32)
        for w in range(KW):
            du = du + dw_ref[w:w + 1, :] * _shifted(dysh, KW - 1 - w, tm)
            prod = dyc * _shifted(ush, HALO - KW + 1 + w, tm)
            ddw8[w] += jnp.sum(prod.reshape(tm // 8, 8, D), axis=0)

        @pl.when(m == nt - 1)
        def _():
            ddw_ref[...] = jnp.sum(ddw8[...], axis=1)

        dcv, dcg = vjp_u(du)
        dp_ref[:, 0:D] = dcv.astype(BF16)
        dp_ref[:, D:2 * D] = dcg.astype(BF16)
        acc_ref[0:1, :] += jnp.sum(dyc, axis=0, keepdims=True)
        acc_ref[1:2, :] += dg
        acc_ref[2:3, :] += db
        acc_ref[3:4, :] += jnp.sum(dcv, axis=0, keepdims=True)
        acc_ref[4:5, :] += jnp.sum(dcg, axis=0, keepdims=True)
        pl.when(m == nt - 1)(finish)

    vec = pl.BlockSpec((1, D), lambda m: (0, 0))
    row = pl.BlockSpec((tm, D), lambda m: (m, 0))
    prev = lambda j: pl.BlockSpec((HALO, D), lambda m: (jnp.maximum(m * per - 1, 0), j))
    nxt = pl.BlockSpec((HALO, D), lambda m: (jnp.minimum((m + 1) * per, t // HALO - 1), 0))
    res = pl.pallas_call(
        body, name="conv_backward", grid=(nt,),
        out_shape=(jax.ShapeDtypeStruct(dp.shape, dp.dtype), jax.ShapeDtypeStruct((8, D), F32),
                   jax.ShapeDtypeStruct((HALO, D), F32)) + _exchanged(blocks),
        in_specs=[pl.BlockSpec((tm, D), lambda m: (m, 4)), pl.BlockSpec((tm, D), lambda m: (m, 5)), prev(4), prev(5),
                  row, nxt, row, nxt, pl.BlockSpec((HALO, D), lambda m: (0, 0)), vec, vec] + [ANYSPEC] * (n + 1),
        out_specs=(pl.BlockSpec((tm, 2 * D), lambda m: (m, 3)), pl.BlockSpec((8, D), lambda m: (0, 0)),
                   pl.BlockSpec((HALO, D), lambda m: (0, 0))) + (ANYSPEC,) * n,
        scratch_shapes=[pltpu.VMEM((HALO + tm, D), F32), pltpu.VMEM((tm + HALO, D), F32),
                        pltpu.VMEM((8, tm + HALO, D), F32), pltpu.VMEM((8, tm + HALO, D), F32),
                        pltpu.VMEM((HALO, 8, D), F32)] + _comm_sems(n, False),
        input_output_aliases={11 + n: 0},
        compiler_params=_params("arbitrary"),
    )(p, p, p, p, yc, yc, dact, dact, dw, ln_g, ln_b, *blocks, dp)
    return res[:3] + (res[3:],)


def hgrn_backward(p, logits, do, dqe, gt, st_prev, blocks, dp):
    t = p.shape[0]
    nc = t // CH
    n = len(blocks)

    def body(*refs):
        q_ref, f_ref, v_ref, l_ref, do_ref, dqe_ref, gt_ref, st_ref = refs[:8]
        dp_ref, acc_ref = refs[9 + n:11 + n]
        start, finish = _exchange_ops(refs[8:8 + n], refs[11 + n:11 + 2 * n], *refs[11 + 2 * n:])

        @pl.when(pl.program_id(0) == 0)
        def _():
            start()
            acc_ref[...] = jnp.zeros_like(acc_ref)

        gt_v = gt_ref[...]
        ddec = jnp.sum(gt_v * st_ref[...], axis=0, keepdims=True)
        _, vjp = jax.vjp(f_hgrn_chunk, q_ref[...], f_ref[...], v_ref[...], l_ref[...])
        dq, df, dv, dl = vjp((do_ref[...], dqe_ref[...], gt_v, ddec))
        dp_ref[:, 0:D] = dq.astype(BF16)
        dp_ref[:, D:2 * D] = df.astype(BF16)
        dp_ref[:, 2 * D:3 * D] = dv.astype(BF16)
        acc_ref[0:2, :] += dl
        acc_ref[2:3, :] += jnp.sum(dq, axis=0, keepdims=True)
        acc_ref[3:4, :] += jnp.sum(df, axis=0, keepdims=True)
        acc_ref[4:5, :] += jnp.sum(dv, axis=0, keepdims=True)
        pl.when(pl.program_id(0) == nc - 1)(finish)

    col = lambda j: pl.BlockSpec((CH, D), lambda c: (c, j))
    row = pl.BlockSpec((CH, D), lambda c: (c, 0))
    stspec = pl.BlockSpec((None, DK, D), lambda c: (c, 0, 0))
    res = pl.pallas_call(
        body, name="hgrn_backward", grid=(nc,),
        out_shape=(jax.ShapeDtypeStruct(dp.shape, dp.dtype), jax.ShapeDtypeStruct((8, D), F32)) + _exchanged(blocks),
        in_specs=[col(0), col(1), col(2), pl.BlockSpec((2, D), lambda c: (0, 0)), row, row, stspec, stspec]
        + [ANYSPEC] * (n + 1),
        out_specs=(pl.BlockSpec((CH, 3 * D), lambda c: (c, 1)), pl.BlockSpec((8, D), lambda c: (0, 0)))
        + (ANYSPEC,) * n,
        scratch_shapes=_comm_sems(n, False),
        input_output_aliases={8 + n: 0},
        compiler_params=_params("arbitrary"),
    )(p, p, p, logits, do, dqe, gt, st_prev, *blocks, dp)
    return res[:2] + (res[2:],)


def in_proj_backward(dp, w_all, x, dx1, gain, sc, sh, blocks):
    t = x.shape[0]
    tm = min(256, t)
    nt = t // tm
    n = len(blocks)

    def body(*refs):
        dp_ref, w_ref, x_ref, dx1_ref, g_ref, sc_ref, sh_ref = refs[:7]
        gx_ref, acc_ref = refs[7 + n:9 + n]
        start = finish = lambda: None
        if n:
            start, finish = _exchange_ops(refs[7:7 + n], refs[9 + n:9 + 2 * n], *refs[9 + 2 * n:])

        @pl.when(pl.program_id(0) == 0)
        def _():
            start()
            acc_ref[...] = jnp.zeros_like(acc_ref)

        dh = jnp.zeros((tm, D), F32)
        for j in range(NDEV):
            dh = dh + _dot(dp_ref[:, j * D:(j + 1) * D], w_ref[DP_SPLIT[j]], _NT)
        _, vjp_h = jax.vjp(f_modulate, x_ref[...], g_ref[...], sc_ref[...], sh_ref[...])
        dx, dg, dsc, dsh = vjp_h(dh)
        gx_ref[...] = dx1_ref[...] + dx
        acc_ref[0:1, :] += dg
        acc_ref[1:2, :] += dsc
        acc_ref[2:3, :] += dsh
        pl.when(pl.program_id(0) == nt - 1)(finish)

    row = pl.BlockSpec((tm, D), lambda m: (m, 0))
    vec = pl.BlockSpec((1, D), lambda m: (0, 0))
    res = pl.pallas_call(
        body, name="in_proj_backward", grid=(nt,),
        out_shape=(jax.ShapeDtypeStruct((t, D), F32), jax.ShapeDtypeStruct((8, D), F32)) + _exchanged(blocks),
        in_specs=[pl.BlockSpec((tm, NDEV * D), lambda m: (m, 0)), VSPEC, row, row, vec, vec, vec] + [ANYSPEC] * n,
        out_specs=(row, pl.BlockSpec((8, D), lambda m: (0, 0))) + (ANYSPEC,) * n,
        scratch_shapes=_comm_sems(n, False) if n else [],
        compiler_params=_params("arbitrary"),
    )(dp, w_all, x, dx1, gain, sc, sh, *blocks)
    return res[0], res[1], res[2:]


def _dp_column(n):
    return jnp.where(n < 3, n + 3, jnp.where(n == 3, 0, jnp.where(n < 6, n + 2, n - 5)))


def weight_grad(a, b, nblk, ka, bn, a_blocked, name, b_col=lambda n: n):
    t = a.shape[0]
    tk = min(2048, t)
    nk = t // tk

    def body(a_ref, b_ref, f_ref, h_ref, acc):
        k = pl.program_id(1)

        @pl.when(k == 0)
        def _():
            acc[...] = jnp.zeros_like(acc)

        acc[...] += _dot(a_ref[...], b_ref[...], _TN)

        @pl.when(k == nk - 1)
        def _():
            f_ref[...] = acc[...]
            h_ref[...] = acc[...].astype(BF16)

    a_idx = (lambda n, k: (k, n)) if a_blocked else (lambda n, k: (k, 0))
    b_idx = (lambda n, k: (k, 0)) if a_blocked else (lambda n, k: (k, b_col(n)))
    out = pl.BlockSpec((None, ka, bn), lambda n, k: (n, 0, 0))
    return pl.pallas_call(
        body, name=name, grid=(nblk, nk),
        out_shape=(jax.ShapeDtypeStruct((nblk, ka, bn), F32), jax.ShapeDtypeStruct((nblk, ka, bn), BF16)),
        in_specs=[pl.BlockSpec((tk, ka), a_idx), pl.BlockSpec((tk, bn), b_idx)],
        out_specs=(out, out),
        scratch_shapes=[pltpu.VMEM((ka, bn), F32)],
        compiler_params=_params("parallel", "arbitrary"),
    )(a, b)


def ada_backward(call_t, dmod_cols, w, m, v):
    def body(c_ref, d_ref, w_ref, m_ref, v_ref, g_ref, dl_ref, nm_ref, nv_ref):
        ct, dm = c_ref[...], d_ref[...]
        g = ct[:, 0:1] * dm[0:1, :]
        for r in range(1, NDEV):
            g = g + ct[:, r:r + 1] * dm[r:r + 1, :]
        g_ref[...] = g
        dl_ref[...], nm_ref[...], nv_ref[...] = _adamw(w_ref[...], g, m_ref[...], v_ref[...])

    s = jax.ShapeDtypeStruct(w.shape, F32)
    return pl.pallas_call(
        body, name="ada_backward", out_shape=(s, s, s, s),
        in_specs=[VSPEC] * 5, out_specs=(VSPEC,) * 4, compiler_params=_params(),
    )(call_t, dmod_cols, w, m, v)


def adamw_small(total, ddw_mine, recipes, ws, ms, vs):
    n = len(ws)

    def body(*refs):
        tot, ddw = refs[0], refs[1]
        w_refs, m_refs, v_refs = refs[2:2 + n], refs[2 + n:2 + 2 * n], refs[2 + 2 * n:2 + 3 * n]
        outs = refs[2 + 3 * n:2 + 7 * n]
        loss_ref = refs[2 + 7 * n]
        for i, rec in enumerate(recipes):
            if rec == "dw":
                g = ddw[...]
            elif isinstance(rec, tuple):
                g = tot[rec[0]:rec[1], :]
            else:
                g = jnp.concatenate([tot[r:r + 1, :] for r in rec], axis=1) if len(rec) > 1 else tot[rec[0]:rec[0] + 1, :]
            dl, nm, nv = _adamw(w_refs[i][...], g, m_refs[i][...], v_refs[i][...])
            outs[4 * i][...] = g
            outs[4 * i + 1][...] = dl
            outs[4 * i + 2][...] = nm
            outs[4 * i + 3][...] = nv
        loss_ref[...] = tot[LOSS_ROW:LOSS_ROW + 1, 0:128]

    shapes = []
    for w in ws:
        shapes += [jax.ShapeDtypeStruct(w.shape, F32)] * 4
    res = pl.pallas_call(
        body, name="adamw_small", out_shape=tuple(shapes) + (jax.ShapeDtypeStruct((1, 128), F32),),
        in_specs=[VSPEC] * (2 + 3 * n), out_specs=(VSPEC,) * (4 * n + 1), compiler_params=_params(),
    )(total, ddw_mine, *ws, *ms, *vs)
    return [res[4 * i:4 * i + 4] for i in range(n)], res[4 * n]


def reduce_and_adamw(me, w, g_all, g_recv, m, v, name):
    r, c = w.shape
    br = min(128, r)

    def body(me_ref, w_ref, go_ref, gr_ref, m_ref, v_ref, g_ref, dl_ref, nm_ref, nv_ref):
        g = go_ref[...]
        for k in range(NDEV - 1):
            g = g + gr_ref[k].astype(F32)
        g_ref[...] = g
        dl_ref[...], nm_ref[...], nv_ref[...] = _adamw(w_ref[...], g, m_ref[...], v_ref[...])

    blk = pl.BlockSpec((br, c), lambda i, me_ref: (i, 0))
    s = jax.ShapeDtypeStruct(w.shape, F32)
    return pl.pallas_call(
        body, name=name, out_shape=(s, s, s, s),
        grid_spec=pltpu.PrefetchScalarGridSpec(
            num_scalar_prefetch=1, grid=(r // br,),
            in_specs=[blk, pl.BlockSpec((None, br, c), lambda i, me_ref: (me_ref[0], i, 0)),
                      pl.BlockSpec((NDEV - 1, br, c), lambda i, me_ref: (0, i, 0)), blk, blk],
            out_specs=(blk, blk, blk, blk)),
        compiler_params=_params("parallel"),
    )(me, w, g_all, g_recv, m, v)


def kernel(x, c, w_ada, b_ada, pre_norm_tm, post_norm_tm, pre_norm_cm, post_norm_cm, w_in, b_in, hg_lb_logits, hg_norm, conv_dw, conv_db, conv_ln_g, conv_ln_b, w_br_a, w_br_b, w_out, w_ff1, w_ff2, loss_target, m_w_ada, m_b_ada, m_pre_norm_tm, m_post_norm_tm, m_pre_norm_cm, m_post_norm_cm, m_w_in, m_b_in, m_hg_lb_logits, m_hg_norm, m_conv_dw, m_conv_db, m_conv_ln_g, m_conv_ln_b, m_w_br_a, m_w_br_b, m_w_out, m_w_ff1, m_w_ff2, v_w_ada, v_b_ada, v_pre_norm_tm, v_post_norm_tm, v_pre_norm_cm, v_post_norm_cm, v_w_in, v_b_in, v_hg_lb_logits, v_hg_norm, v_conv_dw, v_conv_db, v_conv_ln_g, v_conv_ln_b, v_w_br_a, v_w_br_b, v_w_out, v_w_ff1, v_w_ff2):
    t = x.shape[1]
    me = 4 * lax.axis_index("x") + 2 * lax.axis_index("y") + lax.axis_index("c")
    xs = x[0]
    tgt = loss_target[0]

    mod, call = ada_forward(c, w_ada[0], b_ada)
    sh1, sc1, g1, sh2, sc2, g2 = [mod[:, i * D:(i + 1) * D] for i in range(6)]
    win_all, dw_all = all_gather_blocks([w_in[0].astype(BF16), conv_dw[0]])
    dw = jnp.pad(dw_all.transpose(1, 0, 2).reshape(KW, D), ((0, HALO - KW), (0, 0)))

    p, h, (w1_all,) = in_proj(xs, pre_norm_tm, sc1, sh1, win_all, b_in, [w_ff1[0].astype(BF16)])
    intra, qe, ut, dec, (w2_all,) = hgrn_local(p, hg_lb_logits, [w_ff2[0].astype(BF16)])
    st_prev = state_scan(ut, dec, False)
    yc, act, (wa_all, wb_all, wo_all) = conv_forward(
        p, dw, conv_db, conv_ln_g, conv_ln_b, [w_br_a[0].astype(BF16), w_br_b[0].astype(BF16), w_out[0].astype(BF16)])
    wa, wb, wo = wa_all.reshape(D, D), wb_all.reshape(D, D), wo_all.reshape(D, D)
    x1 = mix_forward(xs, p, qe, intra, st_prev, act, wa, wb, wo, hg_norm, g1, post_norm_tm)

    dx1, h2, r, dz, dy2, acc_ffn = ffn_forward_backward(x1, tgt, w1_all, w2_all, pre_norm_cm, sc2, sh2, g2,
                                                        post_norm_cm)
    rows = D // NDEV
    g1_f, g1_h = weight_grad(h2, dz, NDEV, D, DFF // NDEV, False, "grad_w_ff1")
    g2_f, g2_h = weight_grad(r, dy2, 4, D, D, True, "grad_w_ff2")
    g2_f, g2_h = g2_f.reshape(NDEV, DFF // NDEV, D), g2_h.reshape(NDEV, DFF // NDEV, D)

    (dp, do, dqe, vt, dact, a16, mg16, dya, dyb, dy, acc_mix) = mix_backward(
        xs, p, qe, intra, st_prev, act, wa, wb, wo, hg_norm, g1, post_norm_tm, dx1)
    gt = state_scan(vt, dec, True)
    dp, acc_hg, (r_ff1, r_ff2) = hgrn_backward(p, hg_lb_logits, do, dqe, gt, st_prev, [g1_h, g2_h], dp)
    ga_f, ga_h = weight_grad(a16, dya, 1, D, D, False, "grad_w_br_a")
    gb_f, gb_h = weight_grad(act, dyb, 1, D, D, False, "grad_w_br_b")
    go_f, go_h = weight_grad(mg16, dy, 1, D, D, False, "grad_w_out")
    dp, acc_conv, ddw, (r_a, r_b, r_o) = conv_backward(
        p, yc, dact, dw, conv_ln_g, conv_ln_b,
        [ga_h.reshape(NDEV, rows, D), gb_h.reshape(NDEV, rows, D), go_h.reshape(NDEV, rows, D)], dp)
    gin_f, gin_h = weight_grad(h, dp, NDEV, D, D, False, "grad_w_in", _dp_column)
    in_send, in_recv, gin_thru, in_landing, token = exchange_start(gin_h)
    grad_x, acc_in, _ = in_proj_backward(dp, win_all, xs, dx1, pre_norm_tm, sc1, sh1 + token[0:1, 0:1], [])

    own = [ga_f.reshape(NDEV, rows, D), gb_f.reshape(NDEV, rows, D), go_f.reshape(NDEV, rows, D), g1_f, g2_f]
    recv = [r_a, r_b, r_o, r_ff1, r_ff2]
    big = {}
    names = ["w_br_a", "w_br_b", "w_out", "w_ff1", "w_ff2"]
    ws = [w_br_a, w_br_b, w_out, w_ff1, w_ff2]
    ms = [m_w_br_a, m_w_br_b, m_w_out, m_w_ff1, m_w_ff2]
    vs = [v_w_br_a, v_w_br_b, v_w_out, v_w_ff1, v_w_ff2]
    me1 = me.astype(jnp.int32).reshape(1)
    for i, nm in enumerate(names):
        big[nm] = [o[None] for o in reduce_and_adamw(me1, ws[i][0], own[i], recv[i], ms[i][0], vs[i][0], "adamw_" + nm)]

    dmod_rows = [2, 1, 8, 19, 18, 20]
    total, kept = gather_and_sum_rows([acc_in, acc_mix, acc_ffn, acc_hg, acc_conv, ddw], dmod_rows)
    dmod_all = kept[:, 0:6, :].reshape(NDEV, 6 * D)
    wcols = w_ada.shape[2]
    gwa, dwa, nmwa, nvwa = ada_backward(call.T, lax.dynamic_slice_in_dim(dmod_all, me * wcols, wcols, axis=1),
                                        w_ada[0], m_w_ada[0], v_w_ada[0])
    ddw_mine = lax.dynamic_slice_in_dim(total[40:40 + KW], me * (D // NDEV), D // NDEV, axis=1)

    small_names = ["b_ada", "pre_norm_tm", "post_norm_tm", "pre_norm_cm", "post_norm_cm", "b_in", "hg_lb_logits",
                   "hg_norm", "conv_db", "conv_ln_g", "conv_ln_b", "conv_dw"]
    recipes = [dmod_rows, [0], [9], [16], [17], [26, 27, 28, 11, 35, 36, 12, 13], (24, 26), [10], [32], [33], [34],
               "dw"]
    small_w = [b_ada, pre_norm_tm, post_norm_tm, pre_norm_cm, post_norm_cm, b_in, hg_lb_logits, hg_norm, conv_db,
               conv_ln_g, conv_ln_b, conv_dw[0]]
    small_m = [m_b_ada, m_pre_norm_tm, m_post_norm_tm, m_pre_norm_cm, m_post_norm_cm, m_b_in, m_hg_lb_logits,
               m_hg_norm, m_conv_db, m_conv_ln_g, m_conv_ln_b, m_conv_dw[0]]
    small_v = [v_b_ada, v_pre_norm_tm, v_post_norm_tm, v_pre_norm_cm, v_post_norm_cm, v_b_in, v_hg_lb_logits,
               v_hg_norm, v_conv_db, v_conv_ln_g, v_conv_ln_b, v_conv_dw[0]]
    small_out, loss_row = adamw_small(total, ddw_mine, recipes, small_w, small_m, small_v)
    loss = loss_row[0, 0]
    sm = {nm: list(o) for nm, o in zip(small_names, small_out)}
    sm["conv_dw"] = [o[None] for o in sm["conv_dw"]]
    loss_row, big, grad_x = lax.optimization_barrier((loss_row, big, grad_x))
    r_in = exchange_wait(in_send, in_recv, gin_thru, in_landing, loss_row)
    big["w_in"] = [o[None] for o in reduce_and_adamw(me1, w_in[0], gin_f, r_in, m_w_in[0], v_w_in[0], "adamw_w_in")]

    order = ["w_ada", "b_ada", "pre_norm_tm", "post_norm_tm", "pre_norm_cm", "post_norm_cm", "w_in", "b_in",
             "hg_lb_logits", "hg_norm", "conv_dw", "conv_db", "conv_ln_g", "conv_ln_b", "w_br_a", "w_br_b", "w_out",
             "w_ff1", "w_ff2"]
    res = dict(sm)
    res.update(big)
    res["w_ada"] = [gwa[None], dwa[None], nmwa[None], nvwa[None]]
    outs = [loss, grad_x[None]]
    for j in range(4):
        outs += [res[nm][j] for nm in order]
    return tuple(outs)
```

```python
import functools

import jax
import jax.numpy as jnp
from jax import lax
from jax.experimental import pallas as pl
from jax.experimental.pallas import tpu as pltpu

F32 = jnp.float32
BF16 = jnp.bfloat16
MESH = pl.DeviceIdType.MESH
AXES = ("x", "y", "c")

D = 1024
HEADS = 8
DK = 128
LANE = 128
CH = 128
LEVELS = 7
KW = 31
HALO = 32
DFF = 4096
NDEV = 8
EPS = 1e-6
DP_SPLIT = (3, 6, 7, 0, 1, 2, 4, 5)
LOSS_ROW = 21
ADAM_LR, ADAM_B1, ADAM_B2, ADAM_EPS, ADAM_WD, ADAM_STEP = 0.001, 0.9, 0.999, 1e-08, 0.01, 10
VMEM_LIMIT = 58 * 1024 * 1024

_NN = (((1,), (0,)), ((), ()))
_NT = (((1,), (1,)), ((), ()))
_TN = (((0,), (0,)), ((), ()))

VSPEC = pl.BlockSpec(memory_space=pltpu.VMEM)
ANYSPEC = pl.BlockSpec(memory_space=pl.ANY)


def _params(*sem):
    return pltpu.CompilerParams(dimension_semantics=sem or None, vmem_limit_bytes=VMEM_LIMIT)


def _dot(a, b, dims):
    return lax.dot_general(a, b, dims, preferred_element_type=F32)


@jax.custom_vjp
def mm_nn(a, b):
    return _dot(a.astype(BF16), b.astype(BF16), _NN)


def _mm_nn_fwd(a, b):
    ab, bb = a.astype(BF16), b.astype(BF16)
    return _dot(ab, bb, _NN), (ab, bb)


def _mm_nn_bwd(res, ct):
    ab, bb = res
    cb = ct.astype(BF16)
    return _dot(cb, bb, _NT), _dot(ab, cb, _TN)


mm_nn.defvjp(_mm_nn_fwd, _mm_nn_bwd)


@jax.custom_vjp
def mm_nt(a, b):
    return _dot(a.astype(BF16), b.astype(BF16), _NT)


def _mm_nt_fwd(a, b):
    ab, bb = a.astype(BF16), b.astype(BF16)
    return _dot(ab, bb, _NT), (ab, bb)


def _mm_nt_bwd(res, ct):
    ab, bb = res
    cb = ct.astype(BF16)
    return _dot(cb, bb, _NN), _dot(cb, ab, _TN)


mm_nt.defvjp(_mm_nt_fwd, _mm_nt_bwd)


@jax.custom_vjp
def mm_tn(a, b):
    return _dot(a.astype(BF16), b.astype(BF16), _TN)


def _mm_tn_fwd(a, b):
    ab, bb = a.astype(BF16), b.astype(BF16)
    return _dot(ab, bb, _TN), (ab, bb)


def _mm_tn_bwd(res, ct):
    ab, bb = res
    cb = ct.astype(BF16)
    return _dot(bb, cb, _NT), _dot(ab, cb, _NN)


mm_tn.defvjp(_mm_tn_fwd, _mm_tn_bwd)


def _rms(x):
    return x * lax.rsqrt(jnp.mean(x * x, axis=-1, keepdims=True) + EPS)


def _silu(x):
    return x * jax.nn.sigmoid(x)


def f_modulate(x, gain, sc, sh):
    return _rms(x) * gain * (1.0 + sc) + sh


def f_residual(x, y, gate, gain):
    return x + gate * (_rms(y) * gain)


def f_merge(ga, gb, ya, yb):
    return jax.nn.sigmoid(ga) * ya + jax.nn.sigmoid(gb) * yb


def f_head_out(o, og, hg):
    heads = [_rms(o[:, h * DK:(h + 1) * DK]) for h in range(HEADS)]
    return jnp.concatenate(heads, axis=1) * hg * _silu(og)


def f_conv_act(u, g, b):
    mu = jnp.mean(u, axis=-1, keepdims=True)
    var = jnp.mean(jnp.square(u - mu), axis=-1, keepdims=True)
    return _silu((u - mu) * lax.rsqrt(var + EPS) * g + b)


def f_glu(cv, cg):
    return cv * jax.nn.sigmoid(cg)


def _split2(x):
    hi = x.astype(BF16)
    return hi, (x - hi.astype(F32)).astype(BF16)


def _tri(transposed):
    i = lax.broadcasted_iota(jnp.int32, (CH, CH), 1 if transposed else 0)
    t = lax.broadcasted_iota(jnp.int32, (CH, CH), 0 if transposed else 1)
    return jnp.where(t <= i, 1.0, 0.0).astype(BF16)


def _blocks3(x, rows):
    return x.reshape(CH // rows, rows, x.shape[-1])


def _mid_broadcast(b, lev):
    h = 1 << (LEVELS - 1 - lev)
    if h >= 4:
        x3 = _blocks3(b, 2 * h)
        return jnp.broadcast_to(x3[:, h - 1:h, :], x3.shape).reshape(b.shape)
    x3 = _blocks3(b, 8)
    sub = lax.broadcasted_iota(jnp.int32, x3.shape, 1)
    out = None
    for first in range(0, 8, 2 * h):
        piece = jnp.broadcast_to(x3[:, first + h - 1:first + h, :], x3.shape)
        out = piece if out is None else jnp.where(sub >= first, piece, out)
    return out.reshape(b.shape)


def _mid_scatter(d, lev):
    h = 1 << (LEVELS - 1 - lev)
    if h >= 4:
        x3 = _blocks3(d, 2 * h)
        row = lax.broadcasted_iota(jnp.int32, x3.shape, 1)
        total = jnp.sum(x3, axis=1, keepdims=True)
        return jnp.where(row == h - 1, total, 0.0).reshape(d.shape)
    x3 = _blocks3(d, 8)
    sub = lax.broadcasted_iota(jnp.int32, x3.shape, 1)
    out = jnp.zeros_like(x3)
    for first in range(0, 8, 2 * h):
        inside = (sub >= first) & (sub < first + 2 * h)
        total = jnp.sum(jnp.where(inside, x3, 0.0), axis=1, keepdims=True)
        out = jnp.where(sub == first + h - 1, total, out)
    return out.reshape(d.shape)


@jax.custom_vjp
def decay_sums(g):
    hi, lo = _split2(g)
    tri = _tri(False)
    b = _dot(tri, hi, _NN) + _dot(tri, lo, _NN)
    return (b,) + tuple(b - _mid_broadcast(b, lev) for lev in range(LEVELS))


def _decay_sums_fwd(g):
    return decay_sums(g), None


def _decay_sums_bwd(_, cts):
    db = cts[0]
    for lev in range(LEVELS):
        db = db + cts[1 + lev] - _mid_scatter(cts[1 + lev], lev)
    hi, lo = _split2(db)
    tri = _tri(True)
    return (_dot(tri, hi, _NN) + _dot(tri, lo, _NN),)


decay_sums.defvjp(_decay_sums_fwd, _decay_sums_bwd)


def _score_masks():
    i = lax.broadcasted_iota(jnp.int32, (CH, CH), 0)
    j = lax.broadcasted_iota(jnp.int32, (CH, CH), 1)
    masks = [i == j]
    for lev in range(LEVELS):
        sh = LEVELS - 1 - lev
        same = (i >> (sh + 1)) == (j >> (sh + 1))
        masks.append(same & (((i >> sh) & 1) == 1) & (((j >> sh) & 1) == 0))
    return [jnp.where(m, 1.0, 0.0) for m in masks]


def f_hgrn_chunk(q_r, f_r, v, logits):
    l0, l1 = logits[0:1, :], logits[1:2, :]
    mx = lax.stop_gradient(jnp.maximum(l0, l1))
    e0, e1 = jnp.exp(l0 - mx), jnp.exp(l1 - mx)
    lb = e0 / (e0 + e1)
    q = _silu(q_r)
    f = lb + (1.0 - lb) * jax.nn.sigmoid(f_r)
    k = 1.0 - f
    sums = decay_sums(jnp.log(f))
    b = sums[0]
    btot = b[CH - 1:CH, :]
    qe = q * jnp.exp(b)
    ke = k * jnp.exp(btot - b)
    dec = jnp.exp(btot)
    qs, ks = [q], [k]
    row = lax.broadcasted_iota(jnp.int32, b.shape, 0)
    for lev in range(LEVELS):
        upper = ((row >> (LEVELS - 1 - lev)) & 1) == 1
        e = sums[1 + lev]
        qs.append(q * jnp.exp(jnp.where(upper, e, 0.0)))
        ks.append(k * jnp.exp(jnp.where(upper, 0.0, -e)))
    masks = _score_masks()
    intra, ut = [], []
    for h in range(HEADS):
        sl = slice(h * DK, (h + 1) * DK)
        sc = None
        for lev in range(LEVELS + 1):
            s = mm_nt(qs[lev][:, sl], ks[lev][:, sl]) * masks[lev]
            sc = s if sc is None else sc + s
        intra.append(mm_nn(sc, v[:, sl]))
        ut.append(mm_tn(v[:, sl], ke[:, sl]))
    return jnp.concatenate(intra, axis=1), qe, jnp.concatenate(ut, axis=1), dec


def _inter(qe_b, st_ref, rows):
    out = []
    for ci in range(rows // CH):
        st = st_ref[ci].astype(BF16)
        heads = [_dot(qe_b[ci * CH:(ci + 1) * CH, h * DK:(h + 1) * DK], st[:, h * DK:(h + 1) * DK], _NT)
                 for h in range(HEADS)]
        out.append(jnp.concatenate(heads, axis=1))
    return jnp.concatenate(out, axis=0)


def _shift_stack(src, dst, rows):
    dst[0, 0:rows, :] = src[0:rows, :]
    for b in range(1, 8):
        dst[b, 0:rows - 8, :] = src[pl.ds(b, rows - 8), :]


def _shifted(stack, offset, rows, lanes=slice(None)):
    return stack[offset % 8, pl.ds(8 * (offset // 8), rows), lanes]


def _adamw(w, g, m, v):
    m = ADAM_B1 * m + (1.0 - ADAM_B1) * g
    v = ADAM_B2 * v + (1.0 - ADAM_B2) * jnp.square(g)
    m_hat = m / (1.0 - ADAM_B1 ** ADAM_STEP)
    v_hat = v / (1.0 - ADAM_B2 ** ADAM_STEP)
    delta = -ADAM_LR * (m_hat / (jnp.sqrt(v_hat) + ADAM_EPS) + ADAM_WD * w)
    return delta, m, v


def _me():
    return lax.axis_index("x"), lax.axis_index("y"), lax.axis_index("c")


def _peer(k):
    x, y, c = _me()
    mask = k + 1
    px = (1 - x) if (mask >> 2) & 1 else x
    py = (1 - y) if (mask >> 1) & 1 else y
    pc = (1 - c) if mask & 1 else c
    return (px, py, pc), 4 * px + 2 * py + pc


def ada_forward(c, w_ada, b_ada):
    wcols = w_ada.shape[1]

    def body(c_ref, w_ref, b_ref, mod_ref, call_ref, part_ref, modp_ref, send_sems, recv_sems):
        x, y, cc = _me()
        me = 4 * x + 2 * y + cc
        call_ref[me] = _silu(c_ref[...])
        sends = []
        for k in range(NDEV - 1):
            dev, _ = _peer(k)
            cp = pltpu.make_async_remote_copy(call_ref.at[me], call_ref.at[me], send_sems.at[k], recv_sems.at[k],
                                              device_id=dev, device_id_type=MESH)
            cp.start()
            sends.append(cp)
        for k in range(NDEV - 1):
            _, pidx = _peer(k)
            pltpu.make_async_remote_copy(call_ref.at[pidx], call_ref.at[pidx], send_sems.at[k], recv_sems.at[k],
                                         device_id=_peer(k)[0], device_id_type=MESH).wait_recv()
        call = jnp.concatenate([call_ref[r] for r in range(NDEV)], axis=0)
        part = _dot(call.astype(BF16), w_ref[...].astype(BF16), _NN)
        for r in range(NDEV):
            part_ref[r] = part[r:r + 1, :]
        modp_ref[me] = part_ref[me]
        for k in range(NDEV - 1):
            dev, pidx = _peer(k)
            cp = pltpu.make_async_remote_copy(part_ref.at[pidx], modp_ref.at[me], send_sems.at[NDEV - 1 + k],
                                              recv_sems.at[NDEV - 1 + k], device_id=dev, device_id_type=MESH)
            cp.start()
            sends.append(cp)
        for k in range(NDEV - 1):
            dev, pidx = _peer(k)
            pltpu.make_async_remote_copy(part_ref.at[pidx], modp_ref.at[pidx], send_sems.at[NDEV - 1 + k],
                                         recv_sems.at[NDEV - 1 + k], device_id=dev, device_id_type=MESH).wait_recv()
        for cp in sends:
            cp.wait_send()
        mod_ref[...] = jnp.concatenate([modp_ref[r] for r in range(NDEV)], axis=1) + b_ref[...]

    mod, call = pl.pallas_call(
        body, name="ada_forward",
        out_shape=(jax.ShapeDtypeStruct((1, NDEV * wcols), F32), jax.ShapeDtypeStruct((NDEV, 1, D), F32)),
        in_specs=[VSPEC, VSPEC, VSPEC], out_specs=(VSPEC, VSPEC),
        scratch_shapes=[pltpu.VMEM((NDEV, 1, wcols), F32), pltpu.VMEM((NDEV, 1, wcols), F32),
                        pltpu.SemaphoreType.DMA((2 * (NDEV - 1),)), pltpu.SemaphoreType.DMA((2 * (NDEV - 1),))],
        compiler_params=_params(),
    )(c, w_ada, b_ada)
    return mod, call.reshape(NDEV, D)


def _comm_sems(n, local):
    sems = [pltpu.SemaphoreType.DMA((7 * n,)), pltpu.SemaphoreType.DMA((7 * n,))]
    return sems + ([pltpu.SemaphoreType.DMA((n,))] if local else [])


def _gather2_ops(ins, outs, send_sems, recv_sems, local_sems):
    n = len(ins)
    x, y, c = _me()
    me, sibling = (x, y, c), (x, y, 1 - c)
    chips = [(1 - x, y), (x, 1 - y), (1 - x, 1 - y)]

    def slot(p):
        return 4 * p[0] + 2 * p[1] + p[2]

    def copy(a, k, block, to, src=None):
        return pltpu.make_async_remote_copy(
            src_ref=outs[a].at[slot(block)] if src is None else src, dst_ref=outs[a].at[slot(block)],
            send_sem=send_sems.at[a * 7 + k], recv_sem=recv_sems.at[a * 7 + k], device_id=to, device_id_type=MESH)

    def local(a):
        return pltpu.make_async_copy(ins[a], outs[a].at[slot(me)], local_sems.at[a])

    def first(a):
        return [copy(a, 0, me, sibling, src=ins[a])] + [copy(a, 1 + j, me, (*chip, c), src=ins[a])
                                                        for j, chip in enumerate(chips)]

    def passed(a):
        return [copy(a, 4 + j, (*chip, c), sibling) for j, chip in enumerate(chips)]

    def start():
        for a in range(n):
            local(a).start()
            for cp in first(a):
                cp.start()

    def forward():
        for j, chip in enumerate(chips):
            for a in range(n):
                copy(a, 1 + j, (*chip, c), me).wait_recv()
                passed(a)[j].start()

    def finish():
        for a in range(n):
            copy(a, 0, sibling, me).wait_recv()
            for j, chip in enumerate(chips):
                copy(a, 4 + j, (*chip, 1 - c), me).wait_recv()
        for a in range(n):
            for cp in first(a) + passed(a):
                cp.wait_send()
            local(a).wait()

    return start, forward, finish


def _gather1_ops(ins, outs, send_sems, recv_sems, local_sems):
    n = len(ins)
    x, y, c = _me()
    me = 4 * x + 2 * y + c

    def copy(a, k, block):
        return pltpu.make_async_remote_copy(ins[a], outs[a].at[block], send_sems.at[a * 7 + k],
                                            recv_sems.at[a * 7 + k], device_id=_peer(k)[0], device_id_type=MESH)

    def local(a):
        return pltpu.make_async_copy(ins[a], outs[a].at[me], local_sems.at[a])

    def start():
        for a in range(n):
            local(a).start()
            for k in range(NDEV - 1):
                copy(a, k, me).start()

    def finish():
        for a in range(n):
            for k in range(NDEV - 1):
                copy(a, k, _peer(k)[1]).wait_recv()
        for a in range(n):
            for k in range(NDEV - 1):
                copy(a, k, me).wait_send()
            local(a).wait()

    return start, finish


def _exchange_ops(ins, outs, send_sems, recv_sems):
    n = len(ins)

    def copy(a, k):
        dev, pidx = _peer(k)
        return pltpu.make_async_remote_copy(ins[a].at[pidx], outs[a].at[k], send_sems.at[a * 7 + k],
                                            recv_sems.at[a * 7 + k], device_id=dev, device_id_type=MESH)

    def start():
        for k in range(NDEV - 1):
            for a in range(n):
                copy(a, k).start()

    def finish():
        for k in range(NDEV - 1):
            for a in range(n):
                copy(a, k).wait_recv()
        for k in range(NDEV - 1):
            for a in range(n):
                copy(a, k).wait_send()

    return start, finish


def _gathered(shards):
    return tuple(jax.ShapeDtypeStruct((NDEV,) + s.shape, s.dtype) for s in shards)


def _exchanged(blocks):
    return tuple(jax.ShapeDtypeStruct((NDEV - 1,) + b.shape[1:], b.dtype) for b in blocks)


def exchange_start(blocks):
    landing = lax.empty((NDEV - 1,) + blocks.shape[1:], blocks.dtype)
    hbm = pl.BlockSpec(memory_space=pltpu.HBM)
    sem = pl.BlockSpec(memory_space=pltpu.SEMAPHORE)

    def body(src_ref, land_ref, send_sems, recv_sems, src_thru, land_thru, token):
        for k in range(NDEV - 1):
            dev, pidx = _peer(k)
            pltpu.make_async_remote_copy(src_ref.at[pidx], land_ref.at[k], send_sems.at[k], recv_sems.at[k],
                                         device_id=dev, device_id_type=MESH).start()
        token[...] = jnp.zeros_like(token)

    return pl.pallas_call(
        body, name="exchange_start",
        out_shape=(pltpu.SemaphoreType.DMA((NDEV - 1,)), pltpu.SemaphoreType.DMA((NDEV - 1,)),
                   pltpu.HBM(blocks.shape, blocks.dtype), pltpu.HBM(landing.shape, landing.dtype),
                   jax.ShapeDtypeStruct((8, 128), F32)),
        in_specs=(hbm, hbm), out_specs=(sem, sem, hbm, hbm, VSPEC), input_output_aliases={0: 2, 1: 3},
        compiler_params=pltpu.CompilerParams(has_side_effects=pltpu.SideEffectType.DATAFLOW_SIDE_EFFECTING),
    )(pltpu.with_memory_space_constraint(blocks, pltpu.HBM), pltpu.with_memory_space_constraint(landing, pltpu.HBM))


def exchange_wait(send_sems, recv_sems, src_thru, land_thru, after):
    hbm = pl.BlockSpec(memory_space=pltpu.HBM)
    sem = pl.BlockSpec(memory_space=pltpu.SEMAPHORE)

    def body(src_ref, land_ref, send_sems, recv_sems, after_ref, src_dead, got_ref):
        for k in range(NDEV - 1):
            dev, pidx = _peer(k)
            cp = pltpu.make_async_remote_copy(src_ref.at[pidx], land_ref.at[k], send_sems.at[k], recv_sems.at[k],
                                              device_id=dev, device_id_type=MESH)
            cp.wait_send()
            cp.wait_recv()

    return pl.pallas_call(
        body, name="exchange_wait",
        out_shape=(pltpu.HBM(src_thru.shape, src_thru.dtype), pltpu.HBM(land_thru.shape, land_thru.dtype)),
        in_specs=(hbm, hbm, sem, sem, ANYSPEC), out_specs=(hbm, hbm), input_output_aliases={0: 0, 1: 1},
        compiler_params=pltpu.CompilerParams(has_side_effects=pltpu.SideEffectType.DATAFLOW_SIDE_EFFECTING),
    )(src_thru, land_thru, send_sems, recv_sems, after)[1]


def all_gather_blocks(shards):
    n = len(shards)

    def body(*refs):
        start, forward, finish = _gather2_ops(refs[:n], refs[n:2 * n], *refs[2 * n:])
        start()
        forward()
        finish()

    return pl.pallas_call(
        body, name="all_gather_blocks", out_shape=_gathered(shards),
        in_specs=[ANYSPEC] * n, out_specs=tuple([ANYSPEC] * n),
        scratch_shapes=_comm_sems(n, True), compiler_params=_params(),
    )(*shards)


def gather_and_sum_rows(parts, keep_rows):
    n = len(parts)
    offs = [sum(p.shape[0] for p in parts[:i]) for i in range(n)]
    rows = sum(p.shape[0] for p in parts)

    def body(*refs):
        sum_ref, keep_ref, buf_ref, send_sems, recv_sems = refs[n:]
        x, y, c = _me()
        me = 4 * x + 2 * y + c
        for i in range(n):
            buf_ref[me, offs[i]:offs[i] + parts[i].shape[0], :] = refs[i][...]

        def copy(k, block):
            return pltpu.make_async_remote_copy(buf_ref.at[block], buf_ref.at[block], send_sems.at[k],
                                                recv_sems.at[k], device_id=_peer(k)[0], device_id_type=MESH)

        for k in range(NDEV - 1):
            copy(k, me).start()
        for k in range(NDEV - 1):
            copy(k, _peer(k)[1]).wait_recv()
        for k in range(NDEV - 1):
            copy(k, me).wait_send()
        total = buf_ref[0]
        for r in range(1, NDEV):
            total = total + buf_ref[r]
        sum_ref[...] = total
        keep_ref[...] = jnp.zeros_like(keep_ref)
        for r in range(NDEV):
            for j, src in enumerate(keep_rows):
                keep_ref[r, j:j + 1, :] = buf_ref[r, src:src + 1, :]

    return pl.pallas_call(
        body, name="gather_and_sum_rows",
        out_shape=(jax.ShapeDtypeStruct((rows, D), F32), jax.ShapeDtypeStruct((NDEV, 8, D), F32)),
        in_specs=[VSPEC] * n, out_specs=(VSPEC, VSPEC),
        scratch_shapes=[pltpu.VMEM((NDEV, rows, D), F32), pltpu.SemaphoreType.DMA((NDEV - 1,)),
                        pltpu.SemaphoreType.DMA((NDEV - 1,))],
        compiler_params=_params(),
    )(*parts)


def in_proj(x, gain, sc, sh, w_all, b_in, shards):
    t = x.shape[0]
    tm = min(256, t)
    n = len(shards)
    steps = t // tm

    def body(*refs):
        x_ref, g_ref, sc_ref, sh_ref, w_ref, b_ref = refs[:6]
        p_ref, h_ref = refs[6 + n:8 + n]
        start, forward, finish = _gather2_ops(refs[6:6 + n], refs[8 + n:8 + 2 * n], *refs[8 + 2 * n:])
        step = pl.program_id(0)
        pl.when(step == 0)(start)
        h = f_modulate(x_ref[...], g_ref[...], sc_ref[...], sh_ref[...]).astype(BF16)
        h_ref[...] = h
        for j in range(NDEV):
            p_ref[:, j * D:(j + 1) * D] = _dot(h, w_ref[j], _NN) + b_ref[:, j * D:(j + 1) * D]
        pl.when(step == (3 * steps) // 4)(forward)
        pl.when(step == steps - 1)(finish)

    vec = pl.BlockSpec((1, D), lambda m: (0, 0))
    res = pl.pallas_call(
        body, name="in_proj", grid=(steps,),
        out_shape=(jax.ShapeDtypeStruct((t, NDEV * D), F32), jax.ShapeDtypeStruct((t, D), BF16)) + _gathered(shards),
        in_specs=[pl.BlockSpec((tm, D), lambda m: (m, 0)), vec, vec, vec, VSPEC,
                  pl.BlockSpec((1, NDEV * D), lambda m: (0, 0))] + [ANYSPEC] * n,
        out_specs=(pl.BlockSpec((tm, NDEV * D), lambda m: (m, 0)), pl.BlockSpec((tm, D), lambda m: (m, 0)))
        + (ANYSPEC,) * n,
        scratch_shapes=_comm_sems(n, True),
        compiler_params=_params("arbitrary"),
    )(x, gain, sc, sh, w_all, b_in, *shards)
    return res[0], res[1], res[2:]


def hgrn_local(p, logits, shards):
    t = p.shape[0]
    nc = t // CH
    n = len(shards)

    def body(*refs):
        q_ref, f_ref, v_ref, l_ref = refs[:4]
        intra_ref, qe_ref, st_ref, dec_ref = refs[4 + n:8 + n]
        state = refs[8 + 2 * n]
        start, forward, finish = _gather2_ops(refs[4:4 + n], refs[8 + n:8 + 2 * n], *refs[9 + 2 * n:])
        step = pl.program_id(0)

        @pl.when(step == 0)
        def _():
            start()
            state[...] = jnp.zeros_like(state)

        intra, qe, ut, dec = f_hgrn_chunk(q_ref[...], f_ref[...], v_ref[...], l_ref[...])
        intra_ref[...] = intra
        qe_ref[...] = qe.astype(BF16)
        dec_ref[...] = dec
        s = state[...]
        st_ref[...] = s
        state[...] = dec * s + ut
        pl.when(step == (3 * nc) // 4)(forward)
        pl.when(step == nc - 1)(finish)

    col = lambda j: pl.BlockSpec((CH, D), lambda c: (c, j))
    res = pl.pallas_call(
        body, name="hgrn_local", grid=(nc,),
        out_shape=(jax.ShapeDtypeStruct((t, D), F32), jax.ShapeDtypeStruct((t, D), BF16),
                   jax.ShapeDtypeStruct((nc, DK, D), F32), jax.ShapeDtypeStruct((nc, 1, D), F32)) + _gathered(shards),
        in_specs=[col(0), col(1), col(2), pl.BlockSpec((2, D), lambda c: (0, 0))] + [ANYSPEC] * n,
        out_specs=(pl.BlockSpec((CH, D), lambda c: (c, 0)), pl.BlockSpec((CH, D), lambda c: (c, 0)),
                   pl.BlockSpec((None, DK, D), lambda c: (c, 0, 0)), pl.BlockSpec((None, 1, D), lambda c: (c, 0, 0)))
        + (ANYSPEC,) * n,
        scratch_shapes=[pltpu.VMEM((DK, D), F32)] + _comm_sems(n, True),
        compiler_params=_params("arbitrary"),
    )(p, p, p, logits, *shards)
    return res[:4] + (res[4:],)


def _conv_tile(t):
    return min(256, t)


def conv_forward(p, dw, db, ln_g, ln_b, shards):
    t = p.shape[0]
    tm = _conv_tile(t)
    per = tm // HALO
    n = len(shards)
    nt = t // tm

    def body(*refs):
        cv_ref, cg_ref, cvp_ref, cgp_ref, dw_ref, db_ref, g_ref, b_ref = refs[:8]
        yc_ref, act_ref = refs[8 + n:10 + n]
        uext, ush = refs[10 + 2 * n:12 + 2 * n]
        start, finish = _gather1_ops(refs[8:8 + n], refs[10 + n:10 + 2 * n], *refs[12 + 2 * n:])
        step = pl.program_id(0)
        pl.when(step == 0)(start)
        uext[0:HALO, :] = jnp.where(step == 0, 0.0, f_glu(cvp_ref[...], cgp_ref[...]))
        uext[HALO:HALO + tm, :] = f_glu(cv_ref[...], cg_ref[...])
        _shift_stack(uext, ush, tm + HALO)
        acc = jnp.zeros((tm, D), F32) + db_ref[...]
        for w in range(KW):
            acc = acc + dw_ref[w:w + 1, :] * _shifted(ush, HALO - KW + 1 + w, tm)
        yc_ref[...] = acc
        act_ref[...] = f_conv_act(acc, g_ref[...], b_ref[...]).astype(BF16)
        pl.when(step == nt - 1)(finish)

    vec = pl.BlockSpec((1, D), lambda m: (0, 0))
    prev = lambda j: pl.BlockSpec((HALO, D), lambda m: (jnp.maximum(m * per - 1, 0), j))
    res = pl.pallas_call(
        body, name="conv_forward", grid=(nt,),
        out_shape=(jax.ShapeDtypeStruct((t, D), F32), jax.ShapeDtypeStruct((t, D), BF16)) + _gathered(shards),
        in_specs=[pl.BlockSpec((tm, D), lambda m: (m, 4)), pl.BlockSpec((tm, D), lambda m: (m, 5)), prev(4), prev(5),
                  pl.BlockSpec((HALO, D), lambda m: (0, 0)), vec, vec, vec] + [ANYSPEC] * n,
        out_specs=(pl.BlockSpec((tm, D), lambda m: (m, 0)), pl.BlockSpec((tm, D), lambda m: (m, 0))) + (ANYSPEC,) * n,
        scratch_shapes=[pltpu.VMEM((HALO + tm, D), F32), pltpu.VMEM((8, tm + HALO, D), F32)] + _comm_sems(n, True),
        compiler_params=_params("arbitrary"),
    )(p, p, p, p, dw, db, ln_g, ln_b, *shards)
    return res[0], res[1], res[2:]


def _mix_tile(t):
    return min(256, t)


def _mix_forward_tile(x_ref, og_ref, ga_ref, gb_ref, qe_ref, intra_ref, st_ref, act_ref, wa_ref, wb_ref, wo_ref,
                      hg_ref, rows):
    o = _inter(qe_ref[...], st_ref, rows) + intra_ref[...]
    a = f_head_out(o, og_ref[...], hg_ref[...])
    ya = _dot(a.astype(BF16), wa_ref[...], _NN)
    yb = _dot(act_ref[...], wb_ref[...], _NN)
    merged = f_merge(ga_ref[...], gb_ref[...], ya, yb)
    y = _dot(merged.astype(BF16), wo_ref[...], _NN)
    return o, a, ya, yb, merged, y


def _mix_specs(tm, tiles=None):
    at = (lambda m: m) if tiles is None else (lambda m: tiles - 1 - m)
    col = lambda j: pl.BlockSpec((tm, D), lambda m: (at(m), j))
    row = pl.BlockSpec((tm, D), lambda m: (at(m), 0))
    per_chunk = lambda rows: pl.BlockSpec((tm // CH, rows, D), lambda m: (at(m), 0, 0))
    return col, row, pl.BlockSpec((1, D), lambda m: (0, 0)), per_chunk


def mix_forward(x, p, qe, intra, st_prev, act, wa, wb, wo, hg, g1, post):
    t = x.shape[0]
    tm = _mix_tile(t)

    def body(x_ref, og_ref, ga_ref, gb_ref, qe_ref, intra_ref, st_ref, act_ref, wa_ref, wb_ref, wo_ref, hg_ref,
             g1_ref, post_ref, x1_ref):
        y = _mix_forward_tile(x_ref, og_ref, ga_ref, gb_ref, qe_ref, intra_ref, st_ref, act_ref, wa_ref, wb_ref,
                              wo_ref, hg_ref, tm)[-1]
        x1_ref[...] = f_residual(x_ref[...], y, g1_ref[...], post_ref[...])

    col, row, vec, per_chunk = _mix_specs(tm)
    return pl.pallas_call(
        body, name="mix_forward", grid=(t // tm,),
        out_shape=jax.ShapeDtypeStruct((t, D), F32),
        in_specs=[row, col(3), col(6), col(7), row, row, per_chunk(DK), row, VSPEC, VSPEC, VSPEC, vec, vec, vec],
        out_specs=row,
        compiler_params=_params("parallel"),
    )(x, p, p, p, qe, intra, st_prev, act, wa, wb, wo, hg, g1, post)


def ffn_forward_backward(x1, target, w1, w2, pre, sc, sh, g2, post):
    t = x1.shape[0]
    tm = min(256, t)
    nb = w1.shape[0]
    fb = w1.shape[2]

    def body(x_ref, tg_ref, w1_ref, w2_ref, pre_ref, sc_ref, sh_ref, g2_ref, post_ref,
             dx_ref, h2_ref, r_ref, dz_ref, dy2_ref, acc_ref, z_sc):
        @pl.when(pl.program_id(0) == 0)
        def _():
            acc_ref[...] = jnp.zeros_like(acc_ref)

        x1v = x_ref[...]
        h2, vjp_h = jax.vjp(f_modulate, x1v, pre_ref[...], sc_ref[...], sh_ref[...])
        h2b = h2.astype(BF16)
        h2_ref[...] = h2b
        y2 = jnp.zeros((tm, D), F32)
        for n in range(nb):
            z = _dot(h2b, w1_ref[n], _NN)
            z_sc[:, n * fb:(n + 1) * fb] = z
            r = jnp.square(jnp.maximum(z, 0.0)).astype(BF16)
            r_ref[:, n * fb:(n + 1) * fb] = r
            y2 = y2 + _dot(r, w2_ref[n], _NN)
        out, vjp_r = jax.vjp(f_residual, x1v, y2, g2_ref[...], post_ref[...])
        err = out - tg_ref[...]
        tok = jnp.mean(jnp.square(err), axis=-1, keepdims=True)
        acc_ref[5:6, :] += 0.5 * jnp.sum(tok, axis=0, keepdims=True)
        dx_a, dy2, dg2, dpost = vjp_r(err * (1.0 / D))
        dy2b = dy2.astype(BF16)
        dy2_ref[...] = dy2b
        dh2 = jnp.zeros((tm, D), F32)
        for n in range(nb):
            dr = _dot(dy2b, w2_ref[n], _NT)
            dz = (dr * (2.0 * jnp.maximum(z_sc[:, n * fb:(n + 1) * fb], 0.0))).astype(BF16)
            dz_ref[:, n * fb:(n + 1) * fb] = dz
            dh2 = dh2 + _dot(dz, w1_ref[n], _NT)
        dx_b, dpre, dsc, dsh = vjp_h(dh2)
        dx_ref[...] = dx_a + dx_b
        acc_ref[0:1, :] += dpre
        acc_ref[1:2, :] += dpost
        acc_ref[2:3, :] += dsc
        acc_ref[3:4, :] += dsh
        acc_ref[4:5, :] += dg2

    row = pl.BlockSpec((tm, D), lambda m: (m, 0))
    wide = pl.BlockSpec((tm, DFF), lambda m: (m, 0))
    vec = pl.BlockSpec((1, D), lambda m: (0, 0))
    return pl.pallas_call(
        body, name="ffn_forward_backward", grid=(t // tm,),
        out_shape=(jax.ShapeDtypeStruct((t, D), F32), jax.ShapeDtypeStruct((t, D), BF16),
                   jax.ShapeDtypeStruct((t, DFF), BF16), jax.ShapeDtypeStruct((t, DFF), BF16),
                   jax.ShapeDtypeStruct((t, D), BF16), jax.ShapeDtypeStruct((8, D), F32)),
        in_specs=[row, row, VSPEC, VSPEC, vec, vec, vec, vec, vec],
        out_specs=(row, row, wide, wide, row, pl.BlockSpec((8, D), lambda m: (0, 0))),
        scratch_shapes=[pltpu.VMEM((tm, DFF), F32)],
        compiler_params=_params("arbitrary"),
    )(x1, target, w1, w2, pre, sc, sh, g2, post)


def mix_backward(x, p, qe, intra, st_prev, act, wa, wb, wo, hg, g1, post, dx1, dec):
    t = x.shape[0]
    tm = _mix_tile(t)
    nc = t // CH

    def body(x_ref, og_ref, ga_ref, gb_ref, qe_ref, intra_ref, st_ref, act_ref, wa_ref, wb_ref, wo_ref, hg_ref,
             g1_ref, post_ref, dx1_ref, dec_ref,
             dp_ref, do_ref, dqe_ref, gt_ref, dact_ref, a_ref, mg_ref, dya_ref, dyb_ref, dy_ref, acc_ref, gstate):
        @pl.when(pl.program_id(0) == 0)
        def _():
            acc_ref[...] = jnp.zeros_like(acc_ref)
            gstate[...] = jnp.zeros_like(gstate)

        o, a, ya, yb, merged, y = _mix_forward_tile(x_ref, og_ref, ga_ref, gb_ref, qe_ref, intra_ref, st_ref, act_ref,
                                                    wa_ref, wb_ref, wo_ref, hg_ref, tm)
        a_ref[...] = a.astype(BF16)
        mg_ref[...] = merged.astype(BF16)
        _, vjp_r = jax.vjp(f_residual, x_ref[...], y, g1_ref[...], post_ref[...])
        _, dy, dg1, dpost = vjp_r(dx1_ref[...])
        dyb16 = dy.astype(BF16)
        dy_ref[...] = dyb16
        dmerged = _dot(dyb16, wo_ref[...], _NT)
        _, vjp_m = jax.vjp(f_merge, ga_ref[...], gb_ref[...], ya, yb)
        dga, dgb, dya, dyb = vjp_m(dmerged)
        dp_ref[:, D:2 * D] = dga.astype(BF16)
        dp_ref[:, 2 * D:3 * D] = dgb.astype(BF16)
        dya16, dyb16b = dya.astype(BF16), dyb.astype(BF16)
        dya_ref[...] = dya16
        dyb_ref[...] = dyb16b
        da = _dot(dya16, wa_ref[...], _NT)
        dact_ref[...] = _dot(dyb16b, wb_ref[...], _NT)
        _, vjp_a = jax.vjp(f_head_out, o, og_ref[...], hg_ref[...])
        do, dog, dhg = vjp_a(da)
        dp_ref[:, 0:D] = dog.astype(BF16)
        do_ref[...] = do
        do16 = do.astype(BF16)
        qe16 = qe_ref[...]
        for ci in reversed(range(tm // CH)):
            st = st_ref[ci].astype(BF16)
            rows = slice(ci * CH, (ci + 1) * CH)
            dqe, vt = [], []
            for h in range(HEADS):
                sl = slice(h * DK, (h + 1) * DK)
                dqe.append(_dot(do16[rows, sl], st[:, sl], _NN))
                vt.append(_dot(do16[rows, sl], qe16[rows, sl], _TN))
            dqe_ref[rows, :] = jnp.concatenate(dqe, axis=1)
            g = gstate[...]
            gt_ref[ci] = g
            gstate[...] = dec_ref[ci] * g + jnp.concatenate(vt, axis=1)
        acc_ref[0:1, :] += dg1
        acc_ref[1:2, :] += dpost
        acc_ref[2:3, :] += dhg
        acc_ref[3:4, :] += jnp.sum(dog, axis=0, keepdims=True)
        acc_ref[4:5, :] += jnp.sum(dga, axis=0, keepdims=True)
        acc_ref[5:6, :] += jnp.sum(dgb, axis=0, keepdims=True)

    nt = t // tm
    col, row, vec, per_chunk = _mix_specs(tm, nt)
    b16 = jax.ShapeDtypeStruct((t, D), BF16)
    f32 = jax.ShapeDtypeStruct((t, D), F32)
    return pl.pallas_call(
        body, name="mix_backward", grid=(nt,),
        out_shape=(jax.ShapeDtypeStruct((t, NDEV * D), BF16), f32, f32, jax.ShapeDtypeStruct((nc, DK, D), F32), f32,
                   b16, b16, b16, b16, b16, jax.ShapeDtypeStruct((8, D), F32)),
        in_specs=[row, col(3), col(6), col(7), row, row, per_chunk(DK), row, VSPEC, VSPEC, VSPEC, vec, vec, vec, row,
                  per_chunk(1)],
        out_specs=(pl.BlockSpec((tm, 3 * D), lambda m: (nt - 1 - m, 0)), row, row, per_chunk(DK), row, row, row, row,
                   row, row, pl.BlockSpec((8, D), lambda m: (0, 0))),
        scratch_shapes=[pltpu.VMEM((DK, D), F32)],
        compiler_params=_params("arbitrary"),
    )(x, p, p, p, qe, intra, st_prev, act, wa, wb, wo, hg, g1, post, dx1, dec)


def conv_backward(p, yc, dact, dw, ln_g, ln_b, blocks, dp):
    t = p.shape[0]
    tm = _conv_tile(t)
    per = tm // HALO
    nt = t // tm
    n = len(blocks)

    def body(*refs):
        cv_ref, cg_ref, cvp_ref, cgp_ref, yc_ref, ycn_ref, da_ref, dan_ref, dw_ref, g_ref, b_ref = refs[:11]
        dp_ref, acc_ref, ddw_ref = refs[12 + n:15 + n]
        uext, dyext, ush, dysh, ddw8, du_sc = refs[15 + 2 * n:21 + 2 * n]
        start, finish = _exchange_ops(refs[11:11 + n], refs[15 + n:15 + 2 * n], *refs[21 + 2 * n:])
        m = pl.program_id(0)

        @pl.when(m == 0)
        def _():
            start()
            acc_ref[...] = jnp.zeros_like(acc_ref)
            ddw8[...] = jnp.zeros_like(ddw8)

        cv, cg = cv_ref[...], cg_ref[...]
        u, vjp_u = jax.vjp(f_glu, cv, cg)
        uext[0:HALO, :] = jnp.where(m == 0, 0.0, f_glu(cvp_ref[...], cgp_ref[...]))
        uext[HALO:HALO + tm, :] = u
        _shift_stack(uext, ush, tm + HALO)
        _, vjp_c = jax.vjp(f_conv_act, yc_ref[...], g_ref[...], b_ref[...])
        dyc, dg, db = vjp_c(da_ref[...])
        _, vjp_n = jax.vjp(f_conv_act, ycn_ref[...], g_ref[...], b_ref[...])
        dyn = vjp_n(dan_ref[...])[0]
        dyext[0:tm, :] = dyc
        dyext[tm:tm + HALO, :] = jnp.where(m == nt - 1, 0.0, dyn)
        _shift_stack(dyext, dysh, tm + HALO)
        for lt in range(D // LANE):
            ls = slice(lt * LANE, (lt + 1) * LANE)
            du_l = jnp.zeros((tm, LANE), F32)
            for w in range(KW):
                du_l = du_l + dw_ref[w:w + 1, ls] * _shifted(dysh, KW - 1 - w, tm, ls)
            du_sc[:, ls] = du_l
            dyc_l = dyext[0:tm, ls]
            for w in range(KW):
                prod = dyc_l * _shifted(ush, HALO - KW + 1 + w, tm, ls)
                ddw8[w, :, ls] += jnp.sum(prod.reshape(tm // 8, 8, LANE), axis=0)
        du = du_sc[...]

        @pl.when(m == nt - 1)
        def _():
            ddw_ref[...] = jnp.sum(ddw8[...], axis=1)

        dcv, dcg = vjp_u(du)
        dp_ref[:, 0:D] = dcv.astype(BF16)
        dp_ref[:, D:2 * D] = dcg.astype(BF16)
        acc_ref[0:1, :] += jnp.sum(dyc, axis=0, keepdims=True)
        acc_ref[1:2, :] += dg
        acc_ref[2:3, :] += db
        acc_ref[3:4, :] += jnp.sum(dcv, axis=0, keepdims=True)
        acc_ref[4:5, :] += jnp.sum(dcg, axis=0, keepdims=True)
        pl.when(m == nt - 1)(finish)

    vec = pl.BlockSpec((1, D), lambda m: (0, 0))
    row = pl.BlockSpec((tm, D), lambda m: (m, 0))
    prev = lambda j: pl.BlockSpec((HALO, D), lambda m: (jnp.maximum(m * per - 1, 0), j))
    nxt = pl.BlockSpec((HALO, D), lambda m: (jnp.minimum((m + 1) * per, t // HALO - 1), 0))
    res = pl.pallas_call(
        body, name="conv_backward", grid=(nt,),
        out_shape=(jax.ShapeDtypeStruct(dp.shape, dp.dtype), jax.ShapeDtypeStruct((8, D), F32),
                   jax.ShapeDtypeStruct((HALO, D), F32)) + _exchanged(blocks),
        in_specs=[pl.BlockSpec((tm, D), lambda m: (m, 4)), pl.BlockSpec((tm, D), lambda m: (m, 5)), prev(4), prev(5),
                  row, nxt, row, nxt, pl.BlockSpec((HALO, D), lambda m: (0, 0)), vec, vec] + [ANYSPEC] * (n + 1),
        out_specs=(pl.BlockSpec((tm, 2 * D), lambda m: (m, 3)), pl.BlockSpec((8, D), lambda m: (0, 0)),
                   pl.BlockSpec((HALO, D), lambda m: (0, 0))) + (ANYSPEC,) * n,
        scratch_shapes=[pltpu.VMEM((HALO + tm, D), F32), pltpu.VMEM((tm + HALO, D), F32),
                        pltpu.VMEM((8, tm + HALO, D), F32), pltpu.VMEM((8, tm + HALO, D), F32),
                        pltpu.VMEM((HALO, 8, D), F32), pltpu.VMEM((tm, D), F32)] + _comm_sems(n, False),
        input_output_aliases={11 + n: 0},
        compiler_params=_params("arbitrary"),
    )(p, p, p, p, yc, yc, dact, dact, dw, ln_g, ln_b, *blocks, dp)
    return res[:3] + (res[3:],)


def hgrn_backward(p, logits, do, dqe, gt, st_prev, blocks, dp):
    t = p.shape[0]
    nc = t // CH
    n = len(blocks)

    def body(*refs):
        q_ref, f_ref, v_ref, l_ref, do_ref, dqe_ref, gt_ref, st_ref = refs[:8]
        dp_ref, acc_ref = refs[9 + n:11 + n]
        start, finish = _exchange_ops(refs[8:8 + n], refs[11 + n:11 + 2 * n], *refs[11 + 2 * n:])

        @pl.when(pl.program_id(0) == 0)
        def _():
            start()
            acc_ref[...] = jnp.zeros_like(acc_ref)

        gt_v = gt_ref[...]
        ddec = jnp.sum(gt_v * st_ref[...], axis=0, keepdims=True)
        _, vjp = jax.vjp(f_hgrn_chunk, q_ref[...], f_ref[...], v_ref[...], l_ref[...])
        dq, df, dv, dl = vjp((do_ref[...], dqe_ref[...], gt_v, ddec))
        dp_ref[:, 0:D] = dq.astype(BF16)
        dp_ref[:, D:2 * D] = df.astype(BF16)
        dp_ref[:, 2 * D:3 * D] = dv.astype(BF16)
        acc_ref[0:2, :] += dl
        acc_ref[2:3, :] += jnp.sum(dq, axis=0, keepdims=True)
        acc_ref[3:4, :] += jnp.sum(df, axis=0, keepdims=True)
        acc_ref[4:5, :] += jnp.sum(dv, axis=0, keepdims=True)
        pl.when(pl.program_id(0) == nc - 1)(finish)

    col = lambda j: pl.BlockSpec((CH, D), lambda c: (c, j))
    row = pl.BlockSpec((CH, D), lambda c: (c, 0))
    stspec = pl.BlockSpec((None, DK, D), lambda c: (c, 0, 0))
    res = pl.pallas_call(
        body, name="hgrn_backward", grid=(nc,),
        out_shape=(jax.ShapeDtypeStruct(dp.shape, dp.dtype), jax.ShapeDtypeStruct((8, D), F32)) + _exchanged(blocks),
        in_specs=[col(0), col(1), col(2), pl.BlockSpec((2, D), lambda c: (0, 0)), row, row, stspec, stspec]
        + [ANYSPEC] * (n + 1),
        out_specs=(pl.BlockSpec((CH, 3 * D), lambda c: (c, 1)), pl.BlockSpec((8, D), lambda c: (0, 0)))
        + (ANYSPEC,) * n,
        scratch_shapes=_comm_sems(n, False),
        input_output_aliases={8 + n: 0},
        compiler_params=_params("arbitrary"),
    )(p, p, p, logits, do, dqe, gt, st_prev, *blocks, dp)
    return res[:2] + (res[2:],)


def in_proj_backward(dp, w_all, x, dx1, gain, sc, sh, blocks):
    t = x.shape[0]
    tm = min(256, t)
    nt = t // tm
    n = len(blocks)

    def body(*refs):
        dp_ref, w_ref, x_ref, dx1_ref, g_ref, sc_ref, sh_ref = refs[:7]
        gx_ref, acc_ref = refs[7 + n:9 + n]
        start = finish = lambda: None
        if n:
            start, finish = _exchange_ops(refs[7:7 + n], refs[9 + n:9 + 2 * n], *refs[9 + 2 * n:])

        @pl.when(pl.program_id(0) == 0)
        def _():
            start()
            acc_ref[...] = jnp.zeros_like(acc_ref)

        dh = jnp.zeros((tm, D), F32)
        for j in range(NDEV):
            dh = dh + _dot(dp_ref[:, j * D:(j + 1) * D], w_ref[DP_SPLIT[j]], _NT)
        _, vjp_h = jax.vjp(f_modulate, x_ref[...], g_ref[...], sc_ref[...], sh_ref[...])
        dx, dg, dsc, dsh = vjp_h(dh)
        gx_ref[...] = dx1_ref[...] + dx
        acc_ref[0:1, :] += dg
        acc_ref[1:2, :] += dsc
        acc_ref[2:3, :] += dsh
        pl.when(pl.program_id(0) == nt - 1)(finish)

    row = pl.BlockSpec((tm, D), lambda m: (m, 0))
    vec = pl.BlockSpec((1, D), lambda m: (0, 0))
    res = pl.pallas_call(
        body, name="in_proj_backward", grid=(nt,),
        out_shape=(jax.ShapeDtypeStruct((t, D), F32), jax.ShapeDtypeStruct((8, D), F32)) + _exchanged(blocks),
        in_specs=[pl.BlockSpec((tm, NDEV * D), lambda m: (m, 0)), VSPEC, row, row, vec, vec, vec] + [ANYSPEC] * n,
        out_specs=(row, pl.BlockSpec((8, D), lambda m: (0, 0))) + (ANYSPEC,) * n,
        scratch_shapes=_comm_sems(n, False) if n else [],
        compiler_params=_params("arbitrary"),
    )(dp, w_all, x, dx1, gain, sc, sh, *blocks)
    return res[0], res[1], res[2:]


def _dp_column(n):
    return jnp.where(n < 3, n + 3, jnp.where(n == 3, 0, jnp.where(n < 6, n + 2, n - 5)))


def weight_grad(a, b, nblk, ka, bn, a_blocked, name, b_col=lambda n: n):
    t = a.shape[0]
    tk = min(2048, t)
    nk = t // tk

    def body(a_ref, b_ref, f_ref, h_ref, acc):
        k = pl.program_id(1)

        @pl.when(k == 0)
        def _():
            acc[...] = jnp.zeros_like(acc)

        acc[...] += _dot(a_ref[...], b_ref[...], _TN)

        @pl.when(k == nk - 1)
        def _():
            f_ref[...] = acc[...]
            h_ref[...] = acc[...].astype(BF16)

    a_idx = (lambda n, k: (k, n)) if a_blocked else (lambda n, k: (k, 0))
    b_idx = (lambda n, k: (k, 0)) if a_blocked else (lambda n, k: (k, b_col(n)))
    out = pl.BlockSpec((None, ka, bn), lambda n, k: (n, 0, 0))
    return pl.pallas_call(
        body, name=name, grid=(nblk, nk),
        out_shape=(jax.ShapeDtypeStruct((nblk, ka, bn), F32), jax.ShapeDtypeStruct((nblk, ka, bn), BF16)),
        in_specs=[pl.BlockSpec((tk, ka), a_idx), pl.BlockSpec((tk, bn), b_idx)],
        out_specs=(out, out),
        scratch_shapes=[pltpu.VMEM((ka, bn), F32)],
        compiler_params=_params("parallel", "arbitrary"),
    )(a, b)


def ada_backward(call_t, dmod_cols, w, m, v):
    def body(c_ref, d_ref, w_ref, m_ref, v_ref, g_ref, dl_ref, nm_ref, nv_ref):
        ct, dm = c_ref[...], d_ref[...]
        g = ct[:, 0:1] * dm[0:1, :]
        for r in range(1, NDEV):
            g = g + ct[:, r:r + 1] * dm[r:r + 1, :]
        g_ref[...] = g
        dl_ref[...], nm_ref[...], nv_ref[...] = _adamw(w_ref[...], g, m_ref[...], v_ref[...])

    s = jax.ShapeDtypeStruct(w.shape, F32)
    return pl.pallas_call(
        body, name="ada_backward", out_shape=(s, s, s, s),
        in_specs=[VSPEC] * 5, out_specs=(VSPEC,) * 4, compiler_params=_params(),
    )(call_t, dmod_cols, w, m, v)


def adamw_small(total, ddw_mine, recipes, ws, ms, vs):
    n = len(ws)

    def body(*refs):
        tot, ddw = refs[0], refs[1]
        w_refs, m_refs, v_refs = refs[2:2 + n], refs[2 + n:2 + 2 * n], refs[2 + 2 * n:2 + 3 * n]
        outs = refs[2 + 3 * n:2 + 7 * n]
        loss_ref = refs[2 + 7 * n]
        for i, rec in enumerate(recipes):
            if rec == "dw":
                g = ddw[...]
            elif isinstance(rec, tuple):
                g = tot[rec[0]:rec[1], :]
            else:
                g = jnp.concatenate([tot[r:r + 1, :] for r in rec], axis=1) if len(rec) > 1 else tot[rec[0]:rec[0] + 1, :]
            dl, nm, nv = _adamw(w_refs[i][...], g, m_refs[i][...], v_refs[i][...])
            outs[4 * i][...] = g
            outs[4 * i + 1][...] = dl
            outs[4 * i + 2][...] = nm
            outs[4 * i + 3][...] = nv
        loss_ref[...] = tot[LOSS_ROW:LOSS_ROW + 1, 0:128]

    shapes = []
    for w in ws:
        shapes += [jax.ShapeDtypeStruct(w.shape, F32)] * 4
    res = pl.pallas_call(
        body, name="adamw_small", out_shape=tuple(shapes) + (jax.ShapeDtypeStruct((1, 128), F32),),
        in_specs=[VSPEC] * (2 + 3 * n), out_specs=(VSPEC,) * (4 * n + 1), compiler_params=_params(),
    )(total, ddw_mine, *ws, *ms, *vs)
    return [res[4 * i:4 * i + 4] for i in range(n)], res[4 * n]


def reduce_and_adamw(me, w, g_all, g_recv, m, v, name):
    r, c = w.shape
    br = min(128, r)

    def body(me_ref, w_ref, go_ref, gr_ref, m_ref, v_ref, g_ref, dl_ref, nm_ref, nv_ref):
        g = go_ref[...]
        for k in range(NDEV - 1):
            g = g + gr_ref[k].astype(F32)
        g_ref[...] = g
        dl_ref[...], nm_ref[...], nv_ref[...] = _adamw(w_ref[...], g, m_ref[...], v_ref[...])

    blk = pl.BlockSpec((br, c), lambda i, me_ref: (i, 0))
    s = jax.ShapeDtypeStruct(w.shape, F32)
    return pl.pallas_call(
        body, name=name, out_shape=(s, s, s, s),
        grid_spec=pltpu.PrefetchScalarGridSpec(
            num_scalar_prefetch=1, grid=(r // br,),
            in_specs=[blk, pl.BlockSpec((None, br, c), lambda i, me_ref: (me_ref[0], i, 0)),
                      pl.BlockSpec((NDEV - 1, br, c), lambda i, me_ref: (0, i, 0)), blk, blk],
            out_specs=(blk, blk, blk, blk)),
        compiler_params=_params("parallel"),
    )(me, w, g_all, g_recv, m, v)


def kernel(x, c, w_ada, b_ada, pre_norm_tm, post_norm_tm, pre_norm_cm, post_norm_cm, w_in, b_in, hg_lb_logits, hg_norm, conv_dw, conv_db, conv_ln_g, conv_ln_b, w_br_a, w_br_b, w_out, w_ff1, w_ff2, loss_target, m_w_ada, m_b_ada, m_pre_norm_tm, m_post_norm_tm, m_pre_norm_cm, m_post_norm_cm, m_w_in, m_b_in, m_hg_lb_logits, m_hg_norm, m_conv_dw, m_conv_db, m_conv_ln_g, m_conv_ln_b, m_w_br_a, m_w_br_b, m_w_out, m_w_ff1, m_w_ff2, v_w_ada, v_b_ada, v_pre_norm_tm, v_post_norm_tm, v_pre_norm_cm, v_post_norm_cm, v_w_in, v_b_in, v_hg_lb_logits, v_hg_norm, v_conv_dw, v_conv_db, v_conv_ln_g, v_conv_ln_b, v_w_br_a, v_w_br_b, v_w_out, v_w_ff1, v_w_ff2):
    t = x.shape[1]
    me = 4 * lax.axis_index("x") + 2 * lax.axis_index("y") + lax.axis_index("c")
    xs = x[0]
    tgt = loss_target[0]

    mod, call = ada_forward(c, w_ada[0], b_ada)
    sh1, sc1, g1, sh2, sc2, g2 = [mod[:, i * D:(i + 1) * D] for i in range(6)]
    win_all, dw_all = all_gather_blocks([w_in[0].astype(BF16), conv_dw[0]])
    dw = jnp.pad(dw_all.transpose(1, 0, 2).reshape(KW, D), ((0, HALO - KW), (0, 0)))

    p, h, (w1_all,) = in_proj(xs, pre_norm_tm, sc1, sh1, win_all, b_in, [w_ff1[0].astype(BF16)])
    intra, qe, st_prev, dec, (w2_all,) = hgrn_local(p, hg_lb_logits, [w_ff2[0].astype(BF16)])
    yc, act, (wa_all, wb_all, wo_all) = conv_forward(
        p, dw, conv_db, conv_ln_g, conv_ln_b, [w_br_a[0].astype(BF16), w_br_b[0].astype(BF16), w_out[0].astype(BF16)])
    wa, wb, wo = wa_all.reshape(D, D), wb_all.reshape(D, D), wo_all.reshape(D, D)
    x1 = mix_forward(xs, p, qe, intra, st_prev, act, wa, wb, wo, hg_norm, g1, post_norm_tm)

    dx1, h2, r, dz, dy2, acc_ffn = ffn_forward_backward(x1, tgt, w1_all, w2_all, pre_norm_cm, sc2, sh2, g2,
                                                        post_norm_cm)
    rows = D // NDEV
    g1_f, g1_h = weight_grad(h2, dz, NDEV, D, DFF // NDEV, False, "grad_w_ff1")
    g2_f, g2_h = weight_grad(r, dy2, 4, D, D, True, "grad_w_ff2")
    g2_f, g2_h = g2_f.reshape(NDEV, DFF // NDEV, D), g2_h.reshape(NDEV, DFF // NDEV, D)

    (dp, do, dqe, gt, dact, a16, mg16, dya, dyb, dy, acc_mix) = mix_backward(
        xs, p, qe, intra, st_prev, act, wa, wb, wo, hg_norm, g1, post_norm_tm, dx1, dec)
    dp, acc_hg, (r_ff1, r_ff2) = hgrn_backward(p, hg_lb_logits, do, dqe, gt, st_prev, [g1_h, g2_h], dp)
    ga_f, ga_h = weight_grad(a16, dya, 1, D, D, False, "grad_w_br_a")
    gb_f, gb_h = weight_grad(act, dyb, 1, D, D, False, "grad_w_br_b")
    go_f, go_h = weight_grad(mg16, dy, 1, D, D, False, "grad_w_out")
    dp, acc_conv, ddw, (r_a, r_b, r_o) = conv_backward(
        p, yc, dact, dw, conv_ln_g, conv_ln_b,
        [ga_h.reshape(NDEV, rows, D), gb_h.reshape(NDEV, rows, D), go_h.reshape(NDEV, rows, D)], dp)
    gin_f, gin_h = weight_grad(h, dp, NDEV, D, D, False, "grad_w_in", _dp_column)
    in_send, in_recv, gin_thru, in_landing, token = exchange_start(gin_h)
    grad_x, acc_in, _ = in_proj_backward(dp, win_all, xs, dx1, pre_norm_tm, sc1, sh1 + token[0:1, 0:1], [])

    own = [ga_f.reshape(NDEV, rows, D), gb_f.reshape(NDEV, rows, D), go_f.reshape(NDEV, rows, D), g1_f, g2_f]
    recv = [r_a, r_b, r_o, r_ff1, r_ff2]
    big = {}
    names = ["w_br_a", "w_br_b", "w_out", "w_ff1", "w_ff2"]
    ws = [w_br_a, w_br_b, w_out, w_ff1, w_ff2]
    ms = [m_w_br_a, m_w_br_b, m_w_out, m_w_ff1, m_w_ff2]
    vs = [v_w_br_a, v_w_br_b, v_w_out, v_w_ff1, v_w_ff2]
    me1 = me.astype(jnp.int32).reshape(1)
    for i, nm in enumerate(names):
        big[nm] = [o[None] for o in reduce_and_adamw(me1, ws[i][0], own[i], recv[i], ms[i][0], vs[i][0], "adamw_" + nm)]
    big, grad_x, acc_in = lax.optimization_barrier((big, grad_x, acc_in))
    r_in = exchange_wait(in_send, in_recv, gin_thru, in_landing, acc_in)
    r_in, acc_in = lax.optimization_barrier((r_in, acc_in))
    big["w_in"] = [o[None] for o in reduce_and_adamw(me1, w_in[0], gin_f, r_in, m_w_in[0], v_w_in[0], "adamw_w_in")]

    dmod_rows = [2, 1, 8, 19, 18, 20]
    total, kept = gather_and_sum_rows([acc_in, acc_mix, acc_ffn, acc_hg, acc_conv, ddw], dmod_rows)
    dmod_all = kept[:, 0:6, :].reshape(NDEV, 6 * D)
    wcols = w_ada.shape[2]
    gwa, dwa, nmwa, nvwa = ada_backward(call.T, lax.dynamic_slice_in_dim(dmod_all, me * wcols, wcols, axis=1),
                                        w_ada[0], m_w_ada[0], v_w_ada[0])
    ddw_mine = lax.dynamic_slice_in_dim(total[40:40 + KW], me * (D // NDEV), D // NDEV, axis=1)

    small_names = ["b_ada", "pre_norm_tm", "post_norm_tm", "pre_norm_cm", "post_norm_cm", "b_in", "hg_lb_logits",
                   "hg_norm", "conv_db", "conv_ln_g", "conv_ln_b", "conv_dw"]
    recipes = [dmod_rows, [0], [9], [16], [17], [26, 27, 28, 11, 35, 36, 12, 13], (24, 26), [10], [32], [33], [34],
               "dw"]
    small_w = [b_ada, pre_norm_tm, post_norm_tm, pre_norm_cm, post_norm_cm, b_in, hg_lb_logits, hg_norm, conv_db,
               conv_ln_g, conv_ln_b, conv_dw[0]]
    small_m = [m_b_ada, m_pre_norm_tm, m_post_norm_tm, m_pre_norm_cm, m_post_norm_cm, m_b_in, m_hg_lb_logits,
               m_hg_norm, m_conv_db, m_conv_ln_g, m_conv_ln_b, m_conv_dw[0]]
    small_v = [v_b_ada, v_pre_norm_tm, v_post_norm_tm, v_pre_norm_cm, v_post_norm_cm, v_b_in, v_hg_lb_logits,
               v_hg_norm, v_conv_db, v_conv_ln_g, v_conv_ln_b, v_conv_dw[0]]
    small_out, loss_row = adamw_small(total, ddw_mine, recipes, small_w, small_m, small_v)
    loss = loss_row[0, 0]
    sm = {nm: list(o) for nm, o in zip(small_names, small_out)}
    sm["conv_dw"] = [o[None] for o in sm["conv_dw"]]

    order = ["w_ada", "b_ada", "pre_norm_tm", "post_norm_tm", "pre_norm_cm", "post_norm_cm", "w_in", "b_in",
             "hg_lb_logits", "hg_norm", "conv_dw", "conv_db", "conv_ln_g", "conv_ln_b", "w_br_a", "w_br_b", "w_out",
             "w_ff1", "w_ff2"]
    res = dict(sm)
    res.update(big)
    res["w_ada"] = [gwa[None], dwa[None], nmwa[None], nvwa[None]]
    outs = [loss, grad_x[None]]
    for j in range(4):
        outs += [res[nm][j] for nm in order]
    return tuple(outs)
```

```python
import functools

import jax
import jax.numpy as jnp
from jax import lax
from jax.experimental import pallas as pl
from jax.experimental.pallas import tpu as pltpu

F32 = jnp.float32
BF16 = jnp.bfloat16
MESH = pl.DeviceIdType.MESH
AXES = ("x", "y", "c")

D = 1024
HEADS = 8
DK = 128
LANE = 128
CH = 128
LEVELS = 7
KW = 31
HALO = 32
DFF = 4096
NDEV = 8
EPS = 1e-6
DP_SPLIT = (3, 6, 7, 0, 1, 2, 4, 5)
LOSS_ROW = 21
ADAM_LR, ADAM_B1, ADAM_B2, ADAM_EPS, ADAM_WD, ADAM_STEP = 0.001, 0.9, 0.999, 1e-08, 0.01, 10
VMEM_LIMIT = 58 * 1024 * 1024

_NN = (((1,), (0,)), ((), ()))
_NT = (((1,), (1,)), ((), ()))
_TN = (((0,), (0,)), ((), ()))

VSPEC = pl.BlockSpec(memory_space=pltpu.VMEM)
ANYSPEC = pl.BlockSpec(memory_space=pl.ANY)


def _params(*sem):
    return pltpu.CompilerParams(dimension_semantics=sem or None, vmem_limit_bytes=VMEM_LIMIT)


def _dot(a, b, dims):
    return lax.dot_general(a, b, dims, preferred_element_type=F32)


@jax.custom_vjp
def mm_nn(a, b):
    return _dot(a.astype(BF16), b.astype(BF16), _NN)


def _mm_nn_fwd(a, b):
    ab, bb = a.astype(BF16), b.astype(BF16)
    return _dot(ab, bb, _NN), (ab, bb)


def _mm_nn_bwd(res, ct):
    ab, bb = res
    cb = ct.astype(BF16)
    return _dot(cb, bb, _NT), _dot(ab, cb, _TN)


mm_nn.defvjp(_mm_nn_fwd, _mm_nn_bwd)


@jax.custom_vjp
def mm_nt(a, b):
    return _dot(a.astype(BF16), b.astype(BF16), _NT)


def _mm_nt_fwd(a, b):
    ab, bb = a.astype(BF16), b.astype(BF16)
    return _dot(ab, bb, _NT), (ab, bb)


def _mm_nt_bwd(res, ct):
    ab, bb = res
    cb = ct.astype(BF16)
    return _dot(cb, bb, _NN), _dot(cb, ab, _TN)


mm_nt.defvjp(_mm_nt_fwd, _mm_nt_bwd)


@jax.custom_vjp
def mm_tn(a, b):
    return _dot(a.astype(BF16), b.astype(BF16), _TN)


def _mm_tn_fwd(a, b):
    ab, bb = a.astype(BF16), b.astype(BF16)
    return _dot(ab, bb, _TN), (ab, bb)


def _mm_tn_bwd(res, ct):
    ab, bb = res
    cb = ct.astype(BF16)
    return _dot(bb, cb, _NT), _dot(ab, cb, _NN)


mm_tn.defvjp(_mm_tn_fwd, _mm_tn_bwd)


def _rms(x):
    return x * lax.rsqrt(jnp.mean(x * x, axis=-1, keepdims=True) + EPS)


def _silu(x):
    return x * jax.nn.sigmoid(x)


def f_modulate(x, gain, sc, sh):
    return _rms(x) * gain * (1.0 + sc) + sh


def f_residual(x, y, gate, gain):
    return x + gate * (_rms(y) * gain)


def f_merge(ga, gb, ya, yb):
    return jax.nn.sigmoid(ga) * ya + jax.nn.sigmoid(gb) * yb


def f_head_out(o, og, hg):
    heads = [_rms(o[:, h * DK:(h + 1) * DK]) for h in range(HEADS)]
    return jnp.concatenate(heads, axis=1) * hg * _silu(og)


def f_conv_act(u, g, b):
    mu = jnp.mean(u, axis=-1, keepdims=True)
    var = jnp.mean(jnp.square(u - mu), axis=-1, keepdims=True)
    return _silu((u - mu) * lax.rsqrt(var + EPS) * g + b)


def f_glu(cv, cg):
    return cv * jax.nn.sigmoid(cg)


def _split2(x):
    hi = x.astype(BF16)
    return hi, (x - hi.astype(F32)).astype(BF16)


def _tri(transposed):
    i = lax.broadcasted_iota(jnp.int32, (CH, CH), 1 if transposed else 0)
    t = lax.broadcasted_iota(jnp.int32, (CH, CH), 0 if transposed else 1)
    return jnp.where(t <= i, 1.0, 0.0).astype(BF16)


def _blocks3(x, rows):
    return x.reshape(CH // rows, rows, x.shape[-1])


def _mid_broadcast(b, lev):
    h = 1 << (LEVELS - 1 - lev)
    if h >= 4:
        x3 = _blocks3(b, 2 * h)
        return jnp.broadcast_to(x3[:, h - 1:h, :], x3.shape).reshape(b.shape)
    x3 = _blocks3(b, 8)
    sub = lax.broadcasted_iota(jnp.int32, x3.shape, 1)
    out = None
    for first in range(0, 8, 2 * h):
        piece = jnp.broadcast_to(x3[:, first + h - 1:first + h, :], x3.shape)
        out = piece if out is None else jnp.where(sub >= first, piece, out)
    return out.reshape(b.shape)


def _mid_scatter(d, lev):
    h = 1 << (LEVELS - 1 - lev)
    if h >= 4:
        x3 = _blocks3(d, 2 * h)
        row = lax.broadcasted_iota(jnp.int32, x3.shape, 1)
        total = jnp.sum(x3, axis=1, keepdims=True)
        return jnp.where(row == h - 1, total, 0.0).reshape(d.shape)
    x3 = _blocks3(d, 8)
    sub = lax.broadcasted_iota(jnp.int32, x3.shape, 1)
    out = jnp.zeros_like(x3)
    for first in range(0, 8, 2 * h):
        inside = (sub >= first) & (sub < first + 2 * h)
        total = jnp.sum(jnp.where(inside, x3, 0.0), axis=1, keepdims=True)
        out = jnp.where(sub == first + h - 1, total, out)
    return out.reshape(d.shape)


@jax.custom_vjp
def decay_sums(g):
    hi, lo = _split2(g)
    tri = _tri(False)
    b = _dot(tri, hi, _NN) + _dot(tri, lo, _NN)
    return (b,) + tuple(b - _mid_broadcast(b, lev) for lev in range(LEVELS))


def _decay_sums_fwd(g):
    return decay_sums(g), None


def _decay_sums_bwd(_, cts):
    db = cts[0]
    for lev in range(LEVELS):
        db = db + cts[1 + lev] - _mid_scatter(cts[1 + lev], lev)
    hi, lo = _split2(db)
    tri = _tri(True)
    return (_dot(tri, hi, _NN) + _dot(tri, lo, _NN),)


decay_sums.defvjp(_decay_sums_fwd, _decay_sums_bwd)


def _score_masks():
    i = lax.broadcasted_iota(jnp.int32, (CH, CH), 0)
    j = lax.broadcasted_iota(jnp.int32, (CH, CH), 1)
    masks = [i == j]
    for lev in range(LEVELS):
        sh = LEVELS - 1 - lev
        same = (i >> (sh + 1)) == (j >> (sh + 1))
        masks.append(same & (((i >> sh) & 1) == 1) & (((j >> sh) & 1) == 0))
    return [jnp.where(m, 1.0, 0.0) for m in masks]


def f_hgrn_chunk(q_r, f_r, v, logits):
    l0, l1 = logits[0:1, :], logits[1:2, :]
    mx = lax.stop_gradient(jnp.maximum(l0, l1))
    e0, e1 = jnp.exp(l0 - mx), jnp.exp(l1 - mx)
    lb = e0 / (e0 + e1)
    q = _silu(q_r)
    f = lb + (1.0 - lb) * jax.nn.sigmoid(f_r)
    k = 1.0 - f
    sums = decay_sums(jnp.log(f))
    b = sums[0]
    btot = b[CH - 1:CH, :]
    qe = q * jnp.exp(b)
    ke = k * jnp.exp(btot - b)
    dec = jnp.exp(btot)
    qs, ks = [q], [k]
    row = lax.broadcasted_iota(jnp.int32, b.shape, 0)
    for lev in range(LEVELS):
        upper = ((row >> (LEVELS - 1 - lev)) & 1) == 1
        e = sums[1 + lev]
        qs.append(q * jnp.exp(jnp.where(upper, e, 0.0)))
        ks.append(k * jnp.exp(jnp.where(upper, 0.0, -e)))
    masks = _score_masks()
    intra, ut = [], []
    for h in range(HEADS):
        sl = slice(h * DK, (h + 1) * DK)
        sc = None
        for lev in range(LEVELS + 1):
            s = mm_nt(qs[lev][:, sl], ks[lev][:, sl]) * masks[lev]
            sc = s if sc is None else sc + s
        intra.append(mm_nn(sc, v[:, sl]))
        ut.append(mm_tn(v[:, sl], ke[:, sl]))
    return jnp.concatenate(intra, axis=1), qe, jnp.concatenate(ut, axis=1), dec


def _inter(qe_b, st_ref, rows):
    out = []
    for ci in range(rows // CH):
        st = st_ref[ci].astype(BF16)
        heads = [_dot(qe_b[ci * CH:(ci + 1) * CH, h * DK:(h + 1) * DK], st[:, h * DK:(h + 1) * DK], _NT)
                 for h in range(HEADS)]
        out.append(jnp.concatenate(heads, axis=1))
    return jnp.concatenate(out, axis=0)


def _shift_stack(src, dst, rows):
    dst[0, 0:rows, :] = src[0:rows, :]
    for b in range(1, 8):
        dst[b, 0:rows - 8, :] = src[pl.ds(b, rows - 8), :]


def _shifted(stack, offset, rows, lanes=slice(None)):
    return stack[offset % 8, pl.ds(8 * (offset // 8), rows), lanes]


def _adamw(w, g, m, v):
    m = ADAM_B1 * m + (1.0 - ADAM_B1) * g
    v = ADAM_B2 * v + (1.0 - ADAM_B2) * jnp.square(g)
    m_hat = m / (1.0 - ADAM_B1 ** ADAM_STEP)
    v_hat = v / (1.0 - ADAM_B2 ** ADAM_STEP)
    delta = -ADAM_LR * (m_hat / (jnp.sqrt(v_hat) + ADAM_EPS) + ADAM_WD * w)
    return delta, m, v


def _me():
    return lax.axis_index("x"), lax.axis_index("y"), lax.axis_index("c")


def _peer(k):
    x, y, c = _me()
    mask = k + 1
    px = (1 - x) if (mask >> 2) & 1 else x
    py = (1 - y) if (mask >> 1) & 1 else y
    pc = (1 - c) if mask & 1 else c
    return (px, py, pc), 4 * px + 2 * py + pc


def ada_forward(c, w_ada, b_ada, dw):
    wcols = w_ada.shape[1]

    def body(c_ref, w_ref, b_ref, dw_ref, mod_ref, call_ref, dwall_ref, part_ref, modp_ref, send_sems, recv_sems):
        x, y, cc = _me()
        me = 4 * x + 2 * y + cc
        call_ref[me] = _silu(c_ref[...])
        dwall_ref[me] = dw_ref[...]
        sends = []
        for k in range(NDEV - 1):
            dev, pidx = _peer(k)
            cp = pltpu.make_async_remote_copy(dwall_ref.at[me], dwall_ref.at[me], send_sems.at[2 * (NDEV - 1) + k],
                                              recv_sems.at[2 * (NDEV - 1) + k], device_id=dev, device_id_type=MESH)
            cp.start()
            sends.append(cp)
        for k in range(NDEV - 1):
            dev, _ = _peer(k)
            cp = pltpu.make_async_remote_copy(call_ref.at[me], call_ref.at[me], send_sems.at[k], recv_sems.at[k],
                                              device_id=dev, device_id_type=MESH)
            cp.start()
            sends.append(cp)
        for k in range(NDEV - 1):
            _, pidx = _peer(k)
            pltpu.make_async_remote_copy(call_ref.at[pidx], call_ref.at[pidx], send_sems.at[k], recv_sems.at[k],
                                         device_id=_peer(k)[0], device_id_type=MESH).wait_recv()
        call = jnp.concatenate([call_ref[r] for r in range(NDEV)], axis=0)
        part = _dot(call.astype(BF16), w_ref[...].astype(BF16), _NN)
        for r in range(NDEV):
            part_ref[r] = part[r:r + 1, :]
        modp_ref[me] = part_ref[me]
        for k in range(NDEV - 1):
            dev, pidx = _peer(k)
            cp = pltpu.make_async_remote_copy(part_ref.at[pidx], modp_ref.at[me], send_sems.at[NDEV - 1 + k],
                                              recv_sems.at[NDEV - 1 + k], device_id=dev, device_id_type=MESH)
            cp.start()
            sends.append(cp)
        for k in range(NDEV - 1):
            dev, pidx = _peer(k)
            pltpu.make_async_remote_copy(part_ref.at[pidx], modp_ref.at[pidx], send_sems.at[NDEV - 1 + k],
                                         recv_sems.at[NDEV - 1 + k], device_id=dev, device_id_type=MESH).wait_recv()
        for k in range(NDEV - 1):
            dev, pidx = _peer(k)
            pltpu.make_async_remote_copy(dwall_ref.at[pidx], dwall_ref.at[pidx], send_sems.at[2 * (NDEV - 1) + k],
                                         recv_sems.at[2 * (NDEV - 1) + k], device_id=dev,
                                         device_id_type=MESH).wait_recv()
        for cp in sends:
            cp.wait_send()
        mod_ref[...] = jnp.concatenate([modp_ref[r] for r in range(NDEV)], axis=1) + b_ref[...]

    mod, call, dw_all = pl.pallas_call(
        body, name="ada_forward",
        out_shape=(jax.ShapeDtypeStruct((1, NDEV * wcols), F32), jax.ShapeDtypeStruct((NDEV, 1, D), F32),
                   jax.ShapeDtypeStruct((NDEV,) + dw.shape, F32)),
        in_specs=[VSPEC] * 4, out_specs=(VSPEC,) * 3,
        scratch_shapes=[pltpu.VMEM((NDEV, 1, wcols), F32), pltpu.VMEM((NDEV, 1, wcols), F32),
                        pltpu.SemaphoreType.DMA((3 * (NDEV - 1),)), pltpu.SemaphoreType.DMA((3 * (NDEV - 1),))],
        compiler_params=_params(),
    )(c, w_ada, b_ada, dw)
    return mod, call.reshape(NDEV, D), dw_all


def _comm_sems(n, local):
    sems = [pltpu.SemaphoreType.DMA((7 * n,)), pltpu.SemaphoreType.DMA((7 * n,))]
    return sems + ([pltpu.SemaphoreType.DMA((n,))] if local else [])


def _gather2_ops(ins, outs, send_sems, recv_sems, local_sems):
    n = len(ins)
    x, y, c = _me()
    me, sibling = (x, y, c), (x, y, 1 - c)
    chips = [(1 - x, y), (x, 1 - y), (1 - x, 1 - y)]

    def slot(p):
        return 4 * p[0] + 2 * p[1] + p[2]

    def copy(a, k, block, to, src=None):
        return pltpu.make_async_remote_copy(
            src_ref=outs[a].at[slot(block)] if src is None else src, dst_ref=outs[a].at[slot(block)],
            send_sem=send_sems.at[a * 7 + k], recv_sem=recv_sems.at[a * 7 + k], device_id=to, device_id_type=MESH)

    def local(a):
        return pltpu.make_async_copy(ins[a], outs[a].at[slot(me)], local_sems.at[a])

    def first(a):
        return [copy(a, 0, me, sibling, src=ins[a])] + [copy(a, 1 + j, me, (*chip, c), src=ins[a])
                                                        for j, chip in enumerate(chips)]

    def passed(a):
        return [copy(a, 4 + j, (*chip, c), sibling) for j, chip in enumerate(chips)]

    def start():
        for a in range(n):
            local(a).start()
            for cp in first(a):
                cp.start()

    def forward():
        for j, chip in enumerate(chips):
            for a in range(n):
                copy(a, 1 + j, (*chip, c), me).wait_recv()
                passed(a)[j].start()

    def finish():
        for a in range(n):
            copy(a, 0, sibling, me).wait_recv()
            for j, chip in enumerate(chips):
                copy(a, 4 + j, (*chip, 1 - c), me).wait_recv()
        for a in range(n):
            for cp in first(a) + passed(a):
                cp.wait_send()
            local(a).wait()

    return start, forward, finish


def _exchange_ops(ins, outs, send_sems, recv_sems):
    n = len(ins)

    def copy(a, k):
        dev, pidx = _peer(k)
        return pltpu.make_async_remote_copy(ins[a].at[pidx], outs[a].at[k], send_sems.at[a * 7 + k],
                                            recv_sems.at[a * 7 + k], device_id=dev, device_id_type=MESH)

    def start():
        for k in range(NDEV - 1):
            for a in range(n):
                copy(a, k).start()

    def finish():
        for k in range(NDEV - 1):
            for a in range(n):
                copy(a, k).wait_recv()
        for k in range(NDEV - 1):
            for a in range(n):
                copy(a, k).wait_send()

    return start, finish


def _gathered(shards):
    return tuple(jax.ShapeDtypeStruct((NDEV,) + s.shape, s.dtype) for s in shards)


def _exchanged(blocks):
    return tuple(jax.ShapeDtypeStruct((NDEV - 1,) + b.shape[1:], b.dtype) for b in blocks)


def exchange_start(blocks):
    landing = lax.empty((NDEV - 1,) + blocks.shape[1:], blocks.dtype)
    hbm = pl.BlockSpec(memory_space=pltpu.HBM)
    sem = pl.BlockSpec(memory_space=pltpu.SEMAPHORE)

    def body(src_ref, land_ref, send_sems, recv_sems, src_thru, land_thru, token):
        for k in range(NDEV - 1):
            dev, pidx = _peer(k)
            pltpu.make_async_remote_copy(src_ref.at[pidx], land_ref.at[k], send_sems.at[k], recv_sems.at[k],
                                         device_id=dev, device_id_type=MESH).start()
        token[...] = jnp.zeros_like(token)

    return pl.pallas_call(
        body, name="exchange_start",
        out_shape=(pltpu.SemaphoreType.DMA((NDEV - 1,)), pltpu.SemaphoreType.DMA((NDEV - 1,)),
                   pltpu.HBM(blocks.shape, blocks.dtype), pltpu.HBM(landing.shape, landing.dtype),
                   jax.ShapeDtypeStruct((8, 128), F32)),
        in_specs=(hbm, hbm), out_specs=(sem, sem, hbm, hbm, VSPEC), input_output_aliases={0: 2, 1: 3},
        compiler_params=pltpu.CompilerParams(has_side_effects=pltpu.SideEffectType.DATAFLOW_SIDE_EFFECTING),
    )(pltpu.with_memory_space_constraint(blocks, pltpu.HBM), pltpu.with_memory_space_constraint(landing, pltpu.HBM))


def exchange_wait(send_sems, recv_sems, src_thru, land_thru, after):
    hbm = pl.BlockSpec(memory_space=pltpu.HBM)
    sem = pl.BlockSpec(memory_space=pltpu.SEMAPHORE)

    def body(src_ref, land_ref, send_sems, recv_sems, after_ref, src_dead, got_ref):
        for k in range(NDEV - 1):
            dev, pidx = _peer(k)
            cp = pltpu.make_async_remote_copy(src_ref.at[pidx], land_ref.at[k], send_sems.at[k], recv_sems.at[k],
                                              device_id=dev, device_id_type=MESH)
            cp.wait_send()
            cp.wait_recv()

    return pl.pallas_call(
        body, name="exchange_wait",
        out_shape=(pltpu.HBM(src_thru.shape, src_thru.dtype), pltpu.HBM(land_thru.shape, land_thru.dtype)),
        in_specs=(hbm, hbm, sem, sem, ANYSPEC), out_specs=(hbm, hbm), input_output_aliases={0: 0, 1: 1},
        compiler_params=pltpu.CompilerParams(has_side_effects=pltpu.SideEffectType.DATAFLOW_SIDE_EFFECTING),
    )(src_thru, land_thru, send_sems, recv_sems, after)[1]


def gather_and_sum_rows(parts, keep_rows):
    n = len(parts)
    offs = [sum(p.shape[0] for p in parts[:i]) for i in range(n)]
    rows = sum(p.shape[0] for p in parts)

    def body(*refs):
        sum_ref, keep_ref, buf_ref, send_sems, recv_sems = refs[n:]
        x, y, c = _me()
        me = 4 * x + 2 * y + c
        for i in range(n):
            buf_ref[me, offs[i]:offs[i] + parts[i].shape[0], :] = refs[i][...]

        def copy(k, block):
            return pltpu.make_async_remote_copy(buf_ref.at[block], buf_ref.at[block], send_sems.at[k],
                                                recv_sems.at[k], device_id=_peer(k)[0], device_id_type=MESH)

        for k in range(NDEV - 1):
            copy(k, me).start()
        for k in range(NDEV - 1):
            copy(k, _peer(k)[1]).wait_recv()
        for k in range(NDEV - 1):
            copy(k, me).wait_send()
        total = buf_ref[0]
        for r in range(1, NDEV):
            total = total + buf_ref[r]
        sum_ref[...] = total
        keep_ref[...] = jnp.zeros_like(keep_ref)
        for r in range(NDEV):
            for j, src in enumerate(keep_rows):
                keep_ref[r, j:j + 1, :] = buf_ref[r, src:src + 1, :]

    return pl.pallas_call(
        body, name="gather_and_sum_rows",
        out_shape=(jax.ShapeDtypeStruct((rows, D), F32), jax.ShapeDtypeStruct((NDEV, 8, D), F32)),
        in_specs=[VSPEC] * n, out_specs=(VSPEC, VSPEC),
        scratch_shapes=[pltpu.VMEM((NDEV, rows, D), F32), pltpu.SemaphoreType.DMA((NDEV - 1,)),
                        pltpu.SemaphoreType.DMA((NDEV - 1,))],
        compiler_params=_params(),
    )(*parts)


def _arrival_order():
    x, y, c = _me()
    chips = [(1 - x, y), (x, 1 - y), (1 - x, 1 - y)]
    devs = [(x, y, c), (x, y, 1 - c)] + [(*q, c) for q in chips] + [(*q, 1 - c) for q in chips]
    return jnp.stack([4 * d[0] + 2 * d[1] + d[2] for d in devs]).astype(jnp.int32)


def in_proj(x, gain, sc, sh, w_shard, b_in):
    t = x.shape[0]
    tm = min(256, t)
    nm = t // tm
    order = _arrival_order()

    def body(order_ref, x_ref, g_ref, sc_ref, sh_ref, b_ref, wsh_ref, p_ref, h_ref, wall_ref,
             wbuf, h_all, send_sems, recv_sems, local_sems):
        k, m = pl.program_id(0), pl.program_id(1)
        mx, my, mc = _me()
        me, sibling = (mx, my, mc), (mx, my, 1 - mc)
        chips = [(1 - mx, my), (mx, 1 - my), (1 - mx, 1 - my)]

        def slot(d):
            return 4 * d[0] + 2 * d[1] + d[2]

        def copy(sem, block, to, src=None):
            return pltpu.make_async_remote_copy(
                src_ref=wall_ref.at[slot(block)] if src is None else src, dst_ref=wall_ref.at[slot(block)],
                send_sem=send_sems.at[sem], recv_sem=recv_sems.at[sem], device_id=to, device_id_type=MESH)

        own = pltpu.make_async_copy(wsh_ref, wall_ref.at[slot(me)], local_sems.at[0])
        first = [copy(0, me, sibling, src=wsh_ref)] + [copy(1 + j, me, (*q, mc), src=wsh_ref)
                                                       for j, q in enumerate(chips)]
        passed = [copy(4 + j, (*q, mc), sibling) for j, q in enumerate(chips)]

        def load(kk):
            src = wsh_ref if kk == 0 else wall_ref.at[order_ref[kk]]
            return pltpu.make_async_copy(src, wbuf.at[kk % 2], local_sems.at[1 + kk % 2])

        def arrived(kk):
            if kk == 1:
                copy(0, sibling, me).wait_recv()
            elif kk < 5:
                copy(kk - 1, (*chips[kk - 2], mc), me).wait_recv()
                passed[kk - 2].start()
            else:
                copy(kk - 1, (*chips[kk - 5], 1 - mc), me).wait_recv()

        @pl.when((k == 0) & (m == 0))
        def _():
            own.start()
            for cp in first:
                cp.start()
            load(0).start()
            load(0).wait()

        for kk in range(1, NDEV):
            @pl.when((k == kk - 1) & (m == nm // 2))
            def _(kk=kk):
                arrived(kk)
                load(kk).start()

            @pl.when((k == kk) & (m == 0))
            def _(kk=kk):
                load(kk).wait()

        rows = pl.ds(pl.multiple_of(m * tm, tm), tm)

        @pl.when(k == 0)
        def _():
            h_all[rows, :] = f_modulate(x_ref[...], g_ref[...], sc_ref[...], sh_ref[...]).astype(BF16)

        p_ref[...] = _dot(h_all[rows, :], wbuf[k % 2], _NN) + b_ref[...]

        @pl.when((k == NDEV - 1) & (m == nm - 1))
        def _():
            for cp in first + passed:
                cp.wait_send()
            own.wait()
            out = pltpu.make_async_copy(h_all, h_ref, local_sems.at[0])
            out.start()
            out.wait()

    vec = pl.BlockSpec((1, D), lambda k, m, o: (0, 0))
    return pl.pallas_call(
        body, name="in_proj",
        out_shape=(jax.ShapeDtypeStruct((t, NDEV * D), F32), jax.ShapeDtypeStruct((t, D), BF16),
                   jax.ShapeDtypeStruct((NDEV, D, D), BF16)),
        grid_spec=pltpu.PrefetchScalarGridSpec(
            num_scalar_prefetch=1, grid=(NDEV, nm),
            in_specs=[pl.BlockSpec((tm, D), lambda k, m, o: (jnp.where(k == 0, m, 0), 0)), vec, vec, vec,
                      pl.BlockSpec((1, D), lambda k, m, o: (0, o[k])), ANYSPEC],
            out_specs=(pl.BlockSpec((tm, D), lambda k, m, o: (m, o[k])), ANYSPEC, ANYSPEC),
            scratch_shapes=[pltpu.VMEM((2, D, D), BF16), pltpu.VMEM((t, D), BF16), pltpu.SemaphoreType.DMA((7,)),
                            pltpu.SemaphoreType.DMA((7,)), pltpu.SemaphoreType.DMA((3,))]),
        compiler_params=_params("arbitrary", "arbitrary"),
    )(order, x, gain, sc, sh, b_in, w_shard)


def hgrn_local(p, logits, shards):
    t = p.shape[0]
    nc = t // CH
    n = len(shards)

    def body(*refs):
        q_ref, f_ref, v_ref, l_ref = refs[:4]
        intra_ref, qe_ref, st_ref, dec_ref = refs[4 + n:8 + n]
        state = refs[8 + 2 * n]
        start, forward, finish = _gather2_ops(refs[4:4 + n], refs[8 + n:8 + 2 * n], *refs[9 + 2 * n:])
        step = pl.program_id(0)

        @pl.when(step == 0)
        def _():
            start()
            state[...] = jnp.zeros_like(state)

        intra, qe, ut, dec = f_hgrn_chunk(q_ref[...], f_ref[...], v_ref[...], l_ref[...])
        intra_ref[...] = intra
        qe_ref[...] = qe.astype(BF16)
        dec_ref[...] = dec
        s = state[...]
        st_ref[...] = s
        state[...] = dec * s + ut
        pl.when(step == (3 * nc) // 4)(forward)
        pl.when(step == nc - 1)(finish)

    col = lambda j: pl.BlockSpec((CH, D), lambda c: (c, j))
    res = pl.pallas_call(
        body, name="hgrn_local", grid=(nc,),
        out_shape=(jax.ShapeDtypeStruct((t, D), F32), jax.ShapeDtypeStruct((t, D), BF16),
                   jax.ShapeDtypeStruct((nc, DK, D), F32), jax.ShapeDtypeStruct((nc, 1, D), F32)) + _gathered(shards),
        in_specs=[col(0), col(1), col(2), pl.BlockSpec((2, D), lambda c: (0, 0))] + [ANYSPEC] * n,
        out_specs=(pl.BlockSpec((CH, D), lambda c: (c, 0)), pl.BlockSpec((CH, D), lambda c: (c, 0)),
                   pl.BlockSpec((None, DK, D), lambda c: (c, 0, 0)), pl.BlockSpec((None, 1, D), lambda c: (c, 0, 0)))
        + (ANYSPEC,) * n,
        scratch_shapes=[pltpu.VMEM((DK, D), F32)] + _comm_sems(n, True),
        compiler_params=_params("arbitrary"),
    )(p, p, p, logits, *shards)
    return res[:4] + (res[4:],)


def _conv_tile(t):
    return min(256, t)


def conv_forward(p, dw, db, ln_g, ln_b, shards):
    t = p.shape[0]
    tm = _conv_tile(t)
    per = tm // HALO
    n = len(shards)
    nt = t // tm

    def body(*refs):
        cv_ref, cg_ref, cvp_ref, cgp_ref, dw_ref, db_ref, g_ref, b_ref = refs[:8]
        yc_ref, act_ref = refs[8 + n:10 + n]
        uext, ush = refs[10 + 2 * n:12 + 2 * n]
        start, forward, finish = _gather2_ops(refs[8:8 + n], refs[10 + n:10 + 2 * n], *refs[12 + 2 * n:])
        step = pl.program_id(0)
        pl.when(step == 0)(start)
        pl.when(step == nt - 1)(forward)
        uext[0:HALO, :] = jnp.where(step == 0, 0.0, f_glu(cvp_ref[...], cgp_ref[...]))
        uext[HALO:HALO + tm, :] = f_glu(cv_ref[...], cg_ref[...])
        _shift_stack(uext, ush, tm + HALO)
        acc = jnp.zeros((tm, D), F32) + db_ref[...]
        for w in range(KW):
            acc = acc + dw_ref[w:w + 1, :] * _shifted(ush, HALO - KW + 1 + w, tm)
        yc_ref[...] = acc
        act_ref[...] = f_conv_act(acc, g_ref[...], b_ref[...]).astype(BF16)
        pl.when(step == nt - 1)(finish)

    vec = pl.BlockSpec((1, D), lambda m: (0, 0))
    prev = lambda j: pl.BlockSpec((HALO, D), lambda m: (jnp.maximum(m * per - 1, 0), j))
    res = pl.pallas_call(
        body, name="conv_forward", grid=(nt,),
        out_shape=(jax.ShapeDtypeStruct((t, D), F32), jax.ShapeDtypeStruct((t, D), BF16)) + _gathered(shards),
        in_specs=[pl.BlockSpec((tm, D), lambda m: (m, 4)), pl.BlockSpec((tm, D), lambda m: (m, 5)), prev(4), prev(5),
                  pl.BlockSpec((HALO, D), lambda m: (0, 0)), vec, vec, vec] + [ANYSPEC] * n,
        out_specs=(pl.BlockSpec((tm, D), lambda m: (m, 0)), pl.BlockSpec((tm, D), lambda m: (m, 0))) + (ANYSPEC,) * n,
        scratch_shapes=[pltpu.VMEM((HALO + tm, D), F32), pltpu.VMEM((8, tm + HALO, D), F32)] + _comm_sems(n, True),
        compiler_params=_params("arbitrary"),
    )(p, p, p, p, dw, db, ln_g, ln_b, *shards)
    return res[0], res[1], res[2:]


def _mix_tile(t):
    return min(256, t)


def _mix_forward_tile(x_ref, og_ref, ga_ref, gb_ref, qe_ref, intra_ref, st_ref, act_ref, wa_ref, wb_ref, wo_ref,
                      hg_ref, rows):
    o = _inter(qe_ref[...], st_ref, rows) + intra_ref[...]
    a = f_head_out(o, og_ref[...], hg_ref[...])
    ya = _dot(a.astype(BF16), wa_ref[...], _NN)
    yb = _dot(act_ref[...], wb_ref[...], _NN)
    merged = f_merge(ga_ref[...], gb_ref[...], ya, yb)
    y = _dot(merged.astype(BF16), wo_ref[...], _NN)
    return o, a, ya, yb, merged, y


def _mix_specs(tm, tiles=None):
    at = (lambda m: m) if tiles is None else (lambda m: tiles - 1 - m)
    col = lambda j: pl.BlockSpec((tm, D), lambda m: (at(m), j))
    row = pl.BlockSpec((tm, D), lambda m: (at(m), 0))
    per_chunk = lambda rows: pl.BlockSpec((tm // CH, rows, D), lambda m: (at(m), 0, 0))
    return col, row, pl.BlockSpec((1, D), lambda m: (0, 0)), per_chunk


def mix_forward(x, p, qe, intra, st_prev, act, wa, wb, wo, hg, g1, post):
    t = x.shape[0]
    tm = _mix_tile(t)

    def body(x_ref, og_ref, ga_ref, gb_ref, qe_ref, intra_ref, st_ref, act_ref, wa_ref, wb_ref, wo_ref, hg_ref,
             g1_ref, post_ref, x1_ref):
        y = _mix_forward_tile(x_ref, og_ref, ga_ref, gb_ref, qe_ref, intra_ref, st_ref, act_ref, wa_ref, wb_ref,
                              wo_ref, hg_ref, tm)[-1]
        x1_ref[...] = f_residual(x_ref[...], y, g1_ref[...], post_ref[...])

    col, row, vec, per_chunk = _mix_specs(tm)
    return pl.pallas_call(
        body, name="mix_forward", grid=(t // tm,),
        out_shape=jax.ShapeDtypeStruct((t, D), F32),
        in_specs=[row, col(3), col(6), col(7), row, row, per_chunk(DK), row, VSPEC, VSPEC, VSPEC, vec, vec, vec],
        out_specs=row,
        compiler_params=_params("parallel"),
    )(x, p, p, p, qe, intra, st_prev, act, wa, wb, wo, hg, g1, post)


def ffn_forward_backward(x1, target, w1, w2, pre, sc, sh, g2, post):
    t = x1.shape[0]
    tm = min(256, t)
    nb = w1.shape[0]
    fb = w1.shape[2]

    def body(x_ref, tg_ref, w1_ref, w2_ref, pre_ref, sc_ref, sh_ref, g2_ref, post_ref,
             dx_ref, h2_ref, r_ref, dz_ref, dy2_ref, acc_ref, z_sc):
        @pl.when(pl.program_id(0) == 0)
        def _():
            acc_ref[...] = jnp.zeros_like(acc_ref)

        x1v = x_ref[...]
        h2, vjp_h = jax.vjp(f_modulate, x1v, pre_ref[...], sc_ref[...], sh_ref[...])
        h2b = h2.astype(BF16)
        h2_ref[...] = h2b
        y2 = jnp.zeros((tm, D), F32)
        for n in range(nb):
            z = _dot(h2b, w1_ref[n], _NN)
            z_sc[:, n * fb:(n + 1) * fb] = z
            r = jnp.square(jnp.maximum(z, 0.0)).astype(BF16)
            r_ref[:, n * fb:(n + 1) * fb] = r
            y2 = y2 + _dot(r, w2_ref[n], _NN)
        out, vjp_r = jax.vjp(f_residual, x1v, y2, g2_ref[...], post_ref[...])
        err = out - tg_ref[...]
        tok = jnp.mean(jnp.square(err), axis=-1, keepdims=True)
        acc_ref[5:6, :] += 0.5 * jnp.sum(tok, axis=0, keepdims=True)
        dx_a, dy2, dg2, dpost = vjp_r(err * (1.0 / D))
        dy2b = dy2.astype(BF16)
        dy2_ref[...] = dy2b
        dh2 = jnp.zeros((tm, D), F32)
        for n in range(nb):
            dr = _dot(dy2b, w2_ref[n], _NT)
            dz = (dr * (2.0 * jnp.maximum(z_sc[:, n * fb:(n + 1) * fb], 0.0))).astype(BF16)
            dz_ref[:, n * fb:(n + 1) * fb] = dz
            dh2 = dh2 + _dot(dz, w1_ref[n], _NT)
        dx_b, dpre, dsc, dsh = vjp_h(dh2)
        dx_ref[...] = dx_a + dx_b
        acc_ref[0:1, :] += dpre
        acc_ref[1:2, :] += dpost
        acc_ref[2:3, :] += dsc
        acc_ref[3:4, :] += dsh
        acc_ref[4:5, :] += dg2

    row = pl.BlockSpec((tm, D), lambda m: (m, 0))
    wide = pl.BlockSpec((tm, DFF), lambda m: (m, 0))
    vec = pl.BlockSpec((1, D), lambda m: (0, 0))
    return pl.pallas_call(
        body, name="ffn_forward_backward", grid=(t // tm,),
        out_shape=(jax.ShapeDtypeStruct((t, D), F32), jax.ShapeDtypeStruct((t, D), BF16),
                   jax.ShapeDtypeStruct((t, DFF), BF16), jax.ShapeDtypeStruct((t, DFF), BF16),
                   jax.ShapeDtypeStruct((t, D), BF16), jax.ShapeDtypeStruct((8, D), F32)),
        in_specs=[row, row, VSPEC, VSPEC, vec, vec, vec, vec, vec],
        out_specs=(row, row, wide, wide, row, pl.BlockSpec((8, D), lambda m: (0, 0))),
        scratch_shapes=[pltpu.VMEM((tm, DFF), F32)],
        compiler_params=_params("arbitrary"),
    )(x1, target, w1, w2, pre, sc, sh, g2, post)


def mix_backward(x, p, qe, intra, st_prev, act, wa, wb, wo, hg, g1, post, dx1, dec):
    t = x.shape[0]
    tm = _mix_tile(t)
    nc = t // CH

    def body(x_ref, og_ref, ga_ref, gb_ref, qe_ref, intra_ref, st_ref, act_ref, wa_ref, wb_ref, wo_ref, hg_ref,
             g1_ref, post_ref, dx1_ref, dec_ref,
             dp_ref, do_ref, dqe_ref, gt_ref, dact_ref, a_ref, mg_ref, dya_ref, dyb_ref, dy_ref, acc_ref, gstate):
        @pl.when(pl.program_id(0) == 0)
        def _():
            acc_ref[...] = jnp.zeros_like(acc_ref)
            gstate[...] = jnp.zeros_like(gstate)

        o, a, ya, yb, merged, y = _mix_forward_tile(x_ref, og_ref, ga_ref, gb_ref, qe_ref, intra_ref, st_ref, act_ref,
                                                    wa_ref, wb_ref, wo_ref, hg_ref, tm)
        a_ref[...] = a.astype(BF16)
        mg_ref[...] = merged.astype(BF16)
        _, vjp_r = jax.vjp(f_residual, x_ref[...], y, g1_ref[...], post_ref[...])
        _, dy, dg1, dpost = vjp_r(dx1_ref[...])
        dyb16 = dy.astype(BF16)
        dy_ref[...] = dyb16
        dmerged = _dot(dyb16, wo_ref[...], _NT)
        _, vjp_m = jax.vjp(f_merge, ga_ref[...], gb_ref[...], ya, yb)
        dga, dgb, dya, dyb = vjp_m(dmerged)
        dp_ref[:, D:2 * D] = dga.astype(BF16)
        dp_ref[:, 2 * D:3 * D] = dgb.astype(BF16)
        dya16, dyb16b = dya.astype(BF16), dyb.astype(BF16)
        dya_ref[...] = dya16
        dyb_ref[...] = dyb16b
        da = _dot(dya16, wa_ref[...], _NT)
        dact_ref[...] = _dot(dyb16b, wb_ref[...], _NT)
        _, vjp_a = jax.vjp(f_head_out, o, og_ref[...], hg_ref[...])
        do, dog, dhg = vjp_a(da)
        dp_ref[:, 0:D] = dog.astype(BF16)
        do_ref[...] = do
        do16 = do.astype(BF16)
        qe16 = qe_ref[...]
        for ci in reversed(range(tm // CH)):
            st = st_ref[ci].astype(BF16)
            rows = slice(ci * CH, (ci + 1) * CH)
            dqe, vt = [], []
            for h in range(HEADS):
                sl = slice(h * DK, (h + 1) * DK)
                dqe.append(_dot(do16[rows, sl], st[:, sl], _NN))
                vt.append(_dot(do16[rows, sl], qe16[rows, sl], _TN))
            dqe_ref[rows, :] = jnp.concatenate(dqe, axis=1)
            g = gstate[...]
            gt_ref[ci] = g
            gstate[...] = dec_ref[ci] * g + jnp.concatenate(vt, axis=1)
        acc_ref[0:1, :] += dg1
        acc_ref[1:2, :] += dpost
        acc_ref[2:3, :] += dhg
        acc_ref[3:4, :] += jnp.sum(dog, axis=0, keepdims=True)
        acc_ref[4:5, :] += jnp.sum(dga, axis=0, keepdims=True)
        acc_ref[5:6, :] += jnp.sum(dgb, axis=0, keepdims=True)

    nt = t // tm
    col, row, vec, per_chunk = _mix_specs(tm, nt)
    b16 = jax.ShapeDtypeStruct((t, D), BF16)
    f32 = jax.ShapeDtypeStruct((t, D), F32)
    return pl.pallas_call(
        body, name="mix_backward", grid=(nt,),
        out_shape=(jax.ShapeDtypeStruct((t, NDEV * D), BF16), f32, f32, jax.ShapeDtypeStruct((nc, DK, D), F32), f32,
                   b16, b16, b16, b16, b16, jax.ShapeDtypeStruct((8, D), F32)),
        in_specs=[row, col(3), col(6), col(7), row, row, per_chunk(DK), row, VSPEC, VSPEC, VSPEC, vec, vec, vec, row,
                  per_chunk(1)],
        out_specs=(pl.BlockSpec((tm, 3 * D), lambda m: (nt - 1 - m, 0)), row, row, per_chunk(DK), row, row, row, row,
                   row, row, pl.BlockSpec((8, D), lambda m: (0, 0))),
        scratch_shapes=[pltpu.VMEM((DK, D), F32)],
        compiler_params=_params("arbitrary"),
    )(x, p, p, p, qe, intra, st_prev, act, wa, wb, wo, hg, g1, post, dx1, dec)


def conv_backward(p, yc, dact, dw, ln_g, ln_b, blocks, dp):
    t = p.shape[0]
    tm = _conv_tile(t)
    per = tm // HALO
    nt = t // tm
    n = len(blocks)

    def body(*refs):
        cv_ref, cg_ref, cvp_ref, cgp_ref, yc_ref, ycn_ref, da_ref, dan_ref, dw_ref, g_ref, b_ref = refs[:11]
        dp_ref, acc_ref, ddw_ref = refs[12 + n:15 + n]
        uext, dyext, ush, dysh, ddw8, du_sc = refs[15 + 2 * n:21 + 2 * n]
        start, finish = _exchange_ops(refs[11:11 + n], refs[15 + n:15 + 2 * n], *refs[21 + 2 * n:])
        m = pl.program_id(0)

        @pl.when(m == 0)
        def _():
            start()
            acc_ref[...] = jnp.zeros_like(acc_ref)
            ddw8[...] = jnp.zeros_like(ddw8)

        cv, cg = cv_ref[...], cg_ref[...]
        u, vjp_u = jax.vjp(f_glu, cv, cg)
        uext[0:HALO, :] = jnp.where(m == 0, 0.0, f_glu(cvp_ref[...], cgp_ref[...]))
        uext[HALO:HALO + tm, :] = u
        _shift_stack(uext, ush, tm + HALO)
        _, vjp_c = jax.vjp(f_conv_act, yc_ref[...], g_ref[...], b_ref[...])
        dyc, dg, db = vjp_c(da_ref[...])
        _, vjp_n = jax.vjp(f_conv_act, ycn_ref[...], g_ref[...], b_ref[...])
        dyn = vjp_n(dan_ref[...])[0]
        dyext[0:tm, :] = dyc
        dyext[tm:tm + HALO, :] = jnp.where(m == nt - 1, 0.0, dyn)
        _shift_stack(dyext, dysh, tm + HALO)
        rb = min(128, tm)
        for lt in range(D // LANE):
            ls = slice(lt * LANE, (lt + 1) * LANE)
            for r0 in range(0, tm, rb):
                du_l = jnp.zeros((rb, LANE), F32)
                for w in range(KW):
                    du_l = du_l + dw_ref[w:w + 1, ls] * _shifted(dysh, KW - 1 - w + r0, rb, ls)
                du_sc[r0:r0 + rb, ls] = du_l
                dyc_l = dyext[r0:r0 + rb, ls]
                for w in range(KW):
                    prod = dyc_l * _shifted(ush, HALO - KW + 1 + w + r0, rb, ls)
                    part = jnp.sum(prod.reshape(4, rb // 32, 8, LANE), axis=1)
                    ddw8[w, :, ls] += (part[0] + part[1]) + (part[2] + part[3])
        du = du_sc[...]

        @pl.when(m == nt - 1)
        def _():
            ddw_ref[...] = jnp.sum(ddw8[...], axis=1)

        dcv, dcg = vjp_u(du)
        dp_ref[:, 0:D] = dcv.astype(BF16)
        dp_ref[:, D:2 * D] = dcg.astype(BF16)
        acc_ref[0:1, :] += jnp.sum(dyc, axis=0, keepdims=True)
        acc_ref[1:2, :] += dg
        acc_ref[2:3, :] += db
        acc_ref[3:4, :] += jnp.sum(dcv, axis=0, keepdims=True)
        acc_ref[4:5, :] += jnp.sum(dcg, axis=0, keepdims=True)
        pl.when(m == nt - 1)(finish)

    vec = pl.BlockSpec((1, D), lambda m: (0, 0))
    row = pl.BlockSpec((tm, D), lambda m: (m, 0))
    prev = lambda j: pl.BlockSpec((HALO, D), lambda m: (jnp.maximum(m * per - 1, 0), j))
    nxt = pl.BlockSpec((HALO, D), lambda m: (jnp.minimum((m + 1) * per, t // HALO - 1), 0))
    res = pl.pallas_call(
        body, name="conv_backward", grid=(nt,),
        out_shape=(jax.ShapeDtypeStruct(dp.shape, dp.dtype), jax.ShapeDtypeStruct((8, D), F32),
                   jax.ShapeDtypeStruct((HALO, D), F32)) + _exchanged(blocks),
        in_specs=[pl.BlockSpec((tm, D), lambda m: (m, 4)), pl.BlockSpec((tm, D), lambda m: (m, 5)), prev(4), prev(5),
                  row, nxt, row, nxt, pl.BlockSpec((HALO, D), lambda m: (0, 0)), vec, vec] + [ANYSPEC] * (n + 1),
        out_specs=(pl.BlockSpec((tm, 2 * D), lambda m: (m, 3)), pl.BlockSpec((8, D), lambda m: (0, 0)),
                   pl.BlockSpec((HALO, D), lambda m: (0, 0))) + (ANYSPEC,) * n,
        scratch_shapes=[pltpu.VMEM((HALO + tm, D), F32), pltpu.VMEM((tm + HALO, D), F32),
                        pltpu.VMEM((8, tm + HALO, D), F32), pltpu.VMEM((8, tm + HALO, D), F32),
                        pltpu.VMEM((HALO, 8, D), F32), pltpu.VMEM((tm, D), F32)] + _comm_sems(n, False),
        input_output_aliases={11 + n: 0},
        compiler_params=_params("arbitrary"),
    )(p, p, p, p, yc, yc, dact, dact, dw, ln_g, ln_b, *blocks, dp)
    return res[:3] + (res[3:],)


def hgrn_backward(p, logits, do, dqe, gt, st_prev, blocks, dp):
    t = p.shape[0]
    nc = t // CH
    n = len(blocks)

    def body(*refs):
        q_ref, f_ref, v_ref, l_ref, do_ref, dqe_ref, gt_ref, st_ref = refs[:8]
        dp_ref, acc_ref = refs[9 + n:11 + n]
        start, finish = _exchange_ops(refs[8:8 + n], refs[11 + n:11 + 2 * n], *refs[11 + 2 * n:])

        @pl.when(pl.program_id(0) == 0)
        def _():
            start()
            acc_ref[...] = jnp.zeros_like(acc_ref)

        gt_v = gt_ref[...]
        ddec = jnp.sum(gt_v * st_ref[...], axis=0, keepdims=True)
        _, vjp = jax.vjp(f_hgrn_chunk, q_ref[...], f_ref[...], v_ref[...], l_ref[...])
        dq, df, dv, dl = vjp((do_ref[...], dqe_ref[...], gt_v, ddec))
        dp_ref[:, 0:D] = dq.astype(BF16)
        dp_ref[:, D:2 * D] = df.astype(BF16)
        dp_ref[:, 2 * D:3 * D] = dv.astype(BF16)
        acc_ref[0:2, :] += dl
        acc_ref[2:3, :] += jnp.sum(dq, axis=0, keepdims=True)
        acc_ref[3:4, :] += jnp.sum(df, axis=0, keepdims=True)
        acc_ref[4:5, :] += jnp.sum(dv, axis=0, keepdims=True)
        pl.when(pl.program_id(0) == nc - 1)(finish)

    col = lambda j: pl.BlockSpec((CH, D), lambda c: (c, j))
    row = pl.BlockSpec((CH, D), lambda c: (c, 0))
    stspec = pl.BlockSpec((None, DK, D), lambda c: (c, 0, 0))
    res = pl.pallas_call(
        body, name="hgrn_backward", grid=(nc,),
        out_shape=(jax.ShapeDtypeStruct(dp.shape, dp.dtype), jax.ShapeDtypeStruct((8, D), F32)) + _exchanged(blocks),
        in_specs=[col(0), col(1), col(2), pl.BlockSpec((2, D), lambda c: (0, 0)), row, row, stspec, stspec]
        + [ANYSPEC] * (n + 1),
        out_specs=(pl.BlockSpec((CH, 3 * D), lambda c: (c, 1)), pl.BlockSpec((8, D), lambda c: (0, 0)))
        + (ANYSPEC,) * n,
        scratch_shapes=_comm_sems(n, False),
        input_output_aliases={8 + n: 0},
        compiler_params=_params("arbitrary"),
    )(p, p, p, logits, do, dqe, gt, st_prev, *blocks, dp)
    return res[:2] + (res[2:],)


def in_proj_backward(dp, w_all, x, dx1, gain, sc, sh, blocks):
    t = x.shape[0]
    tm = min(256, t)
    nt = t // tm
    n = len(blocks)

    def body(*refs):
        dp_ref, w_ref, x_ref, dx1_ref, g_ref, sc_ref, sh_ref = refs[:7]
        gx_ref, acc_ref = refs[7 + n:9 + n]
        start = finish = lambda: None
        if n:
            start, finish = _exchange_ops(refs[7:7 + n], refs[9 + n:9 + 2 * n], *refs[9 + 2 * n:])

        @pl.when(pl.program_id(0) == 0)
        def _():
            start()
            acc_ref[...] = jnp.zeros_like(acc_ref)

        dh = jnp.zeros((tm, D), F32)
        for j in range(NDEV):
            dh = dh + _dot(dp_ref[:, j * D:(j + 1) * D], w_ref[DP_SPLIT[j]], _NT)
        _, vjp_h = jax.vjp(f_modulate, x_ref[...], g_ref[...], sc_ref[...], sh_ref[...])
        dx, dg, dsc, dsh = vjp_h(dh)
        gx_ref[...] = dx1_ref[...] + dx
        acc_ref[0:1, :] += dg
        acc_ref[1:2, :] += dsc
        acc_ref[2:3, :] += dsh
        pl.when(pl.program_id(0) == nt - 1)(finish)

    row = pl.BlockSpec((tm, D), lambda m: (m, 0))
    vec = pl.BlockSpec((1, D), lambda m: (0, 0))
    res = pl.pallas_call(
        body, name="in_proj_backward", grid=(nt,),
        out_shape=(jax.ShapeDtypeStruct((t, D), F32), jax.ShapeDtypeStruct((8, D), F32)) + _exchanged(blocks),
        in_specs=[pl.BlockSpec((tm, NDEV * D), lambda m: (m, 0)), VSPEC, row, row, vec, vec, vec] + [ANYSPEC] * n,
        out_specs=(row, pl.BlockSpec((8, D), lambda m: (0, 0))) + (ANYSPEC,) * n,
        scratch_shapes=_comm_sems(n, False) if n else [],
        compiler_params=_params("arbitrary"),
    )(dp, w_all, x, dx1, gain, sc, sh, *blocks)
    return res[0], res[1], res[2:]


def _dp_column(n):
    return jnp.where(n < 3, n + 3, jnp.where(n == 3, 0, jnp.where(n < 6, n + 2, n - 5)))


def weight_grad(a, b, nblk, ka, bn, a_blocked, name, b_col=lambda n: n):
    t = a.shape[0]
    tk = min(2048, t)
    nk = t // tk

    def body(a_ref, b_ref, f_ref, h_ref, acc):
        k = pl.program_id(1)

        @pl.when(k == 0)
        def _():
            acc[...] = jnp.zeros_like(acc)

        acc[...] += _dot(a_ref[...], b_ref[...], _TN)

        @pl.when(k == nk - 1)
        def _():
            f_ref[...] = acc[...]
            h_ref[...] = acc[...].astype(BF16)

    a_idx = (lambda n, k: (k, n)) if a_blocked else (lambda n, k: (k, 0))
    b_idx = (lambda n, k: (k, 0)) if a_blocked else (lambda n, k: (k, b_col(n)))
    out = pl.BlockSpec((None, ka, bn), lambda n, k: (n, 0, 0))
    return pl.pallas_call(
        body, name=name, grid=(nblk, nk),
        out_shape=(jax.ShapeDtypeStruct((nblk, ka, bn), F32), jax.ShapeDtypeStruct((nblk, ka, bn), BF16)),
        in_specs=[pl.BlockSpec((tk, ka), a_idx), pl.BlockSpec((tk, bn), b_idx)],
        out_specs=(out, out),
        scratch_shapes=[pltpu.VMEM((ka, bn), F32)],
        compiler_params=_params("parallel", "arbitrary"),
    )(a, b)


def ada_backward(call_t, dmod_cols, w, m, v):
    def body(c_ref, d_ref, w_ref, m_ref, v_ref, g_ref, dl_ref, nm_ref, nv_ref):
        ct, dm = c_ref[...], d_ref[...]
        g = ct[:, 0:1] * dm[0:1, :]
        for r in range(1, NDEV):
            g = g + ct[:, r:r + 1] * dm[r:r + 1, :]
        g_ref[...] = g
        dl_ref[...], nm_ref[...], nv_ref[...] = _adamw(w_ref[...], g, m_ref[...], v_ref[...])

    s = jax.ShapeDtypeStruct(w.shape, F32)
    return pl.pallas_call(
        body, name="ada_backward", out_shape=(s, s, s, s),
        in_specs=[VSPEC] * 5, out_specs=(VSPEC,) * 4, compiler_params=_params(),
    )(call_t, dmod_cols, w, m, v)


def adamw_small(total, ddw_mine, recipes, ws, ms, vs):
    n = len(ws)

    def body(*refs):
        tot, ddw = refs[0], refs[1]
        w_refs, m_refs, v_refs = refs[2:2 + n], refs[2 + n:2 + 2 * n], refs[2 + 2 * n:2 + 3 * n]
        outs = refs[2 + 3 * n:2 + 7 * n]
        loss_ref = refs[2 + 7 * n]
        for i, rec in enumerate(recipes):
            if rec == "dw":
                g = ddw[...]
            elif isinstance(rec, tuple):
                g = tot[rec[0]:rec[1], :]
            else:
                g = jnp.concatenate([tot[r:r + 1, :] for r in rec], axis=1) if len(rec) > 1 else tot[rec[0]:rec[0] + 1, :]
            dl, nm, nv = _adamw(w_refs[i][...], g, m_refs[i][...], v_refs[i][...])
            outs[4 * i][...] = g
            outs[4 * i + 1][...] = dl
            outs[4 * i + 2][...] = nm
            outs[4 * i + 3][...] = nv
        loss_ref[...] = tot[LOSS_ROW:LOSS_ROW + 1, 0:128]

    shapes = []
    for w in ws:
        shapes += [jax.ShapeDtypeStruct(w.shape, F32)] * 4
    res = pl.pallas_call(
        body, name="adamw_small", out_shape=tuple(shapes) + (jax.ShapeDtypeStruct((1, 128), F32),),
        in_specs=[VSPEC] * (2 + 3 * n), out_specs=(VSPEC,) * (4 * n + 1), compiler_params=_params(),
    )(total, ddw_mine, *ws, *ms, *vs)
    return [res[4 * i:4 * i + 4] for i in range(n)], res[4 * n]


def reduce_and_adamw(me, w, g_all, g_recv, m, v, name):
    r, c = w.shape
    br = min(128, r)

    def body(me_ref, w_ref, go_ref, gr_ref, m_ref, v_ref, g_ref, dl_ref, nm_ref, nv_ref):
        g = go_ref[...]
        for k in range(NDEV - 1):
            g = g + gr_ref[k].astype(F32)
        g_ref[...] = g
        dl_ref[...], nm_ref[...], nv_ref[...] = _adamw(w_ref[...], g, m_ref[...], v_ref[...])

    blk = pl.BlockSpec((br, c), lambda i, me_ref: (i, 0))
    s = jax.ShapeDtypeStruct(w.shape, F32)
    return pl.pallas_call(
        body, name=name, out_shape=(s, s, s, s),
        grid_spec=pltpu.PrefetchScalarGridSpec(
            num_scalar_prefetch=1, grid=(r // br,),
            in_specs=[blk, pl.BlockSpec((None, br, c), lambda i, me_ref: (me_ref[0], i, 0)),
                      pl.BlockSpec((NDEV - 1, br, c), lambda i, me_ref: (0, i, 0)), blk, blk],
            out_specs=(blk, blk, blk, blk)),
        compiler_params=_params("parallel"),
    )(me, w, g_all, g_recv, m, v)


def kernel(x, c, w_ada, b_ada, pre_norm_tm, post_norm_tm, pre_norm_cm, post_norm_cm, w_in, b_in, hg_lb_logits, hg_norm, conv_dw, conv_db, conv_ln_g, conv_ln_b, w_br_a, w_br_b, w_out, w_ff1, w_ff2, loss_target, m_w_ada, m_b_ada, m_pre_norm_tm, m_post_norm_tm, m_pre_norm_cm, m_post_norm_cm, m_w_in, m_b_in, m_hg_lb_logits, m_hg_norm, m_conv_dw, m_conv_db, m_conv_ln_g, m_conv_ln_b, m_w_br_a, m_w_br_b, m_w_out, m_w_ff1, m_w_ff2, v_w_ada, v_b_ada, v_pre_norm_tm, v_post_norm_tm, v_pre_norm_cm, v_post_norm_cm, v_w_in, v_b_in, v_hg_lb_logits, v_hg_norm, v_conv_dw, v_conv_db, v_conv_ln_g, v_conv_ln_b, v_w_br_a, v_w_br_b, v_w_out, v_w_ff1, v_w_ff2):
    t = x.shape[1]
    me = 4 * lax.axis_index("x") + 2 * lax.axis_index("y") + lax.axis_index("c")
    xs = x[0]
    tgt = loss_target[0]

    mod, call, dw_all = ada_forward(c, w_ada[0], b_ada, conv_dw[0])
    sh1, sc1, g1, sh2, sc2, g2 = [mod[:, i * D:(i + 1) * D] for i in range(6)]
    dw = jnp.pad(dw_all.transpose(1, 0, 2).reshape(KW, D), ((0, HALO - KW), (0, 0)))

    p, h, win_all = in_proj(xs, pre_norm_tm, sc1, sh1, w_in[0].astype(BF16), b_in)
    yc, act, (w1_all,) = conv_forward(p, dw, conv_db, conv_ln_g, conv_ln_b, [w_ff1[0].astype(BF16)])
    intra, qe, st_prev, dec, (w2_all, wa_all, wb_all, wo_all) = hgrn_local(
        p, hg_lb_logits, [w_ff2[0].astype(BF16), w_br_a[0].astype(BF16), w_br_b[0].astype(BF16),
                          w_out[0].astype(BF16)])
    wa, wb, wo = wa_all.reshape(D, D), wb_all.reshape(D, D), wo_all.reshape(D, D)
    x1 = mix_forward(xs, p, qe, intra, st_prev, act, wa, wb, wo, hg_norm, g1, post_norm_tm)

    dx1, h2, r, dz, dy2, acc_ffn = ffn_forward_backward(x1, tgt, w1_all, w2_all, pre_norm_cm, sc2, sh2, g2,
                                                        post_norm_cm)
    rows = D // NDEV
    g1_f, g1_h = weight_grad(h2, dz, NDEV, D, DFF // NDEV, False, "grad_w_ff1")
    g2_f, g2_h = weight_grad(r, dy2, 4, D, D, True, "grad_w_ff2")
    g2_f, g2_h = g2_f.reshape(NDEV, DFF // NDEV, D), g2_h.reshape(NDEV, DFF // NDEV, D)

    (dp, do, dqe, gt, dact, a16, mg16, dya, dyb, dy, acc_mix) = mix_backward(
        xs, p, qe, intra, st_prev, act, wa, wb, wo, hg_norm, g1, post_norm_tm, dx1, dec)
    dp, acc_hg, (r_ff1, r_ff2) = hgrn_backward(p, hg_lb_logits, do, dqe, gt, st_prev, [g1_h, g2_h], dp)
    ga_f, ga_h = weight_grad(a16, dya, 1, D, D, False, "grad_w_br_a")
    gb_f, gb_h = weight_grad(act, dyb, 1, D, D, False, "grad_w_br_b")
    go_f, go_h = weight_grad(mg16, dy, 1, D, D, False, "grad_w_out")
    dp, acc_conv, ddw, (r_a, r_b, r_o) = conv_backward(
        p, yc, dact, dw, conv_ln_g, conv_ln_b,
        [ga_h.reshape(NDEV, rows, D), gb_h.reshape(NDEV, rows, D), go_h.reshape(NDEV, rows, D)], dp)
    gin_f, gin_h = weight_grad(h, dp, NDEV, D, D, False, "grad_w_in", _dp_column)
    in_send, in_recv, gin_thru, in_landing, token = exchange_start(gin_h)
    grad_x, acc_in, _ = in_proj_backward(dp, win_all, xs, dx1, pre_norm_tm, sc1, sh1 + token[0:1, 0:1], [])

    own = [ga_f.reshape(NDEV, rows, D), gb_f.reshape(NDEV, rows, D), go_f.reshape(NDEV, rows, D), g1_f, g2_f]
    recv = [r_a, r_b, r_o, r_ff1, r_ff2]
    big = {}
    names = ["w_br_a", "w_br_b", "w_out", "w_ff1", "w_ff2"]
    ws = [w_br_a, w_br_b, w_out, w_ff1, w_ff2]
    ms = [m_w_br_a, m_w_br_b, m_w_out, m_w_ff1, m_w_ff2]
    vs = [v_w_br_a, v_w_br_b, v_w_out, v_w_ff1, v_w_ff2]
    me1 = me.astype(jnp.int32).reshape(1)
    for i, nm in enumerate(names):
        big[nm] = [o[None] for o in reduce_and_adamw(me1, ws[i][0], own[i], recv[i], ms[i][0], vs[i][0], "adamw_" + nm)]
    big, grad_x, acc_in = lax.optimization_barrier((big, grad_x, acc_in))
    r_in = exchange_wait(in_send, in_recv, gin_thru, in_landing, acc_in)
    r_in, acc_in = lax.optimization_barrier((r_in, acc_in))
    big["w_in"] = [o[None] for o in reduce_and_adamw(me1, w_in[0], gin_f, r_in, m_w_in[0], v_w_in[0], "adamw_w_in")]

    dmod_rows = [2, 1, 8, 19, 18, 20]
    total, kept = gather_and_sum_rows([acc_in, acc_mix, acc_ffn, acc_hg, acc_conv, ddw], dmod_rows)
    dmod_all = kept[:, 0:6, :].reshape(NDEV, 6 * D)
    wcols = w_ada.shape[2]
    gwa, dwa, nmwa, nvwa = ada_backward(call.T, lax.dynamic_slice_in_dim(dmod_all, me * wcols, wcols, axis=1),
                                        w_ada[0], m_w_ada[0], v_w_ada[0])
    ddw_mine = lax.dynamic_slice_in_dim(total[40:40 + KW], me * (D // NDEV), D // NDEV, axis=1)

    small_names = ["b_ada", "pre_norm_tm", "post_norm_tm", "pre_norm_cm", "post_norm_cm", "b_in", "hg_lb_logits",
                   "hg_norm", "conv_db", "conv_ln_g", "conv_ln_b", "conv_dw"]
    recipes = [dmod_rows, [0], [9], [16], [17], [26, 27, 28, 11, 35, 36, 12, 13], (24, 26), [10], [32], [33], [34],
               "dw"]
    small_w = [b_ada, pre_norm_tm, post_norm_tm, pre_norm_cm, post_norm_cm, b_in, hg_lb_logits, hg_norm, conv_db,
               conv_ln_g, conv_ln_b, conv_dw[0]]
    small_m = [m_b_ada, m_pre_norm_tm, m_post_norm_tm, m_pre_norm_cm, m_post_norm_cm, m_b_in, m_hg_lb_logits,
               m_hg_norm, m_conv_db, m_conv_ln_g, m_conv_ln_b, m_conv_dw[0]]
    small_v = [v_b_ada, v_pre_norm_tm, v_post_norm_tm, v_pre_norm_cm, v_post_norm_cm, v_b_in, v_hg_lb_logits,
               v_hg_norm, v_conv_db, v_conv_ln_g, v_conv_ln_b, v_conv_dw[0]]
    small_out, loss_row = adamw_small(total, ddw_mine, recipes, small_w, small_m, small_v)
    loss = loss_row[0, 0]
    sm = {nm: list(o) for nm, o in zip(small_names, small_out)}
    sm["conv_dw"] = [o[None] for o in sm["conv_dw"]]

    order = ["w_ada", "b_ada", "pre_norm_tm", "post_norm_tm", "pre_norm_cm", "post_norm_cm", "w_in", "b_in",
             "hg_lb_logits", "hg_norm", "conv_dw", "conv_db", "conv_ln_g", "conv_ln_b", "w_br_a", "w_br_b", "w_out",
             "w_ff1", "w_ff2"]
    res = dict(sm)
    res.update(big)
    res["w_ada"] = [gwa[None], dwa[None], nmwa[None], nvwa[None]]
    outs = [loss, grad_x[None]]
    for j in range(4):
        outs += [res[nm][j] for nm in order]
    return tuple(outs)
```

```python
import functools

import jax
import jax.numpy as jnp
from jax import lax
from jax.experimental import pallas as pl
from jax.experimental.pallas import tpu as pltpu

F32 = jnp.float32
BF16 = jnp.bfloat16
MESH = pl.DeviceIdType.MESH
AXES = ("x", "y", "c")

D = 1024
HEADS = 8
DK = 128
LANE = 128
CH = 128
LEVELS = 7
KW = 31
HALO = 32
DFF = 4096
NDEV = 8
EPS = 1e-6
DP_SPLIT = (3, 6, 7, 0, 1, 2, 4, 5)
LOSS_ROW = 21
ADAM_LR, ADAM_B1, ADAM_B2, ADAM_EPS, ADAM_WD, ADAM_STEP = 0.001, 0.9, 0.999, 1e-08, 0.01, 10
VMEM_LIMIT = 58 * 1024 * 1024

_NN = (((1,), (0,)), ((), ()))
_NT = (((1,), (1,)), ((), ()))
_TN = (((0,), (0,)), ((), ()))

VSPEC = pl.BlockSpec(memory_space=pltpu.VMEM)
ANYSPEC = pl.BlockSpec(memory_space=pl.ANY)


def _params(*sem):
    return pltpu.CompilerParams(dimension_semantics=sem or None, vmem_limit_bytes=VMEM_LIMIT)


def _dot(a, b, dims):
    return lax.dot_general(a, b, dims, preferred_element_type=F32)


@jax.custom_vjp
def mm_nn(a, b):
    return _dot(a.astype(BF16), b.astype(BF16), _NN)


def _mm_nn_fwd(a, b):
    ab, bb = a.astype(BF16), b.astype(BF16)
    return _dot(ab, bb, _NN), (ab, bb)


def _mm_nn_bwd(res, ct):
    ab, bb = res
    cb = ct.astype(BF16)
    return _dot(cb, bb, _NT), _dot(ab, cb, _TN)


mm_nn.defvjp(_mm_nn_fwd, _mm_nn_bwd)


@jax.custom_vjp
def mm_nt(a, b):
    return _dot(a.astype(BF16), b.astype(BF16), _NT)


def _mm_nt_fwd(a, b):
    ab, bb = a.astype(BF16), b.astype(BF16)
    return _dot(ab, bb, _NT), (ab, bb)


def _mm_nt_bwd(res, ct):
    ab, bb = res
    cb = ct.astype(BF16)
    return _dot(cb, bb, _NN), _dot(cb, ab, _TN)


mm_nt.defvjp(_mm_nt_fwd, _mm_nt_bwd)


@jax.custom_vjp
def mm_tn(a, b):
    return _dot(a.astype(BF16), b.astype(BF16), _TN)


def _mm_tn_fwd(a, b):
    ab, bb = a.astype(BF16), b.astype(BF16)
    return _dot(ab, bb, _TN), (ab, bb)


def _mm_tn_bwd(res, ct):
    ab, bb = res
    cb = ct.astype(BF16)
    return _dot(bb, cb, _NT), _dot(ab, cb, _NN)


mm_tn.defvjp(_mm_tn_fwd, _mm_tn_bwd)


def _rms(x):
    return x * lax.rsqrt(jnp.mean(x * x, axis=-1, keepdims=True) + EPS)


def _silu(x):
    return x * jax.nn.sigmoid(x)


def f_modulate(x, gain, sc, sh):
    return _rms(x) * gain * (1.0 + sc) + sh


def f_residual(x, y, gate, gain):
    return x + gate * (_rms(y) * gain)


def f_merge(ga, gb, ya, yb):
    return jax.nn.sigmoid(ga) * ya + jax.nn.sigmoid(gb) * yb


def f_head_out(o, og, hg):
    heads = [_rms(o[:, h * DK:(h + 1) * DK]) for h in range(HEADS)]
    return jnp.concatenate(heads, axis=1) * hg * _silu(og)


def f_conv_act(u, g, b):
    mu = jnp.mean(u, axis=-1, keepdims=True)
    var = jnp.mean(jnp.square(u - mu), axis=-1, keepdims=True)
    return _silu((u - mu) * lax.rsqrt(var + EPS) * g + b)


def f_glu(cv, cg):
    return cv * jax.nn.sigmoid(cg)


def _split2(x):
    hi = x.astype(BF16)
    return hi, (x - hi.astype(F32)).astype(BF16)


def _tri(transposed):
    i = lax.broadcasted_iota(jnp.int32, (CH, CH), 1 if transposed else 0)
    t = lax.broadcasted_iota(jnp.int32, (CH, CH), 0 if transposed else 1)
    return jnp.where(t <= i, 1.0, 0.0).astype(BF16)


def _blocks3(x, rows):
    return x.reshape(CH // rows, rows, x.shape[-1])


def _mid_broadcast(b, lev):
    h = 1 << (LEVELS - 1 - lev)
    if h >= 4:
        x3 = _blocks3(b, 2 * h)
        return jnp.broadcast_to(x3[:, h - 1:h, :], x3.shape).reshape(b.shape)
    x3 = _blocks3(b, 8)
    sub = lax.broadcasted_iota(jnp.int32, x3.shape, 1)
    out = None
    for first in range(0, 8, 2 * h):
        piece = jnp.broadcast_to(x3[:, first + h - 1:first + h, :], x3.shape)
        out = piece if out is None else jnp.where(sub >= first, piece, out)
    return out.reshape(b.shape)


def _mid_scatter(d, lev):
    h = 1 << (LEVELS - 1 - lev)
    if h >= 4:
        x3 = _blocks3(d, 2 * h)
        row = lax.broadcasted_iota(jnp.int32, x3.shape, 1)
        total = jnp.sum(x3, axis=1, keepdims=True)
        return jnp.where(row == h - 1, total, 0.0).reshape(d.shape)
    x3 = _blocks3(d, 8)
    sub = lax.broadcasted_iota(jnp.int32, x3.shape, 1)
    out = jnp.zeros_like(x3)
    for first in range(0, 8, 2 * h):
        inside = (sub >= first) & (sub < first + 2 * h)
        total = jnp.sum(jnp.where(inside, x3, 0.0), axis=1, keepdims=True)
        out = jnp.where(sub == first + h - 1, total, out)
    return out.reshape(d.shape)


@jax.custom_vjp
def decay_sums(g):
    hi, lo = _split2(g)
    tri = _tri(False)
    b = _dot(tri, hi, _NN) + _dot(tri, lo, _NN)
    return (b,) + tuple(b - _mid_broadcast(b, lev) for lev in range(LEVELS))


def _decay_sums_fwd(g):
    return decay_sums(g), None


def _decay_sums_bwd(_, cts):
    db = cts[0]
    for lev in range(LEVELS):
        db = db + cts[1 + lev] - _mid_scatter(cts[1 + lev], lev)
    hi, lo = _split2(db)
    tri = _tri(True)
    return (_dot(tri, hi, _NN) + _dot(tri, lo, _NN),)


decay_sums.defvjp(_decay_sums_fwd, _decay_sums_bwd)


def _score_masks():
    i = lax.broadcasted_iota(jnp.int32, (CH, CH), 0)
    j = lax.broadcasted_iota(jnp.int32, (CH, CH), 1)
    masks = [i == j]
    for lev in range(LEVELS):
        sh = LEVELS - 1 - lev
        same = (i >> (sh + 1)) == (j >> (sh + 1))
        masks.append(same & (((i >> sh) & 1) == 1) & (((j >> sh) & 1) == 0))
    return [jnp.where(m, 1.0, 0.0) for m in masks]


def f_hgrn_chunk(q_r, f_r, v, logits):
    l0, l1 = logits[0:1, :], logits[1:2, :]
    mx = lax.stop_gradient(jnp.maximum(l0, l1))
    e0, e1 = jnp.exp(l0 - mx), jnp.exp(l1 - mx)
    lb = e0 / (e0 + e1)
    q = _silu(q_r)
    f = lb + (1.0 - lb) * jax.nn.sigmoid(f_r)
    k = 1.0 - f
    sums = decay_sums(jnp.log(f))
    b = sums[0]
    btot = b[CH - 1:CH, :]
    qe = q * jnp.exp(b)
    ke = k * jnp.exp(btot - b)
    dec = jnp.exp(btot)
    qs, ks = [q], [k]
    row = lax.broadcasted_iota(jnp.int32, b.shape, 0)
    for lev in range(LEVELS):
        upper = ((row >> (LEVELS - 1 - lev)) & 1) == 1
        e = sums[1 + lev]
        qs.append(q * jnp.exp(jnp.where(upper, e, 0.0)))
        ks.append(k * jnp.exp(jnp.where(upper, 0.0, -e)))
    masks = _score_masks()
    intra, ut = [], []
    for h in range(HEADS):
        sl = slice(h * DK, (h + 1) * DK)
        sc = None
        for lev in range(LEVELS + 1):
            s = mm_nt(qs[lev][:, sl], ks[lev][:, sl]) * masks[lev]
            sc = s if sc is None else sc + s
        intra.append(mm_nn(sc, v[:, sl]))
        ut.append(mm_tn(v[:, sl], ke[:, sl]))
    return jnp.concatenate(intra, axis=1), qe, jnp.concatenate(ut, axis=1), dec


def _inter(qe_b, st_ref, rows):
    out = []
    for ci in range(rows // CH):
        st = st_ref[ci].astype(BF16)
        heads = [_dot(qe_b[ci * CH:(ci + 1) * CH, h * DK:(h + 1) * DK], st[:, h * DK:(h + 1) * DK], _NT)
                 for h in range(HEADS)]
        out.append(jnp.concatenate(heads, axis=1))
    return jnp.concatenate(out, axis=0)


def _shift_stack(src, dst, rows):
    dst[0, 0:rows, :] = src[0:rows, :]
    for b in range(1, 8):
        dst[b, 0:rows - 8, :] = src[pl.ds(b, rows - 8), :]


def _shifted(stack, offset, rows, lanes=slice(None)):
    return stack[offset % 8, pl.ds(8 * (offset // 8), rows), lanes]


def _adamw(w, g, m, v):
    m = ADAM_B1 * m + (1.0 - ADAM_B1) * g
    v = ADAM_B2 * v + (1.0 - ADAM_B2) * jnp.square(g)
    m_hat = m / (1.0 - ADAM_B1 ** ADAM_STEP)
    v_hat = v / (1.0 - ADAM_B2 ** ADAM_STEP)
    delta = -ADAM_LR * (m_hat / (jnp.sqrt(v_hat) + ADAM_EPS) + ADAM_WD * w)
    return delta, m, v


def _me():
    return lax.axis_index("x"), lax.axis_index("y"), lax.axis_index("c")


def _peer(k):
    x, y, c = _me()
    mask = k + 1
    px = (1 - x) if (mask >> 2) & 1 else x
    py = (1 - y) if (mask >> 1) & 1 else y
    pc = (1 - c) if mask & 1 else c
    return (px, py, pc), 4 * px + 2 * py + pc


def ada_forward(c, w_ada, b_ada, dw):
    wcols = w_ada.shape[1]

    def body(c_ref, w_ref, b_ref, dw_ref, mod_ref, call_ref, dwall_ref, part_ref, modp_ref, send_sems, recv_sems):
        x, y, cc = _me()
        me = 4 * x + 2 * y + cc
        call_ref[me] = _silu(c_ref[...])
        dwall_ref[me] = dw_ref[...]
        sends = []
        for k in range(NDEV - 1):
            dev, pidx = _peer(k)
            cp = pltpu.make_async_remote_copy(dwall_ref.at[me], dwall_ref.at[me], send_sems.at[2 * (NDEV - 1) + k],
                                              recv_sems.at[2 * (NDEV - 1) + k], device_id=dev, device_id_type=MESH)
            cp.start()
            sends.append(cp)
        for k in range(NDEV - 1):
            dev, _ = _peer(k)
            cp = pltpu.make_async_remote_copy(call_ref.at[me], call_ref.at[me], send_sems.at[k], recv_sems.at[k],
                                              device_id=dev, device_id_type=MESH)
            cp.start()
            sends.append(cp)
        for k in range(NDEV - 1):
            _, pidx = _peer(k)
            pltpu.make_async_remote_copy(call_ref.at[pidx], call_ref.at[pidx], send_sems.at[k], recv_sems.at[k],
                                         device_id=_peer(k)[0], device_id_type=MESH).wait_recv()
        call = jnp.concatenate([call_ref[r] for r in range(NDEV)], axis=0)
        part = _dot(call.astype(BF16), w_ref[...].astype(BF16), _NN)
        for r in range(NDEV):
            part_ref[r] = part[r:r + 1, :]
        modp_ref[me] = part_ref[me]
        for k in range(NDEV - 1):
            dev, pidx = _peer(k)
            cp = pltpu.make_async_remote_copy(part_ref.at[pidx], modp_ref.at[me], send_sems.at[NDEV - 1 + k],
                                              recv_sems.at[NDEV - 1 + k], device_id=dev, device_id_type=MESH)
            cp.start()
            sends.append(cp)
        for k in range(NDEV - 1):
            dev, pidx = _peer(k)
            pltpu.make_async_remote_copy(part_ref.at[pidx], modp_ref.at[pidx], send_sems.at[NDEV - 1 + k],
                                         recv_sems.at[NDEV - 1 + k], device_id=dev, device_id_type=MESH).wait_recv()
        for k in range(NDEV - 1):
            dev, pidx = _peer(k)
            pltpu.make_async_remote_copy(dwall_ref.at[pidx], dwall_ref.at[pidx], send_sems.at[2 * (NDEV - 1) + k],
                                         recv_sems.at[2 * (NDEV - 1) + k], device_id=dev,
                                         device_id_type=MESH).wait_recv()
        for cp in sends:
            cp.wait_send()
        mod_ref[...] = jnp.concatenate([modp_ref[r] for r in range(NDEV)], axis=1) + b_ref[...]

    mod, call, dw_all = pl.pallas_call(
        body, name="ada_forward",
        out_shape=(jax.ShapeDtypeStruct((1, NDEV * wcols), F32), jax.ShapeDtypeStruct((NDEV, 1, D), F32),
                   jax.ShapeDtypeStruct((NDEV,) + dw.shape, F32)),
        in_specs=[VSPEC] * 4, out_specs=(VSPEC,) * 3,
        scratch_shapes=[pltpu.VMEM((NDEV, 1, wcols), F32), pltpu.VMEM((NDEV, 1, wcols), F32),
                        pltpu.SemaphoreType.DMA((3 * (NDEV - 1),)), pltpu.SemaphoreType.DMA((3 * (NDEV - 1),))],
        compiler_params=_params(),
    )(c, w_ada, b_ada, dw)
    return mod, call.reshape(NDEV, D), dw_all


def _comm_sems(n, local):
    sems = [pltpu.SemaphoreType.DMA((7 * n,)), pltpu.SemaphoreType.DMA((7 * n,))]
    return sems + ([pltpu.SemaphoreType.DMA((n,))] if local else [])


def _gather2_ops(ins, outs, send_sems, recv_sems, local_sems):
    n = len(ins)
    x, y, c = _me()
    me, sibling = (x, y, c), (x, y, 1 - c)
    chips = [(1 - x, y), (x, 1 - y), (1 - x, 1 - y)]

    def slot(p):
        return 4 * p[0] + 2 * p[1] + p[2]

    def copy(a, k, block, to, src=None):
        return pltpu.make_async_remote_copy(
            src_ref=outs[a].at[slot(block)] if src is None else src, dst_ref=outs[a].at[slot(block)],
            send_sem=send_sems.at[a * 7 + k], recv_sem=recv_sems.at[a * 7 + k], device_id=to, device_id_type=MESH)

    def local(a):
        return pltpu.make_async_copy(ins[a], outs[a].at[slot(me)], local_sems.at[a])

    def first(a):
        return [copy(a, 0, me, sibling, src=ins[a])] + [copy(a, 1 + j, me, (*chip, c), src=ins[a])
                                                        for j, chip in enumerate(chips)]

    def passed(a):
        return [copy(a, 4 + j, (*chip, c), sibling) for j, chip in enumerate(chips)]

    def start():
        for a in range(n):
            local(a).start()
            for cp in first(a):
                cp.start()

    def forward():
        for j, chip in enumerate(chips):
            for a in range(n):
                copy(a, 1 + j, (*chip, c), me).wait_recv()
                passed(a)[j].start()

    def finish():
        for a in range(n):
            copy(a, 0, sibling, me).wait_recv()
            for j, chip in enumerate(chips):
                copy(a, 4 + j, (*chip, 1 - c), me).wait_recv()
        for a in range(n):
            for cp in first(a) + passed(a):
                cp.wait_send()
            local(a).wait()

    return start, forward, finish


def _exchange_ops(ins, outs, send_sems, recv_sems):
    n = len(ins)

    def copy(a, k):
        dev, pidx = _peer(k)
        return pltpu.make_async_remote_copy(ins[a].at[pidx], outs[a].at[k], send_sems.at[a * 7 + k],
                                            recv_sems.at[a * 7 + k], device_id=dev, device_id_type=MESH)

    def start():
        for k in range(NDEV - 1):
            for a in range(n):
                copy(a, k).start()

    def finish():
        for k in range(NDEV - 1):
            for a in range(n):
                copy(a, k).wait_recv()
        for k in range(NDEV - 1):
            for a in range(n):
                copy(a, k).wait_send()

    return start, finish


def _gathered(shards):
    return tuple(jax.ShapeDtypeStruct((NDEV,) + s.shape, s.dtype) for s in shards)


def _exchanged(blocks):
    return tuple(jax.ShapeDtypeStruct((NDEV - 1,) + b.shape[1:], b.dtype) for b in blocks)


def exchange_start(blocks):
    landing = lax.empty((NDEV - 1,) + blocks.shape[1:], blocks.dtype)
    hbm = pl.BlockSpec(memory_space=pltpu.HBM)
    sem = pl.BlockSpec(memory_space=pltpu.SEMAPHORE)

    def body(src_ref, land_ref, send_sems, recv_sems, src_thru, land_thru, token):
        for k in range(NDEV - 1):
            dev, pidx = _peer(k)
            pltpu.make_async_remote_copy(src_ref.at[pidx], land_ref.at[k], send_sems.at[k], recv_sems.at[k],
                                         device_id=dev, device_id_type=MESH).start()
        token[...] = jnp.zeros_like(token)

    return pl.pallas_call(
        body, name="exchange_start",
        out_shape=(pltpu.SemaphoreType.DMA((NDEV - 1,)), pltpu.SemaphoreType.DMA((NDEV - 1,)),
                   pltpu.HBM(blocks.shape, blocks.dtype), pltpu.HBM(landing.shape, landing.dtype),
                   jax.ShapeDtypeStruct((8, 128), F32)),
        in_specs=(hbm, hbm), out_specs=(sem, sem, hbm, hbm, VSPEC), input_output_aliases={0: 2, 1: 3},
        compiler_params=pltpu.CompilerParams(has_side_effects=pltpu.SideEffectType.DATAFLOW_SIDE_EFFECTING),
    )(pltpu.with_memory_space_constraint(blocks, pltpu.HBM), pltpu.with_memory_space_constraint(landing, pltpu.HBM))


def exchange_wait(send_sems, recv_sems, src_thru, land_thru, after):
    hbm = pl.BlockSpec(memory_space=pltpu.HBM)
    sem = pl.BlockSpec(memory_space=pltpu.SEMAPHORE)

    def body(src_ref, land_ref, send_sems, recv_sems, after_ref, src_dead, got_ref):
        for k in range(NDEV - 1):
            dev, pidx = _peer(k)
            cp = pltpu.make_async_remote_copy(src_ref.at[pidx], land_ref.at[k], send_sems.at[k], recv_sems.at[k],
                                              device_id=dev, device_id_type=MESH)
            cp.wait_send()
            cp.wait_recv()

    return pl.pallas_call(
        body, name="exchange_wait",
        out_shape=(pltpu.HBM(src_thru.shape, src_thru.dtype), pltpu.HBM(land_thru.shape, land_thru.dtype)),
        in_specs=(hbm, hbm, sem, sem, ANYSPEC), out_specs=(hbm, hbm), input_output_aliases={0: 0, 1: 1},
        compiler_params=pltpu.CompilerParams(has_side_effects=pltpu.SideEffectType.DATAFLOW_SIDE_EFFECTING),
    )(src_thru, land_thru, send_sems, recv_sems, after)[1]


def gather_and_sum_rows(parts, keep_rows):
    n = len(parts)
    offs = [sum(p.shape[0] for p in parts[:i]) for i in range(n)]
    rows = sum(p.shape[0] for p in parts)

    def body(*refs):
        sum_ref, keep_ref, buf_ref, send_sems, recv_sems = refs[n:]
        x, y, c = _me()
        me = 4 * x + 2 * y + c
        for i in range(n):
            buf_ref[me, offs[i]:offs[i] + parts[i].shape[0], :] = refs[i][...]

        def copy(k, block):
            return pltpu.make_async_remote_copy(buf_ref.at[block], buf_ref.at[block], send_sems.at[k],
                                                recv_sems.at[k], device_id=_peer(k)[0], device_id_type=MESH)

        for k in range(NDEV - 1):
            copy(k, me).start()
        for k in range(NDEV - 1):
            copy(k, _peer(k)[1]).wait_recv()
        for k in range(NDEV - 1):
            copy(k, me).wait_send()
        total = buf_ref[0]
        for r in range(1, NDEV):
            total = total + buf_ref[r]
        sum_ref[...] = total
        keep_ref[...] = jnp.zeros_like(keep_ref)
        for r in range(NDEV):
            for j, src in enumerate(keep_rows):
                keep_ref[r, j:j + 1, :] = buf_ref[r, src:src + 1, :]

    return pl.pallas_call(
        body, name="gather_and_sum_rows",
        out_shape=(jax.ShapeDtypeStruct((rows, D), F32), jax.ShapeDtypeStruct((NDEV, 8, D), F32)),
        in_specs=[VSPEC] * n, out_specs=(VSPEC, VSPEC),
        scratch_shapes=[pltpu.VMEM((NDEV, rows, D), F32), pltpu.SemaphoreType.DMA((NDEV - 1,)),
                        pltpu.SemaphoreType.DMA((NDEV - 1,))],
        compiler_params=_params(),
    )(*parts)


def _arrival_order():
    x, y, c = _me()
    near, far = [(1 - x, y), (x, 1 - y)], (1 - x, 1 - y)
    devs = ([(x, y, c), (x, y, 1 - c)] + [(*q, c) for q in near] + [(*q, 1 - c) for q in near]
            + [(*far, c), (*far, 1 - c)])
    return jnp.stack([4 * d[0] + 2 * d[1] + d[2] for d in devs]).astype(jnp.int32)


def in_proj(x, gain, sc, sh, w_shard, b_in):
    t = x.shape[0]
    tm = min(256, t)
    nm = t // tm
    order = _arrival_order()

    def body(order_ref, x_ref, g_ref, sc_ref, sh_ref, b_ref, wsh_ref, p_ref, h_ref, wall_ref,
             wbuf, h_all, send_sems, recv_sems, local_sems):
        k, m = pl.program_id(0), pl.program_id(1)
        mx, my, mc = _me()
        me, sibling = (mx, my, mc), (mx, my, 1 - mc)
        chips = [(1 - mx, my), (mx, 1 - my), (1 - mx, 1 - my)]

        def slot(d):
            return 4 * d[0] + 2 * d[1] + d[2]

        def copy(sem, block, to, src=None):
            return pltpu.make_async_remote_copy(
                src_ref=wall_ref.at[slot(block)] if src is None else src, dst_ref=wall_ref.at[slot(block)],
                send_sem=send_sems.at[sem], recv_sem=recv_sems.at[sem], device_id=to, device_id_type=MESH)

        own = pltpu.make_async_copy(wsh_ref, wall_ref.at[slot(me)], local_sems.at[0])
        first = [copy(0, me, sibling, src=wsh_ref)] + [copy(1 + j, me, (*q, mc), src=wsh_ref)
                                                       for j, q in enumerate(chips)]
        passed = [copy(4 + j, (*q, mc), sibling) for j, q in enumerate(chips)]

        def load(kk):
            src = wsh_ref if kk == 0 else wall_ref.at[order_ref[kk]]
            return pltpu.make_async_copy(src, wbuf.at[kk % 2], local_sems.at[1 + kk % 2])

        def arrived(kk):
            if kk == 1:
                copy(0, sibling, me).wait_recv()
            elif kk in (2, 3, 6):
                j = {2: 0, 3: 1, 6: 2}[kk]
                copy(1 + j, (*chips[j], mc), me).wait_recv()
                passed[j].start()
                if kk == 3:
                    first[3].start()
            else:
                j = {4: 0, 5: 1, 7: 2}[kk]
                copy(4 + j, (*chips[j], 1 - mc), me).wait_recv()

        @pl.when((k == 0) & (m == 0))
        def _():
            own.start()
            for cp in first[:3]:
                cp.start()
            load(0).start()
            load(0).wait()

        for kk in range(1, NDEV):
            @pl.when((k == kk - 1) & (m == nm // 2))
            def _(kk=kk):
                arrived(kk)
                load(kk).start()

            @pl.when((k == kk) & (m == 0))
            def _(kk=kk):
                load(kk).wait()

        rows = pl.ds(pl.multiple_of(m * tm, tm), tm)

        @pl.when(k == 0)
        def _():
            h_all[rows, :] = f_modulate(x_ref[...], g_ref[...], sc_ref[...], sh_ref[...]).astype(BF16)

        p_ref[...] = _dot(h_all[rows, :], wbuf[k % 2], _NN) + b_ref[...]

        @pl.when((k == NDEV - 1) & (m == nm - 1))
        def _():
            for cp in first + passed:
                cp.wait_send()
            own.wait()
            out = pltpu.make_async_copy(h_all, h_ref, local_sems.at[0])
            out.start()
            out.wait()

    vec = pl.BlockSpec((1, D), lambda k, m, o: (0, 0))
    return pl.pallas_call(
        body, name="in_proj",
        out_shape=(jax.ShapeDtypeStruct((t, NDEV * D), F32), jax.ShapeDtypeStruct((t, D), BF16),
                   jax.ShapeDtypeStruct((NDEV, D, D), BF16)),
        grid_spec=pltpu.PrefetchScalarGridSpec(
            num_scalar_prefetch=1, grid=(NDEV, nm),
            in_specs=[pl.BlockSpec((tm, D), lambda k, m, o: (jnp.where(k == 0, m, 0), 0)), vec, vec, vec,
                      pl.BlockSpec((1, D), lambda k, m, o: (0, o[k])), ANYSPEC],
            out_specs=(pl.BlockSpec((tm, D), lambda k, m, o: (m, o[k])), ANYSPEC, ANYSPEC),
            scratch_shapes=[pltpu.VMEM((2, D, D), BF16), pltpu.VMEM((t, D), BF16), pltpu.SemaphoreType.DMA((7,)),
                            pltpu.SemaphoreType.DMA((7,)), pltpu.SemaphoreType.DMA((3,))]),
        compiler_params=_params("arbitrary", "arbitrary"),
    )(order, x, gain, sc, sh, b_in, w_shard)


def hgrn_local(p, logits, shards):
    t = p.shape[0]
    nc = t // CH
    n = len(shards)

    def body(*refs):
        q_ref, f_ref, v_ref, l_ref = refs[:4]
        intra_ref, qe_ref, st_ref, dec_ref = refs[4 + n:8 + n]
        state = refs[8 + 2 * n]
        start, forward, finish = _gather2_ops(refs[4:4 + n], refs[8 + n:8 + 2 * n], *refs[9 + 2 * n:])
        step = pl.program_id(0)

        @pl.when(step == 0)
        def _():
            start()
            state[...] = jnp.zeros_like(state)

        intra, qe, ut, dec = f_hgrn_chunk(q_ref[...], f_ref[...], v_ref[...], l_ref[...])
        intra_ref[...] = intra
        qe_ref[...] = qe.astype(BF16)
        dec_ref[...] = dec
        s = state[...]
        st_ref[...] = s
        state[...] = dec * s + ut
        pl.when(step == (3 * nc) // 4)(forward)
        pl.when(step == nc - 1)(finish)

    col = lambda j: pl.BlockSpec((CH, D), lambda c: (c, j))
    res = pl.pallas_call(
        body, name="hgrn_local", grid=(nc,),
        out_shape=(jax.ShapeDtypeStruct((t, D), F32), jax.ShapeDtypeStruct((t, D), BF16),
                   jax.ShapeDtypeStruct((nc, DK, D), F32), jax.ShapeDtypeStruct((nc, 1, D), F32)) + _gathered(shards),
        in_specs=[col(0), col(1), col(2), pl.BlockSpec((2, D), lambda c: (0, 0))] + [ANYSPEC] * n,
        out_specs=(pl.BlockSpec((CH, D), lambda c: (c, 0)), pl.BlockSpec((CH, D), lambda c: (c, 0)),
                   pl.BlockSpec((None, DK, D), lambda c: (c, 0, 0)), pl.BlockSpec((None, 1, D), lambda c: (c, 0, 0)))
        + (ANYSPEC,) * n,
        scratch_shapes=[pltpu.VMEM((DK, D), F32)] + _comm_sems(n, True),
        compiler_params=_params("arbitrary"),
    )(p, p, p, logits, *shards)
    return res[:4] + (res[4:],)


def _conv_tile(t):
    return min(256, t)


def conv_forward(p, dw, db, ln_g, ln_b, shards):
    t = p.shape[0]
    tm = _conv_tile(t)
    per = tm // HALO
    n = len(shards)
    nt = t // tm

    def body(*refs):
        cv_ref, cg_ref, cvp_ref, cgp_ref, dw_ref, db_ref, g_ref, b_ref = refs[:8]
        yc_ref, act_ref = refs[8 + n:10 + n]
        uext, ush = refs[10 + 2 * n:12 + 2 * n]
        start, forward, finish = _gather2_ops(refs[8:8 + n], refs[10 + n:10 + 2 * n], *refs[12 + 2 * n:])
        step = pl.program_id(0)
        pl.when(step == 0)(start)
        pl.when(step == nt - 1)(forward)
        uext[0:HALO, :] = jnp.where(step == 0, 0.0, f_glu(cvp_ref[...], cgp_ref[...]))
        uext[HALO:HALO + tm, :] = f_glu(cv_ref[...], cg_ref[...])
        _shift_stack(uext, ush, tm + HALO)
        acc = jnp.zeros((tm, D), F32) + db_ref[...]
        for w in range(KW):
            acc = acc + dw_ref[w:w + 1, :] * _shifted(ush, HALO - KW + 1 + w, tm)
        yc_ref[...] = acc
        act_ref[...] = f_conv_act(acc, g_ref[...], b_ref[...]).astype(BF16)
        pl.when(step == nt - 1)(finish)

    vec = pl.BlockSpec((1, D), lambda m: (0, 0))
    prev = lambda j: pl.BlockSpec((HALO, D), lambda m: (jnp.maximum(m * per - 1, 0), j))
    res = pl.pallas_call(
        body, name="conv_forward", grid=(nt,),
        out_shape=(jax.ShapeDtypeStruct((t, D), F32), jax.ShapeDtypeStruct((t, D), BF16)) + _gathered(shards),
        in_specs=[pl.BlockSpec((tm, D), lambda m: (m, 4)), pl.BlockSpec((tm, D), lambda m: (m, 5)), prev(4), prev(5),
                  pl.BlockSpec((HALO, D), lambda m: (0, 0)), vec, vec, vec] + [ANYSPEC] * n,
        out_specs=(pl.BlockSpec((tm, D), lambda m: (m, 0)), pl.BlockSpec((tm, D), lambda m: (m, 0))) + (ANYSPEC,) * n,
        scratch_shapes=[pltpu.VMEM((HALO + tm, D), F32), pltpu.VMEM((8, tm + HALO, D), F32)] + _comm_sems(n, True),
        compiler_params=_params("arbitrary"),
    )(p, p, p, p, dw, db, ln_g, ln_b, *shards)
    return res[0], res[1], res[2:]


def _mix_tile(t):
    return min(256, t)


def _mix_forward_tile(x_ref, og_ref, ga_ref, gb_ref, qe_ref, intra_ref, st_ref, act_ref, wa_ref, wb_ref, wo_ref,
                      hg_ref, rows):
    o = _inter(qe_ref[...], st_ref, rows) + intra_ref[...]
    a = f_head_out(o, og_ref[...], hg_ref[...])
    ya = _dot(a.astype(BF16), wa_ref[...], _NN)
    yb = _dot(act_ref[...], wb_ref[...], _NN)
    merged = f_merge(ga_ref[...], gb_ref[...], ya, yb)
    y = _dot(merged.astype(BF16), wo_ref[...], _NN)
    return o, a, ya, yb, merged, y


def _mix_specs(tm, tiles=None):
    at = (lambda m: m) if tiles is None else (lambda m: tiles - 1 - m)
    col = lambda j: pl.BlockSpec((tm, D), lambda m: (at(m), j))
    row = pl.BlockSpec((tm, D), lambda m: (at(m), 0))
    per_chunk = lambda rows: pl.BlockSpec((tm // CH, rows, D), lambda m: (at(m), 0, 0))
    return col, row, pl.BlockSpec((1, D), lambda m: (0, 0)), per_chunk


def mix_forward(x, p, qe, intra, st_prev, act, wa, wb, wo, hg, g1, post):
    t = x.shape[0]
    tm = _mix_tile(t)

    def body(x_ref, og_ref, ga_ref, gb_ref, qe_ref, intra_ref, st_ref, act_ref, wa_ref, wb_ref, wo_ref, hg_ref,
             g1_ref, post_ref, x1_ref):
        y = _mix_forward_tile(x_ref, og_ref, ga_ref, gb_ref, qe_ref, intra_ref, st_ref, act_ref, wa_ref, wb_ref,
                              wo_ref, hg_ref, tm)[-1]
        x1_ref[...] = f_residual(x_ref[...], y, g1_ref[...], post_ref[...])

    col, row, vec, per_chunk = _mix_specs(tm)
    return pl.pallas_call(
        body, name="mix_forward", grid=(t // tm,),
        out_shape=jax.ShapeDtypeStruct((t, D), F32),
        in_specs=[row, col(3), col(6), col(7), row, row, per_chunk(DK), row, VSPEC, VSPEC, VSPEC, vec, vec, vec],
        out_specs=row,
        compiler_params=_params("parallel"),
    )(x, p, p, p, qe, intra, st_prev, act, wa, wb, wo, hg, g1, post)


def ffn_forward_backward(x1, target, w1, w2, pre, sc, sh, g2, post):
    t = x1.shape[0]
    tm = min(256, t)
    nb = w1.shape[0]
    fb = w1.shape[2]

    def body(x_ref, tg_ref, w1_ref, w2_ref, pre_ref, sc_ref, sh_ref, g2_ref, post_ref,
             dx_ref, h2_ref, r_ref, dz_ref, dy2_ref, acc_ref, z_sc):
        @pl.when(pl.program_id(0) == 0)
        def _():
            acc_ref[...] = jnp.zeros_like(acc_ref)

        x1v = x_ref[...]
        h2, vjp_h = jax.vjp(f_modulate, x1v, pre_ref[...], sc_ref[...], sh_ref[...])
        h2b = h2.astype(BF16)
        h2_ref[...] = h2b
        y2 = jnp.zeros((tm, D), F32)
        for n in range(nb):
            z = _dot(h2b, w1_ref[n], _NN)
            z_sc[:, n * fb:(n + 1) * fb] = z
            r = jnp.square(jnp.maximum(z, 0.0)).astype(BF16)
            r_ref[:, n * fb:(n + 1) * fb] = r
            y2 = y2 + _dot(r, w2_ref[n], _NN)
        out, vjp_r = jax.vjp(f_residual, x1v, y2, g2_ref[...], post_ref[...])
        err = out - tg_ref[...]
        tok = jnp.mean(jnp.square(err), axis=-1, keepdims=True)
        acc_ref[5:6, :] += 0.5 * jnp.sum(tok, axis=0, keepdims=True)
        dx_a, dy2, dg2, dpost = vjp_r(err * (1.0 / D))
        dy2b = dy2.astype(BF16)
        dy2_ref[...] = dy2b
        dh2 = jnp.zeros((tm, D), F32)
        for n in range(nb):
            dr = _dot(dy2b, w2_ref[n], _NT)
            dz = (dr * (2.0 * jnp.maximum(z_sc[:, n * fb:(n + 1) * fb], 0.0))).astype(BF16)
            dz_ref[:, n * fb:(n + 1) * fb] = dz
            dh2 = dh2 + _dot(dz, w1_ref[n], _NT)
        dx_b, dpre, dsc, dsh = vjp_h(dh2)
        dx_ref[...] = dx_a + dx_b
        acc_ref[0:1, :] += dpre
        acc_ref[1:2, :] += dpost
        acc_ref[2:3, :] += dsc
        acc_ref[3:4, :] += dsh
        acc_ref[4:5, :] += dg2

    row = pl.BlockSpec((tm, D), lambda m: (m, 0))
    wide = pl.BlockSpec((tm, DFF), lambda m: (m, 0))
    vec = pl.BlockSpec((1, D), lambda m: (0, 0))
    return pl.pallas_call(
        body, name="ffn_forward_backward", grid=(t // tm,),
        out_shape=(jax.ShapeDtypeStruct((t, D), F32), jax.ShapeDtypeStruct((t, D), BF16),
                   jax.ShapeDtypeStruct((t, DFF), BF16), jax.ShapeDtypeStruct((t, DFF), BF16),
                   jax.ShapeDtypeStruct((t, D), BF16), jax.ShapeDtypeStruct((8, D), F32)),
        in_specs=[row, row, VSPEC, VSPEC, vec, vec, vec, vec, vec],
        out_specs=(row, row, wide, wide, row, pl.BlockSpec((8, D), lambda m: (0, 0))),
        scratch_shapes=[pltpu.VMEM((tm, DFF), F32)],
        compiler_params=_params("arbitrary"),
    )(x1, target, w1, w2, pre, sc, sh, g2, post)


def mix_backward(x, p, qe, intra, st_prev, act, wa, wb, wo, hg, g1, post, dx1, dec):
    t = x.shape[0]
    tm = _mix_tile(t)
    nc = t // CH

    def body(x_ref, og_ref, ga_ref, gb_ref, qe_ref, intra_ref, st_ref, act_ref, wa_ref, wb_ref, wo_ref, hg_ref,
             g1_ref, post_ref, dx1_ref, dec_ref,
             dp_ref, do_ref, dqe_ref, gt_ref, dact_ref, a_ref, mg_ref, dya_ref, dyb_ref, dy_ref, acc_ref, gstate):
        @pl.when(pl.program_id(0) == 0)
        def _():
            acc_ref[...] = jnp.zeros_like(acc_ref)
            gstate[...] = jnp.zeros_like(gstate)

        o, a, ya, yb, merged, y = _mix_forward_tile(x_ref, og_ref, ga_ref, gb_ref, qe_ref, intra_ref, st_ref, act_ref,
                                                    wa_ref, wb_ref, wo_ref, hg_ref, tm)
        a_ref[...] = a.astype(BF16)
        mg_ref[...] = merged.astype(BF16)
        _, vjp_r = jax.vjp(f_residual, x_ref[...], y, g1_ref[...], post_ref[...])
        _, dy, dg1, dpost = vjp_r(dx1_ref[...])
        dyb16 = dy.astype(BF16)
        dy_ref[...] = dyb16
        dmerged = _dot(dyb16, wo_ref[...], _NT)
        _, vjp_m = jax.vjp(f_merge, ga_ref[...], gb_ref[...], ya, yb)
        dga, dgb, dya, dyb = vjp_m(dmerged)
        dp_ref[:, D:2 * D] = dga.astype(BF16)
        dp_ref[:, 2 * D:3 * D] = dgb.astype(BF16)
        dya16, dyb16b = dya.astype(BF16), dyb.astype(BF16)
        dya_ref[...] = dya16
        dyb_ref[...] = dyb16b
        da = _dot(dya16, wa_ref[...], _NT)
        dact_ref[...] = _dot(dyb16b, wb_ref[...], _NT)
        _, vjp_a = jax.vjp(f_head_out, o, og_ref[...], hg_ref[...])
        do, dog, dhg = vjp_a(da)
        dp_ref[:, 0:D] = dog.astype(BF16)
        do_ref[...] = do
        do16 = do.astype(BF16)
        qe16 = qe_ref[...]
        for ci in reversed(range(tm // CH)):
            st = st_ref[ci].astype(BF16)
            rows = slice(ci * CH, (ci + 1) * CH)
            dqe, vt = [], []
            for h in range(HEADS):
                sl = slice(h * DK, (h + 1) * DK)
                dqe.append(_dot(do16[rows, sl], st[:, sl], _NN))
                vt.append(_dot(do16[rows, sl], qe16[rows, sl], _TN))
            dqe_ref[rows, :] = jnp.concatenate(dqe, axis=1)
            g = gstate[...]
            gt_ref[ci] = g
            gstate[...] = dec_ref[ci] * g + jnp.concatenate(vt, axis=1)
        acc_ref[0:1, :] += dg1
        acc_ref[1:2, :] += dpost
        acc_ref[2:3, :] += dhg
        acc_ref[3:4, :] += jnp.sum(dog, axis=0, keepdims=True)
        acc_ref[4:5, :] += jnp.sum(dga, axis=0, keepdims=True)
        acc_ref[5:6, :] += jnp.sum(dgb, axis=0, keepdims=True)

    nt = t // tm
    col, row, vec, per_chunk = _mix_specs(tm, nt)
    b16 = jax.ShapeDtypeStruct((t, D), BF16)
    f32 = jax.ShapeDtypeStruct((t, D), F32)
    return pl.pallas_call(
        body, name="mix_backward", grid=(nt,),
        out_shape=(jax.ShapeDtypeStruct((t, NDEV * D), BF16), f32, f32, jax.ShapeDtypeStruct((nc, DK, D), F32), f32,
                   b16, b16, b16, b16, b16, jax.ShapeDtypeStruct((8, D), F32)),
        in_specs=[row, col(3), col(6), col(7), row, row, per_chunk(DK), row, VSPEC, VSPEC, VSPEC, vec, vec, vec, row,
                  per_chunk(1)],
        out_specs=(pl.BlockSpec((tm, 3 * D), lambda m: (nt - 1 - m, 0)), row, row, per_chunk(DK), row, row, row, row,
                   row, row, pl.BlockSpec((8, D), lambda m: (0, 0))),
        scratch_shapes=[pltpu.VMEM((DK, D), F32)],
        compiler_params=_params("arbitrary"),
    )(x, p, p, p, qe, intra, st_prev, act, wa, wb, wo, hg, g1, post, dx1, dec)


def conv_backward(p, yc, dact, dw, ln_g, ln_b, blocks, dp):
    t = p.shape[0]
    tm = _conv_tile(t)
    per = tm // HALO
    nt = t // tm
    n = len(blocks)

    def body(*refs):
        cv_ref, cg_ref, cvp_ref, cgp_ref, yc_ref, ycn_ref, da_ref, dan_ref, dw_ref, g_ref, b_ref = refs[:11]
        dp_ref, acc_ref, ddw_ref = refs[12 + n:15 + n]
        uext, dyext, ush, dysh, ddw8, du_sc = refs[15 + 2 * n:21 + 2 * n]
        start, finish = _exchange_ops(refs[11:11 + n], refs[15 + n:15 + 2 * n], *refs[21 + 2 * n:])
        m = pl.program_id(0)

        @pl.when(m == 0)
        def _():
            start()
            acc_ref[...] = jnp.zeros_like(acc_ref)
            ddw8[...] = jnp.zeros_like(ddw8)

        cv, cg = cv_ref[...], cg_ref[...]
        u, vjp_u = jax.vjp(f_glu, cv, cg)
        uext[0:HALO, :] = jnp.where(m == 0, 0.0, f_glu(cvp_ref[...], cgp_ref[...]))
        uext[HALO:HALO + tm, :] = u
        _shift_stack(uext, ush, tm + HALO)
        _, vjp_c = jax.vjp(f_conv_act, yc_ref[...], g_ref[...], b_ref[...])
        dyc, dg, db = vjp_c(da_ref[...])
        _, vjp_n = jax.vjp(f_conv_act, ycn_ref[...], g_ref[...], b_ref[...])
        dyn = vjp_n(dan_ref[...])[0]
        dyext[0:tm, :] = dyc
        dyext[tm:tm + HALO, :] = jnp.where(m == nt - 1, 0.0, dyn)
        _shift_stack(dyext, dysh, tm + HALO)
        rb = min(128, tm)
        for lt in range(D // LANE):
            ls = slice(lt * LANE, (lt + 1) * LANE)
            for r0 in range(0, tm, rb):
                du_l = jnp.zeros((rb, LANE), F32)
                for w in range(KW):
                    du_l = du_l + dw_ref[w:w + 1, ls] * _shifted(dysh, KW - 1 - w + r0, rb, ls)
                du_sc[r0:r0 + rb, ls] = du_l
                dyc_l = dyext[r0:r0 + rb, ls]
                for w in range(KW):
                    prod = dyc_l * _shifted(ush, HALO - KW + 1 + w + r0, rb, ls)
                    part = jnp.sum(prod.reshape(4, rb // 32, 8, LANE), axis=1)
                    ddw8[w, :, ls] += (part[0] + part[1]) + (part[2] + part[3])
        du = du_sc[...]

        @pl.when(m == nt - 1)
        def _():
            ddw_ref[...] = jnp.sum(ddw8[...], axis=1)

        dcv, dcg = vjp_u(du)
        dp_ref[:, 0:D] = dcv.astype(BF16)
        dp_ref[:, D:2 * D] = dcg.astype(BF16)
        acc_ref[0:1, :] += jnp.sum(dyc, axis=0, keepdims=True)
        acc_ref[1:2, :] += dg
        acc_ref[2:3, :] += db
        acc_ref[3:4, :] += jnp.sum(dcv, axis=0, keepdims=True)
        acc_ref[4:5, :] += jnp.sum(dcg, axis=0, keepdims=True)
        pl.when(m == nt - 1)(finish)

    vec = pl.BlockSpec((1, D), lambda m: (0, 0))
    row = pl.BlockSpec((tm, D), lambda m: (m, 0))
    prev = lambda j: pl.BlockSpec((HALO, D), lambda m: (jnp.maximum(m * per - 1, 0), j))
    nxt = pl.BlockSpec((HALO, D), lambda m: (jnp.minimum((m + 1) * per, t // HALO - 1), 0))
    res = pl.pallas_call(
        body, name="conv_backward", grid=(nt,),
        out_shape=(jax.ShapeDtypeStruct(dp.shape, dp.dtype), jax.ShapeDtypeStruct((8, D), F32),
                   jax.ShapeDtypeStruct((HALO, D), F32)) + _exchanged(blocks),
        in_specs=[pl.BlockSpec((tm, D), lambda m: (m, 4)), pl.BlockSpec((tm, D), lambda m: (m, 5)), prev(4), prev(5),
                  row, nxt, row, nxt, pl.BlockSpec((HALO, D), lambda m: (0, 0)), vec, vec] + [ANYSPEC] * (n + 1),
        out_specs=(pl.BlockSpec((tm, 2 * D), lambda m: (m, 3)), pl.BlockSpec((8, D), lambda m: (0, 0)),
                   pl.BlockSpec((HALO, D), lambda m: (0, 0))) + (ANYSPEC,) * n,
        scratch_shapes=[pltpu.VMEM((HALO + tm, D), F32), pltpu.VMEM((tm + HALO, D), F32),
                        pltpu.VMEM((8, tm + HALO, D), F32), pltpu.VMEM((8, tm + HALO, D), F32),
                        pltpu.VMEM((HALO, 8, D), F32), pltpu.VMEM((tm, D), F32)] + _comm_sems(n, False),
        input_output_aliases={11 + n: 0},
        compiler_params=_params("arbitrary"),
    )(p, p, p, p, yc, yc, dact, dact, dw, ln_g, ln_b, *blocks, dp)
    return res[:3] + (res[3:],)


def hgrn_backward(p, logits, do, dqe, gt, st_prev, blocks, dp):
    t = p.shape[0]
    nc = t // CH
    n = len(blocks)

    def body(*refs):
        q_ref, f_ref, v_ref, l_ref, do_ref, dqe_ref, gt_ref, st_ref = refs[:8]
        dp_ref, acc_ref = refs[9 + n:11 + n]
        start, finish = _exchange_ops(refs[8:8 + n], refs[11 + n:11 + 2 * n], *refs[11 + 2 * n:])

        @pl.when(pl.program_id(0) == 0)
        def _():
            start()
            acc_ref[...] = jnp.zeros_like(acc_ref)

        gt_v = gt_ref[...]
        ddec = jnp.sum(gt_v * st_ref[...], axis=0, keepdims=True)
        _, vjp = jax.vjp(f_hgrn_chunk, q_ref[...], f_ref[...], v_ref[...], l_ref[...])
        dq, df, dv, dl = vjp((do_ref[...], dqe_ref[...], gt_v, ddec))
        dp_ref[:, 0:D] = dq.astype(BF16)
        dp_ref[:, D:2 * D] = df.astype(BF16)
        dp_ref[:, 2 * D:3 * D] = dv.astype(BF16)
        acc_ref[0:2, :] += dl
        acc_ref[2:3, :] += jnp.sum(dq, axis=0, keepdims=True)
        acc_ref[3:4, :] += jnp.sum(df, axis=0, keepdims=True)
        acc_ref[4:5, :] += jnp.sum(dv, axis=0, keepdims=True)
        pl.when(pl.program_id(0) == nc - 1)(finish)

    col = lambda j: pl.BlockSpec((CH, D), lambda c: (c, j))
    row = pl.BlockSpec((CH, D), lambda c: (c, 0))
    stspec = pl.BlockSpec((None, DK, D), lambda c: (c, 0, 0))
    res = pl.pallas_call(
        body, name="hgrn_backward", grid=(nc,),
        out_shape=(jax.ShapeDtypeStruct(dp.shape, dp.dtype), jax.ShapeDtypeStruct((8, D), F32)) + _exchanged(blocks),
        in_specs=[col(0), col(1), col(2), pl.BlockSpec((2, D), lambda c: (0, 0)), row, row, stspec, stspec]
        + [ANYSPEC] * (n + 1),
        out_specs=(pl.BlockSpec((CH, 3 * D), lambda c: (c, 1)), pl.BlockSpec((8, D), lambda c: (0, 0)))
        + (ANYSPEC,) * n,
        scratch_shapes=_comm_sems(n, False),
        input_output_aliases={8 + n: 0},
        compiler_params=_params("arbitrary"),
    )(p, p, p, logits, do, dqe, gt, st_prev, *blocks, dp)
    return res[:2] + (res[2:],)


def in_proj_backward(dp, w_all, x, dx1, gain, sc, sh, blocks):
    t = x.shape[0]
    tm = min(256, t)
    nt = t // tm
    n = len(blocks)

    def body(*refs):
        dp_ref, w_ref, x_ref, dx1_ref, g_ref, sc_ref, sh_ref = refs[:7]
        gx_ref, acc_ref = refs[7 + n:9 + n]
        start = finish = lambda: None
        if n:
            start, finish = _exchange_ops(refs[7:7 + n], refs[9 + n:9 + 2 * n], *refs[9 + 2 * n:])

        @pl.when(pl.program_id(0) == 0)
        def _():
            start()
            acc_ref[...] = jnp.zeros_like(acc_ref)

        dh = jnp.zeros((tm, D), F32)
        for j in range(NDEV):
            dh = dh + _dot(dp_ref[:, j * D:(j + 1) * D], w_ref[DP_SPLIT[j]], _NT)
        _, vjp_h = jax.vjp(f_modulate, x_ref[...], g_ref[...], sc_ref[...], sh_ref[...])
        dx, dg, dsc, dsh = vjp_h(dh)
        gx_ref[...] = dx1_ref[...] + dx
        acc_ref[0:1, :] += dg
        acc_ref[1:2, :] += dsc
        acc_ref[2:3, :] += dsh
        pl.when(pl.program_id(0) == nt - 1)(finish)

    row = pl.BlockSpec((tm, D), lambda m: (m, 0))
    vec = pl.BlockSpec((1, D), lambda m: (0, 0))
    res = pl.pallas_call(
        body, name="in_proj_backward", grid=(nt,),
        out_shape=(jax.ShapeDtypeStruct((t, D), F32), jax.ShapeDtypeStruct((8, D), F32)) + _exchanged(blocks),
        in_specs=[pl.BlockSpec((tm, NDEV * D), lambda m: (m, 0)), VSPEC, row, row, vec, vec, vec] + [ANYSPEC] * n,
        out_specs=(row, pl.BlockSpec((8, D), lambda m: (0, 0))) + (ANYSPEC,) * n,
        scratch_shapes=_comm_sems(n, False) if n else [],
        compiler_params=_params("arbitrary"),
    )(dp, w_all, x, dx1, gain, sc, sh, *blocks)
    return res[0], res[1], res[2:]


def _dp_column(n):
    return jnp.where(n < 3, n + 3, jnp.where(n == 3, 0, jnp.where(n < 6, n + 2, n - 5)))


def weight_grad(a, b, nblk, ka, bn, a_blocked, name, b_col=lambda n: n):
    t = a.shape[0]
    tk = min(2048, t)
    nk = t // tk

    def body(a_ref, b_ref, f_ref, h_ref, acc):
        k = pl.program_id(1)

        @pl.when(k == 0)
        def _():
            acc[...] = jnp.zeros_like(acc)

        acc[...] += _dot(a_ref[...], b_ref[...], _TN)

        @pl.when(k == nk - 1)
        def _():
            f_ref[...] = acc[...]
            h_ref[...] = acc[...].astype(BF16)

    a_idx = (lambda n, k: (k, n)) if a_blocked else (lambda n, k: (k, 0))
    b_idx = (lambda n, k: (k, 0)) if a_blocked else (lambda n, k: (k, b_col(n)))
    out = pl.BlockSpec((None, ka, bn), lambda n, k: (n, 0, 0))
    return pl.pallas_call(
        body, name=name, grid=(nblk, nk),
        out_shape=(jax.ShapeDtypeStruct((nblk, ka, bn), F32), jax.ShapeDtypeStruct((nblk, ka, bn), BF16)),
        in_specs=[pl.BlockSpec((tk, ka), a_idx), pl.BlockSpec((tk, bn), b_idx)],
        out_specs=(out, out),
        scratch_shapes=[pltpu.VMEM((ka, bn), F32)],
        compiler_params=_params("parallel", "arbitrary"),
    )(a, b)


def ada_backward(call_t, dmod_cols, w, m, v):
    def body(c_ref, d_ref, w_ref, m_ref, v_ref, g_ref, dl_ref, nm_ref, nv_ref):
        ct, dm = c_ref[...], d_ref[...]
        g = ct[:, 0:1] * dm[0:1, :]
        for r in range(1, NDEV):
            g = g + ct[:, r:r + 1] * dm[r:r + 1, :]
        g_ref[...] = g
        dl_ref[...], nm_ref[...], nv_ref[...] = _adamw(w_ref[...], g, m_ref[...], v_ref[...])

    s = jax.ShapeDtypeStruct(w.shape, F32)
    return pl.pallas_call(
        body, name="ada_backward", out_shape=(s, s, s, s),
        in_specs=[VSPEC] * 5, out_specs=(VSPEC,) * 4, compiler_params=_params(),
    )(call_t, dmod_cols, w, m, v)


def adamw_small(total, ddw_mine, recipes, ws, ms, vs):
    n = len(ws)

    def body(*refs):
        tot, ddw = refs[0], refs[1]
        w_refs, m_refs, v_refs = refs[2:2 + n], refs[2 + n:2 + 2 * n], refs[2 + 2 * n:2 + 3 * n]
        outs = refs[2 + 3 * n:2 + 7 * n]
        loss_ref = refs[2 + 7 * n]
        for i, rec in enumerate(recipes):
            if rec == "dw":
                g = ddw[...]
            elif isinstance(rec, tuple):
                g = tot[rec[0]:rec[1], :]
            else:
                g = jnp.concatenate([tot[r:r + 1, :] for r in rec], axis=1) if len(rec) > 1 else tot[rec[0]:rec[0] + 1, :]
            dl, nm, nv = _adamw(w_refs[i][...], g, m_refs[i][...], v_refs[i][...])
            outs[4 * i][...] = g
            outs[4 * i + 1][...] = dl
            outs[4 * i + 2][...] = nm
            outs[4 * i + 3][...] = nv
        loss_ref[...] = tot[LOSS_ROW:LOSS_ROW + 1, 0:128]

    shapes = []
    for w in ws:
        shapes += [jax.ShapeDtypeStruct(w.shape, F32)] * 4
    res = pl.pallas_call(
        body, name="adamw_small", out_shape=tuple(shapes) + (jax.ShapeDtypeStruct((1, 128), F32),),
        in_specs=[VSPEC] * (2 + 3 * n), out_specs=(VSPEC,) * (4 * n + 1), compiler_params=_params(),
    )(total, ddw_mine, *ws, *ms, *vs)
    return [res[4 * i:4 * i + 4] for i in range(n)], res[4 * n]


def reduce_and_adamw(me, w, g_all, g_recv, m, v, name):
    r, c = w.shape
    br = min(128, r)

    def body(me_ref, w_ref, go_ref, gr_ref, m_ref, v_ref, g_ref, dl_ref, nm_ref, nv_ref):
        g = go_ref[...]
        for k in range(NDEV - 1):
            g = g + gr_ref[k].astype(F32)
        g_ref[...] = g
        dl_ref[...], nm_ref[...], nv_ref[...] = _adamw(w_ref[...], g, m_ref[...], v_ref[...])

    blk = pl.BlockSpec((br, c), lambda i, me_ref: (i, 0))
    s = jax.ShapeDtypeStruct(w.shape, F32)
    return pl.pallas_call(
        body, name=name, out_shape=(s, s, s, s),
        grid_spec=pltpu.PrefetchScalarGridSpec(
            num_scalar_prefetch=1, grid=(r // br,),
            in_specs=[blk, pl.BlockSpec((None, br, c), lambda i, me_ref: (me_ref[0], i, 0)),
                      pl.BlockSpec((NDEV - 1, br, c), lambda i, me_ref: (0, i, 0)), blk, blk],
            out_specs=(blk, blk, blk, blk)),
        compiler_params=_params("parallel"),
    )(me, w, g_all, g_recv, m, v)


def kernel(x, c, w_ada, b_ada, pre_norm_tm, post_norm_tm, pre_norm_cm, post_norm_cm, w_in, b_in, hg_lb_logits, hg_norm, conv_dw, conv_db, conv_ln_g, conv_ln_b, w_br_a, w_br_b, w_out, w_ff1, w_ff2, loss_target, m_w_ada, m_b_ada, m_pre_norm_tm, m_post_norm_tm, m_pre_norm_cm, m_post_norm_cm, m_w_in, m_b_in, m_hg_lb_logits, m_hg_norm, m_conv_dw, m_conv_db, m_conv_ln_g, m_conv_ln_b, m_w_br_a, m_w_br_b, m_w_out, m_w_ff1, m_w_ff2, v_w_ada, v_b_ada, v_pre_norm_tm, v_post_norm_tm, v_pre_norm_cm, v_post_norm_cm, v_w_in, v_b_in, v_hg_lb_logits, v_hg_norm, v_conv_dw, v_conv_db, v_conv_ln_g, v_conv_ln_b, v_w_br_a, v_w_br_b, v_w_out, v_w_ff1, v_w_ff2):
    t = x.shape[1]
    me = 4 * lax.axis_index("x") + 2 * lax.axis_index("y") + lax.axis_index("c")
    xs = x[0]
    tgt = loss_target[0]

    mod, call, dw_all = ada_forward(c, w_ada[0], b_ada, conv_dw[0])
    sh1, sc1, g1, sh2, sc2, g2 = [mod[:, i * D:(i + 1) * D] for i in range(6)]
    dw = jnp.pad(dw_all.transpose(1, 0, 2).reshape(KW, D), ((0, HALO - KW), (0, 0)))

    p, h, win_all = in_proj(xs, pre_norm_tm, sc1, sh1, w_in[0].astype(BF16), b_in)
    yc, act, (w1_all,) = conv_forward(p, dw, conv_db, conv_ln_g, conv_ln_b, [w_ff1[0].astype(BF16)])
    intra, qe, st_prev, dec, (w2_all, wa_all, wb_all, wo_all) = hgrn_local(
        p, hg_lb_logits, [w_ff2[0].astype(BF16), w_br_a[0].astype(BF16), w_br_b[0].astype(BF16),
                          w_out[0].astype(BF16)])
    wa, wb, wo = wa_all.reshape(D, D), wb_all.reshape(D, D), wo_all.reshape(D, D)
    x1 = mix_forward(xs, p, qe, intra, st_prev, act, wa, wb, wo, hg_norm, g1, post_norm_tm)

    dx1, h2, r, dz, dy2, acc_ffn = ffn_forward_backward(x1, tgt, w1_all, w2_all, pre_norm_cm, sc2, sh2, g2,
                                                        post_norm_cm)
    rows = D // NDEV
    g1_f, g1_h = weight_grad(h2, dz, NDEV, D, DFF // NDEV, False, "grad_w_ff1")
    g2_f, g2_h = weight_grad(r, dy2, 4, D, D, True, "grad_w_ff2")
    g2_f, g2_h = g2_f.reshape(NDEV, DFF // NDEV, D), g2_h.reshape(NDEV, DFF // NDEV, D)

    (dp, do, dqe, gt, dact, a16, mg16, dya, dyb, dy, acc_mix) = mix_backward(
        xs, p, qe, intra, st_prev, act, wa, wb, wo, hg_norm, g1, post_norm_tm, dx1, dec)
    dp, acc_hg, (r_ff1, r_ff2) = hgrn_backward(p, hg_lb_logits, do, dqe, gt, st_prev, [g1_h, g2_h], dp)
    ga_f, ga_h = weight_grad(a16, dya, 1, D, D, False, "grad_w_br_a")
    gb_f, gb_h = weight_grad(act, dyb, 1, D, D, False, "grad_w_br_b")
    go_f, go_h = weight_grad(mg16, dy, 1, D, D, False, "grad_w_out")
    dp, acc_conv, ddw, (r_a, r_b, r_o) = conv_backward(
        p, yc, dact, dw, conv_ln_g, conv_ln_b,
        [ga_h.reshape(NDEV, rows, D), gb_h.reshape(NDEV, rows, D), go_h.reshape(NDEV, rows, D)], dp)
    gin_f, gin_h = weight_grad(h, dp, NDEV, D, D, False, "grad_w_in", _dp_column)
    in_send, in_recv, gin_thru, in_landing, token = exchange_start(gin_h)
    grad_x, acc_in, _ = in_proj_backward(dp, win_all, xs, dx1, pre_norm_tm, sc1, sh1 + token[0:1, 0:1], [])

    own = [ga_f.reshape(NDEV, rows, D), gb_f.reshape(NDEV, rows, D), go_f.reshape(NDEV, rows, D), g1_f, g2_f]
    recv = [r_a, r_b, r_o, r_ff1, r_ff2]
    big = {}
    names = ["w_br_a", "w_br_b", "w_out", "w_ff1", "w_ff2"]
    ws = [w_br_a, w_br_b, w_out, w_ff1, w_ff2]
    ms = [m_w_br_a, m_w_br_b, m_w_out, m_w_ff1, m_w_ff2]
    vs = [v_w_br_a, v_w_br_b, v_w_out, v_w_ff1, v_w_ff2]
    me1 = me.astype(jnp.int32).reshape(1)
    for i, nm in enumerate(names):
        big[nm] = [o[None] for o in reduce_and_adamw(me1, ws[i][0], own[i], recv[i], ms[i][0], vs[i][0], "adamw_" + nm)]
    big, grad_x, acc_in = lax.optimization_barrier((big, grad_x, acc_in))
    r_in = exchange_wait(in_send, in_recv, gin_thru, in_landing, acc_in)
    r_in, acc_in = lax.optimization_barrier((r_in, acc_in))
    big["w_in"] = [o[None] for o in reduce_and_adamw(me1, w_in[0], gin_f, r_in, m_w_in[0], v_w_in[0], "adamw_w_in")]

    dmod_rows = [2, 1, 8, 19, 18, 20]
    total, kept = gather_and_sum_rows([acc_in, acc_mix, acc_ffn, acc_hg, acc_conv, ddw], dmod_rows)
    dmod_all = kept[:, 0:6, :].reshape(NDEV, 6 * D)
    wcols = w_ada.shape[2]
    gwa, dwa, nmwa, nvwa = ada_backward(call.T, lax.dynamic_slice_in_dim(dmod_all, me * wcols, wcols, axis=1),
                                        w_ada[0], m_w_ada[0], v_w_ada[0])
    ddw_mine = lax.dynamic_slice_in_dim(total[40:40 + KW], me * (D // NDEV), D // NDEV, axis=1)

    small_names = ["b_ada", "pre_norm_tm", "post_norm_tm", "pre_norm_cm", "post_norm_cm", "b_in", "hg_lb_logits",
                   "hg_norm", "conv_db", "conv_ln_g", "conv_ln_b", "conv_dw"]
    recipes = [dmod_rows, [0], [9], [16], [17], [26, 27, 28, 11, 35, 36, 12, 13], (24, 26), [10], [32], [33], [34],
               "dw"]
    small_w = [b_ada, pre_norm_tm, post_norm_tm, pre_norm_cm, post_norm_cm, b_in, hg_lb_logits, hg_norm, conv_db,
               conv_ln_g, conv_ln_b, conv_dw[0]]
    small_m = [m_b_ada, m_pre_norm_tm, m_post_norm_tm, m_pre_norm_cm, m_post_norm_cm, m_b_in, m_hg_lb_logits,
               m_hg_norm, m_conv_db, m_conv_ln_g, m_conv_ln_b, m_conv_dw[0]]
    small_v = [v_b_ada, v_pre_norm_tm, v_post_norm_tm, v_pre_norm_cm, v_post_norm_cm, v_b_in, v_hg_lb_logits,
               v_hg_norm, v_conv_db, v_conv_ln_g, v_conv_ln_b, v_conv_dw[0]]
    small_out, loss_row = adamw_small(total, ddw_mine, recipes, small_w, small_m, small_v)
    loss = loss_row[0, 0]
    sm = {nm: list(o) for nm, o in zip(small_names, small_out)}
    sm["conv_dw"] = [o[None] for o in sm["conv_dw"]]

    order = ["w_ada", "b_ada", "pre_norm_tm", "post_norm_tm", "pre_norm_cm", "post_norm_cm", "w_in", "b_in",
             "hg_lb_logits", "hg_norm", "conv_dw", "conv_db", "conv_ln_g", "conv_ln_b", "w_br_a", "w_br_b", "w_out",
             "w_ff1", "w_ff2"]
    res = dict(sm)
    res.update(big)
    res["w_ada"] = [gwa[None], dwa[None], nmwa[None], nvwa[None]]
    outs = [loss, grad_x[None]]
    for j in range(4):
        outs += [res[nm][j] for nm in order]
    return tuple(outs)
```

```python
import functools

import jax
import jax.numpy as jnp
from jax import lax
from jax.experimental import pallas as pl
from jax.experimental.pallas import tpu as pltpu

F32 = jnp.float32
BF16 = jnp.bfloat16
MESH = pl.DeviceIdType.MESH
AXES = ("x", "y", "c")

D = 1024
HEADS = 8
DK = 128
LANE = 128
CH = 128
LEVELS = 7
KW = 31
HALO = 32
DFF = 4096
NDEV = 8
EPS = 1e-6
DP_SPLIT = (3, 6, 7, 0, 1, 2, 4, 5)
LOSS_ROW = 21
ADAM_LR, ADAM_B1, ADAM_B2, ADAM_EPS, ADAM_WD, ADAM_STEP = 0.001, 0.9, 0.999, 1e-08, 0.01, 10
VMEM_LIMIT = 58 * 1024 * 1024

_NN = (((1,), (0,)), ((), ()))
_NT = (((1,), (1,)), ((), ()))
_TN = (((0,), (0,)), ((), ()))

VSPEC = pl.BlockSpec(memory_space=pltpu.VMEM)
ANYSPEC = pl.BlockSpec(memory_space=pl.ANY)


def _params(*sem):
    return pltpu.CompilerParams(dimension_semantics=sem or None, vmem_limit_bytes=VMEM_LIMIT)


def _dot(a, b, dims):
    return lax.dot_general(a, b, dims, preferred_element_type=F32)


@jax.custom_vjp
def mm_nn(a, b):
    return _dot(a.astype(BF16), b.astype(BF16), _NN)


def _mm_nn_fwd(a, b):
    ab, bb = a.astype(BF16), b.astype(BF16)
    return _dot(ab, bb, _NN), (ab, bb)


def _mm_nn_bwd(res, ct):
    ab, bb = res
    cb = ct.astype(BF16)
    return _dot(cb, bb, _NT), _dot(ab, cb, _TN)


mm_nn.defvjp(_mm_nn_fwd, _mm_nn_bwd)


@jax.custom_vjp
def mm_nt(a, b):
    return _dot(a.astype(BF16), b.astype(BF16), _NT)


def _mm_nt_fwd(a, b):
    ab, bb = a.astype(BF16), b.astype(BF16)
    return _dot(ab, bb, _NT), (ab, bb)


def _mm_nt_bwd(res, ct):
    ab, bb = res
    cb = ct.astype(BF16)
    return _dot(cb, bb, _NN), _dot(cb, ab, _TN)


mm_nt.defvjp(_mm_nt_fwd, _mm_nt_bwd)


@jax.custom_vjp
def mm_tn(a, b):
    return _dot(a.astype(BF16), b.astype(BF16), _TN)


def _mm_tn_fwd(a, b):
    ab, bb = a.astype(BF16), b.astype(BF16)
    return _dot(ab, bb, _TN), (ab, bb)


def _mm_tn_bwd(res, ct):
    ab, bb = res
    cb = ct.astype(BF16)
    return _dot(bb, cb, _NT), _dot(ab, cb, _NN)


mm_tn.defvjp(_mm_tn_fwd, _mm_tn_bwd)


def _rms(x):
    return x * lax.rsqrt(jnp.mean(x * x, axis=-1, keepdims=True) + EPS)


def _silu(x):
    return x * jax.nn.sigmoid(x)


def f_modulate(x, gain, sc, sh):
    return _rms(x) * gain * (1.0 + sc) + sh


def f_residual(x, y, gate, gain):
    return x + gate * (_rms(y) * gain)


def f_merge(ga, gb, ya, yb):
    return jax.nn.sigmoid(ga) * ya + jax.nn.sigmoid(gb) * yb


def f_head_out(o, og, hg):
    heads = [_rms(o[:, h * DK:(h + 1) * DK]) for h in range(HEADS)]
    return jnp.concatenate(heads, axis=1) * hg * _silu(og)


def f_conv_act(u, g, b):
    mu = jnp.mean(u, axis=-1, keepdims=True)
    var = jnp.mean(jnp.square(u - mu), axis=-1, keepdims=True)
    return _silu((u - mu) * lax.rsqrt(var + EPS) * g + b)


def f_glu(cv, cg):
    return cv * jax.nn.sigmoid(cg)


def _split2(x):
    hi = x.astype(BF16)
    return hi, (x - hi.astype(F32)).astype(BF16)


def _tri(transposed):
    i = lax.broadcasted_iota(jnp.int32, (CH, CH), 1 if transposed else 0)
    t = lax.broadcasted_iota(jnp.int32, (CH, CH), 0 if transposed else 1)
    return jnp.where(t <= i, 1.0, 0.0).astype(BF16)


def _blocks3(x, rows):
    return x.reshape(CH // rows, rows, x.shape[-1])


def _mid_broadcast(b, lev):
    h = 1 << (LEVELS - 1 - lev)
    if h >= 4:
        x3 = _blocks3(b, 2 * h)
        return jnp.broadcast_to(x3[:, h - 1:h, :], x3.shape).reshape(b.shape)
    x3 = _blocks3(b, 8)
    sub = lax.broadcasted_iota(jnp.int32, x3.shape, 1)
    out = None
    for first in range(0, 8, 2 * h):
        piece = jnp.broadcast_to(x3[:, first + h - 1:first + h, :], x3.shape)
        out = piece if out is None else jnp.where(sub >= first, piece, out)
    return out.reshape(b.shape)


def _mid_scatter(d, lev):
    h = 1 << (LEVELS - 1 - lev)
    if h >= 4:
        x3 = _blocks3(d, 2 * h)
        row = lax.broadcasted_iota(jnp.int32, x3.shape, 1)
        total = jnp.sum(x3, axis=1, keepdims=True)
        return jnp.where(row == h - 1, total, 0.0).reshape(d.shape)
    x3 = _blocks3(d, 8)
    sub = lax.broadcasted_iota(jnp.int32, x3.shape, 1)
    out = jnp.zeros_like(x3)
    for first in range(0, 8, 2 * h):
        inside = (sub >= first) & (sub < first + 2 * h)
        total = jnp.sum(jnp.where(inside, x3, 0.0), axis=1, keepdims=True)
        out = jnp.where(sub == first + h - 1, total, out)
    return out.reshape(d.shape)


@jax.custom_vjp
def decay_sums(g):
    hi, lo = _split2(g)
    tri = _tri(False)
    b = _dot(tri, hi, _NN) + _dot(tri, lo, _NN)
    return (b,) + tuple(b - _mid_broadcast(b, lev) for lev in range(LEVELS))


def _decay_sums_fwd(g):
    return decay_sums(g), None


def _decay_sums_bwd(_, cts):
    db = cts[0]
    for lev in range(LEVELS):
        db = db + cts[1 + lev] - _mid_scatter(cts[1 + lev], lev)
    hi, lo = _split2(db)
    tri = _tri(True)
    return (_dot(tri, hi, _NN) + _dot(tri, lo, _NN),)


decay_sums.defvjp(_decay_sums_fwd, _decay_sums_bwd)


def _score_masks():
    i = lax.broadcasted_iota(jnp.int32, (CH, CH), 0)
    j = lax.broadcasted_iota(jnp.int32, (CH, CH), 1)
    masks = [i == j]
    for lev in range(LEVELS):
        sh = LEVELS - 1 - lev
        same = (i >> (sh + 1)) == (j >> (sh + 1))
        masks.append(same & (((i >> sh) & 1) == 1) & (((j >> sh) & 1) == 0))
    return [jnp.where(m, 1.0, 0.0) for m in masks]


def f_hgrn_chunk(q_r, f_r, v, logits):
    l0, l1 = logits[0:1, :], logits[1:2, :]
    mx = lax.stop_gradient(jnp.maximum(l0, l1))
    e0, e1 = jnp.exp(l0 - mx), jnp.exp(l1 - mx)
    lb = e0 / (e0 + e1)
    q = _silu(q_r)
    f = lb + (1.0 - lb) * jax.nn.sigmoid(f_r)
    k = 1.0 - f
    sums = decay_sums(jnp.log(f))
    b = sums[0]
    btot = b[CH - 1:CH, :]
    qe = q * jnp.exp(b)
    ke = k * jnp.exp(btot - b)
    dec = jnp.exp(btot)
    qs, ks = [q], [k]
    row = lax.broadcasted_iota(jnp.int32, b.shape, 0)
    for lev in range(LEVELS):
        upper = ((row >> (LEVELS - 1 - lev)) & 1) == 1
        e = sums[1 + lev]
        qs.append(q * jnp.exp(jnp.where(upper, e, 0.0)))
        ks.append(k * jnp.exp(jnp.where(upper, 0.0, -e)))
    masks = _score_masks()
    intra, ut = [], []
    for h in range(HEADS):
        sl = slice(h * DK, (h + 1) * DK)
        sc = None
        for lev in range(LEVELS + 1):
            s = mm_nt(qs[lev][:, sl], ks[lev][:, sl]) * masks[lev]
            sc = s if sc is None else sc + s
        intra.append(mm_nn(sc, v[:, sl]))
        ut.append(mm_tn(v[:, sl], ke[:, sl]))
    return jnp.concatenate(intra, axis=1), qe, jnp.concatenate(ut, axis=1), dec


def _inter(qe_b, st_ref, rows):
    out = []
    for ci in range(rows // CH):
        st = st_ref[ci].astype(BF16)
        heads = [_dot(qe_b[ci * CH:(ci + 1) * CH, h * DK:(h + 1) * DK], st[:, h * DK:(h + 1) * DK], _NT)
                 for h in range(HEADS)]
        out.append(jnp.concatenate(heads, axis=1))
    return jnp.concatenate(out, axis=0)


def _shift_stack(src, dst, rows):
    dst[0, 0:rows, :] = src[0:rows, :]
    for b in range(1, 8):
        dst[b, 0:rows - 8, :] = src[pl.ds(b, rows - 8), :]


def _shifted(stack, offset, rows, lanes=slice(None)):
    return stack[offset % 8, pl.ds(8 * (offset // 8), rows), lanes]


def _adamw(w, g, m, v):
    m = ADAM_B1 * m + (1.0 - ADAM_B1) * g
    v = ADAM_B2 * v + (1.0 - ADAM_B2) * jnp.square(g)
    m_hat = m / (1.0 - ADAM_B1 ** ADAM_STEP)
    v_hat = v / (1.0 - ADAM_B2 ** ADAM_STEP)
    delta = -ADAM_LR * (m_hat / (jnp.sqrt(v_hat) + ADAM_EPS) + ADAM_WD * w)
    return delta, m, v


def _me():
    return lax.axis_index("x"), lax.axis_index("y"), lax.axis_index("c")


def _peer(k):
    x, y, c = _me()
    mask = k + 1
    px = (1 - x) if (mask >> 2) & 1 else x
    py = (1 - y) if (mask >> 1) & 1 else y
    pc = (1 - c) if mask & 1 else c
    return (px, py, pc), 4 * px + 2 * py + pc


def ada_forward(c, w_ada, b_ada, dw):
    wcols = w_ada.shape[1]

    def body(c_ref, w_ref, b_ref, dw_ref, mod_ref, call_ref, dwall_ref, part_ref, modp_ref, send_sems, recv_sems):
        x, y, cc = _me()
        me = 4 * x + 2 * y + cc
        call_ref[me] = _silu(c_ref[...])
        dwall_ref[me] = dw_ref[...]
        sends = []
        for k in range(NDEV - 1):
            dev, pidx = _peer(k)
            cp = pltpu.make_async_remote_copy(dwall_ref.at[me], dwall_ref.at[me], send_sems.at[2 * (NDEV - 1) + k],
                                              recv_sems.at[2 * (NDEV - 1) + k], device_id=dev, device_id_type=MESH)
            cp.start()
            sends.append(cp)
        for k in range(NDEV - 1):
            dev, _ = _peer(k)
            cp = pltpu.make_async_remote_copy(call_ref.at[me], call_ref.at[me], send_sems.at[k], recv_sems.at[k],
                                              device_id=dev, device_id_type=MESH)
            cp.start()
            sends.append(cp)
        for k in range(NDEV - 1):
            _, pidx = _peer(k)
            pltpu.make_async_remote_copy(call_ref.at[pidx], call_ref.at[pidx], send_sems.at[k], recv_sems.at[k],
                                         device_id=_peer(k)[0], device_id_type=MESH).wait_recv()
        call = jnp.concatenate([call_ref[r] for r in range(NDEV)], axis=0)
        part = _dot(call.astype(BF16), w_ref[...].astype(BF16), _NN)
        for r in range(NDEV):
            part_ref[r] = part[r:r + 1, :]
        modp_ref[me] = part_ref[me]
        for k in range(NDEV - 1):
            dev, pidx = _peer(k)
            cp = pltpu.make_async_remote_copy(part_ref.at[pidx], modp_ref.at[me], send_sems.at[NDEV - 1 + k],
                                              recv_sems.at[NDEV - 1 + k], device_id=dev, device_id_type=MESH)
            cp.start()
            sends.append(cp)
        for k in range(NDEV - 1):
            dev, pidx = _peer(k)
            pltpu.make_async_remote_copy(part_ref.at[pidx], modp_ref.at[pidx], send_sems.at[NDEV - 1 + k],
                                         recv_sems.at[NDEV - 1 + k], device_id=dev, device_id_type=MESH).wait_recv()
        for k in range(NDEV - 1):
            dev, pidx = _peer(k)
            pltpu.make_async_remote_copy(dwall_ref.at[pidx], dwall_ref.at[pidx], send_sems.at[2 * (NDEV - 1) + k],
                                         recv_sems.at[2 * (NDEV - 1) + k], device_id=dev,
                                         device_id_type=MESH).wait_recv()
        for cp in sends:
            cp.wait_send()
        mod_ref[...] = jnp.concatenate([modp_ref[r] for r in range(NDEV)], axis=1) + b_ref[...]

    mod, call, dw_all = pl.pallas_call(
        body, name="ada_forward",
        out_shape=(jax.ShapeDtypeStruct((1, NDEV * wcols), F32), jax.ShapeDtypeStruct((NDEV, 1, D), F32),
                   jax.ShapeDtypeStruct((NDEV,) + dw.shape, F32)),
        in_specs=[VSPEC] * 4, out_specs=(VSPEC,) * 3,
        scratch_shapes=[pltpu.VMEM((NDEV, 1, wcols), F32), pltpu.VMEM((NDEV, 1, wcols), F32),
                        pltpu.SemaphoreType.DMA((3 * (NDEV - 1),)), pltpu.SemaphoreType.DMA((3 * (NDEV - 1),))],
        compiler_params=_params(),
    )(c, w_ada, b_ada, dw)
    return mod, call.reshape(NDEV, D), dw_all


def _comm_sems(n, local):
    sems = [pltpu.SemaphoreType.DMA((7 * n,)), pltpu.SemaphoreType.DMA((7 * n,))]
    return sems + ([pltpu.SemaphoreType.DMA((n,))] if local else [])


def _gather2_ops(ins, outs, send_sems, recv_sems, local_sems):
    n = len(ins)
    x, y, c = _me()
    me, sibling = (x, y, c), (x, y, 1 - c)
    chips = [(1 - x, y), (x, 1 - y), (1 - x, 1 - y)]

    def slot(p):
        return 4 * p[0] + 2 * p[1] + p[2]

    def copy(a, k, block, to, src=None):
        return pltpu.make_async_remote_copy(
            src_ref=outs[a].at[slot(block)] if src is None else src, dst_ref=outs[a].at[slot(block)],
            send_sem=send_sems.at[a * 7 + k], recv_sem=recv_sems.at[a * 7 + k], device_id=to, device_id_type=MESH)

    def local(a):
        return pltpu.make_async_copy(ins[a], outs[a].at[slot(me)], local_sems.at[a])

    def first(a):
        return [copy(a, 0, me, sibling, src=ins[a])] + [copy(a, 1 + j, me, (*chip, c), src=ins[a])
                                                        for j, chip in enumerate(chips)]

    def passed(a):
        return [copy(a, 4 + j, (*chip, c), sibling) for j, chip in enumerate(chips)]

    def start():
        for a in range(n):
            local(a).start()
            for cp in first(a):
                cp.start()

    def forward():
        for j, chip in enumerate(chips):
            for a in range(n):
                copy(a, 1 + j, (*chip, c), me).wait_recv()
                passed(a)[j].start()

    def finish():
        for a in range(n):
            copy(a, 0, sibling, me).wait_recv()
            for j, chip in enumerate(chips):
                copy(a, 4 + j, (*chip, 1 - c), me).wait_recv()
        for a in range(n):
            for cp in first(a) + passed(a):
                cp.wait_send()
            local(a).wait()

    return start, forward, finish


def _exchange_ops(ins, outs, send_sems, recv_sems):
    n = len(ins)

    def copy(a, k):
        dev, pidx = _peer(k)
        return pltpu.make_async_remote_copy(ins[a].at[pidx], outs[a].at[k], send_sems.at[a * 7 + k],
                                            recv_sems.at[a * 7 + k], device_id=dev, device_id_type=MESH)

    def start():
        for k in range(NDEV - 1):
            for a in range(n):
                copy(a, k).start()

    def finish():
        for k in range(NDEV - 1):
            for a in range(n):
                copy(a, k).wait_recv()
        for k in range(NDEV - 1):
            for a in range(n):
                copy(a, k).wait_send()

    return start, finish


def _gathered(shards):
    return tuple(jax.ShapeDtypeStruct((NDEV,) + s.shape, s.dtype) for s in shards)


def _exchanged(blocks):
    return tuple(jax.ShapeDtypeStruct((NDEV - 1,) + b.shape[1:], b.dtype) for b in blocks)


def _owner_copy(src_ref, land_ref, send_sems, recv_sems, gi, n):
    x, y, c = _me()
    k = jnp.bitwise_xor(4 * x + 2 * y + c, n) - 1
    return pltpu.make_async_remote_copy(src_ref.at[n], land_ref.at[k], send_sems.at[gi], recv_sems.at[k],
                                        device_id=(n >> 2, (n >> 1) & 1, n & 1), device_id_type=MESH)


def exchange_start(blocks, owners, landing, name):
    if landing is None:
        landing = lax.empty((NDEV - 1,) + blocks.shape[1:], blocks.dtype)
    hbm = pl.BlockSpec(memory_space=pltpu.HBM)
    sem = pl.BlockSpec(memory_space=pltpu.SEMAPHORE)

    def body(src_ref, land_ref, send_sems, recv_sems, src_thru, land_thru, token):
        x, y, c = _me()
        for gi, n in enumerate(owners):
            @pl.when(4 * x + 2 * y + c != n)
            def _(gi=gi, n=n):
                _owner_copy(src_ref, land_ref, send_sems, recv_sems, gi, n).start()
        token[...] = jnp.zeros_like(token)

    return pl.pallas_call(
        body, name=name,
        out_shape=(pltpu.SemaphoreType.DMA((len(owners),)), pltpu.SemaphoreType.DMA((NDEV - 1,)),
                   pltpu.HBM(blocks.shape, blocks.dtype), pltpu.HBM(landing.shape, landing.dtype),
                   jax.ShapeDtypeStruct((8, 128), F32)),
        in_specs=(hbm, hbm), out_specs=(sem, sem, hbm, hbm, VSPEC), input_output_aliases={0: 2, 1: 3},
        compiler_params=pltpu.CompilerParams(has_side_effects=pltpu.SideEffectType.DATAFLOW_SIDE_EFFECTING),
    )(pltpu.with_memory_space_constraint(blocks, pltpu.HBM), pltpu.with_memory_space_constraint(landing, pltpu.HBM))


def exchange_wait(send_sems, recv_sems, src_thru, land_thru, after, owners, name):
    hbm = pl.BlockSpec(memory_space=pltpu.HBM)
    sem = pl.BlockSpec(memory_space=pltpu.SEMAPHORE)

    def body(src_ref, land_ref, send_sems, recv_sems, after_ref, src_dead, got_ref):
        x, y, c = _me()
        me = 4 * x + 2 * y + c
        for gi, n in enumerate(owners):
            @pl.when(me != n)
            def _(gi=gi, n=n):
                _owner_copy(src_ref, land_ref, send_sems, recv_sems, gi, n).wait_send()

            @pl.when(me == n)
            def _(n=n):
                for k in range(NDEV - 1):
                    pltpu.make_async_remote_copy(src_ref.at[n], land_ref.at[k], send_sems.at[0], recv_sems.at[k],
                                                 device_id=_peer(k)[0], device_id_type=MESH).wait_recv()

    return pl.pallas_call(
        body, name=name,
        out_shape=(pltpu.HBM(src_thru.shape, src_thru.dtype), pltpu.HBM(land_thru.shape, land_thru.dtype)),
        in_specs=(hbm, hbm, sem, sem, ANYSPEC), out_specs=(hbm, hbm), input_output_aliases={0: 0, 1: 1},
        compiler_params=pltpu.CompilerParams(has_side_effects=pltpu.SideEffectType.DATAFLOW_SIDE_EFFECTING),
    )(src_thru, land_thru, send_sems, recv_sems, after)[1]


def gather_and_sum_rows(parts, keep_rows):
    n = len(parts)
    offs = [sum(p.shape[0] for p in parts[:i]) for i in range(n)]
    rows = sum(p.shape[0] for p in parts)

    def body(*refs):
        sum_ref, keep_ref, buf_ref, send_sems, recv_sems = refs[n:]
        x, y, c = _me()
        me = 4 * x + 2 * y + c
        for i in range(n):
            buf_ref[me, offs[i]:offs[i] + parts[i].shape[0], :] = refs[i][...]

        def copy(k, block):
            return pltpu.make_async_remote_copy(buf_ref.at[block], buf_ref.at[block], send_sems.at[k],
                                                recv_sems.at[k], device_id=_peer(k)[0], device_id_type=MESH)

        for k in range(NDEV - 1):
            copy(k, me).start()
        for k in range(NDEV - 1):
            copy(k, _peer(k)[1]).wait_recv()
        for k in range(NDEV - 1):
            copy(k, me).wait_send()
        total = buf_ref[0]
        for r in range(1, NDEV):
            total = total + buf_ref[r]
        sum_ref[...] = total
        keep_ref[...] = jnp.zeros_like(keep_ref)
        for r in range(NDEV):
            for j, src in enumerate(keep_rows):
                keep_ref[r, j:j + 1, :] = buf_ref[r, src:src + 1, :]

    return pl.pallas_call(
        body, name="gather_and_sum_rows",
        out_shape=(jax.ShapeDtypeStruct((rows, D), F32), jax.ShapeDtypeStruct((NDEV, 8, D), F32)),
        in_specs=[VSPEC] * n, out_specs=(VSPEC, VSPEC),
        scratch_shapes=[pltpu.VMEM((NDEV, rows, D), F32), pltpu.SemaphoreType.DMA((NDEV - 1,)),
                        pltpu.SemaphoreType.DMA((NDEV - 1,))],
        compiler_params=_params(),
    )(*parts)


def _arrival_order():
    x, y, c = _me()
    near, far = [(1 - x, y), (x, 1 - y)], (1 - x, 1 - y)
    devs = ([(x, y, c), (x, y, 1 - c)] + [(*q, c) for q in near] + [(*q, 1 - c) for q in near]
            + [(*far, c), (*far, 1 - c)])
    return jnp.stack([4 * d[0] + 2 * d[1] + d[2] for d in devs]).astype(jnp.int32)


def in_proj(x, gain, sc, sh, w_shard, b_in):
    t = x.shape[0]
    tm = min(512, t // 2)
    nm = t // tm
    order = _arrival_order()

    def body(order_ref, x_ref, g_ref, sc_ref, sh_ref, b_ref, wsh_ref, p_ref, h_ref, wall_ref,
             wbuf, h_all, send_sems, recv_sems, local_sems):
        k, m = pl.program_id(0), pl.program_id(1)
        mx, my, mc = _me()
        me, sibling = (mx, my, mc), (mx, my, 1 - mc)
        chips = [(1 - mx, my), (mx, 1 - my), (1 - mx, 1 - my)]

        def slot(d):
            return 4 * d[0] + 2 * d[1] + d[2]

        def copy(sem, block, to, src=None):
            return pltpu.make_async_remote_copy(
                src_ref=wall_ref.at[slot(block)] if src is None else src, dst_ref=wall_ref.at[slot(block)],
                send_sem=send_sems.at[sem], recv_sem=recv_sems.at[sem], device_id=to, device_id_type=MESH)

        own = pltpu.make_async_copy(wsh_ref, wall_ref.at[slot(me)], local_sems.at[0])
        first = [copy(0, me, sibling, src=wsh_ref)] + [copy(1 + j, me, (*q, mc), src=wsh_ref)
                                                       for j, q in enumerate(chips)]
        passed = [copy(4 + j, (*q, mc), sibling) for j, q in enumerate(chips)]

        def load(kk):
            src = wsh_ref if kk == 0 else wall_ref.at[order_ref[kk]]
            return pltpu.make_async_copy(src, wbuf.at[kk % 2], local_sems.at[1 + kk % 2])

        def arrived(kk):
            if kk == 1:
                copy(0, sibling, me).wait_recv()
            elif kk in (2, 3, 6):
                j = {2: 0, 3: 1, 6: 2}[kk]
                copy(1 + j, (*chips[j], mc), me).wait_recv()
                passed[j].start()
                if kk == 3:
                    first[3].start()
            else:
                j = {4: 0, 5: 1, 7: 2}[kk]
                copy(4 + j, (*chips[j], 1 - mc), me).wait_recv()

        @pl.when((k == 0) & (m == 0))
        def _():
            own.start()
            for cp in first[:3]:
                cp.start()
            load(0).start()
            load(0).wait()

        for kk in range(1, NDEV):
            @pl.when((k == kk - 1) & (m == nm // 2))
            def _(kk=kk):
                arrived(kk)
                load(kk).start()

            @pl.when((k == kk) & (m == 0))
            def _(kk=kk):
                load(kk).wait()

        rows = pl.ds(pl.multiple_of(m * tm, tm), tm)

        @pl.when(k == 0)
        def _():
            h_all[rows, :] = f_modulate(x_ref[...], g_ref[...], sc_ref[...], sh_ref[...]).astype(BF16)

        p_ref[...] = _dot(h_all[rows, :], wbuf[k % 2], _NN) + b_ref[...]

        @pl.when((k == NDEV - 1) & (m == nm - 1))
        def _():
            for cp in first + passed:
                cp.wait_send()
            own.wait()
            out = pltpu.make_async_copy(h_all, h_ref, local_sems.at[0])
            out.start()
            out.wait()

    vec = pl.BlockSpec((1, D), lambda k, m, o: (0, 0))
    return pl.pallas_call(
        body, name="in_proj",
        out_shape=(jax.ShapeDtypeStruct((t, NDEV * D), F32), jax.ShapeDtypeStruct((t, D), BF16),
                   jax.ShapeDtypeStruct((NDEV, D, D), BF16)),
        grid_spec=pltpu.PrefetchScalarGridSpec(
            num_scalar_prefetch=1, grid=(NDEV, nm),
            in_specs=[pl.BlockSpec((tm, D), lambda k, m, o: (jnp.where(k == 0, m, 0), 0)), vec, vec, vec,
                      pl.BlockSpec((1, D), lambda k, m, o: (0, o[k])), ANYSPEC],
            out_specs=(pl.BlockSpec((tm, D), lambda k, m, o: (m, o[k])), ANYSPEC, ANYSPEC),
            scratch_shapes=[pltpu.VMEM((2, D, D), BF16), pltpu.VMEM((t, D), BF16), pltpu.SemaphoreType.DMA((7,)),
                            pltpu.SemaphoreType.DMA((7,)), pltpu.SemaphoreType.DMA((3,))]),
        compiler_params=_params("arbitrary", "arbitrary"),
    )(order, x, gain, sc, sh, b_in, w_shard)


def hgrn_local(p, logits, shards):
    t = p.shape[0]
    nc = t // CH
    n = len(shards)

    def body(*refs):
        q_ref, f_ref, v_ref, l_ref = refs[:4]
        intra_ref, qe_ref, st_ref, dec_ref = refs[4 + n:8 + n]
        state = refs[8 + 2 * n]
        start, forward, finish = _gather2_ops(refs[4:4 + n], refs[8 + n:8 + 2 * n], *refs[9 + 2 * n:])
        step = pl.program_id(0)

        @pl.when(step == 0)
        def _():
            start()
            state[...] = jnp.zeros_like(state)

        intra, qe, ut, dec = f_hgrn_chunk(q_ref[...], f_ref[...], v_ref[...], l_ref[...])
        intra_ref[...] = intra
        qe_ref[...] = qe.astype(BF16)
        dec_ref[...] = dec
        s = state[...]
        st_ref[...] = s
        state[...] = dec * s + ut
        pl.when(step == (3 * nc) // 4)(forward)
        pl.when(step == nc - 1)(finish)

    col = lambda j: pl.BlockSpec((CH, D), lambda c: (c, j))
    res = pl.pallas_call(
        body, name="hgrn_local", grid=(nc,),
        out_shape=(jax.ShapeDtypeStruct((t, D), F32), jax.ShapeDtypeStruct((t, D), BF16),
                   jax.ShapeDtypeStruct((nc, DK, D), F32), jax.ShapeDtypeStruct((nc, 1, D), F32)) + _gathered(shards),
        in_specs=[col(0), col(1), col(2), pl.BlockSpec((2, D), lambda c: (0, 0))] + [ANYSPEC] * n,
        out_specs=(pl.BlockSpec((CH, D), lambda c: (c, 0)), pl.BlockSpec((CH, D), lambda c: (c, 0)),
                   pl.BlockSpec((None, DK, D), lambda c: (c, 0, 0)), pl.BlockSpec((None, 1, D), lambda c: (c, 0, 0)))
        + (ANYSPEC,) * n,
        scratch_shapes=[pltpu.VMEM((DK, D), F32)] + _comm_sems(n, True),
        compiler_params=_params("arbitrary"),
    )(p, p, p, logits, *shards)
    return res[:4] + (res[4:],)


def _conv_tile(t):
    return min(256, t)


def conv_forward(p, dw, db, ln_g, ln_b, shards):
    t = p.shape[0]
    tm = _conv_tile(t)
    per = tm // HALO
    n = len(shards)
    nt = t // tm

    def body(*refs):
        cv_ref, cg_ref, cvp_ref, cgp_ref, dw_ref, db_ref, g_ref, b_ref = refs[:8]
        yc_ref, act_ref = refs[8 + n:10 + n]
        uext, ush = refs[10 + 2 * n:12 + 2 * n]
        start, forward, finish = _gather2_ops(refs[8:8 + n], refs[10 + n:10 + 2 * n], *refs[12 + 2 * n:])
        step = pl.program_id(0)
        pl.when(step == 0)(start)
        pl.when(step == nt - 1)(forward)
        uext[0:HALO, :] = jnp.where(step == 0, 0.0, f_glu(cvp_ref[...], cgp_ref[...]))
        uext[HALO:HALO + tm, :] = f_glu(cv_ref[...], cg_ref[...])
        _shift_stack(uext, ush, tm + HALO)
        acc = jnp.zeros((tm, D), F32) + db_ref[...]
        for w in range(KW):
            acc = acc + dw_ref[w:w + 1, :] * _shifted(ush, HALO - KW + 1 + w, tm)
        yc_ref[...] = acc
        act_ref[...] = f_conv_act(acc, g_ref[...], b_ref[...]).astype(BF16)
        pl.when(step == nt - 1)(finish)

    vec = pl.BlockSpec((1, D), lambda m: (0, 0))
    prev = lambda j: pl.BlockSpec((HALO, D), lambda m: (jnp.maximum(m * per - 1, 0), j))
    res = pl.pallas_call(
        body, name="conv_forward", grid=(nt,),
        out_shape=(jax.ShapeDtypeStruct((t, D), F32), jax.ShapeDtypeStruct((t, D), BF16)) + _gathered(shards),
        in_specs=[pl.BlockSpec((tm, D), lambda m: (m, 4)), pl.BlockSpec((tm, D), lambda m: (m, 5)), prev(4), prev(5),
                  pl.BlockSpec((HALO, D), lambda m: (0, 0)), vec, vec, vec] + [ANYSPEC] * n,
        out_specs=(pl.BlockSpec((tm, D), lambda m: (m, 0)), pl.BlockSpec((tm, D), lambda m: (m, 0))) + (ANYSPEC,) * n,
        scratch_shapes=[pltpu.VMEM((HALO + tm, D), F32), pltpu.VMEM((8, tm + HALO, D), F32)] + _comm_sems(n, True),
        compiler_params=_params("arbitrary"),
    )(p, p, p, p, dw, db, ln_g, ln_b, *shards)
    return res[0], res[1], res[2:]


def _mix_tile(t):
    return min(256, t)


def _mix_forward_tile(x_ref, og_ref, ga_ref, gb_ref, qe_ref, intra_ref, st_ref, act_ref, wa_ref, wb_ref, wo_ref,
                      hg_ref, rows):
    o = _inter(qe_ref[...], st_ref, rows) + intra_ref[...]
    a = f_head_out(o, og_ref[...], hg_ref[...])
    ya = _dot(a.astype(BF16), wa_ref[...], _NN)
    yb = _dot(act_ref[...], wb_ref[...], _NN)
    merged = f_merge(ga_ref[...], gb_ref[...], ya, yb)
    y = _dot(merged.astype(BF16), wo_ref[...], _NN)
    return o, a, ya, yb, merged, y


def _mix_specs(tm, tiles=None):
    at = (lambda m: m) if tiles is None else (lambda m: tiles - 1 - m)
    col = lambda j: pl.BlockSpec((tm, D), lambda m: (at(m), j))
    row = pl.BlockSpec((tm, D), lambda m: (at(m), 0))
    per_chunk = lambda rows: pl.BlockSpec((tm // CH, rows, D), lambda m: (at(m), 0, 0))
    return col, row, pl.BlockSpec((1, D), lambda m: (0, 0)), per_chunk


def mix_forward(x, p, qe, intra, st_prev, act, wa, wb, wo, hg, g1, post):
    t = x.shape[0]
    tm = _mix_tile(t)

    def body(x_ref, og_ref, ga_ref, gb_ref, qe_ref, intra_ref, st_ref, act_ref, wa_ref, wb_ref, wo_ref, hg_ref,
             g1_ref, post_ref, x1_ref):
        y = _mix_forward_tile(x_ref, og_ref, ga_ref, gb_ref, qe_ref, intra_ref, st_ref, act_ref, wa_ref, wb_ref,
                              wo_ref, hg_ref, tm)[-1]
        x1_ref[...] = f_residual(x_ref[...], y, g1_ref[...], post_ref[...])

    col, row, vec, per_chunk = _mix_specs(tm)
    return pl.pallas_call(
        body, name="mix_forward", grid=(t // tm,),
        out_shape=jax.ShapeDtypeStruct((t, D), F32),
        in_specs=[row, col(3), col(6), col(7), row, row, per_chunk(DK), row, VSPEC, VSPEC, VSPEC, vec, vec, vec],
        out_specs=row,
        compiler_params=_params("parallel"),
    )(x, p, p, p, qe, intra, st_prev, act, wa, wb, wo, hg, g1, post)


def ffn_forward_backward(x1, target, w1, w2, pre, sc, sh, g2, post):
    t = x1.shape[0]
    tm = min(256, t)
    nb = w1.shape[0]
    fb = w1.shape[2]

    def body(x_ref, tg_ref, w1_ref, w2_ref, pre_ref, sc_ref, sh_ref, g2_ref, post_ref,
             dx_ref, h2_ref, r_ref, dz_ref, dy2_ref, acc_ref, z_sc):
        @pl.when(pl.program_id(0) == 0)
        def _():
            acc_ref[...] = jnp.zeros_like(acc_ref)

        x1v = x_ref[...]
        h2, vjp_h = jax.vjp(f_modulate, x1v, pre_ref[...], sc_ref[...], sh_ref[...])
        h2b = h2.astype(BF16)
        h2_ref[...] = h2b
        y2 = jnp.zeros((tm, D), F32)
        for n in range(nb):
            z = _dot(h2b, w1_ref[n], _NN)
            z_sc[:, n * fb:(n + 1) * fb] = z
            r = jnp.square(jnp.maximum(z, 0.0)).astype(BF16)
            r_ref[:, n * fb:(n + 1) * fb] = r
            y2 = y2 + _dot(r, w2_ref[n], _NN)
        out, vjp_r = jax.vjp(f_residual, x1v, y2, g2_ref[...], post_ref[...])
        err = out - tg_ref[...]
        tok = jnp.mean(jnp.square(err), axis=-1, keepdims=True)
        acc_ref[5:6, :] += 0.5 * jnp.sum(tok, axis=0, keepdims=True)
        dx_a, dy2, dg2, dpost = vjp_r(err * (1.0 / D))
        dy2b = dy2.astype(BF16)
        dy2_ref[...] = dy2b
        dh2 = jnp.zeros((tm, D), F32)
        for n in range(nb):
            dr = _dot(dy2b, w2_ref[n], _NT)
            dz = (dr * (2.0 * jnp.maximum(z_sc[:, n * fb:(n + 1) * fb], 0.0))).astype(BF16)
            dz_ref[:, n * fb:(n + 1) * fb] = dz
            dh2 = dh2 + _dot(dz, w1_ref[n], _NT)
        dx_b, dpre, dsc, dsh = vjp_h(dh2)
        dx_ref[...] = dx_a + dx_b
        acc_ref[0:1, :] += dpre
        acc_ref[1:2, :] += dpost
        acc_ref[2:3, :] += dsc
        acc_ref[3:4, :] += dsh
        acc_ref[4:5, :] += dg2

    row = pl.BlockSpec((tm, D), lambda m: (m, 0))
    wide = pl.BlockSpec((tm, DFF), lambda m: (m, 0))
    vec = pl.BlockSpec((1, D), lambda m: (0, 0))
    return pl.pallas_call(
        body, name="ffn_forward_backward", grid=(t // tm,),
        out_shape=(jax.ShapeDtypeStruct((t, D), F32), jax.ShapeDtypeStruct((t, D), BF16),
                   jax.ShapeDtypeStruct((t, DFF), BF16), jax.ShapeDtypeStruct((t, DFF), BF16),
                   jax.ShapeDtypeStruct((t, D), BF16), jax.ShapeDtypeStruct((8, D), F32)),
        in_specs=[row, row, VSPEC, VSPEC, vec, vec, vec, vec, vec],
        out_specs=(row, row, wide, wide, row, pl.BlockSpec((8, D), lambda m: (0, 0))),
        scratch_shapes=[pltpu.VMEM((tm, DFF), F32)],
        compiler_params=_params("arbitrary"),
    )(x1, target, w1, w2, pre, sc, sh, g2, post)


def mix_backward(x, p, qe, intra, st_prev, act, wa, wb, wo, hg, g1, post, dx1, dec):
    t = x.shape[0]
    tm = _mix_tile(t)
    nc = t // CH

    def body(x_ref, og_ref, ga_ref, gb_ref, qe_ref, intra_ref, st_ref, act_ref, wa_ref, wb_ref, wo_ref, hg_ref,
             g1_ref, post_ref, dx1_ref, dec_ref,
             dp_ref, do_ref, dqe_ref, gt_ref, dact_ref, a_ref, mg_ref, dya_ref, dyb_ref, dy_ref, acc_ref, gstate):
        @pl.when(pl.program_id(0) == 0)
        def _():
            acc_ref[...] = jnp.zeros_like(acc_ref)
            gstate[...] = jnp.zeros_like(gstate)

        o, a, ya, yb, merged, y = _mix_forward_tile(x_ref, og_ref, ga_ref, gb_ref, qe_ref, intra_ref, st_ref, act_ref,
                                                    wa_ref, wb_ref, wo_ref, hg_ref, tm)
        a_ref[...] = a.astype(BF16)
        mg_ref[...] = merged.astype(BF16)
        _, vjp_r = jax.vjp(f_residual, x_ref[...], y, g1_ref[...], post_ref[...])
        _, dy, dg1, dpost = vjp_r(dx1_ref[...])
        dyb16 = dy.astype(BF16)
        dy_ref[...] = dyb16
        dmerged = _dot(dyb16, wo_ref[...], _NT)
        _, vjp_m = jax.vjp(f_merge, ga_ref[...], gb_ref[...], ya, yb)
        dga, dgb, dya, dyb = vjp_m(dmerged)
        dp_ref[:, D:2 * D] = dga.astype(BF16)
        dp_ref[:, 2 * D:3 * D] = dgb.astype(BF16)
        dya16, dyb16b = dya.astype(BF16), dyb.astype(BF16)
        dya_ref[...] = dya16
        dyb_ref[...] = dyb16b
        da = _dot(dya16, wa_ref[...], _NT)
        dact_ref[...] = _dot(dyb16b, wb_ref[...], _NT)
        _, vjp_a = jax.vjp(f_head_out, o, og_ref[...], hg_ref[...])
        do, dog, dhg = vjp_a(da)
        dp_ref[:, 0:D] = dog.astype(BF16)
        do_ref[...] = do
        do16 = do.astype(BF16)
        qe16 = qe_ref[...]
        for ci in reversed(range(tm // CH)):
            st = st_ref[ci].astype(BF16)
            rows = slice(ci * CH, (ci + 1) * CH)
            dqe, vt = [], []
            for h in range(HEADS):
                sl = slice(h * DK, (h + 1) * DK)
                dqe.append(_dot(do16[rows, sl], st[:, sl], _NN))
                vt.append(_dot(do16[rows, sl], qe16[rows, sl], _TN))
            dqe_ref[rows, :] = jnp.concatenate(dqe, axis=1)
            g = gstate[...]
            gt_ref[ci] = g
            gstate[...] = dec_ref[ci] * g + jnp.concatenate(vt, axis=1)
        acc_ref[0:1, :] += dg1
        acc_ref[1:2, :] += dpost
        acc_ref[2:3, :] += dhg
        acc_ref[3:4, :] += jnp.sum(dog, axis=0, keepdims=True)
        acc_ref[4:5, :] += jnp.sum(dga, axis=0, keepdims=True)
        acc_ref[5:6, :] += jnp.sum(dgb, axis=0, keepdims=True)

    nt = t // tm
    col, row, vec, per_chunk = _mix_specs(tm, nt)
    b16 = jax.ShapeDtypeStruct((t, D), BF16)
    f32 = jax.ShapeDtypeStruct((t, D), F32)
    return pl.pallas_call(
        body, name="mix_backward", grid=(nt,),
        out_shape=(jax.ShapeDtypeStruct((t, NDEV * D), BF16), f32, f32, jax.ShapeDtypeStruct((nc, DK, D), F32), f32,
                   b16, b16, b16, b16, b16, jax.ShapeDtypeStruct((8, D), F32)),
        in_specs=[row, col(3), col(6), col(7), row, row, per_chunk(DK), row, VSPEC, VSPEC, VSPEC, vec, vec, vec, row,
                  per_chunk(1)],
        out_specs=(pl.BlockSpec((tm, 3 * D), lambda m: (nt - 1 - m, 0)), row, row, per_chunk(DK), row, row, row, row,
                   row, row, pl.BlockSpec((8, D), lambda m: (0, 0))),
        scratch_shapes=[pltpu.VMEM((DK, D), F32)],
        compiler_params=_params("arbitrary"),
    )(x, p, p, p, qe, intra, st_prev, act, wa, wb, wo, hg, g1, post, dx1, dec)


def conv_backward(p, yc, dact, dw, ln_g, ln_b, blocks, dp):
    t = p.shape[0]
    tm = _conv_tile(t)
    per = tm // HALO
    nt = t // tm
    n = len(blocks)

    def body(*refs):
        cv_ref, cg_ref, cvp_ref, cgp_ref, yc_ref, ycn_ref, da_ref, dan_ref, dw_ref, g_ref, b_ref = refs[:11]
        dp_ref, acc_ref, ddw_ref = refs[12 + n:15 + n]
        uext, dyext, ush, dysh, ddw8, du_sc = refs[15 + 2 * n:21 + 2 * n]
        start, finish = _exchange_ops(refs[11:11 + n], refs[15 + n:15 + 2 * n], *refs[21 + 2 * n:])
        m = pl.program_id(0)

        @pl.when(m == 0)
        def _():
            start()
            acc_ref[...] = jnp.zeros_like(acc_ref)
            ddw8[...] = jnp.zeros_like(ddw8)

        cv, cg = cv_ref[...], cg_ref[...]
        u, vjp_u = jax.vjp(f_glu, cv, cg)
        uext[0:HALO, :] = jnp.where(m == 0, 0.0, f_glu(cvp_ref[...], cgp_ref[...]))
        uext[HALO:HALO + tm, :] = u
        _shift_stack(uext, ush, tm + HALO)
        _, vjp_c = jax.vjp(f_conv_act, yc_ref[...], g_ref[...], b_ref[...])
        dyc, dg, db = vjp_c(da_ref[...])
        _, vjp_n = jax.vjp(f_conv_act, ycn_ref[...], g_ref[...], b_ref[...])
        dyn = vjp_n(dan_ref[...])[0]
        dyext[0:tm, :] = dyc
        dyext[tm:tm + HALO, :] = jnp.where(m == nt - 1, 0.0, dyn)
        _shift_stack(dyext, dysh, tm + HALO)
        rb = min(128, tm)
        for lt in range(D // LANE):
            ls = slice(lt * LANE, (lt + 1) * LANE)
            for r0 in range(0, tm, rb):
                du_l = jnp.zeros((rb, LANE), F32)
                for w in range(KW):
                    du_l = du_l + dw_ref[w:w + 1, ls] * _shifted(dysh, KW - 1 - w + r0, rb, ls)
                du_sc[r0:r0 + rb, ls] = du_l
                dyc_l = dyext[r0:r0 + rb, ls]
                for w in range(KW):
                    prod = dyc_l * _shifted(ush, HALO - KW + 1 + w + r0, rb, ls)
                    part = jnp.sum(prod.reshape(4, rb // 32, 8, LANE), axis=1)
                    ddw8[w, :, ls] += (part[0] + part[1]) + (part[2] + part[3])
        du = du_sc[...]

        @pl.when(m == nt - 1)
        def _():
            ddw_ref[...] = jnp.sum(ddw8[...], axis=1)

        dcv, dcg = vjp_u(du)
        dp_ref[:, 0:D] = dcv.astype(BF16)
        dp_ref[:, D:2 * D] = dcg.astype(BF16)
        acc_ref[0:1, :] += jnp.sum(dyc, axis=0, keepdims=True)
        acc_ref[1:2, :] += dg
        acc_ref[2:3, :] += db
        acc_ref[3:4, :] += jnp.sum(dcv, axis=0, keepdims=True)
        acc_ref[4:5, :] += jnp.sum(dcg, axis=0, keepdims=True)
        pl.when(m == nt - 1)(finish)

    vec = pl.BlockSpec((1, D), lambda m: (0, 0))
    row = pl.BlockSpec((tm, D), lambda m: (m, 0))
    prev = lambda j: pl.BlockSpec((HALO, D), lambda m: (jnp.maximum(m * per - 1, 0), j))
    nxt = pl.BlockSpec((HALO, D), lambda m: (jnp.minimum((m + 1) * per, t // HALO - 1), 0))
    res = pl.pallas_call(
        body, name="conv_backward", grid=(nt,),
        out_shape=(jax.ShapeDtypeStruct(dp.shape, dp.dtype), jax.ShapeDtypeStruct((8, D), F32),
                   jax.ShapeDtypeStruct((HALO, D), F32)) + _exchanged(blocks),
        in_specs=[pl.BlockSpec((tm, D), lambda m: (m, 4)), pl.BlockSpec((tm, D), lambda m: (m, 5)), prev(4), prev(5),
                  row, nxt, row, nxt, pl.BlockSpec((HALO, D), lambda m: (0, 0)), vec, vec] + [ANYSPEC] * (n + 1),
        out_specs=(pl.BlockSpec((tm, 2 * D), lambda m: (m, 3)), pl.BlockSpec((8, D), lambda m: (0, 0)),
                   pl.BlockSpec((HALO, D), lambda m: (0, 0))) + (ANYSPEC,) * n,
        scratch_shapes=[pltpu.VMEM((HALO + tm, D), F32), pltpu.VMEM((tm + HALO, D), F32),
                        pltpu.VMEM((8, tm + HALO, D), F32), pltpu.VMEM((8, tm + HALO, D), F32),
                        pltpu.VMEM((HALO, 8, D), F32), pltpu.VMEM((tm, D), F32)] + _comm_sems(n, False),
        input_output_aliases={11 + n: 0},
        compiler_params=_params("arbitrary"),
    )(p, p, p, p, yc, yc, dact, dact, dw, ln_g, ln_b, *blocks, dp)
    return res[:3] + (res[3:],)


def hgrn_backward(p, logits, do, dqe, gt, st_prev, blocks, dp):
    t = p.shape[0]
    nc = t // CH
    n = len(blocks)

    def body(*refs):
        q_ref, f_ref, v_ref, l_ref, do_ref, dqe_ref, gt_ref, st_ref = refs[:8]
        dp_ref, acc_ref = refs[9 + n:11 + n]
        start, finish = _exchange_ops(refs[8:8 + n], refs[11 + n:11 + 2 * n], *refs[11 + 2 * n:])

        @pl.when(pl.program_id(0) == 0)
        def _():
            start()
            acc_ref[...] = jnp.zeros_like(acc_ref)

        gt_v = gt_ref[...]
        ddec = jnp.sum(gt_v * st_ref[...], axis=0, keepdims=True)
        _, vjp = jax.vjp(f_hgrn_chunk, q_ref[...], f_ref[...], v_ref[...], l_ref[...])
        dq, df, dv, dl = vjp((do_ref[...], dqe_ref[...], gt_v, ddec))
        dp_ref[:, 0:D] = dq.astype(BF16)
        dp_ref[:, D:2 * D] = df.astype(BF16)
        dp_ref[:, 2 * D:3 * D] = dv.astype(BF16)
        acc_ref[0:2, :] += dl
        acc_ref[2:3, :] += jnp.sum(dq, axis=0, keepdims=True)
        acc_ref[3:4, :] += jnp.sum(df, axis=0, keepdims=True)
        acc_ref[4:5, :] += jnp.sum(dv, axis=0, keepdims=True)
        pl.when(pl.program_id(0) == nc - 1)(finish)

    col = lambda j: pl.BlockSpec((CH, D), lambda c: (c, j))
    row = pl.BlockSpec((CH, D), lambda c: (c, 0))
    stspec = pl.BlockSpec((None, DK, D), lambda c: (c, 0, 0))
    res = pl.pallas_call(
        body, name="hgrn_backward", grid=(nc,),
        out_shape=(jax.ShapeDtypeStruct(dp.shape, dp.dtype), jax.ShapeDtypeStruct((8, D), F32)) + _exchanged(blocks),
        in_specs=[col(0), col(1), col(2), pl.BlockSpec((2, D), lambda c: (0, 0)), row, row, stspec, stspec]
        + [ANYSPEC] * (n + 1),
        out_specs=(pl.BlockSpec((CH, 3 * D), lambda c: (c, 1)), pl.BlockSpec((8, D), lambda c: (0, 0)))
        + (ANYSPEC,) * n,
        scratch_shapes=_comm_sems(n, False),
        input_output_aliases={8 + n: 0},
        compiler_params=_params("arbitrary"),
    )(p, p, p, logits, do, dqe, gt, st_prev, *blocks, dp)
    return res[:2] + (res[2:],)


def in_proj_backward(dp, w_all, x, dx1, gain, sc, sh, blocks):
    t = x.shape[0]
    tm = min(256, t)
    nt = t // tm
    n = len(blocks)

    def body(*refs):
        dp_ref, w_ref, x_ref, dx1_ref, g_ref, sc_ref, sh_ref = refs[:7]
        gx_ref, acc_ref = refs[7 + n:9 + n]
        start = finish = lambda: None
        if n:
            start, finish = _exchange_ops(refs[7:7 + n], refs[9 + n:9 + 2 * n], *refs[9 + 2 * n:])

        @pl.when(pl.program_id(0) == 0)
        def _():
            start()
            acc_ref[...] = jnp.zeros_like(acc_ref)

        dh = jnp.zeros((tm, D), F32)
        for j in range(NDEV):
            dh = dh + _dot(dp_ref[:, j * D:(j + 1) * D], w_ref[DP_SPLIT[j]], _NT)
        _, vjp_h = jax.vjp(f_modulate, x_ref[...], g_ref[...], sc_ref[...], sh_ref[...])
        dx, dg, dsc, dsh = vjp_h(dh)
        gx_ref[...] = dx1_ref[...] + dx
        acc_ref[0:1, :] += dg
        acc_ref[1:2, :] += dsc
        acc_ref[2:3, :] += dsh
        pl.when(pl.program_id(0) == nt - 1)(finish)

    row = pl.BlockSpec((tm, D), lambda m: (m, 0))
    vec = pl.BlockSpec((1, D), lambda m: (0, 0))
    res = pl.pallas_call(
        body, name="in_proj_backward", grid=(nt,),
        out_shape=(jax.ShapeDtypeStruct((t, D), F32), jax.ShapeDtypeStruct((8, D), F32)) + _exchanged(blocks),
        in_specs=[pl.BlockSpec((tm, NDEV * D), lambda m: (m, 0)), VSPEC, row, row, vec, vec, vec] + [ANYSPEC] * n,
        out_specs=(row, pl.BlockSpec((8, D), lambda m: (0, 0))) + (ANYSPEC,) * n,
        scratch_shapes=_comm_sems(n, False) if n else [],
        compiler_params=_params("arbitrary"),
    )(dp, w_all, x, dx1, gain, sc, sh, *blocks)
    return res[0], res[1], res[2:]


def weight_grad(a, b, nblk, ka, bn, a_blocked, name, b_col=lambda n: n, out_idx=lambda n: n, nout=None, fill=None):
    t = a.shape[0]
    tk = min(2048, t)
    nk = t // tk
    nout = nout or nblk
    extra = [] if fill is None else [fill]

    def body(*refs):
        a_ref, b_ref = refs[:2]
        f_ref, h_ref, acc = refs[2 + len(extra):]
        k = pl.program_id(1)

        @pl.when(k == 0)
        def _():
            acc[...] = jnp.zeros_like(acc)

        acc[...] += _dot(a_ref[...], b_ref[...], _TN)

        @pl.when(k == nk - 1)
        def _():
            f_ref[...] = acc[...]
            h_ref[...] = acc[...].astype(BF16)

    a_idx = (lambda n, k: (k, n)) if a_blocked else (lambda n, k: (k, 0))
    b_idx = (lambda n, k: (k, 0)) if a_blocked else (lambda n, k: (k, b_col(n)))
    out = pl.BlockSpec((None, ka, bn), lambda n, k: (out_idx(n), 0, 0))
    return pl.pallas_call(
        body, name=name, grid=(nblk, nk),
        out_shape=(jax.ShapeDtypeStruct((nout, ka, bn), F32), jax.ShapeDtypeStruct((nout, ka, bn), BF16)),
        in_specs=[pl.BlockSpec((tk, ka), a_idx), pl.BlockSpec((tk, bn), b_idx)] + [ANYSPEC] * len(extra),
        out_specs=(out, out),
        scratch_shapes=[pltpu.VMEM((ka, bn), F32)],
        input_output_aliases={2: 0} if extra else {},
        compiler_params=_params("parallel", "arbitrary"),
    )(a, b, *extra)


def ada_backward(call_t, dmod_cols, w, m, v):
    def body(c_ref, d_ref, w_ref, m_ref, v_ref, g_ref, dl_ref, nm_ref, nv_ref):
        ct, dm = c_ref[...], d_ref[...]
        g = ct[:, 0:1] * dm[0:1, :]
        for r in range(1, NDEV):
            g = g + ct[:, r:r + 1] * dm[r:r + 1, :]
        g_ref[...] = g
        dl_ref[...], nm_ref[...], nv_ref[...] = _adamw(w_ref[...], g, m_ref[...], v_ref[...])

    s = jax.ShapeDtypeStruct(w.shape, F32)
    return pl.pallas_call(
        body, name="ada_backward", out_shape=(s, s, s, s),
        in_specs=[VSPEC] * 5, out_specs=(VSPEC,) * 4, compiler_params=_params(),
    )(call_t, dmod_cols, w, m, v)


def adamw_small(total, ddw_mine, recipes, ws, ms, vs):
    n = len(ws)

    def body(*refs):
        tot, ddw = refs[0], refs[1]
        w_refs, m_refs, v_refs = refs[2:2 + n], refs[2 + n:2 + 2 * n], refs[2 + 2 * n:2 + 3 * n]
        outs = refs[2 + 3 * n:2 + 7 * n]
        loss_ref = refs[2 + 7 * n]
        for i, rec in enumerate(recipes):
            if rec == "dw":
                g = ddw[...]
            elif isinstance(rec, tuple):
                g = tot[rec[0]:rec[1], :]
            else:
                g = jnp.concatenate([tot[r:r + 1, :] for r in rec], axis=1) if len(rec) > 1 else tot[rec[0]:rec[0] + 1, :]
            dl, nm, nv = _adamw(w_refs[i][...], g, m_refs[i][...], v_refs[i][...])
            outs[4 * i][...] = g
            outs[4 * i + 1][...] = dl
            outs[4 * i + 2][...] = nm
            outs[4 * i + 3][...] = nv
        loss_ref[...] = tot[LOSS_ROW:LOSS_ROW + 1, 0:128]

    shapes = []
    for w in ws:
        shapes += [jax.ShapeDtypeStruct(w.shape, F32)] * 4
    res = pl.pallas_call(
        body, name="adamw_small", out_shape=tuple(shapes) + (jax.ShapeDtypeStruct((1, 128), F32),),
        in_specs=[VSPEC] * (2 + 3 * n), out_specs=(VSPEC,) * (4 * n + 1), compiler_params=_params(),
    )(total, ddw_mine, *ws, *ms, *vs)
    return [res[4 * i:4 * i + 4] for i in range(n)], res[4 * n]


def reduce_and_adamw(me, w, g_all, g_recv, m, v, name):
    r, c = w.shape
    br = min(128, r)

    def body(me_ref, w_ref, go_ref, gr_ref, m_ref, v_ref, g_ref, dl_ref, nm_ref, nv_ref):
        g = go_ref[...]
        for k in range(NDEV - 1):
            g = g + gr_ref[k].astype(F32)
        g_ref[...] = g
        dl_ref[...], nm_ref[...], nv_ref[...] = _adamw(w_ref[...], g, m_ref[...], v_ref[...])

    blk = pl.BlockSpec((br, c), lambda i, me_ref: (i, 0))
    s = jax.ShapeDtypeStruct(w.shape, F32)
    return pl.pallas_call(
        body, name=name, out_shape=(s, s, s, s),
        grid_spec=pltpu.PrefetchScalarGridSpec(
            num_scalar_prefetch=1, grid=(r // br,),
            in_specs=[blk, pl.BlockSpec((None, br, c), lambda i, me_ref: (me_ref[0], i, 0)),
                      pl.BlockSpec((NDEV - 1, br, c), lambda i, me_ref: (0, i, 0)), blk, blk],
            out_specs=(blk, blk, blk, blk)),
        compiler_params=_params("parallel"),
    )(me, w, g_all, g_recv, m, v)


def kernel(x, c, w_ada, b_ada, pre_norm_tm, post_norm_tm, pre_norm_cm, post_norm_cm, w_in, b_in, hg_lb_logits, hg_norm, conv_dw, conv_db, conv_ln_g, conv_ln_b, w_br_a, w_br_b, w_out, w_ff1, w_ff2, loss_target, m_w_ada, m_b_ada, m_pre_norm_tm, m_post_norm_tm, m_pre_norm_cm, m_post_norm_cm, m_w_in, m_b_in, m_hg_lb_logits, m_hg_norm, m_conv_dw, m_conv_db, m_conv_ln_g, m_conv_ln_b, m_w_br_a, m_w_br_b, m_w_out, m_w_ff1, m_w_ff2, v_w_ada, v_b_ada, v_pre_norm_tm, v_post_norm_tm, v_pre_norm_cm, v_post_norm_cm, v_w_in, v_b_in, v_hg_lb_logits, v_hg_norm, v_conv_dw, v_conv_db, v_conv_ln_g, v_conv_ln_b, v_w_br_a, v_w_br_b, v_w_out, v_w_ff1, v_w_ff2):
    t = x.shape[1]
    me = 4 * lax.axis_index("x") + 2 * lax.axis_index("y") + lax.axis_index("c")
    xs = x[0]
    tgt = loss_target[0]

    mod, call, dw_all = ada_forward(c, w_ada[0], b_ada, conv_dw[0])
    sh1, sc1, g1, sh2, sc2, g2 = [mod[:, i * D:(i + 1) * D] for i in range(6)]
    dw = jnp.pad(dw_all.transpose(1, 0, 2).reshape(KW, D), ((0, HALO - KW), (0, 0)))

    p, h, win_all = in_proj(xs, pre_norm_tm, sc1, sh1, w_in[0].astype(BF16), b_in)
    yc, act, (w1_all,) = conv_forward(p, dw, conv_db, conv_ln_g, conv_ln_b, [w_ff1[0].astype(BF16)])
    intra, qe, st_prev, dec, (w2_all, wa_all, wb_all, wo_all) = hgrn_local(
        p, hg_lb_logits, [w_ff2[0].astype(BF16), w_br_a[0].astype(BF16), w_br_b[0].astype(BF16),
                          w_out[0].astype(BF16)])
    wa, wb, wo = wa_all.reshape(D, D), wb_all.reshape(D, D), wo_all.reshape(D, D)
    x1 = mix_forward(xs, p, qe, intra, st_prev, act, wa, wb, wo, hg_norm, g1, post_norm_tm)

    dx1, h2, r, dz, dy2, acc_ffn = ffn_forward_backward(x1, tgt, w1_all, w2_all, pre_norm_cm, sc2, sh2, g2,
                                                        post_norm_cm)
    rows = D // NDEV
    g1_f, g1_h = weight_grad(h2, dz, NDEV, D, DFF // NDEV, False, "grad_w_ff1")
    g2_f, g2_h = weight_grad(r, dy2, 4, D, D, True, "grad_w_ff2")
    g2_f, g2_h = g2_f.reshape(NDEV, DFF // NDEV, D), g2_h.reshape(NDEV, DFF // NDEV, D)

    (dp, do, dqe, gt, dact, a16, mg16, dya, dyb, dy, acc_mix) = mix_backward(
        xs, p, qe, intra, st_prev, act, wa, wb, wo, hg_norm, g1, post_norm_tm, dx1, dec)
    ga_f, ga_h = weight_grad(a16, dya, 1, D, D, False, "grad_w_br_a")
    gb_f, gb_h = weight_grad(act, dyb, 1, D, D, False, "grad_w_br_b")
    go_f, go_h = weight_grad(mg16, dy, 1, D, D, False, "grad_w_out")
    dp, acc_conv, ddw, (r_ff1, r_ff2) = conv_backward(p, yc, dact, dw, conv_ln_g, conv_ln_b, [g1_h, g2_h], dp)
    gin_f, gin_h1 = weight_grad(
        h, dp, 5, D, D, False, "grad_w_in_first", b_col=lambda j: jnp.where(j < 3, j, j + 3),
        out_idx=lambda j: jnp.where(j == 0, 3, jnp.where(j < 3, j + 5, j + 1)), nout=NDEV)
    owners1, owners2 = [3, 6, 7, 4, 5], [0, 1, 2]
    send1, recv1, thru1, landing, token1 = exchange_start(gin_h1, owners1, None, "exchange_start_first")
    dp, acc_hg, (r_a, r_b, r_o) = hgrn_backward(
        p, hg_lb_logits + token1[0:1, 0:1], do, dqe, gt, st_prev,
        [ga_h.reshape(NDEV, rows, D), gb_h.reshape(NDEV, rows, D), go_h.reshape(NDEV, rows, D)], dp)
    landing = exchange_wait(send1, recv1, thru1, landing, acc_hg, owners1, "exchange_wait_first")
    gin_f, gin_h2 = weight_grad(h, dp, 3, D, D, False, "grad_w_in_second", b_col=lambda j: j + 3, nout=NDEV,
                                fill=gin_f)
    send2, recv2, thru2, landing, token = exchange_start(gin_h2, owners2, landing, "exchange_start_second")
    grad_x, acc_in, _ = in_proj_backward(dp, win_all, xs, dx1, pre_norm_tm, sc1, sh1 + token[0:1, 0:1], [])

    own = [ga_f.reshape(NDEV, rows, D), gb_f.reshape(NDEV, rows, D), go_f.reshape(NDEV, rows, D), g1_f, g2_f]
    recv = [r_a, r_b, r_o, r_ff1, r_ff2]
    big = {}
    names = ["w_br_a", "w_br_b", "w_out", "w_ff1", "w_ff2"]
    ws = [w_br_a, w_br_b, w_out, w_ff1, w_ff2]
    ms = [m_w_br_a, m_w_br_b, m_w_out, m_w_ff1, m_w_ff2]
    vs = [v_w_br_a, v_w_br_b, v_w_out, v_w_ff1, v_w_ff2]
    me1 = me.astype(jnp.int32).reshape(1)
    for i, nm in enumerate(names):
        big[nm] = [o[None] for o in reduce_and_adamw(me1, ws[i][0], own[i], recv[i], ms[i][0], vs[i][0], "adamw_" + nm)]
    big, grad_x, acc_in = lax.optimization_barrier((big, grad_x, acc_in))
    r_in = exchange_wait(send2, recv2, thru2, landing, acc_in, owners2, "exchange_wait_second")
    r_in, acc_in = lax.optimization_barrier((r_in, acc_in))
    big["w_in"] = [o[None] for o in reduce_and_adamw(me1, w_in[0], gin_f, r_in, m_w_in[0], v_w_in[0], "adamw_w_in")]

    dmod_rows = [2, 1, 8, 19, 18, 20]
    total, kept = gather_and_sum_rows([acc_in, acc_mix, acc_ffn, acc_hg, acc_conv, ddw], dmod_rows)
    dmod_all = kept[:, 0:6, :].reshape(NDEV, 6 * D)
    wcols = w_ada.shape[2]
    gwa, dwa, nmwa, nvwa = ada_backward(call.T, lax.dynamic_slice_in_dim(dmod_all, me * wcols, wcols, axis=1),
                                        w_ada[0], m_w_ada[0], v_w_ada[0])
    ddw_mine = lax.dynamic_slice_in_dim(total[40:40 + KW], me * (D // NDEV), D // NDEV, axis=1)

    small_names = ["b_ada", "pre_norm_tm", "post_norm_tm", "pre_norm_cm", "post_norm_cm", "b_in", "hg_lb_logits",
                   "hg_norm", "conv_db", "conv_ln_g", "conv_ln_b", "conv_dw"]
    recipes = [dmod_rows, [0], [9], [16], [17], [26, 27, 28, 11, 35, 36, 12, 13], (24, 26), [10], [32], [33], [34],
               "dw"]
    small_w = [b_ada, pre_norm_tm, post_norm_tm, pre_norm_cm, post_norm_cm, b_in, hg_lb_logits, hg_norm, conv_db,
               conv_ln_g, conv_ln_b, conv_dw[0]]
    small_m = [m_b_ada, m_pre_norm_tm, m_post_norm_tm, m_pre_norm_cm, m_post_norm_cm, m_b_in, m_hg_lb_logits,
               m_hg_norm, m_conv_db, m_conv_ln_g, m_conv_ln_b, m_conv_dw[0]]
    small_v = [v_b_ada, v_pre_norm_tm, v_post_norm_tm, v_pre_norm_cm, v_post_norm_cm, v_b_in, v_hg_lb_logits,
               v_hg_norm, v_conv_db, v_conv_ln_g, v_conv_ln_b, v_conv_dw[0]]
    small_out, loss_row = adamw_small(total, ddw_mine, recipes, small_w, small_m, small_v)
    loss = loss_row[0, 0]
    sm = {nm: list(o) for nm, o in zip(small_names, small_out)}
    sm["conv_dw"] = [o[None] for o in sm["conv_dw"]]

    order = ["w_ada", "b_ada", "pre_norm_tm", "post_norm_tm", "pre_norm_cm", "post_norm_cm", "w_in", "b_in",
             "hg_lb_logits", "hg_norm", "conv_dw", "conv_db", "conv_ln_g", "conv_ln_b", "w_br_a", "w_br_b", "w_out",
             "w_ff1", "w_ff2"]
    res = dict(sm)
    res.update(big)
    res["w_ada"] = [gwa[None], dwa[None], nmwa[None], nvwa[None]]
    outs = [loss, grad_x[None]]
    for j in range(4):
        outs += [res[nm][j] for nm in order]
    return tuple(outs)
```

```python
import functools

import jax
import jax.numpy as jnp
from jax import lax
from jax.experimental import pallas as pl
from jax.experimental.pallas import tpu as pltpu

F32 = jnp.float32
BF16 = jnp.bfloat16
MESH = pl.DeviceIdType.MESH
AXES = ("x", "y", "c")

D = 1024
HEADS = 8
DK = 128
LANE = 128
CH = 128
LEVELS = 7
KW = 31
HALO = 32
DFF = 4096
NDEV = 8
EPS = 1e-6
DP_SPLIT = (3, 6, 7, 0, 1, 2, 4, 5)
LOSS_ROW = 21
ADAM_LR, ADAM_B1, ADAM_B2, ADAM_EPS, ADAM_WD, ADAM_STEP = 0.001, 0.9, 0.999, 1e-08, 0.01, 10
VMEM_LIMIT = 58 * 1024 * 1024

_NN = (((1,), (0,)), ((), ()))
_NT = (((1,), (1,)), ((), ()))
_TN = (((0,), (0,)), ((), ()))

VSPEC = pl.BlockSpec(memory_space=pltpu.VMEM)
ANYSPEC = pl.BlockSpec(memory_space=pl.ANY)


def _params(*sem):
    return pltpu.CompilerParams(dimension_semantics=sem or None, vmem_limit_bytes=VMEM_LIMIT)


def _dot(a, b, dims):
    return lax.dot_general(a, b, dims, preferred_element_type=F32)


@jax.custom_vjp
def mm_nn(a, b):
    return _dot(a.astype(BF16), b.astype(BF16), _NN)


def _mm_nn_fwd(a, b):
    ab, bb = a.astype(BF16), b.astype(BF16)
    return _dot(ab, bb, _NN), (ab, bb)


def _mm_nn_bwd(res, ct):
    ab, bb = res
    cb = ct.astype(BF16)
    return _dot(cb, bb, _NT), _dot(ab, cb, _TN)


mm_nn.defvjp(_mm_nn_fwd, _mm_nn_bwd)


@jax.custom_vjp
def mm_nt(a, b):
    return _dot(a.astype(BF16), b.astype(BF16), _NT)


def _mm_nt_fwd(a, b):
    ab, bb = a.astype(BF16), b.astype(BF16)
    return _dot(ab, bb, _NT), (ab, bb)


def _mm_nt_bwd(res, ct):
    ab, bb = res
    cb = ct.astype(BF16)
    return _dot(cb, bb, _NN), _dot(cb, ab, _TN)


mm_nt.defvjp(_mm_nt_fwd, _mm_nt_bwd)


@jax.custom_vjp
def mm_tn(a, b):
    return _dot(a.astype(BF16), b.astype(BF16), _TN)


def _mm_tn_fwd(a, b):
    ab, bb = a.astype(BF16), b.astype(BF16)
    return _dot(ab, bb, _TN), (ab, bb)


def _mm_tn_bwd(res, ct):
    ab, bb = res
    cb = ct.astype(BF16)
    return _dot(bb, cb, _NT), _dot(ab, cb, _NN)


mm_tn.defvjp(_mm_tn_fwd, _mm_tn_bwd)


def _rms(x):
    return x * lax.rsqrt(jnp.mean(x * x, axis=-1, keepdims=True) + EPS)


def _silu(x):
    return x * jax.nn.sigmoid(x)


def f_modulate(x, gain, sc, sh):
    return _rms(x) * gain * (1.0 + sc) + sh


def f_residual(x, y, gate, gain):
    return x + gate * (_rms(y) * gain)


def f_merge(ga, gb, ya, yb):
    return jax.nn.sigmoid(ga) * ya + jax.nn.sigmoid(gb) * yb


def f_head_out(o, og, hg):
    heads = [_rms(o[:, h * DK:(h + 1) * DK]) for h in range(HEADS)]
    return jnp.concatenate(heads, axis=1) * hg * _silu(og)


def f_conv_act(u, g, b):
    mu = jnp.mean(u, axis=-1, keepdims=True)
    var = jnp.mean(jnp.square(u - mu), axis=-1, keepdims=True)
    return _silu((u - mu) * lax.rsqrt(var + EPS) * g + b)


def f_glu(cv, cg):
    return cv * jax.nn.sigmoid(cg)


def _split2(x):
    hi = x.astype(BF16)
    return hi, (x - hi.astype(F32)).astype(BF16)


def _tri(transposed):
    i = lax.broadcasted_iota(jnp.int32, (CH, CH), 1 if transposed else 0)
    t = lax.broadcasted_iota(jnp.int32, (CH, CH), 0 if transposed else 1)
    return jnp.where(t <= i, 1.0, 0.0).astype(BF16)


def _blocks3(x, rows):
    return x.reshape(CH // rows, rows, x.shape[-1])


def _mid_broadcast(b, lev):
    h = 1 << (LEVELS - 1 - lev)
    if h >= 4:
        x3 = _blocks3(b, 2 * h)
        return jnp.broadcast_to(x3[:, h - 1:h, :], x3.shape).reshape(b.shape)
    x3 = _blocks3(b, 8)
    sub = lax.broadcasted_iota(jnp.int32, x3.shape, 1)
    out = None
    for first in range(0, 8, 2 * h):
        piece = jnp.broadcast_to(x3[:, first + h - 1:first + h, :], x3.shape)
        out = piece if out is None else jnp.where(sub >= first, piece, out)
    return out.reshape(b.shape)


def _mid_scatter(d, lev):
    h = 1 << (LEVELS - 1 - lev)
    if h >= 4:
        x3 = _blocks3(d, 2 * h)
        row = lax.broadcasted_iota(jnp.int32, x3.shape, 1)
        total = jnp.sum(x3, axis=1, keepdims=True)
        return jnp.where(row == h - 1, total, 0.0).reshape(d.shape)
    x3 = _blocks3(d, 8)
    sub = lax.broadcasted_iota(jnp.int32, x3.shape, 1)
    out = jnp.zeros_like(x3)
    for first in range(0, 8, 2 * h):
        inside = (sub >= first) & (sub < first + 2 * h)
        total = jnp.sum(jnp.where(inside, x3, 0.0), axis=1, keepdims=True)
        out = jnp.where(sub == first + h - 1, total, out)
    return out.reshape(d.shape)


@jax.custom_vjp
def decay_sums(g):
    hi, lo = _split2(g)
    tri = _tri(False)
    b = _dot(tri, hi, _NN) + _dot(tri, lo, _NN)
    return (b,) + tuple(b - _mid_broadcast(b, lev) for lev in range(LEVELS))


def _decay_sums_fwd(g):
    return decay_sums(g), None


def _decay_sums_bwd(_, cts):
    db = cts[0]
    for lev in range(LEVELS):
        db = db + cts[1 + lev] - _mid_scatter(cts[1 + lev], lev)
    hi, lo = _split2(db)
    tri = _tri(True)
    return (_dot(tri, hi, _NN) + _dot(tri, lo, _NN),)


decay_sums.defvjp(_decay_sums_fwd, _decay_sums_bwd)


def _score_masks():
    i = lax.broadcasted_iota(jnp.int32, (CH, CH), 0)
    j = lax.broadcasted_iota(jnp.int32, (CH, CH), 1)
    masks = [i == j]
    for lev in range(LEVELS):
        sh = LEVELS - 1 - lev
        same = (i >> (sh + 1)) == (j >> (sh + 1))
        masks.append(same & (((i >> sh) & 1) == 1) & (((j >> sh) & 1) == 0))
    return [jnp.where(m, 1.0, 0.0) for m in masks]


def f_hgrn_chunk(q_r, f_r, v, logits):
    l0, l1 = logits[0:1, :], logits[1:2, :]
    mx = lax.stop_gradient(jnp.maximum(l0, l1))
    e0, e1 = jnp.exp(l0 - mx), jnp.exp(l1 - mx)
    lb = e0 / (e0 + e1)
    q = _silu(q_r)
    f = lb + (1.0 - lb) * jax.nn.sigmoid(f_r)
    k = 1.0 - f
    sums = decay_sums(jnp.log(f))
    b = sums[0]
    btot = b[CH - 1:CH, :]
    qe = q * jnp.exp(b)
    ke = k * jnp.exp(btot - b)
    dec = jnp.exp(btot)
    qs, ks = [q], [k]
    row = lax.broadcasted_iota(jnp.int32, b.shape, 0)
    for lev in range(LEVELS):
        upper = ((row >> (LEVELS - 1 - lev)) & 1) == 1
        e = sums[1 + lev]
        qs.append(q * jnp.exp(jnp.where(upper, e, 0.0)))
        ks.append(k * jnp.exp(jnp.where(upper, 0.0, -e)))
    masks = _score_masks()
    intra, ut = [], []
    for h in range(HEADS):
        sl = slice(h * DK, (h + 1) * DK)
        sc = None
        for lev in range(LEVELS + 1):
            s = mm_nt(qs[lev][:, sl], ks[lev][:, sl]) * masks[lev]
            sc = s if sc is None else sc + s
        intra.append(mm_nn(sc, v[:, sl]))
        ut.append(mm_tn(v[:, sl], ke[:, sl]))
    return jnp.concatenate(intra, axis=1), qe, jnp.concatenate(ut, axis=1), dec


def _inter(qe_b, st_ref, rows):
    out = []
    for ci in range(rows // CH):
        st = st_ref[ci].astype(BF16)
        heads = [_dot(qe_b[ci * CH:(ci + 1) * CH, h * DK:(h + 1) * DK], st[:, h * DK:(h + 1) * DK], _NT)
                 for h in range(HEADS)]
        out.append(jnp.concatenate(heads, axis=1))
    return jnp.concatenate(out, axis=0)


def _shift_stack(src, dst, rows):
    dst[0, 0:rows, :] = src[0:rows, :]
    for b in range(1, 8):
        dst[b, 0:rows - 8, :] = src[pl.ds(b, rows - 8), :]


def _shifted(stack, offset, rows, lanes=slice(None)):
    return stack[offset % 8, pl.ds(8 * (offset // 8), rows), lanes]


def _adamw(w, g, m, v):
    m = ADAM_B1 * m + (1.0 - ADAM_B1) * g
    v = ADAM_B2 * v + (1.0 - ADAM_B2) * jnp.square(g)
    m_hat = m / (1.0 - ADAM_B1 ** ADAM_STEP)
    v_hat = v / (1.0 - ADAM_B2 ** ADAM_STEP)
    delta = -ADAM_LR * (m_hat / (jnp.sqrt(v_hat) + ADAM_EPS) + ADAM_WD * w)
    return delta, m, v


def _me():
    return lax.axis_index("x"), lax.axis_index("y"), lax.axis_index("c")


def _peer(k):
    x, y, c = _me()
    mask = k + 1
    px = (1 - x) if (mask >> 2) & 1 else x
    py = (1 - y) if (mask >> 1) & 1 else y
    pc = (1 - c) if mask & 1 else c
    return (px, py, pc), 4 * px + 2 * py + pc


def ada_forward(c, w_ada, b_ada, dw):
    wcols = w_ada.shape[1]

    def body(c_ref, w_ref, b_ref, dw_ref, mod_ref, call_ref, dwall_ref, part_ref, modp_ref, send_sems, recv_sems):
        x, y, cc = _me()
        me = 4 * x + 2 * y + cc
        call_ref[me] = _silu(c_ref[...])
        dwall_ref[me] = dw_ref[...]
        sends = []
        for k in range(NDEV - 1):
            dev, pidx = _peer(k)
            cp = pltpu.make_async_remote_copy(dwall_ref.at[me], dwall_ref.at[me], send_sems.at[2 * (NDEV - 1) + k],
                                              recv_sems.at[2 * (NDEV - 1) + k], device_id=dev, device_id_type=MESH)
            cp.start()
            sends.append(cp)
        for k in range(NDEV - 1):
            dev, _ = _peer(k)
            cp = pltpu.make_async_remote_copy(call_ref.at[me], call_ref.at[me], send_sems.at[k], recv_sems.at[k],
                                              device_id=dev, device_id_type=MESH)
            cp.start()
            sends.append(cp)
        for k in range(NDEV - 1):
            _, pidx = _peer(k)
            pltpu.make_async_remote_copy(call_ref.at[pidx], call_ref.at[pidx], send_sems.at[k], recv_sems.at[k],
                                         device_id=_peer(k)[0], device_id_type=MESH).wait_recv()
        call = jnp.concatenate([call_ref[r] for r in range(NDEV)], axis=0)
        part = _dot(call.astype(BF16), w_ref[...].astype(BF16), _NN)
        for r in range(NDEV):
            part_ref[r] = part[r:r + 1, :]
        modp_ref[me] = part_ref[me]
        for k in range(NDEV - 1):
            dev, pidx = _peer(k)
            cp = pltpu.make_async_remote_copy(part_ref.at[pidx], modp_ref.at[me], send_sems.at[NDEV - 1 + k],
                                              recv_sems.at[NDEV - 1 + k], device_id=dev, device_id_type=MESH)
            cp.start()
            sends.append(cp)
        for k in range(NDEV - 1):
            dev, pidx = _peer(k)
            pltpu.make_async_remote_copy(part_ref.at[pidx], modp_ref.at[pidx], send_sems.at[NDEV - 1 + k],
                                         recv_sems.at[NDEV - 1 + k], device_id=dev, device_id_type=MESH).wait_recv()
        for k in range(NDEV - 1):
            dev, pidx = _peer(k)
            pltpu.make_async_remote_copy(dwall_ref.at[pidx], dwall_ref.at[pidx], send_sems.at[2 * (NDEV - 1) + k],
                                         recv_sems.at[2 * (NDEV - 1) + k], device_id=dev,
                                         device_id_type=MESH).wait_recv()
        for cp in sends:
            cp.wait_send()
        mod_ref[...] = jnp.concatenate([modp_ref[r] for r in range(NDEV)], axis=1) + b_ref[...]

    mod, call, dw_all = pl.pallas_call(
        body, name="ada_forward",
        out_shape=(jax.ShapeDtypeStruct((1, NDEV * wcols), F32), jax.ShapeDtypeStruct((NDEV, 1, D), F32),
                   jax.ShapeDtypeStruct((NDEV,) + dw.shape, F32)),
        in_specs=[VSPEC] * 4, out_specs=(VSPEC,) * 3,
        scratch_shapes=[pltpu.VMEM((NDEV, 1, wcols), F32), pltpu.VMEM((NDEV, 1, wcols), F32),
                        pltpu.SemaphoreType.DMA((3 * (NDEV - 1),)), pltpu.SemaphoreType.DMA((3 * (NDEV - 1),))],
        compiler_params=_params(),
    )(c, w_ada, b_ada, dw)
    return mod, call.reshape(NDEV, D), dw_all


def _comm_sems(n, local):
    sems = [pltpu.SemaphoreType.DMA((7 * n,)), pltpu.SemaphoreType.DMA((7 * n,))]
    return sems + ([pltpu.SemaphoreType.DMA((n,))] if local else [])


def _gather2_ops(ins, outs, send_sems, recv_sems, local_sems):
    n = len(ins)
    x, y, c = _me()
    me, sibling = (x, y, c), (x, y, 1 - c)
    chips = [(1 - x, y), (x, 1 - y), (1 - x, 1 - y)]

    def slot(p):
        return 4 * p[0] + 2 * p[1] + p[2]

    def copy(a, k, block, to, src=None):
        return pltpu.make_async_remote_copy(
            src_ref=outs[a].at[slot(block)] if src is None else src, dst_ref=outs[a].at[slot(block)],
            send_sem=send_sems.at[a * 7 + k], recv_sem=recv_sems.at[a * 7 + k], device_id=to, device_id_type=MESH)

    def local(a):
        return pltpu.make_async_copy(ins[a], outs[a].at[slot(me)], local_sems.at[a])

    def first(a):
        return [copy(a, 0, me, sibling, src=ins[a])] + [copy(a, 1 + j, me, (*chip, c), src=ins[a])
                                                        for j, chip in enumerate(chips)]

    def passed(a):
        return [copy(a, 4 + j, (*chip, c), sibling) for j, chip in enumerate(chips)]

    def start():
        for a in range(n):
            local(a).start()
            for cp in first(a):
                cp.start()

    def forward():
        for j, chip in enumerate(chips):
            for a in range(n):
                copy(a, 1 + j, (*chip, c), me).wait_recv()
                passed(a)[j].start()

    def finish():
        for a in range(n):
            copy(a, 0, sibling, me).wait_recv()
            for j, chip in enumerate(chips):
                copy(a, 4 + j, (*chip, 1 - c), me).wait_recv()
        for a in range(n):
            for cp in first(a) + passed(a):
                cp.wait_send()
            local(a).wait()

    return start, forward, finish


def _exchange_ops(ins, outs, send_sems, recv_sems):
    n = len(ins)

    def copy(a, k):
        dev, pidx = _peer(k)
        return pltpu.make_async_remote_copy(ins[a].at[pidx], outs[a].at[k], send_sems.at[a * 7 + k],
                                            recv_sems.at[a * 7 + k], device_id=dev, device_id_type=MESH)

    def start():
        for k in range(NDEV - 1):
            for a in range(n):
                copy(a, k).start()

    def finish():
        for k in range(NDEV - 1):
            for a in range(n):
                copy(a, k).wait_recv()
        for k in range(NDEV - 1):
            for a in range(n):
                copy(a, k).wait_send()

    return start, finish


def _gathered(shards):
    return tuple(jax.ShapeDtypeStruct((NDEV,) + s.shape, s.dtype) for s in shards)


def _exchanged(blocks):
    return tuple(jax.ShapeDtypeStruct((NDEV - 1,) + b.shape[1:], b.dtype) for b in blocks)


def exchange_start(blocks):
    landing = lax.empty((NDEV - 1,) + blocks.shape[1:], blocks.dtype)
    hbm = pl.BlockSpec(memory_space=pltpu.HBM)
    sem = pl.BlockSpec(memory_space=pltpu.SEMAPHORE)

    def body(src_ref, land_ref, send_sems, recv_sems, src_thru, land_thru, token):
        for k in range(NDEV - 1):
            dev, pidx = _peer(k)
            pltpu.make_async_remote_copy(src_ref.at[pidx], land_ref.at[k], send_sems.at[k], recv_sems.at[k],
                                         device_id=dev, device_id_type=MESH).start()
        token[...] = jnp.zeros_like(token)

    return pl.pallas_call(
        body, name="exchange_start",
        out_shape=(pltpu.SemaphoreType.DMA((NDEV - 1,)), pltpu.SemaphoreType.DMA((NDEV - 1,)),
                   pltpu.HBM(blocks.shape, blocks.dtype), pltpu.HBM(landing.shape, landing.dtype),
                   jax.ShapeDtypeStruct((8, 128), F32)),
        in_specs=(hbm, hbm), out_specs=(sem, sem, hbm, hbm, VSPEC), input_output_aliases={0: 2, 1: 3},
        compiler_params=pltpu.CompilerParams(has_side_effects=pltpu.SideEffectType.DATAFLOW_SIDE_EFFECTING),
    )(pltpu.with_memory_space_constraint(blocks, pltpu.HBM), pltpu.with_memory_space_constraint(landing, pltpu.HBM))


def exchange_wait(send_sems, recv_sems, src_thru, land_thru, after):
    hbm = pl.BlockSpec(memory_space=pltpu.HBM)
    sem = pl.BlockSpec(memory_space=pltpu.SEMAPHORE)

    def body(src_ref, land_ref, send_sems, recv_sems, after_ref, src_dead, got_ref):
        for k in range(NDEV - 1):
            dev, pidx = _peer(k)
            cp = pltpu.make_async_remote_copy(src_ref.at[pidx], land_ref.at[k], send_sems.at[k], recv_sems.at[k],
                                              device_id=dev, device_id_type=MESH)
            cp.wait_send()
            cp.wait_recv()

    return pl.pallas_call(
        body, name="exchange_wait",
        out_shape=(pltpu.HBM(src_thru.shape, src_thru.dtype), pltpu.HBM(land_thru.shape, land_thru.dtype)),
        in_specs=(hbm, hbm, sem, sem, ANYSPEC), out_specs=(hbm, hbm), input_output_aliases={0: 0, 1: 1},
        compiler_params=pltpu.CompilerParams(has_side_effects=pltpu.SideEffectType.DATAFLOW_SIDE_EFFECTING),
    )(src_thru, land_thru, send_sems, recv_sems, after)[1]


def gather_and_sum_rows(parts, keep_rows):
    n = len(parts)
    offs = [sum(p.shape[0] for p in parts[:i]) for i in range(n)]
    rows = sum(p.shape[0] for p in parts)

    def body(*refs):
        sum_ref, keep_ref, buf_ref, send_sems, recv_sems = refs[n:]
        x, y, c = _me()
        me = 4 * x + 2 * y + c
        for i in range(n):
            buf_ref[me, offs[i]:offs[i] + parts[i].shape[0], :] = refs[i][...]

        def copy(k, block):
            return pltpu.make_async_remote_copy(buf_ref.at[block], buf_ref.at[block], send_sems.at[k],
                                                recv_sems.at[k], device_id=_peer(k)[0], device_id_type=MESH)

        for k in range(NDEV - 1):
            copy(k, me).start()
        for k in range(NDEV - 1):
            copy(k, _peer(k)[1]).wait_recv()
        for k in range(NDEV - 1):
            copy(k, me).wait_send()
        total = buf_ref[0]
        for r in range(1, NDEV):
            total = total + buf_ref[r]
        sum_ref[...] = total
        keep_ref[...] = jnp.zeros_like(keep_ref)
        for r in range(NDEV):
            for j, src in enumerate(keep_rows):
                keep_ref[r, j:j + 1, :] = buf_ref[r, src:src + 1, :]

    return pl.pallas_call(
        body, name="gather_and_sum_rows",
        out_shape=(jax.ShapeDtypeStruct((rows, D), F32), jax.ShapeDtypeStruct((NDEV, 8, D), F32)),
        in_specs=[VSPEC] * n, out_specs=(VSPEC, VSPEC),
        scratch_shapes=[pltpu.VMEM((NDEV, rows, D), F32), pltpu.SemaphoreType.DMA((NDEV - 1,)),
                        pltpu.SemaphoreType.DMA((NDEV - 1,))],
        compiler_params=_params(),
    )(*parts)


def _gather_chips():
    x, y, c = _me()
    north = c == 1
    first = (jnp.where(north, 1 - x, x), jnp.where(north, y, 1 - y))
    second = (jnp.where(north, x, 1 - x), jnp.where(north, 1 - y, y))
    return [first, second, (1 - x, 1 - y)]


def _arrival_order():
    x, y, c = _me()
    first, second, far = _gather_chips()
    devs = [(x, y, c), (x, y, 1 - c), (*first, c), (*second, 1 - c), (*second, c), (*first, 1 - c), (*far, c),
            (*far, 1 - c)]
    return jnp.stack([4 * d[0] + 2 * d[1] + d[2] for d in devs]).astype(jnp.int32)


def in_proj(x, gain, sc, sh, w_shard, b_in):
    t = x.shape[0]
    tm = min(512, t // 2)
    nm = t // tm
    order = _arrival_order()

    def body(order_ref, x_ref, g_ref, sc_ref, sh_ref, b_ref, wsh_ref, p_ref, h_ref, wall_ref,
             wbuf, h_all, send_sems, recv_sems, local_sems):
        k, m = pl.program_id(0), pl.program_id(1)
        mx, my, mc = _me()
        me, sibling = (mx, my, mc), (mx, my, 1 - mc)
        chips = _gather_chips()

        def slot(d):
            return 4 * d[0] + 2 * d[1] + d[2]

        def copy(sem, block, to, src=None):
            return pltpu.make_async_remote_copy(
                src_ref=wall_ref.at[slot(block)] if src is None else src, dst_ref=wall_ref.at[slot(block)],
                send_sem=send_sems.at[sem], recv_sem=recv_sems.at[sem], device_id=to, device_id_type=MESH)

        own = pltpu.make_async_copy(wsh_ref, wall_ref.at[slot(me)], local_sems.at[0])
        first = [copy(0, me, sibling, src=wsh_ref)] + [copy(1 + j, me, (*q, mc), src=wsh_ref)
                                                       for j, q in enumerate(chips)]
        passed = [copy(4 + j, (*q, mc), sibling) for j, q in enumerate(chips)]

        def load(kk):
            src = wsh_ref if kk == 0 else wall_ref.at[order_ref[kk]]
            return pltpu.make_async_copy(src, wbuf.at[kk % 2], local_sems.at[1 + kk % 2])

        def arrived(kk):
            if kk == 1:
                copy(0, sibling, me).wait_recv()
            elif kk in (2, 4, 6):
                j = kk // 2 - 1
                copy(1 + j, (*chips[j], mc), me).wait_recv()
                passed[j].start()
                if j < 2:
                    first[2 + j].start()
            else:
                j = {3: 1, 5: 0, 7: 2}[kk]
                copy(4 + {1: 0, 0: 1, 2: 2}[j], (*chips[j], 1 - mc), me).wait_recv()

        @pl.when((k == 0) & (m == 0))
        def _():
            own.start()
            for cp in first[:2]:
                cp.start()
            load(0).start()
            load(0).wait()

        for kk in range(1, NDEV):
            @pl.when((k == kk - 1) & (m == nm // 2))
            def _(kk=kk):
                arrived(kk)
                load(kk).start()

            @pl.when((k == kk) & (m == 0))
            def _(kk=kk):
                load(kk).wait()

        rows = pl.ds(pl.multiple_of(m * tm, tm), tm)

        @pl.when(k == 0)
        def _():
            h_all[rows, :] = f_modulate(x_ref[...], g_ref[...], sc_ref[...], sh_ref[...]).astype(BF16)

        p_ref[...] = _dot(h_all[rows, :], wbuf[k % 2], _NN) + b_ref[...]

        @pl.when((k == NDEV - 1) & (m == nm - 1))
        def _():
            for cp in first + passed:
                cp.wait_send()
            own.wait()
            out = pltpu.make_async_copy(h_all, h_ref, local_sems.at[0])
            out.start()
            out.wait()

    vec = pl.BlockSpec((1, D), lambda k, m, o: (0, 0))
    return pl.pallas_call(
        body, name="in_proj",
        out_shape=(jax.ShapeDtypeStruct((t, NDEV * D), F32), jax.ShapeDtypeStruct((t, D), BF16),
                   jax.ShapeDtypeStruct((NDEV, D, D), BF16)),
        grid_spec=pltpu.PrefetchScalarGridSpec(
            num_scalar_prefetch=1, grid=(NDEV, nm),
            in_specs=[pl.BlockSpec((tm, D), lambda k, m, o: (jnp.where(k == 0, m, 0), 0)), vec, vec, vec,
                      pl.BlockSpec((1, D), lambda k, m, o: (0, o[k])), ANYSPEC],
            out_specs=(pl.BlockSpec((tm, D), lambda k, m, o: (m, o[k])), ANYSPEC, ANYSPEC),
            scratch_shapes=[pltpu.VMEM((2, D, D), BF16), pltpu.VMEM((t, D), BF16), pltpu.SemaphoreType.DMA((7,)),
                            pltpu.SemaphoreType.DMA((7,)), pltpu.SemaphoreType.DMA((3,))]),
        compiler_params=_params("arbitrary", "arbitrary"),
    )(order, x, gain, sc, sh, b_in, w_shard)


def hgrn_local(p, logits, shards):
    t = p.shape[0]
    nc = t // CH
    n = len(shards)

    def body(*refs):
        q_ref, f_ref, v_ref, l_ref = refs[:4]
        intra_ref, qe_ref, st_ref, dec_ref = refs[4 + n:8 + n]
        state = refs[8 + 2 * n]
        start, forward, finish = _gather2_ops(refs[4:4 + n], refs[8 + n:8 + 2 * n], *refs[9 + 2 * n:])
        step = pl.program_id(0)

        @pl.when(step == 0)
        def _():
            start()
            state[...] = jnp.zeros_like(state)

        intra, qe, ut, dec = f_hgrn_chunk(q_ref[...], f_ref[...], v_ref[...], l_ref[...])
        intra_ref[...] = intra
        qe_ref[...] = qe.astype(BF16)
        dec_ref[...] = dec
        s = state[...]
        st_ref[...] = s
        state[...] = dec * s + ut
        pl.when(step == (3 * nc) // 4)(forward)
        pl.when(step == nc - 1)(finish)

    col = lambda j: pl.BlockSpec((CH, D), lambda c: (c, j))
    res = pl.pallas_call(
        body, name="hgrn_local", grid=(nc,),
        out_shape=(jax.ShapeDtypeStruct((t, D), F32), jax.ShapeDtypeStruct((t, D), BF16),
                   jax.ShapeDtypeStruct((nc, DK, D), F32), jax.ShapeDtypeStruct((nc, 1, D), F32)) + _gathered(shards),
        in_specs=[col(0), col(1), col(2), pl.BlockSpec((2, D), lambda c: (0, 0))] + [ANYSPEC] * n,
        out_specs=(pl.BlockSpec((CH, D), lambda c: (c, 0)), pl.BlockSpec((CH, D), lambda c: (c, 0)),
                   pl.BlockSpec((None, DK, D), lambda c: (c, 0, 0)), pl.BlockSpec((None, 1, D), lambda c: (c, 0, 0)))
        + (ANYSPEC,) * n,
        scratch_shapes=[pltpu.VMEM((DK, D), F32)] + _comm_sems(n, True),
        compiler_params=_params("arbitrary"),
    )(p, p, p, logits, *shards)
    return res[:4] + (res[4:],)


def _conv_tile(t):
    return min(256, t)


def conv_forward(p, dw, db, ln_g, ln_b, shards):
    t = p.shape[0]
    tm = _conv_tile(t)
    per = tm // HALO
    n = len(shards)
    nt = t // tm

    def body(*refs):
        cv_ref, cg_ref, cvp_ref, cgp_ref, dw_ref, db_ref, g_ref, b_ref = refs[:8]
        yc_ref, act_ref = refs[8 + n:10 + n]
        uext, ush = refs[10 + 2 * n:12 + 2 * n]
        start, forward, finish = _gather2_ops(refs[8:8 + n], refs[10 + n:10 + 2 * n], *refs[12 + 2 * n:])
        step = pl.program_id(0)
        pl.when(step == 0)(start)
        pl.when(step == nt - 1)(forward)
        uext[0:HALO, :] = jnp.where(step == 0, 0.0, f_glu(cvp_ref[...], cgp_ref[...]))
        uext[HALO:HALO + tm, :] = f_glu(cv_ref[...], cg_ref[...])
        _shift_stack(uext, ush, tm + HALO)
        acc = jnp.zeros((tm, D), F32) + db_ref[...]
        for w in range(KW):
            acc = acc + dw_ref[w:w + 1, :] * _shifted(ush, HALO - KW + 1 + w, tm)
        yc_ref[...] = acc
        act_ref[...] = f_conv_act(acc, g_ref[...], b_ref[...]).astype(BF16)
        pl.when(step == nt - 1)(finish)

    vec = pl.BlockSpec((1, D), lambda m: (0, 0))
    prev = lambda j: pl.BlockSpec((HALO, D), lambda m: (jnp.maximum(m * per - 1, 0), j))
    res = pl.pallas_call(
        body, name="conv_forward", grid=(nt,),
        out_shape=(jax.ShapeDtypeStruct((t, D), F32), jax.ShapeDtypeStruct((t, D), BF16)) + _gathered(shards),
        in_specs=[pl.BlockSpec((tm, D), lambda m: (m, 4)), pl.BlockSpec((tm, D), lambda m: (m, 5)), prev(4), prev(5),
                  pl.BlockSpec((HALO, D), lambda m: (0, 0)), vec, vec, vec] + [ANYSPEC] * n,
        out_specs=(pl.BlockSpec((tm, D), lambda m: (m, 0)), pl.BlockSpec((tm, D), lambda m: (m, 0))) + (ANYSPEC,) * n,
        scratch_shapes=[pltpu.VMEM((HALO + tm, D), F32), pltpu.VMEM((8, tm + HALO, D), F32)] + _comm_sems(n, True),
        compiler_params=_params("arbitrary"),
    )(p, p, p, p, dw, db, ln_g, ln_b, *shards)
    return res[0], res[1], res[2:]


def _mix_tile(t):
    return min(256, t)


def _mix_forward_tile(x_ref, og_ref, ga_ref, gb_ref, qe_ref, intra_ref, st_ref, act_ref, wa_ref, wb_ref, wo_ref,
                      hg_ref, rows):
    o = _inter(qe_ref[...], st_ref, rows) + intra_ref[...]
    a = f_head_out(o, og_ref[...], hg_ref[...])
    ya = _dot(a.astype(BF16), wa_ref[...], _NN)
    yb = _dot(act_ref[...], wb_ref[...], _NN)
    merged = f_merge(ga_ref[...], gb_ref[...], ya, yb)
    y = _dot(merged.astype(BF16), wo_ref[...], _NN)
    return o, a, ya, yb, merged, y


def _mix_specs(tm, tiles=None):
    at = (lambda m: m) if tiles is None else (lambda m: tiles - 1 - m)
    col = lambda j: pl.BlockSpec((tm, D), lambda m: (at(m), j))
    row = pl.BlockSpec((tm, D), lambda m: (at(m), 0))
    per_chunk = lambda rows: pl.BlockSpec((tm // CH, rows, D), lambda m: (at(m), 0, 0))
    return col, row, pl.BlockSpec((1, D), lambda m: (0, 0)), per_chunk


def mix_forward(x, p, qe, intra, st_prev, act, wa, wb, wo, hg, g1, post):
    t = x.shape[0]
    tm = _mix_tile(t)

    def body(x_ref, og_ref, ga_ref, gb_ref, qe_ref, intra_ref, st_ref, act_ref, wa_ref, wb_ref, wo_ref, hg_ref,
             g1_ref, post_ref, x1_ref):
        y = _mix_forward_tile(x_ref, og_ref, ga_ref, gb_ref, qe_ref, intra_ref, st_ref, act_ref, wa_ref, wb_ref,
                              wo_ref, hg_ref, tm)[-1]
        x1_ref[...] = f_residual(x_ref[...], y, g1_ref[...], post_ref[...])

    col, row, vec, per_chunk = _mix_specs(tm)
    return pl.pallas_call(
        body, name="mix_forward", grid=(t // tm,),
        out_shape=jax.ShapeDtypeStruct((t, D), F32),
        in_specs=[row, col(3), col(6), col(7), row, row, per_chunk(DK), row, VSPEC, VSPEC, VSPEC, vec, vec, vec],
        out_specs=row,
        compiler_params=_params("parallel"),
    )(x, p, p, p, qe, intra, st_prev, act, wa, wb, wo, hg, g1, post)


def ffn_forward_backward(x1, target, w1, w2, pre, sc, sh, g2, post):
    t = x1.shape[0]
    tm = min(256, t)
    nb = w1.shape[0]
    fb = w1.shape[2]

    def body(x_ref, tg_ref, w1_ref, w2_ref, pre_ref, sc_ref, sh_ref, g2_ref, post_ref,
             dx_ref, h2_ref, r_ref, dz_ref, dy2_ref, acc_ref, z_sc):
        @pl.when(pl.program_id(0) == 0)
        def _():
            acc_ref[...] = jnp.zeros_like(acc_ref)

        x1v = x_ref[...]
        h2, vjp_h = jax.vjp(f_modulate, x1v, pre_ref[...], sc_ref[...], sh_ref[...])
        h2b = h2.astype(BF16)
        h2_ref[...] = h2b
        y2 = jnp.zeros((tm, D), F32)
        for n in range(nb):
            z = _dot(h2b, w1_ref[n], _NN)
            z_sc[:, n * fb:(n + 1) * fb] = z
            r = jnp.square(jnp.maximum(z, 0.0)).astype(BF16)
            r_ref[:, n * fb:(n + 1) * fb] = r
            y2 = y2 + _dot(r, w2_ref[n], _NN)
        out, vjp_r = jax.vjp(f_residual, x1v, y2, g2_ref[...], post_ref[...])
        err = out - tg_ref[...]
        tok = jnp.mean(jnp.square(err), axis=-1, keepdims=True)
        acc_ref[5:6, :] += 0.5 * jnp.sum(tok, axis=0, keepdims=True)
        dx_a, dy2, dg2, dpost = vjp_r(err * (1.0 / D))
        dy2b = dy2.astype(BF16)
        dy2_ref[...] = dy2b
        dh2 = jnp.zeros((tm, D), F32)
        for n in range(nb):
            dr = _dot(dy2b, w2_ref[n], _NT)
            dz = (dr * (2.0 * jnp.maximum(z_sc[:, n * fb:(n + 1) * fb], 0.0))).astype(BF16)
            dz_ref[:, n * fb:(n + 1) * fb] = dz
            dh2 = dh2 + _dot(dz, w1_ref[n], _NT)
        dx_b, dpre, dsc, dsh = vjp_h(dh2)
        dx_ref[...] = dx_a + dx_b
        acc_ref[0:1, :] += dpre
        acc_ref[1:2, :] += dpost
        acc_ref[2:3, :] += dsc
        acc_ref[3:4, :] += dsh
        acc_ref[4:5, :] += dg2

    row = pl.BlockSpec((tm, D), lambda m: (m, 0))
    wide = pl.BlockSpec((tm, DFF), lambda m: (m, 0))
    vec = pl.BlockSpec((1, D), lambda m: (0, 0))
    return pl.pallas_call(
        body, name="ffn_forward_backward", grid=(t // tm,),
        out_shape=(jax.ShapeDtypeStruct((t, D), F32), jax.ShapeDtypeStruct((t, D), BF16),
                   jax.ShapeDtypeStruct((t, DFF), BF16), jax.ShapeDtypeStruct((t, DFF), BF16),
                   jax.ShapeDtypeStruct((t, D), BF16), jax.ShapeDtypeStruct((8, D), F32)),
        in_specs=[row, row, VSPEC, VSPEC, vec, vec, vec, vec, vec],
        out_specs=(row, row, wide, wide, row, pl.BlockSpec((8, D), lambda m: (0, 0))),
        scratch_shapes=[pltpu.VMEM((tm, DFF), F32)],
        compiler_params=_params("arbitrary"),
    )(x1, target, w1, w2, pre, sc, sh, g2, post)


def mix_backward(x, p, qe, intra, st_prev, act, wa, wb, wo, hg, g1, post, dx1, dec):
    t = x.shape[0]
    tm = _mix_tile(t)
    nc = t // CH

    def body(x_ref, og_ref, ga_ref, gb_ref, qe_ref, intra_ref, st_ref, act_ref, wa_ref, wb_ref, wo_ref, hg_ref,
             g1_ref, post_ref, dx1_ref, dec_ref,
             dp_ref, do_ref, dqe_ref, gt_ref, dact_ref, a_ref, mg_ref, dya_ref, dyb_ref, dy_ref, acc_ref, gstate):
        @pl.when(pl.program_id(0) == 0)
        def _():
            acc_ref[...] = jnp.zeros_like(acc_ref)
            gstate[...] = jnp.zeros_like(gstate)

        o, a, ya, yb, merged, y = _mix_forward_tile(x_ref, og_ref, ga_ref, gb_ref, qe_ref, intra_ref, st_ref, act_ref,
                                                    wa_ref, wb_ref, wo_ref, hg_ref, tm)
        a_ref[...] = a.astype(BF16)
        mg_ref[...] = merged.astype(BF16)
        _, vjp_r = jax.vjp(f_residual, x_ref[...], y, g1_ref[...], post_ref[...])
        _, dy, dg1, dpost = vjp_r(dx1_ref[...])
        dyb16 = dy.astype(BF16)
        dy_ref[...] = dyb16
        dmerged = _dot(dyb16, wo_ref[...], _NT)
        _, vjp_m = jax.vjp(f_merge, ga_ref[...], gb_ref[...], ya, yb)
        dga, dgb, dya, dyb = vjp_m(dmerged)
        dp_ref[:, D:2 * D] = dga.astype(BF16)
        dp_ref[:, 2 * D:3 * D] = dgb.astype(BF16)
        dya16, dyb16b = dya.astype(BF16), dyb.astype(BF16)
        dya_ref[...] = dya16
        dyb_ref[...] = dyb16b
        da = _dot(dya16, wa_ref[...], _NT)
        dact_ref[...] = _dot(dyb16b, wb_ref[...], _NT)
        _, vjp_a = jax.vjp(f_head_out, o, og_ref[...], hg_ref[...])
        do, dog, dhg = vjp_a(da)
        dp_ref[:, 0:D] = dog.astype(BF16)
        do_ref[...] = do
        do16 = do.astype(BF16)
        qe16 = qe_ref[...]
        for ci in reversed(range(tm // CH)):
            st = st_ref[ci].astype(BF16)
            rows = slice(ci * CH, (ci + 1) * CH)
            dqe, vt = [], []
            for h in range(HEADS):
                sl = slice(h * DK, (h + 1) * DK)
                dqe.append(_dot(do16[rows, sl], st[:, sl], _NN))
                vt.append(_dot(do16[rows, sl], qe16[rows, sl], _TN))
            dqe_ref[rows, :] = jnp.concatenate(dqe, axis=1)
            g = gstate[...]
            gt_ref[ci] = g
            gstate[...] = dec_ref[ci] * g + jnp.concatenate(vt, axis=1)
        acc_ref[0:1, :] += dg1
        acc_ref[1:2, :] += dpost
        acc_ref[2:3, :] += dhg
        acc_ref[3:4, :] += jnp.sum(dog, axis=0, keepdims=True)
        acc_ref[4:5, :] += jnp.sum(dga, axis=0, keepdims=True)
        acc_ref[5:6, :] += jnp.sum(dgb, axis=0, keepdims=True)

    nt = t // tm
    col, row, vec, per_chunk = _mix_specs(tm, nt)
    b16 = jax.ShapeDtypeStruct((t, D), BF16)
    f32 = jax.ShapeDtypeStruct((t, D), F32)
    return pl.pallas_call(
        body, name="mix_backward", grid=(nt,),
        out_shape=(jax.ShapeDtypeStruct((t, NDEV * D), BF16), f32, f32, jax.ShapeDtypeStruct((nc, DK, D), F32), f32,
                   b16, b16, b16, b16, b16, jax.ShapeDtypeStruct((8, D), F32)),
        in_specs=[row, col(3), col(6), col(7), row, row, per_chunk(DK), row, VSPEC, VSPEC, VSPEC, vec, vec, vec, row,
                  per_chunk(1)],
        out_specs=(pl.BlockSpec((tm, 3 * D), lambda m: (nt - 1 - m, 0)), row, row, per_chunk(DK), row, row, row, row,
                   row, row, pl.BlockSpec((8, D), lambda m: (0, 0))),
        scratch_shapes=[pltpu.VMEM((DK, D), F32)],
        compiler_params=_params("arbitrary"),
    )(x, p, p, p, qe, intra, st_prev, act, wa, wb, wo, hg, g1, post, dx1, dec)


def conv_backward(p, yc, dact, dw, ln_g, ln_b, blocks, dp):
    t = p.shape[0]
    tm = _conv_tile(t)
    per = tm // HALO
    nt = t // tm
    n = len(blocks)

    def body(*refs):
        cv_ref, cg_ref, cvp_ref, cgp_ref, yc_ref, ycn_ref, da_ref, dan_ref, dw_ref, g_ref, b_ref = refs[:11]
        dp_ref, acc_ref, ddw_ref = refs[12 + n:15 + n]
        uext, dyext, ush, dysh, ddw8, du_sc = refs[15 + 2 * n:21 + 2 * n]
        start, finish = _exchange_ops(refs[11:11 + n], refs[15 + n:15 + 2 * n], *refs[21 + 2 * n:])
        m = pl.program_id(0)

        @pl.when(m == 0)
        def _():
            start()
            acc_ref[...] = jnp.zeros_like(acc_ref)
            ddw8[...] = jnp.zeros_like(ddw8)

        cv, cg = cv_ref[...], cg_ref[...]
        u, vjp_u = jax.vjp(f_glu, cv, cg)
        uext[0:HALO, :] = jnp.where(m == 0, 0.0, f_glu(cvp_ref[...], cgp_ref[...]))
        uext[HALO:HALO + tm, :] = u
        _shift_stack(uext, ush, tm + HALO)
        _, vjp_c = jax.vjp(f_conv_act, yc_ref[...], g_ref[...], b_ref[...])
        dyc, dg, db = vjp_c(da_ref[...])
        _, vjp_n = jax.vjp(f_conv_act, ycn_ref[...], g_ref[...], b_ref[...])
        dyn = vjp_n(dan_ref[...])[0]
        dyext[0:tm, :] = dyc
        dyext[tm:tm + HALO, :] = jnp.where(m == nt - 1, 0.0, dyn)
        _shift_stack(dyext, dysh, tm + HALO)
        rb = min(128, tm)
        for lt in range(D // LANE):
            ls = slice(lt * LANE, (lt + 1) * LANE)
            for r0 in range(0, tm, rb):
                du_l = jnp.zeros((rb, LANE), F32)
                for w in range(KW):
                    du_l = du_l + dw_ref[w:w + 1, ls] * _shifted(dysh, KW - 1 - w + r0, rb, ls)
                du_sc[r0:r0 + rb, ls] = du_l
                dyc_l = dyext[r0:r0 + rb, ls]
                for w in range(KW):
                    prod = dyc_l * _shifted(ush, HALO - KW + 1 + w + r0, rb, ls)
                    part = jnp.sum(prod.reshape(4, rb // 32, 8, LANE), axis=1)
                    ddw8[w, :, ls] += (part[0] + part[1]) + (part[2] + part[3])
        du = du_sc[...]

        @pl.when(m == nt - 1)
        def _():
            ddw_ref[...] = jnp.sum(ddw8[...], axis=1)

        dcv, dcg = vjp_u(du)
        dp_ref[:, 0:D] = dcv.astype(BF16)
        dp_ref[:, D:2 * D] = dcg.astype(BF16)
        acc_ref[0:1, :] += jnp.sum(dyc, axis=0, keepdims=True)
        acc_ref[1:2, :] += dg
        acc_ref[2:3, :] += db
        acc_ref[3:4, :] += jnp.sum(dcv, axis=0, keepdims=True)
        acc_ref[4:5, :] += jnp.sum(dcg, axis=0, keepdims=True)
        pl.when(m == nt - 1)(finish)

    vec = pl.BlockSpec((1, D), lambda m: (0, 0))
    row = pl.BlockSpec((tm, D), lambda m: (m, 0))
    prev = lambda j: pl.BlockSpec((HALO, D), lambda m: (jnp.maximum(m * per - 1, 0), j))
    nxt = pl.BlockSpec((HALO, D), lambda m: (jnp.minimum((m + 1) * per, t // HALO - 1), 0))
    res = pl.pallas_call(
        body, name="conv_backward", grid=(nt,),
        out_shape=(jax.ShapeDtypeStruct(dp.shape, dp.dtype), jax.ShapeDtypeStruct((8, D), F32),
                   jax.ShapeDtypeStruct((HALO, D), F32)) + _exchanged(blocks),
        in_specs=[pl.BlockSpec((tm, D), lambda m: (m, 4)), pl.BlockSpec((tm, D), lambda m: (m, 5)), prev(4), prev(5),
                  row, nxt, row, nxt, pl.BlockSpec((HALO, D), lambda m: (0, 0)), vec, vec] + [ANYSPEC] * (n + 1),
        out_specs=(pl.BlockSpec((tm, 2 * D), lambda m: (m, 3)), pl.BlockSpec((8, D), lambda m: (0, 0)),
                   pl.BlockSpec((HALO, D), lambda m: (0, 0))) + (ANYSPEC,) * n,
        scratch_shapes=[pltpu.VMEM((HALO + tm, D), F32), pltpu.VMEM((tm + HALO, D), F32),
                        pltpu.VMEM((8, tm + HALO, D), F32), pltpu.VMEM((8, tm + HALO, D), F32),
                        pltpu.VMEM((HALO, 8, D), F32), pltpu.VMEM((tm, D), F32)] + _comm_sems(n, False),
        input_output_aliases={11 + n: 0},
        compiler_params=_params("arbitrary"),
    )(p, p, p, p, yc, yc, dact, dact, dw, ln_g, ln_b, *blocks, dp)
    return res[:3] + (res[3:],)


def hgrn_backward(p, logits, do, dqe, gt, st_prev, blocks, dp):
    t = p.shape[0]
    nc = t // CH
    n = len(blocks)

    def body(*refs):
        q_ref, f_ref, v_ref, l_ref, do_ref, dqe_ref, gt_ref, st_ref = refs[:8]
        dp_ref, acc_ref = refs[9 + n:11 + n]
        start, finish = _exchange_ops(refs[8:8 + n], refs[11 + n:11 + 2 * n], *refs[11 + 2 * n:])

        @pl.when(pl.program_id(0) == 0)
        def _():
            start()
            acc_ref[...] = jnp.zeros_like(acc_ref)

        gt_v = gt_ref[...]
        ddec = jnp.sum(gt_v * st_ref[...], axis=0, keepdims=True)
        _, vjp = jax.vjp(f_hgrn_chunk, q_ref[...], f_ref[...], v_ref[...], l_ref[...])
        dq, df, dv, dl = vjp((do_ref[...], dqe_ref[...], gt_v, ddec))
        dp_ref[:, 0:D] = dq.astype(BF16)
        dp_ref[:, D:2 * D] = df.astype(BF16)
        dp_ref[:, 2 * D:3 * D] = dv.astype(BF16)
        acc_ref[0:2, :] += dl
        acc_ref[2:3, :] += jnp.sum(dq, axis=0, keepdims=True)
        acc_ref[3:4, :] += jnp.sum(df, axis=0, keepdims=True)
        acc_ref[4:5, :] += jnp.sum(dv, axis=0, keepdims=True)
        pl.when(pl.program_id(0) == nc - 1)(finish)

    col = lambda j: pl.BlockSpec((CH, D), lambda c: (c, j))
    row = pl.BlockSpec((CH, D), lambda c: (c, 0))
    stspec = pl.BlockSpec((None, DK, D), lambda c: (c, 0, 0))
    res = pl.pallas_call(
        body, name="hgrn_backward", grid=(nc,),
        out_shape=(jax.ShapeDtypeStruct(dp.shape, dp.dtype), jax.ShapeDtypeStruct((8, D), F32)) + _exchanged(blocks),
        in_specs=[col(0), col(1), col(2), pl.BlockSpec((2, D), lambda c: (0, 0)), row, row, stspec, stspec]
        + [ANYSPEC] * (n + 1),
        out_specs=(pl.BlockSpec((CH, 3 * D), lambda c: (c, 1)), pl.BlockSpec((8, D), lambda c: (0, 0)))
        + (ANYSPEC,) * n,
        scratch_shapes=_comm_sems(n, False),
        input_output_aliases={8 + n: 0},
        compiler_params=_params("arbitrary"),
    )(p, p, p, logits, do, dqe, gt, st_prev, *blocks, dp)
    return res[:2] + (res[2:],)


def in_proj_backward(dp, w_all, x, dx1, gain, sc, sh, blocks):
    t = x.shape[0]
    tm = min(256, t)
    nt = t // tm
    n = len(blocks)

    def body(*refs):
        dp_ref, w_ref, x_ref, dx1_ref, g_ref, sc_ref, sh_ref = refs[:7]
        gx_ref, acc_ref = refs[7 + n:9 + n]
        start = finish = lambda: None
        if n:
            start, finish = _exchange_ops(refs[7:7 + n], refs[9 + n:9 + 2 * n], *refs[9 + 2 * n:])

        @pl.when(pl.program_id(0) == 0)
        def _():
            start()
            acc_ref[...] = jnp.zeros_like(acc_ref)

        dh = jnp.zeros((tm, D), F32)
        for j in range(NDEV):
            dh = dh + _dot(dp_ref[:, j * D:(j + 1) * D], w_ref[DP_SPLIT[j]], _NT)
        _, vjp_h = jax.vjp(f_modulate, x_ref[...], g_ref[...], sc_ref[...], sh_ref[...])
        dx, dg, dsc, dsh = vjp_h(dh)
        gx_ref[...] = dx1_ref[...] + dx
        acc_ref[0:1, :] += dg
        acc_ref[1:2, :] += dsc
        acc_ref[2:3, :] += dsh
        pl.when(pl.program_id(0) == nt - 1)(finish)

    row = pl.BlockSpec((tm, D), lambda m: (m, 0))
    vec = pl.BlockSpec((1, D), lambda m: (0, 0))
    res = pl.pallas_call(
        body, name="in_proj_backward", grid=(nt,),
        out_shape=(jax.ShapeDtypeStruct((t, D), F32), jax.ShapeDtypeStruct((8, D), F32)) + _exchanged(blocks),
        in_specs=[pl.BlockSpec((tm, NDEV * D), lambda m: (m, 0)), VSPEC, row, row, vec, vec, vec] + [ANYSPEC] * n,
        out_specs=(row, pl.BlockSpec((8, D), lambda m: (0, 0))) + (ANYSPEC,) * n,
        scratch_shapes=_comm_sems(n, False) if n else [],
        compiler_params=_params("arbitrary"),
    )(dp, w_all, x, dx1, gain, sc, sh, *blocks)
    return res[0], res[1], res[2:]


def weight_grad(a, b, nblk, ka, bn, a_blocked, name, b_col=lambda n: n):
    t = a.shape[0]
    tk = min(2048, t)
    nk = t // tk

    def body(a_ref, b_ref, f_ref, h_ref, acc):
        k = pl.program_id(1)

        @pl.when(k == 0)
        def _():
            acc[...] = jnp.zeros_like(acc)

        acc[...] += _dot(a_ref[...], b_ref[...], _TN)

        @pl.when(k == nk - 1)
        def _():
            f_ref[...] = acc[...]
            h_ref[...] = acc[...].astype(BF16)

    a_idx = (lambda n, k: (k, n)) if a_blocked else (lambda n, k: (k, 0))
    b_idx = (lambda n, k: (k, 0)) if a_blocked else (lambda n, k: (k, b_col(n)))
    out = pl.BlockSpec((None, ka, bn), lambda n, k: (n, 0, 0))
    return pl.pallas_call(
        body, name=name, grid=(nblk, nk),
        out_shape=(jax.ShapeDtypeStruct((nblk, ka, bn), F32), jax.ShapeDtypeStruct((nblk, ka, bn), BF16)),
        in_specs=[pl.BlockSpec((tk, ka), a_idx), pl.BlockSpec((tk, bn), b_idx)],
        out_specs=(out, out),
        scratch_shapes=[pltpu.VMEM((ka, bn), F32)],
        compiler_params=_params("parallel", "arbitrary"),
    )(a, b)


def ada_backward(call_t, dmod_cols, w, m, v):
    def body(c_ref, d_ref, w_ref, m_ref, v_ref, g_ref, dl_ref, nm_ref, nv_ref):
        ct, dm = c_ref[...], d_ref[...]
        g = ct[:, 0:1] * dm[0:1, :]
        for r in range(1, NDEV):
            g = g + ct[:, r:r + 1] * dm[r:r + 1, :]
        g_ref[...] = g
        dl_ref[...], nm_ref[...], nv_ref[...] = _adamw(w_ref[...], g, m_ref[...], v_ref[...])

    s = jax.ShapeDtypeStruct(w.shape, F32)
    return pl.pallas_call(
        body, name="ada_backward", out_shape=(s, s, s, s),
        in_specs=[VSPEC] * 5, out_specs=(VSPEC,) * 4, compiler_params=_params(),
    )(call_t, dmod_cols, w, m, v)


def adamw_small(total, ddw_mine, recipes, ws, ms, vs):
    n = len(ws)

    def body(*refs):
        tot, ddw = refs[0], refs[1]
        w_refs, m_refs, v_refs = refs[2:2 + n], refs[2 + n:2 + 2 * n], refs[2 + 2 * n:2 + 3 * n]
        outs = refs[2 + 3 * n:2 + 7 * n]
        loss_ref = refs[2 + 7 * n]
        for i, rec in enumerate(recipes):
            if rec == "dw":
                g = ddw[...]
            elif isinstance(rec, tuple):
                g = tot[rec[0]:rec[1], :]
            else:
                g = jnp.concatenate([tot[r:r + 1, :] for r in rec], axis=1) if len(rec) > 1 else tot[rec[0]:rec[0] + 1, :]
            dl, nm, nv = _adamw(w_refs[i][...], g, m_refs[i][...], v_refs[i][...])
            outs[4 * i][...] = g
            outs[4 * i + 1][...] = dl
            outs[4 * i + 2][...] = nm
            outs[4 * i + 3][...] = nv
        loss_ref[...] = tot[LOSS_ROW:LOSS_ROW + 1, 0:128]

    shapes = []
    for w in ws:
        shapes += [jax.ShapeDtypeStruct(w.shape, F32)] * 4
    res = pl.pallas_call(
        body, name="adamw_small", out_shape=tuple(shapes) + (jax.ShapeDtypeStruct((1, 128), F32),),
        in_specs=[VSPEC] * (2 + 3 * n), out_specs=(VSPEC,) * (4 * n + 1), compiler_params=_params(),
    )(total, ddw_mine, *ws, *ms, *vs)
    return [res[4 * i:4 * i + 4] for i in range(n)], res[4 * n]


def reduce_and_adamw(me, w, g_all, g_recv, m, v, name):
    r, c = w.shape
    br = min(128, r)

    def body(me_ref, w_ref, go_ref, gr_ref, m_ref, v_ref, g_ref, dl_ref, nm_ref, nv_ref):
        g = go_ref[...]
        for k in range(NDEV - 1):
            g = g + gr_ref[k].astype(F32)
        g_ref[...] = g
        dl_ref[...], nm_ref[...], nv_ref[...] = _adamw(w_ref[...], g, m_ref[...], v_ref[...])

    blk = pl.BlockSpec((br, c), lambda i, me_ref: (i, 0))
    s = jax.ShapeDtypeStruct(w.shape, F32)
    return pl.pallas_call(
        body, name=name, out_shape=(s, s, s, s),
        grid_spec=pltpu.PrefetchScalarGridSpec(
            num_scalar_prefetch=1, grid=(r // br,),
            in_specs=[blk, pl.BlockSpec((None, br, c), lambda i, me_ref: (me_ref[0], i, 0)),
                      pl.BlockSpec((NDEV - 1, br, c), lambda i, me_ref: (0, i, 0)), blk, blk],
            out_specs=(blk, blk, blk, blk)),
        compiler_params=_params("parallel"),
    )(me, w, g_all, g_recv, m, v)


def kernel(x, c, w_ada, b_ada, pre_norm_tm, post_norm_tm, pre_norm_cm, post_norm_cm, w_in, b_in, hg_lb_logits, hg_norm, conv_dw, conv_db, conv_ln_g, conv_ln_b, w_br_a, w_br_b, w_out, w_ff1, w_ff2, loss_target, m_w_ada, m_b_ada, m_pre_norm_tm, m_post_norm_tm, m_pre_norm_cm, m_post_norm_cm, m_w_in, m_b_in, m_hg_lb_logits, m_hg_norm, m_conv_dw, m_conv_db, m_conv_ln_g, m_conv_ln_b, m_w_br_a, m_w_br_b, m_w_out, m_w_ff1, m_w_ff2, v_w_ada, v_b_ada, v_pre_norm_tm, v_post_norm_tm, v_pre_norm_cm, v_post_norm_cm, v_w_in, v_b_in, v_hg_lb_logits, v_hg_norm, v_conv_dw, v_conv_db, v_conv_ln_g, v_conv_ln_b, v_w_br_a, v_w_br_b, v_w_out, v_w_ff1, v_w_ff2):
    t = x.shape[1]
    me = 4 * lax.axis_index("x") + 2 * lax.axis_index("y") + lax.axis_index("c")
    xs = x[0]
    tgt = loss_target[0]

    mod, call, dw_all = ada_forward(c, w_ada[0], b_ada, conv_dw[0])
    sh1, sc1, g1, sh2, sc2, g2 = [mod[:, i * D:(i + 1) * D] for i in range(6)]
    dw = jnp.pad(dw_all.transpose(1, 0, 2).reshape(KW, D), ((0, HALO - KW), (0, 0)))

    p, h, win_all = in_proj(xs, pre_norm_tm, sc1, sh1, w_in[0].astype(BF16), b_in)
    yc, act, (w1_all,) = conv_forward(p, dw, conv_db, conv_ln_g, conv_ln_b, [w_ff1[0].astype(BF16)])
    intra, qe, st_prev, dec, (w2_all, wa_all, wb_all, wo_all) = hgrn_local(
        p, hg_lb_logits, [w_ff2[0].astype(BF16), w_br_a[0].astype(BF16), w_br_b[0].astype(BF16),
                          w_out[0].astype(BF16)])
    wa, wb, wo = wa_all.reshape(D, D), wb_all.reshape(D, D), wo_all.reshape(D, D)
    x1 = mix_forward(xs, p, qe, intra, st_prev, act, wa, wb, wo, hg_norm, g1, post_norm_tm)

    dx1, h2, r, dz, dy2, acc_ffn = ffn_forward_backward(x1, tgt, w1_all, w2_all, pre_norm_cm, sc2, sh2, g2,
                                                        post_norm_cm)
    rows = D // NDEV
    g1_f, g1_h = weight_grad(h2, dz, NDEV, D, DFF // NDEV, False, "grad_w_ff1")
    g2_f, g2_h = weight_grad(r, dy2, 4, D, D, True, "grad_w_ff2")
    g2_f, g2_h = g2_f.reshape(NDEV, DFF // NDEV, D), g2_h.reshape(NDEV, DFF // NDEV, D)

    (dp, do, dqe, gt, dact, a16, mg16, dya, dyb, dy, acc_mix) = mix_backward(
        xs, p, qe, intra, st_prev, act, wa, wb, wo, hg_norm, g1, post_norm_tm, dx1, dec)
    ga_f, ga_h = weight_grad(a16, dya, 1, D, D, False, "grad_w_br_a")
    gb_f, gb_h = weight_grad(act, dyb, 1, D, D, False, "grad_w_br_b")
    go_f, go_h = weight_grad(mg16, dy, 1, D, D, False, "grad_w_out")
    dp, acc_conv, ddw, (r_ff1, r_ff2) = conv_backward(p, yc, dact, dw, conv_ln_g, conv_ln_b, [g1_h, g2_h], dp)
    dp, acc_hg, (r_a, r_b, r_o) = hgrn_backward(
        p, hg_lb_logits, do, dqe, gt, st_prev,
        [ga_h.reshape(NDEV, rows, D), gb_h.reshape(NDEV, rows, D), go_h.reshape(NDEV, rows, D)], dp)
    gin_f, gin_h = weight_grad(
        h, dp, NDEV, D, D, False, "grad_w_in",
        b_col=lambda n: jnp.where(n < 3, n + 3, jnp.where(n == 3, 0, jnp.where(n < 6, n + 2, n - 5))))
    in_send, in_recv, gin_thru, landing, token = exchange_start(gin_h)
    grad_x, acc_in, _ = in_proj_backward(dp, win_all, xs, dx1, pre_norm_tm, sc1, sh1 + token[0:1, 0:1], [])

    own = [ga_f.reshape(NDEV, rows, D), gb_f.reshape(NDEV, rows, D), go_f.reshape(NDEV, rows, D), g1_f, g2_f]
    recv = [r_a, r_b, r_o, r_ff1, r_ff2]
    big = {}
    names = ["w_br_a", "w_br_b", "w_out", "w_ff1", "w_ff2"]
    ws = [w_br_a, w_br_b, w_out, w_ff1, w_ff2]
    ms = [m_w_br_a, m_w_br_b, m_w_out, m_w_ff1, m_w_ff2]
    vs = [v_w_br_a, v_w_br_b, v_w_out, v_w_ff1, v_w_ff2]
    me1 = me.astype(jnp.int32).reshape(1)
    for i, nm in enumerate(names):
        big[nm] = [o[None] for o in reduce_and_adamw(me1, ws[i][0], own[i], recv[i], ms[i][0], vs[i][0], "adamw_" + nm)]
    big, grad_x, acc_in = lax.optimization_barrier((big, grad_x, acc_in))
    r_in = exchange_wait(in_send, in_recv, gin_thru, landing, acc_in)
    r_in, acc_in = lax.optimization_barrier((r_in, acc_in))
    big["w_in"] = [o[None] for o in reduce_and_adamw(me1, w_in[0], gin_f, r_in, m_w_in[0], v_w_in[0], "adamw_w_in")]

    dmod_rows = [2, 1, 8, 19, 18, 20]
    total, kept = gather_and_sum_rows([acc_in, acc_mix, acc_ffn, acc_hg, acc_conv, ddw], dmod_rows)
    dmod_all = kept[:, 0:6, :].reshape(NDEV, 6 * D)
    wcols = w_ada.shape[2]
    gwa, dwa, nmwa, nvwa = ada_backward(call.T, lax.dynamic_slice_in_dim(dmod_all, me * wcols, wcols, axis=1),
                                        w_ada[0], m_w_ada[0], v_w_ada[0])
    ddw_mine = lax.dynamic_slice_in_dim(total[40:40 + KW], me * (D // NDEV), D // NDEV, axis=1)

    small_names = ["b_ada", "pre_norm_tm", "post_norm_tm", "pre_norm_cm", "post_norm_cm", "b_in", "hg_lb_logits",
                   "hg_norm", "conv_db", "conv_ln_g", "conv_ln_b", "conv_dw"]
    recipes = [dmod_rows, [0], [9], [16], [17], [26, 27, 28, 11, 35, 36, 12, 13], (24, 26), [10], [32], [33], [34],
               "dw"]
    small_w = [b_ada, pre_norm_tm, post_norm_tm, pre_norm_cm, post_norm_cm, b_in, hg_lb_logits, hg_norm, conv_db,
               conv_ln_g, conv_ln_b, conv_dw[0]]
    small_m = [m_b_ada, m_pre_norm_tm, m_post_norm_tm, m_pre_norm_cm, m_post_norm_cm, m_b_in, m_hg_lb_logits,
               m_hg_norm, m_conv_db, m_conv_ln_g, m_conv_ln_b, m_conv_dw[0]]
    small_v = [v_b_ada, v_pre_norm_tm, v_post_norm_tm, v_pre_norm_cm, v_post_norm_cm, v_b_in, v_hg_lb_logits,
               v_hg_norm, v_conv_db, v_conv_ln_g, v_conv_ln_b, v_conv_dw[0]]
    small_out, loss_row = adamw_small(total, ddw_mine, recipes, small_w, small_m, small_v)
    loss = loss_row[0, 0]
    sm = {nm: list(o) for nm, o in zip(small_names, small_out)}
    sm["conv_dw"] = [o[None] for o in sm["conv_dw"]]

    order = ["w_ada", "b_ada", "pre_norm_tm", "post_norm_tm", "pre_norm_cm", "post_norm_cm", "w_in", "b_in",
             "hg_lb_logits", "hg_norm", "conv_dw", "conv_db", "conv_ln_g", "conv_ln_b", "w_br_a", "w_br_b", "w_out",
             "w_ff1", "w_ff2"]
    res = dict(sm)
    res.update(big)
    res["w_ada"] = [gwa[None], dwa[None], nmwa[None], nvwa[None]]
    outs = [loss, grad_x[None]]
    for j in range(4):
        outs += [res[nm][j] for nm in order]
    return tuple(outs)
```

```python
import functools

import jax
import jax.numpy as jnp
from jax import lax
from jax.experimental import pallas as pl
from jax.experimental.pallas import tpu as pltpu

F32 = jnp.float32
BF16 = jnp.bfloat16
MESH = pl.DeviceIdType.MESH
AXES = ("x", "y", "c")

D = 1024
HEADS = 8
DK = 128
LANE = 128
CH = 128
LEVELS = 7
KW = 31
HALO = 32
DFF = 4096
NDEV = 8
EPS = 1e-6
DP_SPLIT = (3, 6, 7, 0, 1, 2, 4, 5)
LOSS_ROW = 21
ADAM_LR, ADAM_B1, ADAM_B2, ADAM_EPS, ADAM_WD, ADAM_STEP = 0.001, 0.9, 0.999, 1e-08, 0.01, 10
VMEM_LIMIT = 58 * 1024 * 1024

_NN = (((1,), (0,)), ((), ()))
_NT = (((1,), (1,)), ((), ()))
_TN = (((0,), (0,)), ((), ()))

VSPEC = pl.BlockSpec(memory_space=pltpu.VMEM)
ANYSPEC = pl.BlockSpec(memory_space=pl.ANY)


def _params(*sem):
    return pltpu.CompilerParams(dimension_semantics=sem or None, vmem_limit_bytes=VMEM_LIMIT)


def _dot(a, b, dims):
    return lax.dot_general(a, b, dims, preferred_element_type=F32)


@jax.custom_vjp
def mm_nn(a, b):
    return _dot(a.astype(BF16), b.astype(BF16), _NN)


def _mm_nn_fwd(a, b):
    ab, bb = a.astype(BF16), b.astype(BF16)
    return _dot(ab, bb, _NN), (ab, bb)


def _mm_nn_bwd(res, ct):
    ab, bb = res
    cb = ct.astype(BF16)
    return _dot(cb, bb, _NT), _dot(ab, cb, _TN)


mm_nn.defvjp(_mm_nn_fwd, _mm_nn_bwd)


@jax.custom_vjp
def mm_nt(a, b):
    return _dot(a.astype(BF16), b.astype(BF16), _NT)


def _mm_nt_fwd(a, b):
    ab, bb = a.astype(BF16), b.astype(BF16)
    return _dot(ab, bb, _NT), (ab, bb)


def _mm_nt_bwd(res, ct):
    ab, bb = res
    cb = ct.astype(BF16)
    return _dot(cb, bb, _NN), _dot(cb, ab, _TN)


mm_nt.defvjp(_mm_nt_fwd, _mm_nt_bwd)


@jax.custom_vjp
def mm_tn(a, b):
    return _dot(a.astype(BF16), b.astype(BF16), _TN)


def _mm_tn_fwd(a, b):
    ab, bb = a.astype(BF16), b.astype(BF16)
    return _dot(ab, bb, _TN), (ab, bb)


def _mm_tn_bwd(res, ct):
    ab, bb = res
    cb = ct.astype(BF16)
    return _dot(bb, cb, _NT), _dot(ab, cb, _NN)


mm_tn.defvjp(_mm_tn_fwd, _mm_tn_bwd)


def _rms(x):
    return x * lax.rsqrt(jnp.mean(x * x, axis=-1, keepdims=True) + EPS)


_sigmoid = jax.nn.sigmoid


def _silu(x):
    return x * _sigmoid(x)


def f_modulate(x, gain, sc, sh):
    return _rms(x) * gain * (1.0 + sc) + sh


def f_residual(x, y, gate, gain):
    return x + gate * (_rms(y) * gain)


def f_merge(ga, gb, ya, yb):
    return _sigmoid(ga) * ya + _sigmoid(gb) * yb


def f_head_out(o, og, hg):
    heads = [_rms(o[:, h * DK:(h + 1) * DK]) for h in range(HEADS)]
    return jnp.concatenate(heads, axis=1) * hg * _silu(og)


def f_conv_act(u, g, b):
    mu = jnp.mean(u, axis=-1, keepdims=True)
    var = jnp.mean(jnp.square(u - mu), axis=-1, keepdims=True)
    return _silu((u - mu) * lax.rsqrt(var + EPS) * g + b)


def f_glu(cv, cg):
    return cv * _sigmoid(cg)


def _split2(x):
    hi = x.astype(BF16)
    return hi, (x - hi.astype(F32)).astype(BF16)


def _tri(transposed):
    i = lax.broadcasted_iota(jnp.int32, (CH, CH), 1 if transposed else 0)
    t = lax.broadcasted_iota(jnp.int32, (CH, CH), 0 if transposed else 1)
    return jnp.where(t <= i, 1.0, 0.0).astype(BF16)


def _blocks3(x, rows):
    return x.reshape(CH // rows, rows, x.shape[-1])


def _mid_broadcast(b, lev):
    h = 1 << (LEVELS - 1 - lev)
    if h >= 4:
        x3 = _blocks3(b, 2 * h)
        return jnp.broadcast_to(x3[:, h - 1:h, :], x3.shape).reshape(b.shape)
    x3 = _blocks3(b, 8)
    sub = lax.broadcasted_iota(jnp.int32, x3.shape, 1)
    out = None
    for first in range(0, 8, 2 * h):
        piece = jnp.broadcast_to(x3[:, first + h - 1:first + h, :], x3.shape)
        out = piece if out is None else jnp.where(sub >= first, piece, out)
    return out.reshape(b.shape)


def _mid_scatter(d, lev):
    h = 1 << (LEVELS - 1 - lev)
    if h >= 4:
        x3 = _blocks3(d, 2 * h)
        row = lax.broadcasted_iota(jnp.int32, x3.shape, 1)
        total = jnp.sum(x3, axis=1, keepdims=True)
        return jnp.where(row == h - 1, total, 0.0).reshape(d.shape)
    x3 = _blocks3(d, 8)
    sub = lax.broadcasted_iota(jnp.int32, x3.shape, 1)
    out = jnp.zeros_like(x3)
    for first in range(0, 8, 2 * h):
        inside = (sub >= first) & (sub < first + 2 * h)
        total = jnp.sum(jnp.where(inside, x3, 0.0), axis=1, keepdims=True)
        out = jnp.where(sub == first + h - 1, total, out)
    return out.reshape(d.shape)


@jax.custom_vjp
def decay_sums(g):
    hi, lo = _split2(g)
    tri = _tri(False)
    b = _dot(tri, hi, _NN) + _dot(tri, lo, _NN)
    return (b,) + tuple(b - _mid_broadcast(b, lev) for lev in range(LEVELS))


def _decay_sums_fwd(g):
    return decay_sums(g), None


def _decay_sums_bwd(_, cts):
    db = cts[0]
    for lev in range(LEVELS):
        db = db + cts[1 + lev] - _mid_scatter(cts[1 + lev], lev)
    hi, lo = _split2(db)
    tri = _tri(True)
    return (_dot(tri, hi, _NN) + _dot(tri, lo, _NN),)


decay_sums.defvjp(_decay_sums_fwd, _decay_sums_bwd)


def _score_masks():
    i = lax.broadcasted_iota(jnp.int32, (CH, CH), 0)
    j = lax.broadcasted_iota(jnp.int32, (CH, CH), 1)
    masks = [i == j]
    for lev in range(LEVELS):
        sh = LEVELS - 1 - lev
        same = (i >> (sh + 1)) == (j >> (sh + 1))
        masks.append(same & (((i >> sh) & 1) == 1) & (((j >> sh) & 1) == 0))
    return [jnp.where(m, 1.0, 0.0) for m in masks]


def f_hgrn_chunk(q_r, f_r, v, logits):
    l0, l1 = logits[0:1, :], logits[1:2, :]
    mx = lax.stop_gradient(jnp.maximum(l0, l1))
    e0, e1 = jnp.exp(l0 - mx), jnp.exp(l1 - mx)
    lb = e0 / (e0 + e1)
    q = _silu(q_r)
    f = lb + (1.0 - lb) * _sigmoid(f_r)
    k = 1.0 - f
    sums = decay_sums(jnp.log(f))
    b = sums[0]
    btot = b[CH - 1:CH, :]
    qe = q * jnp.exp(b)
    ke = k * jnp.exp(btot - b)
    dec = jnp.exp(btot)
    qs, ks = [q], [k]
    row = lax.broadcasted_iota(jnp.int32, b.shape, 0)
    for lev in range(LEVELS):
        upper = ((row >> (LEVELS - 1 - lev)) & 1) == 1
        e = sums[1 + lev]
        qs.append(q * jnp.exp(jnp.where(upper, e, 0.0)))
        ks.append(k * jnp.exp(jnp.where(upper, 0.0, -e)))
    masks = _score_masks()
    intra, ut = [], []
    for h in range(HEADS):
        sl = slice(h * DK, (h + 1) * DK)
        sc = None
        for lev in range(LEVELS + 1):
            s = mm_nt(qs[lev][:, sl], ks[lev][:, sl]) * masks[lev]
            sc = s if sc is None else sc + s
        intra.append(mm_nn(sc, v[:, sl]))
        ut.append(mm_tn(v[:, sl], ke[:, sl]))
    return jnp.concatenate(intra, axis=1), qe, jnp.concatenate(ut, axis=1), dec


def _inter(qe_b, st_ref, rows):
    out = []
    for ci in range(rows // CH):
        st = st_ref[ci].astype(BF16)
        heads = [_dot(qe_b[ci * CH:(ci + 1) * CH, h * DK:(h + 1) * DK], st[:, h * DK:(h + 1) * DK], _NT)
                 for h in range(HEADS)]
        out.append(jnp.concatenate(heads, axis=1))
    return jnp.concatenate(out, axis=0)


def _shift_stack(src, dst, rows):
    dst[0, 0:rows, :] = src[0:rows, :]
    for b in range(1, 8):
        dst[b, 0:rows - 8, :] = src[pl.ds(b, rows - 8), :]


def _shifted(stack, offset, rows, lanes=slice(None)):
    return stack[offset % 8, pl.ds(8 * (offset // 8), rows), lanes]


def _adamw(w, g, m, v):
    m = ADAM_B1 * m + (1.0 - ADAM_B1) * g
    v = ADAM_B2 * v + (1.0 - ADAM_B2) * jnp.square(g)
    m_hat = m / (1.0 - ADAM_B1 ** ADAM_STEP)
    v_hat = v / (1.0 - ADAM_B2 ** ADAM_STEP)
    delta = -ADAM_LR * (m_hat / (jnp.sqrt(v_hat) + ADAM_EPS) + ADAM_WD * w)
    return delta, m, v


def _me():
    return lax.axis_index("x"), lax.axis_index("y"), lax.axis_index("c")


def _peer(k):
    x, y, c = _me()
    mask = k + 1
    px = (1 - x) if (mask >> 2) & 1 else x
    py = (1 - y) if (mask >> 1) & 1 else y
    pc = (1 - c) if mask & 1 else c
    return (px, py, pc), 4 * px + 2 * py + pc


def ada_forward(c, w_ada, b_ada, dw):
    wcols = w_ada.shape[1]

    def body(c_ref, w_ref, b_ref, dw_ref, mod_ref, call_ref, dwall_ref, part_ref, modp_ref, send_sems, recv_sems):
        x, y, cc = _me()
        me = 4 * x + 2 * y + cc
        call_ref[me] = _silu(c_ref[...])
        dwall_ref[me] = dw_ref[...]
        sends = []
        for k in range(NDEV - 1):
            dev, pidx = _peer(k)
            cp = pltpu.make_async_remote_copy(dwall_ref.at[me], dwall_ref.at[me], send_sems.at[2 * (NDEV - 1) + k],
                                              recv_sems.at[2 * (NDEV - 1) + k], device_id=dev, device_id_type=MESH)
            cp.start()
            sends.append(cp)
        for k in range(NDEV - 1):
            dev, _ = _peer(k)
            cp = pltpu.make_async_remote_copy(call_ref.at[me], call_ref.at[me], send_sems.at[k], recv_sems.at[k],
                                              device_id=dev, device_id_type=MESH)
            cp.start()
            sends.append(cp)
        for k in range(NDEV - 1):
            _, pidx = _peer(k)
            pltpu.make_async_remote_copy(call_ref.at[pidx], call_ref.at[pidx], send_sems.at[k], recv_sems.at[k],
                                         device_id=_peer(k)[0], device_id_type=MESH).wait_recv()
        call = jnp.concatenate([call_ref[r] for r in range(NDEV)], axis=0)
        part = _dot(call.astype(BF16), w_ref[...].astype(BF16), _NN)
        for r in range(NDEV):
            part_ref[r] = part[r:r + 1, :]
        modp_ref[me] = part_ref[me]
        for k in range(NDEV - 1):
            dev, pidx = _peer(k)
            cp = pltpu.make_async_remote_copy(part_ref.at[pidx], modp_ref.at[me], send_sems.at[NDEV - 1 + k],
                                              recv_sems.at[NDEV - 1 + k], device_id=dev, device_id_type=MESH)
            cp.start()
            sends.append(cp)
        for k in range(NDEV - 1):
            dev, pidx = _peer(k)
            pltpu.make_async_remote_copy(part_ref.at[pidx], modp_ref.at[pidx], send_sems.at[NDEV - 1 + k],
                                         recv_sems.at[NDEV - 1 + k], device_id=dev, device_id_type=MESH).wait_recv()
        for k in range(NDEV - 1):
            dev, pidx = _peer(k)
            pltpu.make_async_remote_copy(dwall_ref.at[pidx], dwall_ref.at[pidx], send_sems.at[2 * (NDEV - 1) + k],
                                         recv_sems.at[2 * (NDEV - 1) + k], device_id=dev,
                                         device_id_type=MESH).wait_recv()
        for cp in sends:
            cp.wait_send()
        mod_ref[...] = jnp.concatenate([modp_ref[r] for r in range(NDEV)], axis=1) + b_ref[...]

    mod, call, dw_all = pl.pallas_call(
        body, name="ada_forward",
        out_shape=(jax.ShapeDtypeStruct((1, NDEV * wcols), F32), jax.ShapeDtypeStruct((NDEV, 1, D), F32),
                   jax.ShapeDtypeStruct((NDEV,) + dw.shape, F32)),
        in_specs=[VSPEC] * 4, out_specs=(VSPEC,) * 3,
        scratch_shapes=[pltpu.VMEM((NDEV, 1, wcols), F32), pltpu.VMEM((NDEV, 1, wcols), F32),
                        pltpu.SemaphoreType.DMA((3 * (NDEV - 1),)), pltpu.SemaphoreType.DMA((3 * (NDEV - 1),))],
        compiler_params=_params(),
    )(c, w_ada, b_ada, dw)
    return mod, call.reshape(NDEV, D), dw_all


def _comm_sems(n, local):
    sems = [pltpu.SemaphoreType.DMA((7 * n,)), pltpu.SemaphoreType.DMA((7 * n,))]
    return sems + ([pltpu.SemaphoreType.DMA((n,))] if local else [])


def _gather2_ops(ins, outs, send_sems, recv_sems, local_sems):
    n = len(ins)
    x, y, c = _me()
    me, sibling = (x, y, c), (x, y, 1 - c)
    chips = [(1 - x, y), (x, 1 - y), (1 - x, 1 - y)]

    def slot(p):
        return 4 * p[0] + 2 * p[1] + p[2]

    def copy(a, k, block, to, src=None):
        return pltpu.make_async_remote_copy(
            src_ref=outs[a].at[slot(block)] if src is None else src, dst_ref=outs[a].at[slot(block)],
            send_sem=send_sems.at[a * 7 + k], recv_sem=recv_sems.at[a * 7 + k], device_id=to, device_id_type=MESH)

    def local(a):
        return pltpu.make_async_copy(ins[a], outs[a].at[slot(me)], local_sems.at[a])

    def first(a):
        return [copy(a, 0, me, sibling, src=ins[a])] + [copy(a, 1 + j, me, (*chip, c), src=ins[a])
                                                        for j, chip in enumerate(chips)]

    def passed(a):
        return [copy(a, 4 + j, (*chip, c), sibling) for j, chip in enumerate(chips)]

    def start():
        for a in range(n):
            local(a).start()
            for cp in first(a):
                cp.start()

    def forward():
        for j, chip in enumerate(chips):
            for a in range(n):
                copy(a, 1 + j, (*chip, c), me).wait_recv()
                passed(a)[j].start()

    def finish():
        for a in range(n):
            copy(a, 0, sibling, me).wait_recv()
            for j, chip in enumerate(chips):
                copy(a, 4 + j, (*chip, 1 - c), me).wait_recv()
        for a in range(n):
            for cp in first(a) + passed(a):
                cp.wait_send()
            local(a).wait()

    return start, forward, finish


def _exchange_ops(ins, outs, send_sems, recv_sems):
    n = len(ins)

    def copy(a, k):
        dev, pidx = _peer(k)
        return pltpu.make_async_remote_copy(ins[a].at[pidx], outs[a].at[k], send_sems.at[a * 7 + k],
                                            recv_sems.at[a * 7 + k], device_id=dev, device_id_type=MESH)

    def start():
        for k in range(NDEV - 1):
            for a in range(n):
                copy(a, k).start()

    def finish():
        for k in range(NDEV - 1):
            for a in range(n):
                copy(a, k).wait_recv()
        for k in range(NDEV - 1):
            for a in range(n):
                copy(a, k).wait_send()

    return start, finish


def _gathered(shards):
    return tuple(jax.ShapeDtypeStruct((NDEV,) + s.shape, s.dtype) for s in shards)


def _exchanged(blocks):
    return tuple(jax.ShapeDtypeStruct((NDEV - 1,) + b.shape[1:], b.dtype) for b in blocks)


def exchange_start(blocks):
    landing = lax.empty((NDEV - 1,) + blocks.shape[1:], blocks.dtype)
    hbm = pl.BlockSpec(memory_space=pltpu.HBM)
    sem = pl.BlockSpec(memory_space=pltpu.SEMAPHORE)

    def body(src_ref, land_ref, send_sems, recv_sems, src_thru, land_thru, token):
        for k in range(NDEV - 1):
            dev, pidx = _peer(k)
            pltpu.make_async_remote_copy(src_ref.at[pidx], land_ref.at[k], send_sems.at[k], recv_sems.at[k],
                                         device_id=dev, device_id_type=MESH).start()
        token[...] = jnp.zeros_like(token)

    return pl.pallas_call(
        body, name="exchange_start",
        out_shape=(pltpu.SemaphoreType.DMA((NDEV - 1,)), pltpu.SemaphoreType.DMA((NDEV - 1,)),
                   pltpu.HBM(blocks.shape, blocks.dtype), pltpu.HBM(landing.shape, landing.dtype),
                   jax.ShapeDtypeStruct((8, 128), F32)),
        in_specs=(hbm, hbm), out_specs=(sem, sem, hbm, hbm, VSPEC), input_output_aliases={0: 2, 1: 3},
        compiler_params=pltpu.CompilerParams(has_side_effects=pltpu.SideEffectType.DATAFLOW_SIDE_EFFECTING),
    )(pltpu.with_memory_space_constraint(blocks, pltpu.HBM), pltpu.with_memory_space_constraint(landing, pltpu.HBM))


def exchange_wait(send_sems, recv_sems, src_thru, land_thru, after):
    hbm = pl.BlockSpec(memory_space=pltpu.HBM)
    sem = pl.BlockSpec(memory_space=pltpu.SEMAPHORE)

    def body(src_ref, land_ref, send_sems, recv_sems, after_ref, src_dead, got_ref):
        for k in range(NDEV - 1):
            dev, pidx = _peer(k)
            cp = pltpu.make_async_remote_copy(src_ref.at[pidx], land_ref.at[k], send_sems.at[k], recv_sems.at[k],
                                              device_id=dev, device_id_type=MESH)
            cp.wait_send()
            cp.wait_recv()

    return pl.pallas_call(
        body, name="exchange_wait",
        out_shape=(pltpu.HBM(src_thru.shape, src_thru.dtype), pltpu.HBM(land_thru.shape, land_thru.dtype)),
        in_specs=(hbm, hbm, sem, sem, ANYSPEC), out_specs=(hbm, hbm), input_output_aliases={0: 0, 1: 1},
        compiler_params=pltpu.CompilerParams(has_side_effects=pltpu.SideEffectType.DATAFLOW_SIDE_EFFECTING),
    )(src_thru, land_thru, send_sems, recv_sems, after)[1]


def gather_and_sum_rows(parts, keep_rows):
    n = len(parts)
    offs = [sum(p.shape[0] for p in parts[:i]) for i in range(n)]
    rows = sum(p.shape[0] for p in parts)

    def body(*refs):
        sum_ref, keep_ref, buf_ref, send_sems, recv_sems = refs[n:]
        x, y, c = _me()
        me = 4 * x + 2 * y + c
        for i in range(n):
            buf_ref[me, offs[i]:offs[i] + parts[i].shape[0], :] = refs[i][...]

        def copy(k, block):
            return pltpu.make_async_remote_copy(buf_ref.at[block], buf_ref.at[block], send_sems.at[k],
                                                recv_sems.at[k], device_id=_peer(k)[0], device_id_type=MESH)

        for k in range(NDEV - 1):
            copy(k, me).start()
        for k in range(NDEV - 1):
            copy(k, _peer(k)[1]).wait_recv()
        for k in range(NDEV - 1):
            copy(k, me).wait_send()
        total = buf_ref[0]
        for r in range(1, NDEV):
            total = total + buf_ref[r]
        sum_ref[...] = total
        keep_ref[...] = jnp.zeros_like(keep_ref)
        for r in range(NDEV):
            for j, src in enumerate(keep_rows):
                keep_ref[r, j:j + 1, :] = buf_ref[r, src:src + 1, :]

    return pl.pallas_call(
        body, name="gather_and_sum_rows",
        out_shape=(jax.ShapeDtypeStruct((rows, D), F32), jax.ShapeDtypeStruct((NDEV, 8, D), F32)),
        in_specs=[VSPEC] * n, out_specs=(VSPEC, VSPEC),
        scratch_shapes=[pltpu.VMEM((NDEV, rows, D), F32), pltpu.SemaphoreType.DMA((NDEV - 1,)),
                        pltpu.SemaphoreType.DMA((NDEV - 1,))],
        compiler_params=_params(),
    )(*parts)


def _gather_chips():
    x, y, c = _me()
    north = c == 1
    first = (jnp.where(north, 1 - x, x), jnp.where(north, y, 1 - y))
    second = (jnp.where(north, x, 1 - x), jnp.where(north, 1 - y, y))
    return [first, second, (1 - x, 1 - y)]


def _arrival_order():
    x, y, c = _me()
    first, second, far = _gather_chips()
    devs = [(x, y, c), (x, y, 1 - c), (*first, c), (*second, 1 - c), (*second, c), (*first, 1 - c), (*far, c),
            (*far, 1 - c)]
    return jnp.stack([4 * d[0] + 2 * d[1] + d[2] for d in devs]).astype(jnp.int32)


def in_proj(x, gain, sc, sh, w_shard, b_in):
    t = x.shape[0]
    tm = min(1024, t // 2)
    nm = t // tm
    order = _arrival_order()

    def body(order_ref, x_ref, g_ref, sc_ref, sh_ref, b_ref, wsh_ref, p_ref, h_ref, wall_ref,
             wbuf, h_all, send_sems, recv_sems, local_sems):
        k, m = pl.program_id(0), pl.program_id(1)
        mx, my, mc = _me()
        me, sibling = (mx, my, mc), (mx, my, 1 - mc)
        chips = _gather_chips()

        def slot(d):
            return 4 * d[0] + 2 * d[1] + d[2]

        def copy(sem, block, to, src=None):
            return pltpu.make_async_remote_copy(
                src_ref=wall_ref.at[slot(block)] if src is None else src, dst_ref=wall_ref.at[slot(block)],
                send_sem=send_sems.at[sem], recv_sem=recv_sems.at[sem], device_id=to, device_id_type=MESH)

        own = pltpu.make_async_copy(wsh_ref, wall_ref.at[slot(me)], local_sems.at[0])
        first = [copy(0, me, sibling, src=wsh_ref)] + [copy(1 + j, me, (*q, mc), src=wsh_ref)
                                                       for j, q in enumerate(chips)]
        passed = [copy(4 + j, (*q, mc), sibling) for j, q in enumerate(chips)]

        def load(kk):
            src = wsh_ref if kk == 0 else wall_ref.at[order_ref[kk]]
            return pltpu.make_async_copy(src, wbuf.at[kk % 2], local_sems.at[1 + kk % 2])

        def arrived(kk):
            if kk == 1:
                copy(0, sibling, me).wait_recv()
            elif kk in (2, 4, 6):
                j = kk // 2 - 1
                copy(1 + j, (*chips[j], mc), me).wait_recv()
                passed[j].start()
                if j < 2:
                    first[2 + j].start()
            else:
                j = {3: 1, 5: 0, 7: 2}[kk]
                copy(4 + {1: 0, 0: 1, 2: 2}[j], (*chips[j], 1 - mc), me).wait_recv()

        @pl.when((k == 0) & (m == 0))
        def _():
            own.start()
            for cp in first[:2]:
                cp.start()
            load(0).start()
            load(0).wait()

        for kk in range(1, NDEV):
            @pl.when((k == kk - 1) & (m == nm // 2))
            def _(kk=kk):
                arrived(kk)
                load(kk).start()

            @pl.when((k == kk) & (m == 0))
            def _(kk=kk):
                load(kk).wait()

        rows = pl.ds(pl.multiple_of(m * tm, tm), tm)

        @pl.when(k == 0)
        def _():
            h_all[rows, :] = f_modulate(x_ref[...], g_ref[...], sc_ref[...], sh_ref[...]).astype(BF16)

        p_ref[...] = _dot(h_all[rows, :], wbuf[k % 2], _NN) + b_ref[...]

        @pl.when((k == NDEV - 1) & (m == nm - 1))
        def _():
            for cp in first + passed:
                cp.wait_send()
            own.wait()
            out = pltpu.make_async_copy(h_all, h_ref, local_sems.at[0])
            out.start()
            out.wait()

    vec = pl.BlockSpec((1, D), lambda k, m, o: (0, 0))
    return pl.pallas_call(
        body, name="in_proj",
        out_shape=(jax.ShapeDtypeStruct((t, NDEV * D), F32), jax.ShapeDtypeStruct((t, D), BF16),
                   jax.ShapeDtypeStruct((NDEV, D, D), BF16)),
        grid_spec=pltpu.PrefetchScalarGridSpec(
            num_scalar_prefetch=1, grid=(NDEV, nm),
            in_specs=[pl.BlockSpec((tm, D), lambda k, m, o: (jnp.where(k == 0, m, 0), 0)), vec, vec, vec,
                      pl.BlockSpec((1, D), lambda k, m, o: (0, o[k])), ANYSPEC],
            out_specs=(pl.BlockSpec((tm, D), lambda k, m, o: (m, o[k])), ANYSPEC, ANYSPEC),
            scratch_shapes=[pltpu.VMEM((2, D, D), BF16), pltpu.VMEM((t, D), BF16), pltpu.SemaphoreType.DMA((7,)),
                            pltpu.SemaphoreType.DMA((7,)), pltpu.SemaphoreType.DMA((3,))]),
        compiler_params=_params("arbitrary", "arbitrary"),
    )(order, x, gain, sc, sh, b_in, w_shard)


def hgrn_local(p, logits, shards):
    t = p.shape[0]
    nc = t // CH
    n = len(shards)

    def body(*refs):
        q_ref, f_ref, v_ref, l_ref = refs[:4]
        intra_ref, qe_ref, st_ref, dec_ref = refs[4 + n:8 + n]
        state = refs[8 + 2 * n]
        start, forward, finish = _gather2_ops(refs[4:4 + n], refs[8 + n:8 + 2 * n], *refs[9 + 2 * n:])
        step = pl.program_id(0)

        @pl.when(step == 0)
        def _():
            start()
            state[...] = jnp.zeros_like(state)

        intra, qe, ut, dec = f_hgrn_chunk(q_ref[...], f_ref[...], v_ref[...], l_ref[...])
        intra_ref[...] = intra
        qe_ref[...] = qe.astype(BF16)
        dec_ref[...] = dec
        s = state[...]
        st_ref[...] = s
        state[...] = dec * s + ut
        pl.when(step == (3 * nc) // 4)(forward)
        pl.when(step == nc - 1)(finish)

    col = lambda j: pl.BlockSpec((CH, D), lambda c: (c, j))
    res = pl.pallas_call(
        body, name="hgrn_local", grid=(nc,),
        out_shape=(jax.ShapeDtypeStruct((t, D), F32), jax.ShapeDtypeStruct((t, D), BF16),
                   jax.ShapeDtypeStruct((nc, DK, D), F32), jax.ShapeDtypeStruct((nc, 1, D), F32)) + _gathered(shards),
        in_specs=[col(0), col(1), col(2), pl.BlockSpec((2, D), lambda c: (0, 0))] + [ANYSPEC] * n,
        out_specs=(pl.BlockSpec((CH, D), lambda c: (c, 0)), pl.BlockSpec((CH, D), lambda c: (c, 0)),
                   pl.BlockSpec((None, DK, D), lambda c: (c, 0, 0)), pl.BlockSpec((None, 1, D), lambda c: (c, 0, 0)))
        + (ANYSPEC,) * n,
        scratch_shapes=[pltpu.VMEM((DK, D), F32)] + _comm_sems(n, True),
        compiler_params=_params("arbitrary"),
    )(p, p, p, logits, *shards)
    return res[:4] + (res[4:],)


def _conv_tile(t):
    return min(256, t)


def conv_forward(p, dw, db, ln_g, ln_b, shards):
    t = p.shape[0]
    tm = _conv_tile(t)
    per = tm // HALO
    n = len(shards)
    nt = t // tm

    def body(*refs):
        cv_ref, cg_ref, cvp_ref, cgp_ref, dw_ref, db_ref, g_ref, b_ref = refs[:8]
        yc_ref, act_ref = refs[8 + n:10 + n]
        uext, ush = refs[10 + 2 * n:12 + 2 * n]
        start, forward, finish = _gather2_ops(refs[8:8 + n], refs[10 + n:10 + 2 * n], *refs[12 + 2 * n:])
        step = pl.program_id(0)
        pl.when(step == 0)(start)
        pl.when(step == nt - 1)(forward)
        uext[0:HALO, :] = jnp.where(step == 0, 0.0, f_glu(cvp_ref[...], cgp_ref[...]))
        uext[HALO:HALO + tm, :] = f_glu(cv_ref[...], cg_ref[...])
        _shift_stack(uext, ush, tm + HALO)
        acc = jnp.zeros((tm, D), F32) + db_ref[...]
        for w in range(KW):
            acc = acc + dw_ref[w:w + 1, :] * _shifted(ush, HALO - KW + 1 + w, tm)
        yc_ref[...] = acc
        act_ref[...] = f_conv_act(acc, g_ref[...], b_ref[...]).astype(BF16)
        pl.when(step == nt - 1)(finish)

    vec = pl.BlockSpec((1, D), lambda m: (0, 0))
    prev = lambda j: pl.BlockSpec((HALO, D), lambda m: (jnp.maximum(m * per - 1, 0), j))
    res = pl.pallas_call(
        body, name="conv_forward", grid=(nt,),
        out_shape=(jax.ShapeDtypeStruct((t, D), F32), jax.ShapeDtypeStruct((t, D), BF16)) + _gathered(shards),
        in_specs=[pl.BlockSpec((tm, D), lambda m: (m, 4)), pl.BlockSpec((tm, D), lambda m: (m, 5)), prev(4), prev(5),
                  pl.BlockSpec((HALO, D), lambda m: (0, 0)), vec, vec, vec] + [ANYSPEC] * n,
        out_specs=(pl.BlockSpec((tm, D), lambda m: (m, 0)), pl.BlockSpec((tm, D), lambda m: (m, 0))) + (ANYSPEC,) * n,
        scratch_shapes=[pltpu.VMEM((HALO + tm, D), F32), pltpu.VMEM((8, tm + HALO, D), F32)] + _comm_sems(n, True),
        compiler_params=_params("arbitrary"),
    )(p, p, p, p, dw, db, ln_g, ln_b, *shards)
    return res[0], res[1], res[2:]


def _mix_tile(t):
    return min(256, t)


def _mix_forward_tile(x_ref, og_ref, ga_ref, gb_ref, qe_ref, intra_ref, st_ref, act_ref, wa_ref, wb_ref, wo_ref,
                      hg_ref, rows):
    o = _inter(qe_ref[...], st_ref, rows) + intra_ref[...]
    a = f_head_out(o, og_ref[...], hg_ref[...])
    ya = _dot(a.astype(BF16), wa_ref[...], _NN)
    yb = _dot(act_ref[...], wb_ref[...], _NN)
    merged = f_merge(ga_ref[...], gb_ref[...], ya, yb)
    y = _dot(merged.astype(BF16), wo_ref[...], _NN)
    return o, a, ya, yb, merged, y


def _mix_specs(tm, tiles=None):
    at = (lambda m: m) if tiles is None else (lambda m: tiles - 1 - m)
    col = lambda j: pl.BlockSpec((tm, D), lambda m: (at(m), j))
    row = pl.BlockSpec((tm, D), lambda m: (at(m), 0))
    per_chunk = lambda rows: pl.BlockSpec((tm // CH, rows, D), lambda m: (at(m), 0, 0))
    return col, row, pl.BlockSpec((1, D), lambda m: (0, 0)), per_chunk


def mix_forward(x, p, qe, intra, st_prev, act, wa, wb, wo, hg, g1, post):
    t = x.shape[0]
    tm = _mix_tile(t)

    def body(x_ref, og_ref, ga_ref, gb_ref, qe_ref, intra_ref, st_ref, act_ref, wa_ref, wb_ref, wo_ref, hg_ref,
             g1_ref, post_ref, x1_ref):
        y = _mix_forward_tile(x_ref, og_ref, ga_ref, gb_ref, qe_ref, intra_ref, st_ref, act_ref, wa_ref, wb_ref,
                              wo_ref, hg_ref, tm)[-1]
        x1_ref[...] = f_residual(x_ref[...], y, g1_ref[...], post_ref[...])

    col, row, vec, per_chunk = _mix_specs(tm)
    return pl.pallas_call(
        body, name="mix_forward", grid=(t // tm,),
        out_shape=jax.ShapeDtypeStruct((t, D), F32),
        in_specs=[row, col(3), col(6), col(7), row, row, per_chunk(DK), row, VSPEC, VSPEC, VSPEC, vec, vec, vec],
        out_specs=row,
        compiler_params=_params("parallel"),
    )(x, p, p, p, qe, intra, st_prev, act, wa, wb, wo, hg, g1, post)


def ffn_forward_backward(x1, target, w1, w2, pre, sc, sh, g2, post):
    t = x1.shape[0]
    tm = min(512, t)
    nb = w1.shape[0]
    fb = w1.shape[2]

    def body(x_ref, tg_ref, w1_ref, w2_ref, pre_ref, sc_ref, sh_ref, g2_ref, post_ref,
             dx_ref, h2_ref, r_ref, dz_ref, dy2_ref, acc_ref, z_sc):
        @pl.when(pl.program_id(0) == 0)
        def _():
            acc_ref[...] = jnp.zeros_like(acc_ref)

        x1v = x_ref[...]
        h2, vjp_h = jax.vjp(f_modulate, x1v, pre_ref[...], sc_ref[...], sh_ref[...])
        h2b = h2.astype(BF16)
        h2_ref[...] = h2b
        y2 = jnp.zeros((tm, D), F32)
        for n in range(nb):
            z = _dot(h2b, w1_ref[n], _NN)
            z_sc[:, n * fb:(n + 1) * fb] = z
            r = jnp.square(jnp.maximum(z, 0.0)).astype(BF16)
            r_ref[:, n * fb:(n + 1) * fb] = r
            y2 = y2 + _dot(r, w2_ref[n], _NN)
        out, vjp_r = jax.vjp(f_residual, x1v, y2, g2_ref[...], post_ref[...])
        err = out - tg_ref[...]
        tok = jnp.mean(jnp.square(err), axis=-1, keepdims=True)
        acc_ref[5:6, :] += 0.5 * jnp.sum(tok, axis=0, keepdims=True)
        dx_a, dy2, dg2, dpost = vjp_r(err * (1.0 / D))
        dy2b = dy2.astype(BF16)
        dy2_ref[...] = dy2b
        dh2 = jnp.zeros((tm, D), F32)
        for n in range(nb):
            dr = _dot(dy2b, w2_ref[n], _NT)
            dz = (dr * (2.0 * jnp.maximum(z_sc[:, n * fb:(n + 1) * fb], 0.0))).astype(BF16)
            dz_ref[:, n * fb:(n + 1) * fb] = dz
            dh2 = dh2 + _dot(dz, w1_ref[n], _NT)
        dx_b, dpre, dsc, dsh = vjp_h(dh2)
        dx_ref[...] = dx_a + dx_b
        acc_ref[0:1, :] += dpre
        acc_ref[1:2, :] += dpost
        acc_ref[2:3, :] += dsc
        acc_ref[3:4, :] += dsh
        acc_ref[4:5, :] += dg2

    row = pl.BlockSpec((tm, D), lambda m: (m, 0))
    wide = pl.BlockSpec((tm, DFF), lambda m: (m, 0), pipeline_mode=pl.Buffered(1))
    vec = pl.BlockSpec((1, D), lambda m: (0, 0))
    return pl.pallas_call(
        body, name="ffn_forward_backward", grid=(t // tm,),
        out_shape=(jax.ShapeDtypeStruct((t, D), F32), jax.ShapeDtypeStruct((t, D), BF16),
                   jax.ShapeDtypeStruct((t, DFF), BF16), jax.ShapeDtypeStruct((t, DFF), BF16),
                   jax.ShapeDtypeStruct((t, D), BF16), jax.ShapeDtypeStruct((8, D), F32)),
        in_specs=[row, row, VSPEC, VSPEC, vec, vec, vec, vec, vec],
        out_specs=(row, row, wide, wide, row, pl.BlockSpec((8, D), lambda m: (0, 0))),
        scratch_shapes=[pltpu.VMEM((tm, DFF), F32)],
        compiler_params=_params("arbitrary"),
    )(x1, target, w1, w2, pre, sc, sh, g2, post)


def mix_backward(x, p, qe, intra, st_prev, act, wa, wb, wo, hg, g1, post, dx1, dec):
    t = x.shape[0]
    tm = _mix_tile(t)
    nc = t // CH

    def body(x_ref, og_ref, ga_ref, gb_ref, qe_ref, intra_ref, st_ref, act_ref, wa_ref, wb_ref, wo_ref, hg_ref,
             g1_ref, post_ref, dx1_ref, dec_ref,
             dp_ref, do_ref, dqe_ref, gt_ref, dact_ref, a_ref, mg_ref, dya_ref, dyb_ref, dy_ref, acc_ref, gstate):
        @pl.when(pl.program_id(0) == 0)
        def _():
            acc_ref[...] = jnp.zeros_like(acc_ref)
            gstate[...] = jnp.zeros_like(gstate)

        o, a, ya, yb, merged, y = _mix_forward_tile(x_ref, og_ref, ga_ref, gb_ref, qe_ref, intra_ref, st_ref, act_ref,
                                                    wa_ref, wb_ref, wo_ref, hg_ref, tm)
        a_ref[...] = a.astype(BF16)
        mg_ref[...] = merged.astype(BF16)
        _, vjp_r = jax.vjp(f_residual, x_ref[...], y, g1_ref[...], post_ref[...])
        _, dy, dg1, dpost = vjp_r(dx1_ref[...])
        dyb16 = dy.astype(BF16)
        dy_ref[...] = dyb16
        dmerged = _dot(dyb16, wo_ref[...], _NT)
        _, vjp_m = jax.vjp(f_merge, ga_ref[...], gb_ref[...], ya, yb)
        dga, dgb, dya, dyb = vjp_m(dmerged)
        dp_ref[:, D:2 * D] = dga.astype(BF16)
        dp_ref[:, 2 * D:3 * D] = dgb.astype(BF16)
        dya16, dyb16b = dya.astype(BF16), dyb.astype(BF16)
        dya_ref[...] = dya16
        dyb_ref[...] = dyb16b
        da = _dot(dya16, wa_ref[...], _NT)
        dact_ref[...] = _dot(dyb16b, wb_ref[...], _NT)
        _, vjp_a = jax.vjp(f_head_out, o, og_ref[...], hg_ref[...])
        do, dog, dhg = vjp_a(da)
        dp_ref[:, 0:D] = dog.astype(BF16)
        do_ref[...] = do
        do16 = do.astype(BF16)
        qe16 = qe_ref[...]
        for ci in reversed(range(tm // CH)):
            st = st_ref[ci].astype(BF16)
            rows = slice(ci * CH, (ci + 1) * CH)
            dqe, vt = [], []
            for h in range(HEADS):
                sl = slice(h * DK, (h + 1) * DK)
                dqe.append(_dot(do16[rows, sl], st[:, sl], _NN))
                vt.append(_dot(do16[rows, sl], qe16[rows, sl], _TN))
            dqe_ref[rows, :] = jnp.concatenate(dqe, axis=1)
            g = gstate[...]
            gt_ref[ci] = g
            gstate[...] = dec_ref[ci] * g + jnp.concatenate(vt, axis=1)
        acc_ref[0:1, :] += dg1
        acc_ref[1:2, :] += dpost
        acc_ref[2:3, :] += dhg
        acc_ref[3:4, :] += jnp.sum(dog, axis=0, keepdims=True)
        acc_ref[4:5, :] += jnp.sum(dga, axis=0, keepdims=True)
        acc_ref[5:6, :] += jnp.sum(dgb, axis=0, keepdims=True)

    nt = t // tm
    col, row, vec, per_chunk = _mix_specs(tm, nt)
    b16 = jax.ShapeDtypeStruct((t, D), BF16)
    f32 = jax.ShapeDtypeStruct((t, D), F32)
    return pl.pallas_call(
        body, name="mix_backward", grid=(nt,),
        out_shape=(jax.ShapeDtypeStruct((t, NDEV * D), BF16), f32, f32, jax.ShapeDtypeStruct((nc, DK, D), F32), f32,
                   b16, b16, b16, b16, b16, jax.ShapeDtypeStruct((8, D), F32)),
        in_specs=[row, col(3), col(6), col(7), row, row, per_chunk(DK), row, VSPEC, VSPEC, VSPEC, vec, vec, vec, row,
                  per_chunk(1)],
        out_specs=(pl.BlockSpec((tm, 3 * D), lambda m: (nt - 1 - m, 0)), row, row, per_chunk(DK), row, row, row, row,
                   row, row, pl.BlockSpec((8, D), lambda m: (0, 0))),
        scratch_shapes=[pltpu.VMEM((DK, D), F32)],
        compiler_params=_params("arbitrary"),
    )(x, p, p, p, qe, intra, st_prev, act, wa, wb, wo, hg, g1, post, dx1, dec)


def conv_backward(p, yc, dact, dw, ln_g, ln_b, blocks, dp):
    t = p.shape[0]
    tm = _conv_tile(t)
    per = tm // HALO
    nt = t // tm
    n = len(blocks)

    def body(*refs):
        cv_ref, cg_ref, cvp_ref, cgp_ref, yc_ref, ycn_ref, da_ref, dan_ref, dw_ref, g_ref, b_ref = refs[:11]
        dp_ref, acc_ref, ddw_ref = refs[12 + n:15 + n]
        uext, dyext, ush, dysh, ddw8, du_sc = refs[15 + 2 * n:21 + 2 * n]
        start, finish = _exchange_ops(refs[11:11 + n], refs[15 + n:15 + 2 * n], *refs[21 + 2 * n:])
        m = pl.program_id(0)

        @pl.when(m == 0)
        def _():
            start()
            acc_ref[...] = jnp.zeros_like(acc_ref)
            ddw8[...] = jnp.zeros_like(ddw8)

        cv, cg = cv_ref[...], cg_ref[...]
        u, vjp_u = jax.vjp(f_glu, cv, cg)
        uext[0:HALO, :] = jnp.where(m == 0, 0.0, f_glu(cvp_ref[...], cgp_ref[...]))
        uext[HALO:HALO + tm, :] = u
        _shift_stack(uext, ush, tm + HALO)
        _, vjp_c = jax.vjp(f_conv_act, yc_ref[...], g_ref[...], b_ref[...])
        dyc, dg, db = vjp_c(da_ref[...])
        _, vjp_n = jax.vjp(f_conv_act, ycn_ref[...], g_ref[...], b_ref[...])
        dyn = vjp_n(dan_ref[...])[0]
        dyext[0:tm, :] = dyc
        dyext[tm:tm + HALO, :] = jnp.where(m == nt - 1, 0.0, dyn)
        _shift_stack(dyext, dysh, tm + HALO)
        rb = min(128, tm)
        for lt in range(D // LANE):
            ls = slice(lt * LANE, (lt + 1) * LANE)
            for r0 in range(0, tm, rb):
                du_l = jnp.zeros((rb, LANE), F32)
                for w in range(KW):
                    du_l = du_l + dw_ref[w:w + 1, ls] * _shifted(dysh, KW - 1 - w + r0, rb, ls)
                du_sc[r0:r0 + rb, ls] = du_l
                dyc_l = dyext[r0:r0 + rb, ls]
                for w in range(KW):
                    prod = dyc_l * _shifted(ush, HALO - KW + 1 + w + r0, rb, ls)
                    part = jnp.sum(prod.reshape(4, rb // 32, 8, LANE), axis=1)
                    ddw8[w, :, ls] += (part[0] + part[1]) + (part[2] + part[3])
        du = du_sc[...]

        @pl.when(m == nt - 1)
        def _():
            ddw_ref[...] = jnp.sum(ddw8[...], axis=1)

        dcv, dcg = vjp_u(du)
        dp_ref[:, 0:D] = dcv.astype(BF16)
        dp_ref[:, D:2 * D] = dcg.astype(BF16)
        acc_ref[0:1, :] += jnp.sum(dyc, axis=0, keepdims=True)
        acc_ref[1:2, :] += dg
        acc_ref[2:3, :] += db
        acc_ref[3:4, :] += jnp.sum(dcv, axis=0, keepdims=True)
        acc_ref[4:5, :] += jnp.sum(dcg, axis=0, keepdims=True)
        pl.when(m == nt - 1)(finish)

    vec = pl.BlockSpec((1, D), lambda m: (0, 0))
    row = pl.BlockSpec((tm, D), lambda m: (m, 0))
    prev = lambda j: pl.BlockSpec((HALO, D), lambda m: (jnp.maximum(m * per - 1, 0), j))
    nxt = pl.BlockSpec((HALO, D), lambda m: (jnp.minimum((m + 1) * per, t // HALO - 1), 0))
    res = pl.pallas_call(
        body, name="conv_backward", grid=(nt,),
        out_shape=(jax.ShapeDtypeStruct(dp.shape, dp.dtype), jax.ShapeDtypeStruct((8, D), F32),
                   jax.ShapeDtypeStruct((HALO, D), F32)) + _exchanged(blocks),
        in_specs=[pl.BlockSpec((tm, D), lambda m: (m, 4)), pl.BlockSpec((tm, D), lambda m: (m, 5)), prev(4), prev(5),
                  row, nxt, row, nxt, pl.BlockSpec((HALO, D), lambda m: (0, 0)), vec, vec] + [ANYSPEC] * (n + 1),
        out_specs=(pl.BlockSpec((tm, 2 * D), lambda m: (m, 3)), pl.BlockSpec((8, D), lambda m: (0, 0)),
                   pl.BlockSpec((HALO, D), lambda m: (0, 0))) + (ANYSPEC,) * n,
        scratch_shapes=[pltpu.VMEM((HALO + tm, D), F32), pltpu.VMEM((tm + HALO, D), F32),
                        pltpu.VMEM((8, tm + HALO, D), F32), pltpu.VMEM((8, tm + HALO, D), F32),
                        pltpu.VMEM((HALO, 8, D), F32), pltpu.VMEM((tm, D), F32)] + _comm_sems(n, False),
        input_output_aliases={11 + n: 0},
        compiler_params=_params("arbitrary"),
    )(p, p, p, p, yc, yc, dact, dact, dw, ln_g, ln_b, *blocks, dp)
    return res[:3] + (res[3:],)


def hgrn_backward(p, logits, do, dqe, gt, st_prev, blocks, dp):
    t = p.shape[0]
    nc = t // CH
    n = len(blocks)

    def body(*refs):
        q_ref, f_ref, v_ref, l_ref, do_ref, dqe_ref, gt_ref, st_ref = refs[:8]
        dp_ref, acc_ref = refs[9 + n:11 + n]
        start, finish = _exchange_ops(refs[8:8 + n], refs[11 + n:11 + 2 * n], *refs[11 + 2 * n:])

        @pl.when(pl.program_id(0) == 0)
        def _():
            start()
            acc_ref[...] = jnp.zeros_like(acc_ref)

        gt_v = gt_ref[...]
        ddec = jnp.sum(gt_v * st_ref[...], axis=0, keepdims=True)
        _, vjp = jax.vjp(f_hgrn_chunk, q_ref[...], f_ref[...], v_ref[...], l_ref[...])
        dq, df, dv, dl = vjp((do_ref[...], dqe_ref[...], gt_v, ddec))
        dp_ref[:, 0:D] = dq.astype(BF16)
        dp_ref[:, D:2 * D] = df.astype(BF16)
        dp_ref[:, 2 * D:3 * D] = dv.astype(BF16)
        acc_ref[0:2, :] += dl
        acc_ref[2:3, :] += jnp.sum(dq, axis=0, keepdims=True)
        acc_ref[3:4, :] += jnp.sum(df, axis=0, keepdims=True)
        acc_ref[4:5, :] += jnp.sum(dv, axis=0, keepdims=True)
        pl.when(pl.program_id(0) == nc - 1)(finish)

    col = lambda j: pl.BlockSpec((CH, D), lambda c: (c, j))
    row = pl.BlockSpec((CH, D), lambda c: (c, 0))
    stspec = pl.BlockSpec((None, DK, D), lambda c: (c, 0, 0))
    res = pl.pallas_call(
        body, name="hgrn_backward", grid=(nc,),
        out_shape=(jax.ShapeDtypeStruct(dp.shape, dp.dtype), jax.ShapeDtypeStruct((8, D), F32)) + _exchanged(blocks),
        in_specs=[col(0), col(1), col(2), pl.BlockSpec((2, D), lambda c: (0, 0)), row, row, stspec, stspec]
        + [ANYSPEC] * (n + 1),
        out_specs=(pl.BlockSpec((CH, 3 * D), lambda c: (c, 1)), pl.BlockSpec((8, D), lambda c: (0, 0)))
        + (ANYSPEC,) * n,
        scratch_shapes=_comm_sems(n, False),
        input_output_aliases={8 + n: 0},
        compiler_params=_params("arbitrary"),
    )(p, p, p, logits, do, dqe, gt, st_prev, *blocks, dp)
    return res[:2] + (res[2:],)


def in_proj_backward(dp, w_all, x, dx1, gain, sc, sh, blocks):
    t = x.shape[0]
    tm = min(256, t)
    nt = t // tm
    n = len(blocks)

    def body(*refs):
        dp_ref, w_ref, x_ref, dx1_ref, g_ref, sc_ref, sh_ref = refs[:7]
        gx_ref, acc_ref = refs[7 + n:9 + n]
        start = finish = lambda: None
        if n:
            start, finish = _exchange_ops(refs[7:7 + n], refs[9 + n:9 + 2 * n], *refs[9 + 2 * n:])

        @pl.when(pl.program_id(0) == 0)
        def _():
            start()
            acc_ref[...] = jnp.zeros_like(acc_ref)

        dh = jnp.zeros((tm, D), F32)
        for j in range(NDEV):
            dh = dh + _dot(dp_ref[:, j * D:(j + 1) * D], w_ref[DP_SPLIT[j]], _NT)
        _, vjp_h = jax.vjp(f_modulate, x_ref[...], g_ref[...], sc_ref[...], sh_ref[...])
        dx, dg, dsc, dsh = vjp_h(dh)
        gx_ref[...] = dx1_ref[...] + dx
        acc_ref[0:1, :] += dg
        acc_ref[1:2, :] += dsc
        acc_ref[2:3, :] += dsh
        pl.when(pl.program_id(0) == nt - 1)(finish)

    row = pl.BlockSpec((tm, D), lambda m: (m, 0))
    vec = pl.BlockSpec((1, D), lambda m: (0, 0))
    res = pl.pallas_call(
        body, name="in_proj_backward", grid=(nt,),
        out_shape=(jax.ShapeDtypeStruct((t, D), F32), jax.ShapeDtypeStruct((8, D), F32)) + _exchanged(blocks),
        in_specs=[pl.BlockSpec((tm, NDEV * D), lambda m: (m, 0)), VSPEC, row, row, vec, vec, vec] + [ANYSPEC] * n,
        out_specs=(row, pl.BlockSpec((8, D), lambda m: (0, 0))) + (ANYSPEC,) * n,
        scratch_shapes=_comm_sems(n, False) if n else [],
        compiler_params=_params("arbitrary"),
    )(dp, w_all, x, dx1, gain, sc, sh, *blocks)
    return res[0], res[1], res[2:]


def weight_grad(a, b, nblk, ka, bn, a_blocked, name, b_col=lambda n: n):
    t = a.shape[0]
    tk = min(2048, t)
    nk = t // tk

    def body(a_ref, b_ref, f_ref, h_ref, acc):
        k = pl.program_id(1)

        @pl.when(k == 0)
        def _():
            acc[...] = jnp.zeros_like(acc)

        acc[...] += _dot(a_ref[...], b_ref[...], _TN)

        @pl.when(k == nk - 1)
        def _():
            f_ref[...] = acc[...]
            h_ref[...] = acc[...].astype(BF16)

    a_idx = (lambda n, k: (k, n)) if a_blocked else (lambda n, k: (k, 0))
    b_idx = (lambda n, k: (k, 0)) if a_blocked else (lambda n, k: (k, b_col(n)))
    out = pl.BlockSpec((None, ka, bn), lambda n, k: (n, 0, 0))
    return pl.pallas_call(
        body, name=name, grid=(nblk, nk),
        out_shape=(jax.ShapeDtypeStruct((nblk, ka, bn), F32), jax.ShapeDtypeStruct((nblk, ka, bn), BF16)),
        in_specs=[pl.BlockSpec((tk, ka), a_idx), pl.BlockSpec((tk, bn), b_idx)],
        out_specs=(out, out),
        scratch_shapes=[pltpu.VMEM((ka, bn), F32)],
        compiler_params=_params("parallel", "arbitrary"),
    )(a, b)


def ada_backward(call_t, dmod_cols, w, m, v):
    def body(c_ref, d_ref, w_ref, m_ref, v_ref, g_ref, dl_ref, nm_ref, nv_ref):
        ct, dm = c_ref[...], d_ref[...]
        g = ct[:, 0:1] * dm[0:1, :]
        for r in range(1, NDEV):
            g = g + ct[:, r:r + 1] * dm[r:r + 1, :]
        g_ref[...] = g
        dl_ref[...], nm_ref[...], nv_ref[...] = _adamw(w_ref[...], g, m_ref[...], v_ref[...])

    s = jax.ShapeDtypeStruct(w.shape, F32)
    return pl.pallas_call(
        body, name="ada_backward", out_shape=(s, s, s, s),
        in_specs=[VSPEC] * 5, out_specs=(VSPEC,) * 4, compiler_params=_params(),
    )(call_t, dmod_cols, w, m, v)


def adamw_small(total, ddw_mine, recipes, ws, ms, vs):
    n = len(ws)

    def body(*refs):
        tot, ddw = refs[0], refs[1]
        w_refs, m_refs, v_refs = refs[2:2 + n], refs[2 + n:2 + 2 * n], refs[2 + 2 * n:2 + 3 * n]
        outs = refs[2 + 3 * n:2 + 7 * n]
        loss_ref = refs[2 + 7 * n]
        for i, rec in enumerate(recipes):
            if rec == "dw":
                g = ddw[...]
            elif isinstance(rec, tuple):
                g = tot[rec[0]:rec[1], :]
            else:
                g = jnp.concatenate([tot[r:r + 1, :] for r in rec], axis=1) if len(rec) > 1 else tot[rec[0]:rec[0] + 1, :]
            dl, nm, nv = _adamw(w_refs[i][...], g, m_refs[i][...], v_refs[i][...])
            outs[4 * i][...] = g
            outs[4 * i + 1][...] = dl
            outs[4 * i + 2][...] = nm
            outs[4 * i + 3][...] = nv
        loss_ref[...] = tot[LOSS_ROW:LOSS_ROW + 1, 0:128]

    shapes = []
    for w in ws:
        shapes += [jax.ShapeDtypeStruct(w.shape, F32)] * 4
    res = pl.pallas_call(
        body, name="adamw_small", out_shape=tuple(shapes) + (jax.ShapeDtypeStruct((1, 128), F32),),
        in_specs=[VSPEC] * (2 + 3 * n), out_specs=(VSPEC,) * (4 * n + 1), compiler_params=_params(),
    )(total, ddw_mine, *ws, *ms, *vs)
    return [res[4 * i:4 * i + 4] for i in range(n)], res[4 * n]


def reduce_and_adamw(me, w, g_all, g_recv, m, v, name):
    r, c = w.shape
    br = min(128, r)

    def body(me_ref, w_ref, go_ref, gr_ref, m_ref, v_ref, g_ref, dl_ref, nm_ref, nv_ref):
        g = go_ref[...]
        for k in range(NDEV - 1):
            g = g + gr_ref[k].astype(F32)
        g_ref[...] = g
        dl_ref[...], nm_ref[...], nv_ref[...] = _adamw(w_ref[...], g, m_ref[...], v_ref[...])

    blk = pl.BlockSpec((br, c), lambda i, me_ref: (i, 0))
    s = jax.ShapeDtypeStruct(w.shape, F32)
    return pl.pallas_call(
        body, name=name, out_shape=(s, s, s, s),
        grid_spec=pltpu.PrefetchScalarGridSpec(
            num_scalar_prefetch=1, grid=(r // br,),
            in_specs=[blk, pl.BlockSpec((None, br, c), lambda i, me_ref: (me_ref[0], i, 0)),
                      pl.BlockSpec((NDEV - 1, br, c), lambda i, me_ref: (0, i, 0)), blk, blk],
            out_specs=(blk, blk, blk, blk)),
        compiler_params=_params("parallel"),
    )(me, w, g_all, g_recv, m, v)


def kernel(x, c, w_ada, b_ada, pre_norm_tm, post_norm_tm, pre_norm_cm, post_norm_cm, w_in, b_in, hg_lb_logits, hg_norm, conv_dw, conv_db, conv_ln_g, conv_ln_b, w_br_a, w_br_b, w_out, w_ff1, w_ff2, loss_target, m_w_ada, m_b_ada, m_pre_norm_tm, m_post_norm_tm, m_pre_norm_cm, m_post_norm_cm, m_w_in, m_b_in, m_hg_lb_logits, m_hg_norm, m_conv_dw, m_conv_db, m_conv_ln_g, m_conv_ln_b, m_w_br_a, m_w_br_b, m_w_out, m_w_ff1, m_w_ff2, v_w_ada, v_b_ada, v_pre_norm_tm, v_post_norm_tm, v_pre_norm_cm, v_post_norm_cm, v_w_in, v_b_in, v_hg_lb_logits, v_hg_norm, v_conv_dw, v_conv_db, v_conv_ln_g, v_conv_ln_b, v_w_br_a, v_w_br_b, v_w_out, v_w_ff1, v_w_ff2):
    t = x.shape[1]
    me = 4 * lax.axis_index("x") + 2 * lax.axis_index("y") + lax.axis_index("c")
    xs = x[0]
    tgt = loss_target[0]

    mod, call, dw_all = ada_forward(c, w_ada[0], b_ada, conv_dw[0])
    sh1, sc1, g1, sh2, sc2, g2 = [mod[:, i * D:(i + 1) * D] for i in range(6)]
    dw = jnp.pad(dw_all.transpose(1, 0, 2).reshape(KW, D), ((0, HALO - KW), (0, 0)))

    p, h, win_all = in_proj(xs, pre_norm_tm, sc1, sh1, w_in[0].astype(BF16), b_in)
    yc, act, (w1_all,) = conv_forward(p, dw, conv_db, conv_ln_g, conv_ln_b, [w_ff1[0].astype(BF16)])
    intra, qe, st_prev, dec, (w2_all, wa_all, wb_all, wo_all) = hgrn_local(
        p, hg_lb_logits, [w_ff2[0].astype(BF16), w_br_a[0].astype(BF16), w_br_b[0].astype(BF16),
                          w_out[0].astype(BF16)])
    wa, wb, wo = wa_all.reshape(D, D), wb_all.reshape(D, D), wo_all.reshape(D, D)
    x1 = mix_forward(xs, p, qe, intra, st_prev, act, wa, wb, wo, hg_norm, g1, post_norm_tm)

    dx1, h2, r, dz, dy2, acc_ffn = ffn_forward_backward(x1, tgt, w1_all, w2_all, pre_norm_cm, sc2, sh2, g2,
                                                        post_norm_cm)
    rows = D // NDEV
    g1_f, g1_h = weight_grad(h2, dz, NDEV, D, DFF // NDEV, False, "grad_w_ff1")
    g2_f, g2_h = weight_grad(r, dy2, 4, D, D, True, "grad_w_ff2")
    g2_f, g2_h = g2_f.reshape(NDEV, DFF // NDEV, D), g2_h.reshape(NDEV, DFF // NDEV, D)

    (dp, do, dqe, gt, dact, a16, mg16, dya, dyb, dy, acc_mix) = mix_backward(
        xs, p, qe, intra, st_prev, act, wa, wb, wo, hg_norm, g1, post_norm_tm, dx1, dec)
    ga_f, ga_h = weight_grad(a16, dya, 1, D, D, False, "grad_w_br_a")
    gb_f, gb_h = weight_grad(act, dyb, 1, D, D, False, "grad_w_br_b")
    go_f, go_h = weight_grad(mg16, dy, 1, D, D, False, "grad_w_out")
    dp, acc_conv, ddw, (r_ff1, r_ff2) = conv_backward(p, yc, dact, dw, conv_ln_g, conv_ln_b, [g1_h, g2_h], dp)
    dp, acc_hg, (r_a, r_b, r_o) = hgrn_backward(
        p, hg_lb_logits, do, dqe, gt, st_prev,
        [ga_h.reshape(NDEV, rows, D), gb_h.reshape(NDEV, rows, D), go_h.reshape(NDEV, rows, D)], dp)
    gin_f, gin_h = weight_grad(
        h, dp, NDEV, D, D, False, "grad_w_in",
        b_col=lambda n: jnp.where(n < 3, n + 3, jnp.where(n == 3, 0, jnp.where(n < 6, n + 2, n - 5))))
    in_send, in_recv, gin_thru, landing, token = exchange_start(gin_h)
    grad_x, acc_in, _ = in_proj_backward(dp, win_all, xs, dx1, pre_norm_tm, sc1, sh1 + token[0:1, 0:1], [])

    own = [ga_f.reshape(NDEV, rows, D), gb_f.reshape(NDEV, rows, D), go_f.reshape(NDEV, rows, D), g1_f, g2_f]
    recv = [r_a, r_b, r_o, r_ff1, r_ff2]
    big = {}
    names = ["w_br_a", "w_br_b", "w_out", "w_ff1", "w_ff2"]
    ws = [w_br_a, w_br_b, w_out, w_ff1, w_ff2]
    ms = [m_w_br_a, m_w_br_b, m_w_out, m_w_ff1, m_w_ff2]
    vs = [v_w_br_a, v_w_br_b, v_w_out, v_w_ff1, v_w_ff2]
    me1 = me.astype(jnp.int32).reshape(1)
    for i, nm in enumerate(names):
        big[nm] = [o[None] for o in reduce_and_adamw(me1, ws[i][0], own[i], recv[i], ms[i][0], vs[i][0], "adamw_" + nm)]
    big, grad_x, acc_in = lax.optimization_barrier((big, grad_x, acc_in))
    r_in = exchange_wait(in_send, in_recv, gin_thru, landing, acc_in)
    r_in, acc_in = lax.optimization_barrier((r_in, acc_in))
    big["w_in"] = [o[None] for o in reduce_and_adamw(me1, w_in[0], gin_f, r_in, m_w_in[0], v_w_in[0], "adamw_w_in")]

    dmod_rows = [2, 1, 8, 19, 18, 20]
    total, kept = gather_and_sum_rows([acc_in, acc_mix, acc_ffn, acc_hg, acc_conv, ddw], dmod_rows)
    dmod_all = kept[:, 0:6, :].reshape(NDEV, 6 * D)
    wcols = w_ada.shape[2]
    gwa, dwa, nmwa, nvwa = ada_backward(call.T, lax.dynamic_slice_in_dim(dmod_all, me * wcols, wcols, axis=1),
                                        w_ada[0], m_w_ada[0], v_w_ada[0])
    ddw_mine = lax.dynamic_slice_in_dim(total[40:40 + KW], me * (D // NDEV), D // NDEV, axis=1)

    small_names = ["b_ada", "pre_norm_tm", "post_norm_tm", "pre_norm_cm", "post_norm_cm", "b_in", "hg_lb_logits",
                   "hg_norm", "conv_db", "conv_ln_g", "conv_ln_b", "conv_dw"]
    recipes = [dmod_rows, [0], [9], [16], [17], [26, 27, 28, 11, 35, 36, 12, 13], (24, 26), [10], [32], [33], [34],
               "dw"]
    small_w = [b_ada, pre_norm_tm, post_norm_tm, pre_norm_cm, post_norm_cm, b_in, hg_lb_logits, hg_norm, conv_db,
               conv_ln_g, conv_ln_b, conv_dw[0]]
    small_m = [m_b_ada, m_pre_norm_tm, m_post_norm_tm, m_pre_norm_cm, m_post_norm_cm, m_b_in, m_hg_lb_logits,
               m_hg_norm, m_conv_db, m_conv_ln_g, m_conv_ln_b, m_conv_dw[0]]
    small_v = [v_b_ada, v_pre_norm_tm, v_post_norm_tm, v_pre_norm_cm, v_post_norm_cm, v_b_in, v_hg_lb_logits,
               v_hg_norm, v_conv_db, v_conv_ln_g, v_conv_ln_b, v_conv_dw[0]]
    small_out, loss_row = adamw_small(total, ddw_mine, recipes, small_w, small_m, small_v)
    loss = loss_row[0, 0]
    sm = {nm: list(o) for nm, o in zip(small_names, small_out)}
    sm["conv_dw"] = [o[None] for o in sm["conv_dw"]]

    order = ["w_ada", "b_ada", "pre_norm_tm", "post_norm_tm", "pre_norm_cm", "post_norm_cm", "w_in", "b_in",
             "hg_lb_logits", "hg_norm", "conv_dw", "conv_db", "conv_ln_g", "conv_ln_b", "w_br_a", "w_br_b", "w_out",
             "w_ff1", "w_ff2"]
    res = dict(sm)
    res.update(big)
    res["w_ada"] = [gwa[None], dwa[None], nmwa[None], nvwa[None]]
    outs = [loss, grad_x[None]]
    for j in range(4):
        outs += [res[nm][j] for nm in order]
    return tuple(outs)
```

```python
import functools

import jax
import jax.numpy as jnp
from jax import lax
from jax.experimental import pallas as pl
from jax.experimental.pallas import tpu as pltpu

F32 = jnp.float32
BF16 = jnp.bfloat16
MESH = pl.DeviceIdType.MESH
AXES = ("x", "y", "c")

D = 1024
HEADS = 8
DK = 128
LANE = 128
CH = 128
LEVELS = 7
KW = 31
HALO = 32
DFF = 4096
NDEV = 8
EPS = 1e-6
DP_SPLIT = (3, 6, 7, 0, 1, 2, 4, 5)
LOSS_ROW = 21
ADAM_LR, ADAM_B1, ADAM_B2, ADAM_EPS, ADAM_WD, ADAM_STEP = 0.001, 0.9, 0.999, 1e-08, 0.01, 10
VMEM_LIMIT = 58 * 1024 * 1024

_NN = (((1,), (0,)), ((), ()))
_NT = (((1,), (1,)), ((), ()))
_TN = (((0,), (0,)), ((), ()))

VSPEC = pl.BlockSpec(memory_space=pltpu.VMEM)
ANYSPEC = pl.BlockSpec(memory_space=pl.ANY)


def _params(*sem):
    return pltpu.CompilerParams(dimension_semantics=sem or None, vmem_limit_bytes=VMEM_LIMIT)


def _dot(a, b, dims):
    return lax.dot_general(a, b, dims, preferred_element_type=F32)


@jax.custom_vjp
def mm_nn(a, b):
    return _dot(a.astype(BF16), b.astype(BF16), _NN)


def _mm_nn_fwd(a, b):
    ab, bb = a.astype(BF16), b.astype(BF16)
    return _dot(ab, bb, _NN), (ab, bb)


def _mm_nn_bwd(res, ct):
    ab, bb = res
    cb = ct.astype(BF16)
    return _dot(cb, bb, _NT), _dot(ab, cb, _TN)


mm_nn.defvjp(_mm_nn_fwd, _mm_nn_bwd)


@jax.custom_vjp
def mm_nt(a, b):
    return _dot(a.astype(BF16), b.astype(BF16), _NT)


def _mm_nt_fwd(a, b):
    ab, bb = a.astype(BF16), b.astype(BF16)
    return _dot(ab, bb, _NT), (ab, bb)


def _mm_nt_bwd(res, ct):
    ab, bb = res
    cb = ct.astype(BF16)
    return _dot(cb, bb, _NN), _dot(cb, ab, _TN)


mm_nt.defvjp(_mm_nt_fwd, _mm_nt_bwd)


@jax.custom_vjp
def mm_tn(a, b):
    return _dot(a.astype(BF16), b.astype(BF16), _TN)


def _mm_tn_fwd(a, b):
    ab, bb = a.astype(BF16), b.astype(BF16)
    return _dot(ab, bb, _TN), (ab, bb)


def _mm_tn_bwd(res, ct):
    ab, bb = res
    cb = ct.astype(BF16)
    return _dot(bb, cb, _NT), _dot(ab, cb, _NN)


mm_tn.defvjp(_mm_tn_fwd, _mm_tn_bwd)


def _rms(x):
    return x * lax.rsqrt(jnp.mean(x * x, axis=-1, keepdims=True) + EPS)


_sigmoid = jax.nn.sigmoid


def _silu(x):
    return x * _sigmoid(x)


def f_modulate(x, gain, sc, sh):
    return _rms(x) * gain * (1.0 + sc) + sh


def f_residual(x, y, gate, gain):
    return x + gate * (_rms(y) * gain)


def f_merge(ga, gb, ya, yb):
    return _sigmoid(ga) * ya + _sigmoid(gb) * yb


def f_head_out(o, og, hg):
    heads = [_rms(o[:, h * DK:(h + 1) * DK]) for h in range(HEADS)]
    return jnp.concatenate(heads, axis=1) * hg * _silu(og)


def f_conv_act(u, g, b):
    mu = jnp.mean(u, axis=-1, keepdims=True)
    var = jnp.mean(jnp.square(u - mu), axis=-1, keepdims=True)
    return _silu((u - mu) * lax.rsqrt(var + EPS) * g + b)


def f_glu(cv, cg):
    return cv * _sigmoid(cg)


def _split2(x):
    hi = x.astype(BF16)
    return hi, (x - hi.astype(F32)).astype(BF16)


def _tri(transposed):
    i = lax.broadcasted_iota(jnp.int32, (CH, CH), 1 if transposed else 0)
    t = lax.broadcasted_iota(jnp.int32, (CH, CH), 0 if transposed else 1)
    return jnp.where(t <= i, 1.0, 0.0).astype(BF16)


def _blocks3(x, rows):
    return x.reshape(CH // rows, rows, x.shape[-1])


def _mid_broadcast(b, lev):
    h = 1 << (LEVELS - 1 - lev)
    if h >= 4:
        x3 = _blocks3(b, 2 * h)
        return jnp.broadcast_to(x3[:, h - 1:h, :], x3.shape).reshape(b.shape)
    x3 = _blocks3(b, 8)
    sub = lax.broadcasted_iota(jnp.int32, x3.shape, 1)
    out = None
    for first in range(0, 8, 2 * h):
        piece = jnp.broadcast_to(x3[:, first + h - 1:first + h, :], x3.shape)
        out = piece if out is None else jnp.where(sub >= first, piece, out)
    return out.reshape(b.shape)


def _mid_scatter(d, lev):
    h = 1 << (LEVELS - 1 - lev)
    if h >= 4:
        x3 = _blocks3(d, 2 * h)
        row = lax.broadcasted_iota(jnp.int32, x3.shape, 1)
        total = jnp.sum(x3, axis=1, keepdims=True)
        return jnp.where(row == h - 1, total, 0.0).reshape(d.shape)
    x3 = _blocks3(d, 8)
    sub = lax.broadcasted_iota(jnp.int32, x3.shape, 1)
    out = jnp.zeros_like(x3)
    for first in range(0, 8, 2 * h):
        inside = (sub >= first) & (sub < first + 2 * h)
        total = jnp.sum(jnp.where(inside, x3, 0.0), axis=1, keepdims=True)
        out = jnp.where(sub == first + h - 1, total, out)
    return out.reshape(d.shape)


@jax.custom_vjp
def decay_sums(g):
    hi, lo = _split2(g)
    tri = _tri(False)
    b = _dot(tri, hi, _NN) + _dot(tri, lo, _NN)
    return (b,) + tuple(b - _mid_broadcast(b, lev) for lev in range(LEVELS))


def _decay_sums_fwd(g):
    return decay_sums(g), None


def _decay_sums_bwd(_, cts):
    db = cts[0]
    for lev in range(LEVELS):
        db = db + cts[1 + lev] - _mid_scatter(cts[1 + lev], lev)
    hi, lo = _split2(db)
    tri = _tri(True)
    return (_dot(tri, hi, _NN) + _dot(tri, lo, _NN),)


decay_sums.defvjp(_decay_sums_fwd, _decay_sums_bwd)


def _score_masks():
    i = lax.broadcasted_iota(jnp.int32, (CH, CH), 0)
    j = lax.broadcasted_iota(jnp.int32, (CH, CH), 1)
    masks = [i == j]
    for lev in range(LEVELS):
        sh = LEVELS - 1 - lev
        same = (i >> (sh + 1)) == (j >> (sh + 1))
        masks.append(same & (((i >> sh) & 1) == 1) & (((j >> sh) & 1) == 0))
    return [jnp.where(m, 1.0, 0.0) for m in masks]


def f_hgrn_chunk(q_r, f_r, v, logits):
    l0, l1 = logits[0:1, :], logits[1:2, :]
    mx = lax.stop_gradient(jnp.maximum(l0, l1))
    e0, e1 = jnp.exp(l0 - mx), jnp.exp(l1 - mx)
    lb = e0 / (e0 + e1)
    q = _silu(q_r)
    f = lb + (1.0 - lb) * _sigmoid(f_r)
    k = 1.0 - f
    sums = decay_sums(jnp.log(f))
    b = sums[0]
    btot = b[CH - 1:CH, :]
    qe = q * jnp.exp(b)
    ke = k * jnp.exp(btot - b)
    dec = jnp.exp(btot)
    qs, ks = [q], [k]
    row = lax.broadcasted_iota(jnp.int32, b.shape, 0)
    for lev in range(LEVELS):
        upper = ((row >> (LEVELS - 1 - lev)) & 1) == 1
        e = sums[1 + lev]
        qs.append(q * jnp.exp(jnp.where(upper, e, 0.0)))
        ks.append(k * jnp.exp(jnp.where(upper, 0.0, -e)))
    masks = _score_masks()
    intra, ut = [], []
    for h in range(HEADS):
        sl = slice(h * DK, (h + 1) * DK)
        sc = None
        for lev in range(LEVELS + 1):
            s = mm_nt(qs[lev][:, sl], ks[lev][:, sl]) * masks[lev]
            sc = s if sc is None else sc + s
        intra.append(mm_nn(sc, v[:, sl]))
        ut.append(mm_tn(v[:, sl], ke[:, sl]))
    return jnp.concatenate(intra, axis=1), qe, jnp.concatenate(ut, axis=1), dec


def _inter(qe_b, st_ref, rows):
    out = []
    for ci in range(rows // CH):
        st = st_ref[ci].astype(BF16)
        heads = [_dot(qe_b[ci * CH:(ci + 1) * CH, h * DK:(h + 1) * DK], st[:, h * DK:(h + 1) * DK], _NT)
                 for h in range(HEADS)]
        out.append(jnp.concatenate(heads, axis=1))
    return jnp.concatenate(out, axis=0)


def _shift_stack(src, dst, rows):
    dst[0, 0:rows, :] = src[0:rows, :]
    for b in range(1, 8):
        dst[b, 0:rows - 8, :] = src[pl.ds(b, rows - 8), :]


def _shifted(stack, offset, rows, lanes=slice(None)):
    return stack[offset % 8, pl.ds(8 * (offset // 8), rows), lanes]


def _adamw(w, g, m, v):
    m = ADAM_B1 * m + (1.0 - ADAM_B1) * g
    v = ADAM_B2 * v + (1.0 - ADAM_B2) * jnp.square(g)
    m_hat = m / (1.0 - ADAM_B1 ** ADAM_STEP)
    v_hat = v / (1.0 - ADAM_B2 ** ADAM_STEP)
    delta = -ADAM_LR * (m_hat / (jnp.sqrt(v_hat) + ADAM_EPS) + ADAM_WD * w)
    return delta, m, v


def _me():
    return lax.axis_index("x"), lax.axis_index("y"), lax.axis_index("c")


def _peer(k):
    x, y, c = _me()
    mask = k + 1
    px = (1 - x) if (mask >> 2) & 1 else x
    py = (1 - y) if (mask >> 1) & 1 else y
    pc = (1 - c) if mask & 1 else c
    return (px, py, pc), 4 * px + 2 * py + pc


def ada_forward(c, w_ada, b_ada, dw):
    wcols = w_ada.shape[1]

    def body(c_ref, w_ref, b_ref, dw_ref, mod_ref, call_ref, dwall_ref, part_ref, modp_ref, send_sems, recv_sems):
        x, y, cc = _me()
        me = 4 * x + 2 * y + cc
        call_ref[me] = _silu(c_ref[...])
        dwall_ref[me] = dw_ref[...]
        sends = []
        for k in range(NDEV - 1):
            dev, pidx = _peer(k)
            cp = pltpu.make_async_remote_copy(dwall_ref.at[me], dwall_ref.at[me], send_sems.at[2 * (NDEV - 1) + k],
                                              recv_sems.at[2 * (NDEV - 1) + k], device_id=dev, device_id_type=MESH)
            cp.start()
            sends.append(cp)
        for k in range(NDEV - 1):
            dev, _ = _peer(k)
            cp = pltpu.make_async_remote_copy(call_ref.at[me], call_ref.at[me], send_sems.at[k], recv_sems.at[k],
                                              device_id=dev, device_id_type=MESH)
            cp.start()
            sends.append(cp)
        for k in range(NDEV - 1):
            _, pidx = _peer(k)
            pltpu.make_async_remote_copy(call_ref.at[pidx], call_ref.at[pidx], send_sems.at[k], recv_sems.at[k],
                                         device_id=_peer(k)[0], device_id_type=MESH).wait_recv()
        call = jnp.concatenate([call_ref[r] for r in range(NDEV)], axis=0)
        part = _dot(call.astype(BF16), w_ref[...].astype(BF16), _NN)
        for r in range(NDEV):
            part_ref[r] = part[r:r + 1, :]
        modp_ref[me] = part_ref[me]
        for k in range(NDEV - 1):
            dev, pidx = _peer(k)
            cp = pltpu.make_async_remote_copy(part_ref.at[pidx], modp_ref.at[me], send_sems.at[NDEV - 1 + k],
                                              recv_sems.at[NDEV - 1 + k], device_id=dev, device_id_type=MESH)
            cp.start()
            sends.append(cp)
        for k in range(NDEV - 1):
            dev, pidx = _peer(k)
            pltpu.make_async_remote_copy(part_ref.at[pidx], modp_ref.at[pidx], send_sems.at[NDEV - 1 + k],
                                         recv_sems.at[NDEV - 1 + k], device_id=dev, device_id_type=MESH).wait_recv()
        for k in range(NDEV - 1):
            dev, pidx = _peer(k)
            pltpu.make_async_remote_copy(dwall_ref.at[pidx], dwall_ref.at[pidx], send_sems.at[2 * (NDEV - 1) + k],
                                         recv_sems.at[2 * (NDEV - 1) + k], device_id=dev,
                                         device_id_type=MESH).wait_recv()
        for cp in sends:
            cp.wait_send()
        mod_ref[...] = jnp.concatenate([modp_ref[r] for r in range(NDEV)], axis=1) + b_ref[...]

    mod, call, dw_all = pl.pallas_call(
        body, name="ada_forward",
        out_shape=(jax.ShapeDtypeStruct((1, NDEV * wcols), F32), jax.ShapeDtypeStruct((NDEV, 1, D), F32),
                   jax.ShapeDtypeStruct((NDEV,) + dw.shape, F32)),
        in_specs=[VSPEC] * 4, out_specs=(VSPEC,) * 3,
        scratch_shapes=[pltpu.VMEM((NDEV, 1, wcols), F32), pltpu.VMEM((NDEV, 1, wcols), F32),
                        pltpu.SemaphoreType.DMA((3 * (NDEV - 1),)), pltpu.SemaphoreType.DMA((3 * (NDEV - 1),))],
        compiler_params=_params(),
    )(c, w_ada, b_ada, dw)
    return mod, call.reshape(NDEV, D), dw_all


def _comm_sems(n, local):
    sems = [pltpu.SemaphoreType.DMA((7 * n,)), pltpu.SemaphoreType.DMA((7 * n,))]
    return sems + ([pltpu.SemaphoreType.DMA((n,))] if local else [])


def _gather2_ops(ins, outs, send_sems, recv_sems, local_sems):
    n = len(ins)
    x, y, c = _me()
    me, sibling = (x, y, c), (x, y, 1 - c)
    chips = [(1 - x, y), (x, 1 - y), (1 - x, 1 - y)]

    def slot(p):
        return 4 * p[0] + 2 * p[1] + p[2]

    def copy(a, k, block, to, src=None):
        return pltpu.make_async_remote_copy(
            src_ref=outs[a].at[slot(block)] if src is None else src, dst_ref=outs[a].at[slot(block)],
            send_sem=send_sems.at[a * 7 + k], recv_sem=recv_sems.at[a * 7 + k], device_id=to, device_id_type=MESH)

    def local(a):
        return pltpu.make_async_copy(ins[a], outs[a].at[slot(me)], local_sems.at[a])

    def first(a):
        return [copy(a, 0, me, sibling, src=ins[a])] + [copy(a, 1 + j, me, (*chip, c), src=ins[a])
                                                        for j, chip in enumerate(chips)]

    def passed(a):
        return [copy(a, 4 + j, (*chip, c), sibling) for j, chip in enumerate(chips)]

    def start():
        for a in range(n):
            local(a).start()
            for cp in first(a):
                cp.start()

    def forward():
        for j, chip in enumerate(chips):
            for a in range(n):
                copy(a, 1 + j, (*chip, c), me).wait_recv()
                passed(a)[j].start()

    def finish():
        for a in range(n):
            copy(a, 0, sibling, me).wait_recv()
            for j, chip in enumerate(chips):
                copy(a, 4 + j, (*chip, 1 - c), me).wait_recv()
        for a in range(n):
            for cp in first(a) + passed(a):
                cp.wait_send()
            local(a).wait()

    return start, forward, finish


def _exchange_ops(ins, outs, send_sems, recv_sems):
    n = len(ins)

    def copy(a, k):
        dev, pidx = _peer(k)
        return pltpu.make_async_remote_copy(ins[a].at[pidx], outs[a].at[k], send_sems.at[a * 7 + k],
                                            recv_sems.at[a * 7 + k], device_id=dev, device_id_type=MESH)

    def start():
        for k in range(NDEV - 1):
            for a in range(n):
                copy(a, k).start()

    def finish():
        for k in range(NDEV - 1):
            for a in range(n):
                copy(a, k).wait_recv()
        for k in range(NDEV - 1):
            for a in range(n):
                copy(a, k).wait_send()

    return start, finish


def _gathered(shards):
    return tuple(jax.ShapeDtypeStruct((NDEV,) + s.shape, s.dtype) for s in shards)


def _exchanged(blocks):
    return tuple(jax.ShapeDtypeStruct((NDEV - 1,) + b.shape[1:], b.dtype) for b in blocks)


def exchange_start(blocks):
    landing = lax.empty((NDEV - 1,) + blocks.shape[1:], blocks.dtype)
    hbm = pl.BlockSpec(memory_space=pltpu.HBM)
    sem = pl.BlockSpec(memory_space=pltpu.SEMAPHORE)

    def body(src_ref, land_ref, send_sems, recv_sems, src_thru, land_thru, token):
        for k in range(NDEV - 1):
            dev, pidx = _peer(k)
            pltpu.make_async_remote_copy(src_ref.at[pidx], land_ref.at[k], send_sems.at[k], recv_sems.at[k],
                                         device_id=dev, device_id_type=MESH).start()
        token[...] = jnp.zeros_like(token)

    return pl.pallas_call(
        body, name="exchange_start",
        out_shape=(pltpu.SemaphoreType.DMA((NDEV - 1,)), pltpu.SemaphoreType.DMA((NDEV - 1,)),
                   pltpu.HBM(blocks.shape, blocks.dtype), pltpu.HBM(landing.shape, landing.dtype),
                   jax.ShapeDtypeStruct((8, 128), F32)),
        in_specs=(hbm, hbm), out_specs=(sem, sem, hbm, hbm, VSPEC), input_output_aliases={0: 2, 1: 3},
        compiler_params=pltpu.CompilerParams(has_side_effects=pltpu.SideEffectType.DATAFLOW_SIDE_EFFECTING),
    )(pltpu.with_memory_space_constraint(blocks, pltpu.HBM), pltpu.with_memory_space_constraint(landing, pltpu.HBM))


def exchange_wait(send_sems, recv_sems, src_thru, land_thru, after):
    hbm = pl.BlockSpec(memory_space=pltpu.HBM)
    sem = pl.BlockSpec(memory_space=pltpu.SEMAPHORE)

    def body(src_ref, land_ref, send_sems, recv_sems, after_ref, src_dead, got_ref):
        for k in range(NDEV - 1):
            dev, pidx = _peer(k)
            cp = pltpu.make_async_remote_copy(src_ref.at[pidx], land_ref.at[k], send_sems.at[k], recv_sems.at[k],
                                              device_id=dev, device_id_type=MESH)
            cp.wait_send()
            cp.wait_recv()

    return pl.pallas_call(
        body, name="exchange_wait",
        out_shape=(pltpu.HBM(src_thru.shape, src_thru.dtype), pltpu.HBM(land_thru.shape, land_thru.dtype)),
        in_specs=(hbm, hbm, sem, sem, ANYSPEC), out_specs=(hbm, hbm), input_output_aliases={0: 0, 1: 1},
        compiler_params=pltpu.CompilerParams(has_side_effects=pltpu.SideEffectType.DATAFLOW_SIDE_EFFECTING),
    )(src_thru, land_thru, send_sems, recv_sems, after)[1]


def gather_and_sum_rows(parts, keep_rows):
    n = len(parts)
    offs = [sum(p.shape[0] for p in parts[:i]) for i in range(n)]
    rows = sum(p.shape[0] for p in parts)

    def body(*refs):
        sum_ref, keep_ref, buf_ref, send_sems, recv_sems = refs[n:]
        x, y, c = _me()
        me = 4 * x + 2 * y + c
        for i in range(n):
            buf_ref[me, offs[i]:offs[i] + parts[i].shape[0], :] = refs[i][...]

        def copy(k, block):
            return pltpu.make_async_remote_copy(buf_ref.at[block], buf_ref.at[block], send_sems.at[k],
                                                recv_sems.at[k], device_id=_peer(k)[0], device_id_type=MESH)

        for k in range(NDEV - 1):
            copy(k, me).start()
        for k in range(NDEV - 1):
            copy(k, _peer(k)[1]).wait_recv()
        for k in range(NDEV - 1):
            copy(k, me).wait_send()
        total = buf_ref[0]
        for r in range(1, NDEV):
            total = total + buf_ref[r]
        sum_ref[...] = total
        keep_ref[...] = jnp.zeros_like(keep_ref)
        for r in range(NDEV):
            for j, src in enumerate(keep_rows):
                keep_ref[r, j:j + 1, :] = buf_ref[r, src:src + 1, :]

    return pl.pallas_call(
        body, name="gather_and_sum_rows",
        out_shape=(jax.ShapeDtypeStruct((rows, D), F32), jax.ShapeDtypeStruct((NDEV, 8, D), F32)),
        in_specs=[VSPEC] * n, out_specs=(VSPEC, VSPEC),
        scratch_shapes=[pltpu.VMEM((NDEV, rows, D), F32), pltpu.SemaphoreType.DMA((NDEV - 1,)),
                        pltpu.SemaphoreType.DMA((NDEV - 1,))],
        compiler_params=_params(),
    )(*parts)


def _gather_chips():
    x, y, c = _me()
    north = c == 1
    first = (jnp.where(north, 1 - x, x), jnp.where(north, y, 1 - y))
    second = (jnp.where(north, x, 1 - x), jnp.where(north, 1 - y, y))
    return [first, second, (1 - x, 1 - y)]


def _arrival_order():
    x, y, c = _me()
    first, second, far = _gather_chips()
    devs = [(x, y, c), (x, y, 1 - c), (*first, c), (*second, 1 - c), (*second, c), (*first, 1 - c), (*far, c),
            (*far, 1 - c)]
    return jnp.stack([4 * d[0] + 2 * d[1] + d[2] for d in devs]).astype(jnp.int32)


def in_proj(x, gain, sc, sh, w_shard, b_in):
    t = x.shape[0]
    tm = min(1024, t // 2)
    nm = t // tm
    order = _arrival_order()

    def body(order_ref, x_ref, g_ref, sc_ref, sh_ref, b_ref, wsh_ref, p_ref, h_ref, wall_ref,
             wbuf, h_all, send_sems, recv_sems, local_sems):
        k, m = pl.program_id(0), pl.program_id(1)
        mx, my, mc = _me()
        me, sibling = (mx, my, mc), (mx, my, 1 - mc)
        chips = _gather_chips()

        def slot(d):
            return 4 * d[0] + 2 * d[1] + d[2]

        def copy(sem, block, to, src=None):
            return pltpu.make_async_remote_copy(
                src_ref=wall_ref.at[slot(block)] if src is None else src, dst_ref=wall_ref.at[slot(block)],
                send_sem=send_sems.at[sem], recv_sem=recv_sems.at[sem], device_id=to, device_id_type=MESH)

        own = pltpu.make_async_copy(wsh_ref, wall_ref.at[slot(me)], local_sems.at[0])
        first = [copy(0, me, sibling, src=wsh_ref)] + [copy(1 + j, me, (*q, mc), src=wsh_ref)
                                                       for j, q in enumerate(chips)]
        passed = [copy(4 + j, (*q, mc), sibling) for j, q in enumerate(chips)]

        def load(kk):
            src = wsh_ref if kk == 0 else wall_ref.at[order_ref[kk]]
            return pltpu.make_async_copy(src, wbuf.at[kk % 2], local_sems.at[1 + kk % 2])

        def arrived(kk):
            if kk == 1:
                copy(0, sibling, me).wait_recv()
            elif kk in (2, 4, 6):
                j = kk // 2 - 1
                copy(1 + j, (*chips[j], mc), me).wait_recv()
                passed[j].start()
                if j < 2:
                    first[2 + j].start()
            else:
                j = {3: 1, 5: 0, 7: 2}[kk]
                copy(4 + {1: 0, 0: 1, 2: 2}[j], (*chips[j], 1 - mc), me).wait_recv()

        @pl.when((k == 0) & (m == 0))
        def _():
            own.start()
            for cp in first[:2]:
                cp.start()
            load(0).start()
            load(0).wait()

        for kk in range(1, NDEV):
            @pl.when((k == kk - 1) & (m == nm // 2))
            def _(kk=kk):
                arrived(kk)
                load(kk).start()

            @pl.when((k == kk) & (m == 0))
            def _(kk=kk):
                load(kk).wait()

        rows = pl.ds(pl.multiple_of(m * tm, tm), tm)

        @pl.when(k == 0)
        def _():
            h_all[rows, :] = f_modulate(x_ref[...], g_ref[...], sc_ref[...], sh_ref[...]).astype(BF16)

        p_ref[...] = _dot(h_all[rows, :], wbuf[k % 2], _NN) + b_ref[...]

        @pl.when((k == NDEV - 1) & (m == nm - 1))
        def _():
            for cp in first + passed:
                cp.wait_send()
            own.wait()
            out = pltpu.make_async_copy(h_all, h_ref, local_sems.at[0])
            out.start()
            out.wait()

    vec = pl.BlockSpec((1, D), lambda k, m, o: (0, 0))
    return pl.pallas_call(
        body, name="in_proj",
        out_shape=(jax.ShapeDtypeStruct((t, NDEV * D), F32), jax.ShapeDtypeStruct((t, D), BF16),
                   jax.ShapeDtypeStruct((NDEV, D, D), BF16)),
        grid_spec=pltpu.PrefetchScalarGridSpec(
            num_scalar_prefetch=1, grid=(NDEV, nm),
            in_specs=[pl.BlockSpec((tm, D), lambda k, m, o: (jnp.where(k == 0, m, 0), 0)), vec, vec, vec,
                      pl.BlockSpec((1, D), lambda k, m, o: (0, o[k])), ANYSPEC],
            out_specs=(pl.BlockSpec((tm, D), lambda k, m, o: (m, o[k])), ANYSPEC, ANYSPEC),
            scratch_shapes=[pltpu.VMEM((2, D, D), BF16), pltpu.VMEM((t, D), BF16), pltpu.SemaphoreType.DMA((7,)),
                            pltpu.SemaphoreType.DMA((7,)), pltpu.SemaphoreType.DMA((3,))]),
        compiler_params=_params("arbitrary", "arbitrary"),
    )(order, x, gain, sc, sh, b_in, w_shard)


def hgrn_local(p, logits, shards):
    t = p.shape[0]
    nc = t // CH
    n = len(shards)

    def body(*refs):
        q_ref, f_ref, v_ref, l_ref = refs[:4]
        intra_ref, qe_ref, st_ref, dec_ref = refs[4 + n:8 + n]
        state = refs[8 + 2 * n]
        start, forward, finish = _gather2_ops(refs[4:4 + n], refs[8 + n:8 + 2 * n], *refs[9 + 2 * n:])
        step = pl.program_id(0)

        @pl.when(step == 0)
        def _():
            start()
            state[...] = jnp.zeros_like(state)

        intra, qe, ut, dec = f_hgrn_chunk(q_ref[...], f_ref[...], v_ref[...], l_ref[...])
        intra_ref[...] = intra
        qe_ref[...] = qe.astype(BF16)
        dec_ref[...] = dec
        s = state[...]
        st_ref[...] = s
        state[...] = dec * s + ut
        pl.when(step == (3 * nc) // 4)(forward)
        pl.when(step == nc - 1)(finish)

    col = lambda j: pl.BlockSpec((CH, D), lambda c: (c, j))
    res = pl.pallas_call(
        body, name="hgrn_local", grid=(nc,),
        out_shape=(jax.ShapeDtypeStruct((t, D), F32), jax.ShapeDtypeStruct((t, D), BF16),
                   jax.ShapeDtypeStruct((nc, DK, D), F32), jax.ShapeDtypeStruct((nc, 1, D), F32)) + _gathered(shards),
        in_specs=[col(0), col(1), col(2), pl.BlockSpec((2, D), lambda c: (0, 0))] + [ANYSPEC] * n,
        out_specs=(pl.BlockSpec((CH, D), lambda c: (c, 0)), pl.BlockSpec((CH, D), lambda c: (c, 0)),
                   pl.BlockSpec((None, DK, D), lambda c: (c, 0, 0)), pl.BlockSpec((None, 1, D), lambda c: (c, 0, 0)))
        + (ANYSPEC,) * n,
        scratch_shapes=[pltpu.VMEM((DK, D), F32)] + _comm_sems(n, True),
        compiler_params=_params("arbitrary"),
    )(p, p, p, logits, *shards)
    return res[:4] + (res[4:],)


def _conv_tile(t):
    return min(256, t)


def conv_forward(p, dw, db, ln_g, ln_b, shards):
    t = p.shape[0]
    tm = _conv_tile(t)
    per = tm // HALO
    n = len(shards)
    nt = t // tm

    def body(*refs):
        cv_ref, cg_ref, cvp_ref, cgp_ref, dw_ref, db_ref, g_ref, b_ref = refs[:8]
        yc_ref, act_ref = refs[8 + n:10 + n]
        uext, ush = refs[10 + 2 * n:12 + 2 * n]
        start, forward, finish = _gather2_ops(refs[8:8 + n], refs[10 + n:10 + 2 * n], *refs[12 + 2 * n:])
        step = pl.program_id(0)
        pl.when(step == 0)(start)
        pl.when(step == nt - 1)(forward)
        uext[0:HALO, :] = jnp.where(step == 0, 0.0, f_glu(cvp_ref[...], cgp_ref[...]))
        uext[HALO:HALO + tm, :] = f_glu(cv_ref[...], cg_ref[...])
        _shift_stack(uext, ush, tm + HALO)
        acc = jnp.zeros((tm, D), F32) + db_ref[...]
        for w in range(KW):
            acc = acc + dw_ref[w:w + 1, :] * _shifted(ush, HALO - KW + 1 + w, tm)
        yc_ref[...] = acc
        act_ref[...] = f_conv_act(acc, g_ref[...], b_ref[...]).astype(BF16)
        pl.when(step == nt - 1)(finish)

    vec = pl.BlockSpec((1, D), lambda m: (0, 0))
    prev = lambda j: pl.BlockSpec((HALO, D), lambda m: (jnp.maximum(m * per - 1, 0), j))
    res = pl.pallas_call(
        body, name="conv_forward", grid=(nt,),
        out_shape=(jax.ShapeDtypeStruct((t, D), F32), jax.ShapeDtypeStruct((t, D), BF16)) + _gathered(shards),
        in_specs=[pl.BlockSpec((tm, D), lambda m: (m, 4)), pl.BlockSpec((tm, D), lambda m: (m, 5)), prev(4), prev(5),
                  pl.BlockSpec((HALO, D), lambda m: (0, 0)), vec, vec, vec] + [ANYSPEC] * n,
        out_specs=(pl.BlockSpec((tm, D), lambda m: (m, 0)), pl.BlockSpec((tm, D), lambda m: (m, 0))) + (ANYSPEC,) * n,
        scratch_shapes=[pltpu.VMEM((HALO + tm, D), F32), pltpu.VMEM((8, tm + HALO, D), F32)] + _comm_sems(n, True),
        compiler_params=_params("arbitrary"),
    )(p, p, p, p, dw, db, ln_g, ln_b, *shards)
    return res[0], res[1], res[2:]


def _mix_tile(t):
    return min(256, t)


def _mix_forward_tile(x_ref, og_ref, ga_ref, gb_ref, qe_ref, intra_ref, st_ref, act_ref, wa_ref, wb_ref, wo_ref,
                      hg_ref, rows):
    o = _inter(qe_ref[...], st_ref, rows) + intra_ref[...]
    a = f_head_out(o, og_ref[...], hg_ref[...])
    ya = _dot(a.astype(BF16), wa_ref[...], _NN)
    yb = _dot(act_ref[...], wb_ref[...], _NN)
    merged = f_merge(ga_ref[...], gb_ref[...], ya, yb)
    y = _dot(merged.astype(BF16), wo_ref[...], _NN)
    return o, a, ya, yb, merged, y


def _mix_specs(tm, tiles=None):
    at = (lambda m: m) if tiles is None else (lambda m: tiles - 1 - m)
    col = lambda j: pl.BlockSpec((tm, D), lambda m: (at(m), j))
    row = pl.BlockSpec((tm, D), lambda m: (at(m), 0))
    per_chunk = lambda rows: pl.BlockSpec((tm // CH, rows, D), lambda m: (at(m), 0, 0))
    return col, row, pl.BlockSpec((1, D), lambda m: (0, 0)), per_chunk


def mix_forward(x, p, qe, intra, st_prev, act, wa, wb, wo, hg, g1, post):
    t = x.shape[0]
    tm = min(512, t)

    def body(x_ref, og_ref, ga_ref, gb_ref, qe_ref, intra_ref, st_ref, act_ref, wa_ref, wb_ref, wo_ref, hg_ref,
             g1_ref, post_ref, x1_ref):
        y = _mix_forward_tile(x_ref, og_ref, ga_ref, gb_ref, qe_ref, intra_ref, st_ref, act_ref, wa_ref, wb_ref,
                              wo_ref, hg_ref, tm)[-1]
        x1_ref[...] = f_residual(x_ref[...], y, g1_ref[...], post_ref[...])

    col, row, vec, per_chunk = _mix_specs(tm)
    return pl.pallas_call(
        body, name="mix_forward", grid=(t // tm,),
        out_shape=jax.ShapeDtypeStruct((t, D), F32),
        in_specs=[row, col(3), col(6), col(7), row, row, per_chunk(DK), row, VSPEC, VSPEC, VSPEC, vec, vec, vec],
        out_specs=row,
        compiler_params=_params("parallel"),
    )(x, p, p, p, qe, intra, st_prev, act, wa, wb, wo, hg, g1, post)


def ffn_forward_backward(x1, target, w1, w2, pre, sc, sh, g2, post):
    t = x1.shape[0]
    tm = min(512, t)
    nb = w1.shape[0]
    fb = w1.shape[2]

    def body(x_ref, tg_ref, w1_ref, w2_ref, pre_ref, sc_ref, sh_ref, g2_ref, post_ref,
             dx_ref, h2_ref, r_ref, dz_ref, dy2_ref, acc_ref, z_sc):
        @pl.when(pl.program_id(0) == 0)
        def _():
            acc_ref[...] = jnp.zeros_like(acc_ref)

        x1v = x_ref[...]
        h2, vjp_h = jax.vjp(f_modulate, x1v, pre_ref[...], sc_ref[...], sh_ref[...])
        h2b = h2.astype(BF16)
        h2_ref[...] = h2b
        y2 = jnp.zeros((tm, D), F32)
        for n in range(nb):
            z = _dot(h2b, w1_ref[n], _NN)
            z_sc[:, n * fb:(n + 1) * fb] = z
            r = jnp.square(jnp.maximum(z, 0.0)).astype(BF16)
            r_ref[:, n * fb:(n + 1) * fb] = r
            y2 = y2 + _dot(r, w2_ref[n], _NN)
        out, vjp_r = jax.vjp(f_residual, x1v, y2, g2_ref[...], post_ref[...])
        err = out - tg_ref[...]
        tok = jnp.mean(jnp.square(err), axis=-1, keepdims=True)
        acc_ref[5:6, :] += 0.5 * jnp.sum(tok, axis=0, keepdims=True)
        dx_a, dy2, dg2, dpost = vjp_r(err * (1.0 / D))
        dy2b = dy2.astype(BF16)
        dy2_ref[...] = dy2b
        dh2 = jnp.zeros((tm, D), F32)
        for n in range(nb):
            dr = _dot(dy2b, w2_ref[n], _NT)
            dz = (dr * (2.0 * jnp.maximum(z_sc[:, n * fb:(n + 1) * fb], 0.0))).astype(BF16)
            dz_ref[:, n * fb:(n + 1) * fb] = dz
            dh2 = dh2 + _dot(dz, w1_ref[n], _NT)
        dx_b, dpre, dsc, dsh = vjp_h(dh2)
        dx_ref[...] = dx_a + dx_b
        acc_ref[0:1, :] += dpre
        acc_ref[1:2, :] += dpost
        acc_ref[2:3, :] += dsc
        acc_ref[3:4, :] += dsh
        acc_ref[4:5, :] += dg2

    row = pl.BlockSpec((tm, D), lambda m: (m, 0))
    wide = pl.BlockSpec((tm, DFF), lambda m: (m, 0), pipeline_mode=pl.Buffered(1))
    vec = pl.BlockSpec((1, D), lambda m: (0, 0))
    return pl.pallas_call(
        body, name="ffn_forward_backward", grid=(t // tm,),
        out_shape=(jax.ShapeDtypeStruct((t, D), F32), jax.ShapeDtypeStruct((t, D), BF16),
                   jax.ShapeDtypeStruct((t, DFF), BF16), jax.ShapeDtypeStruct((t, DFF), BF16),
                   jax.ShapeDtypeStruct((t, D), BF16), jax.ShapeDtypeStruct((8, D), F32)),
        in_specs=[row, row, VSPEC, VSPEC, vec, vec, vec, vec, vec],
        out_specs=(row, row, wide, wide, row, pl.BlockSpec((8, D), lambda m: (0, 0))),
        scratch_shapes=[pltpu.VMEM((tm, DFF), F32)],
        compiler_params=_params("arbitrary"),
    )(x1, target, w1, w2, pre, sc, sh, g2, post)


def mix_backward(x, p, qe, intra, st_prev, act, wa, wb, wo, hg, g1, post, dx1, dec):
    t = x.shape[0]
    tm = _mix_tile(t)
    nc = t // CH

    def body(x_ref, og_ref, ga_ref, gb_ref, qe_ref, intra_ref, st_ref, act_ref, wa_ref, wb_ref, wo_ref, hg_ref,
             g1_ref, post_ref, dx1_ref, dec_ref,
             dp_ref, do_ref, dqe_ref, gt_ref, dact_ref, a_ref, mg_ref, dya_ref, dyb_ref, dy_ref, acc_ref, gstate):
        @pl.when(pl.program_id(0) == 0)
        def _():
            acc_ref[...] = jnp.zeros_like(acc_ref)
            gstate[...] = jnp.zeros_like(gstate)

        o, a, ya, yb, merged, y = _mix_forward_tile(x_ref, og_ref, ga_ref, gb_ref, qe_ref, intra_ref, st_ref, act_ref,
                                                    wa_ref, wb_ref, wo_ref, hg_ref, tm)
        a_ref[...] = a.astype(BF16)
        mg_ref[...] = merged.astype(BF16)
        _, vjp_r = jax.vjp(f_residual, x_ref[...], y, g1_ref[...], post_ref[...])
        _, dy, dg1, dpost = vjp_r(dx1_ref[...])
        dyb16 = dy.astype(BF16)
        dy_ref[...] = dyb16
        dmerged = _dot(dyb16, wo_ref[...], _NT)
        _, vjp_m = jax.vjp(f_merge, ga_ref[...], gb_ref[...], ya, yb)
        dga, dgb, dya, dyb = vjp_m(dmerged)
        dp_ref[:, D:2 * D] = dga.astype(BF16)
        dp_ref[:, 2 * D:3 * D] = dgb.astype(BF16)
        dya16, dyb16b = dya.astype(BF16), dyb.astype(BF16)
        dya_ref[...] = dya16
        dyb_ref[...] = dyb16b
        da = _dot(dya16, wa_ref[...], _NT)
        dact_ref[...] = _dot(dyb16b, wb_ref[...], _NT)
        _, vjp_a = jax.vjp(f_head_out, o, og_ref[...], hg_ref[...])
        do, dog, dhg = vjp_a(da)
        dp_ref[:, 0:D] = dog.astype(BF16)
        do_ref[...] = do
        do16 = do.astype(BF16)
        qe16 = qe_ref[...]
        for ci in reversed(range(tm // CH)):
            st = st_ref[ci].astype(BF16)
            rows = slice(ci * CH, (ci + 1) * CH)
            dqe, vt = [], []
            for h in range(HEADS):
                sl = slice(h * DK, (h + 1) * DK)
                dqe.append(_dot(do16[rows, sl], st[:, sl], _NN))
                vt.append(_dot(do16[rows, sl], qe16[rows, sl], _TN))
            dqe_ref[rows, :] = jnp.concatenate(dqe, axis=1)
            g = gstate[...]
            gt_ref[ci] = g
            gstate[...] = dec_ref[ci] * g + jnp.concatenate(vt, axis=1)
        acc_ref[0:1, :] += dg1
        acc_ref[1:2, :] += dpost
        acc_ref[2:3, :] += dhg
        acc_ref[3:4, :] += jnp.sum(dog, axis=0, keepdims=True)
        acc_ref[4:5, :] += jnp.sum(dga, axis=0, keepdims=True)
        acc_ref[5:6, :] += jnp.sum(dgb, axis=0, keepdims=True)

    nt = t // tm
    col, row, vec, per_chunk = _mix_specs(tm, nt)
    b16 = jax.ShapeDtypeStruct((t, D), BF16)
    f32 = jax.ShapeDtypeStruct((t, D), F32)
    return pl.pallas_call(
        body, name="mix_backward", grid=(nt,),
        out_shape=(jax.ShapeDtypeStruct((t, NDEV * D), BF16), f32, f32, jax.ShapeDtypeStruct((nc, DK, D), F32), f32,
                   b16, b16, b16, b16, b16, jax.ShapeDtypeStruct((8, D), F32)),
        in_specs=[row, col(3), col(6), col(7), row, row, per_chunk(DK), row, VSPEC, VSPEC, VSPEC, vec, vec, vec, row,
                  per_chunk(1)],
        out_specs=(pl.BlockSpec((tm, 3 * D), lambda m: (nt - 1 - m, 0)), row, row, per_chunk(DK), row, row, row, row,
                   row, row, pl.BlockSpec((8, D), lambda m: (0, 0))),
        scratch_shapes=[pltpu.VMEM((DK, D), F32)],
        compiler_params=_params("arbitrary"),
    )(x, p, p, p, qe, intra, st_prev, act, wa, wb, wo, hg, g1, post, dx1, dec)


def conv_backward(p, yc, dact, dw, ln_g, ln_b, blocks, dp):
    t = p.shape[0]
    tm = _conv_tile(t)
    per = tm // HALO
    nt = t // tm
    n = len(blocks)

    def body(*refs):
        cv_ref, cg_ref, cvp_ref, cgp_ref, yc_ref, ycn_ref, da_ref, dan_ref, dw_ref, g_ref, b_ref = refs[:11]
        dp_ref, acc_ref, ddw_ref = refs[12 + n:15 + n]
        uext, dyext, ush, dysh, ddw8, du_sc = refs[15 + 2 * n:21 + 2 * n]
        start, finish = _exchange_ops(refs[11:11 + n], refs[15 + n:15 + 2 * n], *refs[21 + 2 * n:])
        m = pl.program_id(0)

        @pl.when(m == 0)
        def _():
            start()
            acc_ref[...] = jnp.zeros_like(acc_ref)
            ddw8[...] = jnp.zeros_like(ddw8)

        cv, cg = cv_ref[...], cg_ref[...]
        u, vjp_u = jax.vjp(f_glu, cv, cg)
        uext[0:HALO, :] = jnp.where(m == 0, 0.0, f_glu(cvp_ref[...], cgp_ref[...]))
        uext[HALO:HALO + tm, :] = u
        _shift_stack(uext, ush, tm + HALO)
        _, vjp_c = jax.vjp(f_conv_act, yc_ref[...], g_ref[...], b_ref[...])
        dyc, dg, db = vjp_c(da_ref[...])
        _, vjp_n = jax.vjp(f_conv_act, ycn_ref[...], g_ref[...], b_ref[...])
        dyn = vjp_n(dan_ref[...])[0]
        dyext[0:tm, :] = dyc
        dyext[tm:tm + HALO, :] = jnp.where(m == nt - 1, 0.0, dyn)
        _shift_stack(dyext, dysh, tm + HALO)
        rb = min(128, tm)
        for lt in range(D // LANE):
            ls = slice(lt * LANE, (lt + 1) * LANE)
            for r0 in range(0, tm, rb):
                du_l = jnp.zeros((rb, LANE), F32)
                for w in range(KW):
                    du_l = du_l + dw_ref[w:w + 1, ls] * _shifted(dysh, KW - 1 - w + r0, rb, ls)
                du_sc[r0:r0 + rb, ls] = du_l
                dyc_l = dyext[r0:r0 + rb, ls]
                for w in range(KW):
                    prod = dyc_l * _shifted(ush, HALO - KW + 1 + w + r0, rb, ls)
                    part = jnp.sum(prod.reshape(4, rb // 32, 8, LANE), axis=1)
                    ddw8[w, :, ls] += (part[0] + part[1]) + (part[2] + part[3])
        du = du_sc[...]

        @pl.when(m == nt - 1)
        def _():
            ddw_ref[...] = jnp.sum(ddw8[...], axis=1)

        dcv, dcg = vjp_u(du)
        dp_ref[:, 0:D] = dcv.astype(BF16)
        dp_ref[:, D:2 * D] = dcg.astype(BF16)
        acc_ref[0:1, :] += jnp.sum(dyc, axis=0, keepdims=True)
        acc_ref[1:2, :] += dg
        acc_ref[2:3, :] += db
        acc_ref[3:4, :] += jnp.sum(dcv, axis=0, keepdims=True)
        acc_ref[4:5, :] += jnp.sum(dcg, axis=0, keepdims=True)
        pl.when(m == nt - 1)(finish)

    vec = pl.BlockSpec((1, D), lambda m: (0, 0))
    row = pl.BlockSpec((tm, D), lambda m: (m, 0))
    prev = lambda j: pl.BlockSpec((HALO, D), lambda m: (jnp.maximum(m * per - 1, 0), j))
    nxt = pl.BlockSpec((HALO, D), lambda m: (jnp.minimum((m + 1) * per, t // HALO - 1), 0))
    res = pl.pallas_call(
        body, name="conv_backward", grid=(nt,),
        out_shape=(jax.ShapeDtypeStruct(dp.shape, dp.dtype), jax.ShapeDtypeStruct((8, D), F32),
                   jax.ShapeDtypeStruct((HALO, D), F32)) + _exchanged(blocks),
        in_specs=[pl.BlockSpec((tm, D), lambda m: (m, 4)), pl.BlockSpec((tm, D), lambda m: (m, 5)), prev(4), prev(5),
                  row, nxt, row, nxt, pl.BlockSpec((HALO, D), lambda m: (0, 0)), vec, vec] + [ANYSPEC] * (n + 1),
        out_specs=(pl.BlockSpec((tm, 2 * D), lambda m: (m, 3)), pl.BlockSpec((8, D), lambda m: (0, 0)),
                   pl.BlockSpec((HALO, D), lambda m: (0, 0))) + (ANYSPEC,) * n,
        scratch_shapes=[pltpu.VMEM((HALO + tm, D), F32), pltpu.VMEM((tm + HALO, D), F32),
                        pltpu.VMEM((8, tm + HALO, D), F32), pltpu.VMEM((8, tm + HALO, D), F32),
                        pltpu.VMEM((HALO, 8, D), F32), pltpu.VMEM((tm, D), F32)] + _comm_sems(n, False),
        input_output_aliases={11 + n: 0},
        compiler_params=_params("arbitrary"),
    )(p, p, p, p, yc, yc, dact, dact, dw, ln_g, ln_b, *blocks, dp)
    return res[:3] + (res[3:],)


def hgrn_backward(p, logits, do, dqe, gt, st_prev, blocks, dp):
    t = p.shape[0]
    nc = t // CH
    n = len(blocks)

    def body(*refs):
        q_ref, f_ref, v_ref, l_ref, do_ref, dqe_ref, gt_ref, st_ref = refs[:8]
        dp_ref, acc_ref = refs[9 + n:11 + n]
        start, finish = _exchange_ops(refs[8:8 + n], refs[11 + n:11 + 2 * n], *refs[11 + 2 * n:])

        @pl.when(pl.program_id(0) == 0)
        def _():
            start()
            acc_ref[...] = jnp.zeros_like(acc_ref)

        gt_v = gt_ref[...]
        ddec = jnp.sum(gt_v * st_ref[...], axis=0, keepdims=True)
        _, vjp = jax.vjp(f_hgrn_chunk, q_ref[...], f_ref[...], v_ref[...], l_ref[...])
        dq, df, dv, dl = vjp((do_ref[...], dqe_ref[...], gt_v, ddec))
        dp_ref[:, 0:D] = dq.astype(BF16)
        dp_ref[:, D:2 * D] = df.astype(BF16)
        dp_ref[:, 2 * D:3 * D] = dv.astype(BF16)
        acc_ref[0:2, :] += dl
        acc_ref[2:3, :] += jnp.sum(dq, axis=0, keepdims=True)
        acc_ref[3:4, :] += jnp.sum(df, axis=0, keepdims=True)
        acc_ref[4:5, :] += jnp.sum(dv, axis=0, keepdims=True)
        pl.when(pl.program_id(0) == nc - 1)(finish)

    col = lambda j: pl.BlockSpec((CH, D), lambda c: (c, j))
    row = pl.BlockSpec((CH, D), lambda c: (c, 0))
    stspec = pl.BlockSpec((None, DK, D), lambda c: (c, 0, 0))
    res = pl.pallas_call(
        body, name="hgrn_backward", grid=(nc,),
        out_shape=(jax.ShapeDtypeStruct(dp.shape, dp.dtype), jax.ShapeDtypeStruct((8, D), F32)) + _exchanged(blocks),
        in_specs=[col(0), col(1), col(2), pl.BlockSpec((2, D), lambda c: (0, 0)), row, row, stspec, stspec]
        + [ANYSPEC] * (n + 1),
        out_specs=(pl.BlockSpec((CH, 3 * D), lambda c: (c, 1)), pl.BlockSpec((8, D), lambda c: (0, 0)))
        + (ANYSPEC,) * n,
        scratch_shapes=_comm_sems(n, False),
        input_output_aliases={8 + n: 0},
        compiler_params=_params("arbitrary"),
    )(p, p, p, logits, do, dqe, gt, st_prev, *blocks, dp)
    return res[:2] + (res[2:],)


def in_proj_backward(dp, w_all, x, dx1, gain, sc, sh, blocks):
    t = x.shape[0]
    tm = min(512, t)
    nt = t // tm
    n = len(blocks)

    def body(*refs):
        dp_ref, w_ref, x_ref, dx1_ref, g_ref, sc_ref, sh_ref = refs[:7]
        gx_ref, acc_ref = refs[7 + n:9 + n]
        start = finish = lambda: None
        if n:
            start, finish = _exchange_ops(refs[7:7 + n], refs[9 + n:9 + 2 * n], *refs[9 + 2 * n:])

        @pl.when(pl.program_id(0) == 0)
        def _():
            start()
            acc_ref[...] = jnp.zeros_like(acc_ref)

        dh = jnp.zeros((tm, D), F32)
        for j in range(NDEV):
            dh = dh + _dot(dp_ref[:, j * D:(j + 1) * D], w_ref[DP_SPLIT[j]], _NT)
        _, vjp_h = jax.vjp(f_modulate, x_ref[...], g_ref[...], sc_ref[...], sh_ref[...])
        dx, dg, dsc, dsh = vjp_h(dh)
        gx_ref[...] = dx1_ref[...] + dx
        acc_ref[0:1, :] += dg
        acc_ref[1:2, :] += dsc
        acc_ref[2:3, :] += dsh
        pl.when(pl.program_id(0) == nt - 1)(finish)

    row = pl.BlockSpec((tm, D), lambda m: (m, 0))
    vec = pl.BlockSpec((1, D), lambda m: (0, 0))
    res = pl.pallas_call(
        body, name="in_proj_backward", grid=(nt,),
        out_shape=(jax.ShapeDtypeStruct((t, D), F32), jax.ShapeDtypeStruct((8, D), F32)) + _exchanged(blocks),
        in_specs=[pl.BlockSpec((tm, NDEV * D), lambda m: (m, 0)), VSPEC, row, row, vec, vec, vec] + [ANYSPEC] * n,
        out_specs=(row, pl.BlockSpec((8, D), lambda m: (0, 0))) + (ANYSPEC,) * n,
        scratch_shapes=_comm_sems(n, False) if n else [],
        compiler_params=_params("arbitrary"),
    )(dp, w_all, x, dx1, gain, sc, sh, *blocks)
    return res[0], res[1], res[2:]


def weight_grad(a, b, nblk, ka, bn, a_blocked, name, b_col=lambda n: n):
    t = a.shape[0]
    tk = min(2048, t)
    nk = t // tk

    def body(a_ref, b_ref, f_ref, h_ref, acc):
        k = pl.program_id(1)

        @pl.when(k == 0)
        def _():
            acc[...] = jnp.zeros_like(acc)

        acc[...] += _dot(a_ref[...], b_ref[...], _TN)

        @pl.when(k == nk - 1)
        def _():
            f_ref[...] = acc[...]
            h_ref[...] = acc[...].astype(BF16)

    a_idx = (lambda n, k: (k, n)) if a_blocked else (lambda n, k: (k, 0))
    b_idx = (lambda n, k: (k, 0)) if a_blocked else (lambda n, k: (k, b_col(n)))
    out = pl.BlockSpec((None, ka, bn), lambda n, k: (n, 0, 0))
    return pl.pallas_call(
        body, name=name, grid=(nblk, nk),
        out_shape=(jax.ShapeDtypeStruct((nblk, ka, bn), F32), jax.ShapeDtypeStruct((nblk, ka, bn), BF16)),
        in_specs=[pl.BlockSpec((tk, ka), a_idx), pl.BlockSpec((tk, bn), b_idx)],
        out_specs=(out, out),
        scratch_shapes=[pltpu.VMEM((ka, bn), F32)],
        compiler_params=_params("parallel", "arbitrary"),
    )(a, b)


def ada_backward(call_t, dmod_cols, w, m, v):
    def body(c_ref, d_ref, w_ref, m_ref, v_ref, g_ref, dl_ref, nm_ref, nv_ref):
        ct, dm = c_ref[...], d_ref[...]
        g = ct[:, 0:1] * dm[0:1, :]
        for r in range(1, NDEV):
            g = g + ct[:, r:r + 1] * dm[r:r + 1, :]
        g_ref[...] = g
        dl_ref[...], nm_ref[...], nv_ref[...] = _adamw(w_ref[...], g, m_ref[...], v_ref[...])

    s = jax.ShapeDtypeStruct(w.shape, F32)
    return pl.pallas_call(
        body, name="ada_backward", out_shape=(s, s, s, s),
        in_specs=[VSPEC] * 5, out_specs=(VSPEC,) * 4, compiler_params=_params(),
    )(call_t, dmod_cols, w, m, v)


def adamw_small(total, ddw_mine, recipes, ws, ms, vs):
    n = len(ws)

    def body(*refs):
        tot, ddw = refs[0], refs[1]
        w_refs, m_refs, v_refs = refs[2:2 + n], refs[2 + n:2 + 2 * n], refs[2 + 2 * n:2 + 3 * n]
        outs = refs[2 + 3 * n:2 + 7 * n]
        loss_ref = refs[2 + 7 * n]
        for i, rec in enumerate(recipes):
            if rec == "dw":
                g = ddw[...]
            elif isinstance(rec, tuple):
                g = tot[rec[0]:rec[1], :]
            else:
                g = jnp.concatenate([tot[r:r + 1, :] for r in rec], axis=1) if len(rec) > 1 else tot[rec[0]:rec[0] + 1, :]
            dl, nm, nv = _adamw(w_refs[i][...], g, m_refs[i][...], v_refs[i][...])
            outs[4 * i][...] = g
            outs[4 * i + 1][...] = dl
            outs[4 * i + 2][...] = nm
            outs[4 * i + 3][...] = nv
        loss_ref[...] = tot[LOSS_ROW:LOSS_ROW + 1, 0:128]

    shapes = []
    for w in ws:
        shapes += [jax.ShapeDtypeStruct(w.shape, F32)] * 4
    res = pl.pallas_call(
        body, name="adamw_small", out_shape=tuple(shapes) + (jax.ShapeDtypeStruct((1, 128), F32),),
        in_specs=[VSPEC] * (2 + 3 * n), out_specs=(VSPEC,) * (4 * n + 1), compiler_params=_params(),
    )(total, ddw_mine, *ws, *ms, *vs)
    return [res[4 * i:4 * i + 4] for i in range(n)], res[4 * n]


def reduce_and_adamw(me, w, g_all, g_recv, m, v, name):
    r, c = w.shape
    br = min(256, r)

    def body(me_ref, w_ref, go_ref, gr_ref, m_ref, v_ref, g_ref, dl_ref, nm_ref, nv_ref):
        g = go_ref[...]
        for k in range(NDEV - 1):
            g = g + gr_ref[k].astype(F32)
        g_ref[...] = g
        dl_ref[...], nm_ref[...], nv_ref[...] = _adamw(w_ref[...], g, m_ref[...], v_ref[...])

    blk = pl.BlockSpec((br, c), lambda i, me_ref: (i, 0))
    s = jax.ShapeDtypeStruct(w.shape, F32)
    return pl.pallas_call(
        body, name=name, out_shape=(s, s, s, s),
        grid_spec=pltpu.PrefetchScalarGridSpec(
            num_scalar_prefetch=1, grid=(r // br,),
            in_specs=[blk, pl.BlockSpec((None, br, c), lambda i, me_ref: (me_ref[0], i, 0)),
                      pl.BlockSpec((NDEV - 1, br, c), lambda i, me_ref: (0, i, 0)), blk, blk],
            out_specs=(blk, blk, blk, blk)),
        compiler_params=_params("parallel"),
    )(me, w, g_all, g_recv, m, v)


def kernel(x, c, w_ada, b_ada, pre_norm_tm, post_norm_tm, pre_norm_cm, post_norm_cm, w_in, b_in, hg_lb_logits, hg_norm, conv_dw, conv_db, conv_ln_g, conv_ln_b, w_br_a, w_br_b, w_out, w_ff1, w_ff2, loss_target, m_w_ada, m_b_ada, m_pre_norm_tm, m_post_norm_tm, m_pre_norm_cm, m_post_norm_cm, m_w_in, m_b_in, m_hg_lb_logits, m_hg_norm, m_conv_dw, m_conv_db, m_conv_ln_g, m_conv_ln_b, m_w_br_a, m_w_br_b, m_w_out, m_w_ff1, m_w_ff2, v_w_ada, v_b_ada, v_pre_norm_tm, v_post_norm_tm, v_pre_norm_cm, v_post_norm_cm, v_w_in, v_b_in, v_hg_lb_logits, v_hg_norm, v_conv_dw, v_conv_db, v_conv_ln_g, v_conv_ln_b, v_w_br_a, v_w_br_b, v_w_out, v_w_ff1, v_w_ff2):
    t = x.shape[1]
    me = 4 * lax.axis_index("x") + 2 * lax.axis_index("y") + lax.axis_index("c")
    xs = x[0]
    tgt = loss_target[0]

    mod, call, dw_all = ada_forward(c, w_ada[0], b_ada, conv_dw[0])
    sh1, sc1, g1, sh2, sc2, g2 = [mod[:, i * D:(i + 1) * D] for i in range(6)]
    dw = jnp.pad(dw_all.transpose(1, 0, 2).reshape(KW, D), ((0, HALO - KW), (0, 0)))

    p, h, win_all = in_proj(xs, pre_norm_tm, sc1, sh1, w_in[0].astype(BF16), b_in)
    yc, act, (w1_all,) = conv_forward(p, dw, conv_db, conv_ln_g, conv_ln_b, [w_ff1[0].astype(BF16)])
    intra, qe, st_prev, dec, (w2_all, wa_all, wb_all, wo_all) = hgrn_local(
        p, hg_lb_logits, [w_ff2[0].astype(BF16), w_br_a[0].astype(BF16), w_br_b[0].astype(BF16),
                          w_out[0].astype(BF16)])
    wa, wb, wo = wa_all.reshape(D, D), wb_all.reshape(D, D), wo_all.reshape(D, D)
    x1 = mix_forward(xs, p, qe, intra, st_prev, act, wa, wb, wo, hg_norm, g1, post_norm_tm)

    dx1, h2, r, dz, dy2, acc_ffn = ffn_forward_backward(x1, tgt, w1_all, w2_all, pre_norm_cm, sc2, sh2, g2,
                                                        post_norm_cm)
    rows = D // NDEV
    g1_f, g1_h = weight_grad(h2, dz, NDEV, D, DFF // NDEV, False, "grad_w_ff1")
    g2_f, g2_h = weight_grad(r, dy2, 4, D, D, True, "grad_w_ff2")
    g2_f, g2_h = g2_f.reshape(NDEV, DFF // NDEV, D), g2_h.reshape(NDEV, DFF // NDEV, D)

    (dp, do, dqe, gt, dact, a16, mg16, dya, dyb, dy, acc_mix) = mix_backward(
        xs, p, qe, intra, st_prev, act, wa, wb, wo, hg_norm, g1, post_norm_tm, dx1, dec)
    ga_f, ga_h = weight_grad(a16, dya, 1, D, D, False, "grad_w_br_a")
    gb_f, gb_h = weight_grad(act, dyb, 1, D, D, False, "grad_w_br_b")
    go_f, go_h = weight_grad(mg16, dy, 1, D, D, False, "grad_w_out")
    dp, acc_conv, ddw, (r_ff1, r_ff2) = conv_backward(p, yc, dact, dw, conv_ln_g, conv_ln_b, [g1_h, g2_h], dp)
    dp, acc_hg, (r_a, r_b, r_o) = hgrn_backward(
        p, hg_lb_logits, do, dqe, gt, st_prev,
        [ga_h.reshape(NDEV, rows, D), gb_h.reshape(NDEV, rows, D), go_h.reshape(NDEV, rows, D)], dp)
    gin_f, gin_h = weight_grad(
        h, dp, NDEV, D, D, False, "grad_w_in",
        b_col=lambda n: jnp.where(n < 3, n + 3, jnp.where(n == 3, 0, jnp.where(n < 6, n + 2, n - 5))))
    in_send, in_recv, gin_thru, landing, token = exchange_start(gin_h)
    grad_x, acc_in, _ = in_proj_backward(dp, win_all, xs, dx1, pre_norm_tm, sc1, sh1 + token[0:1, 0:1], [])

    own = [ga_f.reshape(NDEV, rows, D), gb_f.reshape(NDEV, rows, D), go_f.reshape(NDEV, rows, D), g1_f, g2_f]
    recv = [r_a, r_b, r_o, r_ff1, r_ff2]
    big = {}
    names = ["w_br_a", "w_br_b", "w_out", "w_ff1", "w_ff2"]
    ws = [w_br_a, w_br_b, w_out, w_ff1, w_ff2]
    ms = [m_w_br_a, m_w_br_b, m_w_out, m_w_ff1, m_w_ff2]
    vs = [v_w_br_a, v_w_br_b, v_w_out, v_w_ff1, v_w_ff2]
    me1 = me.astype(jnp.int32).reshape(1)
    for i, nm in enumerate(names):
        big[nm] = [o[None] for o in reduce_and_adamw(me1, ws[i][0], own[i], recv[i], ms[i][0], vs[i][0], "adamw_" + nm)]
    big, grad_x, acc_in = lax.optimization_barrier((big, grad_x, acc_in))
    r_in = exchange_wait(in_send, in_recv, gin_thru, landing, acc_in)
    r_in, acc_in = lax.optimization_barrier((r_in, acc_in))
    big["w_in"] = [o[None] for o in reduce_and_adamw(me1, w_in[0], gin_f, r_in, m_w_in[0], v_w_in[0], "adamw_w_in")]

    dmod_rows = [2, 1, 8, 19, 18, 20]
    total, kept = gather_and_sum_rows([acc_in, acc_mix, acc_ffn, acc_hg, acc_conv, ddw], dmod_rows)
    dmod_all = kept[:, 0:6, :].reshape(NDEV, 6 * D)
    wcols = w_ada.shape[2]
    gwa, dwa, nmwa, nvwa = ada_backward(call.T, lax.dynamic_slice_in_dim(dmod_all, me * wcols, wcols, axis=1),
                                        w_ada[0], m_w_ada[0], v_w_ada[0])
    ddw_mine = lax.dynamic_slice_in_dim(total[40:40 + KW], me * (D // NDEV), D // NDEV, axis=1)

    small_names = ["b_ada", "pre_norm_tm", "post_norm_tm", "pre_norm_cm", "post_norm_cm", "b_in", "hg_lb_logits",
                   "hg_norm", "conv_db", "conv_ln_g", "conv_ln_b", "conv_dw"]
    recipes = [dmod_rows, [0], [9], [16], [17], [26, 27, 28, 11, 35, 36, 12, 13], (24, 26), [10], [32], [33], [34],
               "dw"]
    small_w = [b_ada, pre_norm_tm, post_norm_tm, pre_norm_cm, post_norm_cm, b_in, hg_lb_logits, hg_norm, conv_db,
               conv_ln_g, conv_ln_b, conv_dw[0]]
    small_m = [m_b_ada, m_pre_norm_tm, m_post_norm_tm, m_pre_norm_cm, m_post_norm_cm, m_b_in, m_hg_lb_logits,
               m_hg_norm, m_conv_db, m_conv_ln_g, m_conv_ln_b, m_conv_dw[0]]
    small_v = [v_b_ada, v_pre_norm_tm, v_post_norm_tm, v_pre_norm_cm, v_post_norm_cm, v_b_in, v_hg_lb_logits,
               v_hg_norm, v_conv_db, v_conv_ln_g, v_conv_ln_b, v_conv_dw[0]]
    small_out, loss_row = adamw_small(total, ddw_mine, recipes, small_w, small_m, small_v)
    loss = loss_row[0, 0]
    sm = {nm: list(o) for nm, o in zip(small_names, small_out)}
    sm["conv_dw"] = [o[None] for o in sm["conv_dw"]]

    order = ["w_ada", "b_ada", "pre_norm_tm", "post_norm_tm", "pre_norm_cm", "post_norm_cm", "w_in", "b_in",
             "hg_lb_logits", "hg_norm", "conv_dw", "conv_db", "conv_ln_g", "conv_ln_b", "w_br_a", "w_br_b", "w_out",
             "w_ff1", "w_ff2"]
    res = dict(sm)
    res.update(big)
    res["w_ada"] = [gwa[None], dwa[None], nmwa[None], nvwa[None]]
    outs = [loss, grad_x[None]]
    for j in range(4):
        outs += [res[nm][j] for nm in order]
    return tuple(outs)
```

```python
import functools

import jax
import jax.numpy as jnp
from jax import lax
from jax.experimental import pallas as pl
from jax.experimental.pallas import tpu as pltpu

F32 = jnp.float32
BF16 = jnp.bfloat16
MESH = pl.DeviceIdType.MESH
AXES = ("x", "y", "c")

D = 1024
HEADS = 8
DK = 128
LANE = 128
CH = 128
LEVELS = 7
KW = 31
HALO = 32
DFF = 4096
NDEV = 8
EPS = 1e-6
DP_SPLIT = (3, 6, 7, 0, 1, 2, 4, 5)
LOSS_ROW = 21
ADAM_LR, ADAM_B1, ADAM_B2, ADAM_EPS, ADAM_WD, ADAM_STEP = 0.001, 0.9, 0.999, 1e-08, 0.01, 10
VMEM_LIMIT = 58 * 1024 * 1024

_NN = (((1,), (0,)), ((), ()))
_NT = (((1,), (1,)), ((), ()))
_TN = (((0,), (0,)), ((), ()))

VSPEC = pl.BlockSpec(memory_space=pltpu.VMEM)
ANYSPEC = pl.BlockSpec(memory_space=pl.ANY)


def _params(*sem):
    return pltpu.CompilerParams(dimension_semantics=sem or None, vmem_limit_bytes=VMEM_LIMIT)


def _dot(a, b, dims):
    return lax.dot_general(a, b, dims, preferred_element_type=F32)


@jax.custom_vjp
def mm_nn(a, b):
    return _dot(a.astype(BF16), b.astype(BF16), _NN)


def _mm_nn_fwd(a, b):
    ab, bb = a.astype(BF16), b.astype(BF16)
    return _dot(ab, bb, _NN), (ab, bb)


def _mm_nn_bwd(res, ct):
    ab, bb = res
    cb = ct.astype(BF16)
    return _dot(cb, bb, _NT), _dot(ab, cb, _TN)


mm_nn.defvjp(_mm_nn_fwd, _mm_nn_bwd)


@jax.custom_vjp
def mm_nt(a, b):
    return _dot(a.astype(BF16), b.astype(BF16), _NT)


def _mm_nt_fwd(a, b):
    ab, bb = a.astype(BF16), b.astype(BF16)
    return _dot(ab, bb, _NT), (ab, bb)


def _mm_nt_bwd(res, ct):
    ab, bb = res
    cb = ct.astype(BF16)
    return _dot(cb, bb, _NN), _dot(cb, ab, _TN)


mm_nt.defvjp(_mm_nt_fwd, _mm_nt_bwd)


@jax.custom_vjp
def mm_tn(a, b):
    return _dot(a.astype(BF16), b.astype(BF16), _TN)


def _mm_tn_fwd(a, b):
    ab, bb = a.astype(BF16), b.astype(BF16)
    return _dot(ab, bb, _TN), (ab, bb)


def _mm_tn_bwd(res, ct):
    ab, bb = res
    cb = ct.astype(BF16)
    return _dot(bb, cb, _NT), _dot(ab, cb, _NN)


mm_tn.defvjp(_mm_tn_fwd, _mm_tn_bwd)


def _rms(x):
    return x * lax.rsqrt(jnp.mean(x * x, axis=-1, keepdims=True) + EPS)


_sigmoid = jax.nn.sigmoid


def _silu(x):
    return x * _sigmoid(x)


def f_modulate(x, gain, sc, sh):
    return _rms(x) * gain * (1.0 + sc) + sh


def f_residual(x, y, gate, gain):
    return x + gate * (_rms(y) * gain)


def f_merge(ga, gb, ya, yb):
    return _sigmoid(ga) * ya + _sigmoid(gb) * yb


def f_head_out(o, og, hg):
    heads = [_rms(o[:, h * DK:(h + 1) * DK]) for h in range(HEADS)]
    return jnp.concatenate(heads, axis=1) * hg * _silu(og)


def f_conv_act(u, g, b):
    mu = jnp.mean(u, axis=-1, keepdims=True)
    var = jnp.mean(jnp.square(u - mu), axis=-1, keepdims=True)
    return _silu((u - mu) * lax.rsqrt(var + EPS) * g + b)


def f_glu(cv, cg):
    return cv * _sigmoid(cg)


def _split2(x):
    hi = x.astype(BF16)
    return hi, (x - hi.astype(F32)).astype(BF16)


def _tri(transposed):
    i = lax.broadcasted_iota(jnp.int32, (CH, CH), 1 if transposed else 0)
    t = lax.broadcasted_iota(jnp.int32, (CH, CH), 0 if transposed else 1)
    return jnp.where(t <= i, 1.0, 0.0).astype(BF16)


def _blocks3(x, rows):
    return x.reshape(CH // rows, rows, x.shape[-1])


def _mid_broadcast(b, lev):
    h = 1 << (LEVELS - 1 - lev)
    if h >= 4:
        x3 = _blocks3(b, 2 * h)
        return jnp.broadcast_to(x3[:, h - 1:h, :], x3.shape).reshape(b.shape)
    x3 = _blocks3(b, 8)
    sub = lax.broadcasted_iota(jnp.int32, x3.shape, 1)
    out = None
    for first in range(0, 8, 2 * h):
        piece = jnp.broadcast_to(x3[:, first + h - 1:first + h, :], x3.shape)
        out = piece if out is None else jnp.where(sub >= first, piece, out)
    return out.reshape(b.shape)


def _mid_scatter(d, lev):
    h = 1 << (LEVELS - 1 - lev)
    if h >= 4:
        x3 = _blocks3(d, 2 * h)
        row = lax.broadcasted_iota(jnp.int32, x3.shape, 1)
        total = jnp.sum(x3, axis=1, keepdims=True)
        return jnp.where(row == h - 1, total, 0.0).reshape(d.shape)
    x3 = _blocks3(d, 8)
    sub = lax.broadcasted_iota(jnp.int32, x3.shape, 1)
    out = jnp.zeros_like(x3)
    for first in range(0, 8, 2 * h):
        inside = (sub >= first) & (sub < first + 2 * h)
        total = jnp.sum(jnp.where(inside, x3, 0.0), axis=1, keepdims=True)
        out = jnp.where(sub == first + h - 1, total, out)
    return out.reshape(d.shape)


@jax.custom_vjp
def decay_sums(g):
    hi, lo = _split2(g)
    tri = _tri(False)
    b = _dot(tri, hi, _NN) + _dot(tri, lo, _NN)
    return (b,) + tuple(b - _mid_broadcast(b, lev) for lev in range(LEVELS))


def _decay_sums_fwd(g):
    return decay_sums(g), None


def _decay_sums_bwd(_, cts):
    db = cts[0]
    for lev in range(LEVELS):
        db = db + cts[1 + lev] - _mid_scatter(cts[1 + lev], lev)
    hi, lo = _split2(db)
    tri = _tri(True)
    return (_dot(tri, hi, _NN) + _dot(tri, lo, _NN),)


decay_sums.defvjp(_decay_sums_fwd, _decay_sums_bwd)


def _score_masks():
    i = lax.broadcasted_iota(jnp.int32, (CH, CH), 0)
    j = lax.broadcasted_iota(jnp.int32, (CH, CH), 1)
    masks = [i == j]
    for lev in range(LEVELS):
        sh = LEVELS - 1 - lev
        same = (i >> (sh + 1)) == (j >> (sh + 1))
        masks.append(same & (((i >> sh) & 1) == 1) & (((j >> sh) & 1) == 0))
    return [jnp.where(m, 1.0, 0.0) for m in masks]


def f_hgrn_chunk(q_r, f_r, v, logits):
    l0, l1 = logits[0:1, :], logits[1:2, :]
    mx = lax.stop_gradient(jnp.maximum(l0, l1))
    e0, e1 = jnp.exp(l0 - mx), jnp.exp(l1 - mx)
    lb = e0 / (e0 + e1)
    q = _silu(q_r)
    f = lb + (1.0 - lb) * _sigmoid(f_r)
    k = 1.0 - f
    sums = decay_sums(jnp.log(f))
    b = sums[0]
    btot = b[CH - 1:CH, :]
    qe = q * jnp.exp(b)
    ke = k * jnp.exp(btot - b)
    dec = jnp.exp(btot)
    qs, ks = [q], [k]
    row = lax.broadcasted_iota(jnp.int32, b.shape, 0)
    for lev in range(LEVELS):
        upper = ((row >> (LEVELS - 1 - lev)) & 1) == 1
        e = sums[1 + lev]
        qs.append(q * jnp.exp(jnp.where(upper, e, 0.0)))
        ks.append(k * jnp.exp(jnp.where(upper, 0.0, -e)))
    masks = _score_masks()
    intra, ut = [], []
    for h in range(HEADS):
        sl = slice(h * DK, (h + 1) * DK)
        sc = None
        for lev in range(LEVELS + 1):
            s = mm_nt(qs[lev][:, sl], ks[lev][:, sl]) * masks[lev]
            sc = s if sc is None else sc + s
        intra.append(mm_nn(sc, v[:, sl]))
        ut.append(mm_tn(v[:, sl], ke[:, sl]))
    return jnp.concatenate(intra, axis=1), qe, jnp.concatenate(ut, axis=1), dec


def _inter(qe_b, st_ref, rows):
    out = []
    for ci in range(rows // CH):
        st = st_ref[ci].astype(BF16)
        heads = [_dot(qe_b[ci * CH:(ci + 1) * CH, h * DK:(h + 1) * DK], st[:, h * DK:(h + 1) * DK], _NT)
                 for h in range(HEADS)]
        out.append(jnp.concatenate(heads, axis=1))
    return jnp.concatenate(out, axis=0)


def _shift_stack(src, dst, rows):
    dst[0, 0:rows, :] = src[0:rows, :]
    for b in range(1, 8):
        dst[b, 0:rows - 8, :] = src[pl.ds(b, rows - 8), :]


def _shifted(stack, offset, rows, lanes=slice(None)):
    return stack[offset % 8, pl.ds(8 * (offset // 8), rows), lanes]


def _adamw(w, g, m, v):
    m = ADAM_B1 * m + (1.0 - ADAM_B1) * g
    v = ADAM_B2 * v + (1.0 - ADAM_B2) * jnp.square(g)
    m_hat = m / (1.0 - ADAM_B1 ** ADAM_STEP)
    v_hat = v / (1.0 - ADAM_B2 ** ADAM_STEP)
    delta = -ADAM_LR * (m_hat / (jnp.sqrt(v_hat) + ADAM_EPS) + ADAM_WD * w)
    return delta, m, v


def _me():
    return lax.axis_index("x"), lax.axis_index("y"), lax.axis_index("c")


def _peer(k):
    x, y, c = _me()
    mask = k + 1
    px = (1 - x) if (mask >> 2) & 1 else x
    py = (1 - y) if (mask >> 1) & 1 else y
    pc = (1 - c) if mask & 1 else c
    return (px, py, pc), 4 * px + 2 * py + pc


def ada_forward(c, w_ada, b_ada, dw):
    wcols = w_ada.shape[1]

    def body(c_ref, w_ref, b_ref, dw_ref, mod_ref, call_ref, dwall_ref, part_ref, modp_ref, send_sems, recv_sems):
        x, y, cc = _me()
        me = 4 * x + 2 * y + cc
        call_ref[me] = _silu(c_ref[...])
        dwall_ref[me] = dw_ref[...]
        sends = []
        for k in range(NDEV - 1):
            dev, pidx = _peer(k)
            cp = pltpu.make_async_remote_copy(dwall_ref.at[me], dwall_ref.at[me], send_sems.at[2 * (NDEV - 1) + k],
                                              recv_sems.at[2 * (NDEV - 1) + k], device_id=dev, device_id_type=MESH)
            cp.start()
            sends.append(cp)
        for k in range(NDEV - 1):
            dev, _ = _peer(k)
            cp = pltpu.make_async_remote_copy(call_ref.at[me], call_ref.at[me], send_sems.at[k], recv_sems.at[k],
                                              device_id=dev, device_id_type=MESH)
            cp.start()
            sends.append(cp)
        for k in range(NDEV - 1):
            _, pidx = _peer(k)
            pltpu.make_async_remote_copy(call_ref.at[pidx], call_ref.at[pidx], send_sems.at[k], recv_sems.at[k],
                                         device_id=_peer(k)[0], device_id_type=MESH).wait_recv()
        call = jnp.concatenate([call_ref[r] for r in range(NDEV)], axis=0)
        part = _dot(call.astype(BF16), w_ref[...].astype(BF16), _NN)
        for r in range(NDEV):
            part_ref[r] = part[r:r + 1, :]
        modp_ref[me] = part_ref[me]
        for k in range(NDEV - 1):
            dev, pidx = _peer(k)
            cp = pltpu.make_async_remote_copy(part_ref.at[pidx], modp_ref.at[me], send_sems.at[NDEV - 1 + k],
                                              recv_sems.at[NDEV - 1 + k], device_id=dev, device_id_type=MESH)
            cp.start()
            sends.append(cp)
        for k in range(NDEV - 1):
            dev, pidx = _peer(k)
            pltpu.make_async_remote_copy(part_ref.at[pidx], modp_ref.at[pidx], send_sems.at[NDEV - 1 + k],
                                         recv_sems.at[NDEV - 1 + k], device_id=dev, device_id_type=MESH).wait_recv()
        for k in range(NDEV - 1):
            dev, pidx = _peer(k)
            pltpu.make_async_remote_copy(dwall_ref.at[pidx], dwall_ref.at[pidx], send_sems.at[2 * (NDEV - 1) + k],
                                         recv_sems.at[2 * (NDEV - 1) + k], device_id=dev,
                                         device_id_type=MESH).wait_recv()
        for cp in sends:
            cp.wait_send()
        mod_ref[...] = jnp.concatenate([modp_ref[r] for r in range(NDEV)], axis=1) + b_ref[...]

    mod, call, dw_all = pl.pallas_call(
        body, name="ada_forward",
        out_shape=(jax.ShapeDtypeStruct((1, NDEV * wcols), F32), jax.ShapeDtypeStruct((NDEV, 1, D), F32),
                   jax.ShapeDtypeStruct((NDEV,) + dw.shape, F32)),
        in_specs=[VSPEC] * 4, out_specs=(VSPEC,) * 3,
        scratch_shapes=[pltpu.VMEM((NDEV, 1, wcols), F32), pltpu.VMEM((NDEV, 1, wcols), F32),
                        pltpu.SemaphoreType.DMA((3 * (NDEV - 1),)), pltpu.SemaphoreType.DMA((3 * (NDEV - 1),))],
        compiler_params=_params(),
    )(c, w_ada, b_ada, dw)
    return mod, call.reshape(NDEV, D), dw_all


def _comm_sems(n, local):
    sems = [pltpu.SemaphoreType.DMA((7 * n,)), pltpu.SemaphoreType.DMA((7 * n,))]
    return sems + ([pltpu.SemaphoreType.DMA((n,))] if local else [])


def _gather2_ops(ins, outs, send_sems, recv_sems, local_sems):
    n = len(ins)
    x, y, c = _me()
    me, sibling = (x, y, c), (x, y, 1 - c)
    chips = [(1 - x, y), (x, 1 - y), (1 - x, 1 - y)]

    def slot(p):
        return 4 * p[0] + 2 * p[1] + p[2]

    def copy(a, k, block, to, src=None):
        return pltpu.make_async_remote_copy(
            src_ref=outs[a].at[slot(block)] if src is None else src, dst_ref=outs[a].at[slot(block)],
            send_sem=send_sems.at[a * 7 + k], recv_sem=recv_sems.at[a * 7 + k], device_id=to, device_id_type=MESH)

    def local(a):
        return pltpu.make_async_copy(ins[a], outs[a].at[slot(me)], local_sems.at[a])

    def first(a):
        return [copy(a, 0, me, sibling, src=ins[a])] + [copy(a, 1 + j, me, (*chip, c), src=ins[a])
                                                        for j, chip in enumerate(chips)]

    def passed(a):
        return [copy(a, 4 + j, (*chip, c), sibling) for j, chip in enumerate(chips)]

    def start():
        for a in range(n):
            local(a).start()
            for cp in first(a):
                cp.start()

    def forward():
        for j, chip in enumerate(chips):
            for a in range(n):
                copy(a, 1 + j, (*chip, c), me).wait_recv()
                passed(a)[j].start()

    def finish():
        for a in range(n):
            copy(a, 0, sibling, me).wait_recv()
            for j, chip in enumerate(chips):
                copy(a, 4 + j, (*chip, 1 - c), me).wait_recv()
        for a in range(n):
            for cp in first(a) + passed(a):
                cp.wait_send()
            local(a).wait()

    return start, forward, finish


def _exchange_ops(ins, outs, send_sems, recv_sems):
    n = len(ins)

    def copy(a, k):
        dev, pidx = _peer(k)
        return pltpu.make_async_remote_copy(ins[a].at[pidx], outs[a].at[k], send_sems.at[a * 7 + k],
                                            recv_sems.at[a * 7 + k], device_id=dev, device_id_type=MESH)

    def start():
        for k in range(NDEV - 1):
            for a in range(n):
                copy(a, k).start()

    def finish():
        for k in range(NDEV - 1):
            for a in range(n):
                copy(a, k).wait_recv()
        for k in range(NDEV - 1):
            for a in range(n):
                copy(a, k).wait_send()

    return start, finish


def _gathered(shards):
    return tuple(jax.ShapeDtypeStruct((NDEV,) + s.shape, s.dtype) for s in shards)


def _exchanged(blocks):
    return tuple(jax.ShapeDtypeStruct((NDEV - 1,) + b.shape[1:], b.dtype) for b in blocks)


def _owner_copy(src_ref, land_ref, send_sems, recv_sems, gi, n):
    x, y, c = _me()
    k = jnp.bitwise_xor(4 * x + 2 * y + c, n) - 1
    return pltpu.make_async_remote_copy(src_ref.at[gi], land_ref.at[k], send_sems.at[gi], recv_sems.at[k],
                                        device_id=(n >> 2, (n >> 1) & 1, n & 1), device_id_type=MESH)


def exchange_start(blocks, owners, name):
    landing = lax.empty((NDEV - 1,) + blocks.shape[1:], blocks.dtype)
    hbm = pl.BlockSpec(memory_space=pltpu.HBM)
    sem = pl.BlockSpec(memory_space=pltpu.SEMAPHORE)

    def body(src_ref, land_ref, send_sems, recv_sems, src_thru, land_thru, token):
        x, y, c = _me()
        for gi, n in enumerate(owners):
            @pl.when(4 * x + 2 * y + c != n)
            def _(gi=gi, n=n):
                _owner_copy(src_ref, land_ref, send_sems, recv_sems, gi, n).start()
        token[...] = jnp.zeros_like(token)

    return pl.pallas_call(
        body, name=name,
        out_shape=(pltpu.SemaphoreType.DMA((len(owners),)), pltpu.SemaphoreType.DMA((NDEV - 1,)),
                   pltpu.HBM(blocks.shape, blocks.dtype), pltpu.HBM(landing.shape, landing.dtype),
                   jax.ShapeDtypeStruct((8, 128), F32)),
        in_specs=(hbm, hbm), out_specs=(sem, sem, hbm, hbm, VSPEC), input_output_aliases={0: 2, 1: 3},
        compiler_params=pltpu.CompilerParams(has_side_effects=pltpu.SideEffectType.DATAFLOW_SIDE_EFFECTING),
    )(pltpu.with_memory_space_constraint(blocks, pltpu.HBM), pltpu.with_memory_space_constraint(landing, pltpu.HBM))


def exchange_wait(send_sems, recv_sems, src_thru, land_thru, after, owners, name):
    hbm = pl.BlockSpec(memory_space=pltpu.HBM)
    sem = pl.BlockSpec(memory_space=pltpu.SEMAPHORE)

    def body(src_ref, land_ref, send_sems, recv_sems, after_ref, src_dead, got_ref):
        x, y, c = _me()
        me = 4 * x + 2 * y + c
        for gi, n in enumerate(owners):
            @pl.when(me != n)
            def _(gi=gi, n=n):
                _owner_copy(src_ref, land_ref, send_sems, recv_sems, gi, n).wait_send()

            @pl.when(me == n)
            def _(gi=gi):
                for k in range(NDEV - 1):
                    pltpu.make_async_remote_copy(src_ref.at[gi], land_ref.at[k], send_sems.at[gi], recv_sems.at[k],
                                                 device_id=_peer(k)[0], device_id_type=MESH).wait_recv()

    return pl.pallas_call(
        body, name=name,
        out_shape=(pltpu.HBM(src_thru.shape, src_thru.dtype), pltpu.HBM(land_thru.shape, land_thru.dtype)),
        in_specs=(hbm, hbm, sem, sem, ANYSPEC), out_specs=(hbm, hbm), input_output_aliases={0: 0, 1: 1},
        compiler_params=pltpu.CompilerParams(has_side_effects=pltpu.SideEffectType.DATAFLOW_SIDE_EFFECTING),
    )(src_thru, land_thru, send_sems, recv_sems, after)[1]


def gather_and_sum_rows(parts, keep_rows):
    n = len(parts)
    offs = [sum(p.shape[0] for p in parts[:i]) for i in range(n)]
    rows = sum(p.shape[0] for p in parts)

    def body(*refs):
        sum_ref, keep_ref, buf_ref, send_sems, recv_sems = refs[n:]
        x, y, c = _me()
        me = 4 * x + 2 * y + c
        for i in range(n):
            buf_ref[me, offs[i]:offs[i] + parts[i].shape[0], :] = refs[i][...]

        def copy(k, block):
            return pltpu.make_async_remote_copy(buf_ref.at[block], buf_ref.at[block], send_sems.at[k],
                                                recv_sems.at[k], device_id=_peer(k)[0], device_id_type=MESH)

        for k in range(NDEV - 1):
            copy(k, me).start()
        for k in range(NDEV - 1):
            copy(k, _peer(k)[1]).wait_recv()
        for k in range(NDEV - 1):
            copy(k, me).wait_send()
        total = buf_ref[0]
        for r in range(1, NDEV):
            total = total + buf_ref[r]
        sum_ref[...] = total
        keep_ref[...] = jnp.zeros_like(keep_ref)
        for r in range(NDEV):
            for j, src in enumerate(keep_rows):
                keep_ref[r, j:j + 1, :] = buf_ref[r, src:src + 1, :]

    return pl.pallas_call(
        body, name="gather_and_sum_rows",
        out_shape=(jax.ShapeDtypeStruct((rows, D), F32), jax.ShapeDtypeStruct((NDEV, 8, D), F32)),
        in_specs=[VSPEC] * n, out_specs=(VSPEC, VSPEC),
        scratch_shapes=[pltpu.VMEM((NDEV, rows, D), F32), pltpu.SemaphoreType.DMA((NDEV - 1,)),
                        pltpu.SemaphoreType.DMA((NDEV - 1,))],
        compiler_params=_params(),
    )(*parts)


def _gather_chips():
    x, y, c = _me()
    north = c == 1
    first = (jnp.where(north, 1 - x, x), jnp.where(north, y, 1 - y))
    second = (jnp.where(north, x, 1 - x), jnp.where(north, 1 - y, y))
    return [first, second, (1 - x, 1 - y)]


def _arrival_order():
    x, y, c = _me()
    first, second, far = _gather_chips()
    devs = [(x, y, c), (x, y, 1 - c), (*first, c), (*second, 1 - c), (*second, c), (*first, 1 - c), (*far, c),
            (*far, 1 - c)]
    return jnp.stack([4 * d[0] + 2 * d[1] + d[2] for d in devs]).astype(jnp.int32)


def in_proj(x, gain, sc, sh, w_shard, b_in):
    t = x.shape[0]
    tm = min(1024, t // 2)
    nm = t // tm
    order = _arrival_order()

    def body(order_ref, x_ref, g_ref, sc_ref, sh_ref, b_ref, wsh_ref, p_ref, h_ref, wall_ref,
             wbuf, h_all, send_sems, recv_sems, local_sems):
        k, m = pl.program_id(0), pl.program_id(1)
        mx, my, mc = _me()
        me, sibling = (mx, my, mc), (mx, my, 1 - mc)
        chips = _gather_chips()

        def slot(d):
            return 4 * d[0] + 2 * d[1] + d[2]

        def copy(sem, block, to, src=None):
            return pltpu.make_async_remote_copy(
                src_ref=wall_ref.at[slot(block)] if src is None else src, dst_ref=wall_ref.at[slot(block)],
                send_sem=send_sems.at[sem], recv_sem=recv_sems.at[sem], device_id=to, device_id_type=MESH)

        own = pltpu.make_async_copy(wsh_ref, wall_ref.at[slot(me)], local_sems.at[0])
        first = [copy(0, me, sibling, src=wsh_ref)] + [copy(1 + j, me, (*q, mc), src=wsh_ref)
                                                       for j, q in enumerate(chips)]
        passed = [copy(4 + j, (*q, mc), sibling) for j, q in enumerate(chips)]

        def load(kk):
            src = wsh_ref if kk == 0 else wall_ref.at[order_ref[kk]]
            return pltpu.make_async_copy(src, wbuf.at[kk % 2], local_sems.at[1 + kk % 2])

        def arrived(kk):
            if kk == 1:
                copy(0, sibling, me).wait_recv()
            elif kk in (2, 4, 6):
                j = kk // 2 - 1
                copy(1 + j, (*chips[j], mc), me).wait_recv()
                passed[j].start()
                if j < 2:
                    first[2 + j].start()
            else:
                j = {3: 1, 5: 0, 7: 2}[kk]
                copy(4 + {1: 0, 0: 1, 2: 2}[j], (*chips[j], 1 - mc), me).wait_recv()

        @pl.when((k == 0) & (m == 0))
        def _():
            own.start()
            for cp in first[:2]:
                cp.start()
            load(0).start()
            load(0).wait()

        for kk in range(1, NDEV):
            @pl.when((k == kk - 1) & (m == nm // 2))
            def _(kk=kk):
                arrived(kk)
                load(kk).start()

            @pl.when((k == kk) & (m == 0))
            def _(kk=kk):
                load(kk).wait()

        rows = pl.ds(pl.multiple_of(m * tm, tm), tm)

        @pl.when(k == 0)
        def _():
            h_all[rows, :] = f_modulate(x_ref[...], g_ref[...], sc_ref[...], sh_ref[...]).astype(BF16)

        p_ref[...] = _dot(h_all[rows, :], wbuf[k % 2], _NN) + b_ref[...]

        @pl.when((k == NDEV - 1) & (m == nm - 1))
        def _():
            for cp in first + passed:
                cp.wait_send()
            own.wait()
            out = pltpu.make_async_copy(h_all, h_ref, local_sems.at[0])
            out.start()
            out.wait()

    vec = pl.BlockSpec((1, D), lambda k, m, o: (0, 0))
    return pl.pallas_call(
        body, name="in_proj",
        out_shape=(jax.ShapeDtypeStruct((t, NDEV * D), F32), jax.ShapeDtypeStruct((t, D), BF16),
                   jax.ShapeDtypeStruct((NDEV, D, D), BF16)),
        grid_spec=pltpu.PrefetchScalarGridSpec(
            num_scalar_prefetch=1, grid=(NDEV, nm),
            in_specs=[pl.BlockSpec((tm, D), lambda k, m, o: (jnp.where(k == 0, m, 0), 0)), vec, vec, vec,
                      pl.BlockSpec((1, D), lambda k, m, o: (0, o[k])), ANYSPEC],
            out_specs=(pl.BlockSpec((tm, D), lambda k, m, o: (m, o[k])), ANYSPEC, ANYSPEC),
            scratch_shapes=[pltpu.VMEM((2, D, D), BF16), pltpu.VMEM((t, D), BF16), pltpu.SemaphoreType.DMA((7,)),
                            pltpu.SemaphoreType.DMA((7,)), pltpu.SemaphoreType.DMA((3,))]),
        compiler_params=_params("arbitrary", "arbitrary"),
    )(order, x, gain, sc, sh, b_in, w_shard)


def hgrn_local(p, logits, shards):
    t = p.shape[0]
    nc = t // CH
    n = len(shards)

    def body(*refs):
        q_ref, f_ref, v_ref, l_ref = refs[:4]
        intra_ref, qe_ref, st_ref, dec_ref = refs[4 + n:8 + n]
        state = refs[8 + 2 * n]
        start, forward, finish = _gather2_ops(refs[4:4 + n], refs[8 + n:8 + 2 * n], *refs[9 + 2 * n:])
        step = pl.program_id(0)

        @pl.when(step == 0)
        def _():
            start()
            state[...] = jnp.zeros_like(state)

        intra, qe, ut, dec = f_hgrn_chunk(q_ref[...], f_ref[...], v_ref[...], l_ref[...])
        intra_ref[...] = intra
        qe_ref[...] = qe.astype(BF16)
        dec_ref[...] = dec
        s = state[...]
        st_ref[...] = s
        state[...] = dec * s + ut
        pl.when(step == (3 * nc) // 4)(forward)
        pl.when(step == nc - 1)(finish)

    col = lambda j: pl.BlockSpec((CH, D), lambda c: (c, j))
    res = pl.pallas_call(
        body, name="hgrn_local", grid=(nc,),
        out_shape=(jax.ShapeDtypeStruct((t, D), F32), jax.ShapeDtypeStruct((t, D), BF16),
                   jax.ShapeDtypeStruct((nc, DK, D), F32), jax.ShapeDtypeStruct((nc, 1, D), F32)) + _gathered(shards),
        in_specs=[col(0), col(1), col(2), pl.BlockSpec((2, D), lambda c: (0, 0))] + [ANYSPEC] * n,
        out_specs=(pl.BlockSpec((CH, D), lambda c: (c, 0)), pl.BlockSpec((CH, D), lambda c: (c, 0)),
                   pl.BlockSpec((None, DK, D), lambda c: (c, 0, 0)), pl.BlockSpec((None, 1, D), lambda c: (c, 0, 0)))
        + (ANYSPEC,) * n,
        scratch_shapes=[pltpu.VMEM((DK, D), F32)] + _comm_sems(n, True),
        compiler_params=_params("arbitrary"),
    )(p, p, p, logits, *shards)
    return res[:4] + (res[4:],)


def _conv_tile(t):
    return min(256, t)


def conv_forward(p, dw, db, ln_g, ln_b, shards):
    t = p.shape[0]
    tm = _conv_tile(t)
    per = tm // HALO
    n = len(shards)
    nt = t // tm

    def body(*refs):
        cv_ref, cg_ref, cvp_ref, cgp_ref, dw_ref, db_ref, g_ref, b_ref = refs[:8]
        yc_ref, act_ref = refs[8 + n:10 + n]
        uext, ush = refs[10 + 2 * n:12 + 2 * n]
        start, forward, finish = _gather2_ops(refs[8:8 + n], refs[10 + n:10 + 2 * n], *refs[12 + 2 * n:])
        step = pl.program_id(0)
        pl.when(step == 0)(start)
        pl.when(step == nt - 1)(forward)
        uext[0:HALO, :] = jnp.where(step == 0, 0.0, f_glu(cvp_ref[...], cgp_ref[...]))
        uext[HALO:HALO + tm, :] = f_glu(cv_ref[...], cg_ref[...])
        _shift_stack(uext, ush, tm + HALO)
        acc = jnp.zeros((tm, D), F32) + db_ref[...]
        for w in range(KW):
            acc = acc + dw_ref[w:w + 1, :] * _shifted(ush, HALO - KW + 1 + w, tm)
        yc_ref[...] = acc
        act_ref[...] = f_conv_act(acc, g_ref[...], b_ref[...]).astype(BF16)
        pl.when(step == nt - 1)(finish)

    vec = pl.BlockSpec((1, D), lambda m: (0, 0))
    prev = lambda j: pl.BlockSpec((HALO, D), lambda m: (jnp.maximum(m * per - 1, 0), j))
    res = pl.pallas_call(
        body, name="conv_forward", grid=(nt,),
        out_shape=(jax.ShapeDtypeStruct((t, D), F32), jax.ShapeDtypeStruct((t, D), BF16)) + _gathered(shards),
        in_specs=[pl.BlockSpec((tm, D), lambda m: (m, 4)), pl.BlockSpec((tm, D), lambda m: (m, 5)), prev(4), prev(5),
                  pl.BlockSpec((HALO, D), lambda m: (0, 0)), vec, vec, vec] + [ANYSPEC] * n,
        out_specs=(pl.BlockSpec((tm, D), lambda m: (m, 0)), pl.BlockSpec((tm, D), lambda m: (m, 0))) + (ANYSPEC,) * n,
        scratch_shapes=[pltpu.VMEM((HALO + tm, D), F32), pltpu.VMEM((8, tm + HALO, D), F32)] + _comm_sems(n, True),
        compiler_params=_params("arbitrary"),
    )(p, p, p, p, dw, db, ln_g, ln_b, *shards)
    return res[0], res[1], res[2:]


def _mix_tile(t):
    return min(256, t)


def _mix_forward_tile(x_ref, og_ref, ga_ref, gb_ref, qe_ref, intra_ref, st_ref, act_ref, wa_ref, wb_ref, wo_ref,
                      hg_ref, rows):
    o = _inter(qe_ref[...], st_ref, rows) + intra_ref[...]
    a = f_head_out(o, og_ref[...], hg_ref[...])
    ya = _dot(a.astype(BF16), wa_ref[...], _NN)
    yb = _dot(act_ref[...], wb_ref[...], _NN)
    merged = f_merge(ga_ref[...], gb_ref[...], ya, yb)
    y = _dot(merged.astype(BF16), wo_ref[...], _NN)
    return o, a, ya, yb, merged, y


def _mix_specs(tm, tiles=None):
    at = (lambda m: m) if tiles is None else (lambda m: tiles - 1 - m)
    col = lambda j: pl.BlockSpec((tm, D), lambda m: (at(m), j))
    row = pl.BlockSpec((tm, D), lambda m: (at(m), 0))
    per_chunk = lambda rows: pl.BlockSpec((tm // CH, rows, D), lambda m: (at(m), 0, 0))
    return col, row, pl.BlockSpec((1, D), lambda m: (0, 0)), per_chunk


def mix_forward(x, p, qe, intra, st_prev, act, wa, wb, wo, hg, g1, post):
    t = x.shape[0]
    tm = min(512, t)

    def body(x_ref, og_ref, ga_ref, gb_ref, qe_ref, intra_ref, st_ref, act_ref, wa_ref, wb_ref, wo_ref, hg_ref,
             g1_ref, post_ref, x1_ref):
        y = _mix_forward_tile(x_ref, og_ref, ga_ref, gb_ref, qe_ref, intra_ref, st_ref, act_ref, wa_ref, wb_ref,
                              wo_ref, hg_ref, tm)[-1]
        x1_ref[...] = f_residual(x_ref[...], y, g1_ref[...], post_ref[...])

    col, row, vec, per_chunk = _mix_specs(tm)
    return pl.pallas_call(
        body, name="mix_forward", grid=(t // tm,),
        out_shape=jax.ShapeDtypeStruct((t, D), F32),
        in_specs=[row, col(3), col(6), col(7), row, row, per_chunk(DK), row, VSPEC, VSPEC, VSPEC, vec, vec, vec],
        out_specs=row,
        compiler_params=_params("parallel"),
    )(x, p, p, p, qe, intra, st_prev, act, wa, wb, wo, hg, g1, post)


def ffn_forward_backward(x1, target, w1, w2, pre, sc, sh, g2, post):
    t = x1.shape[0]
    tm = min(512, t)
    nb = w1.shape[0]
    fb = w1.shape[2]

    def body(x_ref, tg_ref, w1_ref, w2_ref, pre_ref, sc_ref, sh_ref, g2_ref, post_ref,
             dx_ref, h2_ref, r_ref, dz_ref, dy2_ref, acc_ref, z_sc):
        @pl.when(pl.program_id(0) == 0)
        def _():
            acc_ref[...] = jnp.zeros_like(acc_ref)

        x1v = x_ref[...]
        h2, vjp_h = jax.vjp(f_modulate, x1v, pre_ref[...], sc_ref[...], sh_ref[...])
        h2b = h2.astype(BF16)
        h2_ref[...] = h2b
        y2 = jnp.zeros((tm, D), F32)
        for n in range(nb):
            z = _dot(h2b, w1_ref[n], _NN)
            z_sc[:, n * fb:(n + 1) * fb] = z
            r = jnp.square(jnp.maximum(z, 0.0)).astype(BF16)
            r_ref[:, n * fb:(n + 1) * fb] = r
            y2 = y2 + _dot(r, w2_ref[n], _NN)
        out, vjp_r = jax.vjp(f_residual, x1v, y2, g2_ref[...], post_ref[...])
        err = out - tg_ref[...]
        tok = jnp.mean(jnp.square(err), axis=-1, keepdims=True)
        acc_ref[5:6, :] += 0.5 * jnp.sum(tok, axis=0, keepdims=True)
        dx_a, dy2, dg2, dpost = vjp_r(err * (1.0 / D))
        dy2b = dy2.astype(BF16)
        dy2_ref[...] = dy2b
        dh2 = jnp.zeros((tm, D), F32)
        for n in range(nb):
            dr = _dot(dy2b, w2_ref[n], _NT)
            dz = (dr * (2.0 * jnp.maximum(z_sc[:, n * fb:(n + 1) * fb], 0.0))).astype(BF16)
            dz_ref[:, n * fb:(n + 1) * fb] = dz
            dh2 = dh2 + _dot(dz, w1_ref[n], _NT)
        dx_b, dpre, dsc, dsh = vjp_h(dh2)
        dx_ref[...] = dx_a + dx_b
        acc_ref[0:1, :] += dpre
        acc_ref[1:2, :] += dpost
        acc_ref[2:3, :] += dsc
        acc_ref[3:4, :] += dsh
        acc_ref[4:5, :] += dg2

    row = pl.BlockSpec((tm, D), lambda m: (m, 0))
    wide = pl.BlockSpec((tm, DFF), lambda m: (m, 0), pipeline_mode=pl.Buffered(1))
    vec = pl.BlockSpec((1, D), lambda m: (0, 0))
    return pl.pallas_call(
        body, name="ffn_forward_backward", grid=(t // tm,),
        out_shape=(jax.ShapeDtypeStruct((t, D), F32), jax.ShapeDtypeStruct((t, D), BF16),
                   jax.ShapeDtypeStruct((t, DFF), BF16), jax.ShapeDtypeStruct((t, DFF), BF16),
                   jax.ShapeDtypeStruct((t, D), BF16), jax.ShapeDtypeStruct((8, D), F32)),
        in_specs=[row, row, VSPEC, VSPEC, vec, vec, vec, vec, vec],
        out_specs=(row, row, wide, wide, row, pl.BlockSpec((8, D), lambda m: (0, 0))),
        scratch_shapes=[pltpu.VMEM((tm, DFF), F32)],
        compiler_params=_params("arbitrary"),
    )(x1, target, w1, w2, pre, sc, sh, g2, post)


def mix_backward(x, p, qe, intra, st_prev, act, wa, wb, wo, hg, g1, post, dx1, dec):
    t = x.shape[0]
    tm = _mix_tile(t)
    nc = t // CH

    def body(x_ref, og_ref, ga_ref, gb_ref, qe_ref, intra_ref, st_ref, act_ref, wa_ref, wb_ref, wo_ref, hg_ref,
             g1_ref, post_ref, dx1_ref, dec_ref,
             dp_ref, do_ref, dqe_ref, gt_ref, dact_ref, a_ref, mg_ref, dya_ref, dyb_ref, dy_ref, acc_ref, gstate):
        @pl.when(pl.program_id(0) == 0)
        def _():
            acc_ref[...] = jnp.zeros_like(acc_ref)
            gstate[...] = jnp.zeros_like(gstate)

        o, a, ya, yb, merged, y = _mix_forward_tile(x_ref, og_ref, ga_ref, gb_ref, qe_ref, intra_ref, st_ref, act_ref,
                                                    wa_ref, wb_ref, wo_ref, hg_ref, tm)
        a_ref[...] = a.astype(BF16)
        mg_ref[...] = merged.astype(BF16)
        _, vjp_r = jax.vjp(f_residual, x_ref[...], y, g1_ref[...], post_ref[...])
        _, dy, dg1, dpost = vjp_r(dx1_ref[...])
        dyb16 = dy.astype(BF16)
        dy_ref[...] = dyb16
        dmerged = _dot(dyb16, wo_ref[...], _NT)
        _, vjp_m = jax.vjp(f_merge, ga_ref[...], gb_ref[...], ya, yb)
        dga, dgb, dya, dyb = vjp_m(dmerged)
        dp_ref[:, D:2 * D] = dga.astype(BF16)
        dp_ref[:, 2 * D:3 * D] = dgb.astype(BF16)
        dya16, dyb16b = dya.astype(BF16), dyb.astype(BF16)
        dya_ref[...] = dya16
        dyb_ref[...] = dyb16b
        da = _dot(dya16, wa_ref[...], _NT)
        dact_ref[...] = _dot(dyb16b, wb_ref[...], _NT)
        _, vjp_a = jax.vjp(f_head_out, o, og_ref[...], hg_ref[...])
        do, dog, dhg = vjp_a(da)
        dp_ref[:, 0:D] = dog.astype(BF16)
        do_ref[...] = do
        do16 = do.astype(BF16)
        qe16 = qe_ref[...]
        for ci in reversed(range(tm // CH)):
            st = st_ref[ci].astype(BF16)
            rows = slice(ci * CH, (ci + 1) * CH)
            dqe, vt = [], []
            for h in range(HEADS):
                sl = slice(h * DK, (h + 1) * DK)
                dqe.append(_dot(do16[rows, sl], st[:, sl], _NN))
                vt.append(_dot(do16[rows, sl], qe16[rows, sl], _TN))
            dqe_ref[rows, :] = jnp.concatenate(dqe, axis=1)
            g = gstate[...]
            gt_ref[ci] = g
            gstate[...] = dec_ref[ci] * g + jnp.concatenate(vt, axis=1)
        acc_ref[0:1, :] += dg1
        acc_ref[1:2, :] += dpost
        acc_ref[2:3, :] += dhg
        acc_ref[3:4, :] += jnp.sum(dog, axis=0, keepdims=True)
        acc_ref[4:5, :] += jnp.sum(dga, axis=0, keepdims=True)
        acc_ref[5:6, :] += jnp.sum(dgb, axis=0, keepdims=True)

    nt = t // tm
    col, row, vec, per_chunk = _mix_specs(tm, nt)
    b16 = jax.ShapeDtypeStruct((t, D), BF16)
    f32 = jax.ShapeDtypeStruct((t, D), F32)
    return pl.pallas_call(
        body, name="mix_backward", grid=(nt,),
        out_shape=(jax.ShapeDtypeStruct((t, NDEV * D), BF16), f32, f32, jax.ShapeDtypeStruct((nc, DK, D), F32), f32,
                   b16, b16, b16, b16, b16, jax.ShapeDtypeStruct((8, D), F32)),
        in_specs=[row, col(3), col(6), col(7), row, row, per_chunk(DK), row, VSPEC, VSPEC, VSPEC, vec, vec, vec, row,
                  per_chunk(1)],
        out_specs=(pl.BlockSpec((tm, 3 * D), lambda m: (nt - 1 - m, 0)), row, row, per_chunk(DK), row, row, row, row,
                   row, row, pl.BlockSpec((8, D), lambda m: (0, 0))),
        scratch_shapes=[pltpu.VMEM((DK, D), F32)],
        compiler_params=_params("arbitrary"),
    )(x, p, p, p, qe, intra, st_prev, act, wa, wb, wo, hg, g1, post, dx1, dec)


def conv_backward(p, yc, dact, dw, ln_g, ln_b, blocks, dp):
    t = p.shape[0]
    tm = _conv_tile(t)
    per = tm // HALO
    nt = t // tm
    n = len(blocks)

    def body(*refs):
        cv_ref, cg_ref, cvp_ref, cgp_ref, yc_ref, ycn_ref, da_ref, dan_ref, dw_ref, g_ref, b_ref = refs[:11]
        dp_ref, acc_ref, ddw_ref = refs[12 + n:15 + n]
        uext, dyext, ush, dysh, ddw8, du_sc = refs[15 + 2 * n:21 + 2 * n]
        start, finish = _exchange_ops(refs[11:11 + n], refs[15 + n:15 + 2 * n], *refs[21 + 2 * n:])
        m = pl.program_id(0)

        @pl.when(m == 0)
        def _():
            start()
            acc_ref[...] = jnp.zeros_like(acc_ref)
            ddw8[...] = jnp.zeros_like(ddw8)

        cv, cg = cv_ref[...], cg_ref[...]
        u, vjp_u = jax.vjp(f_glu, cv, cg)
        uext[0:HALO, :] = jnp.where(m == 0, 0.0, f_glu(cvp_ref[...], cgp_ref[...]))
        uext[HALO:HALO + tm, :] = u
        _shift_stack(uext, ush, tm + HALO)
        _, vjp_c = jax.vjp(f_conv_act, yc_ref[...], g_ref[...], b_ref[...])
        dyc, dg, db = vjp_c(da_ref[...])
        _, vjp_n = jax.vjp(f_conv_act, ycn_ref[...], g_ref[...], b_ref[...])
        dyn = vjp_n(dan_ref[...])[0]
        dyext[0:tm, :] = dyc
        dyext[tm:tm + HALO, :] = jnp.where(m == nt - 1, 0.0, dyn)
        _shift_stack(dyext, dysh, tm + HALO)
        rb = min(128, tm)
        for lt in range(D // LANE):
            ls = slice(lt * LANE, (lt + 1) * LANE)
            for r0 in range(0, tm, rb):
                du_l = jnp.zeros((rb, LANE), F32)
                for w in range(KW):
                    du_l = du_l + dw_ref[w:w + 1, ls] * _shifted(dysh, KW - 1 - w + r0, rb, ls)
                du_sc[r0:r0 + rb, ls] = du_l
                dyc_l = dyext[r0:r0 + rb, ls]
                for w in range(KW):
                    prod = dyc_l * _shifted(ush, HALO - KW + 1 + w + r0, rb, ls)
                    part = jnp.sum(prod.reshape(4, rb // 32, 8, LANE), axis=1)
                    ddw8[w, :, ls] += (part[0] + part[1]) + (part[2] + part[3])
        du = du_sc[...]

        @pl.when(m == nt - 1)
        def _():
            ddw_ref[...] = jnp.sum(ddw8[...], axis=1)

        dcv, dcg = vjp_u(du)
        dp_ref[:, 0:D] = dcv.astype(BF16)
        dp_ref[:, D:2 * D] = dcg.astype(BF16)
        acc_ref[0:1, :] += jnp.sum(dyc, axis=0, keepdims=True)
        acc_ref[1:2, :] += dg
        acc_ref[2:3, :] += db
        acc_ref[3:4, :] += jnp.sum(dcv, axis=0, keepdims=True)
        acc_ref[4:5, :] += jnp.sum(dcg, axis=0, keepdims=True)
        pl.when(m == nt - 1)(finish)

    vec = pl.BlockSpec((1, D), lambda m: (0, 0))
    row = pl.BlockSpec((tm, D), lambda m: (m, 0))
    prev = lambda j: pl.BlockSpec((HALO, D), lambda m: (jnp.maximum(m * per - 1, 0), j))
    nxt = pl.BlockSpec((HALO, D), lambda m: (jnp.minimum((m + 1) * per, t // HALO - 1), 0))
    res = pl.pallas_call(
        body, name="conv_backward", grid=(nt,),
        out_shape=(jax.ShapeDtypeStruct(dp.shape, dp.dtype), jax.ShapeDtypeStruct((8, D), F32),
                   jax.ShapeDtypeStruct((HALO, D), F32)) + _exchanged(blocks),
        in_specs=[pl.BlockSpec((tm, D), lambda m: (m, 4)), pl.BlockSpec((tm, D), lambda m: (m, 5)), prev(4), prev(5),
                  row, nxt, row, nxt, pl.BlockSpec((HALO, D), lambda m: (0, 0)), vec, vec] + [ANYSPEC] * (n + 1),
        out_specs=(pl.BlockSpec((tm, 2 * D), lambda m: (m, 3)), pl.BlockSpec((8, D), lambda m: (0, 0)),
                   pl.BlockSpec((HALO, D), lambda m: (0, 0))) + (ANYSPEC,) * n,
        scratch_shapes=[pltpu.VMEM((HALO + tm, D), F32), pltpu.VMEM((tm + HALO, D), F32),
                        pltpu.VMEM((8, tm + HALO, D), F32), pltpu.VMEM((8, tm + HALO, D), F32),
                        pltpu.VMEM((HALO, 8, D), F32), pltpu.VMEM((tm, D), F32)] + _comm_sems(n, False),
        input_output_aliases={11 + n: 0},
        compiler_params=_params("arbitrary"),
    )(p, p, p, p, yc, yc, dact, dact, dw, ln_g, ln_b, *blocks, dp)
    return res[:3] + (res[3:],)


def hgrn_backward(p, logits, do, dqe, gt, st_prev, blocks, dp):
    t = p.shape[0]
    nc = t // CH
    n = len(blocks)

    def body(*refs):
        q_ref, f_ref, v_ref, l_ref, do_ref, dqe_ref, gt_ref, st_ref = refs[:8]
        dp_ref, acc_ref = refs[9 + n:11 + n]
        start, finish = _exchange_ops(refs[8:8 + n], refs[11 + n:11 + 2 * n], *refs[11 + 2 * n:])

        @pl.when(pl.program_id(0) == 0)
        def _():
            start()
            acc_ref[...] = jnp.zeros_like(acc_ref)

        gt_v = gt_ref[...]
        ddec = jnp.sum(gt_v * st_ref[...], axis=0, keepdims=True)
        _, vjp = jax.vjp(f_hgrn_chunk, q_ref[...], f_ref[...], v_ref[...], l_ref[...])
        dq, df, dv, dl = vjp((do_ref[...], dqe_ref[...], gt_v, ddec))
        dp_ref[:, 0:D] = dq.astype(BF16)
        dp_ref[:, D:2 * D] = df.astype(BF16)
        dp_ref[:, 2 * D:3 * D] = dv.astype(BF16)
        acc_ref[0:2, :] += dl
        acc_ref[2:3, :] += jnp.sum(dq, axis=0, keepdims=True)
        acc_ref[3:4, :] += jnp.sum(df, axis=0, keepdims=True)
        acc_ref[4:5, :] += jnp.sum(dv, axis=0, keepdims=True)
        pl.when(pl.program_id(0) == nc - 1)(finish)

    col = lambda j: pl.BlockSpec((CH, D), lambda c: (c, j))
    row = pl.BlockSpec((CH, D), lambda c: (c, 0))
    stspec = pl.BlockSpec((None, DK, D), lambda c: (c, 0, 0))
    res = pl.pallas_call(
        body, name="hgrn_backward", grid=(nc,),
        out_shape=(jax.ShapeDtypeStruct(dp.shape, dp.dtype), jax.ShapeDtypeStruct((8, D), F32)) + _exchanged(blocks),
        in_specs=[col(0), col(1), col(2), pl.BlockSpec((2, D), lambda c: (0, 0)), row, row, stspec, stspec]
        + [ANYSPEC] * (n + 1),
        out_specs=(pl.BlockSpec((CH, 3 * D), lambda c: (c, 1)), pl.BlockSpec((8, D), lambda c: (0, 0)))
        + (ANYSPEC,) * n,
        scratch_shapes=_comm_sems(n, False),
        input_output_aliases={8 + n: 0},
        compiler_params=_params("arbitrary"),
    )(p, p, p, logits, do, dqe, gt, st_prev, *blocks, dp)
    return res[:2] + (res[2:],)


def in_proj_backward(dp, w_all, x, dx1, gain, sc, sh, blocks):
    t = x.shape[0]
    tm = min(512, t)
    nt = t // tm
    n = len(blocks)

    def body(*refs):
        dp_ref, w_ref, x_ref, dx1_ref, g_ref, sc_ref, sh_ref = refs[:7]
        gx_ref, acc_ref = refs[7 + n:9 + n]
        start = finish = lambda: None
        if n:
            start, finish = _exchange_ops(refs[7:7 + n], refs[9 + n:9 + 2 * n], *refs[9 + 2 * n:])

        @pl.when(pl.program_id(0) == 0)
        def _():
            start()
            acc_ref[...] = jnp.zeros_like(acc_ref)

        dh = jnp.zeros((tm, D), F32)
        for j in range(NDEV):
            dh = dh + _dot(dp_ref[:, j * D:(j + 1) * D], w_ref[DP_SPLIT[j]], _NT)
        _, vjp_h = jax.vjp(f_modulate, x_ref[...], g_ref[...], sc_ref[...], sh_ref[...])
        dx, dg, dsc, dsh = vjp_h(dh)
        gx_ref[...] = dx1_ref[...] + dx
        acc_ref[0:1, :] += dg
        acc_ref[1:2, :] += dsc
        acc_ref[2:3, :] += dsh
        pl.when(pl.program_id(0) == nt - 1)(finish)

    row = pl.BlockSpec((tm, D), lambda m: (m, 0))
    vec = pl.BlockSpec((1, D), lambda m: (0, 0))
    res = pl.pallas_call(
        body, name="in_proj_backward", grid=(nt,),
        out_shape=(jax.ShapeDtypeStruct((t, D), F32), jax.ShapeDtypeStruct((8, D), F32)) + _exchanged(blocks),
        in_specs=[pl.BlockSpec((tm, NDEV * D), lambda m: (m, 0)), VSPEC, row, row, vec, vec, vec] + [ANYSPEC] * n,
        out_specs=(row, pl.BlockSpec((8, D), lambda m: (0, 0))) + (ANYSPEC,) * n,
        scratch_shapes=_comm_sems(n, False) if n else [],
        compiler_params=_params("arbitrary"),
    )(dp, w_all, x, dx1, gain, sc, sh, *blocks)
    return res[0], res[1], res[2:]


def weight_grad(a, b, nblk, ka, bn, a_blocked, name, b_col=lambda n: n, after=None):
    t = a.shape[0]
    tk = min(2048, t)
    nk = t // tk
    extra = [] if after is None else [after]

    def body(*refs):
        a_ref, b_ref = refs[:2]
        f_ref, h_ref, acc = refs[2 + len(extra):]
        k = pl.program_id(1)

        @pl.when(k == 0)
        def _():
            acc[...] = jnp.zeros_like(acc)

        acc[...] += _dot(a_ref[...], b_ref[...], _TN)

        @pl.when(k == nk - 1)
        def _():
            f_ref[...] = acc[...]
            h_ref[...] = acc[...].astype(BF16)

    a_idx = (lambda n, k: (k, n)) if a_blocked else (lambda n, k: (k, 0))
    b_idx = (lambda n, k: (k, 0)) if a_blocked else (lambda n, k: (k, b_col(n)))
    out = pl.BlockSpec((None, ka, bn), lambda n, k: (n, 0, 0))
    return pl.pallas_call(
        body, name=name, grid=(nblk, nk),
        out_shape=(jax.ShapeDtypeStruct((nblk, ka, bn), F32), jax.ShapeDtypeStruct((nblk, ka, bn), BF16)),
        in_specs=[pl.BlockSpec((tk, ka), a_idx), pl.BlockSpec((tk, bn), b_idx)] + [ANYSPEC] * len(extra),
        out_specs=(out, out),
        scratch_shapes=[pltpu.VMEM((ka, bn), F32)],
        compiler_params=_params("parallel", "arbitrary"),
    )(a, b, *extra)


def ada_backward(call_t, dmod_cols, w, m, v):
    def body(c_ref, d_ref, w_ref, m_ref, v_ref, g_ref, dl_ref, nm_ref, nv_ref):
        ct, dm = c_ref[...], d_ref[...]
        g = ct[:, 0:1] * dm[0:1, :]
        for r in range(1, NDEV):
            g = g + ct[:, r:r + 1] * dm[r:r + 1, :]
        g_ref[...] = g
        dl_ref[...], nm_ref[...], nv_ref[...] = _adamw(w_ref[...], g, m_ref[...], v_ref[...])

    s = jax.ShapeDtypeStruct(w.shape, F32)
    return pl.pallas_call(
        body, name="ada_backward", out_shape=(s, s, s, s),
        in_specs=[VSPEC] * 5, out_specs=(VSPEC,) * 4, compiler_params=_params(),
    )(call_t, dmod_cols, w, m, v)


def adamw_small(total, ddw_mine, recipes, ws, ms, vs):
    n = len(ws)

    def body(*refs):
        tot, ddw = refs[0], refs[1]
        w_refs, m_refs, v_refs = refs[2:2 + n], refs[2 + n:2 + 2 * n], refs[2 + 2 * n:2 + 3 * n]
        outs = refs[2 + 3 * n:2 + 7 * n]
        loss_ref = refs[2 + 7 * n]
        for i, rec in enumerate(recipes):
            if rec == "dw":
                g = ddw[...]
            elif isinstance(rec, tuple):
                g = tot[rec[0]:rec[1], :]
            else:
                g = jnp.concatenate([tot[r:r + 1, :] for r in rec], axis=1) if len(rec) > 1 else tot[rec[0]:rec[0] + 1, :]
            dl, nm, nv = _adamw(w_refs[i][...], g, m_refs[i][...], v_refs[i][...])
            outs[4 * i][...] = g
            outs[4 * i + 1][...] = dl
            outs[4 * i + 2][...] = nm
            outs[4 * i + 3][...] = nv
        loss_ref[...] = tot[LOSS_ROW:LOSS_ROW + 1, 0:128]

    shapes = []
    for w in ws:
        shapes += [jax.ShapeDtypeStruct(w.shape, F32)] * 4
    res = pl.pallas_call(
        body, name="adamw_small", out_shape=tuple(shapes) + (jax.ShapeDtypeStruct((1, 128), F32),),
        in_specs=[VSPEC] * (2 + 3 * n), out_specs=(VSPEC,) * (4 * n + 1), compiler_params=_params(),
    )(total, ddw_mine, *ws, *ms, *vs)
    return [res[4 * i:4 * i + 4] for i in range(n)], res[4 * n]


def reduce_and_adamw(sel, w, g_alls, g_recvs, m, v, name):
    r, c = w.shape
    br = min(256, r)
    npair = len(g_alls)

    def body(sel_ref, w_ref, *refs):
        go_refs, gr_refs = refs[:npair], refs[npair:2 * npair]
        m_ref, v_ref, g_ref, dl_ref, nm_ref, nv_ref = refs[2 * npair:]
        g = None
        for i in range(npair):
            gi = go_refs[i][...]
            for k in range(NDEV - 1):
                gi = gi + gr_refs[i][k].astype(F32)
            g = gi if g is None else jnp.where(sel_ref[1] == i, gi, g)
        g_ref[...] = g
        dl_ref[...], nm_ref[...], nv_ref[...] = _adamw(w_ref[...], g, m_ref[...], v_ref[...])

    blk = pl.BlockSpec((br, c), lambda i, sel_ref: (i, 0))
    own = pl.BlockSpec((None, br, c), lambda i, sel_ref: (sel_ref[0], i, 0))
    recv = pl.BlockSpec((NDEV - 1, br, c), lambda i, sel_ref: (0, i, 0))
    s = jax.ShapeDtypeStruct(w.shape, F32)
    return pl.pallas_call(
        body, name=name, out_shape=(s, s, s, s),
        grid_spec=pltpu.PrefetchScalarGridSpec(
            num_scalar_prefetch=1, grid=(r // br,),
            in_specs=[blk] + [own] * npair + [recv] * npair + [blk, blk],
            out_specs=(blk, blk, blk, blk)),
        compiler_params=_params("parallel"),
    )(sel, w, *g_alls, *g_recvs, m, v)


def kernel(x, c, w_ada, b_ada, pre_norm_tm, post_norm_tm, pre_norm_cm, post_norm_cm, w_in, b_in, hg_lb_logits, hg_norm, conv_dw, conv_db, conv_ln_g, conv_ln_b, w_br_a, w_br_b, w_out, w_ff1, w_ff2, loss_target, m_w_ada, m_b_ada, m_pre_norm_tm, m_post_norm_tm, m_pre_norm_cm, m_post_norm_cm, m_w_in, m_b_in, m_hg_lb_logits, m_hg_norm, m_conv_dw, m_conv_db, m_conv_ln_g, m_conv_ln_b, m_w_br_a, m_w_br_b, m_w_out, m_w_ff1, m_w_ff2, v_w_ada, v_b_ada, v_pre_norm_tm, v_post_norm_tm, v_pre_norm_cm, v_post_norm_cm, v_w_in, v_b_in, v_hg_lb_logits, v_hg_norm, v_conv_dw, v_conv_db, v_conv_ln_g, v_conv_ln_b, v_w_br_a, v_w_br_b, v_w_out, v_w_ff1, v_w_ff2):
    t = x.shape[1]
    me = 4 * lax.axis_index("x") + 2 * lax.axis_index("y") + lax.axis_index("c")
    xs = x[0]
    tgt = loss_target[0]

    mod, call, dw_all = ada_forward(c, w_ada[0], b_ada, conv_dw[0])
    sh1, sc1, g1, sh2, sc2, g2 = [mod[:, i * D:(i + 1) * D] for i in range(6)]
    dw = jnp.pad(dw_all.transpose(1, 0, 2).reshape(KW, D), ((0, HALO - KW), (0, 0)))

    p, h, win_all = in_proj(xs, pre_norm_tm, sc1, sh1, w_in[0].astype(BF16), b_in)
    yc, act, (w1_all,) = conv_forward(p, dw, conv_db, conv_ln_g, conv_ln_b, [w_ff1[0].astype(BF16)])
    intra, qe, st_prev, dec, (w2_all, wa_all, wb_all, wo_all) = hgrn_local(
        p, hg_lb_logits, [w_ff2[0].astype(BF16), w_br_a[0].astype(BF16), w_br_b[0].astype(BF16),
                          w_out[0].astype(BF16)])
    wa, wb, wo = wa_all.reshape(D, D), wb_all.reshape(D, D), wo_all.reshape(D, D)
    x1 = mix_forward(xs, p, qe, intra, st_prev, act, wa, wb, wo, hg_norm, g1, post_norm_tm)

    dx1, h2, r, dz, dy2, acc_ffn = ffn_forward_backward(x1, tgt, w1_all, w2_all, pre_norm_cm, sc2, sh2, g2,
                                                        post_norm_cm)
    rows = D // NDEV
    g1_f, g1_h = weight_grad(h2, dz, NDEV, D, DFF // NDEV, False, "grad_w_ff1")
    g2_f, g2_h = weight_grad(r, dy2, 4, D, D, True, "grad_w_ff2")
    g2_f, g2_h = g2_f.reshape(NDEV, DFF // NDEV, D), g2_h.reshape(NDEV, DFF // NDEV, D)

    (dp, do, dqe, gt, dact, a16, mg16, dya, dyb, dy, acc_mix) = mix_backward(
        xs, p, qe, intra, st_prev, act, wa, wb, wo, hg_norm, g1, post_norm_tm, dx1, dec)
    ga_f, ga_h = weight_grad(a16, dya, 1, D, D, False, "grad_w_br_a")
    gb_f, gb_h = weight_grad(act, dyb, 1, D, D, False, "grad_w_br_b")
    go_f, go_h = weight_grad(mg16, dy, 1, D, D, False, "grad_w_out")
    dp, acc_conv, ddw, (r_ff1, r_ff2) = conv_backward(p, yc, dact, dw, conv_ln_g, conv_ln_b, [g1_h, g2_h], dp)
    dp, acc_hg, (r_a, r_b, r_o) = hgrn_backward(
        p, hg_lb_logits, do, dqe, gt, st_prev,
        [ga_h.reshape(NDEV, rows, D), gb_h.reshape(NDEV, rows, D), go_h.reshape(NDEV, rows, D)], dp)
    dp_col = lambda n: jnp.where(n < 3, n + 3, jnp.where(n == 3, 0, jnp.where(n < 6, n + 2, n - 5)))
    even, odd = [0, 2, 4, 6], [1, 3, 5, 7]
    gin_f0, gin_h0 = weight_grad(h, dp, 4, D, D, False, "grad_w_in_even", b_col=lambda j: dp_col(2 * j))
    send0, recv0, thru0, landing0, token0 = exchange_start(gin_h0, even, "exchange_start_even")
    gin_f1, gin_h1 = weight_grad(h, dp, 4, D, D, False, "grad_w_in_odd", b_col=lambda j: dp_col(2 * j + 1),
                                 after=token0)
    send1, recv1, thru1, landing1, token1 = exchange_start(gin_h1, odd, "exchange_start_odd")
    grad_x, acc_in, _ = in_proj_backward(dp, win_all, xs, dx1, pre_norm_tm, sc1, sh1 + token1[0:1, 0:1], [])

    own = [ga_f.reshape(NDEV, rows, D), gb_f.reshape(NDEV, rows, D), go_f.reshape(NDEV, rows, D), g1_f, g2_f]
    recv = [r_a, r_b, r_o, r_ff1, r_ff2]
    big = {}
    names = ["w_br_a", "w_br_b", "w_out", "w_ff1", "w_ff2"]
    ws = [w_br_a, w_br_b, w_out, w_ff1, w_ff2]
    ms = [m_w_br_a, m_w_br_b, m_w_out, m_w_ff1, m_w_ff2]
    vs = [v_w_br_a, v_w_br_b, v_w_out, v_w_ff1, v_w_ff2]
    sel = jnp.stack([me, 0]).astype(jnp.int32)
    for i, nm in enumerate(names):
        big[nm] = [o[None] for o in reduce_and_adamw(sel, ws[i][0], [own[i]], [recv[i]], ms[i][0], vs[i][0],
                                                     "adamw_" + nm)]
    big, grad_x, acc_in = lax.optimization_barrier((big, grad_x, acc_in))
    r_in0 = exchange_wait(send0, recv0, thru0, landing0, acc_in, even, "exchange_wait_even")
    r_in1 = exchange_wait(send1, recv1, thru1, landing1, acc_in, odd, "exchange_wait_odd")
    r_in0, r_in1, acc_in = lax.optimization_barrier((r_in0, r_in1, acc_in))
    sel_in = jnp.stack([me // 2, me % 2]).astype(jnp.int32)
    big["w_in"] = [o[None] for o in reduce_and_adamw(sel_in, w_in[0], [gin_f0, gin_f1], [r_in0, r_in1], m_w_in[0],
                                                     v_w_in[0], "adamw_w_in")]

    dmod_rows = [2, 1, 8, 19, 18, 20]
    total, kept = gather_and_sum_rows([acc_in, acc_mix, acc_ffn, acc_hg, acc_conv, ddw], dmod_rows)
    dmod_all = kept[:, 0:6, :].reshape(NDEV, 6 * D)
    wcols = w_ada.shape[2]
    gwa, dwa, nmwa, nvwa = ada_backward(call.T, lax.dynamic_slice_in_dim(dmod_all, me * wcols, wcols, axis=1),
                                        w_ada[0], m_w_ada[0], v_w_ada[0])
    ddw_mine = lax.dynamic_slice_in_dim(total[40:40 + KW], me * (D // NDEV), D // NDEV, axis=1)

    small_names = ["b_ada", "pre_norm_tm", "post_norm_tm", "pre_norm_cm", "post_norm_cm", "b_in", "hg_lb_logits",
                   "hg_norm", "conv_db", "conv_ln_g", "conv_ln_b", "conv_dw"]
    recipes = [dmod_rows, [0], [9], [16], [17], [26, 27, 28, 11, 35, 36, 12, 13], (24, 26), [10], [32], [33], [34],
               "dw"]
    small_w = [b_ada, pre_norm_tm, post_norm_tm, pre_norm_cm, post_norm_cm, b_in, hg_lb_logits, hg_norm, conv_db,
               conv_ln_g, conv_ln_b, conv_dw[0]]
    small_m = [m_b_ada, m_pre_norm_tm, m_post_norm_tm, m_pre_norm_cm, m_post_norm_cm, m_b_in, m_hg_lb_logits,
               m_hg_norm, m_conv_db, m_conv_ln_g, m_conv_ln_b, m_conv_dw[0]]
    small_v = [v_b_ada, v_pre_norm_tm, v_post_norm_tm, v_pre_norm_cm, v_post_norm_cm, v_b_in, v_hg_lb_logits,
               v_hg_norm, v_conv_db, v_conv_ln_g, v_conv_ln_b, v_conv_dw[0]]
    small_out, loss_row = adamw_small(total, ddw_mine, recipes, small_w, small_m, small_v)
    loss = loss_row[0, 0]
    sm = {nm: list(o) for nm, o in zip(small_names, small_out)}
    sm["conv_dw"] = [o[None] for o in sm["conv_dw"]]

    order = ["w_ada", "b_ada", "pre_norm_tm", "post_norm_tm", "pre_norm_cm", "post_norm_cm", "w_in", "b_in",
             "hg_lb_logits", "hg_norm", "conv_dw", "conv_db", "conv_ln_g", "conv_ln_b", "w_br_a", "w_br_b", "w_out",
             "w_ff1", "w_ff2"]
    res = dict(sm)
    res.update(big)
    res["w_ada"] = [gwa[None], dwa[None], nmwa[None], nvwa[None]]
    outs = [loss, grad_x[None]]
    for j in range(4):
        outs += [res[nm][j] for nm in order]
    return tuple(outs)
```

```python
import jax
import jax.numpy as jnp
from jax import lax
from jax.experimental import pallas as pl
from jax.experimental.pallas import tpu as pltpu

F32 = jnp.float32
BF16 = jnp.bfloat16
MESH = pl.DeviceIdType.MESH

D = 1024
HEADS = 8
DK = 128
LANE = 128
CH = 128
LEVELS = 7
KW = 31
HALO = 32
DFF = 4096
NDEV = 8
EPS = 1e-6
DP_SPLIT = (3, 6, 7, 0, 1, 2, 4, 5)
LOSS_ROW = 21
ADAM_LR, ADAM_B1, ADAM_B2, ADAM_EPS, ADAM_WD, ADAM_STEP = 0.001, 0.9, 0.999, 1e-08, 0.01, 10
VMEM_LIMIT = 58 * 1024 * 1024

_NN = (((1,), (0,)), ((), ()))
_NT = (((1,), (1,)), ((), ()))
_TN = (((0,), (0,)), ((), ()))

VSPEC = pl.BlockSpec(memory_space=pltpu.VMEM)
ANYSPEC = pl.BlockSpec(memory_space=pl.ANY)


def _params(*sem):
    return pltpu.CompilerParams(dimension_semantics=sem or None, vmem_limit_bytes=VMEM_LIMIT)


def _dot(a, b, dims):
    return lax.dot_general(a, b, dims, preferred_element_type=F32)


@jax.custom_vjp
def mm_nn(a, b):
    return _dot(a.astype(BF16), b.astype(BF16), _NN)


def _mm_nn_fwd(a, b):
    ab, bb = a.astype(BF16), b.astype(BF16)
    return _dot(ab, bb, _NN), (ab, bb)


def _mm_nn_bwd(res, ct):
    ab, bb = res
    cb = ct.astype(BF16)
    return _dot(cb, bb, _NT), _dot(ab, cb, _TN)


mm_nn.defvjp(_mm_nn_fwd, _mm_nn_bwd)


@jax.custom_vjp
def mm_nt(a, b):
    return _dot(a.astype(BF16), b.astype(BF16), _NT)


def _mm_nt_fwd(a, b):
    ab, bb = a.astype(BF16), b.astype(BF16)
    return _dot(ab, bb, _NT), (ab, bb)


def _mm_nt_bwd(res, ct):
    ab, bb = res
    cb = ct.astype(BF16)
    return _dot(cb, bb, _NN), _dot(cb, ab, _TN)


mm_nt.defvjp(_mm_nt_fwd, _mm_nt_bwd)


@jax.custom_vjp
def mm_tn(a, b):
    return _dot(a.astype(BF16), b.astype(BF16), _TN)


def _mm_tn_fwd(a, b):
    ab, bb = a.astype(BF16), b.astype(BF16)
    return _dot(ab, bb, _TN), (ab, bb)


def _mm_tn_bwd(res, ct):
    ab, bb = res
    cb = ct.astype(BF16)
    return _dot(bb, cb, _NT), _dot(ab, cb, _NN)


mm_tn.defvjp(_mm_tn_fwd, _mm_tn_bwd)


def _rms(x):
    return x * lax.rsqrt(jnp.mean(x * x, axis=-1, keepdims=True) + EPS)


_sigmoid = jax.nn.sigmoid


def _silu(x):
    return x * _sigmoid(x)


def f_modulate(x, gain, sc, sh):
    return _rms(x) * gain * (1.0 + sc) + sh


def f_residual(x, y, gate, gain):
    return x + gate * (_rms(y) * gain)


def f_merge(ga, gb, ya, yb):
    return _sigmoid(ga) * ya + _sigmoid(gb) * yb


def f_head_out(o, og, hg):
    heads = [_rms(o[:, h * DK:(h + 1) * DK]) for h in range(HEADS)]
    return jnp.concatenate(heads, axis=1) * hg * _silu(og)


def f_conv_act(u, g, b):
    mu = jnp.mean(u, axis=-1, keepdims=True)
    var = jnp.mean(jnp.square(u - mu), axis=-1, keepdims=True)
    return _silu((u - mu) * lax.rsqrt(var + EPS) * g + b)


def f_glu(cv, cg):
    return cv * _sigmoid(cg)


def _split2(x):
    hi = x.astype(BF16)
    return hi, (x - hi.astype(F32)).astype(BF16)


def _tri(transposed):
    i = lax.broadcasted_iota(jnp.int32, (CH, CH), 1 if transposed else 0)
    t = lax.broadcasted_iota(jnp.int32, (CH, CH), 0 if transposed else 1)
    return jnp.where(t <= i, 1.0, 0.0).astype(BF16)


def _blocks3(x, rows):
    return x.reshape(CH // rows, rows, x.shape[-1])


def _mid_broadcast(b, lev):
    h = 1 << (LEVELS - 1 - lev)
    if h >= 4:
        x3 = _blocks3(b, 2 * h)
        return jnp.broadcast_to(x3[:, h - 1:h, :], x3.shape).reshape(b.shape)
    x3 = _blocks3(b, 8)
    sub = lax.broadcasted_iota(jnp.int32, x3.shape, 1)
    out = None
    for first in range(0, 8, 2 * h):
        piece = jnp.broadcast_to(x3[:, first + h - 1:first + h, :], x3.shape)
        out = piece if out is None else jnp.where(sub >= first, piece, out)
    return out.reshape(b.shape)


def _mid_scatter(d, lev):
    h = 1 << (LEVELS - 1 - lev)
    if h >= 4:
        x3 = _blocks3(d, 2 * h)
        row = lax.broadcasted_iota(jnp.int32, x3.shape, 1)
        total = jnp.sum(x3, axis=1, keepdims=True)
        return jnp.where(row == h - 1, total, 0.0).reshape(d.shape)
    x3 = _blocks3(d, 8)
    sub = lax.broadcasted_iota(jnp.int32, x3.shape, 1)
    out = jnp.zeros_like(x3)
    for first in range(0, 8, 2 * h):
        inside = (sub >= first) & (sub < first + 2 * h)
        total = jnp.sum(jnp.where(inside, x3, 0.0), axis=1, keepdims=True)
        out = jnp.where(sub == first + h - 1, total, out)
    return out.reshape(d.shape)


@jax.custom_vjp
def decay_sums(g):
    hi, lo = _split2(g)
    tri = _tri(False)
    b = _dot(tri, hi, _NN) + _dot(tri, lo, _NN)
    return (b,) + tuple(b - _mid_broadcast(b, lev) for lev in range(LEVELS))


def _decay_sums_fwd(g):
    return decay_sums(g), None


def _decay_sums_bwd(_, cts):
    db = cts[0]
    for lev in range(LEVELS):
        db = db + cts[1 + lev] - _mid_scatter(cts[1 + lev], lev)
    hi, lo = _split2(db)
    tri = _tri(True)
    return (_dot(tri, hi, _NN) + _dot(tri, lo, _NN),)


decay_sums.defvjp(_decay_sums_fwd, _decay_sums_bwd)


def _score_masks():
    i = lax.broadcasted_iota(jnp.int32, (CH, CH), 0)
    j = lax.broadcasted_iota(jnp.int32, (CH, CH), 1)
    masks = [i == j]
    for lev in range(LEVELS):
        sh = LEVELS - 1 - lev
        same = (i >> (sh + 1)) == (j >> (sh + 1))
        masks.append(same & (((i >> sh) & 1) == 1) & (((j >> sh) & 1) == 0))
    return [jnp.where(m, 1.0, 0.0) for m in masks]


def f_hgrn_chunk(q_r, f_r, v, logits):
    l0, l1 = logits[0:1, :], logits[1:2, :]
    mx = lax.stop_gradient(jnp.maximum(l0, l1))
    e0, e1 = jnp.exp(l0 - mx), jnp.exp(l1 - mx)
    lb = e0 / (e0 + e1)
    q = _silu(q_r)
    f = lb + (1.0 - lb) * _sigmoid(f_r)
    k = 1.0 - f
    sums = decay_sums(jnp.log(f))
    b = sums[0]
    btot = b[CH - 1:CH, :]
    qe = q * jnp.exp(b)
    ke = k * jnp.exp(btot - b)
    dec = jnp.exp(btot)
    qs, ks = [q], [k]
    row = lax.broadcasted_iota(jnp.int32, b.shape, 0)
    for lev in range(LEVELS):
        upper = ((row >> (LEVELS - 1 - lev)) & 1) == 1
        e = sums[1 + lev]
        qs.append(q * jnp.exp(jnp.where(upper, e, 0.0)))
        ks.append(k * jnp.exp(jnp.where(upper, 0.0, -e)))
    masks = _score_masks()
    intra, ut = [], []
    for h in range(HEADS):
        sl = slice(h * DK, (h + 1) * DK)
        sc = None
        for lev in range(LEVELS + 1):
            s = mm_nt(qs[lev][:, sl], ks[lev][:, sl]) * masks[lev]
            sc = s if sc is None else sc + s
        intra.append(mm_nn(sc, v[:, sl]))
        ut.append(mm_tn(v[:, sl], ke[:, sl]))
    return jnp.concatenate(intra, axis=1), qe, jnp.concatenate(ut, axis=1), dec


def _inter(qe_b, st_ref, rows):
    out = []
    for ci in range(rows // CH):
        st = st_ref[ci].astype(BF16)
        heads = [_dot(qe_b[ci * CH:(ci + 1) * CH, h * DK:(h + 1) * DK], st[:, h * DK:(h + 1) * DK], _NT)
                 for h in range(HEADS)]
        out.append(jnp.concatenate(heads, axis=1))
    return jnp.concatenate(out, axis=0)


def _shift_stack(src, dst, rows):
    dst[0, 0:rows, :] = src[0:rows, :]
    for b in range(1, 8):
        dst[b, 0:rows - 8, :] = src[pl.ds(b, rows - 8), :]


def _shifted(stack, offset, rows, lanes=slice(None)):
    return stack[offset % 8, pl.ds(8 * (offset // 8), rows), lanes]


def _adamw(w, g, m, v):
    m = ADAM_B1 * m + (1.0 - ADAM_B1) * g
    v = ADAM_B2 * v + (1.0 - ADAM_B2) * jnp.square(g)
    m_hat = m / (1.0 - ADAM_B1 ** ADAM_STEP)
    v_hat = v / (1.0 - ADAM_B2 ** ADAM_STEP)
    delta = -ADAM_LR * (m_hat / (jnp.sqrt(v_hat) + ADAM_EPS) + ADAM_WD * w)
    return delta, m, v


def _me():
    return lax.axis_index("x"), lax.axis_index("y"), lax.axis_index("c")


def _peer(k):
    x, y, c = _me()
    mask = k + 1
    px = (1 - x) if (mask >> 2) & 1 else x
    py = (1 - y) if (mask >> 1) & 1 else y
    pc = (1 - c) if mask & 1 else c
    return (px, py, pc), 4 * px + 2 * py + pc


def ada_forward(c, w_ada, b_ada, dw):
    wcols = w_ada.shape[1]

    def body(c_ref, w_ref, b_ref, dw_ref, mod_ref, call_ref, dwall_ref, part_ref, modp_ref, send_sems, recv_sems):
        x, y, cc = _me()
        me = 4 * x + 2 * y + cc
        call_ref[me] = _silu(c_ref[...])
        dwall_ref[me] = dw_ref[...]
        sends = []
        for k in range(NDEV - 1):
            dev, pidx = _peer(k)
            cp = pltpu.make_async_remote_copy(dwall_ref.at[me], dwall_ref.at[me], send_sems.at[2 * (NDEV - 1) + k],
                                              recv_sems.at[2 * (NDEV - 1) + k], device_id=dev, device_id_type=MESH)
            cp.start()
            sends.append(cp)
        for k in range(NDEV - 1):
            dev, _ = _peer(k)
            cp = pltpu.make_async_remote_copy(call_ref.at[me], call_ref.at[me], send_sems.at[k], recv_sems.at[k],
                                              device_id=dev, device_id_type=MESH)
            cp.start()
            sends.append(cp)
        for k in range(NDEV - 1):
            _, pidx = _peer(k)
            pltpu.make_async_remote_copy(call_ref.at[pidx], call_ref.at[pidx], send_sems.at[k], recv_sems.at[k],
                                         device_id=_peer(k)[0], device_id_type=MESH).wait_recv()
        call = jnp.concatenate([call_ref[r] for r in range(NDEV)], axis=0)
        part = _dot(call.astype(BF16), w_ref[...].astype(BF16), _NN)
        for r in range(NDEV):
            part_ref[r] = part[r:r + 1, :]
        modp_ref[me] = part_ref[me]
        for k in range(NDEV - 1):
            dev, pidx = _peer(k)
            cp = pltpu.make_async_remote_copy(part_ref.at[pidx], modp_ref.at[me], send_sems.at[NDEV - 1 + k],
                                              recv_sems.at[NDEV - 1 + k], device_id=dev, device_id_type=MESH)
            cp.start()
            sends.append(cp)
        for k in range(NDEV - 1):
            dev, pidx = _peer(k)
            pltpu.make_async_remote_copy(part_ref.at[pidx], modp_ref.at[pidx], send_sems.at[NDEV - 1 + k],
                                         recv_sems.at[NDEV - 1 + k], device_id=dev, device_id_type=MESH).wait_recv()
        for k in range(NDEV - 1):
            dev, pidx = _peer(k)
            pltpu.make_async_remote_copy(dwall_ref.at[pidx], dwall_ref.at[pidx], send_sems.at[2 * (NDEV - 1) + k],
                                         recv_sems.at[2 * (NDEV - 1) + k], device_id=dev,
                                         device_id_type=MESH).wait_recv()
        for cp in sends:
            cp.wait_send()
        mod_ref[...] = jnp.concatenate([modp_ref[r] for r in range(NDEV)], axis=1) + b_ref[...]

    mod, call, dw_all = pl.pallas_call(
        body, name="ada_forward",
        out_shape=(jax.ShapeDtypeStruct((1, NDEV * wcols), F32), jax.ShapeDtypeStruct((NDEV, 1, D), F32),
                   jax.ShapeDtypeStruct((NDEV,) + dw.shape, F32)),
        in_specs=[VSPEC] * 4, out_specs=(VSPEC,) * 3,
        scratch_shapes=[pltpu.VMEM((NDEV, 1, wcols), F32), pltpu.VMEM((NDEV, 1, wcols), F32),
                        pltpu.SemaphoreType.DMA((3 * (NDEV - 1),)), pltpu.SemaphoreType.DMA((3 * (NDEV - 1),))],
        compiler_params=_params(),
    )(c, w_ada, b_ada, dw)
    return mod, call.reshape(NDEV, D), dw_all


def _comm_sems(n, local):
    sems = [pltpu.SemaphoreType.DMA((7 * n,)), pltpu.SemaphoreType.DMA((7 * n,))]
    return sems + ([pltpu.SemaphoreType.DMA((n,))] if local else [])


def _gather2_ops(ins, outs, send_sems, recv_sems, local_sems):
    n = len(ins)
    x, y, c = _me()
    me, sibling = (x, y, c), (x, y, 1 - c)
    chips = [(1 - x, y), (x, 1 - y), (1 - x, 1 - y)]

    def slot(p):
        return 4 * p[0] + 2 * p[1] + p[2]

    def copy(a, k, block, to, src=None):
        return pltpu.make_async_remote_copy(
            src_ref=outs[a].at[slot(block)] if src is None else src, dst_ref=outs[a].at[slot(block)],
            send_sem=send_sems.at[a * 7 + k], recv_sem=recv_sems.at[a * 7 + k], device_id=to, device_id_type=MESH)

    def local(a):
        return pltpu.make_async_copy(ins[a], outs[a].at[slot(me)], local_sems.at[a])

    def first(a):
        return [copy(a, 0, me, sibling, src=ins[a])] + [copy(a, 1 + j, me, (*chip, c), src=ins[a])
                                                        for j, chip in enumerate(chips)]

    def passed(a):
        return [copy(a, 4 + j, (*chip, c), sibling) for j, chip in enumerate(chips)]

    def start():
        for a in range(n):
            local(a).start()
            for cp in first(a):
                cp.start()

    def forward():
        for j, chip in enumerate(chips):
            for a in range(n):
                copy(a, 1 + j, (*chip, c), me).wait_recv()
                passed(a)[j].start()

    def finish():
        for a in range(n):
            copy(a, 0, sibling, me).wait_recv()
            for j, chip in enumerate(chips):
                copy(a, 4 + j, (*chip, 1 - c), me).wait_recv()
        for a in range(n):
            for cp in first(a) + passed(a):
                cp.wait_send()
            local(a).wait()

    return start, forward, finish


def _exchange_ops(ins, outs, send_sems, recv_sems):
    n = len(ins)

    def copy(a, k):
        dev, pidx = _peer(k)
        return pltpu.make_async_remote_copy(ins[a].at[pidx], outs[a].at[k], send_sems.at[a * 7 + k],
                                            recv_sems.at[a * 7 + k], device_id=dev, device_id_type=MESH)

    def start():
        for k in range(NDEV - 1):
            for a in range(n):
                copy(a, k).start()

    def finish():
        for k in range(NDEV - 1):
            for a in range(n):
                copy(a, k).wait_recv()
        for k in range(NDEV - 1):
            for a in range(n):
                copy(a, k).wait_send()

    return start, finish


def _gathered(shards):
    return tuple(jax.ShapeDtypeStruct((NDEV,) + s.shape, s.dtype) for s in shards)


def _exchanged(blocks):
    return tuple(jax.ShapeDtypeStruct((NDEV - 1,) + b.shape[1:], b.dtype) for b in blocks)


def _owner_copy(src_ref, land_ref, send_sems, recv_sems, gi, n):
    x, y, c = _me()
    k = jnp.bitwise_xor(4 * x + 2 * y + c, n) - 1
    return pltpu.make_async_remote_copy(src_ref.at[gi], land_ref.at[k], send_sems.at[gi], recv_sems.at[k],
                                        device_id=(n >> 2, (n >> 1) & 1, n & 1), device_id_type=MESH)


def exchange_start(blocks, owners, name):
    landing = lax.empty((NDEV - 1,) + blocks.shape[1:], blocks.dtype)
    hbm = pl.BlockSpec(memory_space=pltpu.HBM)
    sem = pl.BlockSpec(memory_space=pltpu.SEMAPHORE)

    def body(src_ref, land_ref, send_sems, recv_sems, src_thru, land_thru, token):
        x, y, c = _me()
        for gi, n in enumerate(owners):
            @pl.when(4 * x + 2 * y + c != n)
            def _(gi=gi, n=n):
                _owner_copy(src_ref, land_ref, send_sems, recv_sems, gi, n).start()
        token[...] = jnp.zeros_like(token)

    return pl.pallas_call(
        body, name=name,
        out_shape=(pltpu.SemaphoreType.DMA((len(owners),)), pltpu.SemaphoreType.DMA((NDEV - 1,)),
                   pltpu.HBM(blocks.shape, blocks.dtype), pltpu.HBM(landing.shape, landing.dtype),
                   jax.ShapeDtypeStruct((8, 128), F32)),
        in_specs=(hbm, hbm), out_specs=(sem, sem, hbm, hbm, VSPEC), input_output_aliases={0: 2, 1: 3},
        compiler_params=pltpu.CompilerParams(has_side_effects=pltpu.SideEffectType.DATAFLOW_SIDE_EFFECTING),
    )(pltpu.with_memory_space_constraint(blocks, pltpu.HBM), pltpu.with_memory_space_constraint(landing, pltpu.HBM))


def exchange_wait(send_sems, recv_sems, src_thru, land_thru, after, owners, name):
    hbm = pl.BlockSpec(memory_space=pltpu.HBM)
    sem = pl.BlockSpec(memory_space=pltpu.SEMAPHORE)

    def body(src_ref, land_ref, send_sems, recv_sems, after_ref, src_dead, got_ref):
        x, y, c = _me()
        me = 4 * x + 2 * y + c
        for gi, n in enumerate(owners):
            @pl.when(me != n)
            def _(gi=gi, n=n):
                _owner_copy(src_ref, land_ref, send_sems, recv_sems, gi, n).wait_send()

            @pl.when(me == n)
            def _(gi=gi):
                for k in range(NDEV - 1):
                    pltpu.make_async_remote_copy(src_ref.at[gi], land_ref.at[k], send_sems.at[gi], recv_sems.at[k],
                                                 device_id=_peer(k)[0], device_id_type=MESH).wait_recv()

    return pl.pallas_call(
        body, name=name,
        out_shape=(pltpu.HBM(src_thru.shape, src_thru.dtype), pltpu.HBM(land_thru.shape, land_thru.dtype)),
        in_specs=(hbm, hbm, sem, sem, ANYSPEC), out_specs=(hbm, hbm), input_output_aliases={0: 0, 1: 1},
        compiler_params=pltpu.CompilerParams(has_side_effects=pltpu.SideEffectType.DATAFLOW_SIDE_EFFECTING),
    )(src_thru, land_thru, send_sems, recv_sems, after)[1]


def gather_and_sum_rows(parts, keep_rows):
    n = len(parts)
    offs = [sum(p.shape[0] for p in parts[:i]) for i in range(n)]
    rows = sum(p.shape[0] for p in parts)

    def body(*refs):
        sum_ref, keep_ref, buf_ref, send_sems, recv_sems = refs[n:]
        x, y, c = _me()
        me = 4 * x + 2 * y + c
        for i in range(n):
            buf_ref[me, offs[i]:offs[i] + parts[i].shape[0], :] = refs[i][...]

        def copy(k, block):
            return pltpu.make_async_remote_copy(buf_ref.at[block], buf_ref.at[block], send_sems.at[k],
                                                recv_sems.at[k], device_id=_peer(k)[0], device_id_type=MESH)

        for k in range(NDEV - 1):
            copy(k, me).start()
        for k in range(NDEV - 1):
            copy(k, _peer(k)[1]).wait_recv()
        for k in range(NDEV - 1):
            copy(k, me).wait_send()
        total = buf_ref[0]
        for r in range(1, NDEV):
            total = total + buf_ref[r]
        sum_ref[...] = total
        keep_ref[...] = jnp.zeros_like(keep_ref)
        for r in range(NDEV):
            for j, src in enumerate(keep_rows):
                keep_ref[r, j:j + 1, :] = buf_ref[r, src:src + 1, :]

    return pl.pallas_call(
        body, name="gather_and_sum_rows",
        out_shape=(jax.ShapeDtypeStruct((rows, D), F32), jax.ShapeDtypeStruct((NDEV, 8, D), F32)),
        in_specs=[VSPEC] * n, out_specs=(VSPEC, VSPEC),
        scratch_shapes=[pltpu.VMEM((NDEV, rows, D), F32), pltpu.SemaphoreType.DMA((NDEV - 1,)),
                        pltpu.SemaphoreType.DMA((NDEV - 1,))],
        compiler_params=_params(),
    )(*parts)


def _gather_chips():
    x, y, c = _me()
    north = c == 1
    first = (jnp.where(north, 1 - x, x), jnp.where(north, y, 1 - y))
    second = (jnp.where(north, x, 1 - x), jnp.where(north, 1 - y, y))
    return [first, second, (1 - x, 1 - y)]


def _arrival_order():
    x, y, c = _me()
    first, second, far = _gather_chips()
    devs = [(x, y, c), (x, y, 1 - c), (*first, c), (*second, 1 - c), (*second, c), (*first, 1 - c), (*far, c),
            (*far, 1 - c)]
    return jnp.stack([4 * d[0] + 2 * d[1] + d[2] for d in devs]).astype(jnp.int32)


def in_proj(x, gain, sc, sh, w_shard, b_in):
    t = x.shape[0]
    tm = min(1024, t // 2)
    nm = t // tm
    order = _arrival_order()

    def body(order_ref, x_ref, g_ref, sc_ref, sh_ref, b_ref, wsh_ref, p_ref, h_ref, wall_ref,
             wbuf, h_all, send_sems, recv_sems, local_sems):
        k, m = pl.program_id(0), pl.program_id(1)
        mx, my, mc = _me()
        me, sibling = (mx, my, mc), (mx, my, 1 - mc)
        chips = _gather_chips()

        def slot(d):
            return 4 * d[0] + 2 * d[1] + d[2]

        def copy(sem, block, to, src=None):
            return pltpu.make_async_remote_copy(
                src_ref=wall_ref.at[slot(block)] if src is None else src, dst_ref=wall_ref.at[slot(block)],
                send_sem=send_sems.at[sem], recv_sem=recv_sems.at[sem], device_id=to, device_id_type=MESH)

        own = pltpu.make_async_copy(wsh_ref, wall_ref.at[slot(me)], local_sems.at[0])
        first = [copy(0, me, sibling, src=wsh_ref)] + [copy(1 + j, me, (*q, mc), src=wsh_ref)
                                                       for j, q in enumerate(chips)]
        passed = [copy(4 + j, (*q, mc), sibling) for j, q in enumerate(chips)]

        def load(kk):
            src = wsh_ref if kk == 0 else wall_ref.at[order_ref[kk]]
            return pltpu.make_async_copy(src, wbuf.at[kk % 2], local_sems.at[1 + kk % 2])

        def arrived(kk):
            if kk == 1:
                copy(0, sibling, me).wait_recv()
            elif kk in (2, 4, 6):
                j = kk // 2 - 1
                copy(1 + j, (*chips[j], mc), me).wait_recv()
                passed[j].start()
                if j < 2:
                    first[2 + j].start()
            else:
                j = {3: 1, 5: 0, 7: 2}[kk]
                copy(4 + {1: 0, 0: 1, 2: 2}[j], (*chips[j], 1 - mc), me).wait_recv()

        @pl.when((k == 0) & (m == 0))
        def _():
            own.start()
            for cp in first[:2]:
                cp.start()
            load(0).start()
            load(0).wait()

        for kk in range(1, NDEV):
            @pl.when((k == kk - 1) & (m == nm // 2))
            def _(kk=kk):
                arrived(kk)
                load(kk).start()

            @pl.when((k == kk) & (m == 0))
            def _(kk=kk):
                load(kk).wait()

        rows = pl.ds(pl.multiple_of(m * tm, tm), tm)

        @pl.when(k == 0)
        def _():
            h_all[rows, :] = f_modulate(x_ref[...], g_ref[...], sc_ref[...], sh_ref[...]).astype(BF16)

        p_ref[...] = _dot(h_all[rows, :], wbuf[k % 2], _NN) + b_ref[...]

        @pl.when((k == NDEV - 1) & (m == nm - 1))
        def _():
            for cp in first + passed:
                cp.wait_send()
            own.wait()
            out = pltpu.make_async_copy(h_all, h_ref, local_sems.at[0])
            out.start()
            out.wait()

    vec = pl.BlockSpec((1, D), lambda k, m, o: (0, 0))
    return pl.pallas_call(
        body, name="in_proj",
        out_shape=(jax.ShapeDtypeStruct((t, NDEV * D), F32), jax.ShapeDtypeStruct((t, D), BF16),
                   jax.ShapeDtypeStruct((NDEV, D, D), BF16)),
        grid_spec=pltpu.PrefetchScalarGridSpec(
            num_scalar_prefetch=1, grid=(NDEV, nm),
            in_specs=[pl.BlockSpec((tm, D), lambda k, m, o: (jnp.where(k == 0, m, 0), 0)), vec, vec, vec,
                      pl.BlockSpec((1, D), lambda k, m, o: (0, o[k])), ANYSPEC],
            out_specs=(pl.BlockSpec((tm, D), lambda k, m, o: (m, o[k])), ANYSPEC, ANYSPEC),
            scratch_shapes=[pltpu.VMEM((2, D, D), BF16), pltpu.VMEM((t, D), BF16), pltpu.SemaphoreType.DMA((7,)),
                            pltpu.SemaphoreType.DMA((7,)), pltpu.SemaphoreType.DMA((3,))]),
        compiler_params=_params("arbitrary", "arbitrary"),
    )(order, x, gain, sc, sh, b_in, w_shard)


def hgrn_local(p, logits, shards):
    t = p.shape[0]
    nc = t // CH
    n = len(shards)

    def body(*refs):
        q_ref, f_ref, v_ref, l_ref = refs[:4]
        intra_ref, qe_ref, st_ref, dec_ref = refs[4 + n:8 + n]
        state = refs[8 + 2 * n]
        start, forward, finish = _gather2_ops(refs[4:4 + n], refs[8 + n:8 + 2 * n], *refs[9 + 2 * n:])
        step = pl.program_id(0)

        @pl.when(step == 0)
        def _():
            start()
            state[...] = jnp.zeros_like(state)

        intra, qe, ut, dec = f_hgrn_chunk(q_ref[...], f_ref[...], v_ref[...], l_ref[...])
        intra_ref[...] = intra
        qe_ref[...] = qe.astype(BF16)
        dec_ref[...] = dec
        s = state[...]
        st_ref[...] = s
        state[...] = dec * s + ut
        pl.when(step == (3 * nc) // 4)(forward)
        pl.when(step == nc - 1)(finish)

    col = lambda j: pl.BlockSpec((CH, D), lambda c: (c, j))
    res = pl.pallas_call(
        body, name="hgrn_local", grid=(nc,),
        out_shape=(jax.ShapeDtypeStruct((t, D), F32), jax.ShapeDtypeStruct((t, D), BF16),
                   jax.ShapeDtypeStruct((nc, DK, D), F32), jax.ShapeDtypeStruct((nc, 1, D), F32)) + _gathered(shards),
        in_specs=[col(0), col(1), col(2), pl.BlockSpec((2, D), lambda c: (0, 0))] + [ANYSPEC] * n,
        out_specs=(pl.BlockSpec((CH, D), lambda c: (c, 0)), pl.BlockSpec((CH, D), lambda c: (c, 0)),
                   pl.BlockSpec((None, DK, D), lambda c: (c, 0, 0)), pl.BlockSpec((None, 1, D), lambda c: (c, 0, 0)))
        + (ANYSPEC,) * n,
        scratch_shapes=[pltpu.VMEM((DK, D), F32)] + _comm_sems(n, True),
        compiler_params=_params("arbitrary"),
    )(p, p, p, logits, *shards)
    return res[:4] + (res[4:],)


def _conv_tile(t):
    return min(256, t)


def conv_forward(p, dw, db, ln_g, ln_b, shards):
    t = p.shape[0]
    tm = _conv_tile(t)
    per = tm // HALO
    n = len(shards)
    nt = t // tm

    def body(*refs):
        cv_ref, cg_ref, cvp_ref, cgp_ref, dw_ref, db_ref, g_ref, b_ref = refs[:8]
        yc_ref, act_ref = refs[8 + n:10 + n]
        uext, ush = refs[10 + 2 * n:12 + 2 * n]
        start, forward, finish = _gather2_ops(refs[8:8 + n], refs[10 + n:10 + 2 * n], *refs[12 + 2 * n:])
        step = pl.program_id(0)
        pl.when(step == 0)(start)
        pl.when(step == nt - 1)(forward)
        uext[0:HALO, :] = jnp.where(step == 0, 0.0, f_glu(cvp_ref[...], cgp_ref[...]))
        uext[HALO:HALO + tm, :] = f_glu(cv_ref[...], cg_ref[...])
        _shift_stack(uext, ush, tm + HALO)
        acc = jnp.zeros((tm, D), F32) + db_ref[...]
        for w in range(KW):
            acc = acc + dw_ref[w:w + 1, :] * _shifted(ush, HALO - KW + 1 + w, tm)
        yc_ref[...] = acc
        act_ref[...] = f_conv_act(acc, g_ref[...], b_ref[...]).astype(BF16)
        pl.when(step == nt - 1)(finish)

    vec = pl.BlockSpec((1, D), lambda m: (0, 0))
    prev = lambda j: pl.BlockSpec((HALO, D), lambda m: (jnp.maximum(m * per - 1, 0), j))
    res = pl.pallas_call(
        body, name="conv_forward", grid=(nt,),
        out_shape=(jax.ShapeDtypeStruct((t, D), F32), jax.ShapeDtypeStruct((t, D), BF16)) + _gathered(shards),
        in_specs=[pl.BlockSpec((tm, D), lambda m: (m, 4)), pl.BlockSpec((tm, D), lambda m: (m, 5)), prev(4), prev(5),
                  pl.BlockSpec((HALO, D), lambda m: (0, 0)), vec, vec, vec] + [ANYSPEC] * n,
        out_specs=(pl.BlockSpec((tm, D), lambda m: (m, 0)), pl.BlockSpec((tm, D), lambda m: (m, 0))) + (ANYSPEC,) * n,
        scratch_shapes=[pltpu.VMEM((HALO + tm, D), F32), pltpu.VMEM((8, tm + HALO, D), F32)] + _comm_sems(n, True),
        compiler_params=_params("arbitrary"),
    )(p, p, p, p, dw, db, ln_g, ln_b, *shards)
    return res[0], res[1], res[2:]


def _mix_forward_tile(x_ref, og_ref, ga_ref, gb_ref, qe_ref, intra_ref, st_ref, act_ref, wa_ref, wb_ref, wo_ref,
                      hg_ref, rows):
    o = _inter(qe_ref[...], st_ref, rows) + intra_ref[...]
    a = f_head_out(o, og_ref[...], hg_ref[...])
    ya = _dot(a.astype(BF16), wa_ref[...], _NN)
    yb = _dot(act_ref[...], wb_ref[...], _NN)
    merged = f_merge(ga_ref[...], gb_ref[...], ya, yb)
    y = _dot(merged.astype(BF16), wo_ref[...], _NN)
    return o, a, ya, yb, merged, y


def _mix_specs(tm, tiles=None):
    at = (lambda m: m) if tiles is None else (lambda m: tiles - 1 - m)
    col = lambda j: pl.BlockSpec((tm, D), lambda m: (at(m), j))
    row = pl.BlockSpec((tm, D), lambda m: (at(m), 0))
    per_chunk = lambda rows: pl.BlockSpec((tm // CH, rows, D), lambda m: (at(m), 0, 0))
    return col, row, pl.BlockSpec((1, D), lambda m: (0, 0)), per_chunk


def mix_forward(x, p, qe, intra, st_prev, act, wa, wb, wo, hg, g1, post):
    t = x.shape[0]
    tm = min(512, t)

    def body(x_ref, og_ref, ga_ref, gb_ref, qe_ref, intra_ref, st_ref, act_ref, wa_ref, wb_ref, wo_ref, hg_ref,
             g1_ref, post_ref, x1_ref):
        y = _mix_forward_tile(x_ref, og_ref, ga_ref, gb_ref, qe_ref, intra_ref, st_ref, act_ref, wa_ref, wb_ref,
                              wo_ref, hg_ref, tm)[-1]
        x1_ref[...] = f_residual(x_ref[...], y, g1_ref[...], post_ref[...])

    col, row, vec, per_chunk = _mix_specs(tm)
    return pl.pallas_call(
        body, name="mix_forward", grid=(t // tm,),
        out_shape=jax.ShapeDtypeStruct((t, D), F32),
        in_specs=[row, col(3), col(6), col(7), row, row, per_chunk(DK), row, VSPEC, VSPEC, VSPEC, vec, vec, vec],
        out_specs=row,
        compiler_params=_params("parallel"),
    )(x, p, p, p, qe, intra, st_prev, act, wa, wb, wo, hg, g1, post)


def ffn_forward_backward(x1, target, w1, w2, pre, sc, sh, g2, post):
    t = x1.shape[0]
    tm = min(512, t)
    nb = w1.shape[0]
    fb = w1.shape[2]

    def body(x_ref, tg_ref, w1_ref, w2_ref, pre_ref, sc_ref, sh_ref, g2_ref, post_ref,
             dx_ref, h2_ref, r_ref, dz_ref, dy2_ref, acc_ref, z_sc):
        @pl.when(pl.program_id(0) == 0)
        def _():
            acc_ref[...] = jnp.zeros_like(acc_ref)

        x1v = x_ref[...]
        h2, vjp_h = jax.vjp(f_modulate, x1v, pre_ref[...], sc_ref[...], sh_ref[...])
        h2b = h2.astype(BF16)
        h2_ref[...] = h2b
        y2 = jnp.zeros((tm, D), F32)
        for n in range(nb):
            z = _dot(h2b, w1_ref[n], _NN)
            z_sc[:, n * fb:(n + 1) * fb] = z
            r = jnp.square(jnp.maximum(z, 0.0)).astype(BF16)
            r_ref[:, n * fb:(n + 1) * fb] = r
            y2 = y2 + _dot(r, w2_ref[n], _NN)
        out, vjp_r = jax.vjp(f_residual, x1v, y2, g2_ref[...], post_ref[...])
        err = out - tg_ref[...]
        tok = jnp.mean(jnp.square(err), axis=-1, keepdims=True)
        acc_ref[5:6, :] += 0.5 * jnp.sum(tok, axis=0, keepdims=True)
        dx_a, dy2, dg2, dpost = vjp_r(err * (1.0 / D))
        dy2b = dy2.astype(BF16)
        dy2_ref[...] = dy2b
        dh2 = jnp.zeros((tm, D), F32)
        for n in range(nb):
            dr = _dot(dy2b, w2_ref[n], _NT)
            dz = (dr * (2.0 * jnp.maximum(z_sc[:, n * fb:(n + 1) * fb], 0.0))).astype(BF16)
            dz_ref[:, n * fb:(n + 1) * fb] = dz
            dh2 = dh2 + _dot(dz, w1_ref[n], _NT)
        dx_b, dpre, dsc, dsh = vjp_h(dh2)
        dx_ref[...] = dx_a + dx_b
        acc_ref[0:1, :] += dpre
        acc_ref[1:2, :] += dpost
        acc_ref[2:3, :] += dsc
        acc_ref[3:4, :] += dsh
        acc_ref[4:5, :] += dg2

    row = pl.BlockSpec((tm, D), lambda m: (m, 0))
    wide = pl.BlockSpec((tm, DFF), lambda m: (m, 0), pipeline_mode=pl.Buffered(1))
    vec = pl.BlockSpec((1, D), lambda m: (0, 0))
    return pl.pallas_call(
        body, name="ffn_forward_backward", grid=(t // tm,),
        out_shape=(jax.ShapeDtypeStruct((t, D), F32), jax.ShapeDtypeStruct((t, D), BF16),
                   jax.ShapeDtypeStruct((t, DFF), BF16), jax.ShapeDtypeStruct((t, DFF), BF16),
                   jax.ShapeDtypeStruct((t, D), BF16), jax.ShapeDtypeStruct((8, D), F32)),
        in_specs=[row, row, VSPEC, VSPEC, vec, vec, vec, vec, vec],
        out_specs=(row, row, wide, wide, row, pl.BlockSpec((8, D), lambda m: (0, 0))),
        scratch_shapes=[pltpu.VMEM((tm, DFF), F32)],
        compiler_params=_params("arbitrary"),
    )(x1, target, w1, w2, pre, sc, sh, g2, post)


def mix_backward(x, p, qe, intra, st_prev, act, wa, wb, wo, hg, g1, post, dx1, dec):
    t = x.shape[0]
    tm = min(256, t)
    nc = t // CH

    def body(x_ref, og_ref, ga_ref, gb_ref, qe_ref, intra_ref, st_ref, act_ref, wa_ref, wb_ref, wo_ref, hg_ref,
             g1_ref, post_ref, dx1_ref, dec_ref,
             dp_ref, do_ref, dqe_ref, gt_ref, dact_ref, a_ref, mg_ref, dya_ref, dyb_ref, dy_ref, acc_ref, gstate):
        @pl.when(pl.program_id(0) == 0)
        def _():
            acc_ref[...] = jnp.zeros_like(acc_ref)
            gstate[...] = jnp.zeros_like(gstate)

        o, a, ya, yb, merged, y = _mix_forward_tile(x_ref, og_ref, ga_ref, gb_ref, qe_ref, intra_ref, st_ref, act_ref,
                                                    wa_ref, wb_ref, wo_ref, hg_ref, tm)
        a_ref[...] = a.astype(BF16)
        mg_ref[...] = merged.astype(BF16)
        _, vjp_r = jax.vjp(f_residual, x_ref[...], y, g1_ref[...], post_ref[...])
        _, dy, dg1, dpost = vjp_r(dx1_ref[...])
        dyb16 = dy.astype(BF16)
        dy_ref[...] = dyb16
        dmerged = _dot(dyb16, wo_ref[...], _NT)
        _, vjp_m = jax.vjp(f_merge, ga_ref[...], gb_ref[...], ya, yb)
        dga, dgb, dya, dyb = vjp_m(dmerged)
        dp_ref[:, D:2 * D] = dga.astype(BF16)
        dp_ref[:, 2 * D:3 * D] = dgb.astype(BF16)
        dya16, dyb16b = dya.astype(BF16), dyb.astype(BF16)
        dya_ref[...] = dya16
        dyb_ref[...] = dyb16b
        da = _dot(dya16, wa_ref[...], _NT)
        dact_ref[...] = _dot(dyb16b, wb_ref[...], _NT)
        _, vjp_a = jax.vjp(f_head_out, o, og_ref[...], hg_ref[...])
        do, dog, dhg = vjp_a(da)
        dp_ref[:, 0:D] = dog.astype(BF16)
        do_ref[...] = do
        do16 = do.astype(BF16)
        qe16 = qe_ref[...]
        for ci in reversed(range(tm // CH)):
            st = st_ref[ci].astype(BF16)
            rows = slice(ci * CH, (ci + 1) * CH)
            dqe, vt = [], []
            for h in range(HEADS):
                sl = slice(h * DK, (h + 1) * DK)
                dqe.append(_dot(do16[rows, sl], st[:, sl], _NN))
                vt.append(_dot(do16[rows, sl], qe16[rows, sl], _TN))
            dqe_ref[rows, :] = jnp.concatenate(dqe, axis=1)
            g = gstate[...]
            gt_ref[ci] = g
            gstate[...] = dec_ref[ci] * g + jnp.concatenate(vt, axis=1)
        acc_ref[0:1, :] += dg1
        acc_ref[1:2, :] += dpost
        acc_ref[2:3, :] += dhg
        acc_ref[3:4, :] += jnp.sum(dog, axis=0, keepdims=True)
        acc_ref[4:5, :] += jnp.sum(dga, axis=0, keepdims=True)
        acc_ref[5:6, :] += jnp.sum(dgb, axis=0, keepdims=True)

    nt = t // tm
    col, row, vec, per_chunk = _mix_specs(tm, nt)
    b16 = jax.ShapeDtypeStruct((t, D), BF16)
    f32 = jax.ShapeDtypeStruct((t, D), F32)
    return pl.pallas_call(
        body, name="mix_backward", grid=(nt,),
        out_shape=(jax.ShapeDtypeStruct((t, NDEV * D), BF16), f32, f32, jax.ShapeDtypeStruct((nc, DK, D), F32), f32,
                   b16, b16, b16, b16, b16, jax.ShapeDtypeStruct((8, D), F32)),
        in_specs=[row, col(3), col(6), col(7), row, row, per_chunk(DK), row, VSPEC, VSPEC, VSPEC, vec, vec, vec, row,
                  per_chunk(1)],
        out_specs=(pl.BlockSpec((tm, 3 * D), lambda m: (nt - 1 - m, 0)), row, row, per_chunk(DK), row, row, row, row,
                   row, row, pl.BlockSpec((8, D), lambda m: (0, 0))),
        scratch_shapes=[pltpu.VMEM((DK, D), F32)],
        compiler_params=_params("arbitrary"),
    )(x, p, p, p, qe, intra, st_prev, act, wa, wb, wo, hg, g1, post, dx1, dec)


def conv_backward(p, yc, dact, dw, ln_g, ln_b, blocks, dp):
    t = p.shape[0]
    tm = _conv_tile(t)
    per = tm // HALO
    nt = t // tm
    n = len(blocks)

    def body(*refs):
        cv_ref, cg_ref, cvp_ref, cgp_ref, yc_ref, ycn_ref, da_ref, dan_ref, dw_ref, g_ref, b_ref = refs[:11]
        dp_ref, acc_ref, ddw_ref = refs[12 + n:15 + n]
        uext, dyext, ush, dysh, ddw8, du_sc = refs[15 + 2 * n:21 + 2 * n]
        start, finish = _exchange_ops(refs[11:11 + n], refs[15 + n:15 + 2 * n], *refs[21 + 2 * n:])
        m = pl.program_id(0)

        @pl.when(m == 0)
        def _():
            start()
            acc_ref[...] = jnp.zeros_like(acc_ref)
            ddw8[...] = jnp.zeros_like(ddw8)

        cv, cg = cv_ref[...], cg_ref[...]
        u, vjp_u = jax.vjp(f_glu, cv, cg)
        uext[0:HALO, :] = jnp.where(m == 0, 0.0, f_glu(cvp_ref[...], cgp_ref[...]))
        uext[HALO:HALO + tm, :] = u
        _shift_stack(uext, ush, tm + HALO)
        _, vjp_c = jax.vjp(f_conv_act, yc_ref[...], g_ref[...], b_ref[...])
        dyc, dg, db = vjp_c(da_ref[...])
        _, vjp_n = jax.vjp(f_conv_act, ycn_ref[...], g_ref[...], b_ref[...])
        dyn = vjp_n(dan_ref[...])[0]
        dyext[0:tm, :] = dyc
        dyext[tm:tm + HALO, :] = jnp.where(m == nt - 1, 0.0, dyn)
        _shift_stack(dyext, dysh, tm + HALO)
        rb = min(128, tm)
        for lt in range(D // LANE):
            ls = slice(lt * LANE, (lt + 1) * LANE)
            for r0 in range(0, tm, rb):
                du_l = jnp.zeros((rb, LANE), F32)
                for w in range(KW):
                    du_l = du_l + dw_ref[w:w + 1, ls] * _shifted(dysh, KW - 1 - w + r0, rb, ls)
                du_sc[r0:r0 + rb, ls] = du_l
                dyc_l = dyext[r0:r0 + rb, ls]
                for w in range(KW):
                    prod = dyc_l * _shifted(ush, HALO - KW + 1 + w + r0, rb, ls)
                    part = jnp.sum(prod.reshape(4, rb // 32, 8, LANE), axis=1)
                    ddw8[w, :, ls] += (part[0] + part[1]) + (part[2] + part[3])
        du = du_sc[...]

        @pl.when(m == nt - 1)
        def _():
            ddw_ref[...] = jnp.sum(ddw8[...], axis=1)

        dcv, dcg = vjp_u(du)
        dp_ref[:, 0:D] = dcv.astype(BF16)
        dp_ref[:, D:2 * D] = dcg.astype(BF16)
        acc_ref[0:1, :] += jnp.sum(dyc, axis=0, keepdims=True)
        acc_ref[1:2, :] += dg
        acc_ref[2:3, :] += db
        acc_ref[3:4, :] += jnp.sum(dcv, axis=0, keepdims=True)
        acc_ref[4:5, :] += jnp.sum(dcg, axis=0, keepdims=True)
        pl.when(m == nt - 1)(finish)

    vec = pl.BlockSpec((1, D), lambda m: (0, 0))
    row = pl.BlockSpec((tm, D), lambda m: (m, 0))
    prev = lambda j: pl.BlockSpec((HALO, D), lambda m: (jnp.maximum(m * per - 1, 0), j))
    nxt = pl.BlockSpec((HALO, D), lambda m: (jnp.minimum((m + 1) * per, t // HALO - 1), 0))
    res = pl.pallas_call(
        body, name="conv_backward", grid=(nt,),
        out_shape=(jax.ShapeDtypeStruct(dp.shape, dp.dtype), jax.ShapeDtypeStruct((8, D), F32),
                   jax.ShapeDtypeStruct((HALO, D), F32)) + _exchanged(blocks),
        in_specs=[pl.BlockSpec((tm, D), lambda m: (m, 4)), pl.BlockSpec((tm, D), lambda m: (m, 5)), prev(4), prev(5),
                  row, nxt, row, nxt, pl.BlockSpec((HALO, D), lambda m: (0, 0)), vec, vec] + [ANYSPEC] * (n + 1),
        out_specs=(pl.BlockSpec((tm, 2 * D), lambda m: (m, 3)), pl.BlockSpec((8, D), lambda m: (0, 0)),
                   pl.BlockSpec((HALO, D), lambda m: (0, 0))) + (ANYSPEC,) * n,
        scratch_shapes=[pltpu.VMEM((HALO + tm, D), F32), pltpu.VMEM((tm + HALO, D), F32),
                        pltpu.VMEM((8, tm + HALO, D), F32), pltpu.VMEM((8, tm + HALO, D), F32),
                        pltpu.VMEM((HALO, 8, D), F32), pltpu.VMEM((tm, D), F32)] + _comm_sems(n, False),
        input_output_aliases={11 + n: 0},
        compiler_params=_params("arbitrary"),
    )(p, p, p, p, yc, yc, dact, dact, dw, ln_g, ln_b, *blocks, dp)
    return res[:3] + (res[3:],)


def hgrn_backward(p, logits, do, dqe, gt, st_prev, blocks, dp):
    t = p.shape[0]
    nc = t // CH
    n = len(blocks)

    def body(*refs):
        q_ref, f_ref, v_ref, l_ref, do_ref, dqe_ref, gt_ref, st_ref = refs[:8]
        dp_ref, acc_ref = refs[9 + n:11 + n]
        start, finish = _exchange_ops(refs[8:8 + n], refs[11 + n:11 + 2 * n], *refs[11 + 2 * n:])

        @pl.when(pl.program_id(0) == 0)
        def _():
            start()
            acc_ref[...] = jnp.zeros_like(acc_ref)

        gt_v = gt_ref[...]
        ddec = jnp.sum(gt_v * st_ref[...], axis=0, keepdims=True)
        _, vjp = jax.vjp(f_hgrn_chunk, q_ref[...], f_ref[...], v_ref[...], l_ref[...])
        dq, df, dv, dl = vjp((do_ref[...], dqe_ref[...], gt_v, ddec))
        dp_ref[:, 0:D] = dq.astype(BF16)
        dp_ref[:, D:2 * D] = df.astype(BF16)
        dp_ref[:, 2 * D:3 * D] = dv.astype(BF16)
        acc_ref[0:2, :] += dl
        acc_ref[2:3, :] += jnp.sum(dq, axis=0, keepdims=True)
        acc_ref[3:4, :] += jnp.sum(df, axis=0, keepdims=True)
        acc_ref[4:5, :] += jnp.sum(dv, axis=0, keepdims=True)
        pl.when(pl.program_id(0) == nc - 1)(finish)

    col = lambda j: pl.BlockSpec((CH, D), lambda c: (c, j))
    row = pl.BlockSpec((CH, D), lambda c: (c, 0))
    stspec = pl.BlockSpec((None, DK, D), lambda c: (c, 0, 0))
    res = pl.pallas_call(
        body, name="hgrn_backward", grid=(nc,),
        out_shape=(jax.ShapeDtypeStruct(dp.shape, dp.dtype), jax.ShapeDtypeStruct((8, D), F32)) + _exchanged(blocks),
        in_specs=[col(0), col(1), col(2), pl.BlockSpec((2, D), lambda c: (0, 0)), row, row, stspec, stspec]
        + [ANYSPEC] * (n + 1),
        out_specs=(pl.BlockSpec((CH, 3 * D), lambda c: (c, 1)), pl.BlockSpec((8, D), lambda c: (0, 0)))
        + (ANYSPEC,) * n,
        scratch_shapes=_comm_sems(n, False),
        input_output_aliases={8 + n: 0},
        compiler_params=_params("arbitrary"),
    )(p, p, p, logits, do, dqe, gt, st_prev, *blocks, dp)
    return res[:2] + (res[2:],)


def in_proj_backward(dp, w_all, x, dx1, gain, sc, sh):
    t = x.shape[0]
    tm = min(512, t)

    def body(dp_ref, w_ref, x_ref, dx1_ref, g_ref, sc_ref, sh_ref, gx_ref, acc_ref):
        @pl.when(pl.program_id(0) == 0)
        def _():
            acc_ref[...] = jnp.zeros_like(acc_ref)

        dh = jnp.zeros((tm, D), F32)
        for j in range(NDEV):
            dh = dh + _dot(dp_ref[:, j * D:(j + 1) * D], w_ref[DP_SPLIT[j]], _NT)
        _, vjp_h = jax.vjp(f_modulate, x_ref[...], g_ref[...], sc_ref[...], sh_ref[...])
        dx, dg, dsc, dsh = vjp_h(dh)
        gx_ref[...] = dx1_ref[...] + dx
        acc_ref[0:1, :] += dg
        acc_ref[1:2, :] += dsc
        acc_ref[2:3, :] += dsh

    row = pl.BlockSpec((tm, D), lambda m: (m, 0))
    vec = pl.BlockSpec((1, D), lambda m: (0, 0))
    return pl.pallas_call(
        body, name="in_proj_backward", grid=(t // tm,),
        out_shape=(jax.ShapeDtypeStruct((t, D), F32), jax.ShapeDtypeStruct((8, D), F32)),
        in_specs=[pl.BlockSpec((tm, NDEV * D), lambda m: (m, 0)), VSPEC, row, row, vec, vec, vec],
        out_specs=(row, pl.BlockSpec((8, D), lambda m: (0, 0))),
        compiler_params=_params("arbitrary"),
    )(dp, w_all, x, dx1, gain, sc, sh)


def weight_grad(a, b, nblk, ka, bn, a_blocked, name, b_col=lambda n: n, after=None):
    t = a.shape[0]
    tk = min(2048, t)
    nk = t // tk
    extra = [] if after is None else [after]

    def body(*refs):
        a_ref, b_ref = refs[:2]
        f_ref, h_ref, acc = refs[2 + len(extra):]
        k = pl.program_id(1)

        @pl.when(k == 0)
        def _():
            acc[...] = jnp.zeros_like(acc)

        acc[...] += _dot(a_ref[...], b_ref[...], _TN)

        @pl.when(k == nk - 1)
        def _():
            f_ref[...] = acc[...]
            h_ref[...] = acc[...].astype(BF16)

    a_idx = (lambda n, k: (k, n)) if a_blocked else (lambda n, k: (k, 0))
    b_idx = (lambda n, k: (k, 0)) if a_blocked else (lambda n, k: (k, b_col(n)))
    out = pl.BlockSpec((None, ka, bn), lambda n, k: (n, 0, 0))
    return pl.pallas_call(
        body, name=name, grid=(nblk, nk),
        out_shape=(jax.ShapeDtypeStruct((nblk, ka, bn), F32), jax.ShapeDtypeStruct((nblk, ka, bn), BF16)),
        in_specs=[pl.BlockSpec((tk, ka), a_idx), pl.BlockSpec((tk, bn), b_idx)] + [ANYSPEC] * len(extra),
        out_specs=(out, out),
        scratch_shapes=[pltpu.VMEM((ka, bn), F32)],
        compiler_params=_params("parallel", "arbitrary"),
    )(a, b, *extra)


def ada_backward(call_t, dmod_cols, w, m, v):
    def body(c_ref, d_ref, w_ref, m_ref, v_ref, g_ref, dl_ref, nm_ref, nv_ref):
        ct, dm = c_ref[...], d_ref[...]
        g = ct[:, 0:1] * dm[0:1, :]
        for r in range(1, NDEV):
            g = g + ct[:, r:r + 1] * dm[r:r + 1, :]
        g_ref[...] = g
        dl_ref[...], nm_ref[...], nv_ref[...] = _adamw(w_ref[...], g, m_ref[...], v_ref[...])

    br = 256
    blk = pl.BlockSpec((br, w.shape[1]), lambda i: (i, 0))
    s = jax.ShapeDtypeStruct(w.shape, F32)
    return pl.pallas_call(
        body, name="ada_backward", grid=(w.shape[0] // br,), out_shape=(s, s, s, s),
        in_specs=[pl.BlockSpec((br, NDEV), lambda i: (i, 0)), pl.BlockSpec(dmod_cols.shape, lambda i: (0, 0)), blk,
                  blk, blk],
        out_specs=(blk, blk, blk, blk), compiler_params=_params("parallel"),
    )(call_t, dmod_cols, w, m, v)


def adamw_small(total, ddw_mine, recipes, ws, ms, vs):
    n = len(ws)

    def body(*refs):
        tot, ddw = refs[0], refs[1]
        w_refs, m_refs, v_refs = refs[2:2 + n], refs[2 + n:2 + 2 * n], refs[2 + 2 * n:2 + 3 * n]
        outs = refs[2 + 3 * n:2 + 7 * n]
        loss_ref = refs[2 + 7 * n]
        for i, rec in enumerate(recipes):
            if rec == "dw":
                g = ddw[...]
            elif isinstance(rec, tuple):
                g = tot[rec[0]:rec[1], :]
            else:
                g = jnp.concatenate([tot[r:r + 1, :] for r in rec], axis=1) if len(rec) > 1 else tot[rec[0]:rec[0] + 1, :]
            dl, nm, nv = _adamw(w_refs[i][...], g, m_refs[i][...], v_refs[i][...])
            outs[4 * i][...] = g
            outs[4 * i + 1][...] = dl
            outs[4 * i + 2][...] = nm
            outs[4 * i + 3][...] = nv
        loss_ref[...] = tot[LOSS_ROW:LOSS_ROW + 1, 0:128]

    shapes = []
    for w in ws:
        shapes += [jax.ShapeDtypeStruct(w.shape, F32)] * 4
    res = pl.pallas_call(
        body, name="adamw_small", out_shape=tuple(shapes) + (jax.ShapeDtypeStruct((1, 128), F32),),
        in_specs=[VSPEC] * (2 + 3 * n), out_specs=(VSPEC,) * (4 * n + 1), compiler_params=_params(),
    )(total, ddw_mine, *ws, *ms, *vs)
    return [res[4 * i:4 * i + 4] for i in range(n)], res[4 * n]


def reduce_and_adamw(sel, w, g_alls, g_recvs, m, v, name):
    r, c = w.shape
    br = min(256, r)
    npair = len(g_alls)

    def body(sel_ref, w_ref, *refs):
        go_refs, gr_refs = refs[:npair], refs[npair:2 * npair]
        m_ref, v_ref, g_ref, dl_ref, nm_ref, nv_ref = refs[2 * npair:]
        g = None
        for i in range(npair):
            gi = go_refs[i][...]
            for k in range(NDEV - 1):
                gi = gi + gr_refs[i][k].astype(F32)
            g = gi if g is None else jnp.where(sel_ref[1] == i, gi, g)
        g_ref[...] = g
        dl_ref[...], nm_ref[...], nv_ref[...] = _adamw(w_ref[...], g, m_ref[...], v_ref[...])

    blk = pl.BlockSpec((br, c), lambda i, sel_ref: (i, 0))
    own = pl.BlockSpec((None, br, c), lambda i, sel_ref: (sel_ref[0], i, 0))
    recv = pl.BlockSpec((NDEV - 1, br, c), lambda i, sel_ref: (0, i, 0))
    s = jax.ShapeDtypeStruct(w.shape, F32)
    return pl.pallas_call(
        body, name=name, out_shape=(s, s, s, s),
        grid_spec=pltpu.PrefetchScalarGridSpec(
            num_scalar_prefetch=1, grid=(r // br,),
            in_specs=[blk] + [own] * npair + [recv] * npair + [blk, blk],
            out_specs=(blk, blk, blk, blk)),
        compiler_params=_params("parallel"),
    )(sel, w, *g_alls, *g_recvs, m, v)


def kernel(x, c, w_ada, b_ada, pre_norm_tm, post_norm_tm, pre_norm_cm, post_norm_cm, w_in, b_in, hg_lb_logits, hg_norm, conv_dw, conv_db, conv_ln_g, conv_ln_b, w_br_a, w_br_b, w_out, w_ff1, w_ff2, loss_target, m_w_ada, m_b_ada, m_pre_norm_tm, m_post_norm_tm, m_pre_norm_cm, m_post_norm_cm, m_w_in, m_b_in, m_hg_lb_logits, m_hg_norm, m_conv_dw, m_conv_db, m_conv_ln_g, m_conv_ln_b, m_w_br_a, m_w_br_b, m_w_out, m_w_ff1, m_w_ff2, v_w_ada, v_b_ada, v_pre_norm_tm, v_post_norm_tm, v_pre_norm_cm, v_post_norm_cm, v_w_in, v_b_in, v_hg_lb_logits, v_hg_norm, v_conv_dw, v_conv_db, v_conv_ln_g, v_conv_ln_b, v_w_br_a, v_w_br_b, v_w_out, v_w_ff1, v_w_ff2):
    t = x.shape[1]
    me = 4 * lax.axis_index("x") + 2 * lax.axis_index("y") + lax.axis_index("c")
    xs = x[0]
    tgt = loss_target[0]

    mod, call, dw_all = ada_forward(c, w_ada[0], b_ada, conv_dw[0])
    sh1, sc1, g1, sh2, sc2, g2 = [mod[:, i * D:(i + 1) * D] for i in range(6)]
    dw = jnp.pad(dw_all.transpose(1, 0, 2).reshape(KW, D), ((0, HALO - KW), (0, 0)))

    p, h, win_all = in_proj(xs, pre_norm_tm, sc1, sh1, w_in[0].astype(BF16), b_in)
    yc, act, (w1_all,) = conv_forward(p, dw, conv_db, conv_ln_g, conv_ln_b, [w_ff1[0].astype(BF16)])
    intra, qe, st_prev, dec, (w2_all, wa_all, wb_all, wo_all) = hgrn_local(
        p, hg_lb_logits, [w_ff2[0].astype(BF16), w_br_a[0].astype(BF16), w_br_b[0].astype(BF16),
                          w_out[0].astype(BF16)])
    wa, wb, wo = wa_all.reshape(D, D), wb_all.reshape(D, D), wo_all.reshape(D, D)
    x1 = mix_forward(xs, p, qe, intra, st_prev, act, wa, wb, wo, hg_norm, g1, post_norm_tm)

    dx1, h2, r, dz, dy2, acc_ffn = ffn_forward_backward(x1, tgt, w1_all, w2_all, pre_norm_cm, sc2, sh2, g2,
                                                        post_norm_cm)
    rows = D // NDEV
    g1_f, g1_h = weight_grad(h2, dz, NDEV, D, DFF // NDEV, False, "grad_w_ff1")
    g2_f, g2_h = weight_grad(r, dy2, 4, D, D, True, "grad_w_ff2")
    g2_f, g2_h = g2_f.reshape(NDEV, DFF // NDEV, D), g2_h.reshape(NDEV, DFF // NDEV, D)

    (dp, do, dqe, gt, dact, a16, mg16, dya, dyb, dy, acc_mix) = mix_backward(
        xs, p, qe, intra, st_prev, act, wa, wb, wo, hg_norm, g1, post_norm_tm, dx1, dec)
    ga_f, ga_h = weight_grad(a16, dya, 1, D, D, False, "grad_w_br_a")
    gb_f, gb_h = weight_grad(act, dyb, 1, D, D, False, "grad_w_br_b")
    go_f, go_h = weight_grad(mg16, dy, 1, D, D, False, "grad_w_out")
    dp, acc_conv, ddw, (r_ff1, r_ff2) = conv_backward(p, yc, dact, dw, conv_ln_g, conv_ln_b, [g1_h, g2_h], dp)
    dp, acc_hg, (r_a, r_b, r_o) = hgrn_backward(
        p, hg_lb_logits, do, dqe, gt, st_prev,
        [ga_h.reshape(NDEV, rows, D), gb_h.reshape(NDEV, rows, D), go_h.reshape(NDEV, rows, D)], dp)
    dp_col = lambda n: jnp.where(n < 3, n + 3, jnp.where(n == 3, 0, jnp.where(n < 6, n + 2, n - 5)))
    even, odd = [0, 2, 4, 6], [1, 3, 5, 7]
    gin_f0, gin_h0 = weight_grad(h, dp, 4, D, D, False, "grad_w_in_even", b_col=lambda j: dp_col(2 * j))
    send0, recv0, thru0, landing0, token0 = exchange_start(gin_h0, even, "exchange_start_even")
    gin_f1, gin_h1 = weight_grad(h, dp, 4, D, D, False, "grad_w_in_odd", b_col=lambda j: dp_col(2 * j + 1),
                                 after=token0)
    send1, recv1, thru1, landing1, token1 = exchange_start(gin_h1, odd, "exchange_start_odd")
    grad_x, acc_in = in_proj_backward(dp, win_all, xs, dx1, pre_norm_tm, sc1, sh1 + token1[0:1, 0:1])

    own = [ga_f.reshape(NDEV, rows, D), gb_f.reshape(NDEV, rows, D), go_f.reshape(NDEV, rows, D), g1_f, g2_f]
    recv = [r_a, r_b, r_o, r_ff1, r_ff2]
    big = {}
    names = ["w_br_a", "w_br_b", "w_out", "w_ff1", "w_ff2"]
    ws = [w_br_a, w_br_b, w_out, w_ff1, w_ff2]
    ms = [m_w_br_a, m_w_br_b, m_w_out, m_w_ff1, m_w_ff2]
    vs = [v_w_br_a, v_w_br_b, v_w_out, v_w_ff1, v_w_ff2]
    sel = jnp.stack([me, 0]).astype(jnp.int32)
    for i, nm in enumerate(names):
        big[nm] = [o[None] for o in reduce_and_adamw(sel, ws[i][0], [own[i]], [recv[i]], ms[i][0], vs[i][0],
                                                     "adamw_" + nm)]
    big, grad_x, acc_in = lax.optimization_barrier((big, grad_x, acc_in))
    r_in0 = exchange_wait(send0, recv0, thru0, landing0, acc_in, even, "exchange_wait_even")
    r_in1 = exchange_wait(send1, recv1, thru1, landing1, acc_in, odd, "exchange_wait_odd")
    r_in0, r_in1, acc_in = lax.optimization_barrier((r_in0, r_in1, acc_in))
    sel_in = jnp.stack([me // 2, me % 2]).astype(jnp.int32)
    big["w_in"] = [o[None] for o in reduce_and_adamw(sel_in, w_in[0], [gin_f0, gin_f1], [r_in0, r_in1], m_w_in[0],
                                                     v_w_in[0], "adamw_w_in")]

    dmod_rows = [2, 1, 8, 19, 18, 20]
    total, kept = gather_and_sum_rows([acc_in, acc_mix, acc_ffn, acc_hg, acc_conv, ddw], dmod_rows)
    dmod_all = kept[:, 0:6, :].reshape(NDEV, 6 * D)
    wcols = w_ada.shape[2]
    gwa, dwa, nmwa, nvwa = ada_backward(call.T, lax.dynamic_slice_in_dim(dmod_all, me * wcols, wcols, axis=1),
                                        w_ada[0], m_w_ada[0], v_w_ada[0])
    ddw_mine = lax.dynamic_slice_in_dim(total[40:40 + KW], me * (D // NDEV), D // NDEV, axis=1)

    small_names = ["b_ada", "pre_norm_tm", "post_norm_tm", "pre_norm_cm", "post_norm_cm", "b_in", "hg_lb_logits",
                   "hg_norm", "conv_db", "conv_ln_g", "conv_ln_b", "conv_dw"]
    recipes = [dmod_rows, [0], [9], [16], [17], [26, 27, 28, 11, 35, 36, 12, 13], (24, 26), [10], [32], [33], [34],
               "dw"]
    small_w = [b_ada, pre_norm_tm, post_norm_tm, pre_norm_cm, post_norm_cm, b_in, hg_lb_logits, hg_norm, conv_db,
               conv_ln_g, conv_ln_b, conv_dw[0]]
    small_m = [m_b_ada, m_pre_norm_tm, m_post_norm_tm, m_pre_norm_cm, m_post_norm_cm, m_b_in, m_hg_lb_logits,
               m_hg_norm, m_conv_db, m_conv_ln_g, m_conv_ln_b, m_conv_dw[0]]
    small_v = [v_b_ada, v_pre_norm_tm, v_post_norm_tm, v_pre_norm_cm, v_post_norm_cm, v_b_in, v_hg_lb_logits,
               v_hg_norm, v_conv_db, v_conv_ln_g, v_conv_ln_b, v_conv_dw[0]]
    small_out, loss_row = adamw_small(total, ddw_mine, recipes, small_w, small_m, small_v)
    loss = loss_row[0, 0]
    sm = {nm: list(o) for nm, o in zip(small_names, small_out)}
    sm["conv_dw"] = [o[None] for o in sm["conv_dw"]]

    order = ["w_ada", "b_ada", "pre_norm_tm", "post_norm_tm", "pre_norm_cm", "post_norm_cm", "w_in", "b_in",
             "hg_lb_logits", "hg_norm", "conv_dw", "conv_db", "conv_ln_g", "conv_ln_b", "w_br_a", "w_br_b", "w_out",
             "w_ff1", "w_ff2"]
    res = dict(sm)
    res.update(big)
    res["w_ada"] = [gwa[None], dwa[None], nmwa[None], nvwa[None]]
    outs = [loss, grad_x[None]]
    for j in range(4):
        outs += [res[nm][j] for nm in order]
    return tuple(outs)
```

```python
import jax
import jax.numpy as jnp
from jax import lax
from jax.experimental import pallas as pl
from jax.experimental.pallas import tpu as pltpu

F32 = jnp.float32
BF16 = jnp.bfloat16
MESH = pl.DeviceIdType.MESH

D = 1024
HEADS = 8
DK = 128
LANE = 128
CH = 128
LEVELS = 7
KW = 31
HALO = 32
DFF = 4096
NDEV = 8
EPS = 1e-6
DP_SPLIT = (3, 6, 7, 0, 1, 2, 4, 5)
LOSS_ROW = 21
ADAM_LR, ADAM_B1, ADAM_B2, ADAM_EPS, ADAM_WD, ADAM_STEP = 0.001, 0.9, 0.999, 1e-08, 0.01, 10
VMEM_LIMIT = 58 * 1024 * 1024

_NN = (((1,), (0,)), ((), ()))
_NT = (((1,), (1,)), ((), ()))
_TN = (((0,), (0,)), ((), ()))

VSPEC = pl.BlockSpec(memory_space=pltpu.VMEM)
ANYSPEC = pl.BlockSpec(memory_space=pl.ANY)


def _params(*sem):
    return pltpu.CompilerParams(dimension_semantics=sem or None, vmem_limit_bytes=VMEM_LIMIT)


def _dot(a, b, dims):
    return lax.dot_general(a, b, dims, preferred_element_type=F32)


@jax.custom_vjp
def mm_nn(a, b):
    return _dot(a.astype(BF16), b.astype(BF16), _NN)


def _mm_nn_fwd(a, b):
    ab, bb = a.astype(BF16), b.astype(BF16)
    return _dot(ab, bb, _NN), (ab, bb)


def _mm_nn_bwd(res, ct):
    ab, bb = res
    cb = ct.astype(BF16)
    return _dot(cb, bb, _NT), _dot(ab, cb, _TN)


mm_nn.defvjp(_mm_nn_fwd, _mm_nn_bwd)


@jax.custom_vjp
def mm_nt(a, b):
    return _dot(a.astype(BF16), b.astype(BF16), _NT)


def _mm_nt_fwd(a, b):
    ab, bb = a.astype(BF16), b.astype(BF16)
    return _dot(ab, bb, _NT), (ab, bb)


def _mm_nt_bwd(res, ct):
    ab, bb = res
    cb = ct.astype(BF16)
    return _dot(cb, bb, _NN), _dot(cb, ab, _TN)


mm_nt.defvjp(_mm_nt_fwd, _mm_nt_bwd)


@jax.custom_vjp
def mm_tn(a, b):
    return _dot(a.astype(BF16), b.astype(BF16), _TN)


def _mm_tn_fwd(a, b):
    ab, bb = a.astype(BF16), b.astype(BF16)
    return _dot(ab, bb, _TN), (ab, bb)


def _mm_tn_bwd(res, ct):
    ab, bb = res
    cb = ct.astype(BF16)
    return _dot(bb, cb, _NT), _dot(ab, cb, _NN)


mm_tn.defvjp(_mm_tn_fwd, _mm_tn_bwd)


def _rms(x):
    return x * lax.rsqrt(jnp.mean(x * x, axis=-1, keepdims=True) + EPS)


_sigmoid = jax.nn.sigmoid


def _silu(x):
    return x * _sigmoid(x)


def f_modulate(x, gain, sc, sh):
    return _rms(x) * gain * (1.0 + sc) + sh


def f_residual(x, y, gate, gain):
    return x + gate * (_rms(y) * gain)


def f_merge(ga, gb, ya, yb):
    return _sigmoid(ga) * ya + _sigmoid(gb) * yb


def f_head_out(o, og, hg):
    heads = [_rms(o[:, h * DK:(h + 1) * DK]) for h in range(HEADS)]
    return jnp.concatenate(heads, axis=1) * hg * _silu(og)


def f_conv_act(u, g, b):
    mu = jnp.mean(u, axis=-1, keepdims=True)
    var = jnp.mean(jnp.square(u - mu), axis=-1, keepdims=True)
    return _silu((u - mu) * lax.rsqrt(var + EPS) * g + b)


def f_glu(cv, cg):
    return cv * _sigmoid(cg)


def _split2(x):
    hi = x.astype(BF16)
    return hi, (x - hi.astype(F32)).astype(BF16)


def _tri(transposed):
    i = lax.broadcasted_iota(jnp.int32, (CH, CH), 1 if transposed else 0)
    t = lax.broadcasted_iota(jnp.int32, (CH, CH), 0 if transposed else 1)
    return jnp.where(t <= i, 1.0, 0.0).astype(BF16)


def _blocks3(x, rows):
    return x.reshape(CH // rows, rows, x.shape[-1])


def _mid_broadcast(b, lev):
    h = 1 << (LEVELS - 1 - lev)
    if h >= 4:
        x3 = _blocks3(b, 2 * h)
        return jnp.broadcast_to(x3[:, h - 1:h, :], x3.shape).reshape(b.shape)
    x3 = _blocks3(b, 8)
    sub = lax.broadcasted_iota(jnp.int32, x3.shape, 1)
    out = None
    for first in range(0, 8, 2 * h):
        piece = jnp.broadcast_to(x3[:, first + h - 1:first + h, :], x3.shape)
        out = piece if out is None else jnp.where(sub >= first, piece, out)
    return out.reshape(b.shape)


def _mid_scatter(d, lev):
    h = 1 << (LEVELS - 1 - lev)
    if h >= 4:
        x3 = _blocks3(d, 2 * h)
        row = lax.broadcasted_iota(jnp.int32, x3.shape, 1)
        total = jnp.sum(x3, axis=1, keepdims=True)
        return jnp.where(row == h - 1, total, 0.0).reshape(d.shape)
    x3 = _blocks3(d, 8)
    sub = lax.broadcasted_iota(jnp.int32, x3.shape, 1)
    out = jnp.zeros_like(x3)
    for first in range(0, 8, 2 * h):
        inside = (sub >= first) & (sub < first + 2 * h)
        total = jnp.sum(jnp.where(inside, x3, 0.0), axis=1, keepdims=True)
        out = jnp.where(sub == first + h - 1, total, out)
    return out.reshape(d.shape)


@jax.custom_vjp
def decay_sums(g):
    hi, lo = _split2(g)
    tri = _tri(False)
    b = _dot(tri, hi, _NN) + _dot(tri, lo, _NN)
    return (b,) + tuple(b - _mid_broadcast(b, lev) for lev in range(LEVELS))


def _decay_sums_fwd(g):
    return decay_sums(g), None


def _decay_sums_bwd(_, cts):
    db = cts[0]
    for lev in range(LEVELS):
        db = db + cts[1 + lev] - _mid_scatter(cts[1 + lev], lev)
    hi, lo = _split2(db)
    tri = _tri(True)
    return (_dot(tri, hi, _NN) + _dot(tri, lo, _NN),)


decay_sums.defvjp(_decay_sums_fwd, _decay_sums_bwd)


def _score_masks():
    i = lax.broadcasted_iota(jnp.int32, (CH, CH), 0)
    j = lax.broadcasted_iota(jnp.int32, (CH, CH), 1)
    masks = [i == j]
    for lev in range(LEVELS):
        sh = LEVELS - 1 - lev
        same = (i >> (sh + 1)) == (j >> (sh + 1))
        masks.append(same & (((i >> sh) & 1) == 1) & (((j >> sh) & 1) == 0))
    return [jnp.where(m, 1.0, 0.0) for m in masks]


def f_hgrn_chunk(q_r, f_r, v, logits):
    l0, l1 = logits[0:1, :], logits[1:2, :]
    mx = lax.stop_gradient(jnp.maximum(l0, l1))
    e0, e1 = jnp.exp(l0 - mx), jnp.exp(l1 - mx)
    lb = e0 / (e0 + e1)
    q = _silu(q_r)
    f = lb + (1.0 - lb) * _sigmoid(f_r)
    k = 1.0 - f
    sums = decay_sums(jnp.log(f))
    b = sums[0]
    btot = b[CH - 1:CH, :]
    qe = q * jnp.exp(b)
    ke = k * jnp.exp(btot - b)
    dec = jnp.exp(btot)
    qs, ks = [q], [k]
    row = lax.broadcasted_iota(jnp.int32, b.shape, 0)
    for lev in range(LEVELS):
        upper = ((row >> (LEVELS - 1 - lev)) & 1) == 1
        e = sums[1 + lev]
        qs.append(q * jnp.exp(jnp.where(upper, e, 0.0)))
        ks.append(k * jnp.exp(jnp.where(upper, 0.0, -e)))
    masks = _score_masks()
    intra, ut = [], []
    for h in range(HEADS):
        sl = slice(h * DK, (h + 1) * DK)
        sc = None
        for lev in range(LEVELS + 1):
            s = mm_nt(qs[lev][:, sl], ks[lev][:, sl]) * masks[lev]
            sc = s if sc is None else sc + s
        intra.append(mm_nn(sc, v[:, sl]))
        ut.append(mm_tn(v[:, sl], ke[:, sl]))
    return jnp.concatenate(intra, axis=1), qe, jnp.concatenate(ut, axis=1), dec


def _inter(qe_b, st_ref, rows):
    out = []
    for ci in range(rows // CH):
        st = st_ref[ci].astype(BF16)
        heads = [_dot(qe_b[ci * CH:(ci + 1) * CH, h * DK:(h + 1) * DK], st[:, h * DK:(h + 1) * DK], _NT)
                 for h in range(HEADS)]
        out.append(jnp.concatenate(heads, axis=1))
    return jnp.concatenate(out, axis=0)


def _shift_stack(src, dst, rows):
    dst[0, 0:rows, :] = src[0:rows, :]
    for b in range(1, 8):
        dst[b, 0:rows - 8, :] = src[pl.ds(b, rows - 8), :]


def _shifted(stack, offset, rows, lanes=slice(None)):
    return stack[offset % 8, pl.ds(8 * (offset // 8), rows), lanes]


def _adamw(w, g, m, v):
    m = ADAM_B1 * m + (1.0 - ADAM_B1) * g
    v = ADAM_B2 * v + (1.0 - ADAM_B2) * jnp.square(g)
    m_hat = m / (1.0 - ADAM_B1 ** ADAM_STEP)
    v_hat = v / (1.0 - ADAM_B2 ** ADAM_STEP)
    delta = -ADAM_LR * (m_hat / (jnp.sqrt(v_hat) + ADAM_EPS) + ADAM_WD * w)
    return delta, m, v


def _me():
    return lax.axis_index("x"), lax.axis_index("y"), lax.axis_index("c")


def _peer(k):
    x, y, c = _me()
    mask = k + 1
    px = (1 - x) if (mask >> 2) & 1 else x
    py = (1 - y) if (mask >> 1) & 1 else y
    pc = (1 - c) if mask & 1 else c
    return (px, py, pc), 4 * px + 2 * py + pc


def ada_forward(c, w_ada, b_ada, dw):
    wcols = w_ada.shape[1]

    def body(c_ref, w_ref, b_ref, dw_ref, mod_ref, call_ref, dwall_ref, part_ref, modp_ref, send_sems, recv_sems):
        x, y, cc = _me()
        me = 4 * x + 2 * y + cc
        call_ref[me] = _silu(c_ref[...])
        dwall_ref[me] = dw_ref[...]
        sends = []
        for k in range(NDEV - 1):
            dev, pidx = _peer(k)
            cp = pltpu.make_async_remote_copy(dwall_ref.at[me], dwall_ref.at[me], send_sems.at[2 * (NDEV - 1) + k],
                                              recv_sems.at[2 * (NDEV - 1) + k], device_id=dev, device_id_type=MESH)
            cp.start()
            sends.append(cp)
        for k in range(NDEV - 1):
            dev, _ = _peer(k)
            cp = pltpu.make_async_remote_copy(call_ref.at[me], call_ref.at[me], send_sems.at[k], recv_sems.at[k],
                                              device_id=dev, device_id_type=MESH)
            cp.start()
            sends.append(cp)
        for k in range(NDEV - 1):
            _, pidx = _peer(k)
            pltpu.make_async_remote_copy(call_ref.at[pidx], call_ref.at[pidx], send_sems.at[k], recv_sems.at[k],
                                         device_id=_peer(k)[0], device_id_type=MESH).wait_recv()
        call = jnp.concatenate([call_ref[r] for r in range(NDEV)], axis=0)
        part = _dot(call.astype(BF16), w_ref[...].astype(BF16), _NN)
        for r in range(NDEV):
            part_ref[r] = part[r:r + 1, :]
        modp_ref[me] = part_ref[me]
        for k in range(NDEV - 1):
            dev, pidx = _peer(k)
            cp = pltpu.make_async_remote_copy(part_ref.at[pidx], modp_ref.at[me], send_sems.at[NDEV - 1 + k],
                                              recv_sems.at[NDEV - 1 + k], device_id=dev, device_id_type=MESH)
            cp.start()
            sends.append(cp)
        for k in range(NDEV - 1):
            dev, pidx = _peer(k)
            pltpu.make_async_remote_copy(part_ref.at[pidx], modp_ref.at[pidx], send_sems.at[NDEV - 1 + k],
                                         recv_sems.at[NDEV - 1 + k], device_id=dev, device_id_type=MESH).wait_recv()
        for k in range(NDEV - 1):
            dev, pidx = _peer(k)
            pltpu.make_async_remote_copy(dwall_ref.at[pidx], dwall_ref.at[pidx], send_sems.at[2 * (NDEV - 1) + k],
                                         recv_sems.at[2 * (NDEV - 1) + k], device_id=dev,
                                         device_id_type=MESH).wait_recv()
        for cp in sends:
            cp.wait_send()
        mod_ref[...] = jnp.concatenate([modp_ref[r] for r in range(NDEV)], axis=1) + b_ref[...]

    mod, call, dw_all = pl.pallas_call(
        body, name="ada_forward",
        out_shape=(jax.ShapeDtypeStruct((1, NDEV * wcols), F32), jax.ShapeDtypeStruct((NDEV, 1, D), F32),
                   jax.ShapeDtypeStruct((NDEV,) + dw.shape, F32)),
        in_specs=[VSPEC] * 4, out_specs=(VSPEC,) * 3,
        scratch_shapes=[pltpu.VMEM((NDEV, 1, wcols), F32), pltpu.VMEM((NDEV, 1, wcols), F32),
                        pltpu.SemaphoreType.DMA((3 * (NDEV - 1),)), pltpu.SemaphoreType.DMA((3 * (NDEV - 1),))],
        compiler_params=_params(),
    )(c, w_ada, b_ada, dw)
    return mod, call.reshape(NDEV, D), dw_all


def _comm_sems(n, local):
    sems = [pltpu.SemaphoreType.DMA((7 * n,)), pltpu.SemaphoreType.DMA((7 * n,))]
    return sems + ([pltpu.SemaphoreType.DMA((n,))] if local else [])


def _gather2_ops(ins, outs, send_sems, recv_sems, local_sems):
    n = len(ins)
    x, y, c = _me()
    me, sibling = (x, y, c), (x, y, 1 - c)
    chips = [(1 - x, y), (x, 1 - y), (1 - x, 1 - y)]

    def slot(p):
        return 4 * p[0] + 2 * p[1] + p[2]

    def copy(a, k, block, to, src=None):
        return pltpu.make_async_remote_copy(
            src_ref=outs[a].at[slot(block)] if src is None else src, dst_ref=outs[a].at[slot(block)],
            send_sem=send_sems.at[a * 7 + k], recv_sem=recv_sems.at[a * 7 + k], device_id=to, device_id_type=MESH)

    def local(a):
        return pltpu.make_async_copy(ins[a], outs[a].at[slot(me)], local_sems.at[a])

    def first(a):
        return [copy(a, 0, me, sibling, src=ins[a])] + [copy(a, 1 + j, me, (*chip, c), src=ins[a])
                                                        for j, chip in enumerate(chips)]

    def passed(a):
        return [copy(a, 4 + j, (*chip, c), sibling) for j, chip in enumerate(chips)]

    def start():
        for a in range(n):
            local(a).start()
            for cp in first(a):
                cp.start()

    def forward():
        for j, chip in enumerate(chips):
            for a in range(n):
                copy(a, 1 + j, (*chip, c), me).wait_recv()
                passed(a)[j].start()

    def finish():
        for a in range(n):
            copy(a, 0, sibling, me).wait_recv()
            for j, chip in enumerate(chips):
                copy(a, 4 + j, (*chip, 1 - c), me).wait_recv()
        for a in range(n):
            for cp in first(a) + passed(a):
                cp.wait_send()
            local(a).wait()

    return start, forward, finish


def _exchange_ops(ins, outs, send_sems, recv_sems):
    n = len(ins)

    def copy(a, k):
        dev, pidx = _peer(k)
        return pltpu.make_async_remote_copy(ins[a].at[pidx], outs[a].at[k], send_sems.at[a * 7 + k],
                                            recv_sems.at[a * 7 + k], device_id=dev, device_id_type=MESH)

    def start():
        for k in range(NDEV - 1):
            for a in range(n):
                copy(a, k).start()

    def finish():
        for k in range(NDEV - 1):
            for a in range(n):
                copy(a, k).wait_recv()
        for k in range(NDEV - 1):
            for a in range(n):
                copy(a, k).wait_send()

    return start, finish


def _gathered(shards):
    return tuple(jax.ShapeDtypeStruct((NDEV,) + s.shape, s.dtype) for s in shards)


def _exchanged(blocks):
    return tuple(jax.ShapeDtypeStruct((NDEV - 1,) + b.shape[1:], b.dtype) for b in blocks)


def _owner_copy(src_ref, land_ref, send_sems, recv_sems, gi, n):
    x, y, c = _me()
    k = jnp.bitwise_xor(4 * x + 2 * y + c, n) - 1
    return pltpu.make_async_remote_copy(src_ref.at[gi], land_ref.at[k], send_sems.at[gi], recv_sems.at[k],
                                        device_id=(n >> 2, (n >> 1) & 1, n & 1), device_id_type=MESH)


def exchange_start(blocks, owners, name):
    landing = lax.empty((NDEV - 1,) + blocks.shape[1:], blocks.dtype)
    hbm = pl.BlockSpec(memory_space=pltpu.HBM)
    sem = pl.BlockSpec(memory_space=pltpu.SEMAPHORE)

    def body(src_ref, land_ref, send_sems, recv_sems, src_thru, land_thru, token):
        x, y, c = _me()
        for gi, n in enumerate(owners):
            @pl.when(4 * x + 2 * y + c != n)
            def _(gi=gi, n=n):
                _owner_copy(src_ref, land_ref, send_sems, recv_sems, gi, n).start()
        token[...] = jnp.zeros_like(token)

    return pl.pallas_call(
        body, name=name,
        out_shape=(pltpu.SemaphoreType.DMA((len(owners),)), pltpu.SemaphoreType.DMA((NDEV - 1,)),
                   pltpu.HBM(blocks.shape, blocks.dtype), pltpu.HBM(landing.shape, landing.dtype),
                   jax.ShapeDtypeStruct((8, 128), F32)),
        in_specs=(hbm, hbm), out_specs=(sem, sem, hbm, hbm, VSPEC), input_output_aliases={0: 2, 1: 3},
        compiler_params=pltpu.CompilerParams(has_side_effects=pltpu.SideEffectType.DATAFLOW_SIDE_EFFECTING),
    )(pltpu.with_memory_space_constraint(blocks, pltpu.HBM), pltpu.with_memory_space_constraint(landing, pltpu.HBM))


def exchange_wait(send_sems, recv_sems, src_thru, land_thru, after, owners, name):
    hbm = pl.BlockSpec(memory_space=pltpu.HBM)
    sem = pl.BlockSpec(memory_space=pltpu.SEMAPHORE)

    def body(src_ref, land_ref, send_sems, recv_sems, after_ref, src_dead, got_ref):
        x, y, c = _me()
        me = 4 * x + 2 * y + c
        for gi, n in enumerate(owners):
            @pl.when(me != n)
            def _(gi=gi, n=n):
                _owner_copy(src_ref, land_ref, send_sems, recv_sems, gi, n).wait_send()

            @pl.when(me == n)
            def _(gi=gi):
                for k in range(NDEV - 1):
                    pltpu.make_async_remote_copy(src_ref.at[gi], land_ref.at[k], send_sems.at[gi], recv_sems.at[k],
                                                 device_id=_peer(k)[0], device_id_type=MESH).wait_recv()

    return pl.pallas_call(
        body, name=name,
        out_shape=(pltpu.HBM(src_thru.shape, src_thru.dtype), pltpu.HBM(land_thru.shape, land_thru.dtype)),
        in_specs=(hbm, hbm, sem, sem, ANYSPEC), out_specs=(hbm, hbm), input_output_aliases={0: 0, 1: 1},
        compiler_params=pltpu.CompilerParams(has_side_effects=pltpu.SideEffectType.DATAFLOW_SIDE_EFFECTING),
    )(src_thru, land_thru, send_sems, recv_sems, after)[1]


def gather_and_sum_rows(parts, keep_rows):
    n = len(parts)
    offs = [sum(p.shape[0] for p in parts[:i]) for i in range(n)]
    rows = sum(p.shape[0] for p in parts)

    def body(*refs):
        sum_ref, keep_ref, buf_ref, send_sems, recv_sems = refs[n:]
        x, y, c = _me()
        me = 4 * x + 2 * y + c
        for i in range(n):
            buf_ref[me, offs[i]:offs[i] + parts[i].shape[0], :] = refs[i][...]

        def copy(k, block):
            return pltpu.make_async_remote_copy(buf_ref.at[block], buf_ref.at[block], send_sems.at[k],
                                                recv_sems.at[k], device_id=_peer(k)[0], device_id_type=MESH)

        for k in range(NDEV - 1):
            copy(k, me).start()
        for k in range(NDEV - 1):
            copy(k, _peer(k)[1]).wait_recv()
        for k in range(NDEV - 1):
            copy(k, me).wait_send()
        total = buf_ref[0]
        for r in range(1, NDEV):
            total = total + buf_ref[r]
        sum_ref[...] = total
        keep_ref[...] = jnp.zeros_like(keep_ref)
        for r in range(NDEV):
            for j, src in enumerate(keep_rows):
                keep_ref[r, j:j + 1, :] = buf_ref[r, src:src + 1, :]

    return pl.pallas_call(
        body, name="gather_and_sum_rows",
        out_shape=(jax.ShapeDtypeStruct((rows, D), F32), jax.ShapeDtypeStruct((NDEV, 8, D), F32)),
        in_specs=[VSPEC] * n, out_specs=(VSPEC, VSPEC),
        scratch_shapes=[pltpu.VMEM((NDEV, rows, D), F32), pltpu.SemaphoreType.DMA((NDEV - 1,)),
                        pltpu.SemaphoreType.DMA((NDEV - 1,))],
        compiler_params=_params(),
    )(*parts)


def _gather_chips():
    x, y, c = _me()
    north = c == 1
    first = (jnp.where(north, 1 - x, x), jnp.where(north, y, 1 - y))
    second = (jnp.where(north, x, 1 - x), jnp.where(north, 1 - y, y))
    return [first, second, (1 - x, 1 - y)]


def _arrival_order():
    x, y, c = _me()
    first, second, far = _gather_chips()
    devs = [(x, y, c), (x, y, 1 - c), (*first, c), (*second, 1 - c), (*second, c), (*first, 1 - c), (*far, c),
            (*far, 1 - c)]
    return jnp.stack([4 * d[0] + 2 * d[1] + d[2] for d in devs]).astype(jnp.int32)


def in_proj(x, gain, sc, sh, w_shard, b_in):
    t = x.shape[0]
    tm = min(1024, t // 2)
    nm = t // tm
    order = _arrival_order()

    def body(order_ref, x_ref, g_ref, sc_ref, sh_ref, b_ref, wsh_ref, p_ref, h_ref, wall_ref,
             wbuf, h_all, send_sems, recv_sems, local_sems):
        k, m = pl.program_id(0), pl.program_id(1)
        mx, my, mc = _me()
        me, sibling = (mx, my, mc), (mx, my, 1 - mc)
        chips = _gather_chips()

        def slot(d):
            return 4 * d[0] + 2 * d[1] + d[2]

        def copy(sem, block, to, src=None):
            return pltpu.make_async_remote_copy(
                src_ref=wall_ref.at[slot(block)] if src is None else src, dst_ref=wall_ref.at[slot(block)],
                send_sem=send_sems.at[sem], recv_sem=recv_sems.at[sem], device_id=to, device_id_type=MESH)

        own = pltpu.make_async_copy(wsh_ref, wall_ref.at[slot(me)], local_sems.at[0])
        first = [copy(0, me, sibling, src=wsh_ref)] + [copy(1 + j, me, (*q, mc), src=wsh_ref)
                                                       for j, q in enumerate(chips)]
        passed = [copy(4 + j, (*q, mc), sibling) for j, q in enumerate(chips)]

        def load(kk):
            src = wsh_ref if kk == 0 else wall_ref.at[order_ref[kk]]
            return pltpu.make_async_copy(src, wbuf.at[kk % 2], local_sems.at[1 + kk % 2])

        def arrived(kk):
            if kk == 1:
                copy(0, sibling, me).wait_recv()
            elif kk in (2, 4, 6):
                j = kk // 2 - 1
                copy(1 + j, (*chips[j], mc), me).wait_recv()
                passed[j].start()
                if j < 2:
                    first[2 + j].start()
            else:
                j = {3: 1, 5: 0, 7: 2}[kk]
                copy(4 + {1: 0, 0: 1, 2: 2}[j], (*chips[j], 1 - mc), me).wait_recv()

        @pl.when((k == 0) & (m == 0))
        def _():
            own.start()
            for cp in first[:2]:
                cp.start()
            load(0).start()
            load(0).wait()

        for kk in range(1, NDEV):
            @pl.when((k == kk - 1) & (m == nm // 2))
            def _(kk=kk):
                arrived(kk)
                load(kk).start()

            @pl.when((k == kk) & (m == 0))
            def _(kk=kk):
                load(kk).wait()

        rows = pl.ds(pl.multiple_of(m * tm, tm), tm)

        @pl.when(k == 0)
        def _():
            h_all[rows, :] = f_modulate(x_ref[...], g_ref[...], sc_ref[...], sh_ref[...]).astype(BF16)

        p_ref[...] = _dot(h_all[rows, :], wbuf[k % 2], _NN) + b_ref[...]

        @pl.when((k == NDEV - 1) & (m == nm - 1))
        def _():
            for cp in first + passed:
                cp.wait_send()
            own.wait()
            out = pltpu.make_async_copy(h_all, h_ref, local_sems.at[0])
            out.start()
            out.wait()

    vec = pl.BlockSpec((1, D), lambda k, m, o: (0, 0))
    return pl.pallas_call(
        body, name="in_proj",
        out_shape=(jax.ShapeDtypeStruct((t, NDEV * D), F32), jax.ShapeDtypeStruct((t, D), BF16),
                   jax.ShapeDtypeStruct((NDEV, D, D), BF16)),
        grid_spec=pltpu.PrefetchScalarGridSpec(
            num_scalar_prefetch=1, grid=(NDEV, nm),
            in_specs=[pl.BlockSpec((tm, D), lambda k, m, o: (jnp.where(k == 0, m, 0), 0)), vec, vec, vec,
                      pl.BlockSpec((1, D), lambda k, m, o: (0, o[k])), ANYSPEC],
            out_specs=(pl.BlockSpec((tm, D), lambda k, m, o: (m, o[k])), ANYSPEC, ANYSPEC),
            scratch_shapes=[pltpu.VMEM((2, D, D), BF16), pltpu.VMEM((t, D), BF16), pltpu.SemaphoreType.DMA((7,)),
                            pltpu.SemaphoreType.DMA((7,)), pltpu.SemaphoreType.DMA((3,))]),
        compiler_params=_params("arbitrary", "arbitrary"),
    )(order, x, gain, sc, sh, b_in, w_shard)


def hgrn_local(p, logits, shards):
    t = p.shape[0]
    nc = t // CH
    n = len(shards)

    def body(*refs):
        q_ref, f_ref, v_ref, l_ref = refs[:4]
        intra_ref, qe_ref, st_ref, dec_ref = refs[4 + n:8 + n]
        state = refs[8 + 2 * n]
        start, forward, finish = _gather2_ops(refs[4:4 + n], refs[8 + n:8 + 2 * n], *refs[9 + 2 * n:])
        step = pl.program_id(0)

        @pl.when(step == 0)
        def _():
            start()
            state[...] = jnp.zeros_like(state)

        intra, qe, ut, dec = f_hgrn_chunk(q_ref[...], f_ref[...], v_ref[...], l_ref[...])
        intra_ref[...] = intra
        qe_ref[...] = qe.astype(BF16)
        dec_ref[...] = dec
        s = state[...]
        st_ref[...] = s
        state[...] = dec * s + ut
        pl.when(step == (3 * nc) // 4)(forward)
        pl.when(step == nc - 1)(finish)

    col = lambda j: pl.BlockSpec((CH, D), lambda c: (c, j))
    res = pl.pallas_call(
        body, name="hgrn_local", grid=(nc,),
        out_shape=(jax.ShapeDtypeStruct((t, D), F32), jax.ShapeDtypeStruct((t, D), BF16),
                   jax.ShapeDtypeStruct((nc, DK, D), F32), jax.ShapeDtypeStruct((nc, 1, D), F32)) + _gathered(shards),
        in_specs=[col(0), col(1), col(2), pl.BlockSpec((2, D), lambda c: (0, 0))] + [ANYSPEC] * n,
        out_specs=(pl.BlockSpec((CH, D), lambda c: (c, 0)), pl.BlockSpec((CH, D), lambda c: (c, 0)),
                   pl.BlockSpec((None, DK, D), lambda c: (c, 0, 0)), pl.BlockSpec((None, 1, D), lambda c: (c, 0, 0)))
        + (ANYSPEC,) * n,
        scratch_shapes=[pltpu.VMEM((DK, D), F32)] + _comm_sems(n, True),
        compiler_params=_params("arbitrary"),
    )(p, p, p, logits, *shards)
    return res[:4] + (res[4:],)


def _conv_tile(t):
    return min(256, t)


def conv_forward(p, dw, db, ln_g, ln_b, shards):
    t = p.shape[0]
    tm = min(512, t)
    per = tm // HALO
    n = len(shards)
    nt = t // tm

    def body(*refs):
        cv_ref, cg_ref, cvp_ref, cgp_ref, dw_ref, db_ref, g_ref, b_ref = refs[:8]
        yc_ref, act_ref = refs[8 + n:10 + n]
        uext, ush = refs[10 + 2 * n:12 + 2 * n]
        start, forward, finish = _gather2_ops(refs[8:8 + n], refs[10 + n:10 + 2 * n], *refs[12 + 2 * n:])
        step = pl.program_id(0)
        pl.when(step == 0)(start)
        pl.when(step == nt - 1)(forward)
        uext[0:HALO, :] = jnp.where(step == 0, 0.0, f_glu(cvp_ref[...], cgp_ref[...]))
        uext[HALO:HALO + tm, :] = f_glu(cv_ref[...], cg_ref[...])
        _shift_stack(uext, ush, tm + HALO)
        acc = jnp.zeros((tm, D), F32) + db_ref[...]
        for w in range(KW):
            acc = acc + dw_ref[w:w + 1, :] * _shifted(ush, HALO - KW + 1 + w, tm)
        yc_ref[...] = acc
        act_ref[...] = f_conv_act(acc, g_ref[...], b_ref[...]).astype(BF16)
        pl.when(step == nt - 1)(finish)

    vec = pl.BlockSpec((1, D), lambda m: (0, 0))
    prev = lambda j: pl.BlockSpec((HALO, D), lambda m: (jnp.maximum(m * per - 1, 0), j))
    res = pl.pallas_call(
        body, name="conv_forward", grid=(nt,),
        out_shape=(jax.ShapeDtypeStruct((t, D), F32), jax.ShapeDtypeStruct((t, D), BF16)) + _gathered(shards),
        in_specs=[pl.BlockSpec((tm, D), lambda m: (m, 4)), pl.BlockSpec((tm, D), lambda m: (m, 5)), prev(4), prev(5),
                  pl.BlockSpec((HALO, D), lambda m: (0, 0)), vec, vec, vec] + [ANYSPEC] * n,
        out_specs=(pl.BlockSpec((tm, D), lambda m: (m, 0)), pl.BlockSpec((tm, D), lambda m: (m, 0))) + (ANYSPEC,) * n,
        scratch_shapes=[pltpu.VMEM((HALO + tm, D), F32), pltpu.VMEM((8, tm + HALO, D), F32)] + _comm_sems(n, True),
        compiler_params=_params("arbitrary"),
    )(p, p, p, p, dw, db, ln_g, ln_b, *shards)
    return res[0], res[1], res[2:]


def _mix_forward_tile(x_ref, og_ref, ga_ref, gb_ref, qe_ref, intra_ref, st_ref, act_ref, wa_ref, wb_ref, wo_ref,
                      hg_ref, rows):
    o = _inter(qe_ref[...], st_ref, rows) + intra_ref[...]
    a = f_head_out(o, og_ref[...], hg_ref[...])
    ya = _dot(a.astype(BF16), wa_ref[...], _NN)
    yb = _dot(act_ref[...], wb_ref[...], _NN)
    merged = f_merge(ga_ref[...], gb_ref[...], ya, yb)
    y = _dot(merged.astype(BF16), wo_ref[...], _NN)
    return o, a, ya, yb, merged, y


def _mix_specs(tm, tiles=None):
    at = (lambda m: m) if tiles is None else (lambda m: tiles - 1 - m)
    col = lambda j: pl.BlockSpec((tm, D), lambda m: (at(m), j))
    row = pl.BlockSpec((tm, D), lambda m: (at(m), 0))
    per_chunk = lambda rows: pl.BlockSpec((tm // CH, rows, D), lambda m: (at(m), 0, 0))
    return col, row, pl.BlockSpec((1, D), lambda m: (0, 0)), per_chunk


def mix_forward(x, p, qe, intra, st_prev, act, wa, wb, wo, hg, g1, post):
    t = x.shape[0]
    tm = min(512, t)

    def body(x_ref, og_ref, ga_ref, gb_ref, qe_ref, intra_ref, st_ref, act_ref, wa_ref, wb_ref, wo_ref, hg_ref,
             g1_ref, post_ref, x1_ref):
        y = _mix_forward_tile(x_ref, og_ref, ga_ref, gb_ref, qe_ref, intra_ref, st_ref, act_ref, wa_ref, wb_ref,
                              wo_ref, hg_ref, tm)[-1]
        x1_ref[...] = f_residual(x_ref[...], y, g1_ref[...], post_ref[...])

    col, row, vec, per_chunk = _mix_specs(tm)
    return pl.pallas_call(
        body, name="mix_forward", grid=(t // tm,),
        out_shape=jax.ShapeDtypeStruct((t, D), F32),
        in_specs=[row, col(3), col(6), col(7), row, row, per_chunk(DK), row, VSPEC, VSPEC, VSPEC, vec, vec, vec],
        out_specs=row,
        compiler_params=_params("parallel"),
    )(x, p, p, p, qe, intra, st_prev, act, wa, wb, wo, hg, g1, post)


def ffn_forward_backward(x1, target, w1, w2, pre, sc, sh, g2, post):
    t = x1.shape[0]
    tm = min(512, t)
    nb = w1.shape[0]
    fb = w1.shape[2]

    def body(x_ref, tg_ref, w1_ref, w2_ref, pre_ref, sc_ref, sh_ref, g2_ref, post_ref,
             dx_ref, h2_ref, r_ref, dz_ref, dy2_ref, acc_ref, z_sc):
        @pl.when(pl.program_id(0) == 0)
        def _():
            acc_ref[...] = jnp.zeros_like(acc_ref)

        x1v = x_ref[...]
        h2, vjp_h = jax.vjp(f_modulate, x1v, pre_ref[...], sc_ref[...], sh_ref[...])
        h2b = h2.astype(BF16)
        h2_ref[...] = h2b
        y2 = jnp.zeros((tm, D), F32)
        for n in range(nb):
            z = _dot(h2b, w1_ref[n], _NN)
            z_sc[:, n * fb:(n + 1) * fb] = z
            r = jnp.square(jnp.maximum(z, 0.0)).astype(BF16)
            r_ref[:, n * fb:(n + 1) * fb] = r
            y2 = y2 + _dot(r, w2_ref[n], _NN)
        out, vjp_r = jax.vjp(f_residual, x1v, y2, g2_ref[...], post_ref[...])
        err = out - tg_ref[...]
        tok = jnp.mean(jnp.square(err), axis=-1, keepdims=True)
        acc_ref[5:6, :] += 0.5 * jnp.sum(tok, axis=0, keepdims=True)
        dx_a, dy2, dg2, dpost = vjp_r(err * (1.0 / D))
        dy2b = dy2.astype(BF16)
        dy2_ref[...] = dy2b
        dh2 = jnp.zeros((tm, D), F32)
        for n in range(nb):
            dr = _dot(dy2b, w2_ref[n], _NT)
            dz = (dr * (2.0 * jnp.maximum(z_sc[:, n * fb:(n + 1) * fb], 0.0))).astype(BF16)
            dz_ref[:, n * fb:(n + 1) * fb] = dz
            dh2 = dh2 + _dot(dz, w1_ref[n], _NT)
        dx_b, dpre, dsc, dsh = vjp_h(dh2)
        dx_ref[...] = dx_a + dx_b
        acc_ref[0:1, :] += dpre
        acc_ref[1:2, :] += dpost
        acc_ref[2:3, :] += dsc
        acc_ref[3:4, :] += dsh
        acc_ref[4:5, :] += dg2

    row = pl.BlockSpec((tm, D), lambda m: (m, 0))
    wide = pl.BlockSpec((tm, DFF), lambda m: (m, 0), pipeline_mode=pl.Buffered(1))
    vec = pl.BlockSpec((1, D), lambda m: (0, 0))
    return pl.pallas_call(
        body, name="ffn_forward_backward", grid=(t // tm,),
        out_shape=(jax.ShapeDtypeStruct((t, D), F32), jax.ShapeDtypeStruct((t, D), BF16),
                   jax.ShapeDtypeStruct((t, DFF), BF16), jax.ShapeDtypeStruct((t, DFF), BF16),
                   jax.ShapeDtypeStruct((t, D), BF16), jax.ShapeDtypeStruct((8, D), F32)),
        in_specs=[row, row, VSPEC, VSPEC, vec, vec, vec, vec, vec],
        out_specs=(row, row, wide, wide, row, pl.BlockSpec((8, D), lambda m: (0, 0))),
        scratch_shapes=[pltpu.VMEM((tm, DFF), F32)],
        compiler_params=_params("arbitrary"),
    )(x1, target, w1, w2, pre, sc, sh, g2, post)


def mix_backward(x, p, qe, intra, st_prev, act, wa, wb, wo, hg, g1, post, dx1, dec):
    t = x.shape[0]
    tm = min(256, t)
    nc = t // CH

    def body(x_ref, og_ref, ga_ref, gb_ref, qe_ref, intra_ref, st_ref, act_ref, wa_ref, wb_ref, wo_ref, hg_ref,
             g1_ref, post_ref, dx1_ref, dec_ref,
             dp_ref, do_ref, dqe_ref, gt_ref, dact_ref, a_ref, mg_ref, dya_ref, dyb_ref, dy_ref, acc_ref, gstate):
        @pl.when(pl.program_id(0) == 0)
        def _():
            acc_ref[...] = jnp.zeros_like(acc_ref)
            gstate[...] = jnp.zeros_like(gstate)

        o, a, ya, yb, merged, y = _mix_forward_tile(x_ref, og_ref, ga_ref, gb_ref, qe_ref, intra_ref, st_ref, act_ref,
                                                    wa_ref, wb_ref, wo_ref, hg_ref, tm)
        a_ref[...] = a.astype(BF16)
        mg_ref[...] = merged.astype(BF16)
        _, vjp_r = jax.vjp(f_residual, x_ref[...], y, g1_ref[...], post_ref[...])
        _, dy, dg1, dpost = vjp_r(dx1_ref[...])
        dyb16 = dy.astype(BF16)
        dy_ref[...] = dyb16
        dmerged = _dot(dyb16, wo_ref[...], _NT)
        _, vjp_m = jax.vjp(f_merge, ga_ref[...], gb_ref[...], ya, yb)
        dga, dgb, dya, dyb = vjp_m(dmerged)
        dp_ref[:, D:2 * D] = dga.astype(BF16)
        dp_ref[:, 2 * D:3 * D] = dgb.astype(BF16)
        dya16, dyb16b = dya.astype(BF16), dyb.astype(BF16)
        dya_ref[...] = dya16
        dyb_ref[...] = dyb16b
        da = _dot(dya16, wa_ref[...], _NT)
        dact_ref[...] = _dot(dyb16b, wb_ref[...], _NT)
        _, vjp_a = jax.vjp(f_head_out, o, og_ref[...], hg_ref[...])
        do, dog, dhg = vjp_a(da)
        dp_ref[:, 0:D] = dog.astype(BF16)
        do_ref[...] = do
        do16 = do.astype(BF16)
        qe16 = qe_ref[...]
        for ci in reversed(range(tm // CH)):
            st = st_ref[ci].astype(BF16)
            rows = slice(ci * CH, (ci + 1) * CH)
            dqe, vt = [], []
            for h in range(HEADS):
                sl = slice(h * DK, (h + 1) * DK)
                dqe.append(_dot(do16[rows, sl], st[:, sl], _NN))
                vt.append(_dot(do16[rows, sl], qe16[rows, sl], _TN))
            dqe_ref[rows, :] = jnp.concatenate(dqe, axis=1)
            g = gstate[...]
            gt_ref[ci] = g
            gstate[...] = dec_ref[ci] * g + jnp.concatenate(vt, axis=1)
        acc_ref[0:1, :] += dg1
        acc_ref[1:2, :] += dpost
        acc_ref[2:3, :] += dhg
        acc_ref[3:4, :] += jnp.sum(dog, axis=0, keepdims=True)
        acc_ref[4:5, :] += jnp.sum(dga, axis=0, keepdims=True)
        acc_ref[5:6, :] += jnp.sum(dgb, axis=0, keepdims=True)

    nt = t // tm
    col, row, vec, per_chunk = _mix_specs(tm, nt)
    b16 = jax.ShapeDtypeStruct((t, D), BF16)
    f32 = jax.ShapeDtypeStruct((t, D), F32)
    return pl.pallas_call(
        body, name="mix_backward", grid=(nt,),
        out_shape=(jax.ShapeDtypeStruct((t, NDEV * D), BF16), f32, f32, jax.ShapeDtypeStruct((nc, DK, D), F32), f32,
                   b16, b16, b16, b16, b16, jax.ShapeDtypeStruct((8, D), F32)),
        in_specs=[row, col(3), col(6), col(7), row, row, per_chunk(DK), row, VSPEC, VSPEC, VSPEC, vec, vec, vec, row,
                  per_chunk(1)],
        out_specs=(pl.BlockSpec((tm, 3 * D), lambda m: (nt - 1 - m, 0)), row, row, per_chunk(DK), row, row, row, row,
                   row, row, pl.BlockSpec((8, D), lambda m: (0, 0))),
        scratch_shapes=[pltpu.VMEM((DK, D), F32)],
        compiler_params=_params("arbitrary"),
    )(x, p, p, p, qe, intra, st_prev, act, wa, wb, wo, hg, g1, post, dx1, dec)


def conv_backward(p, yc, dact, dw, ln_g, ln_b, blocks, dp):
    t = p.shape[0]
    tm = _conv_tile(t)
    per = tm // HALO
    nt = t // tm
    n = len(blocks)

    def body(*refs):
        cv_ref, cg_ref, cvp_ref, cgp_ref, yc_ref, ycn_ref, da_ref, dan_ref, dw_ref, g_ref, b_ref = refs[:11]
        dp_ref, acc_ref, ddw_ref = refs[12 + n:15 + n]
        uext, dyext, ush, dysh, ddw8, du_sc = refs[15 + 2 * n:21 + 2 * n]
        start, finish = _exchange_ops(refs[11:11 + n], refs[15 + n:15 + 2 * n], *refs[21 + 2 * n:])
        m = pl.program_id(0)

        @pl.when(m == 0)
        def _():
            start()
            acc_ref[...] = jnp.zeros_like(acc_ref)
            ddw8[...] = jnp.zeros_like(ddw8)

        cv, cg = cv_ref[...], cg_ref[...]
        u, vjp_u = jax.vjp(f_glu, cv, cg)
        uext[0:HALO, :] = jnp.where(m == 0, 0.0, f_glu(cvp_ref[...], cgp_ref[...]))
        uext[HALO:HALO + tm, :] = u
        _shift_stack(uext, ush, tm + HALO)
        _, vjp_c = jax.vjp(f_conv_act, yc_ref[...], g_ref[...], b_ref[...])
        dyc, dg, db = vjp_c(da_ref[...])
        _, vjp_n = jax.vjp(f_conv_act, ycn_ref[...], g_ref[...], b_ref[...])
        dyn = vjp_n(dan_ref[...])[0]
        dyext[0:tm, :] = dyc
        dyext[tm:tm + HALO, :] = jnp.where(m == nt - 1, 0.0, dyn)
        _shift_stack(dyext, dysh, tm + HALO)
        rb = min(128, tm)
        for lt in range(D // LANE):
            ls = slice(lt * LANE, (lt + 1) * LANE)
            for r0 in range(0, tm, rb):
                du_l = jnp.zeros((rb, LANE), F32)
                for w in range(KW):
                    du_l = du_l + dw_ref[w:w + 1, ls] * _shifted(dysh, KW - 1 - w + r0, rb, ls)
                du_sc[r0:r0 + rb, ls] = du_l
                dyc_l = dyext[r0:r0 + rb, ls]
                for w in range(KW):
                    prod = dyc_l * _shifted(ush, HALO - KW + 1 + w + r0, rb, ls)
                    part = jnp.sum(prod.reshape(4, rb // 32, 8, LANE), axis=1)
                    ddw8[w, :, ls] += (part[0] + part[1]) + (part[2] + part[3])
        du = du_sc[...]

        @pl.when(m == nt - 1)
        def _():
            ddw_ref[...] = jnp.sum(ddw8[...], axis=1)

        dcv, dcg = vjp_u(du)
        dp_ref[:, 0:D] = dcv.astype(BF16)
        dp_ref[:, D:2 * D] = dcg.astype(BF16)
        acc_ref[0:1, :] += jnp.sum(dyc, axis=0, keepdims=True)
        acc_ref[1:2, :] += dg
        acc_ref[2:3, :] += db
        acc_ref[3:4, :] += jnp.sum(dcv, axis=0, keepdims=True)
        acc_ref[4:5, :] += jnp.sum(dcg, axis=0, keepdims=True)
        pl.when(m == nt - 1)(finish)

    vec = pl.BlockSpec((1, D), lambda m: (0, 0))
    row = pl.BlockSpec((tm, D), lambda m: (m, 0))
    prev = lambda j: pl.BlockSpec((HALO, D), lambda m: (jnp.maximum(m * per - 1, 0), j))
    nxt = pl.BlockSpec((HALO, D), lambda m: (jnp.minimum((m + 1) * per, t // HALO - 1), 0))
    res = pl.pallas_call(
        body, name="conv_backward", grid=(nt,),
        out_shape=(jax.ShapeDtypeStruct(dp.shape, dp.dtype), jax.ShapeDtypeStruct((8, D), F32),
                   jax.ShapeDtypeStruct((HALO, D), F32)) + _exchanged(blocks),
        in_specs=[pl.BlockSpec((tm, D), lambda m: (m, 4)), pl.BlockSpec((tm, D), lambda m: (m, 5)), prev(4), prev(5),
                  row, nxt, row, nxt, pl.BlockSpec((HALO, D), lambda m: (0, 0)), vec, vec] + [ANYSPEC] * (n + 1),
        out_specs=(pl.BlockSpec((tm, 2 * D), lambda m: (m, 3)), pl.BlockSpec((8, D), lambda m: (0, 0)),
                   pl.BlockSpec((HALO, D), lambda m: (0, 0))) + (ANYSPEC,) * n,
        scratch_shapes=[pltpu.VMEM((HALO + tm, D), F32), pltpu.VMEM((tm + HALO, D), F32),
                        pltpu.VMEM((8, tm + HALO, D), F32), pltpu.VMEM((8, tm + HALO, D), F32),
                        pltpu.VMEM((HALO, 8, D), F32), pltpu.VMEM((tm, D), F32)] + _comm_sems(n, False),
        input_output_aliases={11 + n: 0},
        compiler_params=_params("arbitrary"),
    )(p, p, p, p, yc, yc, dact, dact, dw, ln_g, ln_b, *blocks, dp)
    return res[:3] + (res[3:],)


def hgrn_backward(p, logits, do, dqe, gt, st_prev, blocks, dp):
    t = p.shape[0]
    nc = t // CH
    n = len(blocks)

    def body(*refs):
        q_ref, f_ref, v_ref, l_ref, do_ref, dqe_ref, gt_ref, st_ref = refs[:8]
        dp_ref, acc_ref = refs[9 + n:11 + n]
        start, finish = _exchange_ops(refs[8:8 + n], refs[11 + n:11 + 2 * n], *refs[11 + 2 * n:])

        @pl.when(pl.program_id(0) == 0)
        def _():
            start()
            acc_ref[...] = jnp.zeros_like(acc_ref)

        gt_v = gt_ref[...]
        ddec = jnp.sum(gt_v * st_ref[...], axis=0, keepdims=True)
        _, vjp = jax.vjp(f_hgrn_chunk, q_ref[...], f_ref[...], v_ref[...], l_ref[...])
        dq, df, dv, dl = vjp((do_ref[...], dqe_ref[...], gt_v, ddec))
        dp_ref[:, 0:D] = dq.astype(BF16)
        dp_ref[:, D:2 * D] = df.astype(BF16)
        dp_ref[:, 2 * D:3 * D] = dv.astype(BF16)
        acc_ref[0:2, :] += dl
        acc_ref[2:3, :] += jnp.sum(dq, axis=0, keepdims=True)
        acc_ref[3:4, :] += jnp.sum(df, axis=0, keepdims=True)
        acc_ref[4:5, :] += jnp.sum(dv, axis=0, keepdims=True)
        pl.when(pl.program_id(0) == nc - 1)(finish)

    col = lambda j: pl.BlockSpec((CH, D), lambda c: (c, j))
    row = pl.BlockSpec((CH, D), lambda c: (c, 0))
    stspec = pl.BlockSpec((None, DK, D), lambda c: (c, 0, 0))
    res = pl.pallas_call(
        body, name="hgrn_backward", grid=(nc,),
        out_shape=(jax.ShapeDtypeStruct(dp.shape, dp.dtype), jax.ShapeDtypeStruct((8, D), F32)) + _exchanged(blocks),
        in_specs=[col(0), col(1), col(2), pl.BlockSpec((2, D), lambda c: (0, 0)), row, row, stspec, stspec]
        + [ANYSPEC] * (n + 1),
        out_specs=(pl.BlockSpec((CH, 3 * D), lambda c: (c, 1)), pl.BlockSpec((8, D), lambda c: (0, 0)))
        + (ANYSPEC,) * n,
        scratch_shapes=_comm_sems(n, False),
        input_output_aliases={8 + n: 0},
        compiler_params=_params("arbitrary"),
    )(p, p, p, logits, do, dqe, gt, st_prev, *blocks, dp)
    return res[:2] + (res[2:],)


def in_proj_backward(dp, w_all, x, dx1, gain, sc, sh):
    t = x.shape[0]
    tm = min(512, t)

    def body(dp_ref, w_ref, x_ref, dx1_ref, g_ref, sc_ref, sh_ref, gx_ref, acc_ref):
        @pl.when(pl.program_id(0) == 0)
        def _():
            acc_ref[...] = jnp.zeros_like(acc_ref)

        dh = jnp.zeros((tm, D), F32)
        for j in range(NDEV):
            dh = dh + _dot(dp_ref[:, j * D:(j + 1) * D], w_ref[DP_SPLIT[j]], _NT)
        _, vjp_h = jax.vjp(f_modulate, x_ref[...], g_ref[...], sc_ref[...], sh_ref[...])
        dx, dg, dsc, dsh = vjp_h(dh)
        gx_ref[...] = dx1_ref[...] + dx
        acc_ref[0:1, :] += dg
        acc_ref[1:2, :] += dsc
        acc_ref[2:3, :] += dsh

    row = pl.BlockSpec((tm, D), lambda m: (m, 0))
    vec = pl.BlockSpec((1, D), lambda m: (0, 0))
    return pl.pallas_call(
        body, name="in_proj_backward", grid=(t // tm,),
        out_shape=(jax.ShapeDtypeStruct((t, D), F32), jax.ShapeDtypeStruct((8, D), F32)),
        in_specs=[pl.BlockSpec((tm, NDEV * D), lambda m: (m, 0)), VSPEC, row, row, vec, vec, vec],
        out_specs=(row, pl.BlockSpec((8, D), lambda m: (0, 0))),
        compiler_params=_params("arbitrary"),
    )(dp, w_all, x, dx1, gain, sc, sh)


def weight_grad(a, b, nblk, ka, bn, a_blocked, name, b_col=lambda n: n, after=None):
    t = a.shape[0]
    tk = t if nblk == 1 else min(2048, t)
    nk = t // tk
    extra = [] if after is None else [after]

    def body(*refs):
        a_ref, b_ref = refs[:2]
        f_ref, h_ref, acc = refs[2 + len(extra):]
        k = pl.program_id(1)

        @pl.when(k == 0)
        def _():
            acc[...] = jnp.zeros_like(acc)

        acc[...] += _dot(a_ref[...], b_ref[...], _TN)

        @pl.when(k == nk - 1)
        def _():
            f_ref[...] = acc[...]
            h_ref[...] = acc[...].astype(BF16)

    a_idx = (lambda n, k: (k, n)) if a_blocked else (lambda n, k: (k, 0))
    b_idx = (lambda n, k: (k, 0)) if a_blocked else (lambda n, k: (k, b_col(n)))
    out = pl.BlockSpec((None, ka, bn), lambda n, k: (n, 0, 0))
    return pl.pallas_call(
        body, name=name, grid=(nblk, nk),
        out_shape=(jax.ShapeDtypeStruct((nblk, ka, bn), F32), jax.ShapeDtypeStruct((nblk, ka, bn), BF16)),
        in_specs=[pl.BlockSpec((tk, ka), a_idx), pl.BlockSpec((tk, bn), b_idx)] + [ANYSPEC] * len(extra),
        out_specs=(out, out),
        scratch_shapes=[pltpu.VMEM((ka, bn), F32)],
        compiler_params=_params("parallel", "arbitrary"),
    )(a, b, *extra)


def ada_backward(call_t, dmod_cols, w, m, v):
    def body(c_ref, d_ref, w_ref, m_ref, v_ref, g_ref, dl_ref, nm_ref, nv_ref):
        ct, dm = c_ref[...], d_ref[...]
        g = ct[:, 0:1] * dm[0:1, :]
        for r in range(1, NDEV):
            g = g + ct[:, r:r + 1] * dm[r:r + 1, :]
        g_ref[...] = g
        dl_ref[...], nm_ref[...], nv_ref[...] = _adamw(w_ref[...], g, m_ref[...], v_ref[...])

    br = 256
    blk = pl.BlockSpec((br, w.shape[1]), lambda i: (i, 0))
    s = jax.ShapeDtypeStruct(w.shape, F32)
    return pl.pallas_call(
        body, name="ada_backward", grid=(w.shape[0] // br,), out_shape=(s, s, s, s),
        in_specs=[pl.BlockSpec((br, NDEV), lambda i: (i, 0)), pl.BlockSpec(dmod_cols.shape, lambda i: (0, 0)), blk,
                  blk, blk],
        out_specs=(blk, blk, blk, blk), compiler_params=_params("parallel"),
    )(call_t, dmod_cols, w, m, v)


def adamw_small(total, ddw_mine, recipes, ws, ms, vs):
    n = len(ws)

    def body(*refs):
        tot, ddw = refs[0], refs[1]
        w_refs, m_refs, v_refs = refs[2:2 + n], refs[2 + n:2 + 2 * n], refs[2 + 2 * n:2 + 3 * n]
        outs = refs[2 + 3 * n:2 + 7 * n]
        loss_ref = refs[2 + 7 * n]
        for i, rec in enumerate(recipes):
            if rec == "dw":
                g = ddw[...]
            elif isinstance(rec, tuple):
                g = tot[rec[0]:rec[1], :]
            else:
                g = jnp.concatenate([tot[r:r + 1, :] for r in rec], axis=1) if len(rec) > 1 else tot[rec[0]:rec[0] + 1, :]
            dl, nm, nv = _adamw(w_refs[i][...], g, m_refs[i][...], v_refs[i][...])
            outs[4 * i][...] = g
            outs[4 * i + 1][...] = dl
            outs[4 * i + 2][...] = nm
            outs[4 * i + 3][...] = nv
        loss_ref[...] = tot[LOSS_ROW:LOSS_ROW + 1, 0:128]

    shapes = []
    for w in ws:
        shapes += [jax.ShapeDtypeStruct(w.shape, F32)] * 4
    res = pl.pallas_call(
        body, name="adamw_small", out_shape=tuple(shapes) + (jax.ShapeDtypeStruct((1, 128), F32),),
        in_specs=[VSPEC] * (2 + 3 * n), out_specs=(VSPEC,) * (4 * n + 1), compiler_params=_params(),
    )(total, ddw_mine, *ws, *ms, *vs)
    return [res[4 * i:4 * i + 4] for i in range(n)], res[4 * n]


def reduce_and_adamw(sel, w, g_alls, g_recvs, m, v, name):
    r, c = w.shape
    br = min(256, r)
    npair = len(g_alls)

    def body(sel_ref, w_ref, *refs):
        go_refs, gr_refs = refs[:npair], refs[npair:2 * npair]
        m_ref, v_ref, g_ref, dl_ref, nm_ref, nv_ref = refs[2 * npair:]
        g = None
        for i in range(npair):
            gi = go_refs[i][...]
            for k in range(NDEV - 1):
                gi = gi + gr_refs[i][k].astype(F32)
            g = gi if g is None else jnp.where(sel_ref[1] == i, gi, g)
        g_ref[...] = g
        dl_ref[...], nm_ref[...], nv_ref[...] = _adamw(w_ref[...], g, m_ref[...], v_ref[...])

    blk = pl.BlockSpec((br, c), lambda i, sel_ref: (i, 0))
    own = pl.BlockSpec((None, br, c), lambda i, sel_ref: (sel_ref[0], i, 0))
    recv = pl.BlockSpec((NDEV - 1, br, c), lambda i, sel_ref: (0, i, 0))
    s = jax.ShapeDtypeStruct(w.shape, F32)
    return pl.pallas_call(
        body, name=name, out_shape=(s, s, s, s),
        grid_spec=pltpu.PrefetchScalarGridSpec(
            num_scalar_prefetch=1, grid=(r // br,),
            in_specs=[blk] + [own] * npair + [recv] * npair + [blk, blk],
            out_specs=(blk, blk, blk, blk)),
        compiler_params=_params("parallel"),
    )(sel, w, *g_alls, *g_recvs, m, v)


def kernel(x, c, w_ada, b_ada, pre_norm_tm, post_norm_tm, pre_norm_cm, post_norm_cm, w_in, b_in, hg_lb_logits, hg_norm, conv_dw, conv_db, conv_ln_g, conv_ln_b, w_br_a, w_br_b, w_out, w_ff1, w_ff2, loss_target, m_w_ada, m_b_ada, m_pre_norm_tm, m_post_norm_tm, m_pre_norm_cm, m_post_norm_cm, m_w_in, m_b_in, m_hg_lb_logits, m_hg_norm, m_conv_dw, m_conv_db, m_conv_ln_g, m_conv_ln_b, m_w_br_a, m_w_br_b, m_w_out, m_w_ff1, m_w_ff2, v_w_ada, v_b_ada, v_pre_norm_tm, v_post_norm_tm, v_pre_norm_cm, v_post_norm_cm, v_w_in, v_b_in, v_hg_lb_logits, v_hg_norm, v_conv_dw, v_conv_db, v_conv_ln_g, v_conv_ln_b, v_w_br_a, v_w_br_b, v_w_out, v_w_ff1, v_w_ff2):
    t = x.shape[1]
    me = 4 * lax.axis_index("x") + 2 * lax.axis_index("y") + lax.axis_index("c")
    xs = x[0]
    tgt = loss_target[0]

    mod, call, dw_all = ada_forward(c, w_ada[0], b_ada, conv_dw[0])
    sh1, sc1, g1, sh2, sc2, g2 = [mod[:, i * D:(i + 1) * D] for i in range(6)]
    dw = jnp.pad(dw_all.transpose(1, 0, 2).reshape(KW, D), ((0, HALO - KW), (0, 0)))

    p, h, win_all = in_proj(xs, pre_norm_tm, sc1, sh1, w_in[0].astype(BF16), b_in)
    yc, act, (w1_all,) = conv_forward(p, dw, conv_db, conv_ln_g, conv_ln_b, [w_ff1[0].astype(BF16)])
    intra, qe, st_prev, dec, (w2_all, wa_all, wb_all, wo_all) = hgrn_local(
        p, hg_lb_logits, [w_ff2[0].astype(BF16), w_br_a[0].astype(BF16), w_br_b[0].astype(BF16),
                          w_out[0].astype(BF16)])
    wa, wb, wo = wa_all.reshape(D, D), wb_all.reshape(D, D), wo_all.reshape(D, D)
    x1 = mix_forward(xs, p, qe, intra, st_prev, act, wa, wb, wo, hg_norm, g1, post_norm_tm)

    dx1, h2, r, dz, dy2, acc_ffn = ffn_forward_backward(x1, tgt, w1_all, w2_all, pre_norm_cm, sc2, sh2, g2,
                                                        post_norm_cm)
    rows = D // NDEV
    g1_f, g1_h = weight_grad(h2, dz, NDEV, D, DFF // NDEV, False, "grad_w_ff1")
    g2_f, g2_h = weight_grad(r, dy2, 4, D, D, True, "grad_w_ff2")
    g2_f, g2_h = g2_f.reshape(NDEV, DFF // NDEV, D), g2_h.reshape(NDEV, DFF // NDEV, D)

    (dp, do, dqe, gt, dact, a16, mg16, dya, dyb, dy, acc_mix) = mix_backward(
        xs, p, qe, intra, st_prev, act, wa, wb, wo, hg_norm, g1, post_norm_tm, dx1, dec)
    ga_f, ga_h = weight_grad(a16, dya, 1, D, D, False, "grad_w_br_a")
    gb_f, gb_h = weight_grad(act, dyb, 1, D, D, False, "grad_w_br_b")
    go_f, go_h = weight_grad(mg16, dy, 1, D, D, False, "grad_w_out")
    dp, acc_conv, ddw, (r_ff1, r_ff2) = conv_backward(p, yc, dact, dw, conv_ln_g, conv_ln_b, [g1_h, g2_h], dp)
    dp, acc_hg, (r_a, r_b, r_o) = hgrn_backward(
        p, hg_lb_logits, do, dqe, gt, st_prev,
        [ga_h.reshape(NDEV, rows, D), gb_h.reshape(NDEV, rows, D), go_h.reshape(NDEV, rows, D)], dp)
    dp_col = lambda n: jnp.where(n < 3, n + 3, jnp.where(n == 3, 0, jnp.where(n < 6, n + 2, n - 5)))
    even, odd = [0, 2, 4, 6], [1, 3, 5, 7]
    gin_f0, gin_h0 = weight_grad(h, dp, 4, D, D, False, "grad_w_in_even", b_col=lambda j: dp_col(2 * j))
    send0, recv0, thru0, landing0, token0 = exchange_start(gin_h0, even, "exchange_start_even")
    gin_f1, gin_h1 = weight_grad(h, dp, 4, D, D, False, "grad_w_in_odd", b_col=lambda j: dp_col(2 * j + 1),
                                 after=token0)
    send1, recv1, thru1, landing1, token1 = exchange_start(gin_h1, odd, "exchange_start_odd")
    grad_x, acc_in = in_proj_backward(dp, win_all, xs, dx1, pre_norm_tm, sc1, sh1 + token1[0:1, 0:1])

    own = [ga_f.reshape(NDEV, rows, D), gb_f.reshape(NDEV, rows, D), go_f.reshape(NDEV, rows, D), g1_f, g2_f]
    recv = [r_a, r_b, r_o, r_ff1, r_ff2]
    big = {}
    names = ["w_br_a", "w_br_b", "w_out", "w_ff1", "w_ff2"]
    ws = [w_br_a, w_br_b, w_out, w_ff1, w_ff2]
    ms = [m_w_br_a, m_w_br_b, m_w_out, m_w_ff1, m_w_ff2]
    vs = [v_w_br_a, v_w_br_b, v_w_out, v_w_ff1, v_w_ff2]
    sel = jnp.stack([me, 0]).astype(jnp.int32)
    for i, nm in enumerate(names):
        big[nm] = [o[None] for o in reduce_and_adamw(sel, ws[i][0], [own[i]], [recv[i]], ms[i][0], vs[i][0],
                                                     "adamw_" + nm)]
    big, grad_x, acc_in = lax.optimization_barrier((big, grad_x, acc_in))
    r_in0 = exchange_wait(send0, recv0, thru0, landing0, acc_in, even, "exchange_wait_even")
    r_in1 = exchange_wait(send1, recv1, thru1, landing1, acc_in, odd, "exchange_wait_odd")
    r_in0, r_in1, acc_in = lax.optimization_barrier((r_in0, r_in1, acc_in))
    sel_in = jnp.stack([me // 2, me % 2]).astype(jnp.int32)
    big["w_in"] = [o[None] for o in reduce_and_adamw(sel_in, w_in[0], [gin_f0, gin_f1], [r_in0, r_in1], m_w_in[0],
                                                     v_w_in[0], "adamw_w_in")]

    dmod_rows = [2, 1, 8, 19, 18, 20]
    total, kept = gather_and_sum_rows([acc_in, acc_mix, acc_ffn, acc_hg, acc_conv, ddw], dmod_rows)
    dmod_all = kept[:, 0:6, :].reshape(NDEV, 6 * D)
    wcols = w_ada.shape[2]
    gwa, dwa, nmwa, nvwa = ada_backward(call.T, lax.dynamic_slice_in_dim(dmod_all, me * wcols, wcols, axis=1),
                                        w_ada[0], m_w_ada[0], v_w_ada[0])
    ddw_mine = lax.dynamic_slice_in_dim(total[40:40 + KW], me * (D // NDEV), D // NDEV, axis=1)

    small_names = ["b_ada", "pre_norm_tm", "post_norm_tm", "pre_norm_cm", "post_norm_cm", "b_in", "hg_lb_logits",
                   "hg_norm", "conv_db", "conv_ln_g", "conv_ln_b", "conv_dw"]
    recipes = [dmod_rows, [0], [9], [16], [17], [26, 27, 28, 11, 35, 36, 12, 13], (24, 26), [10], [32], [33], [34],
               "dw"]
    small_w = [b_ada, pre_norm_tm, post_norm_tm, pre_norm_cm, post_norm_cm, b_in, hg_lb_logits, hg_norm, conv_db,
               conv_ln_g, conv_ln_b, conv_dw[0]]
    small_m = [m_b_ada, m_pre_norm_tm, m_post_norm_tm, m_pre_norm_cm, m_post_norm_cm, m_b_in, m_hg_lb_logits,
               m_hg_norm, m_conv_db, m_conv_ln_g, m_conv_ln_b, m_conv_dw[0]]
    small_v = [v_b_ada, v_pre_norm_tm, v_post_norm_tm, v_pre_norm_cm, v_post_norm_cm, v_b_in, v_hg_lb_logits,
               v_hg_norm, v_conv_db, v_conv_ln_g, v_conv_ln_b, v_conv_dw[0]]
    small_out, loss_row = adamw_small(total, ddw_mine, recipes, small_w, small_m, small_v)
    loss = loss_row[0, 0]
    sm = {nm: list(o) for nm, o in zip(small_names, small_out)}
    sm["conv_dw"] = [o[None] for o in sm["conv_dw"]]

    order = ["w_ada", "b_ada", "pre_norm_tm", "post_norm_tm", "pre_norm_cm", "post_norm_cm", "w_in", "b_in",
             "hg_lb_logits", "hg_norm", "conv_dw", "conv_db", "conv_ln_g", "conv_ln_b", "w_br_a", "w_br_b", "w_out",
             "w_ff1", "w_ff2"]
    res = dict(sm)
    res.update(big)
    res["w_ada"] = [gwa[None], dwa[None], nmwa[None], nvwa[None]]
    outs = [loss, grad_x[None]]
    for j in range(4):
        outs += [res[nm][j] for nm in order]
    return tuple(outs)
```

```python
import jax
import jax.numpy as jnp
from jax import lax
from jax.experimental import pallas as pl
from jax.experimental.pallas import tpu as pltpu

F32 = jnp.float32
BF16 = jnp.bfloat16
MESH = pl.DeviceIdType.MESH

D = 1024
HEADS = 8
DK = 128
LANE = 128
CH = 128
LEVELS = 7
KW = 31
HALO = 32
DFF = 4096
NDEV = 8
EPS = 1e-6
DP_SPLIT = (3, 6, 7, 0, 1, 2, 4, 5)
LOSS_ROW = 21
ADAM_LR, ADAM_B1, ADAM_B2, ADAM_EPS, ADAM_WD, ADAM_STEP = 0.001, 0.9, 0.999, 1e-08, 0.01, 10
VMEM_LIMIT = 58 * 1024 * 1024

_NN = (((1,), (0,)), ((), ()))
_NT = (((1,), (1,)), ((), ()))
_TN = (((0,), (0,)), ((), ()))

VSPEC = pl.BlockSpec(memory_space=pltpu.VMEM)
ANYSPEC = pl.BlockSpec(memory_space=pl.ANY)


def _params(*sem):
    return pltpu.CompilerParams(dimension_semantics=sem or None, vmem_limit_bytes=VMEM_LIMIT)


def _dot(a, b, dims):
    return lax.dot_general(a, b, dims, preferred_element_type=F32)


@jax.custom_vjp
def mm_nn(a, b):
    return _dot(a.astype(BF16), b.astype(BF16), _NN)


def _mm_nn_fwd(a, b):
    ab, bb = a.astype(BF16), b.astype(BF16)
    return _dot(ab, bb, _NN), (ab, bb)


def _mm_nn_bwd(res, ct):
    ab, bb = res
    cb = ct.astype(BF16)
    return _dot(cb, bb, _NT), _dot(ab, cb, _TN)


mm_nn.defvjp(_mm_nn_fwd, _mm_nn_bwd)


@jax.custom_vjp
def mm_nt(a, b):
    return _dot(a.astype(BF16), b.astype(BF16), _NT)


def _mm_nt_fwd(a, b):
    ab, bb = a.astype(BF16), b.astype(BF16)
    return _dot(ab, bb, _NT), (ab, bb)


def _mm_nt_bwd(res, ct):
    ab, bb = res
    cb = ct.astype(BF16)
    return _dot(cb, bb, _NN), _dot(cb, ab, _TN)


mm_nt.defvjp(_mm_nt_fwd, _mm_nt_bwd)


@jax.custom_vjp
def mm_tn(a, b):
    return _dot(a.astype(BF16), b.astype(BF16), _TN)


def _mm_tn_fwd(a, b):
    ab, bb = a.astype(BF16), b.astype(BF16)
    return _dot(ab, bb, _TN), (ab, bb)


def _mm_tn_bwd(res, ct):
    ab, bb = res
    cb = ct.astype(BF16)
    return _dot(bb, cb, _NT), _dot(ab, cb, _NN)


mm_tn.defvjp(_mm_tn_fwd, _mm_tn_bwd)


def _rms(x):
    return x * lax.rsqrt(jnp.mean(x * x, axis=-1, keepdims=True) + EPS)


_sigmoid = jax.nn.sigmoid


def _silu(x):
    return x * _sigmoid(x)


def f_modulate(x, gain, sc, sh):
    return _rms(x) * gain * (1.0 + sc) + sh


def f_residual(x, y, gate, gain):
    return x + gate * (_rms(y) * gain)


def f_merge(ga, gb, ya, yb):
    return _sigmoid(ga) * ya + _sigmoid(gb) * yb


def f_head_out(o, og, hg):
    heads = [_rms(o[:, h * DK:(h + 1) * DK]) for h in range(HEADS)]
    return jnp.concatenate(heads, axis=1) * hg * _silu(og)


def f_conv_act(u, g, b):
    mu = jnp.mean(u, axis=-1, keepdims=True)
    var = jnp.mean(jnp.square(u - mu), axis=-1, keepdims=True)
    return _silu((u - mu) * lax.rsqrt(var + EPS) * g + b)


def f_glu(cv, cg):
    return cv * _sigmoid(cg)


def _split2(x):
    hi = x.astype(BF16)
    return hi, (x - hi.astype(F32)).astype(BF16)


def _tri(transposed):
    i = lax.broadcasted_iota(jnp.int32, (CH, CH), 1 if transposed else 0)
    t = lax.broadcasted_iota(jnp.int32, (CH, CH), 0 if transposed else 1)
    return jnp.where(t <= i, 1.0, 0.0).astype(BF16)


def _blocks3(x, rows):
    return x.reshape(CH // rows, rows, x.shape[-1])


def _mid_broadcast(b, lev):
    h = 1 << (LEVELS - 1 - lev)
    if h >= 4:
        x3 = _blocks3(b, 2 * h)
        return jnp.broadcast_to(x3[:, h - 1:h, :], x3.shape).reshape(b.shape)
    x3 = _blocks3(b, 8)
    sub = lax.broadcasted_iota(jnp.int32, x3.shape, 1)
    out = None
    for first in range(0, 8, 2 * h):
        piece = jnp.broadcast_to(x3[:, first + h - 1:first + h, :], x3.shape)
        out = piece if out is None else jnp.where(sub >= first, piece, out)
    return out.reshape(b.shape)


def _mid_scatter(d, lev):
    h = 1 << (LEVELS - 1 - lev)
    if h >= 4:
        x3 = _blocks3(d, 2 * h)
        row = lax.broadcasted_iota(jnp.int32, x3.shape, 1)
        total = jnp.sum(x3, axis=1, keepdims=True)
        return jnp.where(row == h - 1, total, 0.0).reshape(d.shape)
    x3 = _blocks3(d, 8)
    sub = lax.broadcasted_iota(jnp.int32, x3.shape, 1)
    out = jnp.zeros_like(x3)
    for first in range(0, 8, 2 * h):
        inside = (sub >= first) & (sub < first + 2 * h)
        total = jnp.sum(jnp.where(inside, x3, 0.0), axis=1, keepdims=True)
        out = jnp.where(sub == first + h - 1, total, out)
    return out.reshape(d.shape)


@jax.custom_vjp
def decay_sums(g):
    hi, lo = _split2(g)
    tri = _tri(False)
    b = _dot(tri, hi, _NN) + _dot(tri, lo, _NN)
    return (b,) + tuple(b - _mid_broadcast(b, lev) for lev in range(LEVELS))


def _decay_sums_fwd(g):
    return decay_sums(g), None


def _decay_sums_bwd(_, cts):
    db = cts[0]
    for lev in range(LEVELS):
        db = db + cts[1 + lev] - _mid_scatter(cts[1 + lev], lev)
    hi, lo = _split2(db)
    tri = _tri(True)
    return (_dot(tri, hi, _NN) + _dot(tri, lo, _NN),)


decay_sums.defvjp(_decay_sums_fwd, _decay_sums_bwd)


def _score_masks():
    i = lax.broadcasted_iota(jnp.int32, (CH, CH), 0)
    j = lax.broadcasted_iota(jnp.int32, (CH, CH), 1)
    masks = [i == j]
    for lev in range(LEVELS):
        sh = LEVELS - 1 - lev
        same = (i >> (sh + 1)) == (j >> (sh + 1))
        masks.append(same & (((i >> sh) & 1) == 1) & (((j >> sh) & 1) == 0))
    return [jnp.where(m, 1.0, 0.0) for m in masks]


def f_hgrn_chunk(q_r, f_r, v, logits):
    l0, l1 = logits[0:1, :], logits[1:2, :]
    mx = lax.stop_gradient(jnp.maximum(l0, l1))
    e0, e1 = jnp.exp(l0 - mx), jnp.exp(l1 - mx)
    lb = e0 / (e0 + e1)
    q = _silu(q_r)
    f = lb + (1.0 - lb) * _sigmoid(f_r)
    k = 1.0 - f
    sums = decay_sums(jnp.log(f))
    b = sums[0]
    btot = b[CH - 1:CH, :]
    qe = q * jnp.exp(b)
    ke = k * jnp.exp(btot - b)
    dec = jnp.exp(btot)
    qs, ks = [q], [k]
    row = lax.broadcasted_iota(jnp.int32, b.shape, 0)
    for lev in range(LEVELS):
        upper = ((row >> (LEVELS - 1 - lev)) & 1) == 1
        e = sums[1 + lev]
        qs.append(q * jnp.exp(jnp.where(upper, e, 0.0)))
        ks.append(k * jnp.exp(jnp.where(upper, 0.0, -e)))
    masks = _score_masks()
    intra, ut = [], []
    for h in range(HEADS):
        sl = slice(h * DK, (h + 1) * DK)
        sc = None
        for lev in range(LEVELS + 1):
            s = mm_nt(qs[lev][:, sl], ks[lev][:, sl]) * masks[lev]
            sc = s if sc is None else sc + s
        intra.append(mm_nn(sc, v[:, sl]))
        ut.append(mm_tn(v[:, sl], ke[:, sl]))
    return jnp.concatenate(intra, axis=1), qe, jnp.concatenate(ut, axis=1), dec


def _inter(qe_b, st_ref, rows):
    out = []
    for ci in range(rows // CH):
        st = st_ref[ci].astype(BF16)
        heads = [_dot(qe_b[ci * CH:(ci + 1) * CH, h * DK:(h + 1) * DK], st[:, h * DK:(h + 1) * DK], _NT)
                 for h in range(HEADS)]
        out.append(jnp.concatenate(heads, axis=1))
    return jnp.concatenate(out, axis=0)


def _shift_stack(src, dst, rows):
    dst[0, 0:rows, :] = src[0:rows, :]
    for b in range(1, 8):
        dst[b, 0:rows - 8, :] = src[pl.ds(b, rows - 8), :]


def _shifted(stack, offset, rows, lanes=slice(None)):
    return stack[offset % 8, pl.ds(8 * (offset // 8), rows), lanes]


def _adamw(w, g, m, v):
    m = ADAM_B1 * m + (1.0 - ADAM_B1) * g
    v = ADAM_B2 * v + (1.0 - ADAM_B2) * jnp.square(g)
    m_hat = m / (1.0 - ADAM_B1 ** ADAM_STEP)
    v_hat = v / (1.0 - ADAM_B2 ** ADAM_STEP)
    delta = -ADAM_LR * (m_hat / (jnp.sqrt(v_hat) + ADAM_EPS) + ADAM_WD * w)
    return delta, m, v


def _me():
    return lax.axis_index("x"), lax.axis_index("y"), lax.axis_index("c")


def _peer(k):
    x, y, c = _me()
    mask = k + 1
    px = (1 - x) if (mask >> 2) & 1 else x
    py = (1 - y) if (mask >> 1) & 1 else y
    pc = (1 - c) if mask & 1 else c
    return (px, py, pc), 4 * px + 2 * py + pc


def ada_forward(c, w_ada, b_ada, dw):
    wcols = w_ada.shape[1]

    def body(c_ref, w_ref, b_ref, dw_ref, mod_ref, call_ref, dwall_ref, part_ref, modp_ref, send_sems, recv_sems):
        x, y, cc = _me()
        me = 4 * x + 2 * y + cc
        call_ref[me] = _silu(c_ref[...])
        dwall_ref[me] = dw_ref[...]
        sends = []
        for k in range(NDEV - 1):
            dev, pidx = _peer(k)
            cp = pltpu.make_async_remote_copy(dwall_ref.at[me], dwall_ref.at[me], send_sems.at[2 * (NDEV - 1) + k],
                                              recv_sems.at[2 * (NDEV - 1) + k], device_id=dev, device_id_type=MESH)
            cp.start()
            sends.append(cp)
        for k in range(NDEV - 1):
            dev, _ = _peer(k)
            cp = pltpu.make_async_remote_copy(call_ref.at[me], call_ref.at[me], send_sems.at[k], recv_sems.at[k],
                                              device_id=dev, device_id_type=MESH)
            cp.start()
            sends.append(cp)
        for k in range(NDEV - 1):
            _, pidx = _peer(k)
            pltpu.make_async_remote_copy(call_ref.at[pidx], call_ref.at[pidx], send_sems.at[k], recv_sems.at[k],
                                         device_id=_peer(k)[0], device_id_type=MESH).wait_recv()
        call = jnp.concatenate([call_ref[r] for r in range(NDEV)], axis=0)
        part = _dot(call.astype(BF16), w_ref[...].astype(BF16), _NN)
        for r in range(NDEV):
            part_ref[r] = part[r:r + 1, :]
        modp_ref[me] = part_ref[me]
        for k in range(NDEV - 1):
            dev, pidx = _peer(k)
            cp = pltpu.make_async_remote_copy(part_ref.at[pidx], modp_ref.at[me], send_sems.at[NDEV - 1 + k],
                                              recv_sems.at[NDEV - 1 + k], device_id=dev, device_id_type=MESH)
            cp.start()
            sends.append(cp)
        for k in range(NDEV - 1):
            dev, pidx = _peer(k)
            pltpu.make_async_remote_copy(part_ref.at[pidx], modp_ref.at[pidx], send_sems.at[NDEV - 1 + k],
                                         recv_sems.at[NDEV - 1 + k], device_id=dev, device_id_type=MESH).wait_recv()
        for k in range(NDEV - 1):
            dev, pidx = _peer(k)
            pltpu.make_async_remote_copy(dwall_ref.at[pidx], dwall_ref.at[pidx], send_sems.at[2 * (NDEV - 1) + k],
                                         recv_sems.at[2 * (NDEV - 1) + k], device_id=dev,
                                         device_id_type=MESH).wait_recv()
        for cp in sends:
            cp.wait_send()
        mod_ref[...] = jnp.concatenate([modp_ref[r] for r in range(NDEV)], axis=1) + b_ref[...]

    mod, call, dw_all = pl.pallas_call(
        body, name="ada_forward",
        out_shape=(jax.ShapeDtypeStruct((1, NDEV * wcols), F32), jax.ShapeDtypeStruct((NDEV, 1, D), F32),
                   jax.ShapeDtypeStruct((NDEV,) + dw.shape, F32)),
        in_specs=[VSPEC] * 4, out_specs=(VSPEC,) * 3,
        scratch_shapes=[pltpu.VMEM((NDEV, 1, wcols), F32), pltpu.VMEM((NDEV, 1, wcols), F32),
                        pltpu.SemaphoreType.DMA((3 * (NDEV - 1),)), pltpu.SemaphoreType.DMA((3 * (NDEV - 1),))],
        compiler_params=_params(),
    )(c, w_ada, b_ada, dw)
    return mod, call.reshape(NDEV, D), dw_all


def _comm_sems(n, local):
    sems = [pltpu.SemaphoreType.DMA((7 * n,)), pltpu.SemaphoreType.DMA((7 * n,))]
    return sems + ([pltpu.SemaphoreType.DMA((n,))] if local else [])


def _gather2_ops(ins, outs, send_sems, recv_sems, local_sems):
    n = len(ins)
    x, y, c = _me()
    me, sibling = (x, y, c), (x, y, 1 - c)
    chips = [(1 - x, y), (x, 1 - y), (1 - x, 1 - y)]

    def slot(p):
        return 4 * p[0] + 2 * p[1] + p[2]

    def copy(a, k, block, to, src=None):
        return pltpu.make_async_remote_copy(
            src_ref=outs[a].at[slot(block)] if src is None else src, dst_ref=outs[a].at[slot(block)],
            send_sem=send_sems.at[a * 7 + k], recv_sem=recv_sems.at[a * 7 + k], device_id=to, device_id_type=MESH)

    def local(a):
        return pltpu.make_async_copy(ins[a], outs[a].at[slot(me)], local_sems.at[a])

    def first(a):
        return [copy(a, 0, me, sibling, src=ins[a])] + [copy(a, 1 + j, me, (*chip, c), src=ins[a])
                                                        for j, chip in enumerate(chips)]

    def passed(a):
        return [copy(a, 4 + j, (*chip, c), sibling) for j, chip in enumerate(chips)]

    def start():
        for a in range(n):
            local(a).start()
            for cp in first(a):
                cp.start()

    def forward():
        for j, chip in enumerate(chips):
            for a in range(n):
                copy(a, 1 + j, (*chip, c), me).wait_recv()
                passed(a)[j].start()

    def finish():
        for a in range(n):
            copy(a, 0, sibling, me).wait_recv()
            for j, chip in enumerate(chips):
                copy(a, 4 + j, (*chip, 1 - c), me).wait_recv()
        for a in range(n):
            for cp in first(a) + passed(a):
                cp.wait_send()
            local(a).wait()

    return start, forward, finish


def _exchange_ops(ins, outs, send_sems, recv_sems):
    n = len(ins)

    def copy(a, k):
        dev, pidx = _peer(k)
        return pltpu.make_async_remote_copy(ins[a].at[pidx], outs[a].at[k], send_sems.at[a * 7 + k],
                                            recv_sems.at[a * 7 + k], device_id=dev, device_id_type=MESH)

    def start():
        for k in range(NDEV - 1):
            for a in range(n):
                copy(a, k).start()

    def finish():
        for k in range(NDEV - 1):
            for a in range(n):
                copy(a, k).wait_recv()
        for k in range(NDEV - 1):
            for a in range(n):
                copy(a, k).wait_send()

    return start, finish


def _gathered(shards):
    return tuple(jax.ShapeDtypeStruct((NDEV,) + s.shape, s.dtype) for s in shards)


def _exchanged(blocks):
    return tuple(jax.ShapeDtypeStruct((NDEV - 1,) + b.shape[1:], b.dtype) for b in blocks)


def _owner_copy(src_ref, land_ref, send_sems, recv_sems, gi, n):
    x, y, c = _me()
    k = jnp.bitwise_xor(4 * x + 2 * y + c, n) - 1
    return pltpu.make_async_remote_copy(src_ref.at[gi], land_ref.at[k], send_sems.at[gi], recv_sems.at[k],
                                        device_id=(n >> 2, (n >> 1) & 1, n & 1), device_id_type=MESH)


def exchange_start(blocks, owners, name):
    landing = lax.empty((NDEV - 1,) + blocks.shape[1:], blocks.dtype)
    hbm = pl.BlockSpec(memory_space=pltpu.HBM)
    sem = pl.BlockSpec(memory_space=pltpu.SEMAPHORE)

    def body(src_ref, land_ref, send_sems, recv_sems, src_thru, land_thru, token):
        x, y, c = _me()
        for gi, n in enumerate(owners):
            @pl.when(4 * x + 2 * y + c != n)
            def _(gi=gi, n=n):
                _owner_copy(src_ref, land_ref, send_sems, recv_sems, gi, n).start()
        token[...] = jnp.zeros_like(token)

    return pl.pallas_call(
        body, name=name,
        out_shape=(pltpu.SemaphoreType.DMA((len(owners),)), pltpu.SemaphoreType.DMA((NDEV - 1,)),
                   pltpu.HBM(blocks.shape, blocks.dtype), pltpu.HBM(landing.shape, landing.dtype),
                   jax.ShapeDtypeStruct((8, 128), F32)),
        in_specs=(hbm, hbm), out_specs=(sem, sem, hbm, hbm, VSPEC), input_output_aliases={0: 2, 1: 3},
        compiler_params=pltpu.CompilerParams(has_side_effects=pltpu.SideEffectType.DATAFLOW_SIDE_EFFECTING),
    )(pltpu.with_memory_space_constraint(blocks, pltpu.HBM), pltpu.with_memory_space_constraint(landing, pltpu.HBM))


def exchange_wait(send_sems, recv_sems, src_thru, land_thru, after, owners, name):
    hbm = pl.BlockSpec(memory_space=pltpu.HBM)
    sem = pl.BlockSpec(memory_space=pltpu.SEMAPHORE)

    def body(src_ref, land_ref, send_sems, recv_sems, after_ref, src_dead, got_ref):
        x, y, c = _me()
        me = 4 * x + 2 * y + c
        for gi, n in enumerate(owners):
            @pl.when(me != n)
            def _(gi=gi, n=n):
                _owner_copy(src_ref, land_ref, send_sems, recv_sems, gi, n).wait_send()

            @pl.when(me == n)
            def _(gi=gi):
                for k in range(NDEV - 1):
                    pltpu.make_async_remote_copy(src_ref.at[gi], land_ref.at[k], send_sems.at[gi], recv_sems.at[k],
                                                 device_id=_peer(k)[0], device_id_type=MESH).wait_recv()

    return pl.pallas_call(
        body, name=name,
        out_shape=(pltpu.HBM(src_thru.shape, src_thru.dtype), pltpu.HBM(land_thru.shape, land_thru.dtype)),
        in_specs=(hbm, hbm, sem, sem, ANYSPEC), out_specs=(hbm, hbm), input_output_aliases={0: 0, 1: 1},
        compiler_params=pltpu.CompilerParams(has_side_effects=pltpu.SideEffectType.DATAFLOW_SIDE_EFFECTING),
    )(src_thru, land_thru, send_sems, recv_sems, after)[1]


def gather_and_sum_rows(parts, keep_rows):
    n = len(parts)
    offs = [sum(p.shape[0] for p in parts[:i]) for i in range(n)]
    rows = sum(p.shape[0] for p in parts)

    def body(*refs):
        sum_ref, keep_ref, mine_ref, buf_ref, send_sems, recv_sems, local_sems = refs[n:]
        for i in range(n):
            mine_ref[offs[i]:offs[i] + parts[i].shape[0], :] = refs[i][...]
        start, forward, finish = _gather2_ops([mine_ref], [buf_ref], send_sems, recv_sems, local_sems)
        start()
        forward()
        finish()
        total = buf_ref[0]
        for r in range(1, NDEV):
            total = total + buf_ref[r]
        sum_ref[...] = total
        keep_ref[...] = jnp.zeros_like(keep_ref)
        for r in range(NDEV):
            for j, src in enumerate(keep_rows):
                keep_ref[r, j:j + 1, :] = buf_ref[r, src:src + 1, :]

    return pl.pallas_call(
        body, name="gather_and_sum_rows",
        out_shape=(jax.ShapeDtypeStruct((rows, D), F32), jax.ShapeDtypeStruct((NDEV, 8, D), F32)),
        in_specs=[VSPEC] * n, out_specs=(VSPEC, VSPEC),
        scratch_shapes=[pltpu.VMEM((rows, D), F32), pltpu.VMEM((NDEV, rows, D), F32)] + _comm_sems(1, True),
        compiler_params=_params(),
    )(*parts)


def _gather_chips():
    x, y, c = _me()
    north = c == 1
    first = (jnp.where(north, 1 - x, x), jnp.where(north, y, 1 - y))
    second = (jnp.where(north, x, 1 - x), jnp.where(north, 1 - y, y))
    return [first, second, (1 - x, 1 - y)]


def _arrival_order():
    x, y, c = _me()
    first, second, far = _gather_chips()
    devs = [(x, y, c), (x, y, 1 - c), (*first, c), (*second, 1 - c), (*second, c), (*first, 1 - c), (*far, c),
            (*far, 1 - c)]
    return jnp.stack([4 * d[0] + 2 * d[1] + d[2] for d in devs]).astype(jnp.int32)


def in_proj(x, gain, sc, sh, w_shard, b_in):
    t = x.shape[0]
    tm = min(1024, t // 2)
    nm = t // tm
    order = _arrival_order()

    def body(order_ref, x_ref, g_ref, sc_ref, sh_ref, b_ref, wsh_ref, p_ref, h_ref, wall_ref,
             wbuf, h_all, send_sems, recv_sems, local_sems):
        k, m = pl.program_id(0), pl.program_id(1)
        mx, my, mc = _me()
        me, sibling = (mx, my, mc), (mx, my, 1 - mc)
        chips = _gather_chips()

        def slot(d):
            return 4 * d[0] + 2 * d[1] + d[2]

        def copy(sem, block, to, src=None):
            return pltpu.make_async_remote_copy(
                src_ref=wall_ref.at[slot(block)] if src is None else src, dst_ref=wall_ref.at[slot(block)],
                send_sem=send_sems.at[sem], recv_sem=recv_sems.at[sem], device_id=to, device_id_type=MESH)

        own = pltpu.make_async_copy(wsh_ref, wall_ref.at[slot(me)], local_sems.at[0])
        first = [copy(0, me, sibling, src=wsh_ref)] + [copy(1 + j, me, (*q, mc), src=wsh_ref)
                                                       for j, q in enumerate(chips)]
        passed = [copy(4 + j, (*q, mc), sibling) for j, q in enumerate(chips)]

        def load(kk):
            src = wsh_ref if kk == 0 else wall_ref.at[order_ref[kk]]
            return pltpu.make_async_copy(src, wbuf.at[kk % 2], local_sems.at[1 + kk % 2])

        def arrived(kk):
            if kk == 1:
                copy(0, sibling, me).wait_recv()
            elif kk in (2, 4, 6):
                j = kk // 2 - 1
                copy(1 + j, (*chips[j], mc), me).wait_recv()
                passed[j].start()
                if j < 2:
                    first[2 + j].start()
            else:
                j = {3: 1, 5: 0, 7: 2}[kk]
                copy(4 + {1: 0, 0: 1, 2: 2}[j], (*chips[j], 1 - mc), me).wait_recv()

        @pl.when((k == 0) & (m == 0))
        def _():
            own.start()
            for cp in first[:2]:
                cp.start()
            load(0).start()
            load(0).wait()

        for kk in range(1, NDEV):
            @pl.when((k == kk - 1) & (m == nm // 2))
            def _(kk=kk):
                arrived(kk)
                load(kk).start()

            @pl.when((k == kk) & (m == 0))
            def _(kk=kk):
                load(kk).wait()

        rows = pl.ds(pl.multiple_of(m * tm, tm), tm)

        @pl.when(k == 0)
        def _():
            h_all[rows, :] = f_modulate(x_ref[...], g_ref[...], sc_ref[...], sh_ref[...]).astype(BF16)

        p_ref[...] = _dot(h_all[rows, :], wbuf[k % 2], _NN) + b_ref[...]

        @pl.when((k == NDEV - 1) & (m == nm - 1))
        def _():
            for cp in first + passed:
                cp.wait_send()
            own.wait()
            out = pltpu.make_async_copy(h_all, h_ref, local_sems.at[0])
            out.start()
            out.wait()

    vec = pl.BlockSpec((1, D), lambda k, m, o: (0, 0))
    return pl.pallas_call(
        body, name="in_proj",
        out_shape=(jax.ShapeDtypeStruct((t, NDEV * D), F32), jax.ShapeDtypeStruct((t, D), BF16),
                   jax.ShapeDtypeStruct((NDEV, D, D), BF16)),
        grid_spec=pltpu.PrefetchScalarGridSpec(
            num_scalar_prefetch=1, grid=(NDEV, nm),
            in_specs=[pl.BlockSpec((tm, D), lambda k, m, o: (jnp.where(k == 0, m, 0), 0)), vec, vec, vec,
                      pl.BlockSpec((1, D), lambda k, m, o: (0, o[k])), ANYSPEC],
            out_specs=(pl.BlockSpec((tm, D), lambda k, m, o: (m, o[k])), ANYSPEC, ANYSPEC),
            scratch_shapes=[pltpu.VMEM((2, D, D), BF16), pltpu.VMEM((t, D), BF16), pltpu.SemaphoreType.DMA((7,)),
                            pltpu.SemaphoreType.DMA((7,)), pltpu.SemaphoreType.DMA((3,))]),
        compiler_params=_params("arbitrary", "arbitrary"),
    )(order, x, gain, sc, sh, b_in, w_shard)


def hgrn_local(p, logits, shards):
    t = p.shape[0]
    nc = t // CH
    n = len(shards)

    def body(*refs):
        q_ref, f_ref, v_ref, l_ref = refs[:4]
        intra_ref, qe_ref, st_ref, dec_ref = refs[4 + n:8 + n]
        state = refs[8 + 2 * n]
        start, forward, finish = _gather2_ops(refs[4:4 + n], refs[8 + n:8 + 2 * n], *refs[9 + 2 * n:])
        step = pl.program_id(0)

        @pl.when(step == 0)
        def _():
            start()
            state[...] = jnp.zeros_like(state)

        intra, qe, ut, dec = f_hgrn_chunk(q_ref[...], f_ref[...], v_ref[...], l_ref[...])
        intra_ref[...] = intra
        qe_ref[...] = qe.astype(BF16)
        dec_ref[...] = dec
        s = state[...]
        st_ref[...] = s
        state[...] = dec * s + ut
        pl.when(step == (3 * nc) // 4)(forward)
        pl.when(step == nc - 1)(finish)

    col = lambda j: pl.BlockSpec((CH, D), lambda c: (c, j))
    res = pl.pallas_call(
        body, name="hgrn_local", grid=(nc,),
        out_shape=(jax.ShapeDtypeStruct((t, D), F32), jax.ShapeDtypeStruct((t, D), BF16),
                   jax.ShapeDtypeStruct((nc, DK, D), F32), jax.ShapeDtypeStruct((nc, 1, D), F32)) + _gathered(shards),
        in_specs=[col(0), col(1), col(2), pl.BlockSpec((2, D), lambda c: (0, 0))] + [ANYSPEC] * n,
        out_specs=(pl.BlockSpec((CH, D), lambda c: (c, 0)), pl.BlockSpec((CH, D), lambda c: (c, 0)),
                   pl.BlockSpec((None, DK, D), lambda c: (c, 0, 0)), pl.BlockSpec((None, 1, D), lambda c: (c, 0, 0)))
        + (ANYSPEC,) * n,
        scratch_shapes=[pltpu.VMEM((DK, D), F32)] + _comm_sems(n, True),
        compiler_params=_params("arbitrary"),
    )(p, p, p, logits, *shards)
    return res[:4] + (res[4:],)


def _conv_tile(t):
    return min(256, t)


def conv_forward(p, dw, db, ln_g, ln_b, shards):
    t = p.shape[0]
    tm = _conv_tile(t)
    per = tm // HALO
    n = len(shards)
    nt = t // tm

    def body(*refs):
        cv_ref, cg_ref, cvp_ref, cgp_ref, dw_ref, db_ref, g_ref, b_ref = refs[:8]
        yc_ref, act_ref = refs[8 + n:10 + n]
        uext, ush = refs[10 + 2 * n:12 + 2 * n]
        start, forward, finish = _gather2_ops(refs[8:8 + n], refs[10 + n:10 + 2 * n], *refs[12 + 2 * n:])
        step = pl.program_id(0)
        pl.when(step == 0)(start)
        pl.when(step == nt - 1)(forward)
        uext[0:HALO, :] = jnp.where(step == 0, 0.0, f_glu(cvp_ref[...], cgp_ref[...]))
        uext[HALO:HALO + tm, :] = f_glu(cv_ref[...], cg_ref[...])
        _shift_stack(uext, ush, tm + HALO)
        acc = jnp.zeros((tm, D), F32) + db_ref[...]
        for w in range(KW):
            acc = acc + dw_ref[w:w + 1, :] * _shifted(ush, HALO - KW + 1 + w, tm)
        yc_ref[...] = acc
        act_ref[...] = f_conv_act(acc, g_ref[...], b_ref[...]).astype(BF16)
        pl.when(step == nt - 1)(finish)

    vec = pl.BlockSpec((1, D), lambda m: (0, 0))
    prev = lambda j: pl.BlockSpec((HALO, D), lambda m: (jnp.maximum(m * per - 1, 0), j))
    res = pl.pallas_call(
        body, name="conv_forward", grid=(nt,),
        out_shape=(jax.ShapeDtypeStruct((t, D), F32), jax.ShapeDtypeStruct((t, D), BF16)) + _gathered(shards),
        in_specs=[pl.BlockSpec((tm, D), lambda m: (m, 4)), pl.BlockSpec((tm, D), lambda m: (m, 5)), prev(4), prev(5),
                  pl.BlockSpec((HALO, D), lambda m: (0, 0)), vec, vec, vec] + [ANYSPEC] * n,
        out_specs=(pl.BlockSpec((tm, D), lambda m: (m, 0)), pl.BlockSpec((tm, D), lambda m: (m, 0))) + (ANYSPEC,) * n,
        scratch_shapes=[pltpu.VMEM((HALO + tm, D), F32), pltpu.VMEM((8, tm + HALO, D), F32)] + _comm_sems(n, True),
        compiler_params=_params("arbitrary"),
    )(p, p, p, p, dw, db, ln_g, ln_b, *shards)
    return res[0], res[1], res[2:]


def _mix_forward_tile(x_ref, og_ref, ga_ref, gb_ref, qe_ref, intra_ref, st_ref, act_ref, wa_ref, wb_ref, wo_ref,
                      hg_ref, rows):
    o = _inter(qe_ref[...], st_ref, rows) + intra_ref[...]
    a = f_head_out(o, og_ref[...], hg_ref[...])
    ya = _dot(a.astype(BF16), wa_ref[...], _NN)
    yb = _dot(act_ref[...], wb_ref[...], _NN)
    merged = f_merge(ga_ref[...], gb_ref[...], ya, yb)
    y = _dot(merged.astype(BF16), wo_ref[...], _NN)
    return o, a, ya, yb, merged, y


def _mix_specs(tm, tiles=None):
    at = (lambda m: m) if tiles is None else (lambda m: tiles - 1 - m)
    col = lambda j: pl.BlockSpec((tm, D), lambda m: (at(m), j))
    row = pl.BlockSpec((tm, D), lambda m: (at(m), 0))
    per_chunk = lambda rows: pl.BlockSpec((tm // CH, rows, D), lambda m: (at(m), 0, 0))
    return col, row, pl.BlockSpec((1, D), lambda m: (0, 0)), per_chunk


def mix_forward(x, p, qe, intra, st_prev, act, wa, wb, wo, hg, g1, post):
    t = x.shape[0]
    tm = min(512, t)

    def body(x_ref, og_ref, ga_ref, gb_ref, qe_ref, intra_ref, st_ref, act_ref, wa_ref, wb_ref, wo_ref, hg_ref,
             g1_ref, post_ref, x1_ref):
        y = _mix_forward_tile(x_ref, og_ref, ga_ref, gb_ref, qe_ref, intra_ref, st_ref, act_ref, wa_ref, wb_ref,
                              wo_ref, hg_ref, tm)[-1]
        x1_ref[...] = f_residual(x_ref[...], y, g1_ref[...], post_ref[...])

    col, row, vec, per_chunk = _mix_specs(tm)
    return pl.pallas_call(
        body, name="mix_forward", grid=(t // tm,),
        out_shape=jax.ShapeDtypeStruct((t, D), F32),
        in_specs=[row, col(3), col(6), col(7), row, row, per_chunk(DK), row, VSPEC, VSPEC, VSPEC, vec, vec, vec],
        out_specs=row,
        compiler_params=_params("parallel"),
    )(x, p, p, p, qe, intra, st_prev, act, wa, wb, wo, hg, g1, post)


def ffn_forward_backward(x1, target, w1, w2, pre, sc, sh, g2, post):
    t = x1.shape[0]
    tm = min(512, t)
    nb = w1.shape[0]
    fb = w1.shape[2]

    def body(x_ref, tg_ref, w1_ref, w2_ref, pre_ref, sc_ref, sh_ref, g2_ref, post_ref,
             dx_ref, h2_ref, r_ref, dz_ref, dy2_ref, acc_ref, z_sc):
        @pl.when(pl.program_id(0) == 0)
        def _():
            acc_ref[...] = jnp.zeros_like(acc_ref)

        x1v = x_ref[...]
        h2, vjp_h = jax.vjp(f_modulate, x1v, pre_ref[...], sc_ref[...], sh_ref[...])
        h2b = h2.astype(BF16)
        h2_ref[...] = h2b
        y2 = jnp.zeros((tm, D), F32)
        for n in range(nb):
            z = _dot(h2b, w1_ref[n], _NN)
            z_sc[:, n * fb:(n + 1) * fb] = z
            r = jnp.square(jnp.maximum(z, 0.0)).astype(BF16)
            r_ref[:, n * fb:(n + 1) * fb] = r
            y2 = y2 + _dot(r, w2_ref[n], _NN)
        out, vjp_r = jax.vjp(f_residual, x1v, y2, g2_ref[...], post_ref[...])
        err = out - tg_ref[...]
        tok = jnp.mean(jnp.square(err), axis=-1, keepdims=True)
        acc_ref[5:6, :] += 0.5 * jnp.sum(tok, axis=0, keepdims=True)
        dx_a, dy2, dg2, dpost = vjp_r(err * (1.0 / D))
        dy2b = dy2.astype(BF16)
        dy2_ref[...] = dy2b
        dh2 = jnp.zeros((tm, D), F32)
        for n in range(nb):
            dr = _dot(dy2b, w2_ref[n], _NT)
            dz = (dr * (2.0 * jnp.maximum(z_sc[:, n * fb:(n + 1) * fb], 0.0))).astype(BF16)
            dz_ref[:, n * fb:(n + 1) * fb] = dz
            dh2 = dh2 + _dot(dz, w1_ref[n], _NT)
        dx_b, dpre, dsc, dsh = vjp_h(dh2)
        dx_ref[...] = dx_a + dx_b
        acc_ref[0:1, :] += dpre
        acc_ref[1:2, :] += dpost
        acc_ref[2:3, :] += dsc
        acc_ref[3:4, :] += dsh
        acc_ref[4:5, :] += dg2

    row = pl.BlockSpec((tm, D), lambda m: (m, 0))
    wide = pl.BlockSpec((tm, DFF), lambda m: (m, 0), pipeline_mode=pl.Buffered(1))
    vec = pl.BlockSpec((1, D), lambda m: (0, 0))
    return pl.pallas_call(
        body, name="ffn_forward_backward", grid=(t // tm,),
        out_shape=(jax.ShapeDtypeStruct((t, D), F32), jax.ShapeDtypeStruct((t, D), BF16),
                   jax.ShapeDtypeStruct((t, DFF), BF16), jax.ShapeDtypeStruct((t, DFF), BF16),
                   jax.ShapeDtypeStruct((t, D), BF16), jax.ShapeDtypeStruct((8, D), F32)),
        in_specs=[row, row, VSPEC, VSPEC, vec, vec, vec, vec, vec],
        out_specs=(row, row, wide, wide, row, pl.BlockSpec((8, D), lambda m: (0, 0))),
        scratch_shapes=[pltpu.VMEM((tm, DFF), F32)],
        compiler_params=_params("arbitrary"),
    )(x1, target, w1, w2, pre, sc, sh, g2, post)


def mix_backward(x, p, qe, intra, st_prev, act, wa, wb, wo, hg, g1, post, dx1, dec):
    t = x.shape[0]
    tm = min(256, t)
    nc = t // CH

    def body(x_ref, og_ref, ga_ref, gb_ref, qe_ref, intra_ref, st_ref, act_ref, wa_ref, wb_ref, wo_ref, hg_ref,
             g1_ref, post_ref, dx1_ref, dec_ref,
             dp_ref, do_ref, dqe_ref, gt_ref, dact_ref, a_ref, mg_ref, dya_ref, dyb_ref, dy_ref, acc_ref, gstate):
        @pl.when(pl.program_id(0) == 0)
        def _():
            acc_ref[...] = jnp.zeros_like(acc_ref)
            gstate[...] = jnp.zeros_like(gstate)

        o, a, ya, yb, merged, y = _mix_forward_tile(x_ref, og_ref, ga_ref, gb_ref, qe_ref, intra_ref, st_ref, act_ref,
                                                    wa_ref, wb_ref, wo_ref, hg_ref, tm)
        a_ref[...] = a.astype(BF16)
        mg_ref[...] = merged.astype(BF16)
        _, vjp_r = jax.vjp(f_residual, x_ref[...], y, g1_ref[...], post_ref[...])
        _, dy, dg1, dpost = vjp_r(dx1_ref[...])
        dyb16 = dy.astype(BF16)
        dy_ref[...] = dyb16
        dmerged = _dot(dyb16, wo_ref[...], _NT)
        _, vjp_m = jax.vjp(f_merge, ga_ref[...], gb_ref[...], ya, yb)
        dga, dgb, dya, dyb = vjp_m(dmerged)
        dp_ref[:, D:2 * D] = dga.astype(BF16)
        dp_ref[:, 2 * D:3 * D] = dgb.astype(BF16)
        dya16, dyb16b = dya.astype(BF16), dyb.astype(BF16)
        dya_ref[...] = dya16
        dyb_ref[...] = dyb16b
        da = _dot(dya16, wa_ref[...], _NT)
        dact_ref[...] = _dot(dyb16b, wb_ref[...], _NT)
        _, vjp_a = jax.vjp(f_head_out, o, og_ref[...], hg_ref[...])
        do, dog, dhg = vjp_a(da)
        dp_ref[:, 0:D] = dog.astype(BF16)
        do_ref[...] = do
        do16 = do.astype(BF16)
        qe16 = qe_ref[...]
        for ci in reversed(range(tm // CH)):
            st = st_ref[ci].astype(BF16)
            rows = slice(ci * CH, (ci + 1) * CH)
            dqe, vt = [], []
            for h in range(HEADS):
                sl = slice(h * DK, (h + 1) * DK)
                dqe.append(_dot(do16[rows, sl], st[:, sl], _NN))
                vt.append(_dot(do16[rows, sl], qe16[rows, sl], _TN))
            dqe_ref[rows, :] = jnp.concatenate(dqe, axis=1)
            g = gstate[...]
            gt_ref[ci] = g
            gstate[...] = dec_ref[ci] * g + jnp.concatenate(vt, axis=1)
        acc_ref[0:1, :] += dg1
        acc_ref[1:2, :] += dpost
        acc_ref[2:3, :] += dhg
        acc_ref[3:4, :] += jnp.sum(dog, axis=0, keepdims=True)
        acc_ref[4:5, :] += jnp.sum(dga, axis=0, keepdims=True)
        acc_ref[5:6, :] += jnp.sum(dgb, axis=0, keepdims=True)

    nt = t // tm
    col, row, vec, per_chunk = _mix_specs(tm, nt)
    b16 = jax.ShapeDtypeStruct((t, D), BF16)
    f32 = jax.ShapeDtypeStruct((t, D), F32)
    return pl.pallas_call(
        body, name="mix_backward", grid=(nt,),
        out_shape=(jax.ShapeDtypeStruct((t, NDEV * D), BF16), f32, f32, jax.ShapeDtypeStruct((nc, DK, D), F32), f32,
                   b16, b16, b16, b16, b16, jax.ShapeDtypeStruct((8, D), F32)),
        in_specs=[row, col(3), col(6), col(7), row, row, per_chunk(DK), row, VSPEC, VSPEC, VSPEC, vec, vec, vec, row,
                  per_chunk(1)],
        out_specs=(pl.BlockSpec((tm, 3 * D), lambda m: (nt - 1 - m, 0)), row, row, per_chunk(DK), row, row, row, row,
                   row, row, pl.BlockSpec((8, D), lambda m: (0, 0))),
        scratch_shapes=[pltpu.VMEM((DK, D), F32)],
        compiler_params=_params("arbitrary"),
    )(x, p, p, p, qe, intra, st_prev, act, wa, wb, wo, hg, g1, post, dx1, dec)


def conv_backward(p, yc, dact, dw, ln_g, ln_b, blocks, dp):
    t = p.shape[0]
    tm = _conv_tile(t)
    per = tm // HALO
    nt = t // tm
    n = len(blocks)

    def body(*refs):
        cv_ref, cg_ref, cvp_ref, cgp_ref, yc_ref, ycn_ref, da_ref, dan_ref, dw_ref, g_ref, b_ref = refs[:11]
        dp_ref, acc_ref, ddw_ref = refs[12 + n:15 + n]
        uext, dyext, ush, dysh, ddw8, du_sc = refs[15 + 2 * n:21 + 2 * n]
        start, finish = _exchange_ops(refs[11:11 + n], refs[15 + n:15 + 2 * n], *refs[21 + 2 * n:])
        m = pl.program_id(0)

        @pl.when(m == 0)
        def _():
            start()
            acc_ref[...] = jnp.zeros_like(acc_ref)
            ddw8[...] = jnp.zeros_like(ddw8)

        cv, cg = cv_ref[...], cg_ref[...]
        u, vjp_u = jax.vjp(f_glu, cv, cg)
        uext[0:HALO, :] = jnp.where(m == 0, 0.0, f_glu(cvp_ref[...], cgp_ref[...]))
        uext[HALO:HALO + tm, :] = u
        _shift_stack(uext, ush, tm + HALO)
        _, vjp_c = jax.vjp(f_conv_act, yc_ref[...], g_ref[...], b_ref[...])
        dyc, dg, db = vjp_c(da_ref[...])
        _, vjp_n = jax.vjp(f_conv_act, ycn_ref[...], g_ref[...], b_ref[...])
        dyn = vjp_n(dan_ref[...])[0]
        dyext[0:tm, :] = dyc
        dyext[tm:tm + HALO, :] = jnp.where(m == nt - 1, 0.0, dyn)
        _shift_stack(dyext, dysh, tm + HALO)
        rb = min(128, tm)
        for lt in range(D // LANE):
            ls = slice(lt * LANE, (lt + 1) * LANE)
            for r0 in range(0, tm, rb):
                du_l = jnp.zeros((rb, LANE), F32)
                for w in range(KW):
                    du_l = du_l + dw_ref[w:w + 1, ls] * _shifted(dysh, KW - 1 - w + r0, rb, ls)
                du_sc[r0:r0 + rb, ls] = du_l
                dyc_l = dyext[r0:r0 + rb, ls]
                for w in range(KW):
                    prod = dyc_l * _shifted(ush, HALO - KW + 1 + w + r0, rb, ls)
                    part = jnp.sum(prod.reshape(4, rb // 32, 8, LANE), axis=1)
                    ddw8[w, :, ls] += (part[0] + part[1]) + (part[2] + part[3])
        du = du_sc[...]

        @pl.when(m == nt - 1)
        def _():
            ddw_ref[...] = jnp.sum(ddw8[...], axis=1)

        dcv, dcg = vjp_u(du)
        dp_ref[:, 0:D] = dcv.astype(BF16)
        dp_ref[:, D:2 * D] = dcg.astype(BF16)
        acc_ref[0:1, :] += jnp.sum(dyc, axis=0, keepdims=True)
        acc_ref[1:2, :] += dg
        acc_ref[2:3, :] += db
        acc_ref[3:4, :] += jnp.sum(dcv, axis=0, keepdims=True)
        acc_ref[4:5, :] += jnp.sum(dcg, axis=0, keepdims=True)
        pl.when(m == nt - 1)(finish)

    vec = pl.BlockSpec((1, D), lambda m: (0, 0))
    row = pl.BlockSpec((tm, D), lambda m: (m, 0))
    prev = lambda j: pl.BlockSpec((HALO, D), lambda m: (jnp.maximum(m * per - 1, 0), j))
    nxt = pl.BlockSpec((HALO, D), lambda m: (jnp.minimum((m + 1) * per, t // HALO - 1), 0))
    res = pl.pallas_call(
        body, name="conv_backward", grid=(nt,),
        out_shape=(jax.ShapeDtypeStruct(dp.shape, dp.dtype), jax.ShapeDtypeStruct((8, D), F32),
                   jax.ShapeDtypeStruct((HALO, D), F32)) + _exchanged(blocks),
        in_specs=[pl.BlockSpec((tm, D), lambda m: (m, 4)), pl.BlockSpec((tm, D), lambda m: (m, 5)), prev(4), prev(5),
                  row, nxt, row, nxt, pl.BlockSpec((HALO, D), lambda m: (0, 0)), vec, vec] + [ANYSPEC] * (n + 1),
        out_specs=(pl.BlockSpec((tm, 2 * D), lambda m: (m, 3)), pl.BlockSpec((8, D), lambda m: (0, 0)),
                   pl.BlockSpec((HALO, D), lambda m: (0, 0))) + (ANYSPEC,) * n,
        scratch_shapes=[pltpu.VMEM((HALO + tm, D), F32), pltpu.VMEM((tm + HALO, D), F32),
                        pltpu.VMEM((8, tm + HALO, D), F32), pltpu.VMEM((8, tm + HALO, D), F32),
                        pltpu.VMEM((HALO, 8, D), F32), pltpu.VMEM((tm, D), F32)] + _comm_sems(n, False),
        input_output_aliases={11 + n: 0},
        compiler_params=_params("arbitrary"),
    )(p, p, p, p, yc, yc, dact, dact, dw, ln_g, ln_b, *blocks, dp)
    return res[:3] + (res[3:],)


def hgrn_backward(p, logits, do, dqe, gt, st_prev, blocks, dp):
    t = p.shape[0]
    nc = t // CH
    n = len(blocks)

    def body(*refs):
        q_ref, f_ref, v_ref, l_ref, do_ref, dqe_ref, gt_ref, st_ref = refs[:8]
        dp_ref, acc_ref = refs[9 + n:11 + n]
        start, finish = _exchange_ops(refs[8:8 + n], refs[11 + n:11 + 2 * n], *refs[11 + 2 * n:])

        @pl.when(pl.program_id(0) == 0)
        def _():
            start()
            acc_ref[...] = jnp.zeros_like(acc_ref)

        gt_v = gt_ref[...]
        ddec = jnp.sum(gt_v * st_ref[...], axis=0, keepdims=True)
        _, vjp = jax.vjp(f_hgrn_chunk, q_ref[...], f_ref[...], v_ref[...], l_ref[...])
        dq, df, dv, dl = vjp((do_ref[...], dqe_ref[...], gt_v, ddec))
        dp_ref[:, 0:D] = dq.astype(BF16)
        dp_ref[:, D:2 * D] = df.astype(BF16)
        dp_ref[:, 2 * D:3 * D] = dv.astype(BF16)
        acc_ref[0:2, :] += dl
        acc_ref[2:3, :] += jnp.sum(dq, axis=0, keepdims=True)
        acc_ref[3:4, :] += jnp.sum(df, axis=0, keepdims=True)
        acc_ref[4:5, :] += jnp.sum(dv, axis=0, keepdims=True)
        pl.when(pl.program_id(0) == nc - 1)(finish)

    col = lambda j: pl.BlockSpec((CH, D), lambda c: (c, j))
    row = pl.BlockSpec((CH, D), lambda c: (c, 0))
    stspec = pl.BlockSpec((None, DK, D), lambda c: (c, 0, 0))
    res = pl.pallas_call(
        body, name="hgrn_backward", grid=(nc,),
        out_shape=(jax.ShapeDtypeStruct(dp.shape, dp.dtype), jax.ShapeDtypeStruct((8, D), F32)) + _exchanged(blocks),
        in_specs=[col(0), col(1), col(2), pl.BlockSpec((2, D), lambda c: (0, 0)), row, row, stspec, stspec]
        + [ANYSPEC] * (n + 1),
        out_specs=(pl.BlockSpec((CH, 3 * D), lambda c: (c, 1)), pl.BlockSpec((8, D), lambda c: (0, 0)))
        + (ANYSPEC,) * n,
        scratch_shapes=_comm_sems(n, False),
        input_output_aliases={8 + n: 0},
        compiler_params=_params("arbitrary"),
    )(p, p, p, logits, do, dqe, gt, st_prev, *blocks, dp)
    return res[:2] + (res[2:],)


def in_proj_backward(dp, w_all, x, dx1, gain, sc, sh):
    t = x.shape[0]
    tm = min(512, t)

    def body(dp_ref, w_ref, x_ref, dx1_ref, g_ref, sc_ref, sh_ref, gx_ref, acc_ref):
        @pl.when(pl.program_id(0) == 0)
        def _():
            acc_ref[...] = jnp.zeros_like(acc_ref)

        dh = jnp.zeros((tm, D), F32)
        for j in range(NDEV):
            dh = dh + _dot(dp_ref[:, j * D:(j + 1) * D], w_ref[DP_SPLIT[j]], _NT)
        _, vjp_h = jax.vjp(f_modulate, x_ref[...], g_ref[...], sc_ref[...], sh_ref[...])
        dx, dg, dsc, dsh = vjp_h(dh)
        gx_ref[...] = dx1_ref[...] + dx
        acc_ref[0:1, :] += dg
        acc_ref[1:2, :] += dsc
        acc_ref[2:3, :] += dsh

    row = pl.BlockSpec((tm, D), lambda m: (m, 0))
    vec = pl.BlockSpec((1, D), lambda m: (0, 0))
    return pl.pallas_call(
        body, name="in_proj_backward", grid=(t // tm,),
        out_shape=(jax.ShapeDtypeStruct((t, D), F32), jax.ShapeDtypeStruct((8, D), F32)),
        in_specs=[pl.BlockSpec((tm, NDEV * D), lambda m: (m, 0)), VSPEC, row, row, vec, vec, vec],
        out_specs=(row, pl.BlockSpec((8, D), lambda m: (0, 0))),
        compiler_params=_params("arbitrary"),
    )(dp, w_all, x, dx1, gain, sc, sh)


def weight_grad(a, b, nblk, ka, bn, a_blocked, name, b_col=lambda n: n, after=None):
    t = a.shape[0]
    tk = min(2048, t)
    nk = t // tk
    extra = [] if after is None else [after]

    def body(*refs):
        a_ref, b_ref = refs[:2]
        f_ref, h_ref, acc = refs[2 + len(extra):]
        k = pl.program_id(1)

        @pl.when(k == 0)
        def _():
            acc[...] = jnp.zeros_like(acc)

        acc[...] += _dot(a_ref[...], b_ref[...], _TN)

        @pl.when(k == nk - 1)
        def _():
            f_ref[...] = acc[...]
            h_ref[...] = acc[...].astype(BF16)

    a_idx = (lambda n, k: (k, n)) if a_blocked else (lambda n, k: (k, 0))
    b_idx = (lambda n, k: (k, 0)) if a_blocked else (lambda n, k: (k, b_col(n)))
    out = pl.BlockSpec((None, ka, bn), lambda n, k: (n, 0, 0))
    return pl.pallas_call(
        body, name=name, grid=(nblk, nk),
        out_shape=(jax.ShapeDtypeStruct((nblk, ka, bn), F32), jax.ShapeDtypeStruct((nblk, ka, bn), BF16)),
        in_specs=[pl.BlockSpec((tk, ka), a_idx), pl.BlockSpec((tk, bn), b_idx)] + [ANYSPEC] * len(extra),
        out_specs=(out, out),
        scratch_shapes=[pltpu.VMEM((ka, bn), F32)],
        compiler_params=_params("parallel", "arbitrary"),
    )(a, b, *extra)


def ada_backward(call_t, dmod_cols, w, m, v):
    def body(c_ref, d_ref, w_ref, m_ref, v_ref, g_ref, dl_ref, nm_ref, nv_ref):
        ct, dm = c_ref[...], d_ref[...]
        g = ct[:, 0:1] * dm[0:1, :]
        for r in range(1, NDEV):
            g = g + ct[:, r:r + 1] * dm[r:r + 1, :]
        g_ref[...] = g
        dl_ref[...], nm_ref[...], nv_ref[...] = _adamw(w_ref[...], g, m_ref[...], v_ref[...])

    br = 256
    blk = pl.BlockSpec((br, w.shape[1]), lambda i: (i, 0))
    s = jax.ShapeDtypeStruct(w.shape, F32)
    return pl.pallas_call(
        body, name="ada_backward", grid=(w.shape[0] // br,), out_shape=(s, s, s, s),
        in_specs=[pl.BlockSpec((br, NDEV), lambda i: (i, 0)), pl.BlockSpec(dmod_cols.shape, lambda i: (0, 0)), blk,
                  blk, blk],
        out_specs=(blk, blk, blk, blk), compiler_params=_params("parallel"),
    )(call_t, dmod_cols, w, m, v)


def adamw_small(total, ddw_mine, recipes, ws, ms, vs):
    n = len(ws)

    def body(*refs):
        tot, ddw = refs[0], refs[1]
        w_refs, m_refs, v_refs = refs[2:2 + n], refs[2 + n:2 + 2 * n], refs[2 + 2 * n:2 + 3 * n]
        outs = refs[2 + 3 * n:2 + 7 * n]
        loss_ref = refs[2 + 7 * n]
        for i, rec in enumerate(recipes):
            if rec == "dw":
                g = ddw[...]
            elif isinstance(rec, tuple):
                g = tot[rec[0]:rec[1], :]
            else:
                g = jnp.concatenate([tot[r:r + 1, :] for r in rec], axis=1) if len(rec) > 1 else tot[rec[0]:rec[0] + 1, :]
            dl, nm, nv = _adamw(w_refs[i][...], g, m_refs[i][...], v_refs[i][...])
            outs[4 * i][...] = g
            outs[4 * i + 1][...] = dl
            outs[4 * i + 2][...] = nm
            outs[4 * i + 3][...] = nv
        loss_ref[...] = tot[LOSS_ROW:LOSS_ROW + 1, 0:128]

    shapes = []
    for w in ws:
        shapes += [jax.ShapeDtypeStruct(w.shape, F32)] * 4
    res = pl.pallas_call(
        body, name="adamw_small", out_shape=tuple(shapes) + (jax.ShapeDtypeStruct((1, 128), F32),),
        in_specs=[VSPEC] * (2 + 3 * n), out_specs=(VSPEC,) * (4 * n + 1), compiler_params=_params(),
    )(total, ddw_mine, *ws, *ms, *vs)
    return [res[4 * i:4 * i + 4] for i in range(n)], res[4 * n]


def reduce_and_adamw(sel, w, g_alls, g_recvs, m, v, name):
    r, c = w.shape
    br = min(256, r)
    npair = len(g_alls)

    def body(sel_ref, w_ref, *refs):
        go_refs, gr_refs = refs[:npair], refs[npair:2 * npair]
        m_ref, v_ref, g_ref, dl_ref, nm_ref, nv_ref = refs[2 * npair:]
        g = None
        for i in range(npair):
            gi = go_refs[i][...]
            for k in range(NDEV - 1):
                gi = gi + gr_refs[i][k].astype(F32)
            g = gi if g is None else jnp.where(sel_ref[1] == i, gi, g)
        g_ref[...] = g
        dl_ref[...], nm_ref[...], nv_ref[...] = _adamw(w_ref[...], g, m_ref[...], v_ref[...])

    blk = pl.BlockSpec((br, c), lambda i, sel_ref: (i, 0))
    own = pl.BlockSpec((None, br, c), lambda i, sel_ref: (sel_ref[0], i, 0))
    recv = pl.BlockSpec((NDEV - 1, br, c), lambda i, sel_ref: (0, i, 0))
    s = jax.ShapeDtypeStruct(w.shape, F32)
    return pl.pallas_call(
        body, name=name, out_shape=(s, s, s, s),
        grid_spec=pltpu.PrefetchScalarGridSpec(
            num_scalar_prefetch=1, grid=(r // br,),
            in_specs=[blk] + [own] * npair + [recv] * npair + [blk, blk],
            out_specs=(blk, blk, blk, blk)),
        compiler_params=_params("parallel"),
    )(sel, w, *g_alls, *g_recvs, m, v)


def kernel(x, c, w_ada, b_ada, pre_norm_tm, post_norm_tm, pre_norm_cm, post_norm_cm, w_in, b_in, hg_lb_logits, hg_norm, conv_dw, conv_db, conv_ln_g, conv_ln_b, w_br_a, w_br_b, w_out, w_ff1, w_ff2, loss_target, m_w_ada, m_b_ada, m_pre_norm_tm, m_post_norm_tm, m_pre_norm_cm, m_post_norm_cm, m_w_in, m_b_in, m_hg_lb_logits, m_hg_norm, m_conv_dw, m_conv_db, m_conv_ln_g, m_conv_ln_b, m_w_br_a, m_w_br_b, m_w_out, m_w_ff1, m_w_ff2, v_w_ada, v_b_ada, v_pre_norm_tm, v_post_norm_tm, v_pre_norm_cm, v_post_norm_cm, v_w_in, v_b_in, v_hg_lb_logits, v_hg_norm, v_conv_dw, v_conv_db, v_conv_ln_g, v_conv_ln_b, v_w_br_a, v_w_br_b, v_w_out, v_w_ff1, v_w_ff2):
    t = x.shape[1]
    me = 4 * lax.axis_index("x") + 2 * lax.axis_index("y") + lax.axis_index("c")
    xs = x[0]
    tgt = loss_target[0]

    mod, call, dw_all = ada_forward(c, w_ada[0], b_ada, conv_dw[0])
    sh1, sc1, g1, sh2, sc2, g2 = [mod[:, i * D:(i + 1) * D] for i in range(6)]
    dw = jnp.pad(dw_all.transpose(1, 0, 2).reshape(KW, D), ((0, HALO - KW), (0, 0)))

    p, h, win_all = in_proj(xs, pre_norm_tm, sc1, sh1, w_in[0].astype(BF16), b_in)
    yc, act, (w1_all,) = conv_forward(p, dw, conv_db, conv_ln_g, conv_ln_b, [w_ff1[0].astype(BF16)])
    intra, qe, st_prev, dec, (w2_all, wa_all, wb_all, wo_all) = hgrn_local(
        p, hg_lb_logits, [w_ff2[0].astype(BF16), w_br_a[0].astype(BF16), w_br_b[0].astype(BF16),
                          w_out[0].astype(BF16)])
    wa, wb, wo = wa_all.reshape(D, D), wb_all.reshape(D, D), wo_all.reshape(D, D)
    x1 = mix_forward(xs, p, qe, intra, st_prev, act, wa, wb, wo, hg_norm, g1, post_norm_tm)

    dx1, h2, r, dz, dy2, acc_ffn = ffn_forward_backward(x1, tgt, w1_all, w2_all, pre_norm_cm, sc2, sh2, g2,
                                                        post_norm_cm)
    rows = D // NDEV
    g1_f, g1_h = weight_grad(h2, dz, NDEV, D, DFF // NDEV, False, "grad_w_ff1")
    g2_f, g2_h = weight_grad(r, dy2, 4, D, D, True, "grad_w_ff2")
    g2_f, g2_h = g2_f.reshape(NDEV, DFF // NDEV, D), g2_h.reshape(NDEV, DFF // NDEV, D)

    (dp, do, dqe, gt, dact, a16, mg16, dya, dyb, dy, acc_mix) = mix_backward(
        xs, p, qe, intra, st_prev, act, wa, wb, wo, hg_norm, g1, post_norm_tm, dx1, dec)
    ga_f, ga_h = weight_grad(a16, dya, 1, D, D, False, "grad_w_br_a")
    gb_f, gb_h = weight_grad(act, dyb, 1, D, D, False, "grad_w_br_b")
    go_f, go_h = weight_grad(mg16, dy, 1, D, D, False, "grad_w_out")
    dp, acc_conv, ddw, (r_ff1, r_ff2) = conv_backward(p, yc, dact, dw, conv_ln_g, conv_ln_b, [g1_h, g2_h], dp)
    dp, acc_hg, (r_a, r_b, r_o) = hgrn_backward(
        p, hg_lb_logits, do, dqe, gt, st_prev,
        [ga_h.reshape(NDEV, rows, D), gb_h.reshape(NDEV, rows, D), go_h.reshape(NDEV, rows, D)], dp)
    dp_col = lambda n: jnp.where(n < 3, n + 3, jnp.where(n == 3, 0, jnp.where(n < 6, n + 2, n - 5)))
    even, odd = [0, 2, 4, 6], [1, 3, 5, 7]
    gin_f0, gin_h0 = weight_grad(h, dp, 4, D, D, False, "grad_w_in_even", b_col=lambda j: dp_col(2 * j))
    send0, recv0, thru0, landing0, token0 = exchange_start(gin_h0, even, "exchange_start_even")
    gin_f1, gin_h1 = weight_grad(h, dp, 4, D, D, False, "grad_w_in_odd", b_col=lambda j: dp_col(2 * j + 1),
                                 after=token0)
    send1, recv1, thru1, landing1, token1 = exchange_start(gin_h1, odd, "exchange_start_odd")
    grad_x, acc_in = in_proj_backward(dp, win_all, xs, dx1, pre_norm_tm, sc1, sh1 + token1[0:1, 0:1])

    own = [ga_f.reshape(NDEV, rows, D), gb_f.reshape(NDEV, rows, D), go_f.reshape(NDEV, rows, D), g1_f, g2_f]
    recv = [r_a, r_b, r_o, r_ff1, r_ff2]
    big = {}
    names = ["w_br_a", "w_br_b", "w_out", "w_ff1", "w_ff2"]
    ws = [w_br_a, w_br_b, w_out, w_ff1, w_ff2]
    ms = [m_w_br_a, m_w_br_b, m_w_out, m_w_ff1, m_w_ff2]
    vs = [v_w_br_a, v_w_br_b, v_w_out, v_w_ff1, v_w_ff2]
    sel = jnp.stack([me, 0]).astype(jnp.int32)
    for i, nm in enumerate(names):
        big[nm] = [o[None] for o in reduce_and_adamw(sel, ws[i][0], [own[i]], [recv[i]], ms[i][0], vs[i][0],
                                                     "adamw_" + nm)]
    big, grad_x, acc_in = lax.optimization_barrier((big, grad_x, acc_in))
    r_in0 = exchange_wait(send0, recv0, thru0, landing0, acc_in, even, "exchange_wait_even")
    r_in1 = exchange_wait(send1, recv1, thru1, landing1, acc_in, odd, "exchange_wait_odd")
    r_in0, r_in1, acc_in = lax.optimization_barrier((r_in0, r_in1, acc_in))
    sel_in = jnp.stack([me // 2, me % 2]).astype(jnp.int32)
    big["w_in"] = [o[None] for o in reduce_and_adamw(sel_in, w_in[0], [gin_f0, gin_f1], [r_in0, r_in1], m_w_in[0],
                                                     v_w_in[0], "adamw_w_in")]

    dmod_rows = [2, 1, 8, 19, 18, 20]
    total, kept = gather_and_sum_rows([acc_in, acc_mix, acc_ffn, acc_hg, acc_conv, ddw], dmod_rows)
    dmod_all = kept[:, 0:6, :].reshape(NDEV, 6 * D)
    wcols = w_ada.shape[2]
    gwa, dwa, nmwa, nvwa = ada_backward(call.T, lax.dynamic_slice_in_dim(dmod_all, me * wcols, wcols, axis=1),
                                        w_ada[0], m_w_ada[0], v_w_ada[0])
    ddw_mine = lax.dynamic_slice_in_dim(total[40:40 + KW], me * (D // NDEV), D // NDEV, axis=1)

    small_names = ["b_ada", "pre_norm_tm", "post_norm_tm", "pre_norm_cm", "post_norm_cm", "b_in", "hg_lb_logits",
                   "hg_norm", "conv_db", "conv_ln_g", "conv_ln_b", "conv_dw"]
    recipes = [dmod_rows, [0], [9], [16], [17], [26, 27, 28, 11, 35, 36, 12, 13], (24, 26), [10], [32], [33], [34],
               "dw"]
    small_w = [b_ada, pre_norm_tm, post_norm_tm, pre_norm_cm, post_norm_cm, b_in, hg_lb_logits, hg_norm, conv_db,
               conv_ln_g, conv_ln_b, conv_dw[0]]
    small_m = [m_b_ada, m_pre_norm_tm, m_post_norm_tm, m_pre_norm_cm, m_post_norm_cm, m_b_in, m_hg_lb_logits,
               m_hg_norm, m_conv_db, m_conv_ln_g, m_conv_ln_b, m_conv_dw[0]]
    small_v = [v_b_ada, v_pre_norm_tm, v_post_norm_tm, v_pre_norm_cm, v_post_norm_cm, v_b_in, v_hg_lb_logits,
               v_hg_norm, v_conv_db, v_conv_ln_g, v_conv_ln_b, v_conv_dw[0]]
    small_out, loss_row = adamw_small(total, ddw_mine, recipes, small_w, small_m, small_v)
    loss = loss_row[0, 0]
    sm = {nm: list(o) for nm, o in zip(small_names, small_out)}
    sm["conv_dw"] = [o[None] for o in sm["conv_dw"]]

    order = ["w_ada", "b_ada", "pre_norm_tm", "post_norm_tm", "pre_norm_cm", "post_norm_cm", "w_in", "b_in",
             "hg_lb_logits", "hg_norm", "conv_dw", "conv_db", "conv_ln_g", "conv_ln_b", "w_br_a", "w_br_b", "w_out",
             "w_ff1", "w_ff2"]
    res = dict(sm)
    res.update(big)
    res["w_ada"] = [gwa[None], dwa[None], nmwa[None], nvwa[None]]
    outs = [loss, grad_x[None]]
    for j in range(4):
        outs += [res[nm][j] for nm in order]
    return tuple(outs)
```

```python
import jax
import jax.numpy as jnp
from jax import lax
from jax.experimental import pallas as pl
from jax.experimental.pallas import tpu as pltpu

F32 = jnp.float32
BF16 = jnp.bfloat16
MESH = pl.DeviceIdType.MESH

D = 1024
HEADS = 8
DK = 128
LANE = 128
CH = 128
LEVELS = 7
KW = 31
HALO = 32
DFF = 4096
NDEV = 8
EPS = 1e-6
DP_SPLIT = (3, 6, 7, 0, 1, 2, 4, 5)
LOSS_ROW = 21
ADAM_LR, ADAM_B1, ADAM_B2, ADAM_EPS, ADAM_WD, ADAM_STEP = 0.001, 0.9, 0.999, 1e-08, 0.01, 10
VMEM_LIMIT = 58 * 1024 * 1024

_NN = (((1,), (0,)), ((), ()))
_NT = (((1,), (1,)), ((), ()))
_TN = (((0,), (0,)), ((), ()))

VSPEC = pl.BlockSpec(memory_space=pltpu.VMEM)
ANYSPEC = pl.BlockSpec(memory_space=pl.ANY)


def _params(*sem):
    return pltpu.CompilerParams(dimension_semantics=sem or None, vmem_limit_bytes=VMEM_LIMIT)


def _dot(a, b, dims):
    return lax.dot_general(a, b, dims, preferred_element_type=F32)


@jax.custom_vjp
def mm_nn(a, b):
    return _dot(a.astype(BF16), b.astype(BF16), _NN)


def _mm_nn_fwd(a, b):
    ab, bb = a.astype(BF16), b.astype(BF16)
    return _dot(ab, bb, _NN), (ab, bb)


def _mm_nn_bwd(res, ct):
    ab, bb = res
    cb = ct.astype(BF16)
    return _dot(cb, bb, _NT), _dot(ab, cb, _TN)


mm_nn.defvjp(_mm_nn_fwd, _mm_nn_bwd)


@jax.custom_vjp
def mm_nt(a, b):
    return _dot(a.astype(BF16), b.astype(BF16), _NT)


def _mm_nt_fwd(a, b):
    ab, bb = a.astype(BF16), b.astype(BF16)
    return _dot(ab, bb, _NT), (ab, bb)


def _mm_nt_bwd(res, ct):
    ab, bb = res
    cb = ct.astype(BF16)
    return _dot(cb, bb, _NN), _dot(cb, ab, _TN)


mm_nt.defvjp(_mm_nt_fwd, _mm_nt_bwd)


@jax.custom_vjp
def mm_tn(a, b):
    return _dot(a.astype(BF16), b.astype(BF16), _TN)


def _mm_tn_fwd(a, b):
    ab, bb = a.astype(BF16), b.astype(BF16)
    return _dot(ab, bb, _TN), (ab, bb)


def _mm_tn_bwd(res, ct):
    ab, bb = res
    cb = ct.astype(BF16)
    return _dot(bb, cb, _NT), _dot(ab, cb, _NN)


mm_tn.defvjp(_mm_tn_fwd, _mm_tn_bwd)


def _rms(x):
    return x * lax.rsqrt(jnp.mean(x * x, axis=-1, keepdims=True) + EPS)


_sigmoid = jax.nn.sigmoid


def _silu(x):
    return x * _sigmoid(x)


def f_modulate(x, gain, sc, sh):
    return _rms(x) * gain * (1.0 + sc) + sh


def f_residual(x, y, gate, gain):
    return x + gate * (_rms(y) * gain)


def f_merge(ga, gb, ya, yb):
    return _sigmoid(ga) * ya + _sigmoid(gb) * yb


def f_head_out(o, og, hg):
    heads = [_rms(o[:, h * DK:(h + 1) * DK]) for h in range(HEADS)]
    return jnp.concatenate(heads, axis=1) * hg * _silu(og)


def f_conv_act(u, g, b):
    mu = jnp.mean(u, axis=-1, keepdims=True)
    var = jnp.mean(jnp.square(u - mu), axis=-1, keepdims=True)
    return _silu((u - mu) * lax.rsqrt(var + EPS) * g + b)


def f_glu(cv, cg):
    return cv * _sigmoid(cg)


def _split2(x):
    hi = x.astype(BF16)
    return hi, (x - hi.astype(F32)).astype(BF16)


def _tri(transposed):
    i = lax.broadcasted_iota(jnp.int32, (CH, CH), 1 if transposed else 0)
    t = lax.broadcasted_iota(jnp.int32, (CH, CH), 0 if transposed else 1)
    return jnp.where(t <= i, 1.0, 0.0).astype(BF16)


def _blocks3(x, rows):
    return x.reshape(CH // rows, rows, x.shape[-1])


def _mid_broadcast(b, lev):
    h = 1 << (LEVELS - 1 - lev)
    if h >= 4:
        x3 = _blocks3(b, 2 * h)
        return jnp.broadcast_to(x3[:, h - 1:h, :], x3.shape).reshape(b.shape)
    x3 = _blocks3(b, 8)
    sub = lax.broadcasted_iota(jnp.int32, x3.shape, 1)
    out = None
    for first in range(0, 8, 2 * h):
        piece = jnp.broadcast_to(x3[:, first + h - 1:first + h, :], x3.shape)
        out = piece if out is None else jnp.where(sub >= first, piece, out)
    return out.reshape(b.shape)


def _mid_scatter(d, lev):
    h = 1 << (LEVELS - 1 - lev)
    if h >= 4:
        x3 = _blocks3(d, 2 * h)
        row = lax.broadcasted_iota(jnp.int32, x3.shape, 1)
        total = jnp.sum(x3, axis=1, keepdims=True)
        return jnp.where(row == h - 1, total, 0.0).reshape(d.shape)
    x3 = _blocks3(d, 8)
    sub = lax.broadcasted_iota(jnp.int32, x3.shape, 1)
    out = jnp.zeros_like(x3)
    for first in range(0, 8, 2 * h):
        inside = (sub >= first) & (sub < first + 2 * h)
        total = jnp.sum(jnp.where(inside, x3, 0.0), axis=1, keepdims=True)
        out = jnp.where(sub == first + h - 1, total, out)
    return out.reshape(d.shape)


@jax.custom_vjp
def decay_sums(g):
    hi, lo = _split2(g)
    tri = _tri(False)
    b = _dot(tri, hi, _NN) + _dot(tri, lo, _NN)
    return (b,) + tuple(b - _mid_broadcast(b, lev) for lev in range(LEVELS))


def _decay_sums_fwd(g):
    return decay_sums(g), None


def _decay_sums_bwd(_, cts):
    db = cts[0]
    for lev in range(LEVELS):
        db = db + cts[1 + lev] - _mid_scatter(cts[1 + lev], lev)
    hi, lo = _split2(db)
    tri = _tri(True)
    return (_dot(tri, hi, _NN) + _dot(tri, lo, _NN),)


decay_sums.defvjp(_decay_sums_fwd, _decay_sums_bwd)


def _score_masks():
    i = lax.broadcasted_iota(jnp.int32, (CH, CH), 0)
    j = lax.broadcasted_iota(jnp.int32, (CH, CH), 1)
    masks = [i == j]
    for lev in range(LEVELS):
        sh = LEVELS - 1 - lev
        same = (i >> (sh + 1)) == (j >> (sh + 1))
        masks.append(same & (((i >> sh) & 1) == 1) & (((j >> sh) & 1) == 0))
    return [jnp.where(m, 1.0, 0.0) for m in masks]


def f_hgrn_chunk(q_r, f_r, v, logits):
    l0, l1 = logits[0:1, :], logits[1:2, :]
    mx = lax.stop_gradient(jnp.maximum(l0, l1))
    e0, e1 = jnp.exp(l0 - mx), jnp.exp(l1 - mx)
    lb = e0 / (e0 + e1)
    q = _silu(q_r)
    f = lb + (1.0 - lb) * _sigmoid(f_r)
    k = 1.0 - f
    sums = decay_sums(jnp.log(f))
    b = sums[0]
    btot = b[CH - 1:CH, :]
    qe = q * jnp.exp(b)
    ke = k * jnp.exp(btot - b)
    dec = jnp.exp(btot)
    qs, ks = [q], [k]
    row = lax.broadcasted_iota(jnp.int32, b.shape, 0)
    for lev in range(LEVELS):
        upper = ((row >> (LEVELS - 1 - lev)) & 1) == 1
        e = sums[1 + lev]
        qs.append(q * jnp.exp(jnp.where(upper, e, 0.0)))
        ks.append(k * jnp.exp(jnp.where(upper, 0.0, -e)))
    masks = _score_masks()
    intra, ut = [], []
    for h in range(HEADS):
        sl = slice(h * DK, (h + 1) * DK)
        sc = None
        for lev in range(LEVELS + 1):
            s = mm_nt(qs[lev][:, sl], ks[lev][:, sl]) * masks[lev]
            sc = s if sc is None else sc + s
        intra.append(mm_nn(sc, v[:, sl]))
        ut.append(mm_tn(v[:, sl], ke[:, sl]))
    return jnp.concatenate(intra, axis=1), qe, jnp.concatenate(ut, axis=1), dec


def _inter(qe_b, st_ref, rows):
    out = []
    for ci in range(rows // CH):
        st = st_ref[ci].astype(BF16)
        heads = [_dot(qe_b[ci * CH:(ci + 1) * CH, h * DK:(h + 1) * DK], st[:, h * DK:(h + 1) * DK], _NT)
                 for h in range(HEADS)]
        out.append(jnp.concatenate(heads, axis=1))
    return jnp.concatenate(out, axis=0)


def _shift_stack(src, dst, rows):
    dst[0, 0:rows, :] = src[0:rows, :]
    for b in range(1, 8):
        dst[b, 0:rows - 8, :] = src[pl.ds(b, rows - 8), :]


def _shifted(stack, offset, rows, lanes=slice(None)):
    return stack[offset % 8, pl.ds(8 * (offset // 8), rows), lanes]


def _adamw(w, g, m, v):
    m = ADAM_B1 * m + (1.0 - ADAM_B1) * g
    v = ADAM_B2 * v + (1.0 - ADAM_B2) * jnp.square(g)
    m_hat = m / (1.0 - ADAM_B1 ** ADAM_STEP)
    v_hat = v / (1.0 - ADAM_B2 ** ADAM_STEP)
    delta = -ADAM_LR * (m_hat / (jnp.sqrt(v_hat) + ADAM_EPS) + ADAM_WD * w)
    return delta, m, v


def _me():
    return lax.axis_index("x"), lax.axis_index("y"), lax.axis_index("c")


def _peer(k):
    x, y, c = _me()
    mask = k + 1
    px = (1 - x) if (mask >> 2) & 1 else x
    py = (1 - y) if (mask >> 1) & 1 else y
    pc = (1 - c) if mask & 1 else c
    return (px, py, pc), 4 * px + 2 * py + pc


def ada_forward(c, w_ada, b_ada, dw):
    wcols = w_ada.shape[1]

    def body(c_ref, w_ref, b_ref, dw_ref, mod_ref, call_ref, dwall_ref, part_ref, modp_ref, send_sems, recv_sems):
        x, y, cc = _me()
        me = 4 * x + 2 * y + cc
        call_ref[me] = _silu(c_ref[...])
        dwall_ref[me] = dw_ref[...]
        sends = []
        for k in range(NDEV - 1):
            dev, pidx = _peer(k)
            cp = pltpu.make_async_remote_copy(dwall_ref.at[me], dwall_ref.at[me], send_sems.at[2 * (NDEV - 1) + k],
                                              recv_sems.at[2 * (NDEV - 1) + k], device_id=dev, device_id_type=MESH)
            cp.start()
            sends.append(cp)
        for k in range(NDEV - 1):
            dev, _ = _peer(k)
            cp = pltpu.make_async_remote_copy(call_ref.at[me], call_ref.at[me], send_sems.at[k], recv_sems.at[k],
                                              device_id=dev, device_id_type=MESH)
            cp.start()
            sends.append(cp)
        for k in range(NDEV - 1):
            _, pidx = _peer(k)
            pltpu.make_async_remote_copy(call_ref.at[pidx], call_ref.at[pidx], send_sems.at[k], recv_sems.at[k],
                                         device_id=_peer(k)[0], device_id_type=MESH).wait_recv()
        call = jnp.concatenate([call_ref[r] for r in range(NDEV)], axis=0)
        part = _dot(call.astype(BF16), w_ref[...].astype(BF16), _NN)
        for r in range(NDEV):
            part_ref[r] = part[r:r + 1, :]
        modp_ref[me] = part_ref[me]
        for k in range(NDEV - 1):
            dev, pidx = _peer(k)
            cp = pltpu.make_async_remote_copy(part_ref.at[pidx], modp_ref.at[me], send_sems.at[NDEV - 1 + k],
                                              recv_sems.at[NDEV - 1 + k], device_id=dev, device_id_type=MESH)
            cp.start()
            sends.append(cp)
        for k in range(NDEV - 1):
            dev, pidx = _peer(k)
            pltpu.make_async_remote_copy(part_ref.at[pidx], modp_ref.at[pidx], send_sems.at[NDEV - 1 + k],
                                         recv_sems.at[NDEV - 1 + k], device_id=dev, device_id_type=MESH).wait_recv()
        for k in range(NDEV - 1):
            dev, pidx = _peer(k)
            pltpu.make_async_remote_copy(dwall_ref.at[pidx], dwall_ref.at[pidx], send_sems.at[2 * (NDEV - 1) + k],
                                         recv_sems.at[2 * (NDEV - 1) + k], device_id=dev,
                                         device_id_type=MESH).wait_recv()
        for cp in sends:
            cp.wait_send()
        mod_ref[...] = jnp.concatenate([modp_ref[r] for r in range(NDEV)], axis=1) + b_ref[...]

    mod, call, dw_all = pl.pallas_call(
        body, name="ada_forward",
        out_shape=(jax.ShapeDtypeStruct((1, NDEV * wcols), F32), jax.ShapeDtypeStruct((NDEV, 1, D), F32),
                   jax.ShapeDtypeStruct((NDEV,) + dw.shape, F32)),
        in_specs=[VSPEC] * 4, out_specs=(VSPEC,) * 3,
        scratch_shapes=[pltpu.VMEM((NDEV, 1, wcols), F32), pltpu.VMEM((NDEV, 1, wcols), F32),
                        pltpu.SemaphoreType.DMA((3 * (NDEV - 1),)), pltpu.SemaphoreType.DMA((3 * (NDEV - 1),))],
        compiler_params=_params(),
    )(c, w_ada, b_ada, dw)
    return mod, call.reshape(NDEV, D), dw_all


def _comm_sems(n, local):
    sems = [pltpu.SemaphoreType.DMA((7 * n,)), pltpu.SemaphoreType.DMA((7 * n,))]
    return sems + ([pltpu.SemaphoreType.DMA((n,))] if local else [])


def _gather2_ops(ins, outs, send_sems, recv_sems, local_sems):
    n = len(ins)
    x, y, c = _me()
    me, sibling = (x, y, c), (x, y, 1 - c)
    chips = [(1 - x, y), (x, 1 - y), (1 - x, 1 - y)]

    def slot(p):
        return 4 * p[0] + 2 * p[1] + p[2]

    def copy(a, k, block, to, src=None):
        return pltpu.make_async_remote_copy(
            src_ref=outs[a].at[slot(block)] if src is None else src, dst_ref=outs[a].at[slot(block)],
            send_sem=send_sems.at[a * 7 + k], recv_sem=recv_sems.at[a * 7 + k], device_id=to, device_id_type=MESH)

    def local(a):
        return pltpu.make_async_copy(ins[a], outs[a].at[slot(me)], local_sems.at[a])

    def first(a):
        return [copy(a, 0, me, sibling, src=ins[a])] + [copy(a, 1 + j, me, (*chip, c), src=ins[a])
                                                        for j, chip in enumerate(chips)]

    def passed(a):
        return [copy(a, 4 + j, (*chip, c), sibling) for j, chip in enumerate(chips)]

    def start():
        for a in range(n):
            local(a).start()
            for cp in first(a):
                cp.start()

    def forward():
        for j, chip in enumerate(chips):
            for a in range(n):
                copy(a, 1 + j, (*chip, c), me).wait_recv()
                passed(a)[j].start()

    def finish():
        for a in range(n):
            copy(a, 0, sibling, me).wait_recv()
            for j, chip in enumerate(chips):
                copy(a, 4 + j, (*chip, 1 - c), me).wait_recv()
        for a in range(n):
            for cp in first(a) + passed(a):
                cp.wait_send()
            local(a).wait()

    return start, forward, finish


def _exchange_ops(ins, outs, send_sems, recv_sems):
    n = len(ins)

    def copy(a, k):
        dev, pidx = _peer(k)
        return pltpu.make_async_remote_copy(ins[a].at[pidx], outs[a].at[k], send_sems.at[a * 7 + k],
                                            recv_sems.at[a * 7 + k], device_id=dev, device_id_type=MESH)

    def start():
        for k in range(NDEV - 1):
            for a in range(n):
                copy(a, k).start()

    def finish():
        for k in range(NDEV - 1):
            for a in range(n):
                copy(a, k).wait_recv()
        for k in range(NDEV - 1):
            for a in range(n):
                copy(a, k).wait_send()

    return start, finish


def _gathered(shards):
    return tuple(jax.ShapeDtypeStruct((NDEV,) + s.shape, s.dtype) for s in shards)


def _exchanged(blocks):
    return tuple(jax.ShapeDtypeStruct((NDEV - 1,) + b.shape[1:], b.dtype) for b in blocks)


def _owner_copy(src_ref, land_ref, send_sems, recv_sems, gi, n):
    x, y, c = _me()
    k = jnp.bitwise_xor(4 * x + 2 * y + c, n) - 1
    return pltpu.make_async_remote_copy(src_ref.at[gi], land_ref.at[k], send_sems.at[gi], recv_sems.at[k],
                                        device_id=(n >> 2, (n >> 1) & 1, n & 1), device_id_type=MESH)


def exchange_start(blocks, owners, name):
    landing = lax.empty((NDEV - 1,) + blocks.shape[1:], blocks.dtype)
    hbm = pl.BlockSpec(memory_space=pltpu.HBM)
    sem = pl.BlockSpec(memory_space=pltpu.SEMAPHORE)

    def body(src_ref, land_ref, send_sems, recv_sems, src_thru, land_thru, token):
        x, y, c = _me()
        for gi, n in enumerate(owners):
            @pl.when(4 * x + 2 * y + c != n)
            def _(gi=gi, n=n):
                _owner_copy(src_ref, land_ref, send_sems, recv_sems, gi, n).start()
        token[...] = jnp.zeros_like(token)

    return pl.pallas_call(
        body, name=name,
        out_shape=(pltpu.SemaphoreType.DMA((len(owners),)), pltpu.SemaphoreType.DMA((NDEV - 1,)),
                   pltpu.HBM(blocks.shape, blocks.dtype), pltpu.HBM(landing.shape, landing.dtype),
                   jax.ShapeDtypeStruct((8, 128), F32)),
        in_specs=(hbm, hbm), out_specs=(sem, sem, hbm, hbm, VSPEC), input_output_aliases={0: 2, 1: 3},
        compiler_params=pltpu.CompilerParams(has_side_effects=pltpu.SideEffectType.DATAFLOW_SIDE_EFFECTING),
    )(pltpu.with_memory_space_constraint(blocks, pltpu.HBM), pltpu.with_memory_space_constraint(landing, pltpu.HBM))


def exchange_wait(send_sems, recv_sems, src_thru, land_thru, after, owners, name):
    hbm = pl.BlockSpec(memory_space=pltpu.HBM)
    sem = pl.BlockSpec(memory_space=pltpu.SEMAPHORE)

    def body(src_ref, land_ref, send_sems, recv_sems, after_ref, src_dead, got_ref):
        x, y, c = _me()
        me = 4 * x + 2 * y + c
        for gi, n in enumerate(owners):
            @pl.when(me != n)
            def _(gi=gi, n=n):
                _owner_copy(src_ref, land_ref, send_sems, recv_sems, gi, n).wait_send()

            @pl.when(me == n)
            def _(gi=gi):
                for k in range(NDEV - 1):
                    pltpu.make_async_remote_copy(src_ref.at[gi], land_ref.at[k], send_sems.at[gi], recv_sems.at[k],
                                                 device_id=_peer(k)[0], device_id_type=MESH).wait_recv()

    return pl.pallas_call(
        body, name=name,
        out_shape=(pltpu.HBM(src_thru.shape, src_thru.dtype), pltpu.HBM(land_thru.shape, land_thru.dtype)),
        in_specs=(hbm, hbm, sem, sem, ANYSPEC), out_specs=(hbm, hbm), input_output_aliases={0: 0, 1: 1},
        compiler_params=pltpu.CompilerParams(has_side_effects=pltpu.SideEffectType.DATAFLOW_SIDE_EFFECTING),
    )(src_thru, land_thru, send_sems, recv_sems, after)[1]


def gather_and_sum_rows(parts, keep_rows):
    n = len(parts)
    offs = [sum(p.shape[0] for p in parts[:i]) for i in range(n)]
    rows = sum(p.shape[0] for p in parts)

    def body(*refs):
        sum_ref, keep_ref, mine_ref, buf_ref, send_sems, recv_sems, local_sems = refs[n:]
        for i in range(n):
            mine_ref[offs[i]:offs[i] + parts[i].shape[0], :] = refs[i][...]
        start, forward, finish = _gather2_ops([mine_ref], [buf_ref], send_sems, recv_sems, local_sems)
        start()
        forward()
        finish()
        total = buf_ref[0]
        for r in range(1, NDEV):
            total = total + buf_ref[r]
        sum_ref[...] = total
        keep_ref[...] = jnp.zeros_like(keep_ref)
        for r in range(NDEV):
            for j, src in enumerate(keep_rows):
                keep_ref[r, j:j + 1, :] = buf_ref[r, src:src + 1, :]

    return pl.pallas_call(
        body, name="gather_and_sum_rows",
        out_shape=(jax.ShapeDtypeStruct((rows, D), F32), jax.ShapeDtypeStruct((NDEV, 8, D), F32)),
        in_specs=[VSPEC] * n, out_specs=(VSPEC, VSPEC),
        scratch_shapes=[pltpu.VMEM((rows, D), F32), pltpu.VMEM((NDEV, rows, D), F32)] + _comm_sems(1, True),
        compiler_params=_params(),
    )(*parts)


def _gather_chips():
    x, y, c = _me()
    north = c == 1
    first = (jnp.where(north, 1 - x, x), jnp.where(north, y, 1 - y))
    second = (jnp.where(north, x, 1 - x), jnp.where(north, 1 - y, y))
    return [first, second, (1 - x, 1 - y)]


def _arrival_order():
    x, y, c = _me()
    first, second, far = _gather_chips()
    devs = [(x, y, c), (x, y, 1 - c), (*first, c), (*second, 1 - c), (*second, c), (*first, 1 - c), (*far, c),
            (*far, 1 - c)]
    return jnp.stack([4 * d[0] + 2 * d[1] + d[2] for d in devs]).astype(jnp.int32)


def in_proj(x, gain, sc, sh, w_shard, b_in):
    t = x.shape[0]
    tm = min(1024, t // 2)
    nm = t // tm
    order = _arrival_order()

    def body(order_ref, x_ref, g_ref, sc_ref, sh_ref, b_ref, wsh_ref, p_ref, h_ref, wall_ref,
             wbuf, h_all, send_sems, recv_sems, local_sems):
        k, m = pl.program_id(0), pl.program_id(1)
        mx, my, mc = _me()
        me, sibling = (mx, my, mc), (mx, my, 1 - mc)
        chips = _gather_chips()

        def slot(d):
            return 4 * d[0] + 2 * d[1] + d[2]

        def copy(sem, block, to, src=None):
            return pltpu.make_async_remote_copy(
                src_ref=wall_ref.at[slot(block)] if src is None else src, dst_ref=wall_ref.at[slot(block)],
                send_sem=send_sems.at[sem], recv_sem=recv_sems.at[sem], device_id=to, device_id_type=MESH)

        own = pltpu.make_async_copy(wsh_ref, wall_ref.at[slot(me)], local_sems.at[0])
        first = [copy(0, me, sibling, src=wsh_ref)] + [copy(1 + j, me, (*q, mc), src=wsh_ref)
                                                       for j, q in enumerate(chips)]
        passed = [copy(4 + j, (*q, mc), sibling) for j, q in enumerate(chips)]

        def load(kk):
            src = wsh_ref if kk == 0 else wall_ref.at[order_ref[kk]]
            return pltpu.make_async_copy(src, wbuf.at[kk % 2], local_sems.at[1 + kk % 2])

        def arrived(kk):
            if kk == 1:
                copy(0, sibling, me).wait_recv()
            elif kk in (2, 4, 6):
                j = kk // 2 - 1
                copy(1 + j, (*chips[j], mc), me).wait_recv()
                passed[j].start()
                if j < 2:
                    first[2 + j].start()
            else:
                j = {3: 1, 5: 0, 7: 2}[kk]
                copy(4 + {1: 0, 0: 1, 2: 2}[j], (*chips[j], 1 - mc), me).wait_recv()

        @pl.when((k == 0) & (m == 0))
        def _():
            own.start()
            for cp in first[:2]:
                cp.start()
            load(0).start()
            load(0).wait()

        for kk in range(1, NDEV):
            @pl.when((k == kk - 1) & (m == nm - 1))
            def _(kk=kk):
                arrived(kk)
                load(kk).start()

            @pl.when((k == kk) & (m == 0))
            def _(kk=kk):
                load(kk).wait()

        rows = pl.ds(pl.multiple_of(m * tm, tm), tm)

        @pl.when(k == 0)
        def _():
            h_all[rows, :] = f_modulate(x_ref[...], g_ref[...], sc_ref[...], sh_ref[...]).astype(BF16)

        p_ref[...] = _dot(h_all[rows, :], wbuf[k % 2], _NN) + b_ref[...]

        @pl.when((k == NDEV - 1) & (m == nm - 1))
        def _():
            for cp in first + passed:
                cp.wait_send()
            own.wait()
            out = pltpu.make_async_copy(h_all, h_ref, local_sems.at[0])
            out.start()
            out.wait()

    vec = pl.BlockSpec((1, D), lambda k, m, o: (0, 0))
    return pl.pallas_call(
        body, name="in_proj",
        out_shape=(jax.ShapeDtypeStruct((t, NDEV * D), F32), jax.ShapeDtypeStruct((t, D), BF16),
                   jax.ShapeDtypeStruct((NDEV, D, D), BF16)),
        grid_spec=pltpu.PrefetchScalarGridSpec(
            num_scalar_prefetch=1, grid=(NDEV, nm),
            in_specs=[pl.BlockSpec((tm, D), lambda k, m, o: (jnp.where(k == 0, m, 0), 0)), vec, vec, vec,
                      pl.BlockSpec((1, D), lambda k, m, o: (0, o[k])), ANYSPEC],
            out_specs=(pl.BlockSpec((tm, D), lambda k, m, o: (m, o[k])), ANYSPEC, ANYSPEC),
            scratch_shapes=[pltpu.VMEM((2, D, D), BF16), pltpu.VMEM((t, D), BF16), pltpu.SemaphoreType.DMA((7,)),
                            pltpu.SemaphoreType.DMA((7,)), pltpu.SemaphoreType.DMA((3,))]),
        compiler_params=_params("arbitrary", "arbitrary"),
    )(order, x, gain, sc, sh, b_in, w_shard)


def hgrn_local(p, logits, shards):
    t = p.shape[0]
    nc = t // CH
    n = len(shards)

    def body(*refs):
        q_ref, f_ref, v_ref, l_ref = refs[:4]
        intra_ref, qe_ref, st_ref, dec_ref = refs[4 + n:8 + n]
        state = refs[8 + 2 * n]
        start, forward, finish = _gather2_ops(refs[4:4 + n], refs[8 + n:8 + 2 * n], *refs[9 + 2 * n:])
        step = pl.program_id(0)

        @pl.when(step == 0)
        def _():
            start()
            state[...] = jnp.zeros_like(state)

        intra, qe, ut, dec = f_hgrn_chunk(q_ref[...], f_ref[...], v_ref[...], l_ref[...])
        intra_ref[...] = intra
        qe_ref[...] = qe.astype(BF16)
        dec_ref[...] = dec
        s = state[...]
        st_ref[...] = s
        state[...] = dec * s + ut
        pl.when(step == (3 * nc) // 4)(forward)
        pl.when(step == nc - 1)(finish)

    col = lambda j: pl.BlockSpec((CH, D), lambda c: (c, j))
    res = pl.pallas_call(
        body, name="hgrn_local", grid=(nc,),
        out_shape=(jax.ShapeDtypeStruct((t, D), F32), jax.ShapeDtypeStruct((t, D), BF16),
                   jax.ShapeDtypeStruct((nc, DK, D), F32), jax.ShapeDtypeStruct((nc, 1, D), F32)) + _gathered(shards),
        in_specs=[col(0), col(1), col(2), pl.BlockSpec((2, D), lambda c: (0, 0))] + [ANYSPEC] * n,
        out_specs=(pl.BlockSpec((CH, D), lambda c: (c, 0)), pl.BlockSpec((CH, D), lambda c: (c, 0)),
                   pl.BlockSpec((None, DK, D), lambda c: (c, 0, 0)), pl.BlockSpec((None, 1, D), lambda c: (c, 0, 0)))
        + (ANYSPEC,) * n,
        scratch_shapes=[pltpu.VMEM((DK, D), F32)] + _comm_sems(n, True),
        compiler_params=_params("arbitrary"),
    )(p, p, p, logits, *shards)
    return res[:4] + (res[4:],)


def _conv_tile(t):
    return min(256, t)


def conv_forward(p, dw, db, ln_g, ln_b, shards):
    t = p.shape[0]
    tm = _conv_tile(t)
    per = tm // HALO
    n = len(shards)
    nt = t // tm

    def body(*refs):
        cv_ref, cg_ref, cvp_ref, cgp_ref, dw_ref, db_ref, g_ref, b_ref = refs[:8]
        yc_ref, act_ref = refs[8 + n:10 + n]
        uext, ush = refs[10 + 2 * n:12 + 2 * n]
        start, forward, finish = _gather2_ops(refs[8:8 + n], refs[10 + n:10 + 2 * n], *refs[12 + 2 * n:])
        step = pl.program_id(0)
        pl.when(step == 0)(start)
        pl.when(step == nt - 1)(forward)
        uext[0:HALO, :] = jnp.where(step == 0, 0.0, f_glu(cvp_ref[...], cgp_ref[...]))
        uext[HALO:HALO + tm, :] = f_glu(cv_ref[...], cg_ref[...])
        _shift_stack(uext, ush, tm + HALO)
        acc = jnp.zeros((tm, D), F32) + db_ref[...]
        for w in range(KW):
            acc = acc + dw_ref[w:w + 1, :] * _shifted(ush, HALO - KW + 1 + w, tm)
        yc_ref[...] = acc
        act_ref[...] = f_conv_act(acc, g_ref[...], b_ref[...]).astype(BF16)
        pl.when(step == nt - 1)(finish)

    vec = pl.BlockSpec((1, D), lambda m: (0, 0))
    prev = lambda j: pl.BlockSpec((HALO, D), lambda m: (jnp.maximum(m * per - 1, 0), j))
    res = pl.pallas_call(
        body, name="conv_forward", grid=(nt,),
        out_shape=(jax.ShapeDtypeStruct((t, D), F32), jax.ShapeDtypeStruct((t, D), BF16)) + _gathered(shards),
        in_specs=[pl.BlockSpec((tm, D), lambda m: (m, 4)), pl.BlockSpec((tm, D), lambda m: (m, 5)), prev(4), prev(5),
                  pl.BlockSpec((HALO, D), lambda m: (0, 0)), vec, vec, vec] + [ANYSPEC] * n,
        out_specs=(pl.BlockSpec((tm, D), lambda m: (m, 0)), pl.BlockSpec((tm, D), lambda m: (m, 0))) + (ANYSPEC,) * n,
        scratch_shapes=[pltpu.VMEM((HALO + tm, D), F32), pltpu.VMEM((8, tm + HALO, D), F32)] + _comm_sems(n, True),
        compiler_params=_params("arbitrary"),
    )(p, p, p, p, dw, db, ln_g, ln_b, *shards)
    return res[0], res[1], res[2:]


def _mix_forward_tile(x_ref, og_ref, ga_ref, gb_ref, qe_ref, intra_ref, st_ref, act_ref, wa_ref, wb_ref, wo_ref,
                      hg_ref, rows):
    o = _inter(qe_ref[...], st_ref, rows) + intra_ref[...]
    a = f_head_out(o, og_ref[...], hg_ref[...])
    ya = _dot(a.astype(BF16), wa_ref[...], _NN)
    yb = _dot(act_ref[...], wb_ref[...], _NN)
    merged = f_merge(ga_ref[...], gb_ref[...], ya, yb)
    y = _dot(merged.astype(BF16), wo_ref[...], _NN)
    return o, a, ya, yb, merged, y


def _mix_specs(tm, tiles=None):
    at = (lambda m: m) if tiles is None else (lambda m: tiles - 1 - m)
    col = lambda j: pl.BlockSpec((tm, D), lambda m: (at(m), j))
    row = pl.BlockSpec((tm, D), lambda m: (at(m), 0))
    per_chunk = lambda rows: pl.BlockSpec((tm // CH, rows, D), lambda m: (at(m), 0, 0))
    return col, row, pl.BlockSpec((1, D), lambda m: (0, 0)), per_chunk


def mix_forward(x, p, qe, intra, st_prev, act, wa, wb, wo, hg, g1, post):
    t = x.shape[0]
    tm = min(512, t)

    def body(x_ref, og_ref, ga_ref, gb_ref, qe_ref, intra_ref, st_ref, act_ref, wa_ref, wb_ref, wo_ref, hg_ref,
             g1_ref, post_ref, x1_ref):
        y = _mix_forward_tile(x_ref, og_ref, ga_ref, gb_ref, qe_ref, intra_ref, st_ref, act_ref, wa_ref, wb_ref,
                              wo_ref, hg_ref, tm)[-1]
        x1_ref[...] = f_residual(x_ref[...], y, g1_ref[...], post_ref[...])

    col, row, vec, per_chunk = _mix_specs(tm)
    return pl.pallas_call(
        body, name="mix_forward", grid=(t // tm,),
        out_shape=jax.ShapeDtypeStruct((t, D), F32),
        in_specs=[row, col(3), col(6), col(7), row, row, per_chunk(DK), row, VSPEC, VSPEC, VSPEC, vec, vec, vec],
        out_specs=row,
        compiler_params=_params("parallel"),
    )(x, p, p, p, qe, intra, st_prev, act, wa, wb, wo, hg, g1, post)


def ffn_forward_backward(x1, target, w1, w2, pre, sc, sh, g2, post):
    t = x1.shape[0]
    tm = min(512, t)
    nb = w1.shape[0]
    fb = w1.shape[2]

    def body(x_ref, tg_ref, w1_ref, w2_ref, pre_ref, sc_ref, sh_ref, g2_ref, post_ref,
             dx_ref, h2_ref, r_ref, dz_ref, dy2_ref, acc_ref, z_sc):
        @pl.when(pl.program_id(0) == 0)
        def _():
            acc_ref[...] = jnp.zeros_like(acc_ref)

        x1v = x_ref[...]
        h2, vjp_h = jax.vjp(f_modulate, x1v, pre_ref[...], sc_ref[...], sh_ref[...])
        h2b = h2.astype(BF16)
        h2_ref[...] = h2b
        y2 = jnp.zeros((tm, D), F32)
        for n in range(nb):
            z = _dot(h2b, w1_ref[n], _NN)
            z_sc[:, n * fb:(n + 1) * fb] = z
            r = jnp.square(jnp.maximum(z, 0.0)).astype(BF16)
            r_ref[:, n * fb:(n + 1) * fb] = r
            y2 = y2 + _dot(r, w2_ref[n], _NN)
        out, vjp_r = jax.vjp(f_residual, x1v, y2, g2_ref[...], post_ref[...])
        err = out - tg_ref[...]
        tok = jnp.mean(jnp.square(err), axis=-1, keepdims=True)
        acc_ref[5:6, :] += 0.5 * jnp.sum(tok, axis=0, keepdims=True)
        dx_a, dy2, dg2, dpost = vjp_r(err * (1.0 / D))
        dy2b = dy2.astype(BF16)
        dy2_ref[...] = dy2b
        dh2 = jnp.zeros((tm, D), F32)
        for n in range(nb):
            dr = _dot(dy2b, w2_ref[n], _NT)
            dz = (dr * (2.0 * jnp.maximum(z_sc[:, n * fb:(n + 1) * fb], 0.0))).astype(BF16)
            dz_ref[:, n * fb:(n + 1) * fb] = dz
            dh2 = dh2 + _dot(dz, w1_ref[n], _NT)
        dx_b, dpre, dsc, dsh = vjp_h(dh2)
        dx_ref[...] = dx_a + dx_b
        acc_ref[0:1, :] += dpre
        acc_ref[1:2, :] += dpost
        acc_ref[2:3, :] += dsc
        acc_ref[3:4, :] += dsh
        acc_ref[4:5, :] += dg2

    row = pl.BlockSpec((tm, D), lambda m: (m, 0))
    wide = pl.BlockSpec((tm, DFF), lambda m: (m, 0), pipeline_mode=pl.Buffered(1))
    vec = pl.BlockSpec((1, D), lambda m: (0, 0))
    return pl.pallas_call(
        body, name="ffn_forward_backward", grid=(t // tm,),
        out_shape=(jax.ShapeDtypeStruct((t, D), F32), jax.ShapeDtypeStruct((t, D), BF16),
                   jax.ShapeDtypeStruct((t, DFF), BF16), jax.ShapeDtypeStruct((t, DFF), BF16),
                   jax.ShapeDtypeStruct((t, D), BF16), jax.ShapeDtypeStruct((8, D), F32)),
        in_specs=[row, row, VSPEC, VSPEC, vec, vec, vec, vec, vec],
        out_specs=(row, row, wide, wide, row, pl.BlockSpec((8, D), lambda m: (0, 0))),
        scratch_shapes=[pltpu.VMEM((tm, DFF), F32)],
        compiler_params=_params("arbitrary"),
    )(x1, target, w1, w2, pre, sc, sh, g2, post)


def mix_backward(x, p, qe, intra, st_prev, act, wa, wb, wo, hg, g1, post, dx1, dec):
    t = x.shape[0]
    tm = min(256, t)
    nc = t // CH

    def body(x_ref, og_ref, ga_ref, gb_ref, qe_ref, intra_ref, st_ref, act_ref, wa_ref, wb_ref, wo_ref, hg_ref,
             g1_ref, post_ref, dx1_ref, dec_ref,
             dp_ref, do_ref, dqe_ref, gt_ref, dact_ref, a_ref, mg_ref, dya_ref, dyb_ref, dy_ref, acc_ref, gstate):
        @pl.when(pl.program_id(0) == 0)
        def _():
            acc_ref[...] = jnp.zeros_like(acc_ref)
            gstate[...] = jnp.zeros_like(gstate)

        o, a, ya, yb, merged, y = _mix_forward_tile(x_ref, og_ref, ga_ref, gb_ref, qe_ref, intra_ref, st_ref, act_ref,
                                                    wa_ref, wb_ref, wo_ref, hg_ref, tm)
        a_ref[...] = a.astype(BF16)
        mg_ref[...] = merged.astype(BF16)
        _, vjp_r = jax.vjp(f_residual, x_ref[...], y, g1_ref[...], post_ref[...])
        _, dy, dg1, dpost = vjp_r(dx1_ref[...])
        dyb16 = dy.astype(BF16)
        dy_ref[...] = dyb16
        dmerged = _dot(dyb16, wo_ref[...], _NT)
        _, vjp_m = jax.vjp(f_merge, ga_ref[...], gb_ref[...], ya, yb)
        dga, dgb, dya, dyb = vjp_m(dmerged)
        dp_ref[:, D:2 * D] = dga.astype(BF16)
        dp_ref[:, 2 * D:3 * D] = dgb.astype(BF16)
        dya16, dyb16b = dya.astype(BF16), dyb.astype(BF16)
        dya_ref[...] = dya16
        dyb_ref[...] = dyb16b
        da = _dot(dya16, wa_ref[...], _NT)
        dact_ref[...] = _dot(dyb16b, wb_ref[...], _NT)
        _, vjp_a = jax.vjp(f_head_out, o, og_ref[...], hg_ref[...])
        do, dog, dhg = vjp_a(da)
        dp_ref[:, 0:D] = dog.astype(BF16)
        do_ref[...] = do
        do16 = do.astype(BF16)
        qe16 = qe_ref[...]
        for ci in reversed(range(tm // CH)):
            st = st_ref[ci].astype(BF16)
            rows = slice(ci * CH, (ci + 1) * CH)
            dqe, vt = [], []
            for h in range(HEADS):
                sl = slice(h * DK, (h + 1) * DK)
                dqe.append(_dot(do16[rows, sl], st[:, sl], _NN))
                vt.append(_dot(do16[rows, sl], qe16[rows, sl], _TN))
            dqe_ref[rows, :] = jnp.concatenate(dqe, axis=1)
            g = gstate[...]
            gt_ref[ci] = g
            gstate[...] = dec_ref[ci] * g + jnp.concatenate(vt, axis=1)
        acc_ref[0:1, :] += dg1
        acc_ref[1:2, :] += dpost
        acc_ref[2:3, :] += dhg
        acc_ref[3:4, :] += jnp.sum(dog, axis=0, keepdims=True)
        acc_ref[4:5, :] += jnp.sum(dga, axis=0, keepdims=True)
        acc_ref[5:6, :] += jnp.sum(dgb, axis=0, keepdims=True)

    nt = t // tm
    col, row, vec, per_chunk = _mix_specs(tm, nt)
    b16 = jax.ShapeDtypeStruct((t, D), BF16)
    f32 = jax.ShapeDtypeStruct((t, D), F32)
    return pl.pallas_call(
        body, name="mix_backward", grid=(nt,),
        out_shape=(jax.ShapeDtypeStruct((t, NDEV * D), BF16), f32, f32, jax.ShapeDtypeStruct((nc, DK, D), F32), f32,
                   b16, b16, b16, b16, b16, jax.ShapeDtypeStruct((8, D), F32)),
        in_specs=[row, col(3), col(6), col(7), row, row, per_chunk(DK), row, VSPEC, VSPEC, VSPEC, vec, vec, vec, row,
                  per_chunk(1)],
        out_specs=(pl.BlockSpec((tm, 3 * D), lambda m: (nt - 1 - m, 0)), row, row, per_chunk(DK), row, row, row, row,
                   row, row, pl.BlockSpec((8, D), lambda m: (0, 0))),
        scratch_shapes=[pltpu.VMEM((DK, D), F32)],
        compiler_params=_params("arbitrary"),
    )(x, p, p, p, qe, intra, st_prev, act, wa, wb, wo, hg, g1, post, dx1, dec)


def conv_backward(p, yc, dact, dw, ln_g, ln_b, blocks, dp):
    t = p.shape[0]
    tm = _conv_tile(t)
    per = tm // HALO
    nt = t // tm
    n = len(blocks)

    def body(*refs):
        cv_ref, cg_ref, cvp_ref, cgp_ref, yc_ref, ycn_ref, da_ref, dan_ref, dw_ref, g_ref, b_ref = refs[:11]
        dp_ref, acc_ref, ddw_ref = refs[12 + n:15 + n]
        uext, dyext, ush, dysh, ddw8, du_sc = refs[15 + 2 * n:21 + 2 * n]
        start, finish = _exchange_ops(refs[11:11 + n], refs[15 + n:15 + 2 * n], *refs[21 + 2 * n:])
        m = pl.program_id(0)

        @pl.when(m == 0)
        def _():
            start()
            acc_ref[...] = jnp.zeros_like(acc_ref)
            ddw8[...] = jnp.zeros_like(ddw8)

        cv, cg = cv_ref[...], cg_ref[...]
        u, vjp_u = jax.vjp(f_glu, cv, cg)
        uext[0:HALO, :] = jnp.where(m == 0, 0.0, f_glu(cvp_ref[...], cgp_ref[...]))
        uext[HALO:HALO + tm, :] = u
        _shift_stack(uext, ush, tm + HALO)
        _, vjp_c = jax.vjp(f_conv_act, yc_ref[...], g_ref[...], b_ref[...])
        dyc, dg, db = vjp_c(da_ref[...])
        _, vjp_n = jax.vjp(f_conv_act, ycn_ref[...], g_ref[...], b_ref[...])
        dyn = vjp_n(dan_ref[...])[0]
        dyext[0:tm, :] = dyc
        dyext[tm:tm + HALO, :] = jnp.where(m == nt - 1, 0.0, dyn)
        _shift_stack(dyext, dysh, tm + HALO)
        rb = min(128, tm)
        for lt in range(D // LANE):
            ls = slice(lt * LANE, (lt + 1) * LANE)
            for r0 in range(0, tm, rb):
                du_l = jnp.zeros((rb, LANE), F32)
                for w in range(KW):
                    du_l = du_l + dw_ref[w:w + 1, ls] * _shifted(dysh, KW - 1 - w + r0, rb, ls)
                du_sc[r0:r0 + rb, ls] = du_l
                dyc_l = dyext[r0:r0 + rb, ls]
                for w in range(KW):
                    prod = dyc_l * _shifted(ush, HALO - KW + 1 + w + r0, rb, ls)
                    part = jnp.sum(prod.reshape(4, rb // 32, 8, LANE), axis=1)
                    ddw8[w, :, ls] += (part[0] + part[1]) + (part[2] + part[3])
        du = du_sc[...]

        @pl.when(m == nt - 1)
        def _():
            ddw_ref[...] = jnp.sum(ddw8[...], axis=1)

        dcv, dcg = vjp_u(du)
        dp_ref[:, 0:D] = dcv.astype(BF16)
        dp_ref[:, D:2 * D] = dcg.astype(BF16)
        acc_ref[0:1, :] += jnp.sum(dyc, axis=0, keepdims=True)
        acc_ref[1:2, :] += dg
        acc_ref[2:3, :] += db
        acc_ref[3:4, :] += jnp.sum(dcv, axis=0, keepdims=True)
        acc_ref[4:5, :] += jnp.sum(dcg, axis=0, keepdims=True)
        pl.when(m == nt - 1)(finish)

    vec = pl.BlockSpec((1, D), lambda m: (0, 0))
    row = pl.BlockSpec((tm, D), lambda m: (m, 0))
    prev = lambda j: pl.BlockSpec((HALO, D), lambda m: (jnp.maximum(m * per - 1, 0), j))
    nxt = pl.BlockSpec((HALO, D), lambda m: (jnp.minimum((m + 1) * per, t // HALO - 1), 0))
    res = pl.pallas_call(
        body, name="conv_backward", grid=(nt,),
        out_shape=(jax.ShapeDtypeStruct(dp.shape, dp.dtype), jax.ShapeDtypeStruct((8, D), F32),
                   jax.ShapeDtypeStruct((HALO, D), F32)) + _exchanged(blocks),
        in_specs=[pl.BlockSpec((tm, D), lambda m: (m, 4)), pl.BlockSpec((tm, D), lambda m: (m, 5)), prev(4), prev(5),
                  row, nxt, row, nxt, pl.BlockSpec((HALO, D), lambda m: (0, 0)), vec, vec] + [ANYSPEC] * (n + 1),
        out_specs=(pl.BlockSpec((tm, 2 * D), lambda m: (m, 3)), pl.BlockSpec((8, D), lambda m: (0, 0)),
                   pl.BlockSpec((HALO, D), lambda m: (0, 0))) + (ANYSPEC,) * n,
        scratch_shapes=[pltpu.VMEM((HALO + tm, D), F32), pltpu.VMEM((tm + HALO, D), F32),
                        pltpu.VMEM((8, tm + HALO, D), F32), pltpu.VMEM((8, tm + HALO, D), F32),
                        pltpu.VMEM((HALO, 8, D), F32), pltpu.VMEM((tm, D), F32)] + _comm_sems(n, False),
        input_output_aliases={11 + n: 0},
        compiler_params=_params("arbitrary"),
    )(p, p, p, p, yc, yc, dact, dact, dw, ln_g, ln_b, *blocks, dp)
    return res[:3] + (res[3:],)


def hgrn_backward(p, logits, do, dqe, gt, st_prev, blocks, dp):
    t = p.shape[0]
    nc = t // CH
    n = len(blocks)

    def body(*refs):
        q_ref, f_ref, v_ref, l_ref, do_ref, dqe_ref, gt_ref, st_ref = refs[:8]
        dp_ref, acc_ref = refs[9 + n:11 + n]
        start, finish = _exchange_ops(refs[8:8 + n], refs[11 + n:11 + 2 * n], *refs[11 + 2 * n:])

        @pl.when(pl.program_id(0) == 0)
        def _():
            start()
            acc_ref[...] = jnp.zeros_like(acc_ref)

        gt_v = gt_ref[...]
        ddec = jnp.sum(gt_v * st_ref[...], axis=0, keepdims=True)
        _, vjp = jax.vjp(f_hgrn_chunk, q_ref[...], f_ref[...], v_ref[...], l_ref[...])
        dq, df, dv, dl = vjp((do_ref[...], dqe_ref[...], gt_v, ddec))
        dp_ref[:, 0:D] = dq.astype(BF16)
        dp_ref[:, D:2 * D] = df.astype(BF16)
        dp_ref[:, 2 * D:3 * D] = dv.astype(BF16)
        acc_ref[0:2, :] += dl
        acc_ref[2:3, :] += jnp.sum(dq, axis=0, keepdims=True)
        acc_ref[3:4, :] += jnp.sum(df, axis=0, keepdims=True)
        acc_ref[4:5, :] += jnp.sum(dv, axis=0, keepdims=True)
        pl.when(pl.program_id(0) == nc - 1)(finish)

    col = lambda j: pl.BlockSpec((CH, D), lambda c: (c, j))
    row = pl.BlockSpec((CH, D), lambda c: (c, 0))
    stspec = pl.BlockSpec((None, DK, D), lambda c: (c, 0, 0))
    res = pl.pallas_call(
        body, name="hgrn_backward", grid=(nc,),
        out_shape=(jax.ShapeDtypeStruct(dp.shape, dp.dtype), jax.ShapeDtypeStruct((8, D), F32)) + _exchanged(blocks),
        in_specs=[col(0), col(1), col(2), pl.BlockSpec((2, D), lambda c: (0, 0)), row, row, stspec, stspec]
        + [ANYSPEC] * (n + 1),
        out_specs=(pl.BlockSpec((CH, 3 * D), lambda c: (c, 1)), pl.BlockSpec((8, D), lambda c: (0, 0)))
        + (ANYSPEC,) * n,
        scratch_shapes=_comm_sems(n, False),
        input_output_aliases={8 + n: 0},
        compiler_params=_params("arbitrary"),
    )(p, p, p, logits, do, dqe, gt, st_prev, *blocks, dp)
    return res[:2] + (res[2:],)


def in_proj_backward(dp, w_all, x, dx1, gain, sc, sh):
    t = x.shape[0]
    tm = min(512, t)

    def body(dp_ref, w_ref, x_ref, dx1_ref, g_ref, sc_ref, sh_ref, gx_ref, acc_ref):
        @pl.when(pl.program_id(0) == 0)
        def _():
            acc_ref[...] = jnp.zeros_like(acc_ref)

        dh = jnp.zeros((tm, D), F32)
        for j in range(NDEV):
            dh = dh + _dot(dp_ref[:, j * D:(j + 1) * D], w_ref[DP_SPLIT[j]], _NT)
        _, vjp_h = jax.vjp(f_modulate, x_ref[...], g_ref[...], sc_ref[...], sh_ref[...])
        dx, dg, dsc, dsh = vjp_h(dh)
        gx_ref[...] = dx1_ref[...] + dx
        acc_ref[0:1, :] += dg
        acc_ref[1:2, :] += dsc
        acc_ref[2:3, :] += dsh

    row = pl.BlockSpec((tm, D), lambda m: (m, 0))
    vec = pl.BlockSpec((1, D), lambda m: (0, 0))
    return pl.pallas_call(
        body, name="in_proj_backward", grid=(t // tm,),
        out_shape=(jax.ShapeDtypeStruct((t, D), F32), jax.ShapeDtypeStruct((8, D), F32)),
        in_specs=[pl.BlockSpec((tm, NDEV * D), lambda m: (m, 0)), VSPEC, row, row, vec, vec, vec],
        out_specs=(row, pl.BlockSpec((8, D), lambda m: (0, 0))),
        compiler_params=_params("arbitrary"),
    )(dp, w_all, x, dx1, gain, sc, sh)


def weight_grad(a, b, nblk, ka, bn, a_blocked, name, b_col=lambda n: n, after=None):
    t = a.shape[0]
    tk = min(2048, t)
    nk = t // tk
    extra = [] if after is None else [after]

    def body(*refs):
        a_ref, b_ref = refs[:2]
        f_ref, h_ref, acc = refs[2 + len(extra):]
        k = pl.program_id(1)

        @pl.when(k == 0)
        def _():
            acc[...] = jnp.zeros_like(acc)

        acc[...] += _dot(a_ref[...], b_ref[...], _TN)

        @pl.when(k == nk - 1)
        def _():
            f_ref[...] = acc[...]
            h_ref[...] = acc[...].astype(BF16)

    a_idx = (lambda n, k: (k, n)) if a_blocked else (lambda n, k: (k, 0))
    b_idx = (lambda n, k: (k, 0)) if a_blocked else (lambda n, k: (k, b_col(n)))
    out = pl.BlockSpec((None, ka, bn), lambda n, k: (n, 0, 0))
    return pl.pallas_call(
        body, name=name, grid=(nblk, nk),
        out_shape=(jax.ShapeDtypeStruct((nblk, ka, bn), F32), jax.ShapeDtypeStruct((nblk, ka, bn), BF16)),
        in_specs=[pl.BlockSpec((tk, ka), a_idx), pl.BlockSpec((tk, bn), b_idx)] + [ANYSPEC] * len(extra),
        out_specs=(out, out),
        scratch_shapes=[pltpu.VMEM((ka, bn), F32)],
        compiler_params=_params("parallel", "arbitrary"),
    )(a, b, *extra)


def ada_backward(call_t, dmod_cols, w, m, v):
    def body(c_ref, d_ref, w_ref, m_ref, v_ref, g_ref, dl_ref, nm_ref, nv_ref):
        ct, dm = c_ref[...], d_ref[...]
        g = ct[:, 0:1] * dm[0:1, :]
        for r in range(1, NDEV):
            g = g + ct[:, r:r + 1] * dm[r:r + 1, :]
        g_ref[...] = g
        dl_ref[...], nm_ref[...], nv_ref[...] = _adamw(w_ref[...], g, m_ref[...], v_ref[...])

    br = 256
    blk = pl.BlockSpec((br, w.shape[1]), lambda i: (i, 0))
    s = jax.ShapeDtypeStruct(w.shape, F32)
    return pl.pallas_call(
        body, name="ada_backward", grid=(w.shape[0] // br,), out_shape=(s, s, s, s),
        in_specs=[pl.BlockSpec((br, NDEV), lambda i: (i, 0)), pl.BlockSpec(dmod_cols.shape, lambda i: (0, 0)), blk,
                  blk, blk],
        out_specs=(blk, blk, blk, blk), compiler_params=_params("parallel"),
    )(call_t, dmod_cols, w, m, v)


def adamw_small(total, ddw_mine, recipes, ws, ms, vs):
    n = len(ws)

    def body(*refs):
        tot, ddw = refs[0], refs[1]
        w_refs, m_refs, v_refs = refs[2:2 + n], refs[2 + n:2 + 2 * n], refs[2 + 2 * n:2 + 3 * n]
        outs = refs[2 + 3 * n:2 + 7 * n]
        loss_ref = refs[2 + 7 * n]
        for i, rec in enumerate(recipes):
            if rec == "dw":
                g = ddw[...]
            elif isinstance(rec, tuple):
                g = tot[rec[0]:rec[1], :]
            else:
                g = jnp.concatenate([tot[r:r + 1, :] for r in rec], axis=1) if len(rec) > 1 else tot[rec[0]:rec[0] + 1, :]
            dl, nm, nv = _adamw(w_refs[i][...], g, m_refs[i][...], v_refs[i][...])
            outs[4 * i][...] = g
            outs[4 * i + 1][...] = dl
            outs[4 * i + 2][...] = nm
            outs[4 * i + 3][...] = nv
        loss_ref[...] = tot[LOSS_ROW:LOSS_ROW + 1, 0:128]

    shapes = []
    for w in ws:
        shapes += [jax.ShapeDtypeStruct(w.shape, F32)] * 4
    res = pl.pallas_call(
        body, name="adamw_small", out_shape=tuple(shapes) + (jax.ShapeDtypeStruct((1, 128), F32),),
        in_specs=[VSPEC] * (2 + 3 * n), out_specs=(VSPEC,) * (4 * n + 1), compiler_params=_params(),
    )(total, ddw_mine, *ws, *ms, *vs)
    return [res[4 * i:4 * i + 4] for i in range(n)], res[4 * n]


def reduce_and_adamw(sel, w, g_alls, g_recvs, m, v, name):
    r, c = w.shape
    br = min(256, r)
    npair = len(g_alls)

    def body(sel_ref, w_ref, *refs):
        go_refs, gr_refs = refs[:npair], refs[npair:2 * npair]
        m_ref, v_ref, g_ref, dl_ref, nm_ref, nv_ref = refs[2 * npair:]
        g = None
        for i in range(npair):
            gi = go_refs[i][...]
            for k in range(NDEV - 1):
                gi = gi + gr_refs[i][k].astype(F32)
            g = gi if g is None else jnp.where(sel_ref[1] == i, gi, g)
        g_ref[...] = g
        dl_ref[...], nm_ref[...], nv_ref[...] = _adamw(w_ref[...], g, m_ref[...], v_ref[...])

    blk = pl.BlockSpec((br, c), lambda i, sel_ref: (i, 0))
    own = pl.BlockSpec((None, br, c), lambda i, sel_ref: (sel_ref[0], i, 0))
    recv = pl.BlockSpec((NDEV - 1, br, c), lambda i, sel_ref: (0, i, 0))
    s = jax.ShapeDtypeStruct(w.shape, F32)
    return pl.pallas_call(
        body, name=name, out_shape=(s, s, s, s),
        grid_spec=pltpu.PrefetchScalarGridSpec(
            num_scalar_prefetch=1, grid=(r // br,),
            in_specs=[blk] + [own] * npair + [recv] * npair + [blk, blk],
            out_specs=(blk, blk, blk, blk)),
        compiler_params=_params("parallel"),
    )(sel, w, *g_alls, *g_recvs, m, v)


def kernel(x, c, w_ada, b_ada, pre_norm_tm, post_norm_tm, pre_norm_cm, post_norm_cm, w_in, b_in, hg_lb_logits, hg_norm, conv_dw, conv_db, conv_ln_g, conv_ln_b, w_br_a, w_br_b, w_out, w_ff1, w_ff2, loss_target, m_w_ada, m_b_ada, m_pre_norm_tm, m_post_norm_tm, m_pre_norm_cm, m_post_norm_cm, m_w_in, m_b_in, m_hg_lb_logits, m_hg_norm, m_conv_dw, m_conv_db, m_conv_ln_g, m_conv_ln_b, m_w_br_a, m_w_br_b, m_w_out, m_w_ff1, m_w_ff2, v_w_ada, v_b_ada, v_pre_norm_tm, v_post_norm_tm, v_pre_norm_cm, v_post_norm_cm, v_w_in, v_b_in, v_hg_lb_logits, v_hg_norm, v_conv_dw, v_conv_db, v_conv_ln_g, v_conv_ln_b, v_w_br_a, v_w_br_b, v_w_out, v_w_ff1, v_w_ff2):
    t = x.shape[1]
    me = 4 * lax.axis_index("x") + 2 * lax.axis_index("y") + lax.axis_index("c")
    xs = x[0]
    tgt = loss_target[0]

    mod, call, dw_all = ada_forward(c, w_ada[0], b_ada, conv_dw[0])
    sh1, sc1, g1, sh2, sc2, g2 = [mod[:, i * D:(i + 1) * D] for i in range(6)]
    dw = jnp.pad(dw_all.transpose(1, 0, 2).reshape(KW, D), ((0, HALO - KW), (0, 0)))

    p, h, win_all = in_proj(xs, pre_norm_tm, sc1, sh1, w_in[0].astype(BF16), b_in)
    yc, act, (w1_all,) = conv_forward(p, dw, conv_db, conv_ln_g, conv_ln_b, [w_ff1[0].astype(BF16)])
    intra, qe, st_prev, dec, (w2_all, wa_all, wb_all, wo_all) = hgrn_local(
        p, hg_lb_logits, [w_ff2[0].astype(BF16), w_br_a[0].astype(BF16), w_br_b[0].astype(BF16),
                          w_out[0].astype(BF16)])
    wa, wb, wo = wa_all.reshape(D, D), wb_all.reshape(D, D), wo_all.reshape(D, D)
    x1 = mix_forward(xs, p, qe, intra, st_prev, act, wa, wb, wo, hg_norm, g1, post_norm_tm)

    dx1, h2, r, dz, dy2, acc_ffn = ffn_forward_backward(x1, tgt, w1_all, w2_all, pre_norm_cm, sc2, sh2, g2,
                                                        post_norm_cm)
    rows = D // NDEV
    g1_f, g1_h = weight_grad(h2, dz, NDEV, D, DFF // NDEV, False, "grad_w_ff1")
    g2_f, g2_h = weight_grad(r, dy2, 4, D, D, True, "grad_w_ff2")
    g2_f, g2_h = g2_f.reshape(NDEV, DFF // NDEV, D), g2_h.reshape(NDEV, DFF // NDEV, D)

    (dp, do, dqe, gt, dact, a16, mg16, dya, dyb, dy, acc_mix) = mix_backward(
        xs, p, qe, intra, st_prev, act, wa, wb, wo, hg_norm, g1, post_norm_tm, dx1, dec)
    ga_f, ga_h = weight_grad(a16, dya, 1, D, D, False, "grad_w_br_a")
    gb_f, gb_h = weight_grad(act, dyb, 1, D, D, False, "grad_w_br_b")
    go_f, go_h = weight_grad(mg16, dy, 1, D, D, False, "grad_w_out")
    dp, acc_conv, ddw, (r_ff1, r_ff2) = conv_backward(p, yc, dact, dw, conv_ln_g, conv_ln_b, [g1_h, g2_h], dp)
    dp, acc_hg, (r_a, r_b, r_o) = hgrn_backward(
        p, hg_lb_logits, do, dqe, gt, st_prev,
        [ga_h.reshape(NDEV, rows, D), gb_h.reshape(NDEV, rows, D), go_h.reshape(NDEV, rows, D)], dp)
    dp_col = lambda n: jnp.where(n < 3, n + 3, jnp.where(n == 3, 0, jnp.where(n < 6, n + 2, n - 5)))
    even, odd = [0, 2, 4, 6], [1, 3, 5, 7]
    gin_f0, gin_h0 = weight_grad(h, dp, 4, D, D, False, "grad_w_in_even", b_col=lambda j: dp_col(2 * j))
    send0, recv0, thru0, landing0, token0 = exchange_start(gin_h0, even, "exchange_start_even")
    gin_f1, gin_h1 = weight_grad(h, dp, 4, D, D, False, "grad_w_in_odd", b_col=lambda j: dp_col(2 * j + 1),
                                 after=token0)
    send1, recv1, thru1, landing1, token1 = exchange_start(gin_h1, odd, "exchange_start_odd")
    grad_x, acc_in = in_proj_backward(dp, win_all, xs, dx1, pre_norm_tm, sc1, sh1 + token1[0:1, 0:1])

    own = [ga_f.reshape(NDEV, rows, D), gb_f.reshape(NDEV, rows, D), go_f.reshape(NDEV, rows, D), g1_f, g2_f]
    recv = [r_a, r_b, r_o, r_ff1, r_ff2]
    big = {}
    names = ["w_br_a", "w_br_b", "w_out", "w_ff1", "w_ff2"]
    ws = [w_br_a, w_br_b, w_out, w_ff1, w_ff2]
    ms = [m_w_br_a, m_w_br_b, m_w_out, m_w_ff1, m_w_ff2]
    vs = [v_w_br_a, v_w_br_b, v_w_out, v_w_ff1, v_w_ff2]
    sel = jnp.stack([me, 0]).astype(jnp.int32)
    for i, nm in enumerate(names):
        big[nm] = [o[None] for o in reduce_and_adamw(sel, ws[i][0], [own[i]], [recv[i]], ms[i][0], vs[i][0],
                                                     "adamw_" + nm)]
    big, grad_x, acc_in = lax.optimization_barrier((big, grad_x, acc_in))
    r_in0 = exchange_wait(send0, recv0, thru0, landing0, acc_in, even, "exchange_wait_even")
    r_in1 = exchange_wait(send1, recv1, thru1, landing1, acc_in, odd, "exchange_wait_odd")
    r_in0, r_in1, acc_in = lax.optimization_barrier((r_in0, r_in1, acc_in))
    sel_in = jnp.stack([me // 2, me % 2]).astype(jnp.int32)
    big["w_in"] = [o[None] for o in reduce_and_adamw(sel_in, w_in[0], [gin_f0, gin_f1], [r_in0, r_in1], m_w_in[0],
                                                     v_w_in[0], "adamw_w_in")]

    dmod_rows = [2, 1, 8, 19, 18, 20]
    total, kept = gather_and_sum_rows([acc_in, acc_mix, acc_ffn, acc_hg, acc_conv, ddw], dmod_rows)
    dmod_all = kept[:, 0:6, :].reshape(NDEV, 6 * D)
    wcols = w_ada.shape[2]
    gwa, dwa, nmwa, nvwa = ada_backward(call.T, lax.dynamic_slice_in_dim(dmod_all, me * wcols, wcols, axis=1),
                                        w_ada[0], m_w_ada[0], v_w_ada[0])
    ddw_mine = lax.dynamic_slice_in_dim(total[40:40 + KW], me * (D // NDEV), D // NDEV, axis=1)

    small_names = ["b_ada", "pre_norm_tm", "post_norm_tm", "pre_norm_cm", "post_norm_cm", "b_in", "hg_lb_logits",
                   "hg_norm", "conv_db", "conv_ln_g", "conv_ln_b", "conv_dw"]
    recipes = [dmod_rows, [0], [9], [16], [17], [26, 27, 28, 11, 35, 36, 12, 13], (24, 26), [10], [32], [33], [34],
               "dw"]
    small_w = [b_ada, pre_norm_tm, post_norm_tm, pre_norm_cm, post_norm_cm, b_in, hg_lb_logits, hg_norm, conv_db,
               conv_ln_g, conv_ln_b, conv_dw[0]]
    small_m = [m_b_ada, m_pre_norm_tm, m_post_norm_tm, m_pre_norm_cm, m_post_norm_cm, m_b_in, m_hg_lb_logits,
               m_hg_norm, m_conv_db, m_conv_ln_g, m_conv_ln_b, m_conv_dw[0]]
    small_v = [v_b_ada, v_pre_norm_tm, v_post_norm_tm, v_pre_norm_cm, v_post_norm_cm, v_b_in, v_hg_lb_logits,
               v_hg_norm, v_conv_db, v_conv_ln_g, v_conv_ln_b, v_conv_dw[0]]
    small_out, loss_row = adamw_small(total, ddw_mine, recipes, small_w, small_m, small_v)
    loss = loss_row[0, 0]
    sm = {nm: list(o) for nm, o in zip(small_names, small_out)}
    sm["conv_dw"] = [o[None] for o in sm["conv_dw"]]

    order = ["w_ada", "b_ada", "pre_norm_tm", "post_norm_tm", "pre_norm_cm", "post_norm_cm", "w_in", "b_in",
             "hg_lb_logits", "hg_norm", "conv_dw", "conv_db", "conv_ln_g", "conv_ln_b", "w_br_a", "w_br_b", "w_out",
             "w_ff1", "w_ff2"]
    res = dict(sm)
    res.update(big)
    res["w_ada"] = [gwa[None], dwa[None], nmwa[None], nvwa[None]]
    outs = [loss, grad_x[None]]
    for j in range(4):
        outs += [res[nm][j] for nm in order]
    return tuple(outs)
```

```python
import jax
import jax.numpy as jnp
from jax import lax
from jax.experimental import pallas as pl
from jax.experimental.pallas import tpu as pltpu

F32 = jnp.float32
BF16 = jnp.bfloat16
MESH = pl.DeviceIdType.MESH

D = 1024
HEADS = 8
DK = 128
LANE = 128
CH = 128
LEVELS = 7
KW = 31
HALO = 32
DFF = 4096
NDEV = 8
EPS = 1e-6
DP_SPLIT = (3, 6, 7, 0, 1, 2, 4, 5)
LOSS_ROW = 21
ADAM_LR, ADAM_B1, ADAM_B2, ADAM_EPS, ADAM_WD, ADAM_STEP = 0.001, 0.9, 0.999, 1e-08, 0.01, 10
VMEM_LIMIT = 58 * 1024 * 1024

_NN = (((1,), (0,)), ((), ()))
_NT = (((1,), (1,)), ((), ()))
_TN = (((0,), (0,)), ((), ()))

VSPEC = pl.BlockSpec(memory_space=pltpu.VMEM)
ANYSPEC = pl.BlockSpec(memory_space=pl.ANY)


def _params(*sem):
    return pltpu.CompilerParams(dimension_semantics=sem or None, vmem_limit_bytes=VMEM_LIMIT)


def _dot(a, b, dims):
    return lax.dot_general(a, b, dims, preferred_element_type=F32)


@jax.custom_vjp
def mm_nn(a, b):
    return _dot(a.astype(BF16), b.astype(BF16), _NN)


def _mm_nn_fwd(a, b):
    ab, bb = a.astype(BF16), b.astype(BF16)
    return _dot(ab, bb, _NN), (ab, bb)


def _mm_nn_bwd(res, ct):
    ab, bb = res
    cb = ct.astype(BF16)
    return _dot(cb, bb, _NT), _dot(ab, cb, _TN)


mm_nn.defvjp(_mm_nn_fwd, _mm_nn_bwd)


@jax.custom_vjp
def mm_nt(a, b):
    return _dot(a.astype(BF16), b.astype(BF16), _NT)


def _mm_nt_fwd(a, b):
    ab, bb = a.astype(BF16), b.astype(BF16)
    return _dot(ab, bb, _NT), (ab, bb)


def _mm_nt_bwd(res, ct):
    ab, bb = res
    cb = ct.astype(BF16)
    return _dot(cb, bb, _NN), _dot(cb, ab, _TN)


mm_nt.defvjp(_mm_nt_fwd, _mm_nt_bwd)


@jax.custom_vjp
def mm_tn(a, b):
    return _dot(a.astype(BF16), b.astype(BF16), _TN)


def _mm_tn_fwd(a, b):
    ab, bb = a.astype(BF16), b.astype(BF16)
    return _dot(ab, bb, _TN), (ab, bb)


def _mm_tn_bwd(res, ct):
    ab, bb = res
    cb = ct.astype(BF16)
    return _dot(bb, cb, _NT), _dot(ab, cb, _NN)


mm_tn.defvjp(_mm_tn_fwd, _mm_tn_bwd)


def _rms(x):
    return x * lax.rsqrt(jnp.mean(x * x, axis=-1, keepdims=True) + EPS)


_sigmoid = jax.nn.sigmoid


def _silu(x):
    return x * _sigmoid(x)


def f_modulate(x, gain, sc, sh):
    return _rms(x) * gain * (1.0 + sc) + sh


def f_residual(x, y, gate, gain):
    return x + gate * (_rms(y) * gain)


def f_merge(ga, gb, ya, yb):
    return _sigmoid(ga) * ya + _sigmoid(gb) * yb


def f_head_out(o, og, hg):
    heads = [_rms(o[:, h * DK:(h + 1) * DK]) for h in range(HEADS)]
    return jnp.concatenate(heads, axis=1) * hg * _silu(og)


def f_conv_act(u, g, b):
    mu = jnp.mean(u, axis=-1, keepdims=True)
    var = jnp.mean(jnp.square(u - mu), axis=-1, keepdims=True)
    return _silu((u - mu) * lax.rsqrt(var + EPS) * g + b)


def f_glu(cv, cg):
    return cv * _sigmoid(cg)


def _split2(x):
    hi = x.astype(BF16)
    return hi, (x - hi.astype(F32)).astype(BF16)


def _tri(transposed):
    i = lax.broadcasted_iota(jnp.int32, (CH, CH), 1 if transposed else 0)
    t = lax.broadcasted_iota(jnp.int32, (CH, CH), 0 if transposed else 1)
    return jnp.where(t <= i, 1.0, 0.0).astype(BF16)


def _blocks3(x, rows):
    return x.reshape(CH // rows, rows, x.shape[-1])


def _mid_broadcast(b, lev):
    h = 1 << (LEVELS - 1 - lev)
    if h >= 4:
        x3 = _blocks3(b, 2 * h)
        return jnp.broadcast_to(x3[:, h - 1:h, :], x3.shape).reshape(b.shape)
    x3 = _blocks3(b, 8)
    sub = lax.broadcasted_iota(jnp.int32, x3.shape, 1)
    out = None
    for first in range(0, 8, 2 * h):
        piece = jnp.broadcast_to(x3[:, first + h - 1:first + h, :], x3.shape)
        out = piece if out is None else jnp.where(sub >= first, piece, out)
    return out.reshape(b.shape)


def _mid_scatter(d, lev):
    h = 1 << (LEVELS - 1 - lev)
    if h >= 4:
        x3 = _blocks3(d, 2 * h)
        row = lax.broadcasted_iota(jnp.int32, x3.shape, 1)
        total = jnp.sum(x3, axis=1, keepdims=True)
        return jnp.where(row == h - 1, total, 0.0).reshape(d.shape)
    x3 = _blocks3(d, 8)
    sub = lax.broadcasted_iota(jnp.int32, x3.shape, 1)
    out = jnp.zeros_like(x3)
    for first in range(0, 8, 2 * h):
        inside = (sub >= first) & (sub < first + 2 * h)
        total = jnp.sum(jnp.where(inside, x3, 0.0), axis=1, keepdims=True)
        out = jnp.where(sub == first + h - 1, total, out)
    return out.reshape(d.shape)


@jax.custom_vjp
def decay_sums(g):
    hi, lo = _split2(g)
    tri = _tri(False)
    b = _dot(tri, hi, _NN) + _dot(tri, lo, _NN)
    return (b,) + tuple(b - _mid_broadcast(b, lev) for lev in range(LEVELS))


def _decay_sums_fwd(g):
    return decay_sums(g), None


def _decay_sums_bwd(_, cts):
    db = cts[0]
    for lev in range(LEVELS):
        db = db + cts[1 + lev] - _mid_scatter(cts[1 + lev], lev)
    hi, lo = _split2(db)
    tri = _tri(True)
    return (_dot(tri, hi, _NN) + _dot(tri, lo, _NN),)


decay_sums.defvjp(_decay_sums_fwd, _decay_sums_bwd)


def _score_masks():
    i = lax.broadcasted_iota(jnp.int32, (CH, CH), 0)
    j = lax.broadcasted_iota(jnp.int32, (CH, CH), 1)
    masks = [i == j]
    for lev in range(LEVELS):
        sh = LEVELS - 1 - lev
        same = (i >> (sh + 1)) == (j >> (sh + 1))
        masks.append(same & (((i >> sh) & 1) == 1) & (((j >> sh) & 1) == 0))
    return [jnp.where(m, 1.0, 0.0) for m in masks]


def f_hgrn_chunk(q_r, f_r, v, logits):
    l0, l1 = logits[0:1, :], logits[1:2, :]
    mx = lax.stop_gradient(jnp.maximum(l0, l1))
    e0, e1 = jnp.exp(l0 - mx), jnp.exp(l1 - mx)
    lb = e0 / (e0 + e1)
    q = _silu(q_r)
    f = lb + (1.0 - lb) * _sigmoid(f_r)
    k = 1.0 - f
    sums = decay_sums(jnp.log(f))
    b = sums[0]
    btot = b[CH - 1:CH, :]
    qe = q * jnp.exp(b)
    ke = k * jnp.exp(btot - b)
    dec = jnp.exp(btot)
    qs, ks = [q], [k]
    row = lax.broadcasted_iota(jnp.int32, b.shape, 0)
    for lev in range(LEVELS):
        upper = ((row >> (LEVELS - 1 - lev)) & 1) == 1
        e = sums[1 + lev]
        qs.append(q * jnp.exp(jnp.where(upper, e, 0.0)))
        ks.append(k * jnp.exp(jnp.where(upper, 0.0, -e)))
    masks = _score_masks()
    intra, ut = [], []
    for h in range(HEADS):
        sl = slice(h * DK, (h + 1) * DK)
        sc = None
        for lev in range(LEVELS + 1):
            s = mm_nt(qs[lev][:, sl], ks[lev][:, sl]) * masks[lev]
            sc = s if sc is None else sc + s
        intra.append(mm_nn(sc, v[:, sl]))
        ut.append(mm_tn(v[:, sl], ke[:, sl]))
    return jnp.concatenate(intra, axis=1), qe, jnp.concatenate(ut, axis=1), dec


def _inter(qe_b, st_ref, rows):
    out = []
    for ci in range(rows // CH):
        st = st_ref[ci].astype(BF16)
        heads = [_dot(qe_b[ci * CH:(ci + 1) * CH, h * DK:(h + 1) * DK], st[:, h * DK:(h + 1) * DK], _NT)
                 for h in range(HEADS)]
        out.append(jnp.concatenate(heads, axis=1))
    return jnp.concatenate(out, axis=0)


def _shift_stack(src, dst, rows):
    dst[0, 0:rows, :] = src[0:rows, :]
    for b in range(1, 8):
        dst[b, 0:rows - 8, :] = src[pl.ds(b, rows - 8), :]


def _shifted(stack, offset, rows, lanes=slice(None)):
    return stack[offset % 8, pl.ds(8 * (offset // 8), rows), lanes]


def _adamw(w, g, m, v):
    m = ADAM_B1 * m + (1.0 - ADAM_B1) * g
    v = ADAM_B2 * v + (1.0 - ADAM_B2) * jnp.square(g)
    m_hat = m / (1.0 - ADAM_B1 ** ADAM_STEP)
    v_hat = v / (1.0 - ADAM_B2 ** ADAM_STEP)
    delta = -ADAM_LR * (m_hat / (jnp.sqrt(v_hat) + ADAM_EPS) + ADAM_WD * w)
    return delta, m, v


def _me():
    return lax.axis_index("x"), lax.axis_index("y"), lax.axis_index("c")


def _peer(k):
    x, y, c = _me()
    mask = k + 1
    px = (1 - x) if (mask >> 2) & 1 else x
    py = (1 - y) if (mask >> 1) & 1 else y
    pc = (1 - c) if mask & 1 else c
    return (px, py, pc), 4 * px + 2 * py + pc


def ada_forward(c, w_ada, b_ada, dw):
    wcols = w_ada.shape[1]

    def body(c_ref, w_ref, b_ref, dw_ref, mod_ref, call_ref, dwall_ref, part_ref, modp_ref, send_sems, recv_sems):
        x, y, cc = _me()
        me = 4 * x + 2 * y + cc
        call_ref[me] = _silu(c_ref[...])
        dwall_ref[me] = dw_ref[...]
        sends = []
        for k in range(NDEV - 1):
            dev, pidx = _peer(k)
            cp = pltpu.make_async_remote_copy(dwall_ref.at[me], dwall_ref.at[me], send_sems.at[2 * (NDEV - 1) + k],
                                              recv_sems.at[2 * (NDEV - 1) + k], device_id=dev, device_id_type=MESH)
            cp.start()
            sends.append(cp)
        for k in range(NDEV - 1):
            dev, _ = _peer(k)
            cp = pltpu.make_async_remote_copy(call_ref.at[me], call_ref.at[me], send_sems.at[k], recv_sems.at[k],
                                              device_id=dev, device_id_type=MESH)
            cp.start()
            sends.append(cp)
        for k in range(NDEV - 1):
            _, pidx = _peer(k)
            pltpu.make_async_remote_copy(call_ref.at[pidx], call_ref.at[pidx], send_sems.at[k], recv_sems.at[k],
                                         device_id=_peer(k)[0], device_id_type=MESH).wait_recv()
        call = jnp.concatenate([call_ref[r] for r in range(NDEV)], axis=0)
        part = _dot(call.astype(BF16), w_ref[...].astype(BF16), _NN)
        for r in range(NDEV):
            part_ref[r] = part[r:r + 1, :]
        modp_ref[me] = part_ref[me]
        for k in range(NDEV - 1):
            dev, pidx = _peer(k)
            cp = pltpu.make_async_remote_copy(part_ref.at[pidx], modp_ref.at[me], send_sems.at[NDEV - 1 + k],
                                              recv_sems.at[NDEV - 1 + k], device_id=dev, device_id_type=MESH)
            cp.start()
            sends.append(cp)
        for k in range(NDEV - 1):
            dev, pidx = _peer(k)
            pltpu.make_async_remote_copy(part_ref.at[pidx], modp_ref.at[pidx], send_sems.at[NDEV - 1 + k],
                                         recv_sems.at[NDEV - 1 + k], device_id=dev, device_id_type=MESH).wait_recv()
        for k in range(NDEV - 1):
            dev, pidx = _peer(k)
            pltpu.make_async_remote_copy(dwall_ref.at[pidx], dwall_ref.at[pidx], send_sems.at[2 * (NDEV - 1) + k],
                                         recv_sems.at[2 * (NDEV - 1) + k], device_id=dev,
                                         device_id_type=MESH).wait_recv()
        for cp in sends:
            cp.wait_send()
        mod_ref[...] = jnp.concatenate([modp_ref[r] for r in range(NDEV)], axis=1) + b_ref[...]

    mod, call, dw_all = pl.pallas_call(
        body, name="ada_forward",
        out_shape=(jax.ShapeDtypeStruct((1, NDEV * wcols), F32), jax.ShapeDtypeStruct((NDEV, 1, D), F32),
                   jax.ShapeDtypeStruct((NDEV,) + dw.shape, F32)),
        in_specs=[VSPEC] * 4, out_specs=(VSPEC,) * 3,
        scratch_shapes=[pltpu.VMEM((NDEV, 1, wcols), F32), pltpu.VMEM((NDEV, 1, wcols), F32),
                        pltpu.SemaphoreType.DMA((3 * (NDEV - 1),)), pltpu.SemaphoreType.DMA((3 * (NDEV - 1),))],
        compiler_params=_params(),
    )(c, w_ada, b_ada, dw)
    return mod, call.reshape(NDEV, D), dw_all


def _comm_sems(n, local):
    sems = [pltpu.SemaphoreType.DMA((7 * n,)), pltpu.SemaphoreType.DMA((7 * n,))]
    return sems + ([pltpu.SemaphoreType.DMA((n,))] if local else [])


def _gather2_ops(ins, outs, send_sems, recv_sems, local_sems):
    n = len(ins)
    x, y, c = _me()
    me, sibling = (x, y, c), (x, y, 1 - c)
    chips = [(1 - x, y), (x, 1 - y), (1 - x, 1 - y)]

    def slot(p):
        return 4 * p[0] + 2 * p[1] + p[2]

    def copy(a, k, block, to, src=None):
        return pltpu.make_async_remote_copy(
            src_ref=outs[a].at[slot(block)] if src is None else src, dst_ref=outs[a].at[slot(block)],
            send_sem=send_sems.at[a * 7 + k], recv_sem=recv_sems.at[a * 7 + k], device_id=to, device_id_type=MESH)

    def local(a):
        return pltpu.make_async_copy(ins[a], outs[a].at[slot(me)], local_sems.at[a])

    def first(a):
        return [copy(a, 0, me, sibling, src=ins[a])] + [copy(a, 1 + j, me, (*chip, c), src=ins[a])
                                                        for j, chip in enumerate(chips)]

    def passed(a):
        return [copy(a, 4 + j, (*chip, c), sibling) for j, chip in enumerate(chips)]

    def start():
        for a in range(n):
            local(a).start()
            for cp in first(a):
                cp.start()

    def forward():
        for j, chip in enumerate(chips):
            for a in range(n):
                copy(a, 1 + j, (*chip, c), me).wait_recv()
                passed(a)[j].start()

    def finish():
        for a in range(n):
            copy(a, 0, sibling, me).wait_recv()
            for j, chip in enumerate(chips):
                copy(a, 4 + j, (*chip, 1 - c), me).wait_recv()
        for a in range(n):
            for cp in first(a) + passed(a):
                cp.wait_send()
            local(a).wait()

    return start, forward, finish


def _exchange_ops(ins, outs, send_sems, recv_sems):
    n = len(ins)

    def copy(a, k):
        dev, pidx = _peer(k)
        return pltpu.make_async_remote_copy(ins[a].at[pidx], outs[a].at[k], send_sems.at[a * 7 + k],
                                            recv_sems.at[a * 7 + k], device_id=dev, device_id_type=MESH)

    def start():
        for k in range(NDEV - 1):
            for a in range(n):
                copy(a, k).start()

    def finish():
        for k in range(NDEV - 1):
            for a in range(n):
                copy(a, k).wait_recv()
        for k in range(NDEV - 1):
            for a in range(n):
                copy(a, k).wait_send()

    return start, finish


def _gathered(shards):
    return tuple(jax.ShapeDtypeStruct((NDEV,) + s.shape, s.dtype) for s in shards)


def _exchanged(blocks):
    return tuple(jax.ShapeDtypeStruct((NDEV - 1,) + b.shape[1:], b.dtype) for b in blocks)


def _owner_copy(src_ref, land_ref, send_sems, recv_sems, gi, n):
    x, y, c = _me()
    k = jnp.bitwise_xor(4 * x + 2 * y + c, n) - 1
    return pltpu.make_async_remote_copy(src_ref.at[gi], land_ref.at[k], send_sems.at[gi], recv_sems.at[k],
                                        device_id=(n >> 2, (n >> 1) & 1, n & 1), device_id_type=MESH)


def exchange_start(blocks, owners, name):
    landing = lax.empty((NDEV - 1,) + blocks.shape[1:], blocks.dtype)
    hbm = pl.BlockSpec(memory_space=pltpu.HBM)
    sem = pl.BlockSpec(memory_space=pltpu.SEMAPHORE)

    def body(src_ref, land_ref, send_sems, recv_sems, src_thru, land_thru, token):
        x, y, c = _me()
        for gi, n in enumerate(owners):
            @pl.when(4 * x + 2 * y + c != n)
            def _(gi=gi, n=n):
                _owner_copy(src_ref, land_ref, send_sems, recv_sems, gi, n).start()
        token[...] = jnp.zeros_like(token)

    return pl.pallas_call(
        body, name=name,
        out_shape=(pltpu.SemaphoreType.DMA((len(owners),)), pltpu.SemaphoreType.DMA((NDEV - 1,)),
                   pltpu.HBM(blocks.shape, blocks.dtype), pltpu.HBM(landing.shape, landing.dtype),
                   jax.ShapeDtypeStruct((8, 128), F32)),
        in_specs=(hbm, hbm), out_specs=(sem, sem, hbm, hbm, VSPEC), input_output_aliases={0: 2, 1: 3},
        compiler_params=pltpu.CompilerParams(has_side_effects=pltpu.SideEffectType.DATAFLOW_SIDE_EFFECTING),
    )(pltpu.with_memory_space_constraint(blocks, pltpu.HBM), pltpu.with_memory_space_constraint(landing, pltpu.HBM))


def exchange_wait(send_sems, recv_sems, src_thru, land_thru, after, owners, name):
    hbm = pl.BlockSpec(memory_space=pltpu.HBM)
    sem = pl.BlockSpec(memory_space=pltpu.SEMAPHORE)

    def body(src_ref, land_ref, send_sems, recv_sems, after_ref, src_dead, got_ref):
        x, y, c = _me()
        me = 4 * x + 2 * y + c
        for gi, n in enumerate(owners):
            @pl.when(me != n)
            def _(gi=gi, n=n):
                _owner_copy(src_ref, land_ref, send_sems, recv_sems, gi, n).wait_send()

            @pl.when(me == n)
            def _(gi=gi):
                for k in range(NDEV - 1):
                    pltpu.make_async_remote_copy(src_ref.at[gi], land_ref.at[k], send_sems.at[gi], recv_sems.at[k],
                                                 device_id=_peer(k)[0], device_id_type=MESH).wait_recv()

    return pl.pallas_call(
        body, name=name,
        out_shape=(pltpu.HBM(src_thru.shape, src_thru.dtype), pltpu.HBM(land_thru.shape, land_thru.dtype)),
        in_specs=(hbm, hbm, sem, sem, ANYSPEC), out_specs=(hbm, hbm), input_output_aliases={0: 0, 1: 1},
        compiler_params=pltpu.CompilerParams(has_side_effects=pltpu.SideEffectType.DATAFLOW_SIDE_EFFECTING),
    )(src_thru, land_thru, send_sems, recv_sems, after)[1]


def gather_and_sum_rows(parts, keep_rows):
    n = len(parts)
    offs = [sum(p.shape[0] for p in parts[:i]) for i in range(n)]
    rows = sum(p.shape[0] for p in parts)

    def body(*refs):
        sum_ref, keep_ref, mine_ref, buf_ref, send_sems, recv_sems, local_sems = refs[n:]
        for i in range(n):
            mine_ref[offs[i]:offs[i] + parts[i].shape[0], :] = refs[i][...]
        start, forward, finish = _gather2_ops([mine_ref], [buf_ref], send_sems, recv_sems, local_sems)
        start()
        forward()
        finish()
        total = buf_ref[0]
        for r in range(1, NDEV):
            total = total + buf_ref[r]
        sum_ref[...] = total
        keep_ref[...] = jnp.zeros_like(keep_ref)
        for r in range(NDEV):
            for j, src in enumerate(keep_rows):
                keep_ref[r, j:j + 1, :] = buf_ref[r, src:src + 1, :]

    return pl.pallas_call(
        body, name="gather_and_sum_rows",
        out_shape=(jax.ShapeDtypeStruct((rows, D), F32), jax.ShapeDtypeStruct((NDEV, 8, D), F32)),
        in_specs=[VSPEC] * n, out_specs=(VSPEC, VSPEC),
        scratch_shapes=[pltpu.VMEM((rows, D), F32), pltpu.VMEM((NDEV, rows, D), F32)] + _comm_sems(1, True),
        compiler_params=_params(),
    )(*parts)


def _gather_chips():
    x, y, c = _me()
    north = c == 1
    first = (jnp.where(north, 1 - x, x), jnp.where(north, y, 1 - y))
    second = (jnp.where(north, x, 1 - x), jnp.where(north, 1 - y, y))
    return [first, second, (1 - x, 1 - y)]


def _arrival_order():
    x, y, c = _me()
    first, second, far = _gather_chips()
    devs = [(x, y, c), (x, y, 1 - c), (*first, c), (*second, 1 - c), (*second, c), (*first, 1 - c), (*far, c),
            (*far, 1 - c)]
    return jnp.stack([4 * d[0] + 2 * d[1] + d[2] for d in devs]).astype(jnp.int32)


def in_proj(x, gain, sc, sh, w_shard, b_in):
    t = x.shape[0]
    tm = min(1024, t // 2)
    nm = t // tm
    order = _arrival_order()

    def body(order_ref, x_ref, g_ref, sc_ref, sh_ref, b_ref, wsh_ref, p_ref, h_ref, wall_ref,
             wbuf, h_all, send_sems, recv_sems, local_sems):
        k, m = pl.program_id(0), pl.program_id(1)
        mx, my, mc = _me()
        me, sibling = (mx, my, mc), (mx, my, 1 - mc)
        chips = _gather_chips()

        def slot(d):
            return 4 * d[0] + 2 * d[1] + d[2]

        def copy(sem, block, to, src=None):
            return pltpu.make_async_remote_copy(
                src_ref=wall_ref.at[slot(block)] if src is None else src, dst_ref=wall_ref.at[slot(block)],
                send_sem=send_sems.at[sem], recv_sem=recv_sems.at[sem], device_id=to, device_id_type=MESH)

        own = pltpu.make_async_copy(wsh_ref, wall_ref.at[slot(me)], local_sems.at[0])
        first = [copy(0, me, sibling, src=wsh_ref)] + [copy(1 + j, me, (*q, mc), src=wsh_ref)
                                                       for j, q in enumerate(chips)]
        passed = [copy(4 + j, (*q, mc), sibling) for j, q in enumerate(chips)]

        def load(kk):
            src = wsh_ref if kk == 0 else wall_ref.at[order_ref[kk]]
            return pltpu.make_async_copy(src, wbuf.at[kk % 2], local_sems.at[1 + kk % 2])

        def arrived(kk):
            if kk == 1:
                copy(0, sibling, me).wait_recv()
            elif kk in (2, 4, 6):
                j = kk // 2 - 1
                copy(1 + j, (*chips[j], mc), me).wait_recv()
                passed[j].start()
                if j < 2:
                    first[2 + j].start()
            else:
                j = {3: 1, 5: 0, 7: 2}[kk]
                copy(4 + {1: 0, 0: 1, 2: 2}[j], (*chips[j], 1 - mc), me).wait_recv()

        @pl.when((k == 0) & (m == 0))
        def _():
            own.start()
            for cp in first[:2]:
                cp.start()
            load(0).start()
            load(0).wait()

        for kk in range(1, NDEV):
            @pl.when((k == kk - 1) & (m == nm // 2))
            def _(kk=kk):
                arrived(kk)
                load(kk).start()

            @pl.when((k == kk) & (m == 0))
            def _(kk=kk):
                load(kk).wait()

        rows = pl.ds(pl.multiple_of(m * tm, tm), tm)

        @pl.when(k == 0)
        def _():
            h_all[rows, :] = f_modulate(x_ref[...], g_ref[...], sc_ref[...], sh_ref[...]).astype(BF16)

        p_ref[...] = _dot(h_all[rows, :], wbuf[k % 2], _NN) + b_ref[...]

        @pl.when((k == NDEV - 1) & (m == nm - 1))
        def _():
            for cp in first + passed:
                cp.wait_send()
            own.wait()
            out = pltpu.make_async_copy(h_all, h_ref, local_sems.at[0])
            out.start()
            out.wait()

    vec = pl.BlockSpec((1, D), lambda k, m, o: (0, 0))
    return pl.pallas_call(
        body, name="in_proj",
        out_shape=(jax.ShapeDtypeStruct((t, NDEV * D), F32), jax.ShapeDtypeStruct((t, D), BF16),
                   jax.ShapeDtypeStruct((NDEV, D, D), BF16)),
        grid_spec=pltpu.PrefetchScalarGridSpec(
            num_scalar_prefetch=1, grid=(NDEV, nm),
            in_specs=[pl.BlockSpec((tm, D), lambda k, m, o: (jnp.where(k == 0, m, 0), 0)), vec, vec, vec,
                      pl.BlockSpec((1, D), lambda k, m, o: (0, o[k])), ANYSPEC],
            out_specs=(pl.BlockSpec((tm, D), lambda k, m, o: (m, o[k])), ANYSPEC, ANYSPEC),
            scratch_shapes=[pltpu.VMEM((2, D, D), BF16), pltpu.VMEM((t, D), BF16), pltpu.SemaphoreType.DMA((7,)),
                            pltpu.SemaphoreType.DMA((7,)), pltpu.SemaphoreType.DMA((3,))]),
        compiler_params=_params("arbitrary", "arbitrary"),
    )(order, x, gain, sc, sh, b_in, w_shard)


def hgrn_local(p, logits, shards):
    t = p.shape[0]
    nc = t // CH
    n = len(shards)

    def body(*refs):
        q_ref, f_ref, v_ref, l_ref = refs[:4]
        intra_ref, qe_ref, st_ref, dec_ref = refs[4 + n:8 + n]
        state = refs[8 + 2 * n]
        start, forward, finish = _gather2_ops(refs[4:4 + n], refs[8 + n:8 + 2 * n], *refs[9 + 2 * n:])
        step = pl.program_id(0)

        @pl.when(step == 0)
        def _():
            start()
            state[...] = jnp.zeros_like(state)

        intra, qe, ut, dec = f_hgrn_chunk(q_ref[...], f_ref[...], v_ref[...], l_ref[...])
        intra_ref[...] = intra
        qe_ref[...] = qe.astype(BF16)
        dec_ref[...] = dec
        s = state[...]
        st_ref[...] = s
        state[...] = dec * s + ut
        pl.when(step == (7 * nc) // 8)(forward)
        pl.when(step == nc - 1)(finish)

    col = lambda j: pl.BlockSpec((CH, D), lambda c: (c, j))
    res = pl.pallas_call(
        body, name="hgrn_local", grid=(nc,),
        out_shape=(jax.ShapeDtypeStruct((t, D), F32), jax.ShapeDtypeStruct((t, D), BF16),
                   jax.ShapeDtypeStruct((nc, DK, D), F32), jax.ShapeDtypeStruct((nc, 1, D), F32)) + _gathered(shards),
        in_specs=[col(0), col(1), col(2), pl.BlockSpec((2, D), lambda c: (0, 0))] + [ANYSPEC] * n,
        out_specs=(pl.BlockSpec((CH, D), lambda c: (c, 0)), pl.BlockSpec((CH, D), lambda c: (c, 0)),
                   pl.BlockSpec((None, DK, D), lambda c: (c, 0, 0)), pl.BlockSpec((None, 1, D), lambda c: (c, 0, 0)))
        + (ANYSPEC,) * n,
        scratch_shapes=[pltpu.VMEM((DK, D), F32)] + _comm_sems(n, True),
        compiler_params=_params("arbitrary"),
    )(p, p, p, logits, *shards)
    return res[:4] + (res[4:],)


def _conv_tile(t):
    return min(256, t)


def conv_forward(p, dw, db, ln_g, ln_b, shards):
    t = p.shape[0]
    tm = _conv_tile(t)
    per = tm // HALO
    n = len(shards)
    nt = t // tm

    def body(*refs):
        cv_ref, cg_ref, cvp_ref, cgp_ref, dw_ref, db_ref, g_ref, b_ref = refs[:8]
        yc_ref, act_ref = refs[8 + n:10 + n]
        uext, ush = refs[10 + 2 * n:12 + 2 * n]
        start, forward, finish = _gather2_ops(refs[8:8 + n], refs[10 + n:10 + 2 * n], *refs[12 + 2 * n:])
        step = pl.program_id(0)
        pl.when(step == 0)(start)
        pl.when(step == nt - 1)(forward)
        uext[0:HALO, :] = jnp.where(step == 0, 0.0, f_glu(cvp_ref[...], cgp_ref[...]))
        uext[HALO:HALO + tm, :] = f_glu(cv_ref[...], cg_ref[...])
        _shift_stack(uext, ush, tm + HALO)
        acc = jnp.zeros((tm, D), F32) + db_ref[...]
        for w in range(KW):
            acc = acc + dw_ref[w:w + 1, :] * _shifted(ush, HALO - KW + 1 + w, tm)
        yc_ref[...] = acc
        act_ref[...] = f_conv_act(acc, g_ref[...], b_ref[...]).astype(BF16)
        pl.when(step == nt - 1)(finish)

    vec = pl.BlockSpec((1, D), lambda m: (0, 0))
    prev = lambda j: pl.BlockSpec((HALO, D), lambda m: (jnp.maximum(m * per - 1, 0), j))
    res = pl.pallas_call(
        body, name="conv_forward", grid=(nt,),
        out_shape=(jax.ShapeDtypeStruct((t, D), F32), jax.ShapeDtypeStruct((t, D), BF16)) + _gathered(shards),
        in_specs=[pl.BlockSpec((tm, D), lambda m: (m, 4)), pl.BlockSpec((tm, D), lambda m: (m, 5)), prev(4), prev(5),
                  pl.BlockSpec((HALO, D), lambda m: (0, 0)), vec, vec, vec] + [ANYSPEC] * n,
        out_specs=(pl.BlockSpec((tm, D), lambda m: (m, 0)), pl.BlockSpec((tm, D), lambda m: (m, 0))) + (ANYSPEC,) * n,
        scratch_shapes=[pltpu.VMEM((HALO + tm, D), F32), pltpu.VMEM((8, tm + HALO, D), F32)] + _comm_sems(n, True),
        compiler_params=_params("arbitrary"),
    )(p, p, p, p, dw, db, ln_g, ln_b, *shards)
    return res[0], res[1], res[2:]


def _mix_forward_tile(x_ref, og_ref, ga_ref, gb_ref, qe_ref, intra_ref, st_ref, act_ref, wa_ref, wb_ref, wo_ref,
                      hg_ref, rows):
    o = _inter(qe_ref[...], st_ref, rows) + intra_ref[...]
    a = f_head_out(o, og_ref[...], hg_ref[...])
    ya = _dot(a.astype(BF16), wa_ref[...], _NN)
    yb = _dot(act_ref[...], wb_ref[...], _NN)
    merged = f_merge(ga_ref[...], gb_ref[...], ya, yb)
    y = _dot(merged.astype(BF16), wo_ref[...], _NN)
    return o, a, ya, yb, merged, y


def _mix_specs(tm, tiles=None):
    at = (lambda m: m) if tiles is None else (lambda m: tiles - 1 - m)
    col = lambda j: pl.BlockSpec((tm, D), lambda m: (at(m), j))
    row = pl.BlockSpec((tm, D), lambda m: (at(m), 0))
    per_chunk = lambda rows: pl.BlockSpec((tm // CH, rows, D), lambda m: (at(m), 0, 0))
    return col, row, pl.BlockSpec((1, D), lambda m: (0, 0)), per_chunk


def mix_forward(x, p, qe, intra, st_prev, act, wa, wb, wo, hg, g1, post):
    t = x.shape[0]
    tm = min(512, t)

    def body(x_ref, og_ref, ga_ref, gb_ref, qe_ref, intra_ref, st_ref, act_ref, wa_ref, wb_ref, wo_ref, hg_ref,
             g1_ref, post_ref, x1_ref):
        y = _mix_forward_tile(x_ref, og_ref, ga_ref, gb_ref, qe_ref, intra_ref, st_ref, act_ref, wa_ref, wb_ref,
                              wo_ref, hg_ref, tm)[-1]
        x1_ref[...] = f_residual(x_ref[...], y, g1_ref[...], post_ref[...])

    col, row, vec, per_chunk = _mix_specs(tm)
    return pl.pallas_call(
        body, name="mix_forward", grid=(t // tm,),
        out_shape=jax.ShapeDtypeStruct((t, D), F32),
        in_specs=[row, col(3), col(6), col(7), row, row, per_chunk(DK), row, VSPEC, VSPEC, VSPEC, vec, vec, vec],
        out_specs=row,
        compiler_params=_params("parallel"),
    )(x, p, p, p, qe, intra, st_prev, act, wa, wb, wo, hg, g1, post)


def ffn_forward_backward(x1, target, w1, w2, pre, sc, sh, g2, post):
    t = x1.shape[0]
    tm = min(512, t)
    nb = w1.shape[0]
    fb = w1.shape[2]

    def body(x_ref, tg_ref, w1_ref, w2_ref, pre_ref, sc_ref, sh_ref, g2_ref, post_ref,
             dx_ref, h2_ref, r_ref, dz_ref, dy2_ref, acc_ref, z_sc):
        @pl.when(pl.program_id(0) == 0)
        def _():
            acc_ref[...] = jnp.zeros_like(acc_ref)

        x1v = x_ref[...]
        h2, vjp_h = jax.vjp(f_modulate, x1v, pre_ref[...], sc_ref[...], sh_ref[...])
        h2b = h2.astype(BF16)
        h2_ref[...] = h2b
        y2 = jnp.zeros((tm, D), F32)
        for n in range(nb):
            z = _dot(h2b, w1_ref[n], _NN)
            z_sc[:, n * fb:(n + 1) * fb] = z
            r = jnp.square(jnp.maximum(z, 0.0)).astype(BF16)
            r_ref[:, n * fb:(n + 1) * fb] = r
            y2 = y2 + _dot(r, w2_ref[n], _NN)
        out, vjp_r = jax.vjp(f_residual, x1v, y2, g2_ref[...], post_ref[...])
        err = out - tg_ref[...]
        tok = jnp.mean(jnp.square(err), axis=-1, keepdims=True)
        acc_ref[5:6, :] += 0.5 * jnp.sum(tok, axis=0, keepdims=True)
        dx_a, dy2, dg2, dpost = vjp_r(err * (1.0 / D))
        dy2b = dy2.astype(BF16)
        dy2_ref[...] = dy2b
        dh2 = jnp.zeros((tm, D), F32)
        for n in range(nb):
            dr = _dot(dy2b, w2_ref[n], _NT)
            dz = (dr * (2.0 * jnp.maximum(z_sc[:, n * fb:(n + 1) * fb], 0.0))).astype(BF16)
            dz_ref[:, n * fb:(n + 1) * fb] = dz
            dh2 = dh2 + _dot(dz, w1_ref[n], _NT)
        dx_b, dpre, dsc, dsh = vjp_h(dh2)
        dx_ref[...] = dx_a + dx_b
        acc_ref[0:1, :] += dpre
        acc_ref[1:2, :] += dpost
        acc_ref[2:3, :] += dsc
        acc_ref[3:4, :] += dsh
        acc_ref[4:5, :] += dg2

    row = pl.BlockSpec((tm, D), lambda m: (m, 0))
    wide = pl.BlockSpec((tm, DFF), lambda m: (m, 0), pipeline_mode=pl.Buffered(1))
    vec = pl.BlockSpec((1, D), lambda m: (0, 0))
    return pl.pallas_call(
        body, name="ffn_forward_backward", grid=(t // tm,),
        out_shape=(jax.ShapeDtypeStruct((t, D), F32), jax.ShapeDtypeStruct((t, D), BF16),
                   jax.ShapeDtypeStruct((t, DFF), BF16), jax.ShapeDtypeStruct((t, DFF), BF16),
                   jax.ShapeDtypeStruct((t, D), BF16), jax.ShapeDtypeStruct((8, D), F32)),
        in_specs=[row, row, VSPEC, VSPEC, vec, vec, vec, vec, vec],
        out_specs=(row, row, wide, wide, row, pl.BlockSpec((8, D), lambda m: (0, 0))),
        scratch_shapes=[pltpu.VMEM((tm, DFF), F32)],
        compiler_params=_params("arbitrary"),
    )(x1, target, w1, w2, pre, sc, sh, g2, post)


def mix_backward(x, p, qe, intra, st_prev, act, wa, wb, wo, hg, g1, post, dx1, dec):
    t = x.shape[0]
    tm = min(256, t)
    nc = t // CH

    def body(x_ref, og_ref, ga_ref, gb_ref, qe_ref, intra_ref, st_ref, act_ref, wa_ref, wb_ref, wo_ref, hg_ref,
             g1_ref, post_ref, dx1_ref, dec_ref,
             dp_ref, do_ref, dqe_ref, gt_ref, dact_ref, a_ref, mg_ref, dya_ref, dyb_ref, dy_ref, acc_ref, gstate):
        @pl.when(pl.program_id(0) == 0)
        def _():
            acc_ref[...] = jnp.zeros_like(acc_ref)
            gstate[...] = jnp.zeros_like(gstate)

        o, a, ya, yb, merged, y = _mix_forward_tile(x_ref, og_ref, ga_ref, gb_ref, qe_ref, intra_ref, st_ref, act_ref,
                                                    wa_ref, wb_ref, wo_ref, hg_ref, tm)
        a_ref[...] = a.astype(BF16)
        mg_ref[...] = merged.astype(BF16)
        _, vjp_r = jax.vjp(f_residual, x_ref[...], y, g1_ref[...], post_ref[...])
        _, dy, dg1, dpost = vjp_r(dx1_ref[...])
        dyb16 = dy.astype(BF16)
        dy_ref[...] = dyb16
        dmerged = _dot(dyb16, wo_ref[...], _NT)
        _, vjp_m = jax.vjp(f_merge, ga_ref[...], gb_ref[...], ya, yb)
        dga, dgb, dya, dyb = vjp_m(dmerged)
        dp_ref[:, D:2 * D] = dga.astype(BF16)
        dp_ref[:, 2 * D:3 * D] = dgb.astype(BF16)
        dya16, dyb16b = dya.astype(BF16), dyb.astype(BF16)
        dya_ref[...] = dya16
        dyb_ref[...] = dyb16b
        da = _dot(dya16, wa_ref[...], _NT)
        dact_ref[...] = _dot(dyb16b, wb_ref[...], _NT)
        _, vjp_a = jax.vjp(f_head_out, o, og_ref[...], hg_ref[...])
        do, dog, dhg = vjp_a(da)
        dp_ref[:, 0:D] = dog.astype(BF16)
        do_ref[...] = do
        do16 = do.astype(BF16)
        qe16 = qe_ref[...]
        for ci in reversed(range(tm // CH)):
            st = st_ref[ci].astype(BF16)
            rows = slice(ci * CH, (ci + 1) * CH)
            dqe, vt = [], []
            for h in range(HEADS):
                sl = slice(h * DK, (h + 1) * DK)
                dqe.append(_dot(do16[rows, sl], st[:, sl], _NN))
                vt.append(_dot(do16[rows, sl], qe16[rows, sl], _TN))
            dqe_ref[rows, :] = jnp.concatenate(dqe, axis=1)
            g = gstate[...]
            gt_ref[ci] = g
            gstate[...] = dec_ref[ci] * g + jnp.concatenate(vt, axis=1)
        acc_ref[0:1, :] += dg1
        acc_ref[1:2, :] += dpost
        acc_ref[2:3, :] += dhg
        acc_ref[3:4, :] += jnp.sum(dog, axis=0, keepdims=True)
        acc_ref[4:5, :] += jnp.sum(dga, axis=0, keepdims=True)
        acc_ref[5:6, :] += jnp.sum(dgb, axis=0, keepdims=True)

    nt = t // tm
    col, row, vec, per_chunk = _mix_specs(tm, nt)
    b16 = jax.ShapeDtypeStruct((t, D), BF16)
    f32 = jax.ShapeDtypeStruct((t, D), F32)
    return pl.pallas_call(
        body, name="mix_backward", grid=(nt,),
        out_shape=(jax.ShapeDtypeStruct((t, NDEV * D), BF16), f32, f32, jax.ShapeDtypeStruct((nc, DK, D), F32), f32,
                   b16, b16, b16, b16, b16, jax.ShapeDtypeStruct((8, D), F32)),
        in_specs=[row, col(3), col(6), col(7), row, row, per_chunk(DK), row, VSPEC, VSPEC, VSPEC, vec, vec, vec, row,
                  per_chunk(1)],
        out_specs=(pl.BlockSpec((tm, 3 * D), lambda m: (nt - 1 - m, 0)), row, row, per_chunk(DK), row, row, row, row,
                   row, row, pl.BlockSpec((8, D), lambda m: (0, 0))),
        scratch_shapes=[pltpu.VMEM((DK, D), F32)],
        compiler_params=_params("arbitrary"),
    )(x, p, p, p, qe, intra, st_prev, act, wa, wb, wo, hg, g1, post, dx1, dec)


def conv_backward(p, yc, dact, dw, ln_g, ln_b, blocks, dp):
    t = p.shape[0]
    tm = _conv_tile(t)
    per = tm // HALO
    nt = t // tm
    n = len(blocks)

    def body(*refs):
        cv_ref, cg_ref, cvp_ref, cgp_ref, yc_ref, ycn_ref, da_ref, dan_ref, dw_ref, g_ref, b_ref = refs[:11]
        dp_ref, acc_ref, ddw_ref = refs[12 + n:15 + n]
        uext, dyext, ush, dysh, ddw8, du_sc = refs[15 + 2 * n:21 + 2 * n]
        start, finish = _exchange_ops(refs[11:11 + n], refs[15 + n:15 + 2 * n], *refs[21 + 2 * n:])
        m = pl.program_id(0)

        @pl.when(m == 0)
        def _():
            start()
            acc_ref[...] = jnp.zeros_like(acc_ref)
            ddw8[...] = jnp.zeros_like(ddw8)

        cv, cg = cv_ref[...], cg_ref[...]
        u, vjp_u = jax.vjp(f_glu, cv, cg)
        uext[0:HALO, :] = jnp.where(m == 0, 0.0, f_glu(cvp_ref[...], cgp_ref[...]))
        uext[HALO:HALO + tm, :] = u
        _shift_stack(uext, ush, tm + HALO)
        _, vjp_c = jax.vjp(f_conv_act, yc_ref[...], g_ref[...], b_ref[...])
        dyc, dg, db = vjp_c(da_ref[...])
        _, vjp_n = jax.vjp(f_conv_act, ycn_ref[...], g_ref[...], b_ref[...])
        dyn = vjp_n(dan_ref[...])[0]
        dyext[0:tm, :] = dyc
        dyext[tm:tm + HALO, :] = jnp.where(m == nt - 1, 0.0, dyn)
        _shift_stack(dyext, dysh, tm + HALO)
        rb = min(128, tm)
        for lt in range(D // LANE):
            ls = slice(lt * LANE, (lt + 1) * LANE)
            for r0 in range(0, tm, rb):
                du_l = jnp.zeros((rb, LANE), F32)
                for w in range(KW):
                    du_l = du_l + dw_ref[w:w + 1, ls] * _shifted(dysh, KW - 1 - w + r0, rb, ls)
                du_sc[r0:r0 + rb, ls] = du_l
                dyc_l = dyext[r0:r0 + rb, ls]
                for w in range(KW):
                    prod = dyc_l * _shifted(ush, HALO - KW + 1 + w + r0, rb, ls)
                    part = jnp.sum(prod.reshape(4, rb // 32, 8, LANE), axis=1)
                    ddw8[w, :, ls] += (part[0] + part[1]) + (part[2] + part[3])
        du = du_sc[...]

        @pl.when(m == nt - 1)
        def _():
            ddw_ref[...] = jnp.sum(ddw8[...], axis=1)

        dcv, dcg = vjp_u(du)
        dp_ref[:, 0:D] = dcv.astype(BF16)
        dp_ref[:, D:2 * D] = dcg.astype(BF16)
        acc_ref[0:1, :] += jnp.sum(dyc, axis=0, keepdims=True)
        acc_ref[1:2, :] += dg
        acc_ref[2:3, :] += db
        acc_ref[3:4, :] += jnp.sum(dcv, axis=0, keepdims=True)
        acc_ref[4:5, :] += jnp.sum(dcg, axis=0, keepdims=True)
        pl.when(m == nt - 1)(finish)

    vec = pl.BlockSpec((1, D), lambda m: (0, 0))
    row = pl.BlockSpec((tm, D), lambda m: (m, 0))
    prev = lambda j: pl.BlockSpec((HALO, D), lambda m: (jnp.maximum(m * per - 1, 0), j))
    nxt = pl.BlockSpec((HALO, D), lambda m: (jnp.minimum((m + 1) * per, t // HALO - 1), 0))
    res = pl.pallas_call(
        body, name="conv_backward", grid=(nt,),
        out_shape=(jax.ShapeDtypeStruct(dp.shape, dp.dtype), jax.ShapeDtypeStruct((8, D), F32),
                   jax.ShapeDtypeStruct((HALO, D), F32)) + _exchanged(blocks),
        in_specs=[pl.BlockSpec((tm, D), lambda m: (m, 4)), pl.BlockSpec((tm, D), lambda m: (m, 5)), prev(4), prev(5),
                  row, nxt, row, nxt, pl.BlockSpec((HALO, D), lambda m: (0, 0)), vec, vec] + [ANYSPEC] * (n + 1),
        out_specs=(pl.BlockSpec((tm, 2 * D), lambda m: (m, 3)), pl.BlockSpec((8, D), lambda m: (0, 0)),
                   pl.BlockSpec((HALO, D), lambda m: (0, 0))) + (ANYSPEC,) * n,
        scratch_shapes=[pltpu.VMEM((HALO + tm, D), F32), pltpu.VMEM((tm + HALO, D), F32),
                        pltpu.VMEM((8, tm + HALO, D), F32), pltpu.VMEM((8, tm + HALO, D), F32),
                        pltpu.VMEM((HALO, 8, D), F32), pltpu.VMEM((tm, D), F32)] + _comm_sems(n, False),
        input_output_aliases={11 + n: 0},
        compiler_params=_params("arbitrary"),
    )(p, p, p, p, yc, yc, dact, dact, dw, ln_g, ln_b, *blocks, dp)
    return res[:3] + (res[3:],)


def hgrn_backward(p, logits, do, dqe, gt, st_prev, blocks, dp):
    t = p.shape[0]
    nc = t // CH
    n = len(blocks)

    def body(*refs):
        q_ref, f_ref, v_ref, l_ref, do_ref, dqe_ref, gt_ref, st_ref = refs[:8]
        dp_ref, acc_ref = refs[9 + n:11 + n]
        start, finish = _exchange_ops(refs[8:8 + n], refs[11 + n:11 + 2 * n], *refs[11 + 2 * n:])

        @pl.when(pl.program_id(0) == 0)
        def _():
            start()
            acc_ref[...] = jnp.zeros_like(acc_ref)

        gt_v = gt_ref[...]
        ddec = jnp.sum(gt_v * st_ref[...], axis=0, keepdims=True)
        _, vjp = jax.vjp(f_hgrn_chunk, q_ref[...], f_ref[...], v_ref[...], l_ref[...])
        dq, df, dv, dl = vjp((do_ref[...], dqe_ref[...], gt_v, ddec))
        dp_ref[:, 0:D] = dq.astype(BF16)
        dp_ref[:, D:2 * D] = df.astype(BF16)
        dp_ref[:, 2 * D:3 * D] = dv.astype(BF16)
        acc_ref[0:2, :] += dl
        acc_ref[2:3, :] += jnp.sum(dq, axis=0, keepdims=True)
        acc_ref[3:4, :] += jnp.sum(df, axis=0, keepdims=True)
        acc_ref[4:5, :] += jnp.sum(dv, axis=0, keepdims=True)
        pl.when(pl.program_id(0) == nc - 1)(finish)

    col = lambda j: pl.BlockSpec((CH, D), lambda c: (c, j))
    row = pl.BlockSpec((CH, D), lambda c: (c, 0))
    stspec = pl.BlockSpec((None, DK, D), lambda c: (c, 0, 0))
    res = pl.pallas_call(
        body, name="hgrn_backward", grid=(nc,),
        out_shape=(jax.ShapeDtypeStruct(dp.shape, dp.dtype), jax.ShapeDtypeStruct((8, D), F32)) + _exchanged(blocks),
        in_specs=[col(0), col(1), col(2), pl.BlockSpec((2, D), lambda c: (0, 0)), row, row, stspec, stspec]
        + [ANYSPEC] * (n + 1),
        out_specs=(pl.BlockSpec((CH, 3 * D), lambda c: (c, 1)), pl.BlockSpec((8, D), lambda c: (0, 0)))
        + (ANYSPEC,) * n,
        scratch_shapes=_comm_sems(n, False),
        input_output_aliases={8 + n: 0},
        compiler_params=_params("arbitrary"),
    )(p, p, p, logits, do, dqe, gt, st_prev, *blocks, dp)
    return res[:2] + (res[2:],)


def in_proj_backward(dp, w_all, x, dx1, gain, sc, sh):
    t = x.shape[0]
    tm = min(512, t)

    def body(dp_ref, w_ref, x_ref, dx1_ref, g_ref, sc_ref, sh_ref, gx_ref, acc_ref):
        @pl.when(pl.program_id(0) == 0)
        def _():
            acc_ref[...] = jnp.zeros_like(acc_ref)

        dh = jnp.zeros((tm, D), F32)
        for j in range(NDEV):
            dh = dh + _dot(dp_ref[:, j * D:(j + 1) * D], w_ref[DP_SPLIT[j]], _NT)
        _, vjp_h = jax.vjp(f_modulate, x_ref[...], g_ref[...], sc_ref[...], sh_ref[...])
        dx, dg, dsc, dsh = vjp_h(dh)
        gx_ref[...] = dx1_ref[...] + dx
        acc_ref[0:1, :] += dg
        acc_ref[1:2, :] += dsc
        acc_ref[2:3, :] += dsh

    row = pl.BlockSpec((tm, D), lambda m: (m, 0))
    vec = pl.BlockSpec((1, D), lambda m: (0, 0))
    return pl.pallas_call(
        body, name="in_proj_backward", grid=(t // tm,),
        out_shape=(jax.ShapeDtypeStruct((t, D), F32), jax.ShapeDtypeStruct((8, D), F32)),
        in_specs=[pl.BlockSpec((tm, NDEV * D), lambda m: (m, 0)), VSPEC, row, row, vec, vec, vec],
        out_specs=(row, pl.BlockSpec((8, D), lambda m: (0, 0))),
        compiler_params=_params("arbitrary"),
    )(dp, w_all, x, dx1, gain, sc, sh)


def weight_grad(a, b, nblk, ka, bn, a_blocked, name, b_col=lambda n: n, after=None):
    t = a.shape[0]
    tk = min(2048, t)
    nk = t // tk
    extra = [] if after is None else [after]

    def body(*refs):
        a_ref, b_ref = refs[:2]
        f_ref, h_ref, acc = refs[2 + len(extra):]
        k = pl.program_id(1)

        @pl.when(k == 0)
        def _():
            acc[...] = jnp.zeros_like(acc)

        acc[...] += _dot(a_ref[...], b_ref[...], _TN)

        @pl.when(k == nk - 1)
        def _():
            f_ref[...] = acc[...]
            h_ref[...] = acc[...].astype(BF16)

    a_idx = (lambda n, k: (k, n)) if a_blocked else (lambda n, k: (k, 0))
    b_idx = (lambda n, k: (k, 0)) if a_blocked else (lambda n, k: (k, b_col(n)))
    out = pl.BlockSpec((None, ka, bn), lambda n, k: (n, 0, 0))
    return pl.pallas_call(
        body, name=name, grid=(nblk, nk),
        out_shape=(jax.ShapeDtypeStruct((nblk, ka, bn), F32), jax.ShapeDtypeStruct((nblk, ka, bn), BF16)),
        in_specs=[pl.BlockSpec((tk, ka), a_idx), pl.BlockSpec((tk, bn), b_idx)] + [ANYSPEC] * len(extra),
        out_specs=(out, out),
        scratch_shapes=[pltpu.VMEM((ka, bn), F32)],
        compiler_params=_params("parallel", "arbitrary"),
    )(a, b, *extra)


def ada_backward(call_t, dmod_cols, w, m, v):
    def body(c_ref, d_ref, w_ref, m_ref, v_ref, g_ref, dl_ref, nm_ref, nv_ref):
        ct, dm = c_ref[...], d_ref[...]
        g = ct[:, 0:1] * dm[0:1, :]
        for r in range(1, NDEV):
            g = g + ct[:, r:r + 1] * dm[r:r + 1, :]
        g_ref[...] = g
        dl_ref[...], nm_ref[...], nv_ref[...] = _adamw(w_ref[...], g, m_ref[...], v_ref[...])

    br = 256
    blk = pl.BlockSpec((br, w.shape[1]), lambda i: (i, 0))
    s = jax.ShapeDtypeStruct(w.shape, F32)
    return pl.pallas_call(
        body, name="ada_backward", grid=(w.shape[0] // br,), out_shape=(s, s, s, s),
        in_specs=[pl.BlockSpec((br, NDEV), lambda i: (i, 0)), pl.BlockSpec(dmod_cols.shape, lambda i: (0, 0)), blk,
                  blk, blk],
        out_specs=(blk, blk, blk, blk), compiler_params=_params("parallel"),
    )(call_t, dmod_cols, w, m, v)


def adamw_small(total, ddw_mine, recipes, ws, ms, vs):
    n = len(ws)

    def body(*refs):
        tot, ddw = refs[0], refs[1]
        w_refs, m_refs, v_refs = refs[2:2 + n], refs[2 + n:2 + 2 * n], refs[2 + 2 * n:2 + 3 * n]
        outs = refs[2 + 3 * n:2 + 7 * n]
        loss_ref = refs[2 + 7 * n]
        for i, rec in enumerate(recipes):
            if rec == "dw":
                g = ddw[...]
            elif isinstance(rec, tuple):
                g = tot[rec[0]:rec[1], :]
            else:
                g = jnp.concatenate([tot[r:r + 1, :] for r in rec], axis=1) if len(rec) > 1 else tot[rec[0]:rec[0] + 1, :]
            dl, nm, nv = _adamw(w_refs[i][...], g, m_refs[i][...], v_refs[i][...])
            outs[4 * i][...] = g
            outs[4 * i + 1][...] = dl
            outs[4 * i + 2][...] = nm
            outs[4 * i + 3][...] = nv
        loss_ref[...] = tot[LOSS_ROW:LOSS_ROW + 1, 0:128]

    shapes = []
    for w in ws:
        shapes += [jax.ShapeDtypeStruct(w.shape, F32)] * 4
    res = pl.pallas_call(
        body, name="adamw_small", out_shape=tuple(shapes) + (jax.ShapeDtypeStruct((1, 128), F32),),
        in_specs=[VSPEC] * (2 + 3 * n), out_specs=(VSPEC,) * (4 * n + 1), compiler_params=_params(),
    )(total, ddw_mine, *ws, *ms, *vs)
    return [res[4 * i:4 * i + 4] for i in range(n)], res[4 * n]


def reduce_and_adamw(sel, w, g_alls, g_recvs, m, v, name):
    r, c = w.shape
    br = min(256, r)
    npair = len(g_alls)

    def body(sel_ref, w_ref, *refs):
        go_refs, gr_refs = refs[:npair], refs[npair:2 * npair]
        m_ref, v_ref, g_ref, dl_ref, nm_ref, nv_ref = refs[2 * npair:]
        g = None
        for i in range(npair):
            gi = go_refs[i][...]
            for k in range(NDEV - 1):
                gi = gi + gr_refs[i][k].astype(F32)
            g = gi if g is None else jnp.where(sel_ref[1] == i, gi, g)
        g_ref[...] = g
        dl_ref[...], nm_ref[...], nv_ref[...] = _adamw(w_ref[...], g, m_ref[...], v_ref[...])

    blk = pl.BlockSpec((br, c), lambda i, sel_ref: (i, 0))
    own = pl.BlockSpec((None, br, c), lambda i, sel_ref: (sel_ref[0], i, 0))
    recv = pl.BlockSpec((NDEV - 1, br, c), lambda i, sel_ref: (0, i, 0))
    s = jax.ShapeDtypeStruct(w.shape, F32)
    return pl.pallas_call(
        body, name=name, out_shape=(s, s, s, s),
        grid_spec=pltpu.PrefetchScalarGridSpec(
            num_scalar_prefetch=1, grid=(r // br,),
            in_specs=[blk] + [own] * npair + [recv] * npair + [blk, blk],
            out_specs=(blk, blk, blk, blk)),
        compiler_params=_params("parallel"),
    )(sel, w, *g_alls, *g_recvs, m, v)


def kernel(x, c, w_ada, b_ada, pre_norm_tm, post_norm_tm, pre_norm_cm, post_norm_cm, w_in, b_in, hg_lb_logits, hg_norm, conv_dw, conv_db, conv_ln_g, conv_ln_b, w_br_a, w_br_b, w_out, w_ff1, w_ff2, loss_target, m_w_ada, m_b_ada, m_pre_norm_tm, m_post_norm_tm, m_pre_norm_cm, m_post_norm_cm, m_w_in, m_b_in, m_hg_lb_logits, m_hg_norm, m_conv_dw, m_conv_db, m_conv_ln_g, m_conv_ln_b, m_w_br_a, m_w_br_b, m_w_out, m_w_ff1, m_w_ff2, v_w_ada, v_b_ada, v_pre_norm_tm, v_post_norm_tm, v_pre_norm_cm, v_post_norm_cm, v_w_in, v_b_in, v_hg_lb_logits, v_hg_norm, v_conv_dw, v_conv_db, v_conv_ln_g, v_conv_ln_b, v_w_br_a, v_w_br_b, v_w_out, v_w_ff1, v_w_ff2):
    t = x.shape[1]
    me = 4 * lax.axis_index("x") + 2 * lax.axis_index("y") + lax.axis_index("c")
    xs = x[0]
    tgt = loss_target[0]

    mod, call, dw_all = ada_forward(c, w_ada[0], b_ada, conv_dw[0])
    sh1, sc1, g1, sh2, sc2, g2 = [mod[:, i * D:(i + 1) * D] for i in range(6)]
    dw = jnp.pad(dw_all.transpose(1, 0, 2).reshape(KW, D), ((0, HALO - KW), (0, 0)))

    p, h, win_all = in_proj(xs, pre_norm_tm, sc1, sh1, w_in[0].astype(BF16), b_in)
    yc, act, (w1_all,) = conv_forward(p, dw, conv_db, conv_ln_g, conv_ln_b, [w_ff1[0].astype(BF16)])
    intra, qe, st_prev, dec, (w2_all, wa_all, wb_all, wo_all) = hgrn_local(
        p, hg_lb_logits, [w_ff2[0].astype(BF16), w_br_a[0].astype(BF16), w_br_b[0].astype(BF16),
                          w_out[0].astype(BF16)])
    wa, wb, wo = wa_all.reshape(D, D), wb_all.reshape(D, D), wo_all.reshape(D, D)
    x1 = mix_forward(xs, p, qe, intra, st_prev, act, wa, wb, wo, hg_norm, g1, post_norm_tm)

    dx1, h2, r, dz, dy2, acc_ffn = ffn_forward_backward(x1, tgt, w1_all, w2_all, pre_norm_cm, sc2, sh2, g2,
                                                        post_norm_cm)
    rows = D // NDEV
    g1_f, g1_h = weight_grad(h2, dz, NDEV, D, DFF // NDEV, False, "grad_w_ff1")
    g2_f, g2_h = weight_grad(r, dy2, 4, D, D, True, "grad_w_ff2")
    g2_f, g2_h = g2_f.reshape(NDEV, DFF // NDEV, D), g2_h.reshape(NDEV, DFF // NDEV, D)

    (dp, do, dqe, gt, dact, a16, mg16, dya, dyb, dy, acc_mix) = mix_backward(
        xs, p, qe, intra, st_prev, act, wa, wb, wo, hg_norm, g1, post_norm_tm, dx1, dec)
    ga_f, ga_h = weight_grad(a16, dya, 1, D, D, False, "grad_w_br_a")
    gb_f, gb_h = weight_grad(act, dyb, 1, D, D, False, "grad_w_br_b")
    go_f, go_h = weight_grad(mg16, dy, 1, D, D, False, "grad_w_out")
    dp, acc_conv, ddw, (r_ff1, r_ff2) = conv_backward(p, yc, dact, dw, conv_ln_g, conv_ln_b, [g1_h, g2_h], dp)
    dp, acc_hg, (r_a, r_b, r_o) = hgrn_backward(
        p, hg_lb_logits, do, dqe, gt, st_prev,
        [ga_h.reshape(NDEV, rows, D), gb_h.reshape(NDEV, rows, D), go_h.reshape(NDEV, rows, D)], dp)
    dp_col = lambda n: jnp.where(n < 3, n + 3, jnp.where(n == 3, 0, jnp.where(n < 6, n + 2, n - 5)))
    even, odd = [0, 2, 4, 6], [1, 3, 5, 7]
    gin_f0, gin_h0 = weight_grad(h, dp, 4, D, D, False, "grad_w_in_even", b_col=lambda j: dp_col(2 * j))
    send0, recv0, thru0, landing0, token0 = exchange_start(gin_h0, even, "exchange_start_even")
    gin_f1, gin_h1 = weight_grad(h, dp, 4, D, D, False, "grad_w_in_odd", b_col=lambda j: dp_col(2 * j + 1),
                                 after=token0)
    send1, recv1, thru1, landing1, token1 = exchange_start(gin_h1, odd, "exchange_start_odd")
    grad_x, acc_in = in_proj_backward(dp, win_all, xs, dx1, pre_norm_tm, sc1, sh1 + token1[0:1, 0:1])

    own = [ga_f.reshape(NDEV, rows, D), gb_f.reshape(NDEV, rows, D), go_f.reshape(NDEV, rows, D), g1_f, g2_f]
    recv = [r_a, r_b, r_o, r_ff1, r_ff2]
    big = {}
    names = ["w_br_a", "w_br_b", "w_out", "w_ff1", "w_ff2"]
    ws = [w_br_a, w_br_b, w_out, w_ff1, w_ff2]
    ms = [m_w_br_a, m_w_br_b, m_w_out, m_w_ff1, m_w_ff2]
    vs = [v_w_br_a, v_w_br_b, v_w_out, v_w_ff1, v_w_ff2]
    sel = jnp.stack([me, 0]).astype(jnp.int32)
    for i, nm in enumerate(names):
        big[nm] = [o[None] for o in reduce_and_adamw(sel, ws[i][0], [own[i]], [recv[i]], ms[i][0], vs[i][0],
                                                     "adamw_" + nm)]
    big, grad_x, acc_in = lax.optimization_barrier((big, grad_x, acc_in))
    r_in0 = exchange_wait(send0, recv0, thru0, landing0, acc_in, even, "exchange_wait_even")
    r_in1 = exchange_wait(send1, recv1, thru1, landing1, acc_in, odd, "exchange_wait_odd")
    r_in0, r_in1, acc_in = lax.optimization_barrier((r_in0, r_in1, acc_in))
    sel_in = jnp.stack([me // 2, me % 2]).astype(jnp.int32)
    big["w_in"] = [o[None] for o in reduce_and_adamw(sel_in, w_in[0], [gin_f0, gin_f1], [r_in0, r_in1], m_w_in[0],
                                                     v_w_in[0], "adamw_w_in")]

    dmod_rows = [2, 1, 8, 19, 18, 20]
    total, kept = gather_and_sum_rows([acc_in, acc_mix, acc_ffn, acc_hg, acc_conv, ddw], dmod_rows)
    dmod_all = kept[:, 0:6, :].reshape(NDEV, 6 * D)
    wcols = w_ada.shape[2]
    gwa, dwa, nmwa, nvwa = ada_backward(call.T, lax.dynamic_slice_in_dim(dmod_all, me * wcols, wcols, axis=1),
                                        w_ada[0], m_w_ada[0], v_w_ada[0])
    ddw_mine = lax.dynamic_slice_in_dim(total[40:40 + KW], me * (D // NDEV), D // NDEV, axis=1)

    small_names = ["b_ada", "pre_norm_tm", "post_norm_tm", "pre_norm_cm", "post_norm_cm", "b_in", "hg_lb_logits",
                   "hg_norm", "conv_db", "conv_ln_g", "conv_ln_b", "conv_dw"]
    recipes = [dmod_rows, [0], [9], [16], [17], [26, 27, 28, 11, 35, 36, 12, 13], (24, 26), [10], [32], [33], [34],
               "dw"]
    small_w = [b_ada, pre_norm_tm, post_norm_tm, pre_norm_cm, post_norm_cm, b_in, hg_lb_logits, hg_norm, conv_db,
               conv_ln_g, conv_ln_b, conv_dw[0]]
    small_m = [m_b_ada, m_pre_norm_tm, m_post_norm_tm, m_pre_norm_cm, m_post_norm_cm, m_b_in, m_hg_lb_logits,
               m_hg_norm, m_conv_db, m_conv_ln_g, m_conv_ln_b, m_conv_dw[0]]
    small_v = [v_b_ada, v_pre_norm_tm, v_post_norm_tm, v_pre_norm_cm, v_post_norm_cm, v_b_in, v_hg_lb_logits,
               v_hg_norm, v_conv_db, v_conv_ln_g, v_conv_ln_b, v_conv_dw[0]]
    small_out, loss_row = adamw_small(total, ddw_mine, recipes, small_w, small_m, small_v)
    loss = loss_row[0, 0]
    sm = {nm: list(o) for nm, o in zip(small_names, small_out)}
    sm["conv_dw"] = [o[None] for o in sm["conv_dw"]]

    order = ["w_ada", "b_ada", "pre_norm_tm", "post_norm_tm", "pre_norm_cm", "post_norm_cm", "w_in", "b_in",
             "hg_lb_logits", "hg_norm", "conv_dw", "conv_db", "conv_ln_g", "conv_ln_b", "w_br_a", "w_br_b", "w_out",
             "w_ff1", "w_ff2"]
    res = dict(sm)
    res.update(big)
    res["w_ada"] = [gwa[None], dwa[None], nmwa[None], nvwa[None]]
    outs = [loss, grad_x[None]]
    for j in range(4):
        outs += [res[nm][j] for nm in order]
    return tuple(outs)
```

```python
import jax
import jax.numpy as jnp
from jax import lax
from jax.experimental import pallas as pl
from jax.experimental.pallas import tpu as pltpu

F32 = jnp.float32
BF16 = jnp.bfloat16
MESH = pl.DeviceIdType.MESH

D = 1024
HEADS = 8
DK = 128
LANE = 128
CH = 128
LEVELS = 7
KW = 31
HALO = 32
DFF = 4096
NDEV = 8
EPS = 1e-6
DP_SPLIT = (3, 6, 7, 0, 1, 2, 4, 5)
LOSS_ROW = 21
ADAM_LR, ADAM_B1, ADAM_B2, ADAM_EPS, ADAM_WD, ADAM_STEP = 0.001, 0.9, 0.999, 1e-08, 0.01, 10
VMEM_LIMIT = 58 * 1024 * 1024

_NN = (((1,), (0,)), ((), ()))
_NT = (((1,), (1,)), ((), ()))
_TN = (((0,), (0,)), ((), ()))

VSPEC = pl.BlockSpec(memory_space=pltpu.VMEM)
ANYSPEC = pl.BlockSpec(memory_space=pl.ANY)


def _params(*sem):
    return pltpu.CompilerParams(dimension_semantics=sem or None, vmem_limit_bytes=VMEM_LIMIT)


def _dot(a, b, dims):
    return lax.dot_general(a, b, dims, preferred_element_type=F32)


@jax.custom_vjp
def mm_nn(a, b):
    return _dot(a.astype(BF16), b.astype(BF16), _NN)


def _mm_nn_fwd(a, b):
    ab, bb = a.astype(BF16), b.astype(BF16)
    return _dot(ab, bb, _NN), (ab, bb)


def _mm_nn_bwd(res, ct):
    ab, bb = res
    cb = ct.astype(BF16)
    return _dot(cb, bb, _NT), _dot(ab, cb, _TN)


mm_nn.defvjp(_mm_nn_fwd, _mm_nn_bwd)


@jax.custom_vjp
def mm_nt(a, b):
    return _dot(a.astype(BF16), b.astype(BF16), _NT)


def _mm_nt_fwd(a, b):
    ab, bb = a.astype(BF16), b.astype(BF16)
    return _dot(ab, bb, _NT), (ab, bb)


def _mm_nt_bwd(res, ct):
    ab, bb = res
    cb = ct.astype(BF16)
    return _dot(cb, bb, _NN), _dot(cb, ab, _TN)


mm_nt.defvjp(_mm_nt_fwd, _mm_nt_bwd)


@jax.custom_vjp
def mm_tn(a, b):
    return _dot(a.astype(BF16), b.astype(BF16), _TN)


def _mm_tn_fwd(a, b):
    ab, bb = a.astype(BF16), b.astype(BF16)
    return _dot(ab, bb, _TN), (ab, bb)


def _mm_tn_bwd(res, ct):
    ab, bb = res
    cb = ct.astype(BF16)
    return _dot(bb, cb, _NT), _dot(ab, cb, _NN)


mm_tn.defvjp(_mm_tn_fwd, _mm_tn_bwd)


def _rms(x):
    return x * lax.rsqrt(jnp.mean(x * x, axis=-1, keepdims=True) + EPS)


_sigmoid = jax.nn.sigmoid


def _silu(x):
    return x * _sigmoid(x)


def f_modulate(x, gain, sc, sh):
    return _rms(x) * gain * (1.0 + sc) + sh


def f_residual(x, y, gate, gain):
    return x + gate * (_rms(y) * gain)


def f_merge(ga, gb, ya, yb):
    return _sigmoid(ga) * ya + _sigmoid(gb) * yb


def f_head_out(o, og, hg):
    heads = [_rms(o[:, h * DK:(h + 1) * DK]) for h in range(HEADS)]
    return jnp.concatenate(heads, axis=1) * hg * _silu(og)


def f_conv_act(u, g, b):
    mu = jnp.mean(u, axis=-1, keepdims=True)
    var = jnp.mean(jnp.square(u - mu), axis=-1, keepdims=True)
    return _silu((u - mu) * lax.rsqrt(var + EPS) * g + b)


def f_glu(cv, cg):
    return cv * _sigmoid(cg)


def _split2(x):
    hi = x.astype(BF16)
    return hi, (x - hi.astype(F32)).astype(BF16)


def _tri(transposed):
    i = lax.broadcasted_iota(jnp.int32, (CH, CH), 1 if transposed else 0)
    t = lax.broadcasted_iota(jnp.int32, (CH, CH), 0 if transposed else 1)
    return jnp.where(t <= i, 1.0, 0.0).astype(BF16)


def _blocks3(x, rows):
    return x.reshape(CH // rows, rows, x.shape[-1])


def _mid_broadcast(b, lev):
    h = 1 << (LEVELS - 1 - lev)
    if h >= 4:
        x3 = _blocks3(b, 2 * h)
        return jnp.broadcast_to(x3[:, h - 1:h, :], x3.shape).reshape(b.shape)
    x3 = _blocks3(b, 8)
    sub = lax.broadcasted_iota(jnp.int32, x3.shape, 1)
    out = None
    for first in range(0, 8, 2 * h):
        piece = jnp.broadcast_to(x3[:, first + h - 1:first + h, :], x3.shape)
        out = piece if out is None else jnp.where(sub >= first, piece, out)
    return out.reshape(b.shape)


def _mid_scatter(d, lev):
    h = 1 << (LEVELS - 1 - lev)
    if h >= 4:
        x3 = _blocks3(d, 2 * h)
        row = lax.broadcasted_iota(jnp.int32, x3.shape, 1)
        total = jnp.sum(x3, axis=1, keepdims=True)
        return jnp.where(row == h - 1, total, 0.0).reshape(d.shape)
    x3 = _blocks3(d, 8)
    sub = lax.broadcasted_iota(jnp.int32, x3.shape, 1)
    out = jnp.zeros_like(x3)
    for first in range(0, 8, 2 * h):
        inside = (sub >= first) & (sub < first + 2 * h)
        total = jnp.sum(jnp.where(inside, x3, 0.0), axis=1, keepdims=True)
        out = jnp.where(sub == first + h - 1, total, out)
    return out.reshape(d.shape)


@jax.custom_vjp
def decay_sums(g):
    hi, lo = _split2(g)
    tri = _tri(False)
    b = _dot(tri, hi, _NN) + _dot(tri, lo, _NN)
    return (b,) + tuple(b - _mid_broadcast(b, lev) for lev in range(LEVELS))


def _decay_sums_fwd(g):
    return decay_sums(g), None


def _decay_sums_bwd(_, cts):
    db = cts[0]
    for lev in range(LEVELS):
        db = db + cts[1 + lev] - _mid_scatter(cts[1 + lev], lev)
    hi, lo = _split2(db)
    tri = _tri(True)
    return (_dot(tri, hi, _NN) + _dot(tri, lo, _NN),)


decay_sums.defvjp(_decay_sums_fwd, _decay_sums_bwd)


def _score_masks():
    i = lax.broadcasted_iota(jnp.int32, (CH, CH), 0)
    j = lax.broadcasted_iota(jnp.int32, (CH, CH), 1)
    masks = [i == j]
    for lev in range(LEVELS):
        sh = LEVELS - 1 - lev
        same = (i >> (sh + 1)) == (j >> (sh + 1))
        masks.append(same & (((i >> sh) & 1) == 1) & (((j >> sh) & 1) == 0))
    return [jnp.where(m, 1.0, 0.0) for m in masks]


def f_hgrn_chunk(q_r, f_r, v, logits):
    l0, l1 = logits[0:1, :], logits[1:2, :]
    mx = lax.stop_gradient(jnp.maximum(l0, l1))
    e0, e1 = jnp.exp(l0 - mx), jnp.exp(l1 - mx)
    lb = e0 / (e0 + e1)
    q = _silu(q_r)
    f = lb + (1.0 - lb) * _sigmoid(f_r)
    k = 1.0 - f
    sums = decay_sums(jnp.log(f))
    b = sums[0]
    btot = b[CH - 1:CH, :]
    qe = q * jnp.exp(b)
    ke = k * jnp.exp(btot - b)
    dec = jnp.exp(btot)
    qs, ks = [q], [k]
    row = lax.broadcasted_iota(jnp.int32, b.shape, 0)
    for lev in range(LEVELS):
        upper = ((row >> (LEVELS - 1 - lev)) & 1) == 1
        e = sums[1 + lev]
        qs.append(q * jnp.exp(jnp.where(upper, e, 0.0)))
        ks.append(k * jnp.exp(jnp.where(upper, 0.0, -e)))
    masks = _score_masks()
    intra, ut = [], []
    for h in range(HEADS):
        sl = slice(h * DK, (h + 1) * DK)
        sc = None
        for lev in range(LEVELS + 1):
            s = mm_nt(qs[lev][:, sl], ks[lev][:, sl]) * masks[lev]
            sc = s if sc is None else sc + s
        intra.append(mm_nn(sc, v[:, sl]))
        ut.append(mm_tn(v[:, sl], ke[:, sl]))
    return jnp.concatenate(intra, axis=1), qe, jnp.concatenate(ut, axis=1), dec


def _inter(qe_b, st_ref, rows):
    out = []
    for ci in range(rows // CH):
        st = st_ref[ci].astype(BF16)
        heads = [_dot(qe_b[ci * CH:(ci + 1) * CH, h * DK:(h + 1) * DK], st[:, h * DK:(h + 1) * DK], _NT)
                 for h in range(HEADS)]
        out.append(jnp.concatenate(heads, axis=1))
    return jnp.concatenate(out, axis=0)


def _shift_stack(src, dst, rows):
    dst[0, 0:rows, :] = src[0:rows, :]
    for b in range(1, 8):
        dst[b, 0:rows - 8, :] = src[pl.ds(b, rows - 8), :]


def _shifted(stack, offset, rows, lanes=slice(None)):
    return stack[offset % 8, pl.ds(8 * (offset // 8), rows), lanes]


def _adamw(w, g, m, v):
    m = ADAM_B1 * m + (1.0 - ADAM_B1) * g
    v = ADAM_B2 * v + (1.0 - ADAM_B2) * jnp.square(g)
    m_hat = m / (1.0 - ADAM_B1 ** ADAM_STEP)
    v_hat = v / (1.0 - ADAM_B2 ** ADAM_STEP)
    delta = -ADAM_LR * (m_hat / (jnp.sqrt(v_hat) + ADAM_EPS) + ADAM_WD * w)
    return delta, m, v


def _me():
    return lax.axis_index("x"), lax.axis_index("y"), lax.axis_index("c")


def _peer(k):
    x, y, c = _me()
    mask = k + 1
    px = (1 - x) if (mask >> 2) & 1 else x
    py = (1 - y) if (mask >> 1) & 1 else y
    pc = (1 - c) if mask & 1 else c
    return (px, py, pc), 4 * px + 2 * py + pc


def ada_forward(c, w_ada, b_ada, dw):
    wcols = w_ada.shape[1]

    def body(c_ref, w_ref, b_ref, dw_ref, mod_ref, call_ref, dwall_ref, part_ref, modp_ref, send_sems, recv_sems):
        x, y, cc = _me()
        me = 4 * x + 2 * y + cc
        call_ref[me] = _silu(c_ref[...])
        dwall_ref[me] = dw_ref[...]
        sends = []
        for k in range(NDEV - 1):
            dev, pidx = _peer(k)
            cp = pltpu.make_async_remote_copy(dwall_ref.at[me], dwall_ref.at[me], send_sems.at[2 * (NDEV - 1) + k],
                                              recv_sems.at[2 * (NDEV - 1) + k], device_id=dev, device_id_type=MESH)
            cp.start()
            sends.append(cp)
        for k in range(NDEV - 1):
            dev, _ = _peer(k)
            cp = pltpu.make_async_remote_copy(call_ref.at[me], call_ref.at[me], send_sems.at[k], recv_sems.at[k],
                                              device_id=dev, device_id_type=MESH)
            cp.start()
            sends.append(cp)
        for k in range(NDEV - 1):
            _, pidx = _peer(k)
            pltpu.make_async_remote_copy(call_ref.at[pidx], call_ref.at[pidx], send_sems.at[k], recv_sems.at[k],
                                         device_id=_peer(k)[0], device_id_type=MESH).wait_recv()
        call = jnp.concatenate([call_ref[r] for r in range(NDEV)], axis=0)
        part = _dot(call.astype(BF16), w_ref[...].astype(BF16), _NN)
        for r in range(NDEV):
            part_ref[r] = part[r:r + 1, :]
        modp_ref[me] = part_ref[me]
        for k in range(NDEV - 1):
            dev, pidx = _peer(k)
            cp = pltpu.make_async_remote_copy(part_ref.at[pidx], modp_ref.at[me], send_sems.at[NDEV - 1 + k],
                                              recv_sems.at[NDEV - 1 + k], device_id=dev, device_id_type=MESH)
            cp.start()
            sends.append(cp)
        for k in range(NDEV - 1):
            dev, pidx = _peer(k)
            pltpu.make_async_remote_copy(part_ref.at[pidx], modp_ref.at[pidx], send_sems.at[NDEV - 1 + k],
                                         recv_sems.at[NDEV - 1 + k], device_id=dev, device_id_type=MESH).wait_recv()
        for k in range(NDEV - 1):
            dev, pidx = _peer(k)
            pltpu.make_async_remote_copy(dwall_ref.at[pidx], dwall_ref.at[pidx], send_sems.at[2 * (NDEV - 1) + k],
                                         recv_sems.at[2 * (NDEV - 1) + k], device_id=dev,
                                         device_id_type=MESH).wait_recv()
        for cp in sends:
            cp.wait_send()
        mod_ref[...] = jnp.concatenate([modp_ref[r] for r in range(NDEV)], axis=1) + b_ref[...]

    mod, call, dw_all = pl.pallas_call(
        body, name="ada_forward",
        out_shape=(jax.ShapeDtypeStruct((1, NDEV * wcols), F32), jax.ShapeDtypeStruct((NDEV, 1, D), F32),
                   jax.ShapeDtypeStruct((NDEV,) + dw.shape, F32)),
        in_specs=[VSPEC] * 4, out_specs=(VSPEC,) * 3,
        scratch_shapes=[pltpu.VMEM((NDEV, 1, wcols), F32), pltpu.VMEM((NDEV, 1, wcols), F32),
                        pltpu.SemaphoreType.DMA((3 * (NDEV - 1),)), pltpu.SemaphoreType.DMA((3 * (NDEV - 1),))],
        compiler_params=_params(),
    )(c, w_ada, b_ada, dw)
    return mod, call.reshape(NDEV, D), dw_all


def _comm_sems(n, local):
    sems = [pltpu.SemaphoreType.DMA((7 * n,)), pltpu.SemaphoreType.DMA((7 * n,))]
    return sems + ([pltpu.SemaphoreType.DMA((n,))] if local else [])


def _gather2_ops(ins, outs, send_sems, recv_sems, local_sems):
    n = len(ins)
    x, y, c = _me()
    me, sibling = (x, y, c), (x, y, 1 - c)
    chips = [(1 - x, y), (x, 1 - y), (1 - x, 1 - y)]

    def slot(p):
        return 4 * p[0] + 2 * p[1] + p[2]

    def copy(a, k, block, to, src=None):
        return pltpu.make_async_remote_copy(
            src_ref=outs[a].at[slot(block)] if src is None else src, dst_ref=outs[a].at[slot(block)],
            send_sem=send_sems.at[a * 7 + k], recv_sem=recv_sems.at[a * 7 + k], device_id=to, device_id_type=MESH)

    def local(a):
        return pltpu.make_async_copy(ins[a], outs[a].at[slot(me)], local_sems.at[a])

    def first(a):
        return [copy(a, 0, me, sibling, src=ins[a])] + [copy(a, 1 + j, me, (*chip, c), src=ins[a])
                                                        for j, chip in enumerate(chips)]

    def passed(a):
        return [copy(a, 4 + j, (*chip, c), sibling) for j, chip in enumerate(chips)]

    def start():
        for a in range(n):
            local(a).start()
            for cp in first(a):
                cp.start()

    def forward():
        for j, chip in enumerate(chips):
            for a in range(n):
                copy(a, 1 + j, (*chip, c), me).wait_recv()
                passed(a)[j].start()

    def finish():
        for a in range(n):
            copy(a, 0, sibling, me).wait_recv()
            for j, chip in enumerate(chips):
                copy(a, 4 + j, (*chip, 1 - c), me).wait_recv()
        for a in range(n):
            for cp in first(a) + passed(a):
                cp.wait_send()
            local(a).wait()

    return start, forward, finish


def _exchange_ops(ins, outs, send_sems, recv_sems):
    n = len(ins)

    def copy(a, k):
        dev, pidx = _peer(k)
        return pltpu.make_async_remote_copy(ins[a].at[pidx], outs[a].at[k], send_sems.at[a * 7 + k],
                                            recv_sems.at[a * 7 + k], device_id=dev, device_id_type=MESH)

    def start():
        for k in range(NDEV - 1):
            for a in range(n):
                copy(a, k).start()

    def finish():
        for k in range(NDEV - 1):
            for a in range(n):
                copy(a, k).wait_recv()
        for k in range(NDEV - 1):
            for a in range(n):
                copy(a, k).wait_send()

    return start, finish


def _gathered(shards):
    return tuple(jax.ShapeDtypeStruct((NDEV,) + s.shape, s.dtype) for s in shards)


def _exchanged(blocks):
    return tuple(jax.ShapeDtypeStruct((NDEV - 1,) + b.shape[1:], b.dtype) for b in blocks)


def _owner_copy(src_ref, land_ref, send_sems, recv_sems, gi, n):
    x, y, c = _me()
    k = jnp.bitwise_xor(4 * x + 2 * y + c, n) - 1
    return pltpu.make_async_remote_copy(src_ref.at[gi], land_ref.at[k], send_sems.at[gi], recv_sems.at[k],
                                        device_id=(n >> 2, (n >> 1) & 1, n & 1), device_id_type=MESH)


def exchange_start(blocks, owners, name):
    landing = lax.empty((NDEV - 1,) + blocks.shape[1:], blocks.dtype)
    hbm = pl.BlockSpec(memory_space=pltpu.HBM)
    sem = pl.BlockSpec(memory_space=pltpu.SEMAPHORE)

    def body(src_ref, land_ref, send_sems, recv_sems, src_thru, land_thru, token):
        x, y, c = _me()
        for gi, n in enumerate(owners):
            @pl.when(4 * x + 2 * y + c != n)
            def _(gi=gi, n=n):
                _owner_copy(src_ref, land_ref, send_sems, recv_sems, gi, n).start()
        token[...] = jnp.zeros_like(token)

    return pl.pallas_call(
        body, name=name,
        out_shape=(pltpu.SemaphoreType.DMA((len(owners),)), pltpu.SemaphoreType.DMA((NDEV - 1,)),
                   pltpu.HBM(blocks.shape, blocks.dtype), pltpu.HBM(landing.shape, landing.dtype),
                   jax.ShapeDtypeStruct((8, 128), F32)),
        in_specs=(hbm, hbm), out_specs=(sem, sem, hbm, hbm, VSPEC), input_output_aliases={0: 2, 1: 3},
        compiler_params=pltpu.CompilerParams(has_side_effects=pltpu.SideEffectType.DATAFLOW_SIDE_EFFECTING),
    )(pltpu.with_memory_space_constraint(blocks, pltpu.HBM), pltpu.with_memory_space_constraint(landing, pltpu.HBM))


def exchange_wait(send_sems, recv_sems, src_thru, land_thru, after, owners, name):
    hbm = pl.BlockSpec(memory_space=pltpu.HBM)
    sem = pl.BlockSpec(memory_space=pltpu.SEMAPHORE)

    def body(src_ref, land_ref, send_sems, recv_sems, after_ref, src_dead, got_ref):
        x, y, c = _me()
        me = 4 * x + 2 * y + c
        for gi, n in enumerate(owners):
            @pl.when(me != n)
            def _(gi=gi, n=n):
                _owner_copy(src_ref, land_ref, send_sems, recv_sems, gi, n).wait_send()

            @pl.when(me == n)
            def _(gi=gi):
                for k in range(NDEV - 1):
                    pltpu.make_async_remote_copy(src_ref.at[gi], land_ref.at[k], send_sems.at[gi], recv_sems.at[k],
                                                 device_id=_peer(k)[0], device_id_type=MESH).wait_recv()

    return pl.pallas_call(
        body, name=name,
        out_shape=(pltpu.HBM(src_thru.shape, src_thru.dtype), pltpu.HBM(land_thru.shape, land_thru.dtype)),
        in_specs=(hbm, hbm, sem, sem, ANYSPEC), out_specs=(hbm, hbm), input_output_aliases={0: 0, 1: 1},
        compiler_params=pltpu.CompilerParams(has_side_effects=pltpu.SideEffectType.DATAFLOW_SIDE_EFFECTING),
    )(src_thru, land_thru, send_sems, recv_sems, after)[1]


def gather_and_sum_rows(parts, keep_rows):
    n = len(parts)
    offs = [sum(p.shape[0] for p in parts[:i]) for i in range(n)]
    rows = sum(p.shape[0] for p in parts)

    def body(*refs):
        sum_ref, keep_ref, mine_ref, buf_ref, send_sems, recv_sems, local_sems = refs[n:]
        for i in range(n):
            mine_ref[offs[i]:offs[i] + parts[i].shape[0], :] = refs[i][...]
        start, forward, finish = _gather2_ops([mine_ref], [buf_ref], send_sems, recv_sems, local_sems)
        start()
        forward()
        finish()
        total = buf_ref[0]
        for r in range(1, NDEV):
            total = total + buf_ref[r]
        sum_ref[...] = total
        keep_ref[...] = jnp.zeros_like(keep_ref)
        for r in range(NDEV):
            for j, src in enumerate(keep_rows):
                keep_ref[r, j:j + 1, :] = buf_ref[r, src:src + 1, :]

    return pl.pallas_call(
        body, name="gather_and_sum_rows",
        out_shape=(jax.ShapeDtypeStruct((rows, D), F32), jax.ShapeDtypeStruct((NDEV, 8, D), F32)),
        in_specs=[VSPEC] * n, out_specs=(VSPEC, VSPEC),
        scratch_shapes=[pltpu.VMEM((rows, D), F32), pltpu.VMEM((NDEV, rows, D), F32)] + _comm_sems(1, True),
        compiler_params=_params(),
    )(*parts)


def _gather_chips():
    x, y, c = _me()
    north = c == 1
    first = (jnp.where(north, 1 - x, x), jnp.where(north, y, 1 - y))
    second = (jnp.where(north, x, 1 - x), jnp.where(north, 1 - y, y))
    return [first, second, (1 - x, 1 - y)]


def _arrival_order():
    x, y, c = _me()
    first, second, far = _gather_chips()
    devs = [(x, y, c), (x, y, 1 - c), (*first, c), (*second, 1 - c), (*second, c), (*first, 1 - c), (*far, c),
            (*far, 1 - c)]
    return jnp.stack([4 * d[0] + 2 * d[1] + d[2] for d in devs]).astype(jnp.int32)


def in_proj(x, gain, sc, sh, w_shard, b_in):
    t = x.shape[0]
    tm = min(1024, t // 2)
    nm = t // tm
    order = _arrival_order()

    def body(order_ref, x_ref, g_ref, sc_ref, sh_ref, b_ref, wsh_ref, p_ref, h_ref, wall_ref,
             wbuf, h_all, send_sems, recv_sems, local_sems):
        k, m = pl.program_id(0), pl.program_id(1)
        mx, my, mc = _me()
        me, sibling = (mx, my, mc), (mx, my, 1 - mc)
        chips = _gather_chips()

        def slot(d):
            return 4 * d[0] + 2 * d[1] + d[2]

        def copy(sem, block, to, src=None):
            return pltpu.make_async_remote_copy(
                src_ref=wall_ref.at[slot(block)] if src is None else src, dst_ref=wall_ref.at[slot(block)],
                send_sem=send_sems.at[sem], recv_sem=recv_sems.at[sem], device_id=to, device_id_type=MESH)

        own = pltpu.make_async_copy(wsh_ref, wall_ref.at[slot(me)], local_sems.at[0])
        first = [copy(0, me, sibling, src=wsh_ref)] + [copy(1 + j, me, (*q, mc), src=wsh_ref)
                                                       for j, q in enumerate(chips)]
        passed = [copy(4 + j, (*q, mc), sibling) for j, q in enumerate(chips)]

        def load(kk):
            src = wsh_ref if kk == 0 else wall_ref.at[order_ref[kk]]
            return pltpu.make_async_copy(src, wbuf.at[kk % 2], local_sems.at[1 + kk % 2])

        def arrived(kk):
            if kk == 1:
                copy(0, sibling, me).wait_recv()
            elif kk in (2, 4, 6):
                j = kk // 2 - 1
                copy(1 + j, (*chips[j], mc), me).wait_recv()
                passed[j].start()
                if j < 2:
                    first[2 + j].start()
            else:
                j = {3: 1, 5: 0, 7: 2}[kk]
                copy(4 + {1: 0, 0: 1, 2: 2}[j], (*chips[j], 1 - mc), me).wait_recv()

        @pl.when((k == 0) & (m == 0))
        def _():
            own.start()
            for cp in first[:2]:
                cp.start()
            load(0).start()
            load(0).wait()

        for kk in range(1, NDEV):
            @pl.when((k == kk - 1) & (m == nm // 2))
            def _(kk=kk):
                arrived(kk)
                load(kk).start()

            @pl.when((k == kk) & (m == 0))
            def _(kk=kk):
                load(kk).wait()

        rows = pl.ds(pl.multiple_of(m * tm, tm), tm)

        @pl.when(k == 0)
        def _():
            h_all[rows, :] = f_modulate(x_ref[...], g_ref[...], sc_ref[...], sh_ref[...]).astype(BF16)

        p_ref[...] = _dot(h_all[rows, :], wbuf[k % 2], _NN) + b_ref[...]

        @pl.when((k == NDEV - 1) & (m == nm - 1))
        def _():
            for cp in first + passed:
                cp.wait_send()
            own.wait()
            out = pltpu.make_async_copy(h_all, h_ref, local_sems.at[0])
            out.start()
            out.wait()

    vec = pl.BlockSpec((1, D), lambda k, m, o: (0, 0))
    return pl.pallas_call(
        body, name="in_proj",
        out_shape=(jax.ShapeDtypeStruct((t, NDEV * D), F32), jax.ShapeDtypeStruct((t, D), BF16),
                   jax.ShapeDtypeStruct((NDEV, D, D), BF16)),
        grid_spec=pltpu.PrefetchScalarGridSpec(
            num_scalar_prefetch=1, grid=(NDEV, nm),
            in_specs=[pl.BlockSpec((tm, D), lambda k, m, o: (jnp.where(k == 0, m, 0), 0)), vec, vec, vec,
                      pl.BlockSpec((1, D), lambda k, m, o: (0, o[k])), ANYSPEC],
            out_specs=(pl.BlockSpec((tm, D), lambda k, m, o: (m, o[k])), ANYSPEC, ANYSPEC),
            scratch_shapes=[pltpu.VMEM((2, D, D), BF16), pltpu.VMEM((t, D), BF16), pltpu.SemaphoreType.DMA((7,)),
                            pltpu.SemaphoreType.DMA((7,)), pltpu.SemaphoreType.DMA((3,))]),
        compiler_params=_params("arbitrary", "arbitrary"),
    )(order, x, gain, sc, sh, b_in, w_shard)


def hgrn_local(p, logits, shards):
    t = p.shape[0]
    nc = t // CH
    n = len(shards)

    def body(*refs):
        q_ref, f_ref, v_ref, l_ref = refs[:4]
        intra_ref, qe_ref, st_ref, dec_ref = refs[4 + n:8 + n]
        state = refs[8 + 2 * n]
        start, forward, finish = _gather2_ops(refs[4:4 + n], refs[8 + n:8 + 2 * n], *refs[9 + 2 * n:])
        step = pl.program_id(0)

        @pl.when(step == 0)
        def _():
            start()
            state[...] = jnp.zeros_like(state)

        intra, qe, ut, dec = f_hgrn_chunk(q_ref[...], f_ref[...], v_ref[...], l_ref[...])
        intra_ref[...] = intra
        qe_ref[...] = qe.astype(BF16)
        dec_ref[...] = dec
        s = state[...]
        st_ref[...] = s
        state[...] = dec * s + ut
        pl.when(step == (3 * nc) // 4)(forward)
        pl.when(step == nc - 1)(finish)

    col = lambda j: pl.BlockSpec((CH, D), lambda c: (c, j))
    res = pl.pallas_call(
        body, name="hgrn_local", grid=(nc,),
        out_shape=(jax.ShapeDtypeStruct((t, D), F32), jax.ShapeDtypeStruct((t, D), BF16),
                   jax.ShapeDtypeStruct((nc, DK, D), F32), jax.ShapeDtypeStruct((nc, 1, D), F32)) + _gathered(shards),
        in_specs=[col(0), col(1), col(2), pl.BlockSpec((2, D), lambda c: (0, 0))] + [ANYSPEC] * n,
        out_specs=(pl.BlockSpec((CH, D), lambda c: (c, 0)), pl.BlockSpec((CH, D), lambda c: (c, 0)),
                   pl.BlockSpec((None, DK, D), lambda c: (c, 0, 0)), pl.BlockSpec((None, 1, D), lambda c: (c, 0, 0)))
        + (ANYSPEC,) * n,
        scratch_shapes=[pltpu.VMEM((DK, D), F32)] + _comm_sems(n, True),
        compiler_params=_params("arbitrary"),
    )(p, p, p, logits, *shards)
    return res[:4] + (res[4:],)


def _conv_tile(t):
    return min(256, t)


def conv_forward(p, dw, db, ln_g, ln_b, shards):
    t = p.shape[0]
    tm = _conv_tile(t)
    per = tm // HALO
    n = len(shards)
    nt = t // tm

    def body(*refs):
        cv_ref, cg_ref, cvp_ref, cgp_ref, dw_ref, db_ref, g_ref, b_ref = refs[:8]
        yc_ref, act_ref = refs[8 + n:10 + n]
        uext, ush = refs[10 + 2 * n:12 + 2 * n]
        start, forward, finish = _gather2_ops(refs[8:8 + n], refs[10 + n:10 + 2 * n], *refs[12 + 2 * n:])
        step = pl.program_id(0)
        pl.when(step == 0)(start)
        pl.when(step == nt - 1)(forward)
        uext[0:HALO, :] = jnp.where(step == 0, 0.0, f_glu(cvp_ref[...], cgp_ref[...]))
        uext[HALO:HALO + tm, :] = f_glu(cv_ref[...], cg_ref[...])
        _shift_stack(uext, ush, tm + HALO)
        acc = jnp.zeros((tm, D), F32) + db_ref[...]
        for w in range(KW):
            acc = acc + dw_ref[w:w + 1, :] * _shifted(ush, HALO - KW + 1 + w, tm)
        yc_ref[...] = acc
        act_ref[...] = f_conv_act(acc, g_ref[...], b_ref[...]).astype(BF16)
        pl.when(step == nt - 1)(finish)

    vec = pl.BlockSpec((1, D), lambda m: (0, 0))
    prev = lambda j: pl.BlockSpec((HALO, D), lambda m: (jnp.maximum(m * per - 1, 0), j))
    res = pl.pallas_call(
        body, name="conv_forward", grid=(nt,),
        out_shape=(jax.ShapeDtypeStruct((t, D), F32), jax.ShapeDtypeStruct((t, D), BF16)) + _gathered(shards),
        in_specs=[pl.BlockSpec((tm, D), lambda m: (m, 4)), pl.BlockSpec((tm, D), lambda m: (m, 5)), prev(4), prev(5),
                  pl.BlockSpec((HALO, D), lambda m: (0, 0)), vec, vec, vec] + [ANYSPEC] * n,
        out_specs=(pl.BlockSpec((tm, D), lambda m: (m, 0)), pl.BlockSpec((tm, D), lambda m: (m, 0))) + (ANYSPEC,) * n,
        scratch_shapes=[pltpu.VMEM((HALO + tm, D), F32), pltpu.VMEM((8, tm + HALO, D), F32)] + _comm_sems(n, True),
        compiler_params=_params("arbitrary"),
    )(p, p, p, p, dw, db, ln_g, ln_b, *shards)
    return res[0], res[1], res[2:]


def _mix_forward_tile(x_ref, og_ref, ga_ref, gb_ref, qe_ref, intra_ref, st_ref, act_ref, wa_ref, wb_ref, wo_ref,
                      hg_ref, rows):
    o = _inter(qe_ref[...], st_ref, rows) + intra_ref[...]
    a = f_head_out(o, og_ref[...], hg_ref[...])
    ya = _dot(a.astype(BF16), wa_ref[...], _NN)
    yb = _dot(act_ref[...], wb_ref[...], _NN)
    merged = f_merge(ga_ref[...], gb_ref[...], ya, yb)
    y = _dot(merged.astype(BF16), wo_ref[...], _NN)
    return o, a, ya, yb, merged, y


def _mix_specs(tm, tiles=None):
    at = (lambda m: m) if tiles is None else (lambda m: tiles - 1 - m)
    col = lambda j: pl.BlockSpec((tm, D), lambda m: (at(m), j))
    row = pl.BlockSpec((tm, D), lambda m: (at(m), 0))
    per_chunk = lambda rows: pl.BlockSpec((tm // CH, rows, D), lambda m: (at(m), 0, 0))
    return col, row, pl.BlockSpec((1, D), lambda m: (0, 0)), per_chunk


def mix_forward(x, p, qe, intra, st_prev, act, wa, wb, wo, hg, g1, post):
    t = x.shape[0]
    tm = min(512, t)

    def body(x_ref, og_ref, ga_ref, gb_ref, qe_ref, intra_ref, st_ref, act_ref, wa_ref, wb_ref, wo_ref, hg_ref,
             g1_ref, post_ref, x1_ref):
        y = _mix_forward_tile(x_ref, og_ref, ga_ref, gb_ref, qe_ref, intra_ref, st_ref, act_ref, wa_ref, wb_ref,
                              wo_ref, hg_ref, tm)[-1]
        x1_ref[...] = f_residual(x_ref[...], y, g1_ref[...], post_ref[...])

    col, row, vec, per_chunk = _mix_specs(tm)
    return pl.pallas_call(
        body, name="mix_forward", grid=(t // tm,),
        out_shape=jax.ShapeDtypeStruct((t, D), F32),
        in_specs=[row, col(3), col(6), col(7), row, row, per_chunk(DK), row, VSPEC, VSPEC, VSPEC, vec, vec, vec],
        out_specs=row,
        compiler_params=_params("parallel"),
    )(x, p, p, p, qe, intra, st_prev, act, wa, wb, wo, hg, g1, post)


def ffn_forward_backward(x1, target, w1, w2, pre, sc, sh, g2, post):
    t = x1.shape[0]
    tm = min(512, t)
    nb = w1.shape[0]
    fb = w1.shape[2]

    def body(x_ref, tg_ref, w1_ref, w2_ref, pre_ref, sc_ref, sh_ref, g2_ref, post_ref,
             dx_ref, h2_ref, r_ref, dz_ref, dy2_ref, acc_ref, z_sc):
        @pl.when(pl.program_id(0) == 0)
        def _():
            acc_ref[...] = jnp.zeros_like(acc_ref)

        x1v = x_ref[...]
        h2, vjp_h = jax.vjp(f_modulate, x1v, pre_ref[...], sc_ref[...], sh_ref[...])
        h2b = h2.astype(BF16)
        h2_ref[...] = h2b
        y2 = jnp.zeros((tm, D), F32)
        for n in range(nb):
            z = _dot(h2b, w1_ref[n], _NN)
            z_sc[:, n * fb:(n + 1) * fb] = z
            r = jnp.square(jnp.maximum(z, 0.0)).astype(BF16)
            r_ref[:, n * fb:(n + 1) * fb] = r
            y2 = y2 + _dot(r, w2_ref[n], _NN)
        out, vjp_r = jax.vjp(f_residual, x1v, y2, g2_ref[...], post_ref[...])
        err = out - tg_ref[...]
        tok = jnp.mean(jnp.square(err), axis=-1, keepdims=True)
        acc_ref[5:6, :] += 0.5 * jnp.sum(tok, axis=0, keepdims=True)
        dx_a, dy2, dg2, dpost = vjp_r(err * (1.0 / D))
        dy2b = dy2.astype(BF16)
        dy2_ref[...] = dy2b
        dh2 = jnp.zeros((tm, D), F32)
        for n in range(nb):
            dr = _dot(dy2b, w2_ref[n], _NT)
            dz = (dr * (2.0 * jnp.maximum(z_sc[:, n * fb:(n + 1) * fb], 0.0))).astype(BF16)
            dz_ref[:, n * fb:(n + 1) * fb] = dz
            dh2 = dh2 + _dot(dz, w1_ref[n], _NT)
        dx_b, dpre, dsc, dsh = vjp_h(dh2)
        dx_ref[...] = dx_a + dx_b
        acc_ref[0:1, :] += dpre
        acc_ref[1:2, :] += dpost
        acc_ref[2:3, :] += dsc
        acc_ref[3:4, :] += dsh
        acc_ref[4:5, :] += dg2

    row = pl.BlockSpec((tm, D), lambda m: (m, 0))
    wide = pl.BlockSpec((tm, DFF), lambda m: (m, 0), pipeline_mode=pl.Buffered(1))
    vec = pl.BlockSpec((1, D), lambda m: (0, 0))
    return pl.pallas_call(
        body, name="ffn_forward_backward", grid=(t // tm,),
        out_shape=(jax.ShapeDtypeStruct((t, D), F32), jax.ShapeDtypeStruct((t, D), BF16),
                   jax.ShapeDtypeStruct((t, DFF), BF16), jax.ShapeDtypeStruct((t, DFF), BF16),
                   jax.ShapeDtypeStruct((t, D), BF16), jax.ShapeDtypeStruct((8, D), F32)),
        in_specs=[row, row, VSPEC, VSPEC, vec, vec, vec, vec, vec],
        out_specs=(row, row, wide, wide, row, pl.BlockSpec((8, D), lambda m: (0, 0))),
        scratch_shapes=[pltpu.VMEM((tm, DFF), F32)],
        compiler_params=_params("arbitrary"),
    )(x1, target, w1, w2, pre, sc, sh, g2, post)


def mix_backward(x, p, qe, intra, st_prev, act, wa, wb, wo, hg, g1, post, dx1, dec):
    t = x.shape[0]
    tm = min(256, t)
    nc = t // CH

    def body(x_ref, og_ref, ga_ref, gb_ref, qe_ref, intra_ref, st_ref, act_ref, wa_ref, wb_ref, wo_ref, hg_ref,
             g1_ref, post_ref, dx1_ref, dec_ref,
             dp_ref, do_ref, dqe_ref, gt_ref, dact_ref, a_ref, mg_ref, dya_ref, dyb_ref, dy_ref, acc_ref, gstate):
        @pl.when(pl.program_id(0) == 0)
        def _():
            acc_ref[...] = jnp.zeros_like(acc_ref)
            gstate[...] = jnp.zeros_like(gstate)

        o, a, ya, yb, merged, y = _mix_forward_tile(x_ref, og_ref, ga_ref, gb_ref, qe_ref, intra_ref, st_ref, act_ref,
                                                    wa_ref, wb_ref, wo_ref, hg_ref, tm)
        a_ref[...] = a.astype(BF16)
        mg_ref[...] = merged.astype(BF16)
        _, vjp_r = jax.vjp(f_residual, x_ref[...], y, g1_ref[...], post_ref[...])
        _, dy, dg1, dpost = vjp_r(dx1_ref[...])
        dyb16 = dy.astype(BF16)
        dy_ref[...] = dyb16
        dmerged = _dot(dyb16, wo_ref[...], _NT)
        _, vjp_m = jax.vjp(f_merge, ga_ref[...], gb_ref[...], ya, yb)
        dga, dgb, dya, dyb = vjp_m(dmerged)
        dp_ref[:, D:2 * D] = dga.astype(BF16)
        dp_ref[:, 2 * D:3 * D] = dgb.astype(BF16)
        dya16, dyb16b = dya.astype(BF16), dyb.astype(BF16)
        dya_ref[...] = dya16
        dyb_ref[...] = dyb16b
        da = _dot(dya16, wa_ref[...], _NT)
        dact_ref[...] = _dot(dyb16b, wb_ref[...], _NT)
        _, vjp_a = jax.vjp(f_head_out, o, og_ref[...], hg_ref[...])
        do, dog, dhg = vjp_a(da)
        dp_ref[:, 0:D] = dog.astype(BF16)
        do_ref[...] = do
        do16 = do.astype(BF16)
        qe16 = qe_ref[...]
        for ci in reversed(range(tm // CH)):
            st = st_ref[ci].astype(BF16)
            rows = slice(ci * CH, (ci + 1) * CH)
            dqe, vt = [], []
            for h in range(HEADS):
                sl = slice(h * DK, (h + 1) * DK)
                dqe.append(_dot(do16[rows, sl], st[:, sl], _NN))
                vt.append(_dot(do16[rows, sl], qe16[rows, sl], _TN))
            dqe_ref[rows, :] = jnp.concatenate(dqe, axis=1)
            g = gstate[...]
            gt_ref[ci] = g
            gstate[...] = dec_ref[ci] * g + jnp.concatenate(vt, axis=1)
        acc_ref[0:1, :] += dg1
        acc_ref[1:2, :] += dpost
        acc_ref[2:3, :] += dhg
        acc_ref[3:4, :] += jnp.sum(dog, axis=0, keepdims=True)
        acc_ref[4:5, :] += jnp.sum(dga, axis=0, keepdims=True)
        acc_ref[5:6, :] += jnp.sum(dgb, axis=0, keepdims=True)

    nt = t // tm
    col, row, vec, per_chunk = _mix_specs(tm, nt)
    b16 = jax.ShapeDtypeStruct((t, D), BF16)
    f32 = jax.ShapeDtypeStruct((t, D), F32)
    return pl.pallas_call(
        body, name="mix_backward", grid=(nt,),
        out_shape=(jax.ShapeDtypeStruct((t, NDEV * D), BF16), f32, f32, jax.ShapeDtypeStruct((nc, DK, D), F32), f32,
                   b16, b16, b16, b16, b16, jax.ShapeDtypeStruct((8, D), F32)),
        in_specs=[row, col(3), col(6), col(7), row, row, per_chunk(DK), row, VSPEC, VSPEC, VSPEC, vec, vec, vec, row,
                  per_chunk(1)],
        out_specs=(pl.BlockSpec((tm, 3 * D), lambda m: (nt - 1 - m, 0)), row, row, per_chunk(DK), row, row, row, row,
                   row, row, pl.BlockSpec((8, D), lambda m: (0, 0))),
        scratch_shapes=[pltpu.VMEM((DK, D), F32)],
        compiler_params=_params("arbitrary"),
    )(x, p, p, p, qe, intra, st_prev, act, wa, wb, wo, hg, g1, post, dx1, dec)


def conv_backward(p, yc, dact, dw, ln_g, ln_b, blocks, dp):
    t = p.shape[0]
    tm = _conv_tile(t)
    per = tm // HALO
    nt = t // tm
    n = len(blocks)

    def body(*refs):
        cv_ref, cg_ref, cvp_ref, cgp_ref, yc_ref, ycn_ref, da_ref, dan_ref, dw_ref, g_ref, b_ref = refs[:11]
        dp_ref, acc_ref, ddw_ref = refs[12 + n:15 + n]
        uext, dyext, ush, dysh, ddw8, du_sc = refs[15 + 2 * n:21 + 2 * n]
        start, finish = _exchange_ops(refs[11:11 + n], refs[15 + n:15 + 2 * n], *refs[21 + 2 * n:])
        m = pl.program_id(0)

        @pl.when(m == 0)
        def _():
            start()
            acc_ref[...] = jnp.zeros_like(acc_ref)
            ddw8[...] = jnp.zeros_like(ddw8)

        cv, cg = cv_ref[...], cg_ref[...]
        u, vjp_u = jax.vjp(f_glu, cv, cg)
        uext[0:HALO, :] = jnp.where(m == 0, 0.0, f_glu(cvp_ref[...], cgp_ref[...]))
        uext[HALO:HALO + tm, :] = u
        _shift_stack(uext, ush, tm + HALO)
        _, vjp_c = jax.vjp(f_conv_act, yc_ref[...], g_ref[...], b_ref[...])
        dyc, dg, db = vjp_c(da_ref[...])
        _, vjp_n = jax.vjp(f_conv_act, ycn_ref[...], g_ref[...], b_ref[...])
        dyn = vjp_n(dan_ref[...])[0]
        dyext[0:tm, :] = dyc
        dyext[tm:tm + HALO, :] = jnp.where(m == nt - 1, 0.0, dyn)
        _shift_stack(dyext, dysh, tm + HALO)
        rb = min(128, tm)
        for lt in range(D // LANE):
            ls = slice(lt * LANE, (lt + 1) * LANE)
            for r0 in range(0, tm, rb):
                du_l = jnp.zeros((rb, LANE), F32)
                for w in range(KW):
                    du_l = du_l + dw_ref[w:w + 1, ls] * _shifted(dysh, KW - 1 - w + r0, rb, ls)
                du_sc[r0:r0 + rb, ls] = du_l
                dyc_l = dyext[r0:r0 + rb, ls]
                for w in range(KW):
                    prod = dyc_l * _shifted(ush, HALO - KW + 1 + w + r0, rb, ls)
                    part = jnp.sum(prod.reshape(4, rb // 32, 8, LANE), axis=1)
                    ddw8[w, :, ls] += (part[0] + part[1]) + (part[2] + part[3])
        du = du_sc[...]

        @pl.when(m == nt - 1)
        def _():
            ddw_ref[...] = jnp.sum(ddw8[...], axis=1)

        dcv, dcg = vjp_u(du)
        dp_ref[:, 0:D] = dcv.astype(BF16)
        dp_ref[:, D:2 * D] = dcg.astype(BF16)
        acc_ref[0:1, :] += jnp.sum(dyc, axis=0, keepdims=True)
        acc_ref[1:2, :] += dg
        acc_ref[2:3, :] += db
        acc_ref[3:4, :] += jnp.sum(dcv, axis=0, keepdims=True)
        acc_ref[4:5, :] += jnp.sum(dcg, axis=0, keepdims=True)
        pl.when(m == nt - 1)(finish)

    vec = pl.BlockSpec((1, D), lambda m: (0, 0))
    row = pl.BlockSpec((tm, D), lambda m: (m, 0))
    prev = lambda j: pl.BlockSpec((HALO, D), lambda m: (jnp.maximum(m * per - 1, 0), j))
    nxt = pl.BlockSpec((HALO, D), lambda m: (jnp.minimum((m + 1) * per, t // HALO - 1), 0))
    res = pl.pallas_call(
        body, name="conv_backward", grid=(nt,),
        out_shape=(jax.ShapeDtypeStruct(dp.shape, dp.dtype), jax.ShapeDtypeStruct((8, D), F32),
                   jax.ShapeDtypeStruct((HALO, D), F32)) + _exchanged(blocks),
        in_specs=[pl.BlockSpec((tm, D), lambda m: (m, 4)), pl.BlockSpec((tm, D), lambda m: (m, 5)), prev(4), prev(5),
                  row, nxt, row, nxt, pl.BlockSpec((HALO, D), lambda m: (0, 0)), vec, vec] + [ANYSPEC] * (n + 1),
        out_specs=(pl.BlockSpec((tm, 2 * D), lambda m: (m, 3)), pl.BlockSpec((8, D), lambda m: (0, 0)),
                   pl.BlockSpec((HALO, D), lambda m: (0, 0))) + (ANYSPEC,) * n,
        scratch_shapes=[pltpu.VMEM((HALO + tm, D), F32), pltpu.VMEM((tm + HALO, D), F32),
                        pltpu.VMEM((8, tm + HALO, D), F32), pltpu.VMEM((8, tm + HALO, D), F32),
                        pltpu.VMEM((HALO, 8, D), F32), pltpu.VMEM((tm, D), F32)] + _comm_sems(n, False),
        input_output_aliases={11 + n: 0},
        compiler_params=_params("arbitrary"),
    )(p, p, p, p, yc, yc, dact, dact, dw, ln_g, ln_b, *blocks, dp)
    return res[:3] + (res[3:],)


def hgrn_backward(p, logits, do, dqe, gt, st_prev, blocks, dp):
    t = p.shape[0]
    nc = t // CH
    n = len(blocks)

    def body(*refs):
        q_ref, f_ref, v_ref, l_ref, do_ref, dqe_ref, gt_ref, st_ref = refs[:8]
        dp_ref, acc_ref = refs[9 + n:11 + n]
        start, finish = _exchange_ops(refs[8:8 + n], refs[11 + n:11 + 2 * n], *refs[11 + 2 * n:])

        @pl.when(pl.program_id(0) == 0)
        def _():
            start()
            acc_ref[...] = jnp.zeros_like(acc_ref)

        gt_v = gt_ref[...]
        ddec = jnp.sum(gt_v * st_ref[...], axis=0, keepdims=True)
        _, vjp = jax.vjp(f_hgrn_chunk, q_ref[...], f_ref[...], v_ref[...], l_ref[...])
        dq, df, dv, dl = vjp((do_ref[...], dqe_ref[...], gt_v, ddec))
        dp_ref[:, 0:D] = dq.astype(BF16)
        dp_ref[:, D:2 * D] = df.astype(BF16)
        dp_ref[:, 2 * D:3 * D] = dv.astype(BF16)
        acc_ref[0:2, :] += dl
        acc_ref[2:3, :] += jnp.sum(dq, axis=0, keepdims=True)
        acc_ref[3:4, :] += jnp.sum(df, axis=0, keepdims=True)
        acc_ref[4:5, :] += jnp.sum(dv, axis=0, keepdims=True)
        pl.when(pl.program_id(0) == nc - 1)(finish)

    col = lambda j: pl.BlockSpec((CH, D), lambda c: (c, j))
    row = pl.BlockSpec((CH, D), lambda c: (c, 0))
    stspec = pl.BlockSpec((None, DK, D), lambda c: (c, 0, 0))
    res = pl.pallas_call(
        body, name="hgrn_backward", grid=(nc,),
        out_shape=(jax.ShapeDtypeStruct(dp.shape, dp.dtype), jax.ShapeDtypeStruct((8, D), F32)) + _exchanged(blocks),
        in_specs=[col(0), col(1), col(2), pl.BlockSpec((2, D), lambda c: (0, 0)), row, row, stspec, stspec]
        + [ANYSPEC] * (n + 1),
        out_specs=(pl.BlockSpec((CH, 3 * D), lambda c: (c, 1)), pl.BlockSpec((8, D), lambda c: (0, 0)))
        + (ANYSPEC,) * n,
        scratch_shapes=_comm_sems(n, False),
        input_output_aliases={8 + n: 0},
        compiler_params=_params("arbitrary"),
    )(p, p, p, logits, do, dqe, gt, st_prev, *blocks, dp)
    return res[:2] + (res[2:],)


def in_proj_backward(dp, w_all, x, dx1, gain, sc, sh):
    t = x.shape[0]
    tm = min(512, t)

    def body(dp_ref, w_ref, x_ref, dx1_ref, g_ref, sc_ref, sh_ref, gx_ref, acc_ref):
        @pl.when(pl.program_id(0) == 0)
        def _():
            acc_ref[...] = jnp.zeros_like(acc_ref)

        dh = jnp.zeros((tm, D), F32)
        for j in range(NDEV):
            dh = dh + _dot(dp_ref[:, j * D:(j + 1) * D], w_ref[DP_SPLIT[j]], _NT)
        _, vjp_h = jax.vjp(f_modulate, x_ref[...], g_ref[...], sc_ref[...], sh_ref[...])
        dx, dg, dsc, dsh = vjp_h(dh)
        gx_ref[...] = dx1_ref[...] + dx
        acc_ref[0:1, :] += dg
        acc_ref[1:2, :] += dsc
        acc_ref[2:3, :] += dsh

    row = pl.BlockSpec((tm, D), lambda m: (m, 0))
    vec = pl.BlockSpec((1, D), lambda m: (0, 0))
    return pl.pallas_call(
        body, name="in_proj_backward", grid=(t // tm,),
        out_shape=(jax.ShapeDtypeStruct((t, D), F32), jax.ShapeDtypeStruct((8, D), F32)),
        in_specs=[pl.BlockSpec((tm, NDEV * D), lambda m: (m, 0)), VSPEC, row, row, vec, vec, vec],
        out_specs=(row, pl.BlockSpec((8, D), lambda m: (0, 0))),
        compiler_params=_params("arbitrary"),
    )(dp, w_all, x, dx1, gain, sc, sh)


def weight_grad(a, b, nblk, ka, bn, a_blocked, name, b_col=lambda n: n, after=None):
    t = a.shape[0]
    tk = min(2048, t)
    nk = t // tk
    extra = [] if after is None else [after]

    def body(*refs):
        a_ref, b_ref = refs[:2]
        f_ref, h_ref, acc = refs[2 + len(extra):]
        k = pl.program_id(1)

        @pl.when(k == 0)
        def _():
            acc[...] = jnp.zeros_like(acc)

        acc[...] += _dot(a_ref[...], b_ref[...], _TN)

        @pl.when(k == nk - 1)
        def _():
            f_ref[...] = acc[...]
            h_ref[...] = acc[...].astype(BF16)

    a_idx = (lambda n, k: (k, n)) if a_blocked else (lambda n, k: (k, 0))
    b_idx = (lambda n, k: (k, 0)) if a_blocked else (lambda n, k: (k, b_col(n)))
    out = pl.BlockSpec((None, ka, bn), lambda n, k: (n, 0, 0))
    return pl.pallas_call(
        body, name=name, grid=(nblk, nk),
        out_shape=(jax.ShapeDtypeStruct((nblk, ka, bn), F32), jax.ShapeDtypeStruct((nblk, ka, bn), BF16)),
        in_specs=[pl.BlockSpec((tk, ka), a_idx), pl.BlockSpec((tk, bn), b_idx)] + [ANYSPEC] * len(extra),
        out_specs=(out, out),
        scratch_shapes=[pltpu.VMEM((ka, bn), F32)],
        compiler_params=_params("parallel", "arbitrary"),
    )(a, b, *extra)


def ada_backward(call_t, dmod_cols, w, m, v):
    def body(c_ref, d_ref, w_ref, m_ref, v_ref, g_ref, dl_ref, nm_ref, nv_ref):
        ct, dm = c_ref[...], d_ref[...]
        g = ct[:, 0:1] * dm[0:1, :]
        for r in range(1, NDEV):
            g = g + ct[:, r:r + 1] * dm[r:r + 1, :]
        g_ref[...] = g
        dl_ref[...], nm_ref[...], nv_ref[...] = _adamw(w_ref[...], g, m_ref[...], v_ref[...])

    br = 256
    blk = pl.BlockSpec((br, w.shape[1]), lambda i: (i, 0))
    s = jax.ShapeDtypeStruct(w.shape, F32)
    return pl.pallas_call(
        body, name="ada_backward", grid=(w.shape[0] // br,), out_shape=(s, s, s, s),
        in_specs=[pl.BlockSpec((br, NDEV), lambda i: (i, 0)), pl.BlockSpec(dmod_cols.shape, lambda i: (0, 0)), blk,
                  blk, blk],
        out_specs=(blk, blk, blk, blk), compiler_params=_params("parallel"),
    )(call_t, dmod_cols, w, m, v)


def adamw_small(total, ddw_mine, recipes, ws, ms, vs):
    n = len(ws)

    def body(*refs):
        tot, ddw = refs[0], refs[1]
        w_refs, m_refs, v_refs = refs[2:2 + n], refs[2 + n:2 + 2 * n], refs[2 + 2 * n:2 + 3 * n]
        outs = refs[2 + 3 * n:2 + 7 * n]
        loss_ref = refs[2 + 7 * n]
        for i, rec in enumerate(recipes):
            if rec == "dw":
                g = ddw[...]
            elif isinstance(rec, tuple):
                g = tot[rec[0]:rec[1], :]
            else:
                g = jnp.concatenate([tot[r:r + 1, :] for r in rec], axis=1) if len(rec) > 1 else tot[rec[0]:rec[0] + 1, :]
            dl, nm, nv = _adamw(w_refs[i][...], g, m_refs[i][...], v_refs[i][...])
            outs[4 * i][...] = g
            outs[4 * i + 1][...] = dl
            outs[4 * i + 2][...] = nm
            outs[4 * i + 3][...] = nv
        loss_ref[...] = tot[LOSS_ROW:LOSS_ROW + 1, 0:128]

    shapes = []
    for w in ws:
        shapes += [jax.ShapeDtypeStruct(w.shape, F32)] * 4
    res = pl.pallas_call(
        body, name="adamw_small", out_shape=tuple(shapes) + (jax.ShapeDtypeStruct((1, 128), F32),),
        in_specs=[VSPEC] * (2 + 3 * n), out_specs=(VSPEC,) * (4 * n + 1), compiler_params=_params(),
    )(total, ddw_mine, *ws, *ms, *vs)
    return [res[4 * i:4 * i + 4] for i in range(n)], res[4 * n]


def reduce_and_adamw(sel, w, g_alls, g_recvs, m, v, name):
    r, c = w.shape
    br = min(256, r)
    npair = len(g_alls)

    def body(sel_ref, w_ref, *refs):
        go_refs, gr_refs = refs[:npair], refs[npair:2 * npair]
        m_ref, v_ref, g_ref, dl_ref, nm_ref, nv_ref = refs[2 * npair:]
        g = None
        for i in range(npair):
            gi = go_refs[i][...]
            for k in range(NDEV - 1):
                gi = gi + gr_refs[i][k].astype(F32)
            g = gi if g is None else jnp.where(sel_ref[1] == i, gi, g)
        g_ref[...] = g
        dl_ref[...], nm_ref[...], nv_ref[...] = _adamw(w_ref[...], g, m_ref[...], v_ref[...])

    blk = pl.BlockSpec((br, c), lambda i, sel_ref: (i, 0))
    own = [pl.BlockSpec((None, br, c), lambda i, sel_ref, p=p: (jnp.where(sel_ref[1] == p, sel_ref[0], 0),
                                                               jnp.where(sel_ref[1] == p, i, 0), 0))
           for p in range(npair)]
    recv = [pl.BlockSpec((NDEV - 1, br, c), lambda i, sel_ref, p=p: (0, jnp.where(sel_ref[1] == p, i, 0), 0))
            for p in range(npair)]
    s = jax.ShapeDtypeStruct(w.shape, F32)
    return pl.pallas_call(
        body, name=name, out_shape=(s, s, s, s),
        grid_spec=pltpu.PrefetchScalarGridSpec(
            num_scalar_prefetch=1, grid=(r // br,),
            in_specs=[blk] + own + recv + [blk, blk],
            out_specs=(blk, blk, blk, blk)),
        compiler_params=_params("parallel"),
    )(sel, w, *g_alls, *g_recvs, m, v)


def kernel(x, c, w_ada, b_ada, pre_norm_tm, post_norm_tm, pre_norm_cm, post_norm_cm, w_in, b_in, hg_lb_logits, hg_norm, conv_dw, conv_db, conv_ln_g, conv_ln_b, w_br_a, w_br_b, w_out, w_ff1, w_ff2, loss_target, m_w_ada, m_b_ada, m_pre_norm_tm, m_post_norm_tm, m_pre_norm_cm, m_post_norm_cm, m_w_in, m_b_in, m_hg_lb_logits, m_hg_norm, m_conv_dw, m_conv_db, m_conv_ln_g, m_conv_ln_b, m_w_br_a, m_w_br_b, m_w_out, m_w_ff1, m_w_ff2, v_w_ada, v_b_ada, v_pre_norm_tm, v_post_norm_tm, v_pre_norm_cm, v_post_norm_cm, v_w_in, v_b_in, v_hg_lb_logits, v_hg_norm, v_conv_dw, v_conv_db, v_conv_ln_g, v_conv_ln_b, v_w_br_a, v_w_br_b, v_w_out, v_w_ff1, v_w_ff2):
    t = x.shape[1]
    me = 4 * lax.axis_index("x") + 2 * lax.axis_index("y") + lax.axis_index("c")
    xs = x[0]
    tgt = loss_target[0]

    mod, call, dw_all = ada_forward(c, w_ada[0], b_ada, conv_dw[0])
    sh1, sc1, g1, sh2, sc2, g2 = [mod[:, i * D:(i + 1) * D] for i in range(6)]
    dw = jnp.pad(dw_all.transpose(1, 0, 2).reshape(KW, D), ((0, HALO - KW), (0, 0)))

    p, h, win_all = in_proj(xs, pre_norm_tm, sc1, sh1, w_in[0].astype(BF16), b_in)
    yc, act, (w1_all,) = conv_forward(p, dw, conv_db, conv_ln_g, conv_ln_b, [w_ff1[0].astype(BF16)])
    intra, qe, st_prev, dec, (w2_all, wa_all, wb_all, wo_all) = hgrn_local(
        p, hg_lb_logits, [w_ff2[0].astype(BF16), w_br_a[0].astype(BF16), w_br_b[0].astype(BF16),
                          w_out[0].astype(BF16)])
    wa, wb, wo = wa_all.reshape(D, D), wb_all.reshape(D, D), wo_all.reshape(D, D)
    x1 = mix_forward(xs, p, qe, intra, st_prev, act, wa, wb, wo, hg_norm, g1, post_norm_tm)

    dx1, h2, r, dz, dy2, acc_ffn = ffn_forward_backward(x1, tgt, w1_all, w2_all, pre_norm_cm, sc2, sh2, g2,
                                                        post_norm_cm)
    rows = D // NDEV
    g1_f, g1_h = weight_grad(h2, dz, NDEV, D, DFF // NDEV, False, "grad_w_ff1")
    g2_f, g2_h = weight_grad(r, dy2, 4, D, D, True, "grad_w_ff2")
    g2_f, g2_h = g2_f.reshape(NDEV, DFF // NDEV, D), g2_h.reshape(NDEV, DFF // NDEV, D)

    (dp, do, dqe, gt, dact, a16, mg16, dya, dyb, dy, acc_mix) = mix_backward(
        xs, p, qe, intra, st_prev, act, wa, wb, wo, hg_norm, g1, post_norm_tm, dx1, dec)
    ga_f, ga_h = weight_grad(a16, dya, 1, D, D, False, "grad_w_br_a")
    gb_f, gb_h = weight_grad(act, dyb, 1, D, D, False, "grad_w_br_b")
    go_f, go_h = weight_grad(mg16, dy, 1, D, D, False, "grad_w_out")
    dp, acc_conv, ddw, (r_ff1, r_ff2) = conv_backward(p, yc, dact, dw, conv_ln_g, conv_ln_b, [g1_h, g2_h], dp)
    dp, acc_hg, (r_a, r_b, r_o) = hgrn_backward(
        p, hg_lb_logits, do, dqe, gt, st_prev,
        [ga_h.reshape(NDEV, rows, D), gb_h.reshape(NDEV, rows, D), go_h.reshape(NDEV, rows, D)], dp)
    dp_col = lambda n: jnp.where(n < 3, n + 3, jnp.where(n == 3, 0, jnp.where(n < 6, n + 2, n - 5)))
    even, odd = [0, 2, 4, 6], [1, 3, 5, 7]
    gin_f0, gin_h0 = weight_grad(h, dp, 4, D, D, False, "grad_w_in_even", b_col=lambda j: dp_col(2 * j))
    send0, recv0, thru0, landing0, token0 = exchange_start(gin_h0, even, "exchange_start_even")
    gin_f1, gin_h1 = weight_grad(h, dp, 4, D, D, False, "grad_w_in_odd", b_col=lambda j: dp_col(2 * j + 1),
                                 after=token0)
    send1, recv1, thru1, landing1, token1 = exchange_start(gin_h1, odd, "exchange_start_odd")
    grad_x, acc_in = in_proj_backward(dp, win_all, xs, dx1, pre_norm_tm, sc1, sh1 + token1[0:1, 0:1])

    own = [ga_f.reshape(NDEV, rows, D), gb_f.reshape(NDEV, rows, D), go_f.reshape(NDEV, rows, D), g1_f, g2_f]
    recv = [r_a, r_b, r_o, r_ff1, r_ff2]
    big = {}
    names = ["w_br_a", "w_br_b", "w_out", "w_ff1", "w_ff2"]
    ws = [w_br_a, w_br_b, w_out, w_ff1, w_ff2]
    ms = [m_w_br_a, m_w_br_b, m_w_out, m_w_ff1, m_w_ff2]
    vs = [v_w_br_a, v_w_br_b, v_w_out, v_w_ff1, v_w_ff2]
    sel = jnp.stack([me, 0]).astype(jnp.int32)
    for i, nm in enumerate(names):
        big[nm] = [o[None] for o in reduce_and_adamw(sel, ws[i][0], [own[i]], [recv[i]], ms[i][0], vs[i][0],
                                                     "adamw_" + nm)]
    big, grad_x, acc_in = lax.optimization_barrier((big, grad_x, acc_in))
    r_in0 = exchange_wait(send0, recv0, thru0, landing0, acc_in, even, "exchange_wait_even")
    r_in1 = exchange_wait(send1, recv1, thru1, landing1, acc_in, odd, "exchange_wait_odd")
    r_in0, r_in1, acc_in = lax.optimization_barrier((r_in0, r_in1, acc_in))
    sel_in = jnp.stack([me // 2, me % 2]).astype(jnp.int32)
    big["w_in"] = [o[None] for o in reduce_and_adamw(sel_in, w_in[0], [gin_f0, gin_f1], [r_in0, r_in1], m_w_in[0],
                                                     v_w_in[0], "adamw_w_in")]

    dmod_rows = [2, 1, 8, 19, 18, 20]
    total, kept = gather_and_sum_rows([acc_in, acc_mix, acc_ffn, acc_hg, acc_conv, ddw], dmod_rows)
    dmod_all = kept[:, 0:6, :].reshape(NDEV, 6 * D)
    wcols = w_ada.shape[2]
    gwa, dwa, nmwa, nvwa = ada_backward(call.T, lax.dynamic_slice_in_dim(dmod_all, me * wcols, wcols, axis=1),
                                        w_ada[0], m_w_ada[0], v_w_ada[0])
    ddw_mine = lax.dynamic_slice_in_dim(total[40:40 + KW], me * (D // NDEV), D // NDEV, axis=1)

    small_names = ["b_ada", "pre_norm_tm", "post_norm_tm", "pre_norm_cm", "post_norm_cm", "b_in", "hg_lb_logits",
                   "hg_norm", "conv_db", "conv_ln_g", "conv_ln_b", "conv_dw"]
    recipes = [dmod_rows, [0], [9], [16], [17], [26, 27, 28, 11, 35, 36, 12, 13], (24, 26), [10], [32], [33], [34],
               "dw"]
    small_w = [b_ada, pre_norm_tm, post_norm_tm, pre_norm_cm, post_norm_cm, b_in, hg_lb_logits, hg_norm, conv_db,
               conv_ln_g, conv_ln_b, conv_dw[0]]
    small_m = [m_b_ada, m_pre_norm_tm, m_post_norm_tm, m_pre_norm_cm, m_post_norm_cm, m_b_in, m_hg_lb_logits,
               m_hg_norm, m_conv_db, m_conv_ln_g, m_conv_ln_b, m_conv_dw[0]]
    small_v = [v_b_ada, v_pre_norm_tm, v_post_norm_tm, v_pre_norm_cm, v_post_norm_cm, v_b_in, v_hg_lb_logits,
               v_hg_norm, v_conv_db, v_conv_ln_g, v_conv_ln_b, v_conv_dw[0]]
    small_out, loss_row = adamw_small(total, ddw_mine, recipes, small_w, small_m, small_v)
    loss = loss_row[0, 0]
    sm = {nm: list(o) for nm, o in zip(small_names, small_out)}
    sm["conv_dw"] = [o[None] for o in sm["conv_dw"]]

    order = ["w_ada", "b_ada", "pre_norm_tm", "post_norm_tm", "pre_norm_cm", "post_norm_cm", "w_in", "b_in",
             "hg_lb_logits", "hg_norm", "conv_dw", "conv_db", "conv_ln_g", "conv_ln_b", "w_br_a", "w_br_b", "w_out",
             "w_ff1", "w_ff2"]
    res = dict(sm)
    res.update(big)
    res["w_ada"] = [gwa[None], dwa[None], nmwa[None], nvwa[None]]
    outs = [loss, grad_x[None]]
    for j in range(4):
        outs += [res[nm][j] for nm in order]
    return tuple(outs)
```
